```python
import math
import jax, jax.numpy as jnp
from jax import lax
import numpy as np

D_MODEL = 1024
BATCH = 8
SEQ = 2048
DEPTH = 1
DEC_BATCH = 128
DEC_SEQ = 1
PAST_LEN = 16384
PAGE_SIZE = 128

D_MIX = D_MODEL
HEAD_DIM = 64
D_ATTN = D_MIX // 2
N_HEADS = D_ATTN // HEAD_DIM
N_KV = 2
Q_PER_KV = N_HEADS // N_KV
D_KV = N_KV * HEAD_DIM
WINDOW = 128
BLOCK = WINDOW
D_SSM = D_MIX - D_ATTN
SSM_H = 16
SSM_G = D_SSM // SSM_H
SSM_P = 64
D_IN = D_ATTN + 2 * D_KV + D_SSM
DT_MIN = 0.001
DT_MAX = 0.1
N_EXPERTS = 32
TOP_K = 4
D_FF = D_MODEL
SWIGLU_LIMIT = 7.0
SWIGLU_ALPHA = 1.702
PLE_DIM = 256
EPS = 1e-5

kernel_name = 'hymba_s5_swa_sink_moe_ple_step'


def _rms_norm(x, g):
    xf = x.astype(jnp.float32)
    y = xf * lax.rsqrt(jnp.mean(xf * xf, axis=-1, keepdims=True) + EPS)
    return (y * g.astype(jnp.float32)).astype(x.dtype)


def _alibi_slopes():
    return 2.0 ** (-8.0 * (jnp.arange(N_HEADS, dtype=jnp.float32) + 1.0) / N_HEADS)


def _window_attend(q, k, v, q_pos, k_pos, sinks):
    f32 = jnp.float32
    slopes = _alibi_slopes().reshape(N_KV, Q_PER_KV)
    s = jnp.einsum('bnqkgd,bnskd->bnkgqs', q.astype(f32), k.astype(f32)) * (HEAD_DIM ** -0.5)
    dist = q_pos[:, :, None] - k_pos[:, None, :]
    valid = (dist >= 0) & (dist <= WINDOW) & (k_pos[:, None, :] >= 0)
    bias = -slopes[None, None, :, :, None, None] * dist.astype(f32)[None, :, None, None]
    s = jnp.where(valid[None, :, None, None], s + bias, -jnp.inf)
    sink = jnp.broadcast_to(sinks.astype(f32).reshape(1, 1, N_KV, Q_PER_KV, 1, 1), s.shape[:-1] + (1,))
    pr = jax.nn.softmax(jnp.concatenate([s, sink], axis=-1), axis=-1)[..., :-1]
    out = jnp.einsum('bnkgqs,bnskd->bnqkgd', pr, v.astype(f32))
    return out.astype(q.dtype)


def _attention(q, k, v, k_cache, v_cache, sinks):
    B, L = q.shape[0], q.shape[1]
    if k_cache is None:
        nb = L // BLOCK
        qb = q.reshape(B, nb, BLOCK, N_KV, Q_PER_KV, HEAD_DIM)
        kb = k.reshape(B, nb, BLOCK, N_KV, HEAD_DIM)
        vb = v.reshape(B, nb, BLOCK, N_KV, HEAD_DIM)
        pad = ((0, 0), (1, 0), (0, 0), (0, 0), (0, 0))
        kband = jnp.concatenate([jnp.pad(kb[:, :-1], pad), kb], axis=2)
        vband = jnp.concatenate([jnp.pad(vb[:, :-1], pad), vb], axis=2)
        q_pos = jnp.arange(L, dtype=jnp.int32).reshape(nb, BLOCK)
        k_pos = (jnp.arange(nb, dtype=jnp.int32) * BLOCK - BLOCK)[:, None] + jnp.arange(2 * BLOCK, dtype=jnp.int32)[None]
        out = _window_attend(qb, kband, vband, q_pos, k_pos, sinks).reshape(B, L, D_ATTN)
        wb = min(WINDOW, L)
        return out, k[:, L - wb:], v[:, L - wb:]
    wb = k_cache.shape[1]
    k_all = jnp.concatenate([k_cache.astype(k.dtype), k], axis=1)
    v_all = jnp.concatenate([v_cache.astype(v.dtype), v], axis=1)
    q_pos = (PAST_LEN + jnp.arange(L, dtype=jnp.int32))[None]
    k_pos = jnp.concatenate([PAST_LEN - wb + jnp.arange(wb, dtype=jnp.int32),
                             PAST_LEN + jnp.arange(L, dtype=jnp.int32)])[None]
    out = _window_attend(q[:, None], k_all[:, None], v_all[:, None], q_pos, k_pos, sinks)[:, 0].reshape(B, L, D_ATTN)
    return out, k_all[:, -wb:], v_all[:, -wb:]


def _diag_combine(e1, e2):
    a1, b1 = e1
    a2, b2 = e2
    return a1 * a2, a2 * b1 + b2


def _s5(u, x0_re, x0_im, lam_re, lam_im, log_step, b_re, b_im, c_re, c_im, d_skip, w_glu, b_glu):
    f32 = jnp.float32
    B, L = u.shape[0], u.shape[1]
    lam = lax.complex(lam_re.astype(f32), lam_im.astype(f32))
    step = jnp.exp(log_step.astype(f32))[:, None]
    lam_bar = jnp.exp(lam * step)
    b_bar = ((lam_bar - 1.0) / lam)[:, :, None] * lax.complex(b_re.astype(f32), b_im.astype(f32))
    c = lax.complex(c_re.astype(f32), c_im.astype(f32))
    ug = u.astype(f32).reshape(B, L, SSM_G, SSM_H)
    bu = jnp.einsum('gph,blgh->blgp', b_bar, ug.astype(jnp.complex64))
    x0 = lax.complex(x0_re.astype(f32), x0_im.astype(f32))
    bu = bu.at[:, 0].add(lam_bar[None] * x0)
    a = jnp.broadcast_to(lam_bar, bu.shape)
    _, xs = lax.associative_scan(_diag_combine, (a, bu), axis=1)
    y = jnp.einsum('ghp,blgp->blgh', c, xs).real + d_skip.astype(f32) * ug
    y = jax.nn.gelu(y).reshape(B, L, D_SSM)
    y = y * jax.nn.sigmoid(y @ w_glu.astype(f32) + b_glu.astype(f32))
    x_last = xs[:, -1]
    return y.astype(u.dtype), jnp.real(x_last), jnp.imag(x_last)


def _moe(x, w_router, b_router, w_up, b_up, w_down, b_down):
    f32 = jnp.float32
    logits = (x @ w_router + b_router).astype(f32)
    top_v, top_i = lax.top_k(logits, TOP_K)
    top_w = jax.nn.softmax(top_v, axis=-1)
    gates = jnp.sum(jax.nn.one_hot(top_i, N_EXPERTS, dtype=f32) * top_w[..., None], axis=-2)

    def expert_step(acc, ep):
        w1, b1, w2, b2, g = ep
        hdn = x @ w1 + b1
        gt = jnp.minimum(hdn[:, :D_FF], SWIGLU_LIMIT)
        up = jnp.clip(hdn[:, D_FF:], -SWIGLU_LIMIT, SWIGLU_LIMIT)
        act = (up + 1.0) * gt * jax.nn.sigmoid(SWIGLU_ALPHA * gt)
        return acc + g[:, None] * (act @ w2 + b2), None

    acc, _ = lax.scan(expert_step, jnp.zeros((x.shape[0], D_MODEL), f32),
                      (w_up, b_up, w_down, b_down, gates.T))
    return acc.astype(x.dtype)


def _layer(h, p_l, k_cache, v_cache, s_re, s_im, lw):
    B, L, _ = h.shape
    xn = _rms_norm(h, lw['norm_mix'])
    proj = xn @ lw['w_in']
    q, k, v, u = jnp.split(proj, [D_ATTN, D_ATTN + D_KV, D_ATTN + 2 * D_KV], axis=-1)
    q = q.reshape(B, L, N_KV, Q_PER_KV, HEAD_DIM)
    k = k.reshape(B, L, N_KV, HEAD_DIM)
    v = v.reshape(B, L, N_KV, HEAD_DIM)
    attn, new_k, new_v = _attention(q, k, v, k_cache, v_cache, lw['sinks'])
    ssm, new_re, new_im = _s5(u, s_re, s_im, lw['lam_re'], lw['lam_im'], lw['log_step'],
                              lw['b_re'], lw['b_im'], lw['c_re'], lw['c_im'], lw['d_skip'],
                              lw['w_glu'], lw['b_glu'])
    mixed = jnp.concatenate([_rms_norm(attn, lw['norm_attn_out']),
                             _rms_norm(ssm, lw['norm_ssm_out'])], axis=-1)
    h = h + mixed @ lw['w_out']
    hn = _rms_norm(h, lw['norm_ffn']).reshape(B * L, D_MODEL)
    h = h + _moe(hn, lw['w_router'], lw['b_router'], lw['w_up'], lw['b_up'],
                 lw['w_down'], lw['b_down']).reshape(B, L, D_MODEL)
    gate = jax.nn.sigmoid(_rms_norm(h, lw['norm_ple']) @ lw['w_ple_gate'])
    h = h + gate * (p_l @ lw['w_ple_proj'])
    return h, new_k, new_v, new_re, new_im


def setup_inputs(seed: int = 0) -> dict:
    key = jax.random.key(seed)
    ks = jax.random.split(key, 40)
    f32 = jnp.float32
    wb = min(WINDOW, PAST_LEN)

    def nrm(k, shape, scale):
        return jax.random.normal(k, shape, f32) * scale

    lam_re = -0.5 * jnp.exp(nrm(ks[8], (DEPTH, SSM_G, SSM_P), 0.05))
    lam_im = math.pi * jnp.arange(SSM_P, dtype=f32)[None, None] + nrm(ks[9], (DEPTH, SSM_G, SSM_P), 0.01)
    log_step = jax.random.uniform(ks[10], (DEPTH, SSM_G), f32, math.log(DT_MIN), math.log(DT_MAX))
    return {
        'x_prompt': nrm(ks[0], (BATCH, SEQ, D_MODEL), 1.0),
        'x_sample': nrm(ks[1], (DEC_BATCH, DEC_SEQ, D_MODEL), 1.0),
        'cache_k_win': nrm(ks[2], (DEPTH, DEC_BATCH, wb, N_KV, HEAD_DIM), 1.0),
        'cache_v_win': nrm(ks[3], (DEPTH, DEC_BATCH, wb, N_KV, HEAD_DIM), 1.0),
        'state_ssm_re': nrm(ks[4], (DEPTH, DEC_BATCH, SSM_G, SSM_P), 0.1),
        'state_ssm_im': nrm(ks[5], (DEPTH, DEC_BATCH, SSM_G, SSM_P), 0.1),
        'p_prompt': nrm(ks[6], (DEPTH, BATCH, SEQ, PLE_DIM), 1.0),
        'p_sample': nrm(ks[7], (DEPTH, DEC_BATCH, DEC_SEQ, PLE_DIM), 1.0),
        'norm_mix': 1.0 + nrm(ks[11], (DEPTH, D_MODEL), 0.02),
        'w_in': nrm(ks[12], (DEPTH, D_MODEL, D_IN), D_MODEL ** -0.5),
        'sinks': nrm(ks[13], (DEPTH, N_HEADS), 0.5),
        'ssm_lam_re': lam_re,
        'ssm_lam_im': lam_im,
        'ssm_log_step': log_step,
        'ssm_b_re': nrm(ks[14], (DEPTH, SSM_G, SSM_P, SSM_H), (2.0 * SSM_H) ** -0.5),
        'ssm_b_im': nrm(ks[15], (DEPTH, SSM_G, SSM_P, SSM_H), (2.0 * SSM_H) ** -0.5),
        'ssm_c_re': nrm(ks[16], (DEPTH, SSM_G, SSM_H, SSM_P), (2.0 * SSM_P) ** -0.5),
        'ssm_c_im': nrm(ks[17], (DEPTH, SSM_G, SSM_H, SSM_P), (2.0 * SSM_P) ** -0.5),
        'ssm_d': nrm(ks[18], (DEPTH, SSM_G, SSM_H), 1.0),
        'w_glu': nrm(ks[19], (DEPTH, D_SSM, D_SSM), D_SSM ** -0.5),
        'b_glu': nrm(ks[20], (DEPTH, D_SSM), 0.01),
        'norm_attn_out': 1.0 + nrm(ks[21], (DEPTH, D_ATTN), 0.02),
        'norm_ssm_out': 1.0 + nrm(ks[22], (DEPTH, D_SSM), 0.02),
        'w_out': nrm(ks[23], (DEPTH, D_MIX, D_MODEL), D_MIX ** -0.5),
        'norm_ffn': 1.0 + nrm(ks[24], (DEPTH, D_MODEL), 0.02),
        'w_router': nrm(ks[25], (DEPTH, D_MODEL, N_EXPERTS), D_MODEL ** -0.5),
        'b_router': nrm(ks[26], (DEPTH, N_EXPERTS), 0.01),
        'w_up': nrm(ks[27], (DEPTH, N_EXPERTS, D_MODEL, 2 * D_FF), D_MODEL ** -0.5),
        'b_up': nrm(ks[28], (DEPTH, N_EXPERTS, 2 * D_FF), 0.01),
        'w_down': nrm(ks[29], (DEPTH, N_EXPERTS, D_FF, D_MODEL), D_FF ** -0.5),
        'b_down': nrm(ks[30], (DEPTH, N_EXPERTS, D_MODEL), 0.01),
        'norm_ple': 1.0 + nrm(ks[31], (DEPTH, D_MODEL), 0.02),
        'w_ple_gate': nrm(ks[32], (DEPTH, D_MODEL, D_MODEL), D_MODEL ** -0.5),
        'w_ple_proj': nrm(ks[33], (DEPTH, PLE_DIM, D_MODEL), PLE_DIM ** -0.5),
        'norm_final': 1.0 + nrm(ks[34], (D_MODEL,), 0.02),
    }


def reference(x_prompt, x_sample, cache_k_win, cache_v_win, state_ssm_re, state_ssm_im,
              p_prompt, p_sample, norm_mix, w_in, sinks, ssm_lam_re, ssm_lam_im, ssm_log_step,
              ssm_b_re, ssm_b_im, ssm_c_re, ssm_c_im, ssm_d, w_glu, b_glu, norm_attn_out,
              norm_ssm_out, w_out, norm_ffn, w_router, b_router, w_up, b_up, w_down, b_down,
              norm_ple, w_ple_gate, w_ple_proj, norm_final):
    hp = x_prompt
    hs = x_sample
    kp_l, vp_l, rp_l, ip_l = [], [], [], []
    ks_l, vs_l, rs_l, is_l = [], [], [], []
    for i in range(DEPTH):
        lw = dict(norm_mix=norm_mix[i], w_in=w_in[i], sinks=sinks[i],
                  lam_re=ssm_lam_re[i], lam_im=ssm_lam_im[i], log_step=ssm_log_step[i],
                  b_re=ssm_b_re[i], b_im=ssm_b_im[i], c_re=ssm_c_re[i], c_im=ssm_c_im[i],
                  d_skip=ssm_d[i], w_glu=w_glu[i], b_glu=b_glu[i],
                  norm_attn_out=norm_attn_out[i], norm_ssm_out=norm_ssm_out[i], w_out=w_out[i],
                  norm_ffn=norm_ffn[i], w_router=w_router[i], b_router=b_router[i],
                  w_up=w_up[i], b_up=b_up[i], w_down=w_down[i], b_down=b_down[i],
                  norm_ple=norm_ple[i], w_ple_gate=w_ple_gate[i], w_ple_proj=w_ple_proj[i])
        zero_state = jnp.zeros((x_prompt.shape[0], SSM_G, SSM_P), jnp.float32)
        hp, kp, vp, rp, ip = _layer(hp, p_prompt[i], None, None, zero_state, zero_state, lw)
        hs, kss, vss, rss, iss = _layer(hs, p_sample[i], cache_k_win[i], cache_v_win[i],
                                        state_ssm_re[i], state_ssm_im[i], lw)
        kp_l.append(kp); vp_l.append(vp); rp_l.append(rp); ip_l.append(ip)
        ks_l.append(kss); vs_l.append(vss); rs_l.append(rss); is_l.append(iss)
    y_prompt = _rms_norm(hp, norm_final)
    y_sample = _rms_norm(hs, norm_final)
    return (y_prompt, y_sample,
            jnp.stack(kp_l), jnp.stack(vp_l), jnp.stack(rp_l), jnp.stack(ip_l),
            jnp.stack(ks_l), jnp.stack(vs_l), jnp.stack(rs_l), jnp.stack(is_l))
```

```python
import functools

import numpy as np
import jax
import jax.numpy as jnp
from jax import lax
from jax.experimental import pallas as pl
from jax.experimental.pallas import tpu as pltpu
from jax.experimental.pallas import tpu_sc as plsc

F32 = jnp.float32
BF16 = jnp.bfloat16

D_MODEL = 1024
BATCH = 8
SEQ = 2048
DEC_BATCH = 128
PAST_LEN = 16384
HEAD_DIM = 64
D_ATTN = 512
N_HEADS = 8
N_KV = 2
D_KV = N_KV * HEAD_DIM
WINDOW = 128
D_SSM = 512
SSM_H = 16
SSM_G = 32
SSM_P = 64
D_IN = D_ATTN + 2 * D_KV + D_SSM
N_EXPERTS = 32
TOP_K = 4
D_FF = 1024
SWIGLU_LIMIT = 7.0
SWIGLU_ALPHA = 1.702
PLE_DIM = 256
EPS = 1e-5
NEG = -1e30

CHUNK = 128
N_CHUNKS = SEQ // CHUNK
ROWS = BATCH * CHUNK
SUB_T = 32
SUB_ROWS = SUB_T * BATCH
N_SBLK = 4
SBLK = 512
T_REAL = BATCH * SEQ + DEC_BATCH
T_PAD = T_REAL + 128
SAMPLE_ROW0 = BATCH * SEQ
ROUTER_TILE = 640
TM = 256
N_TILES = (T_PAD * TOP_K + N_EXPERTS * (TM - 1)) // TM
P_ROWS = N_TILES * TM
SC_ROWS = 40
VMEM_LIMIT = 56 * 1024 * 1024


def _rms(x, g):
    return x * lax.rsqrt(jnp.mean(x * x, axis=-1, keepdims=True) + EPS) * g


def _sigmoid(x):
    return 1.0 / (1.0 + jnp.exp(-x))


def _gelu_tanh(x):
    c = np.float32(np.sqrt(2.0 / np.pi))
    return 0.5 * x * (1.0 + jnp.tanh(c * (x + 0.044715 * (x * x * x))))


def _bdot(a, b):
    return jnp.dot(a.astype(BF16), b, preferred_element_type=F32)


def _full(shape):
    n = len(shape)
    return pl.BlockSpec(shape, lambda *_: (0,) * n)


def _prep_kernel(lr_ref, li_ref, ls_ref, br_ref, bi_ref, lbr_ref, lbi_ref, bbr_ref, bbi_ref):
    lr = lr_ref[...]
    li = li_ref[...]
    step = jnp.exp(ls_ref[...])
    zr = lr * step
    zi = li * step
    mag = jnp.exp(zr)
    lbr = mag * jnp.cos(zi)
    lbi = mag * jnp.sin(zi)
    lbr_ref[...] = lbr
    lbi_ref[...] = lbi
    nr = lbr - 1.0
    den = lr * lr + li * li
    cr = (nr * lr + lbi * li) / den
    ci = (lbi * lr - nr * li) / den
    br = br_ref[...]
    bi = bi_ref[...]
    bbr_ref[...] = cr * br - ci * bi
    bbi_ref[...] = cr * bi + ci * br


def _prep(lam_re, lam_im, log_step, b_re, b_im):
    g, p, h = SSM_G, SSM_P, SSM_H
    out = pl.pallas_call(
        _prep_kernel,
        out_shape=[jax.ShapeDtypeStruct((g, 1, p), F32), jax.ShapeDtypeStruct((g, 1, p), F32),
                   jax.ShapeDtypeStruct((g, h, p), F32), jax.ShapeDtypeStruct((g, h, p), F32)],
        name="s5_prep",
    )(lam_re.reshape(g, 1, p), lam_im.reshape(g, 1, p), log_step.reshape(g, 1, 1),
      jnp.transpose(b_re, (0, 2, 1)), jnp.transpose(b_im, (0, 2, 1)))
    return out


def _s5_blocks(lbr, lbi, bbr, bbi, c_re, c_im):
    eye = jnp.eye(8, dtype=F32)
    shp = (N_SBLK, 8, SSM_H, SSM_P)
    b_r = jnp.einsum('jghp,gk->jghkp', bbr.reshape(shp), eye).reshape(N_SBLK, 128, SBLK)
    b_i = jnp.einsum('jghp,gk->jghkp', bbi.reshape(shp), eye).reshape(N_SBLK, 128, SBLK)
    bblk = jnp.concatenate([b_r, b_i], axis=-1).astype(BF16)
    c_r = jnp.einsum('jghp,gk->jgpkh', c_re.reshape(shp), eye).reshape(N_SBLK, SBLK, 128)
    c_i = jnp.einsum('jghp,gk->jgpkh', c_im.reshape(shp), eye).reshape(N_SBLK, SBLK, 128)
    cblk = jnp.concatenate([c_r, -c_i], axis=1).astype(BF16)
    return bblk, lbr.reshape(N_SBLK, 1, SBLK), lbi.reshape(N_SBLK, 1, SBLK), cblk


def _ssm_post(y_lin, u, dskip, wglu, bglu, gs):
    y = _gelu_tanh(y_lin + dskip * u)
    y = y * _sigmoid(_bdot(y, wglu) + bglu)
    return _rms(y, gs)


def _mixer_prompt_kernel(sinks_ref, x_ref, gmix_ref, win_ref, bias_ref, bblk_ref, lamr_ref, lami_ref,
                         cblk_ref, dskip_ref, wglu_ref, bglu_ref, ga_ref, gs_ref, wout_ref,
                         hmid_ref, klast_ref, vlast_ref, sre_ref, sim_ref,
                         proj_s, u_s, kprev_s, kprevr_s, vprev_s, vprevr_s, attn_s, ssm_s, utb_s, bu_s, xs_s, ytb_s):
    c = pl.program_id(0)

    @pl.when(c == 0)
    def _():
        zkv = jnp.zeros(kprev_s.shape, BF16)
        kprev_s[...] = zkv
        kprevr_s[...] = zkv
        vprev_s[...] = zkv
        vprevr_s[...] = zkv
        sre_ref[...] = jnp.zeros(sre_ref.shape, F32)
        sim_ref[...] = jnp.zeros(sim_ref.shape, F32)

    x = x_ref[...].reshape(ROWS, D_MODEL)
    proj = _bdot(_rms(x, gmix_ref[...]), win_ref[...])
    u0 = D_ATTN + 2 * D_KV
    proj_s[...] = proj[:, 0:u0]
    for jb in range(D_SSM // 128):
        u_s[jb] = proj[:, u0 + 128 * jb:u0 + 128 * (jb + 1)]

    lo = lax.broadcasted_iota(jnp.int32, (CHUNK, 128), 1) < HEAD_DIM
    table = jnp.minimum(c, 1)

    def attn_body(b, carry):
        r0 = pl.multiple_of(b * CHUNK, CHUNK)
        rows = pl.ds(r0, CHUNK)
        kb = proj_s[rows, D_ATTN:D_ATTN + D_KV]
        vb = proj_s[rows, D_ATTN + D_KV:D_ATTN + 2 * D_KV]
        kb16 = kb.astype(BF16)
        vb16 = vb.astype(BF16)
        kbr16 = pltpu.roll(kb, HEAD_DIM, 1).astype(BF16)
        vbr16 = pltpu.roll(vb, HEAD_DIM, 1).astype(BF16)
        k_nat = jnp.concatenate([kprev_s[b], kb16], axis=0)
        k_rot = jnp.concatenate([kprevr_s[b], kbr16], axis=0)
        v_nat = jnp.concatenate([vprev_s[b], vb16], axis=0)
        v_rot = jnp.concatenate([vprevr_s[b], vbr16], axis=0)
        for jq in range(N_HEADS // 2):
            q2 = proj_s[rows, 128 * jq:128 * (jq + 1)] * (HEAD_DIM ** -0.5)
            kv = jq // 2
            outs = []
            for half in range(2):
                h = 2 * jq + half
                qm = jnp.where(lo if half == 0 else jnp.logical_not(lo), q2, 0.0).astype(BF16)
                kx = k_nat if kv == half else k_rot
                vx = v_nat if kv == half else v_rot
                s = lax.dot_general(qm, kx, (((1,), (1,)), ((), ())), preferred_element_type=F32)
                s = s + bias_ref[table, h]
                sink = sinks_ref[h]
                m = jnp.maximum(jnp.max(s, axis=-1, keepdims=True), sink)
                p = jnp.exp(s - m)
                den = jnp.sum(p, axis=-1, keepdims=True) + jnp.exp(sink - m)
                outs.append(jnp.dot(p.astype(BF16), vx, preferred_element_type=F32) / den)
            attn_s[rows, 128 * jq:128 * (jq + 1)] = jnp.where(lo, outs[0], outs[1])
        kprev_s[b] = kb16
        kprevr_s[b] = kbr16
        vprev_s[b] = vb16
        vprevr_s[b] = vbr16
        return carry

    lax.fori_loop(0, BATCH, attn_body, 0)

    for sc in range(CHUNK // SUB_T):
        t0 = sc * SUB_T
        for i in range(SUB_T):
            for jb in range(D_SSM // 128):
                utb_s[i * BATCH:(i + 1) * BATCH, 128 * jb:128 * (jb + 1)] = (
                    u_s[jb, pl.ds(t0 + i, BATCH, stride=CHUNK), :])
        u_tb = utb_s[...]
        for j in range(N_SBLK):
            bu_s[...] = _bdot(u_tb[:, 128 * j:128 * (j + 1)], bblk_ref[j])
            lr = jnp.broadcast_to(lamr_ref[j], (BATCH, SBLK))
            li = jnp.broadcast_to(lami_ref[j], (BATCH, SBLK))
            cols = slice(SBLK * j, SBLK * (j + 1))

            def step(i, carry):
                sr, si = carry
                r = pl.ds(pl.multiple_of(i * BATCH, BATCH), BATCH)
                nr = lr * sr - li * si + bu_s[r, 0:SBLK]
                ni = lr * si + li * sr + bu_s[r, SBLK:2 * SBLK]
                xs_s[r, 0:SBLK] = nr
                xs_s[r, SBLK:2 * SBLK] = ni
                return nr, ni

            sr, si = lax.fori_loop(0, SUB_T, step, (sre_ref[:, cols], sim_ref[:, cols]))
            sre_ref[:, cols] = sr
            sim_ref[:, cols] = si
            ytb_s[:, 128 * j:128 * (j + 1)] = _bdot(xs_s[...], cblk_ref[j])
        yn = _ssm_post(ytb_s[...], u_tb, dskip_ref[...], wglu_ref[...], bglu_ref[...], gs_ref[...])
        for i in range(SUB_T):
            for jb in range(D_SSM // 128):
                ssm_s[jb, pl.ds(t0 + i, BATCH, stride=CHUNK), :] = (
                    yn[i * BATCH:(i + 1) * BATCH, 128 * jb:128 * (jb + 1)])

    an = _rms(attn_s[...], ga_ref[...])
    sn = jnp.concatenate([ssm_s[jb] for jb in range(D_SSM // 128)], axis=1)
    hmid_ref[...] = x + _bdot(an, wout_ref[0:D_ATTN, :]) + _bdot(sn, wout_ref[D_ATTN:2 * D_ATTN, :])

    @pl.when(c == N_CHUNKS - 1)
    def _():
        klast_ref[...] = proj_s[:, D_ATTN:D_ATTN + D_KV].reshape(BATCH, CHUNK, D_KV)
        vlast_ref[...] = proj_s[:, D_ATTN + D_KV:D_ATTN + 2 * D_KV].reshape(BATCH, CHUNK, D_KV)


def _mixer_prompt(x_prompt, sinks, gmix, win, bias, bblk, lamr, lami, cblk, dskip, wglu, bglu, ga, gs, wout):
    smem = pl.BlockSpec(memory_space=pltpu.SMEM)
    in_specs = [
        smem,
        pl.BlockSpec((BATCH, CHUNK, D_MODEL), lambda c: (0, c, 0)),
        _full((1, D_MODEL)), _full((D_MODEL, D_IN)), _full((2, N_HEADS, CHUNK, 2 * CHUNK)),
        _full((N_SBLK, 128, 2 * SBLK)), _full((N_SBLK, 1, SBLK)), _full((N_SBLK, 1, SBLK)),
        _full((N_SBLK, 2 * SBLK, 128)), _full((1, D_SSM)), _full((D_SSM, D_SSM)), _full((1, D_SSM)),
        _full((1, D_ATTN)), _full((1, D_SSM)), _full((D_MODEL, D_MODEL)),
    ]
    out_shape = [
        jax.ShapeDtypeStruct((T_PAD, D_MODEL), F32),
        jax.ShapeDtypeStruct((BATCH, CHUNK, D_KV), F32),
        jax.ShapeDtypeStruct((BATCH, CHUNK, D_KV), F32),
        jax.ShapeDtypeStruct((BATCH, SSM_G * SSM_P), F32),
        jax.ShapeDtypeStruct((BATCH, SSM_G * SSM_P), F32),
    ]
    out_specs = [
        pl.BlockSpec((ROWS, D_MODEL), lambda c: (c, 0)),
        _full((BATCH, CHUNK, D_KV)), _full((BATCH, CHUNK, D_KV)),
        _full((BATCH, SSM_G * SSM_P)), _full((BATCH, SSM_G * SSM_P)),
    ]
    kv_scr = pltpu.VMEM((BATCH, CHUNK, D_KV), BF16)
    scratch = [
        pltpu.VMEM((ROWS, D_ATTN + 2 * D_KV), F32), pltpu.VMEM((D_SSM // 128, ROWS, 128), F32),
        kv_scr, kv_scr, kv_scr, kv_scr,
        pltpu.VMEM((ROWS, D_ATTN), F32), pltpu.VMEM((D_SSM // 128, ROWS, 128), F32),
        pltpu.VMEM((SUB_ROWS, D_SSM), F32), pltpu.VMEM((SUB_ROWS, 2 * SBLK), F32),
        pltpu.VMEM((SUB_ROWS, 2 * SBLK), F32), pltpu.VMEM((SUB_ROWS, D_SSM), F32),
    ]
    return pl.pallas_call(
        _mixer_prompt_kernel, grid=(N_CHUNKS,), in_specs=in_specs, out_specs=out_specs, out_shape=out_shape,
        scratch_shapes=scratch, name="mixer_prompt",
        compiler_params=pltpu.CompilerParams(dimension_semantics=("arbitrary",), vmem_limit_bytes=VMEM_LIMIT),
    )(sinks, x_prompt, gmix, win, bias, bblk, lamr, lami, cblk, dskip, wglu, bglu, ga, gs, wout)


SGRP = 16
N_SGRP = DEC_BATCH // SGRP


def _mixer_sample_kernel(x_ref, gmix_ref, win_ref, ck_ref, cv_ref, sbias_ref, sinkc_ref, bblk_ref, lamr_ref,
                         lami_ref, cblk_ref, dskip_ref, wglu_ref, bglu_ref, ga_ref, gs_ref, wout_ref,
                         x0r_ref, x0i_ref, hbuf_ref,
                         hmid_ref, knew_ref, vnew_ref, sre_ref, sim_ref,
                         proj_s, qall_s, oall_s, kn8_s, vn8_s):
    del hbuf_ref
    g = pl.program_id(0)
    lo = lax.broadcasted_iota(jnp.int32, (DEC_BATCH, 128), 1) < HEAD_DIM

    @pl.when(g == 0)
    def _():
        proj = _bdot(_rms(x_ref[...], gmix_ref[...]), win_ref[...])
        proj_s[...] = proj
        for h in range(N_HEADS):
            jq, half, kv = h // 2, h % 2, h // 4
            q2 = proj[:, 128 * jq:128 * (jq + 1)] * (HEAD_DIM ** -0.5)
            if half != kv:
                q2 = pltpu.roll(q2, HEAD_DIM, 1)
            qall_s[h * DEC_BATCH:(h + 1) * DEC_BATCH, :] = jnp.where(lo if kv == 0 else jnp.logical_not(lo), q2, 0.0)
            kn8_s[h * DEC_BATCH:(h + 1) * DEC_BATCH, :] = proj[:, D_ATTN:D_ATTN + D_KV]
            vn8_s[h * DEC_BATCH:(h + 1) * DEC_BATCH, :] = proj[:, D_ATTN + D_KV:D_ATTN + 2 * D_KV]

    def attn_body(ii, carry):
        i = g * SGRP + ii
        qe = qall_s[pl.ds(i, N_HEADS, stride=DEC_BATCH), :]
        s = lax.dot_general(qe.astype(BF16), ck_ref[ii].astype(BF16), (((1,), (1,)), ((), ())),
                            preferred_element_type=F32) + sbias_ref[...]
        knew = kn8_s[pl.ds(i, N_HEADS, stride=DEC_BATCH), :]
        vnew = vn8_s[pl.ds(i, N_HEADS, stride=DEC_BATCH), :]
        s_new = jnp.sum(qe * knew, axis=-1, keepdims=True)
        sink = sinkc_ref[...]
        m = jnp.maximum(jnp.maximum(jnp.max(s, axis=-1, keepdims=True), s_new), sink)
        p = jnp.exp(s - m)
        p_new = jnp.exp(s_new - m)
        den = jnp.sum(p, axis=-1, keepdims=True) + p_new + jnp.exp(sink - m)
        o = (jnp.dot(p.astype(BF16), cv_ref[ii].astype(BF16), preferred_element_type=F32) + p_new * vnew) / den
        oall_s[pl.ds(i, N_HEADS, stride=DEC_BATCH), :] = o
        return carry

    lax.fori_loop(0, SGRP, attn_body, 0)

    @pl.when(g == N_SGRP - 1)
    def _():
        proj = proj_s[...]
        blocks = []
        for jq in range(N_HEADS // 2):
            oa = oall_s[(2 * jq) * DEC_BATCH:(2 * jq + 1) * DEC_BATCH, :]
            ob = oall_s[(2 * jq + 1) * DEC_BATCH:(2 * jq + 2) * DEC_BATCH, :]
            if jq // 2 == 0:
                blocks.append(jnp.where(lo, oa, pltpu.roll(ob, HEAD_DIM, 1)))
            else:
                blocks.append(jnp.where(lo, pltpu.roll(oa, HEAD_DIM, 1), ob))
        attn = jnp.concatenate(blocks, axis=1)
        u = proj[:, D_ATTN + 2 * D_KV:]
        ys = []
        for j in range(N_SBLK):
            bu = _bdot(u[:, 128 * j:128 * (j + 1)], bblk_ref[j])
            lr = lamr_ref[j]
            li = lami_ref[j]
            cols = slice(SBLK * j, SBLK * (j + 1))
            sr = x0r_ref[:, cols]
            si = x0i_ref[:, cols]
            nr = lr * sr - li * si + bu[:, 0:SBLK]
            ni = lr * si + li * sr + bu[:, SBLK:2 * SBLK]
            sre_ref[:, cols] = nr
            sim_ref[:, cols] = ni
            ys.append(_bdot(jnp.concatenate([nr, ni], axis=1), cblk_ref[j]))
        sn = _ssm_post(jnp.concatenate(ys, axis=1), u, dskip_ref[...], wglu_ref[...], bglu_ref[...], gs_ref[...])
        an = _rms(attn, ga_ref[...])
        h = x_ref[...] + _bdot(an, wout_ref[0:D_ATTN, :]) + _bdot(sn, wout_ref[D_ATTN:2 * D_ATTN, :])
        hmid_ref[0:DEC_BATCH, :] = h
        hmid_ref[DEC_BATCH:2 * DEC_BATCH, :] = jnp.zeros((DEC_BATCH, D_MODEL), F32)
        knew_ref[...] = proj[:, D_ATTN:D_ATTN + D_KV]
        vnew_ref[...] = proj[:, D_ATTN + D_KV:D_ATTN + 2 * D_KV]


def _mixer_sample(x_s, gmix, win, ck, cv, sbias, sinkc, bblk, lamr, lami, cblk, dskip, wglu, bglu, ga, gs, wout,
                  x0r, x0i, hbuf):
    nst = SSM_G * SSM_P
    in_specs = [
        _full((DEC_BATCH, D_MODEL)), _full((1, D_MODEL)), _full((D_MODEL, D_IN)),
        pl.BlockSpec((SGRP, WINDOW, D_KV), lambda g: (g, 0, 0)),
        pl.BlockSpec((SGRP, WINDOW, D_KV), lambda g: (g, 0, 0)),
        _full((N_HEADS, WINDOW)), _full((N_HEADS, 1)),
        _full((N_SBLK, 128, 2 * SBLK)), _full((N_SBLK, 1, SBLK)), _full((N_SBLK, 1, SBLK)),
        _full((N_SBLK, 2 * SBLK, 128)), _full((1, D_SSM)), _full((D_SSM, D_SSM)), _full((1, D_SSM)),
        _full((1, D_ATTN)), _full((1, D_SSM)), _full((D_MODEL, D_MODEL)),
        _full((DEC_BATCH, nst)), _full((DEC_BATCH, nst)),
        pl.BlockSpec(memory_space=pl.ANY),
    ]
    out_shape = [
        jax.ShapeDtypeStruct((T_PAD, D_MODEL), F32),
        jax.ShapeDtypeStruct((DEC_BATCH, D_KV), F32), jax.ShapeDtypeStruct((DEC_BATCH, D_KV), F32),
        jax.ShapeDtypeStruct((DEC_BATCH, nst), F32), jax.ShapeDtypeStruct((DEC_BATCH, nst), F32),
    ]
    out_specs = [
        pl.BlockSpec((2 * DEC_BATCH, D_MODEL), lambda g: (SAMPLE_ROW0 // (2 * DEC_BATCH), 0)),
        _full((DEC_BATCH, D_KV)), _full((DEC_BATCH, D_KV)),
        _full((DEC_BATCH, nst)), _full((DEC_BATCH, nst)),
    ]
    head_rows = pltpu.VMEM((N_HEADS * DEC_BATCH, 128), F32)
    scratch = [pltpu.VMEM((DEC_BATCH, D_IN), F32), head_rows, head_rows, head_rows, head_rows]
    return pl.pallas_call(
        _mixer_sample_kernel, grid=(N_SGRP,), in_specs=in_specs, out_specs=out_specs, out_shape=out_shape,
        scratch_shapes=scratch, input_output_aliases={19: 0}, name="mixer_sample",
        compiler_params=pltpu.CompilerParams(dimension_semantics=("arbitrary",), vmem_limit_bytes=VMEM_LIMIT),
    )(x_s, gmix, win, ck, cv, sbias, sinkc, bblk, lamr, lami, cblk, dskip, wglu, bglu, ga, gs, wout, x0r, x0i, hbuf)


def _router_kernel(h_ref, g_ref, wrt_ref, br_ref, tri_ref, hn_ref, idx_ref, gw_ref, rank_ref, cnt_ref):
    i = pl.program_id(0)

    @pl.when(i == 0)
    def _():
        cnt_ref[...] = jnp.zeros(cnt_ref.shape, jnp.int32)

    hn = _rms(h_ref[...], g_ref[...])
    hn_ref[...] = hn
    logits = lax.dot_general(wrt_ref[...], hn, (((1,), (1,)), ((), ())), precision=lax.Precision.HIGHEST,
                             preferred_element_type=F32) + br_ref[...]
    eidx = lax.broadcasted_iota(jnp.int32, logits.shape, 0)
    vals, onehots = [], []
    l = logits
    for k in range(TOP_K):
        m = jnp.max(l, axis=0, keepdims=True)
        ik = jnp.min(jnp.where(l == m, eidx, N_EXPERTS), axis=0, keepdims=True)
        oh = eidx == ik
        idx_ref[k:k + 1, :] = ik
        vals.append(m)
        onehots.append(oh)
        l = jnp.where(oh, -jnp.inf, l)
    exps = [jnp.exp(v - vals[0]) for v in vals]
    den = exps[0] + exps[1] + exps[2] + exps[3]
    for k in range(TOP_K):
        gw_ref[k:k + 1, :] = exps[k] / den
    member = jnp.zeros(logits.shape, F32)
    for oh in onehots:
        member = member + jnp.where(oh, 1.0, 0.0)
    before = jnp.dot(member.astype(BF16), tri_ref[...], preferred_element_type=F32) + cnt_ref[...].astype(F32)
    for k in range(TOP_K):
        rank_ref[k:k + 1, :] = jnp.sum(jnp.where(onehots[k], before, 0.0), axis=0, keepdims=True).astype(jnp.int32)
    cnt_ref[...] += jnp.sum(member, axis=1, keepdims=True).astype(jnp.int32)


def _router(hmid, g_ffn, w_router_t, b_router_c):
    rt = ROUTER_TILE
    tri = jnp.asarray(np.triu(np.ones((rt, rt), np.float32), 1), BF16)
    tok = lambda i: (0, i)
    return pl.pallas_call(
        _router_kernel, grid=(T_PAD // rt,),
        in_specs=[pl.BlockSpec((rt, D_MODEL), lambda i: (i, 0)), _full((1, D_MODEL)),
                  _full((N_EXPERTS, D_MODEL)), _full((N_EXPERTS, 1)), _full((rt, rt))],
        out_specs=[pl.BlockSpec((rt, D_MODEL), lambda i: (i, 0)), pl.BlockSpec((TOP_K, rt), tok),
                   pl.BlockSpec((TOP_K, rt), tok), pl.BlockSpec((TOP_K, rt), tok), _full((N_EXPERTS, 1))],
        out_shape=[jax.ShapeDtypeStruct((T_PAD, D_MODEL), F32), jax.ShapeDtypeStruct((TOP_K, T_PAD), jnp.int32),
                   jax.ShapeDtypeStruct((TOP_K, T_PAD), F32), jax.ShapeDtypeStruct((TOP_K, T_PAD), jnp.int32),
                   jax.ShapeDtypeStruct((N_EXPERTS, 1), jnp.int32)],
        name="router",
        compiler_params=pltpu.CompilerParams(dimension_semantics=("arbitrary",), vmem_limit_bytes=VMEM_LIMIT),
    )(hmid, g_ffn, w_router_t, b_router_c, tri)


def _place_kernel(off_ref, idx_ref, rank_ref, pos_ref):
    idx = idx_ref[...]
    pos = rank_ref[...]
    for e in range(N_EXPERTS):
        pos = pos + jnp.where(idx == e, off_ref[e], 0)
    pos_ref[...] = pos


def _place(offsets, idx, rank):
    return pl.pallas_call(
        _place_kernel,
        in_specs=[pl.BlockSpec(memory_space=pltpu.SMEM), pl.BlockSpec(memory_space=pltpu.VMEM),
                  pl.BlockSpec(memory_space=pltpu.VMEM)],
        out_specs=pl.BlockSpec(memory_space=pltpu.VMEM),
        out_shape=jax.ShapeDtypeStruct((TOP_K, T_PAD), jnp.int32), name="place",
    )(offsets, idx, rank)


def _sc_mesh():
    return plsc.VectorSubcoreMesh(core_axis_name="core", subcore_axis_name="subcore")


def _sc_dispatch(rows, pos):
    n, d = rows.shape
    nblk = n // SC_ROWS
    pos_w = pos.reshape(TOP_K, nblk, SC_ROWS).transpose(1, 0, 2)

    @functools.partial(pl.kernel, out_type=jax.ShapeDtypeStruct((P_ROWS, d), rows.dtype), mesh=_sc_mesh(),
                       scratch_types=[], name="dispatch")
    def run(x_hbm, i_hbm, o_hbm):
        def body(x_vmem, i_vmem):
            for k in range(TOP_K):
                pltpu.sync_copy(x_vmem, o_hbm.at[i_vmem.at[0, k]])

        pltpu.emit_pipeline(
            body, grid=(nblk,),
            in_specs=[pl.BlockSpec((SC_ROWS, d), lambda i: (i, 0)),
                      pl.BlockSpec((1, TOP_K, SC_ROWS), lambda i: (i, 0, 0))],
            out_specs=[], core_axis_name=("core", "subcore"), dimension_semantics=(pltpu.PARALLEL,),
        )(x_hbm, i_hbm)

    return run(rows, pos_w)


def _sc_combine(rows, flat_pos):
    _, d = rows.shape
    n = flat_pos.shape[0]
    nblk = n // SC_ROWS
    pos_w = flat_pos.reshape(nblk, 1, SC_ROWS)

    @functools.partial(pl.kernel, out_type=jax.ShapeDtypeStruct((n, d), rows.dtype), mesh=_sc_mesh(),
                       scratch_types=[], name="combine")
    def run(x_hbm, i_hbm, o_hbm):
        def body(i_vmem, o_vmem):
            pltpu.sync_copy(x_hbm.at[i_vmem.at[0, 0]], o_vmem)

        pltpu.emit_pipeline(
            body, grid=(nblk,),
            in_specs=[pl.BlockSpec((1, 1, SC_ROWS), lambda i: (i, 0, 0))],
            out_specs=[pl.BlockSpec((SC_ROWS, d), lambda i: (i, 0))],
            core_axis_name=("core", "subcore"), dimension_semantics=(pltpu.PARALLEL,),
        )(i_hbm, o_hbm)

    return run(rows, pos_w)


def _experts_kernel(te_ref, nv_ref, x_ref, w1_ref, b1_ref, w2_ref, b2_ref, y_ref, w1_s, w2_s):
    i = pl.program_id(0)
    e = te_ref[i]
    prev = te_ref[jnp.maximum(i - 1, 0)]

    @pl.when((i == 0) | (e != prev))
    def _():
        for r in range(4):
            rs = slice(256 * r, 256 * (r + 1))
            w1_s[rs, :] = w1_ref[0, rs, :].astype(BF16)
            w2_s[rs, :] = w2_ref[0, rs, :].astype(BF16)

    @pl.when(i < nv_ref[0])
    def _():
        hdn = _bdot(x_ref[...], w1_s[...]) + b1_ref[0]
        gt = jnp.minimum(hdn[:, :D_FF], SWIGLU_LIMIT)
        up = jnp.clip(hdn[:, D_FF:], -SWIGLU_LIMIT, SWIGLU_LIMIT)
        act = (up + 1.0) * gt * _sigmoid(SWIGLU_ALPHA * gt)
        y_ref[...] = _bdot(act, w2_s[...]) + b2_ref[0]


def _experts(tile_expert, n_valid, xs, w_up, b_up, w_down, b_down):
    row = lambda i, te, nv: (jnp.minimum(i, nv[0] - 1), 0)
    wsel = lambda i, te, nv: (te[i], 0, 0)
    grid_spec = pltpu.PrefetchScalarGridSpec(
        num_scalar_prefetch=2, grid=(N_TILES,),
        in_specs=[pl.BlockSpec((TM, D_MODEL), row),
                  pl.BlockSpec((1, D_MODEL, 2 * D_FF), wsel), pl.BlockSpec((1, 1, 2 * D_FF), wsel),
                  pl.BlockSpec((1, D_FF, D_MODEL), wsel), pl.BlockSpec((1, 1, D_MODEL), wsel)],
        out_specs=pl.BlockSpec((TM, D_MODEL), row),
        scratch_shapes=[pltpu.VMEM((D_MODEL, 2 * D_FF), BF16), pltpu.VMEM((D_FF, D_MODEL), BF16)],
    )
    return pl.pallas_call(
        _experts_kernel, grid_spec=grid_spec, out_shape=jax.ShapeDtypeStruct((P_ROWS, D_MODEL), F32),
        name="experts",
        compiler_params=pltpu.CompilerParams(dimension_semantics=("arbitrary",), vmem_limit_bytes=VMEM_LIMIT),
    )(tile_expert, n_valid, xs, w_up, b_up.reshape(N_EXPERTS, 1, 2 * D_FF), w_down,
      b_down.reshape(N_EXPERTS, 1, D_MODEL))


def _final_kernel(h_ref, yg_ref, gw_ref, p_ref, gple_ref, wg_ref, wp_ref, gfin_ref, out_ref):
    rows = h_ref.shape[0]
    gw = gw_ref[...]
    h = h_ref[...]
    for k in range(TOP_K):
        h = h + gw[:, k:k + 1] * yg_ref[k]
    gate = _sigmoid(_bdot(_rms(h, gple_ref[...]), wg_ref[...]))
    h = h + gate * _bdot(p_ref[...].reshape(rows, PLE_DIM), wp_ref[...])
    out_ref[...] = _rms(h, gfin_ref[...]).reshape(out_ref.shape)


def _final_prompt(hmid, yg, gwt, p_prompt, gple, wg, wp, gfin):
    nb = 4
    rows = nb * CHUNK
    rb = lambda c, b: (c * (BATCH // nb) + b, 0)
    return pl.pallas_call(
        _final_kernel, grid=(N_CHUNKS, BATCH // nb),
        in_specs=[pl.BlockSpec((rows, D_MODEL), rb),
                  pl.BlockSpec((TOP_K, rows, D_MODEL), lambda c, b: (0, c * (BATCH // nb) + b, 0)),
                  pl.BlockSpec((rows, TOP_K), rb),
                  pl.BlockSpec((nb, CHUNK, PLE_DIM), lambda c, b: (b, c, 0)),
                  _full((1, D_MODEL)), _full((D_MODEL, D_MODEL)), _full((PLE_DIM, D_MODEL)), _full((1, D_MODEL))],
        out_specs=pl.BlockSpec((nb, CHUNK, D_MODEL), lambda c, b: (b, c, 0)),
        out_shape=jax.ShapeDtypeStruct((BATCH, SEQ, D_MODEL), F32), name="final_prompt",
        compiler_params=pltpu.CompilerParams(dimension_semantics=("arbitrary", "arbitrary"),
                                             vmem_limit_bytes=VMEM_LIMIT),
    )(hmid, yg, gwt, p_prompt, gple, wg, wp, gfin)


def _final_sample(hmid, yg, gwt, p_sample, gple, wg, wp, gfin):
    blk = SAMPLE_ROW0 // DEC_BATCH
    return pl.pallas_call(
        _final_kernel, grid=(1,),
        in_specs=[pl.BlockSpec((DEC_BATCH, D_MODEL), lambda i: (blk, 0)),
                  pl.BlockSpec((TOP_K, DEC_BATCH, D_MODEL), lambda i: (0, blk, 0)),
                  pl.BlockSpec((DEC_BATCH, TOP_K), lambda i: (blk, 0)),
                  _full((DEC_BATCH, PLE_DIM)),
                  _full((1, D_MODEL)), _full((D_MODEL, D_MODEL)), _full((PLE_DIM, D_MODEL)), _full((1, D_MODEL))],
        out_specs=_full((DEC_BATCH, D_MODEL)),
        out_shape=jax.ShapeDtypeStruct((DEC_BATCH, D_MODEL), F32), name="final_sample",
        compiler_params=pltpu.CompilerParams(dimension_semantics=("arbitrary",), vmem_limit_bytes=VMEM_LIMIT),
    )(hmid, yg, gwt, p_sample, gple, wg, wp, gfin)


def _alibi_tables():
    slopes = 2.0 ** (-8.0 * (np.arange(N_HEADS, dtype=np.float64) + 1.0) / N_HEADS)
    i = np.arange(CHUNK)[:, None]
    j = np.arange(2 * CHUNK)[None, :]
    dist = i + CHUNK - j
    valid = (dist >= 0) & (dist <= WINDOW)
    tabs = []
    for has_prev in (False, True):
        ok = valid & ((j >= CHUNK) | has_prev)
        tabs.append(np.where(ok[None], -slopes[:, None, None] * dist[None], NEG))
    prompt = np.stack(tabs).astype(np.float32)
    wb = min(WINDOW, PAST_LEN)
    sample = (-slopes[:, None] * (wb - np.arange(wb))[None, :]).astype(np.float32)
    return prompt, sample


def kernel(x_prompt, x_sample, cache_k_win, cache_v_win, state_ssm_re, state_ssm_im, p_prompt, p_sample, norm_mix, w_in, sinks, ssm_lam_re, ssm_lam_im, ssm_log_step, ssm_b_re, ssm_b_im, ssm_c_re, ssm_c_im, ssm_d, w_glu, b_glu, norm_attn_out, norm_ssm_out, w_out, norm_ffn, w_router, b_router, w_up, b_up, w_down, b_down, norm_ple, w_ple_gate, w_ple_proj, norm_final):
    nst = SSM_G * SSM_P
    bias_np, sbias_np = _alibi_tables()
    bias = jnp.asarray(bias_np)
    sbias = jnp.asarray(sbias_np)

    lbr, lbi, bbr, bbi = _prep(ssm_lam_re[0], ssm_lam_im[0], ssm_log_step[0], ssm_b_re[0], ssm_b_im[0])
    bblk, lamr, lami, cblk = _s5_blocks(lbr, lbi, bbr, bbi, ssm_c_re[0], ssm_c_im[0])

    gmix = norm_mix[0].reshape(1, D_MODEL)
    win = w_in[0].astype(BF16)
    dskip = ssm_d[0].reshape(1, D_SSM)
    wglu = w_glu[0].astype(BF16)
    bglu = b_glu[0].reshape(1, D_SSM)
    ga = norm_attn_out[0].reshape(1, D_ATTN)
    gs = norm_ssm_out[0].reshape(1, D_SSM)
    wout = w_out[0].astype(BF16)
    sink = sinks[0]

    hbuf, k_last, v_last, re_p, im_p = _mixer_prompt(
        x_prompt, sink, gmix, win, bias, bblk, lamr, lami, cblk, dskip, wglu, bglu, ga, gs, wout)

    ck = cache_k_win[0].reshape(DEC_BATCH, WINDOW, D_KV)
    cv = cache_v_win[0].reshape(DEC_BATCH, WINDOW, D_KV)
    hmid, k_new, v_new, re_s, im_s = _mixer_sample(
        x_sample.reshape(DEC_BATCH, D_MODEL), gmix, win, ck, cv, sbias, sink.reshape(N_HEADS, 1), bblk, lamr, lami,
        cblk, dskip, wglu, bglu, ga, gs, wout, state_ssm_re[0].reshape(DEC_BATCH, nst),
        state_ssm_im[0].reshape(DEC_BATCH, nst), hbuf)

    hn, idx, gw, rank, counts = _router(hmid, norm_ffn[0].reshape(1, D_MODEL), w_router[0].T,
                                        b_router[0].reshape(N_EXPERTS, 1))

    tiles_per = (counts[:, 0] + (TM - 1)) // TM
    tile_end = jnp.cumsum(tiles_per)
    offsets = ((tile_end - tiles_per) * TM).astype(jnp.int32)
    n_valid = tile_end[-1:].astype(jnp.int32)
    tile_expert = jnp.minimum(jnp.searchsorted(tile_end, jnp.arange(N_TILES, dtype=jnp.int32), side='right'),
                              N_EXPERTS - 1).astype(jnp.int32)
    tile_expert = jnp.where(jnp.arange(N_TILES) < n_valid[0], tile_expert, tile_expert[n_valid[0] - 1])

    pos = _place(offsets, idx, rank)
    xs = _sc_dispatch(hn, pos)
    ys = _experts(tile_expert, n_valid, xs, w_up[0], b_up[0], w_down[0], b_down[0])
    yg = _sc_combine(ys, pos.reshape(TOP_K * T_PAD)).reshape(TOP_K, T_PAD, D_MODEL)

    gwt = gw.T
    gple = norm_ple[0].reshape(1, D_MODEL)
    wg = w_ple_gate[0].astype(BF16)
    wp = w_ple_proj[0].astype(BF16)
    gfin = norm_final.reshape(1, D_MODEL)
    y_prompt = _final_prompt(hmid, yg, gwt, p_prompt[0], gple, wg, wp, gfin)
    y_sample = _final_sample(hmid, yg, gwt, p_sample[0].reshape(DEC_BATCH, PLE_DIM), gple, wg, wp, gfin)

    k_win_s = jnp.concatenate([ck[:, 1:], k_new[:, None, :]], axis=1)
    v_win_s = jnp.concatenate([cv[:, 1:], v_new[:, None, :]], axis=1)
    kv5 = (1, BATCH, CHUNK, N_KV, HEAD_DIM)
    skv5 = (1, DEC_BATCH, WINDOW, N_KV, HEAD_DIM)
    return (y_prompt, y_sample.reshape(DEC_BATCH, 1, D_MODEL),
            k_last.reshape(kv5), v_last.reshape(kv5),
            re_p.reshape(1, BATCH, SSM_G, SSM_P), im_p.reshape(1, BATCH, SSM_G, SSM_P),
            k_win_s.reshape(skv5), v_win_s.reshape(skv5),
            re_s.reshape(1, DEC_BATCH, SSM_G, SSM_P), im_s.reshape(1, DEC_BATCH, SSM_G, SSM_P))
```

```python
import functools

import numpy as np
import jax
import jax.numpy as jnp
from jax import lax
from jax.experimental import pallas as pl
from jax.experimental.pallas import tpu as pltpu
from jax.experimental.pallas import tpu_sc as plsc

F32 = jnp.float32
BF16 = jnp.bfloat16

D_MODEL = 1024
BATCH = 8
SEQ = 2048
DEC_BATCH = 128
PAST_LEN = 16384
HEAD_DIM = 64
D_ATTN = 512
N_HEADS = 8
N_KV = 2
D_KV = N_KV * HEAD_DIM
WINDOW = 128
D_SSM = 512
SSM_H = 16
SSM_G = 32
SSM_P = 64
D_IN = D_ATTN + 2 * D_KV + D_SSM
N_EXPERTS = 32
TOP_K = 4
D_FF = 1024
SWIGLU_LIMIT = 7.0
SWIGLU_ALPHA = 1.702
PLE_DIM = 256
EPS = 1e-5
NEG = -1e30

CHUNK = 128
N_CHUNKS = SEQ // CHUNK
ROWS = BATCH * CHUNK
SUB_T = 32
SUB_ROWS = SUB_T * BATCH
N_SBLK = 4
SBLK = 512
T_REAL = BATCH * SEQ + DEC_BATCH
T_PAD = T_REAL + 128
SAMPLE_ROW0 = BATCH * SEQ
ROUTER_TILE = 640
TM = 256
N_TILES = (T_PAD * TOP_K + N_EXPERTS * (TM - 1)) // TM
P_ROWS = N_TILES * TM
SC_ROWS = 40
D_PACK = D_MODEL // 2
VMEM_LIMIT = 56 * 1024 * 1024


def _rms(x, g):
    return x * lax.rsqrt(jnp.mean(x * x, axis=-1, keepdims=True) + EPS) * g


def _sigmoid(x):
    return 1.0 / (1.0 + jnp.exp(-x))


def _gelu_tanh(x):
    c = np.float32(np.sqrt(2.0 / np.pi))
    return 0.5 * x * (1.0 + jnp.tanh(c * (x + 0.044715 * (x * x * x))))


def _bdot(a, b):
    return jnp.dot(a.astype(BF16), b, preferred_element_type=F32)


def _pack_pairs(x):
    n = x.shape[1] // 2
    lo = lax.bitcast_convert_type(x[:, :n].astype(BF16).astype(F32), jnp.int32)
    hi = lax.bitcast_convert_type(x[:, n:].astype(BF16).astype(F32), jnp.int32)
    return lax.shift_right_logical(lo, 16) | hi


def _unpack_pairs(w):
    lo = lax.bitcast_convert_type(lax.shift_left(w, 16), F32)
    hi = lax.bitcast_convert_type(w & jnp.int32(-65536), F32)
    return jnp.concatenate([lo, hi], axis=1)


def _full(shape):
    n = len(shape)
    return pl.BlockSpec(shape, lambda *_: (0,) * n)


def _prep_kernel(lr_ref, li_ref, ls_ref, br_ref, bi_ref, lbr_ref, lbi_ref, bbr_ref, bbi_ref):
    lr = lr_ref[...]
    li = li_ref[...]
    step = jnp.exp(ls_ref[...])
    zr = lr * step
    zi = li * step
    mag = jnp.exp(zr)
    lbr = mag * jnp.cos(zi)
    lbi = mag * jnp.sin(zi)
    lbr_ref[...] = lbr
    lbi_ref[...] = lbi
    nr = lbr - 1.0
    den = lr * lr + li * li
    cr = (nr * lr + lbi * li) / den
    ci = (lbi * lr - nr * li) / den
    br = br_ref[...]
    bi = bi_ref[...]
    bbr_ref[...] = cr * br - ci * bi
    bbi_ref[...] = cr * bi + ci * br


def _prep(lam_re, lam_im, log_step, b_re, b_im):
    g, p, h = SSM_G, SSM_P, SSM_H
    out = pl.pallas_call(
        _prep_kernel,
        out_shape=[jax.ShapeDtypeStruct((g, 1, p), F32), jax.ShapeDtypeStruct((g, 1, p), F32),
                   jax.ShapeDtypeStruct((g, h, p), F32), jax.ShapeDtypeStruct((g, h, p), F32)],
        name="s5_prep",
    )(lam_re.reshape(g, 1, p), lam_im.reshape(g, 1, p), log_step.reshape(g, 1, 1),
      jnp.transpose(b_re, (0, 2, 1)), jnp.transpose(b_im, (0, 2, 1)))
    return out


def _s5_blocks(lbr, lbi, bbr, bbi, c_re, c_im):
    eye = jnp.eye(8, dtype=F32)
    shp = (N_SBLK, 8, SSM_H, SSM_P)
    b_r = jnp.einsum('jghp,gk->jghkp', bbr.reshape(shp), eye).reshape(N_SBLK, 128, SBLK)
    b_i = jnp.einsum('jghp,gk->jghkp', bbi.reshape(shp), eye).reshape(N_SBLK, 128, SBLK)
    bblk = jnp.concatenate([b_r, b_i], axis=-1).astype(BF16)
    c_r = jnp.einsum('jghp,gk->jgpkh', c_re.reshape(shp), eye).reshape(N_SBLK, SBLK, 128)
    c_i = jnp.einsum('jghp,gk->jgpkh', c_im.reshape(shp), eye).reshape(N_SBLK, SBLK, 128)
    cblk = jnp.concatenate([c_r, -c_i], axis=1).astype(BF16)
    return bblk, lbr.reshape(N_SBLK, 1, SBLK), lbi.reshape(N_SBLK, 1, SBLK), cblk


def _ssm_post(y_lin, u, dskip, wglu, bglu, gs):
    y = _gelu_tanh(y_lin + dskip * u)
    y = y * _sigmoid(_bdot(y, wglu) + bglu)
    return _rms(y, gs)


def _mixer_prompt_kernel(sinks_ref, x_ref, gmix_ref, win_ref, bias_ref, bblk_ref, lamr_ref, lami_ref,
                         cblk_ref, dskip_ref, wglu_ref, bglu_ref, ga_ref, gs_ref, wout_ref,
                         hmid_ref, klast_ref, vlast_ref, sre_ref, sim_ref,
                         proj_s, u_s, kprev_s, kprevr_s, vprev_s, vprevr_s, attn_s, ssm_s, utb_s, bu_s, xs_s, ytb_s):
    c = pl.program_id(0)

    @pl.when(c == 0)
    def _():
        zkv = jnp.zeros(kprev_s.shape, BF16)
        kprev_s[...] = zkv
        kprevr_s[...] = zkv
        vprev_s[...] = zkv
        vprevr_s[...] = zkv
        sre_ref[...] = jnp.zeros(sre_ref.shape, F32)
        sim_ref[...] = jnp.zeros(sim_ref.shape, F32)

    x = x_ref[...].reshape(ROWS, D_MODEL)
    proj = _bdot(_rms(x, gmix_ref[...]), win_ref[...])
    u0 = D_ATTN + 2 * D_KV
    proj_s[...] = proj[:, 0:u0]
    for jb in range(D_SSM // 128):
        u_s[jb] = proj[:, u0 + 128 * jb:u0 + 128 * (jb + 1)]

    lo = lax.broadcasted_iota(jnp.int32, (CHUNK, 128), 1) < HEAD_DIM
    table = jnp.minimum(c, 1)

    def attn_body(b, carry):
        r0 = pl.multiple_of(b * CHUNK, CHUNK)
        rows = pl.ds(r0, CHUNK)
        kb = proj_s[rows, D_ATTN:D_ATTN + D_KV]
        vb = proj_s[rows, D_ATTN + D_KV:D_ATTN + 2 * D_KV]
        kb16 = kb.astype(BF16)
        vb16 = vb.astype(BF16)
        kbr16 = pltpu.roll(kb, HEAD_DIM, 1).astype(BF16)
        vbr16 = pltpu.roll(vb, HEAD_DIM, 1).astype(BF16)
        k_nat = jnp.concatenate([kprev_s[b], kb16], axis=0)
        k_rot = jnp.concatenate([kprevr_s[b], kbr16], axis=0)
        v_nat = jnp.concatenate([vprev_s[b], vb16], axis=0)
        v_rot = jnp.concatenate([vprevr_s[b], vbr16], axis=0)
        for jq in range(N_HEADS // 2):
            q2 = proj_s[rows, 128 * jq:128 * (jq + 1)] * (HEAD_DIM ** -0.5)
            kv = jq // 2
            outs = []
            for half in range(2):
                h = 2 * jq + half
                qm = jnp.where(lo if half == 0 else jnp.logical_not(lo), q2, 0.0).astype(BF16)
                kx = k_nat if kv == half else k_rot
                vx = v_nat if kv == half else v_rot
                s = lax.dot_general(qm, kx, (((1,), (1,)), ((), ())), preferred_element_type=F32)
                s = s + bias_ref[table, h]
                sink = sinks_ref[h]
                m = jnp.maximum(jnp.max(s, axis=-1, keepdims=True), sink)
                p = jnp.exp(s - m)
                den = jnp.sum(p, axis=-1, keepdims=True) + jnp.exp(sink - m)
                outs.append(jnp.dot(p.astype(BF16), vx, preferred_element_type=F32) / den)
            attn_s[rows, 128 * jq:128 * (jq + 1)] = jnp.where(lo, outs[0], outs[1])
        kprev_s[b] = kb16
        kprevr_s[b] = kbr16
        vprev_s[b] = vb16
        vprevr_s[b] = vbr16
        return carry

    lax.fori_loop(0, BATCH, attn_body, 0)

    for sc in range(CHUNK // SUB_T):
        t0 = sc * SUB_T
        for i in range(SUB_T):
            for jb in range(D_SSM // 128):
                utb_s[i * BATCH:(i + 1) * BATCH, 128 * jb:128 * (jb + 1)] = (
                    u_s[jb, pl.ds(t0 + i, BATCH, stride=CHUNK), :])
        u_tb = utb_s[...]
        for j in range(N_SBLK):
            bu_s[...] = _bdot(u_tb[:, 128 * j:128 * (j + 1)], bblk_ref[j])
            lr = jnp.broadcast_to(lamr_ref[j], (BATCH, SBLK))
            li = jnp.broadcast_to(lami_ref[j], (BATCH, SBLK))
            cols = slice(SBLK * j, SBLK * (j + 1))

            def step(i, carry):
                sr, si = carry
                r = pl.ds(pl.multiple_of(i * BATCH, BATCH), BATCH)
                nr = lr * sr - li * si + bu_s[r, 0:SBLK]
                ni = lr * si + li * sr + bu_s[r, SBLK:2 * SBLK]
                xs_s[r, 0:SBLK] = nr
                xs_s[r, SBLK:2 * SBLK] = ni
                return nr, ni

            sr, si = lax.fori_loop(0, SUB_T, step, (sre_ref[:, cols], sim_ref[:, cols]))
            sre_ref[:, cols] = sr
            sim_ref[:, cols] = si
            ytb_s[:, 128 * j:128 * (j + 1)] = _bdot(xs_s[...], cblk_ref[j])
        yn = _ssm_post(ytb_s[...], u_tb, dskip_ref[...], wglu_ref[...], bglu_ref[...], gs_ref[...])
        for i in range(SUB_T):
            for jb in range(D_SSM // 128):
                ssm_s[jb, pl.ds(t0 + i, BATCH, stride=CHUNK), :] = (
                    yn[i * BATCH:(i + 1) * BATCH, 128 * jb:128 * (jb + 1)])

    an = _rms(attn_s[...], ga_ref[...])
    sn = jnp.concatenate([ssm_s[jb] for jb in range(D_SSM // 128)], axis=1)
    hmid_ref[...] = x + _bdot(an, wout_ref[0:D_ATTN, :]) + _bdot(sn, wout_ref[D_ATTN:2 * D_ATTN, :])

    @pl.when(c == N_CHUNKS - 1)
    def _():
        klast_ref[...] = proj_s[:, D_ATTN:D_ATTN + D_KV].reshape(BATCH, CHUNK, D_KV)
        vlast_ref[...] = proj_s[:, D_ATTN + D_KV:D_ATTN + 2 * D_KV].reshape(BATCH, CHUNK, D_KV)


def _mixer_prompt(x_prompt, sinks, gmix, win, bias, bblk, lamr, lami, cblk, dskip, wglu, bglu, ga, gs, wout):
    smem = pl.BlockSpec(memory_space=pltpu.SMEM)
    in_specs = [
        smem,
        pl.BlockSpec((BATCH, CHUNK, D_MODEL), lambda c: (0, c, 0)),
        _full((1, D_MODEL)), _full((D_MODEL, D_IN)), _full((2, N_HEADS, CHUNK, 2 * CHUNK)),
        _full((N_SBLK, 128, 2 * SBLK)), _full((N_SBLK, 1, SBLK)), _full((N_SBLK, 1, SBLK)),
        _full((N_SBLK, 2 * SBLK, 128)), _full((1, D_SSM)), _full((D_SSM, D_SSM)), _full((1, D_SSM)),
        _full((1, D_ATTN)), _full((1, D_SSM)), _full((D_MODEL, D_MODEL)),
    ]
    out_shape = [
        jax.ShapeDtypeStruct((T_PAD, D_MODEL), F32),
        jax.ShapeDtypeStruct((BATCH, CHUNK, D_KV), F32),
        jax.ShapeDtypeStruct((BATCH, CHUNK, D_KV), F32),
        jax.ShapeDtypeStruct((BATCH, SSM_G * SSM_P), F32),
        jax.ShapeDtypeStruct((BATCH, SSM_G * SSM_P), F32),
    ]
    out_specs = [
        pl.BlockSpec((ROWS, D_MODEL), lambda c: (c, 0)),
        _full((BATCH, CHUNK, D_KV)), _full((BATCH, CHUNK, D_KV)),
        _full((BATCH, SSM_G * SSM_P)), _full((BATCH, SSM_G * SSM_P)),
    ]
    kv_scr = pltpu.VMEM((BATCH, CHUNK, D_KV), BF16)
    scratch = [
        pltpu.VMEM((ROWS, D_ATTN + 2 * D_KV), F32), pltpu.VMEM((D_SSM // 128, ROWS, 128), F32),
        kv_scr, kv_scr, kv_scr, kv_scr,
        pltpu.VMEM((ROWS, D_ATTN), F32), pltpu.VMEM((D_SSM // 128, ROWS, 128), F32),
        pltpu.VMEM((SUB_ROWS, D_SSM), F32), pltpu.VMEM((SUB_ROWS, 2 * SBLK), F32),
        pltpu.VMEM((SUB_ROWS, 2 * SBLK), F32), pltpu.VMEM((SUB_ROWS, D_SSM), F32),
    ]
    return pl.pallas_call(
        _mixer_prompt_kernel, grid=(N_CHUNKS,), in_specs=in_specs, out_specs=out_specs, out_shape=out_shape,
        scratch_shapes=scratch, name="mixer_prompt",
        compiler_params=pltpu.CompilerParams(dimension_semantics=("arbitrary",), vmem_limit_bytes=VMEM_LIMIT),
    )(sinks, x_prompt, gmix, win, bias, bblk, lamr, lami, cblk, dskip, wglu, bglu, ga, gs, wout)


SGRP = 16
N_SGRP = DEC_BATCH // SGRP


def _mixer_sample_kernel(x_ref, gmix_ref, win_ref, ck_ref, cv_ref, sbias_ref, sinkc_ref, bblk_ref, lamr_ref,
                         lami_ref, cblk_ref, dskip_ref, wglu_ref, bglu_ref, ga_ref, gs_ref, wout_ref,
                         x0r_ref, x0i_ref, hbuf_ref,
                         hmid_ref, knew_ref, vnew_ref, sre_ref, sim_ref,
                         proj_s, qall_s, oall_s, kn8_s, vn8_s):
    del hbuf_ref
    g = pl.program_id(0)
    lo = lax.broadcasted_iota(jnp.int32, (DEC_BATCH, 128), 1) < HEAD_DIM

    @pl.when(g == 0)
    def _():
        proj = _bdot(_rms(x_ref[...], gmix_ref[...]), win_ref[...])
        proj_s[...] = proj
        for h in range(N_HEADS):
            jq, half, kv = h // 2, h % 2, h // 4
            q2 = proj[:, 128 * jq:128 * (jq + 1)] * (HEAD_DIM ** -0.5)
            if half != kv:
                q2 = pltpu.roll(q2, HEAD_DIM, 1)
            qall_s[h * DEC_BATCH:(h + 1) * DEC_BATCH, :] = jnp.where(lo if kv == 0 else jnp.logical_not(lo), q2, 0.0)
            kn8_s[h * DEC_BATCH:(h + 1) * DEC_BATCH, :] = proj[:, D_ATTN:D_ATTN + D_KV]
            vn8_s[h * DEC_BATCH:(h + 1) * DEC_BATCH, :] = proj[:, D_ATTN + D_KV:D_ATTN + 2 * D_KV]

    def attn_body(ii, carry):
        i = g * SGRP + ii
        qe = qall_s[pl.ds(i, N_HEADS, stride=DEC_BATCH), :]
        s = lax.dot_general(qe.astype(BF16), ck_ref[ii].astype(BF16), (((1,), (1,)), ((), ())),
                            preferred_element_type=F32) + sbias_ref[...]
        knew = kn8_s[pl.ds(i, N_HEADS, stride=DEC_BATCH), :]
        vnew = vn8_s[pl.ds(i, N_HEADS, stride=DEC_BATCH), :]
        s_new = jnp.sum(qe * knew, axis=-1, keepdims=True)
        sink = sinkc_ref[...]
        m = jnp.maximum(jnp.maximum(jnp.max(s, axis=-1, keepdims=True), s_new), sink)
        p = jnp.exp(s - m)
        p_new = jnp.exp(s_new - m)
        den = jnp.sum(p, axis=-1, keepdims=True) + p_new + jnp.exp(sink - m)
        o = (jnp.dot(p.astype(BF16), cv_ref[ii].astype(BF16), preferred_element_type=F32) + p_new * vnew) / den
        oall_s[pl.ds(i, N_HEADS, stride=DEC_BATCH), :] = o
        return carry

    lax.fori_loop(0, SGRP, attn_body, 0)

    @pl.when(g == N_SGRP - 1)
    def _():
        proj = proj_s[...]
        blocks = []
        for jq in range(N_HEADS // 2):
            oa = oall_s[(2 * jq) * DEC_BATCH:(2 * jq + 1) * DEC_BATCH, :]
            ob = oall_s[(2 * jq + 1) * DEC_BATCH:(2 * jq + 2) * DEC_BATCH, :]
            if jq // 2 == 0:
                blocks.append(jnp.where(lo, oa, pltpu.roll(ob, HEAD_DIM, 1)))
            else:
                blocks.append(jnp.where(lo, pltpu.roll(oa, HEAD_DIM, 1), ob))
        attn = jnp.concatenate(blocks, axis=1)
        u = proj[:, D_ATTN + 2 * D_KV:]
        ys = []
        for j in range(N_SBLK):
            bu = _bdot(u[:, 128 * j:128 * (j + 1)], bblk_ref[j])
            lr = lamr_ref[j]
            li = lami_ref[j]
            cols = slice(SBLK * j, SBLK * (j + 1))
            sr = x0r_ref[:, cols]
            si = x0i_ref[:, cols]
            nr = lr * sr - li * si + bu[:, 0:SBLK]
            ni = lr * si + li * sr + bu[:, SBLK:2 * SBLK]
            sre_ref[:, cols] = nr
            sim_ref[:, cols] = ni
            ys.append(_bdot(jnp.concatenate([nr, ni], axis=1), cblk_ref[j]))
        sn = _ssm_post(jnp.concatenate(ys, axis=1), u, dskip_ref[...], wglu_ref[...], bglu_ref[...], gs_ref[...])
        an = _rms(attn, ga_ref[...])
        h = x_ref[...] + _bdot(an, wout_ref[0:D_ATTN, :]) + _bdot(sn, wout_ref[D_ATTN:2 * D_ATTN, :])
        hmid_ref[0:DEC_BATCH, :] = h
        hmid_ref[DEC_BATCH:2 * DEC_BATCH, :] = jnp.zeros((DEC_BATCH, D_MODEL), F32)
        knew_ref[...] = proj[:, D_ATTN:D_ATTN + D_KV]
        vnew_ref[...] = proj[:, D_ATTN + D_KV:D_ATTN + 2 * D_KV]


def _mixer_sample(x_s, gmix, win, ck, cv, sbias, sinkc, bblk, lamr, lami, cblk, dskip, wglu, bglu, ga, gs, wout,
                  x0r, x0i, hbuf):
    nst = SSM_G * SSM_P
    in_specs = [
        _full((DEC_BATCH, D_MODEL)), _full((1, D_MODEL)), _full((D_MODEL, D_IN)),
        pl.BlockSpec((SGRP, WINDOW, D_KV), lambda g: (g, 0, 0)),
        pl.BlockSpec((SGRP, WINDOW, D_KV), lambda g: (g, 0, 0)),
        _full((N_HEADS, WINDOW)), _full((N_HEADS, 1)),
        _full((N_SBLK, 128, 2 * SBLK)), _full((N_SBLK, 1, SBLK)), _full((N_SBLK, 1, SBLK)),
        _full((N_SBLK, 2 * SBLK, 128)), _full((1, D_SSM)), _full((D_SSM, D_SSM)), _full((1, D_SSM)),
        _full((1, D_ATTN)), _full((1, D_SSM)), _full((D_MODEL, D_MODEL)),
        _full((DEC_BATCH, nst)), _full((DEC_BATCH, nst)),
        pl.BlockSpec(memory_space=pl.ANY),
    ]
    out_shape = [
        jax.ShapeDtypeStruct((T_PAD, D_MODEL), F32),
        jax.ShapeDtypeStruct((DEC_BATCH, D_KV), F32), jax.ShapeDtypeStruct((DEC_BATCH, D_KV), F32),
        jax.ShapeDtypeStruct((DEC_BATCH, nst), F32), jax.ShapeDtypeStruct((DEC_BATCH, nst), F32),
    ]
    out_specs = [
        pl.BlockSpec((2 * DEC_BATCH, D_MODEL), lambda g: (SAMPLE_ROW0 // (2 * DEC_BATCH), 0)),
        _full((DEC_BATCH, D_KV)), _full((DEC_BATCH, D_KV)),
        _full((DEC_BATCH, nst)), _full((DEC_BATCH, nst)),
    ]
    head_rows = pltpu.VMEM((N_HEADS * DEC_BATCH, 128), F32)
    scratch = [pltpu.VMEM((DEC_BATCH, D_IN), F32), head_rows, head_rows, head_rows, head_rows]
    return pl.pallas_call(
        _mixer_sample_kernel, grid=(N_SGRP,), in_specs=in_specs, out_specs=out_specs, out_shape=out_shape,
        scratch_shapes=scratch, input_output_aliases={19: 0}, name="mixer_sample",
        compiler_params=pltpu.CompilerParams(dimension_semantics=("arbitrary",), vmem_limit_bytes=VMEM_LIMIT),
    )(x_s, gmix, win, ck, cv, sbias, sinkc, bblk, lamr, lami, cblk, dskip, wglu, bglu, ga, gs, wout, x0r, x0i, hbuf)


def _router_kernel(h_ref, g_ref, wrt_ref, br_ref, tri_ref, hn_ref, idx_ref, gw_ref, rank_ref, cnt_ref):
    i = pl.program_id(0)

    @pl.when(i == 0)
    def _():
        cnt_ref[...] = jnp.zeros(cnt_ref.shape, jnp.int32)

    hn = _rms(h_ref[...], g_ref[...])
    hn_ref[...] = _pack_pairs(hn)
    logits = lax.dot_general(wrt_ref[...], hn, (((1,), (1,)), ((), ())), precision=lax.Precision.HIGHEST,
                             preferred_element_type=F32) + br_ref[...]
    eidx = lax.broadcasted_iota(jnp.int32, logits.shape, 0)
    vals, onehots = [], []
    l = logits
    for k in range(TOP_K):
        m = jnp.max(l, axis=0, keepdims=True)
        ik = jnp.min(jnp.where(l == m, eidx, N_EXPERTS), axis=0, keepdims=True)
        oh = eidx == ik
        idx_ref[k:k + 1, :] = ik
        vals.append(m)
        onehots.append(oh)
        l = jnp.where(oh, -jnp.inf, l)
    exps = [jnp.exp(v - vals[0]) for v in vals]
    den = exps[0] + exps[1] + exps[2] + exps[3]
    for k in range(TOP_K):
        gw_ref[k:k + 1, :] = exps[k] / den
    member = jnp.zeros(logits.shape, F32)
    for oh in onehots:
        member = member + jnp.where(oh, 1.0, 0.0)
    before = jnp.dot(member.astype(BF16), tri_ref[...], preferred_element_type=F32) + cnt_ref[...].astype(F32)
    for k in range(TOP_K):
        rank_ref[k:k + 1, :] = jnp.sum(jnp.where(onehots[k], before, 0.0), axis=0, keepdims=True).astype(jnp.int32)
    cnt_ref[...] += jnp.sum(member, axis=1, keepdims=True).astype(jnp.int32)


def _router(hmid, g_ffn, w_router_t, b_router_c):
    rt = ROUTER_TILE
    tri = jnp.asarray(np.triu(np.ones((rt, rt), np.float32), 1), BF16)
    tok = lambda i: (0, i)
    return pl.pallas_call(
        _router_kernel, grid=(T_PAD // rt,),
        in_specs=[pl.BlockSpec((rt, D_MODEL), lambda i: (i, 0)), _full((1, D_MODEL)),
                  _full((N_EXPERTS, D_MODEL)), _full((N_EXPERTS, 1)), _full((rt, rt))],
        out_specs=[pl.BlockSpec((rt, D_PACK), lambda i: (i, 0)), pl.BlockSpec((TOP_K, rt), tok),
                   pl.BlockSpec((TOP_K, rt), tok), pl.BlockSpec((TOP_K, rt), tok), _full((N_EXPERTS, 1))],
        out_shape=[jax.ShapeDtypeStruct((T_PAD, D_PACK), jnp.int32), jax.ShapeDtypeStruct((TOP_K, T_PAD), jnp.int32),
                   jax.ShapeDtypeStruct((TOP_K, T_PAD), F32), jax.ShapeDtypeStruct((TOP_K, T_PAD), jnp.int32),
                   jax.ShapeDtypeStruct((N_EXPERTS, 1), jnp.int32)],
        name="router",
        compiler_params=pltpu.CompilerParams(dimension_semantics=("arbitrary",), vmem_limit_bytes=VMEM_LIMIT),
    )(hmid, g_ffn, w_router_t, b_router_c, tri)


def _place_kernel(off_ref, idx_ref, rank_ref, pos_ref):
    idx = idx_ref[...]
    pos = rank_ref[...]
    for e in range(N_EXPERTS):
        pos = pos + jnp.where(idx == e, off_ref[e], 0)
    pos_ref[...] = pos


def _place(offsets, idx, rank):
    return pl.pallas_call(
        _place_kernel,
        in_specs=[pl.BlockSpec(memory_space=pltpu.SMEM), pl.BlockSpec(memory_space=pltpu.VMEM),
                  pl.BlockSpec(memory_space=pltpu.VMEM)],
        out_specs=pl.BlockSpec(memory_space=pltpu.VMEM),
        out_shape=jax.ShapeDtypeStruct((TOP_K, T_PAD), jnp.int32), name="place",
    )(offsets, idx, rank)


def _sc_mesh():
    return plsc.VectorSubcoreMesh(core_axis_name="core", subcore_axis_name="subcore")


def _sc_dispatch(rows, pos):
    n, d = rows.shape
    nblk = n // SC_ROWS
    pos_w = pos.reshape(TOP_K, nblk, SC_ROWS).transpose(1, 0, 2)

    @functools.partial(pl.kernel, out_type=jax.ShapeDtypeStruct((P_ROWS, d), rows.dtype), mesh=_sc_mesh(),
                       scratch_types=[], name="dispatch")
    def run(x_hbm, i_hbm, o_hbm):
        def body(x_vmem, i_vmem):
            for k in range(TOP_K):
                pltpu.sync_copy(x_vmem, o_hbm.at[i_vmem.at[0, k]])

        pltpu.emit_pipeline(
            body, grid=(nblk,),
            in_specs=[pl.BlockSpec((SC_ROWS, d), lambda i: (i, 0)),
                      pl.BlockSpec((1, TOP_K, SC_ROWS), lambda i: (i, 0, 0))],
            out_specs=[], core_axis_name=("core", "subcore"), dimension_semantics=(pltpu.PARALLEL,),
        )(x_hbm, i_hbm)

    return run(rows, pos_w)


def _sc_combine(rows, flat_pos):
    _, d = rows.shape
    n = flat_pos.shape[0]
    nblk = n // SC_ROWS
    pos_w = flat_pos.reshape(nblk, 1, SC_ROWS)

    @functools.partial(pl.kernel, out_type=jax.ShapeDtypeStruct((n, d), rows.dtype), mesh=_sc_mesh(),
                       scratch_types=[], name="combine")
    def run(x_hbm, i_hbm, o_hbm):
        def body(i_vmem, o_vmem):
            pltpu.sync_copy(x_hbm.at[i_vmem.at[0, 0]], o_vmem)

        pltpu.emit_pipeline(
            body, grid=(nblk,),
            in_specs=[pl.BlockSpec((1, 1, SC_ROWS), lambda i: (i, 0, 0))],
            out_specs=[pl.BlockSpec((SC_ROWS, d), lambda i: (i, 0))],
            core_axis_name=("core", "subcore"), dimension_semantics=(pltpu.PARALLEL,),
        )(i_hbm, o_hbm)

    return run(rows, pos_w)


def _experts_kernel(te_ref, nv_ref, x_ref, w1_ref, b1_ref, w2_ref, b2_ref, y_ref, w1_s, w2_s):
    i = pl.program_id(0)
    e = te_ref[i]
    prev = te_ref[jnp.maximum(i - 1, 0)]

    @pl.when((i == 0) | (e != prev))
    def _():
        for r in range(4):
            rs = slice(256 * r, 256 * (r + 1))
            w1_s[rs, :] = w1_ref[0, rs, :].astype(BF16)
            w2_s[rs, :] = w2_ref[0, rs, :].astype(BF16)

    @pl.when(i < nv_ref[0])
    def _():
        hdn = _bdot(_unpack_pairs(x_ref[...]), w1_s[...]) + b1_ref[0]
        gt = jnp.minimum(hdn[:, :D_FF], SWIGLU_LIMIT)
        up = jnp.clip(hdn[:, D_FF:], -SWIGLU_LIMIT, SWIGLU_LIMIT)
        act = (up + 1.0) * gt * _sigmoid(SWIGLU_ALPHA * gt)
        y_ref[...] = _pack_pairs(_bdot(act, w2_s[...]) + b2_ref[0])


def _experts(tile_expert, n_valid, xs, w_up, b_up, w_down, b_down):
    row = lambda i, te, nv: (jnp.minimum(i, nv[0] - 1), 0)
    wsel = lambda i, te, nv: (te[i], 0, 0)
    grid_spec = pltpu.PrefetchScalarGridSpec(
        num_scalar_prefetch=2, grid=(N_TILES,),
        in_specs=[pl.BlockSpec((TM, D_PACK), row),
                  pl.BlockSpec((1, D_MODEL, 2 * D_FF), wsel), pl.BlockSpec((1, 1, 2 * D_FF), wsel),
                  pl.BlockSpec((1, D_FF, D_MODEL), wsel), pl.BlockSpec((1, 1, D_MODEL), wsel)],
        out_specs=pl.BlockSpec((TM, D_PACK), row),
        scratch_shapes=[pltpu.VMEM((D_MODEL, 2 * D_FF), BF16), pltpu.VMEM((D_FF, D_MODEL), BF16)],
    )
    return pl.pallas_call(
        _experts_kernel, grid_spec=grid_spec, out_shape=jax.ShapeDtypeStruct((P_ROWS, D_PACK), jnp.int32),
        name="experts",
        compiler_params=pltpu.CompilerParams(dimension_semantics=("arbitrary",), vmem_limit_bytes=VMEM_LIMIT),
    )(tile_expert, n_valid, xs, w_up, b_up.reshape(N_EXPERTS, 1, 2 * D_FF), w_down,
      b_down.reshape(N_EXPERTS, 1, D_MODEL))


def _final_kernel(h_ref, yg_ref, gw_ref, p_ref, gple_ref, wg_ref, wp_ref, gfin_ref, out_ref):
    rows = h_ref.shape[0]
    gw = gw_ref[...]
    h = h_ref[...]
    for k in range(TOP_K):
        h = h + gw[:, k:k + 1] * _unpack_pairs(yg_ref[k])
    gate = _sigmoid(_bdot(_rms(h, gple_ref[...]), wg_ref[...]))
    h = h + gate * _bdot(p_ref[...].reshape(rows, PLE_DIM), wp_ref[...])
    out_ref[...] = _rms(h, gfin_ref[...]).reshape(out_ref.shape)


def _final_prompt(hmid, yg, gwt, p_prompt, gple, wg, wp, gfin):
    nb = 4
    rows = nb * CHUNK
    rb = lambda c, b: (c * (BATCH // nb) + b, 0)
    return pl.pallas_call(
        _final_kernel, grid=(N_CHUNKS, BATCH // nb),
        in_specs=[pl.BlockSpec((rows, D_MODEL), rb),
                  pl.BlockSpec((TOP_K, rows, D_PACK), lambda c, b: (0, c * (BATCH // nb) + b, 0)),
                  pl.BlockSpec((rows, TOP_K), rb),
                  pl.BlockSpec((nb, CHUNK, PLE_DIM), lambda c, b: (b, c, 0)),
                  _full((1, D_MODEL)), _full((D_MODEL, D_MODEL)), _full((PLE_DIM, D_MODEL)), _full((1, D_MODEL))],
        out_specs=pl.BlockSpec((nb, CHUNK, D_MODEL), lambda c, b: (b, c, 0)),
        out_shape=jax.ShapeDtypeStruct((BATCH, SEQ, D_MODEL), F32), name="final_prompt",
        compiler_params=pltpu.CompilerParams(dimension_semantics=("arbitrary", "arbitrary"),
                                             vmem_limit_bytes=VMEM_LIMIT),
    )(hmid, yg, gwt, p_prompt, gple, wg, wp, gfin)


def _final_sample(hmid, yg, gwt, p_sample, gple, wg, wp, gfin):
    blk = SAMPLE_ROW0 // DEC_BATCH
    return pl.pallas_call(
        _final_kernel, grid=(1,),
        in_specs=[pl.BlockSpec((DEC_BATCH, D_MODEL), lambda i: (blk, 0)),
                  pl.BlockSpec((TOP_K, DEC_BATCH, D_PACK), lambda i: (0, blk, 0)),
                  pl.BlockSpec((DEC_BATCH, TOP_K), lambda i: (blk, 0)),
                  _full((DEC_BATCH, PLE_DIM)),
                  _full((1, D_MODEL)), _full((D_MODEL, D_MODEL)), _full((PLE_DIM, D_MODEL)), _full((1, D_MODEL))],
        out_specs=_full((DEC_BATCH, D_MODEL)),
        out_shape=jax.ShapeDtypeStruct((DEC_BATCH, D_MODEL), F32), name="final_sample",
        compiler_params=pltpu.CompilerParams(dimension_semantics=("arbitrary",), vmem_limit_bytes=VMEM_LIMIT),
    )(hmid, yg, gwt, p_sample, gple, wg, wp, gfin)


def _alibi_tables():
    slopes = 2.0 ** (-8.0 * (np.arange(N_HEADS, dtype=np.float64) + 1.0) / N_HEADS)
    i = np.arange(CHUNK)[:, None]
    j = np.arange(2 * CHUNK)[None, :]
    dist = i + CHUNK - j
    valid = (dist >= 0) & (dist <= WINDOW)
    tabs = []
    for has_prev in (False, True):
        ok = valid & ((j >= CHUNK) | has_prev)
        tabs.append(np.where(ok[None], -slopes[:, None, None] * dist[None], NEG))
    prompt = np.stack(tabs).astype(np.float32)
    wb = min(WINDOW, PAST_LEN)
    sample = (-slopes[:, None] * (wb - np.arange(wb))[None, :]).astype(np.float32)
    return prompt, sample


def kernel(x_prompt, x_sample, cache_k_win, cache_v_win, state_ssm_re, state_ssm_im, p_prompt, p_sample, norm_mix, w_in, sinks, ssm_lam_re, ssm_lam_im, ssm_log_step, ssm_b_re, ssm_b_im, ssm_c_re, ssm_c_im, ssm_d, w_glu, b_glu, norm_attn_out, norm_ssm_out, w_out, norm_ffn, w_router, b_router, w_up, b_up, w_down, b_down, norm_ple, w_ple_gate, w_ple_proj, norm_final):
    nst = SSM_G * SSM_P
    bias_np, sbias_np = _alibi_tables()
    bias = jnp.asarray(bias_np)
    sbias = jnp.asarray(sbias_np)

    lbr, lbi, bbr, bbi = _prep(ssm_lam_re[0], ssm_lam_im[0], ssm_log_step[0], ssm_b_re[0], ssm_b_im[0])
    bblk, lamr, lami, cblk = _s5_blocks(lbr, lbi, bbr, bbi, ssm_c_re[0], ssm_c_im[0])

    gmix = norm_mix[0].reshape(1, D_MODEL)
    win = w_in[0].astype(BF16)
    dskip = ssm_d[0].reshape(1, D_SSM)
    wglu = w_glu[0].astype(BF16)
    bglu = b_glu[0].reshape(1, D_SSM)
    ga = norm_attn_out[0].reshape(1, D_ATTN)
    gs = norm_ssm_out[0].reshape(1, D_SSM)
    wout = w_out[0].astype(BF16)
    sink = sinks[0]

    hbuf, k_last, v_last, re_p, im_p = _mixer_prompt(
        x_prompt, sink, gmix, win, bias, bblk, lamr, lami, cblk, dskip, wglu, bglu, ga, gs, wout)

    ck = cache_k_win[0].reshape(DEC_BATCH, WINDOW, D_KV)
    cv = cache_v_win[0].reshape(DEC_BATCH, WINDOW, D_KV)
    hmid, k_new, v_new, re_s, im_s = _mixer_sample(
        x_sample.reshape(DEC_BATCH, D_MODEL), gmix, win, ck, cv, sbias, sink.reshape(N_HEADS, 1), bblk, lamr, lami,
        cblk, dskip, wglu, bglu, ga, gs, wout, state_ssm_re[0].reshape(DEC_BATCH, nst),
        state_ssm_im[0].reshape(DEC_BATCH, nst), hbuf)

    hn, idx, gw, rank, counts = _router(hmid, norm_ffn[0].reshape(1, D_MODEL), w_router[0].T,
                                        b_router[0].reshape(N_EXPERTS, 1))

    tiles_per = (counts[:, 0] + (TM - 1)) // TM
    tile_end = jnp.cumsum(tiles_per)
    offsets = ((tile_end - tiles_per) * TM).astype(jnp.int32)
    n_valid = tile_end[-1:].astype(jnp.int32)
    tile_ids = jnp.minimum(jnp.arange(N_TILES, dtype=jnp.int32), n_valid[0] - 1)
    tile_expert = jnp.sum((tile_end[None, :] <= tile_ids[:, None]).astype(jnp.int32), axis=1)

    pos = _place(offsets, idx, rank)
    xs = _sc_dispatch(hn, pos)
    ys = _experts(tile_expert, n_valid, xs, w_up[0], b_up[0], w_down[0], b_down[0])
    yg = _sc_combine(ys, pos.reshape(TOP_K * T_PAD)).reshape(TOP_K, T_PAD, D_PACK)

    gwt = gw.T
    gple = norm_ple[0].reshape(1, D_MODEL)
    wg = w_ple_gate[0].astype(BF16)
    wp = w_ple_proj[0].astype(BF16)
    gfin = norm_final.reshape(1, D_MODEL)
    y_prompt = _final_prompt(hmid, yg, gwt, p_prompt[0], gple, wg, wp, gfin)
    y_sample = _final_sample(hmid, yg, gwt, p_sample[0].reshape(DEC_BATCH, PLE_DIM), gple, wg, wp, gfin)

    k_win_s = jnp.concatenate([ck[:, 1:], k_new[:, None, :]], axis=1)
    v_win_s = jnp.concatenate([cv[:, 1:], v_new[:, None, :]], axis=1)
    kv5 = (1, BATCH, CHUNK, N_KV, HEAD_DIM)
    skv5 = (1, DEC_BATCH, WINDOW, N_KV, HEAD_DIM)
    return (y_prompt, y_sample.reshape(DEC_BATCH, 1, D_MODEL),
            k_last.reshape(kv5), v_last.reshape(kv5),
            re_p.reshape(1, BATCH, SSM_G, SSM_P), im_p.reshape(1, BATCH, SSM_G, SSM_P),
            k_win_s.reshape(skv5), v_win_s.reshape(skv5),
            re_s.reshape(1, DEC_BATCH, SSM_G, SSM_P), im_s.reshape(1, DEC_BATCH, SSM_G, SSM_P))
```

```python
import functools

import numpy as np
import jax
import jax.numpy as jnp
from jax import lax
from jax.experimental import pallas as pl
from jax.experimental.pallas import tpu as pltpu
from jax.experimental.pallas import tpu_sc as plsc

F32 = jnp.float32
BF16 = jnp.bfloat16

D_MODEL = 1024
BATCH = 8
SEQ = 2048
DEC_BATCH = 128
PAST_LEN = 16384
HEAD_DIM = 64
D_ATTN = 512
N_HEADS = 8
N_KV = 2
D_KV = N_KV * HEAD_DIM
WINDOW = 128
D_SSM = 512
SSM_H = 16
SSM_G = 32
SSM_P = 64
D_IN = D_ATTN + 2 * D_KV + D_SSM
N_EXPERTS = 32
TOP_K = 4
D_FF = 1024
SWIGLU_LIMIT = 7.0
SWIGLU_ALPHA = 1.702
PLE_DIM = 256
EPS = 1e-5
NEG = -1e30

CHUNK = 128
N_CHUNKS = SEQ // CHUNK
ROWS = BATCH * CHUNK
PITCH = CHUNK + 8
SUB_T = 32
SUB_ROWS = SUB_T * BATCH
N_SBLK = 4
SBLK = 512
T_REAL = BATCH * SEQ + DEC_BATCH
T_PAD = T_REAL + 128
SAMPLE_ROW0 = BATCH * SEQ
ROUTER_TILE = 640
TM = 256
N_TILES = (T_PAD * TOP_K + N_EXPERTS * (TM - 1)) // TM
P_ROWS = N_TILES * TM
SC_ROWS = 40
D_PACK = D_MODEL // 2
VMEM_LIMIT = 56 * 1024 * 1024


def _rms(x, g):
    return x * lax.rsqrt(jnp.mean(x * x, axis=-1, keepdims=True) + EPS) * g


def _sigmoid(x):
    return 1.0 / (1.0 + jnp.exp(-x))


def _gelu_tanh(x):
    c = np.float32(np.sqrt(2.0 / np.pi))
    return 0.5 * x * (1.0 + jnp.tanh(c * (x + 0.044715 * (x * x * x))))


def _bdot(a, b):
    return jnp.dot(a.astype(BF16), b, preferred_element_type=F32)


def _pack_pairs(x):
    n = x.shape[1] // 2
    lo = lax.bitcast_convert_type(x[:, :n].astype(BF16).astype(F32), jnp.int32)
    hi = lax.bitcast_convert_type(x[:, n:].astype(BF16).astype(F32), jnp.int32)
    return lax.shift_right_logical(lo, 16) | hi


def _unpack_pairs(w):
    lo = lax.bitcast_convert_type(lax.shift_left(w, 16), F32)
    hi = lax.bitcast_convert_type(w & jnp.int32(-65536), F32)
    return jnp.concatenate([lo, hi], axis=1)


def _full(shape):
    n = len(shape)
    return pl.BlockSpec(shape, lambda *_: (0,) * n)


def _prep_kernel(lr_ref, li_ref, ls_ref, br_ref, bi_ref, lbr_ref, lbi_ref, bbr_ref, bbi_ref):
    lr = lr_ref[...]
    li = li_ref[...]
    step = jnp.exp(ls_ref[...])
    zr = lr * step
    zi = li * step
    mag = jnp.exp(zr)
    lbr = mag * jnp.cos(zi)
    lbi = mag * jnp.sin(zi)
    lbr_ref[...] = lbr
    lbi_ref[...] = lbi
    nr = lbr - 1.0
    den = lr * lr + li * li
    cr = (nr * lr + lbi * li) / den
    ci = (lbi * lr - nr * li) / den
    br = br_ref[...]
    bi = bi_ref[...]
    bbr_ref[...] = cr * br - ci * bi
    bbi_ref[...] = cr * bi + ci * br


def _prep(lam_re, lam_im, log_step, b_re, b_im):
    g, p, h = SSM_G, SSM_P, SSM_H
    out = pl.pallas_call(
        _prep_kernel,
        out_shape=[jax.ShapeDtypeStruct((g, 1, p), F32), jax.ShapeDtypeStruct((g, 1, p), F32),
                   jax.ShapeDtypeStruct((g, h, p), F32), jax.ShapeDtypeStruct((g, h, p), F32)],
        name="s5_prep",
    )(lam_re.reshape(g, 1, p), lam_im.reshape(g, 1, p), log_step.reshape(g, 1, 1),
      jnp.transpose(b_re, (0, 2, 1)), jnp.transpose(b_im, (0, 2, 1)))
    return out


def _s5_blocks(lbr, lbi, bbr, bbi, c_re, c_im):
    eye = jnp.eye(8, dtype=F32)
    shp = (N_SBLK, 8, SSM_H, SSM_P)
    b_r = jnp.einsum('jghp,gk->jghkp', bbr.reshape(shp), eye).reshape(N_SBLK, 128, SBLK)
    b_i = jnp.einsum('jghp,gk->jghkp', bbi.reshape(shp), eye).reshape(N_SBLK, 128, SBLK)
    bblk = jnp.concatenate([b_r, b_i], axis=-1).astype(BF16)
    c_r = jnp.einsum('jghp,gk->jgpkh', c_re.reshape(shp), eye).reshape(N_SBLK, SBLK, 128)
    c_i = jnp.einsum('jghp,gk->jgpkh', c_im.reshape(shp), eye).reshape(N_SBLK, SBLK, 128)
    cblk = jnp.concatenate([c_r, -c_i], axis=1).astype(BF16)
    return bblk, lbr.reshape(N_SBLK, 1, SBLK), lbi.reshape(N_SBLK, 1, SBLK), cblk


def _ssm_post(y_lin, u, dskip, wglu, bglu, gs):
    y = _gelu_tanh(y_lin + dskip * u)
    y = y * _sigmoid(_bdot(y, wglu) + bglu)
    return _rms(y, gs)


def _mixer_prompt_kernel(sinks_ref, x_ref, gmix_ref, win_ref, bias_ref, bblk_ref, lamr_ref, lami_ref,
                         cblk_ref, dskip_ref, wglu_ref, bglu_ref, ga_ref, gs_ref, wout_ref,
                         hmid_ref, klast_ref, vlast_ref, sre_ref, sim_ref,
                         proj_s, u_s, kprev_s, kprevr_s, vprev_s, vprevr_s, attn_s, ssm_s, utb_s, bu_s, xs_s, ytb_s):
    c = pl.program_id(0)

    @pl.when(c == 0)
    def _():
        zkv = jnp.zeros(kprev_s.shape, BF16)
        kprev_s[...] = zkv
        kprevr_s[...] = zkv
        vprev_s[...] = zkv
        vprevr_s[...] = zkv
        sre_ref[...] = jnp.zeros(sre_ref.shape, F32)
        sim_ref[...] = jnp.zeros(sim_ref.shape, F32)

    x = x_ref[...].reshape(ROWS, D_MODEL)
    proj = _bdot(_rms(x, gmix_ref[...]), win_ref[...])
    u0 = D_ATTN + 2 * D_KV
    proj_s[...] = proj[:, 0:u0]
    for jb in range(D_SSM // 128):
        for b in range(BATCH):
            u_s[jb, b * PITCH:b * PITCH + CHUNK, :] = proj[b * CHUNK:(b + 1) * CHUNK, u0 + 128 * jb:u0 + 128 * (jb + 1)]

    lo = lax.broadcasted_iota(jnp.int32, (CHUNK, 128), 1) < HEAD_DIM
    table = jnp.minimum(c, 1)

    def attn_body(b, carry):
        r0 = pl.multiple_of(b * CHUNK, CHUNK)
        rows = pl.ds(r0, CHUNK)
        kb = proj_s[rows, D_ATTN:D_ATTN + D_KV]
        vb = proj_s[rows, D_ATTN + D_KV:D_ATTN + 2 * D_KV]
        kb16 = kb.astype(BF16)
        vb16 = vb.astype(BF16)
        kbr16 = pltpu.roll(kb, HEAD_DIM, 1).astype(BF16)
        vbr16 = pltpu.roll(vb, HEAD_DIM, 1).astype(BF16)
        k_nat = jnp.concatenate([kprev_s[b], kb16], axis=0)
        k_rot = jnp.concatenate([kprevr_s[b], kbr16], axis=0)
        v_nat = jnp.concatenate([vprev_s[b], vb16], axis=0)
        v_rot = jnp.concatenate([vprevr_s[b], vbr16], axis=0)
        for jq in range(N_HEADS // 2):
            q2 = proj_s[rows, 128 * jq:128 * (jq + 1)] * (HEAD_DIM ** -0.5)
            kv = jq // 2
            outs = []
            for half in range(2):
                h = 2 * jq + half
                qm = jnp.where(lo if half == 0 else jnp.logical_not(lo), q2, 0.0).astype(BF16)
                kx = k_nat if kv == half else k_rot
                vx = v_nat if kv == half else v_rot
                s = lax.dot_general(qm, kx, (((1,), (1,)), ((), ())), preferred_element_type=F32)
                s = s + bias_ref[table, h]
                sink = sinks_ref[h]
                m = jnp.maximum(jnp.max(s, axis=-1, keepdims=True), sink)
                p = jnp.exp(s - m)
                den = jnp.sum(p, axis=-1, keepdims=True) + jnp.exp(sink - m)
                outs.append(jnp.dot(p.astype(BF16), vx, preferred_element_type=F32) / den)
            attn_s[rows, 128 * jq:128 * (jq + 1)] = jnp.where(lo, outs[0], outs[1])
        kprev_s[b] = kb16
        kprevr_s[b] = kbr16
        vprev_s[b] = vb16
        vprevr_s[b] = vbr16
        return carry

    lax.fori_loop(0, BATCH, attn_body, 0)

    for sc in range(CHUNK // SUB_T):
        t0 = sc * SUB_T
        for i in range(SUB_T):
            for jb in range(D_SSM // 128):
                utb_s[i * BATCH:(i + 1) * BATCH, 128 * jb:128 * (jb + 1)] = (
                    u_s[jb, pl.ds(t0 + i, BATCH, stride=PITCH), :])
        u_tb = utb_s[...]
        for j in range(N_SBLK):
            bu_s[...] = _bdot(u_tb[:, 128 * j:128 * (j + 1)], bblk_ref[j])
            lr = jnp.broadcast_to(lamr_ref[j], (BATCH, SBLK))
            li = jnp.broadcast_to(lami_ref[j], (BATCH, SBLK))
            cols = slice(SBLK * j, SBLK * (j + 1))

            def step(i, carry):
                sr, si = carry
                r = pl.ds(pl.multiple_of(i * BATCH, BATCH), BATCH)
                nr = lr * sr - li * si + bu_s[r, 0:SBLK]
                ni = lr * si + li * sr + bu_s[r, SBLK:2 * SBLK]
                xs_s[r, 0:SBLK] = nr
                xs_s[r, SBLK:2 * SBLK] = ni
                return nr, ni

            sr, si = lax.fori_loop(0, SUB_T, step, (sre_ref[:, cols], sim_ref[:, cols]))
            sre_ref[:, cols] = sr
            sim_ref[:, cols] = si
            ytb_s[:, 128 * j:128 * (j + 1)] = _bdot(xs_s[...], cblk_ref[j])
        yn = _ssm_post(ytb_s[...], u_tb, dskip_ref[...], wglu_ref[...], bglu_ref[...], gs_ref[...])
        for i in range(SUB_T):
            for jb in range(D_SSM // 128):
                ssm_s[jb, pl.ds(t0 + i, BATCH, stride=PITCH), :] = (
                    yn[i * BATCH:(i + 1) * BATCH, 128 * jb:128 * (jb + 1)])

    an = _rms(attn_s[...], ga_ref[...])
    sn = jnp.concatenate(
        [jnp.concatenate([ssm_s[jb, b * PITCH:b * PITCH + CHUNK, :] for b in range(BATCH)], axis=0)
         for jb in range(D_SSM // 128)], axis=1)
    hmid_ref[...] = x + _bdot(an, wout_ref[0:D_ATTN, :]) + _bdot(sn, wout_ref[D_ATTN:2 * D_ATTN, :])

    @pl.when(c == N_CHUNKS - 1)
    def _():
        klast_ref[...] = proj_s[:, D_ATTN:D_ATTN + D_KV].reshape(BATCH, CHUNK, D_KV)
        vlast_ref[...] = proj_s[:, D_ATTN + D_KV:D_ATTN + 2 * D_KV].reshape(BATCH, CHUNK, D_KV)


def _mixer_prompt(x_prompt, sinks, gmix, win, bias, bblk, lamr, lami, cblk, dskip, wglu, bglu, ga, gs, wout):
    smem = pl.BlockSpec(memory_space=pltpu.SMEM)
    in_specs = [
        smem,
        pl.BlockSpec((BATCH, CHUNK, D_MODEL), lambda c: (0, c, 0)),
        _full((1, D_MODEL)), _full((D_MODEL, D_IN)), _full((2, N_HEADS, CHUNK, 2 * CHUNK)),
        _full((N_SBLK, 128, 2 * SBLK)), _full((N_SBLK, 1, SBLK)), _full((N_SBLK, 1, SBLK)),
        _full((N_SBLK, 2 * SBLK, 128)), _full((1, D_SSM)), _full((D_SSM, D_SSM)), _full((1, D_SSM)),
        _full((1, D_ATTN)), _full((1, D_SSM)), _full((D_MODEL, D_MODEL)),
    ]
    out_shape = [
        jax.ShapeDtypeStruct((T_PAD, D_MODEL), F32),
        jax.ShapeDtypeStruct((BATCH, CHUNK, D_KV), F32),
        jax.ShapeDtypeStruct((BATCH, CHUNK, D_KV), F32),
        jax.ShapeDtypeStruct((BATCH, SSM_G * SSM_P), F32),
        jax.ShapeDtypeStruct((BATCH, SSM_G * SSM_P), F32),
    ]
    out_specs = [
        pl.BlockSpec((ROWS, D_MODEL), lambda c: (c, 0)),
        _full((BATCH, CHUNK, D_KV)), _full((BATCH, CHUNK, D_KV)),
        _full((BATCH, SSM_G * SSM_P)), _full((BATCH, SSM_G * SSM_P)),
    ]
    kv_scr = pltpu.VMEM((BATCH, CHUNK, D_KV), BF16)
    scratch = [
        pltpu.VMEM((ROWS, D_ATTN + 2 * D_KV), F32), pltpu.VMEM((D_SSM // 128, BATCH * PITCH, 128), F32),
        kv_scr, kv_scr, kv_scr, kv_scr,
        pltpu.VMEM((ROWS, D_ATTN), F32), pltpu.VMEM((D_SSM // 128, BATCH * PITCH, 128), F32),
        pltpu.VMEM((SUB_ROWS, D_SSM), F32), pltpu.VMEM((SUB_ROWS, 2 * SBLK), F32),
        pltpu.VMEM((SUB_ROWS, 2 * SBLK), F32), pltpu.VMEM((SUB_ROWS, D_SSM), F32),
    ]
    return pl.pallas_call(
        _mixer_prompt_kernel, grid=(N_CHUNKS,), in_specs=in_specs, out_specs=out_specs, out_shape=out_shape,
        scratch_shapes=scratch, name="mixer_prompt",
        compiler_params=pltpu.CompilerParams(dimension_semantics=("arbitrary",), vmem_limit_bytes=VMEM_LIMIT),
    )(sinks, x_prompt, gmix, win, bias, bblk, lamr, lami, cblk, dskip, wglu, bglu, ga, gs, wout)


SGRP = 16
N_SGRP = DEC_BATCH // SGRP


def _mixer_sample_kernel(x_ref, gmix_ref, win_ref, ck_ref, cv_ref, sbias_ref, sinkc_ref, bblk_ref, lamr_ref,
                         lami_ref, cblk_ref, dskip_ref, wglu_ref, bglu_ref, ga_ref, gs_ref, wout_ref,
                         x0r_ref, x0i_ref, hbuf_ref,
                         hmid_ref, knew_ref, vnew_ref, sre_ref, sim_ref,
                         proj_s, qall_s, oall_s, kn8_s, vn8_s):
    del hbuf_ref
    g = pl.program_id(0)
    lo = lax.broadcasted_iota(jnp.int32, (DEC_BATCH, 128), 1) < HEAD_DIM

    @pl.when(g == 0)
    def _():
        proj = _bdot(_rms(x_ref[...], gmix_ref[...]), win_ref[...])
        proj_s[...] = proj
        for h in range(N_HEADS):
            jq, half, kv = h // 2, h % 2, h // 4
            q2 = proj[:, 128 * jq:128 * (jq + 1)] * (HEAD_DIM ** -0.5)
            if half != kv:
                q2 = pltpu.roll(q2, HEAD_DIM, 1)
            qall_s[h * DEC_BATCH:(h + 1) * DEC_BATCH, :] = jnp.where(lo if kv == 0 else jnp.logical_not(lo), q2, 0.0)
            kn8_s[h * DEC_BATCH:(h + 1) * DEC_BATCH, :] = proj[:, D_ATTN:D_ATTN + D_KV]
            vn8_s[h * DEC_BATCH:(h + 1) * DEC_BATCH, :] = proj[:, D_ATTN + D_KV:D_ATTN + 2 * D_KV]

    def attn_body(ii, carry):
        i = g * SGRP + ii
        qe = qall_s[pl.ds(i, N_HEADS, stride=DEC_BATCH), :]
        s = lax.dot_general(qe.astype(BF16), ck_ref[ii].astype(BF16), (((1,), (1,)), ((), ())),
                            preferred_element_type=F32) + sbias_ref[...]
        knew = kn8_s[pl.ds(i, N_HEADS, stride=DEC_BATCH), :]
        vnew = vn8_s[pl.ds(i, N_HEADS, stride=DEC_BATCH), :]
        s_new = jnp.sum(qe * knew, axis=-1, keepdims=True)
        sink = sinkc_ref[...]
        m = jnp.maximum(jnp.maximum(jnp.max(s, axis=-1, keepdims=True), s_new), sink)
        p = jnp.exp(s - m)
        p_new = jnp.exp(s_new - m)
        den = jnp.sum(p, axis=-1, keepdims=True) + p_new + jnp.exp(sink - m)
        o = (jnp.dot(p.astype(BF16), cv_ref[ii].astype(BF16), preferred_element_type=F32) + p_new * vnew) / den
        oall_s[pl.ds(i, N_HEADS, stride=DEC_BATCH), :] = o
        return carry

    lax.fori_loop(0, SGRP, attn_body, 0)

    @pl.when(g == N_SGRP - 1)
    def _():
        proj = proj_s[...]
        blocks = []
        for jq in range(N_HEADS // 2):
            oa = oall_s[(2 * jq) * DEC_BATCH:(2 * jq + 1) * DEC_BATCH, :]
            ob = oall_s[(2 * jq + 1) * DEC_BATCH:(2 * jq + 2) * DEC_BATCH, :]
            if jq // 2 == 0:
                blocks.append(jnp.where(lo, oa, pltpu.roll(ob, HEAD_DIM, 1)))
            else:
                blocks.append(jnp.where(lo, pltpu.roll(oa, HEAD_DIM, 1), ob))
        attn = jnp.concatenate(blocks, axis=1)
        u = proj[:, D_ATTN + 2 * D_KV:]
        ys = []
        for j in range(N_SBLK):
            bu = _bdot(u[:, 128 * j:128 * (j + 1)], bblk_ref[j])
            lr = lamr_ref[j]
            li = lami_ref[j]
            cols = slice(SBLK * j, SBLK * (j + 1))
            sr = x0r_ref[:, cols]
            si = x0i_ref[:, cols]
            nr = lr * sr - li * si + bu[:, 0:SBLK]
            ni = lr * si + li * sr + bu[:, SBLK:2 * SBLK]
            sre_ref[:, cols] = nr
            sim_ref[:, cols] = ni
            ys.append(_bdot(jnp.concatenate([nr, ni], axis=1), cblk_ref[j]))
        sn = _ssm_post(jnp.concatenate(ys, axis=1), u, dskip_ref[...], wglu_ref[...], bglu_ref[...], gs_ref[...])
        an = _rms(attn, ga_ref[...])
        h = x_ref[...] + _bdot(an, wout_ref[0:D_ATTN, :]) + _bdot(sn, wout_ref[D_ATTN:2 * D_ATTN, :])
        hmid_ref[0:DEC_BATCH, :] = h
        hmid_ref[DEC_BATCH:2 * DEC_BATCH, :] = jnp.zeros((DEC_BATCH, D_MODEL), F32)
        knew_ref[...] = proj[:, D_ATTN:D_ATTN + D_KV]
        vnew_ref[...] = proj[:, D_ATTN + D_KV:D_ATTN + 2 * D_KV]


def _mixer_sample(x_s, gmix, win, ck, cv, sbias, sinkc, bblk, lamr, lami, cblk, dskip, wglu, bglu, ga, gs, wout,
                  x0r, x0i, hbuf):
    nst = SSM_G * SSM_P
    in_specs = [
        _full((DEC_BATCH, D_MODEL)), _full((1, D_MODEL)), _full((D_MODEL, D_IN)),
        pl.BlockSpec((SGRP, WINDOW, D_KV), lambda g: (g, 0, 0)),
        pl.BlockSpec((SGRP, WINDOW, D_KV), lambda g: (g, 0, 0)),
        _full((N_HEADS, WINDOW)), _full((N_HEADS, 1)),
        _full((N_SBLK, 128, 2 * SBLK)), _full((N_SBLK, 1, SBLK)), _full((N_SBLK, 1, SBLK)),
        _full((N_SBLK, 2 * SBLK, 128)), _full((1, D_SSM)), _full((D_SSM, D_SSM)), _full((1, D_SSM)),
        _full((1, D_ATTN)), _full((1, D_SSM)), _full((D_MODEL, D_MODEL)),
        _full((DEC_BATCH, nst)), _full((DEC_BATCH, nst)),
        pl.BlockSpec(memory_space=pl.ANY),
    ]
    out_shape = [
        jax.ShapeDtypeStruct((T_PAD, D_MODEL), F32),
        jax.ShapeDtypeStruct((DEC_BATCH, D_KV), F32), jax.ShapeDtypeStruct((DEC_BATCH, D_KV), F32),
        jax.ShapeDtypeStruct((DEC_BATCH, nst), F32), jax.ShapeDtypeStruct((DEC_BATCH, nst), F32),
    ]
    out_specs = [
        pl.BlockSpec((2 * DEC_BATCH, D_MODEL), lambda g: (SAMPLE_ROW0 // (2 * DEC_BATCH), 0)),
        _full((DEC_BATCH, D_KV)), _full((DEC_BATCH, D_KV)),
        _full((DEC_BATCH, nst)), _full((DEC_BATCH, nst)),
    ]
    head_rows = pltpu.VMEM((N_HEADS * DEC_BATCH, 128), F32)
    scratch = [pltpu.VMEM((DEC_BATCH, D_IN), F32), head_rows, head_rows, head_rows, head_rows]
    return pl.pallas_call(
        _mixer_sample_kernel, grid=(N_SGRP,), in_specs=in_specs, out_specs=out_specs, out_shape=out_shape,
        scratch_shapes=scratch, input_output_aliases={19: 0}, name="mixer_sample",
        compiler_params=pltpu.CompilerParams(dimension_semantics=("arbitrary",), vmem_limit_bytes=VMEM_LIMIT),
    )(x_s, gmix, win, ck, cv, sbias, sinkc, bblk, lamr, lami, cblk, dskip, wglu, bglu, ga, gs, wout, x0r, x0i, hbuf)


def _router_kernel(h_ref, g_ref, wrt_ref, br_ref, tri_ref, hn_ref, idx_ref, gw_ref, rank_ref, cnt_ref):
    i = pl.program_id(0)

    @pl.when(i == 0)
    def _():
        cnt_ref[...] = jnp.zeros(cnt_ref.shape, jnp.int32)

    hn = _rms(h_ref[...], g_ref[...])
    hn_ref[...] = _pack_pairs(hn)
    logits = lax.dot_general(wrt_ref[...], hn, (((1,), (1,)), ((), ())), precision=lax.Precision.HIGHEST,
                             preferred_element_type=F32) + br_ref[...]
    eidx = lax.broadcasted_iota(jnp.int32, logits.shape, 0)
    vals, onehots = [], []
    l = logits
    for k in range(TOP_K):
        m = jnp.max(l, axis=0, keepdims=True)
        ik = jnp.min(jnp.where(l == m, eidx, N_EXPERTS), axis=0, keepdims=True)
        oh = eidx == ik
        idx_ref[k:k + 1, :] = ik
        vals.append(m)
        onehots.append(oh)
        l = jnp.where(oh, -jnp.inf, l)
    exps = [jnp.exp(v - vals[0]) for v in vals]
    den = exps[0] + exps[1] + exps[2] + exps[3]
    for k in range(TOP_K):
        gw_ref[k:k + 1, :] = exps[k] / den
    member = jnp.zeros(logits.shape, F32)
    for oh in onehots:
        member = member + jnp.where(oh, 1.0, 0.0)
    before = jnp.dot(member.astype(BF16), tri_ref[...], preferred_element_type=F32) + cnt_ref[...].astype(F32)
    for k in range(TOP_K):
        rank_ref[k:k + 1, :] = jnp.sum(jnp.where(onehots[k], before, 0.0), axis=0, keepdims=True).astype(jnp.int32)
    cnt_ref[...] += jnp.sum(member, axis=1, keepdims=True).astype(jnp.int32)


def _router(hmid, g_ffn, w_router_t, b_router_c):
    rt = ROUTER_TILE
    tri = jnp.asarray(np.triu(np.ones((rt, rt), np.float32), 1), BF16)
    tok = lambda i: (0, i)
    return pl.pallas_call(
        _router_kernel, grid=(T_PAD // rt,),
        in_specs=[pl.BlockSpec((rt, D_MODEL), lambda i: (i, 0)), _full((1, D_MODEL)),
                  _full((N_EXPERTS, D_MODEL)), _full((N_EXPERTS, 1)), _full((rt, rt))],
        out_specs=[pl.BlockSpec((rt, D_PACK), lambda i: (i, 0)), pl.BlockSpec((TOP_K, rt), tok),
                   pl.BlockSpec((TOP_K, rt), tok), pl.BlockSpec((TOP_K, rt), tok), _full((N_EXPERTS, 1))],
        out_shape=[jax.ShapeDtypeStruct((T_PAD, D_PACK), jnp.int32), jax.ShapeDtypeStruct((TOP_K, T_PAD), jnp.int32),
                   jax.ShapeDtypeStruct((TOP_K, T_PAD), F32), jax.ShapeDtypeStruct((TOP_K, T_PAD), jnp.int32),
                   jax.ShapeDtypeStruct((N_EXPERTS, 1), jnp.int32)],
        name="router",
        compiler_params=pltpu.CompilerParams(dimension_semantics=("arbitrary",), vmem_limit_bytes=VMEM_LIMIT),
    )(hmid, g_ffn, w_router_t, b_router_c, tri)


def _place_kernel(off_ref, idx_ref, rank_ref, pos_ref):
    idx = idx_ref[...]
    pos = rank_ref[...]
    for e in range(N_EXPERTS):
        pos = pos + jnp.where(idx == e, off_ref[e], 0)
    pos_ref[...] = pos


def _place(offsets, idx, rank):
    return pl.pallas_call(
        _place_kernel,
        in_specs=[pl.BlockSpec(memory_space=pltpu.SMEM), pl.BlockSpec(memory_space=pltpu.VMEM),
                  pl.BlockSpec(memory_space=pltpu.VMEM)],
        out_specs=pl.BlockSpec(memory_space=pltpu.VMEM),
        out_shape=jax.ShapeDtypeStruct((TOP_K, T_PAD), jnp.int32), name="place",
    )(offsets, idx, rank)


def _sc_mesh():
    return plsc.VectorSubcoreMesh(core_axis_name="core", subcore_axis_name="subcore")


def _sc_dispatch(rows, pos):
    n, d = rows.shape
    nblk = n // SC_ROWS
    pos_w = pos.reshape(TOP_K, nblk, SC_ROWS).transpose(1, 0, 2)

    @functools.partial(pl.kernel, out_type=jax.ShapeDtypeStruct((P_ROWS, d), rows.dtype), mesh=_sc_mesh(),
                       scratch_types=[], name="dispatch")
    def run(x_hbm, i_hbm, o_hbm):
        def body(x_vmem, i_vmem):
            for k in range(TOP_K):
                pltpu.sync_copy(x_vmem, o_hbm.at[i_vmem.at[0, k]])

        pltpu.emit_pipeline(
            body, grid=(nblk,),
            in_specs=[pl.BlockSpec((SC_ROWS, d), lambda i: (i, 0)),
                      pl.BlockSpec((1, TOP_K, SC_ROWS), lambda i: (i, 0, 0))],
            out_specs=[], core_axis_name=("core", "subcore"), dimension_semantics=(pltpu.PARALLEL,),
        )(x_hbm, i_hbm)

    return run(rows, pos_w)


def _sc_combine(rows, flat_pos):
    _, d = rows.shape
    n = flat_pos.shape[0]
    nblk = n // SC_ROWS
    pos_w = flat_pos.reshape(nblk, 1, SC_ROWS)

    @functools.partial(pl.kernel, out_type=jax.ShapeDtypeStruct((n, d), rows.dtype), mesh=_sc_mesh(),
                       scratch_types=[], name="combine")
    def run(x_hbm, i_hbm, o_hbm):
        def body(i_vmem, o_vmem):
            pltpu.sync_copy(x_hbm.at[i_vmem.at[0, 0]], o_vmem)

        pltpu.emit_pipeline(
            body, grid=(nblk,),
            in_specs=[pl.BlockSpec((1, 1, SC_ROWS), lambda i: (i, 0, 0))],
            out_specs=[pl.BlockSpec((SC_ROWS, d), lambda i: (i, 0))],
            core_axis_name=("core", "subcore"), dimension_semantics=(pltpu.PARALLEL,),
        )(i_hbm, o_hbm)

    return run(rows, pos_w)


def _experts_kernel(ts_ref, xs_hbm, w1_hbm, b1_ref, w2_hbm, b2_ref, ys_hbm,
                    w1_st, w2_st, w1_s, w2_s, xbuf, ybuf, wsem, xsem, ysem):
    e = pl.program_id(0)
    n_valid = ts_ref[N_EXPERTS]

    def w_copies(ex, slot):
        return (pltpu.make_async_copy(w1_hbm.at[ex], w1_st.at[slot], wsem.at[0, slot]),
                pltpu.make_async_copy(w2_hbm.at[ex], w2_st.at[slot], wsem.at[1, slot]))

    def x_copy(t, slot):
        return pltpu.make_async_copy(xs_hbm.at[pl.ds(t * TM, TM)], xbuf.at[slot], xsem.at[slot])

    def y_copy(t, slot):
        return pltpu.make_async_copy(ybuf.at[slot], ys_hbm.at[pl.ds(t * TM, TM)], ysem.at[slot])

    @pl.when(e == 0)
    def _():
        for c in w_copies(0, 0):
            c.start()
        x_copy(0, 0).start()

    @pl.when(e + 1 < N_EXPERTS)
    def _():
        for c in w_copies(e + 1, (e + 1) % 2):
            c.start()

    wslot = e % 2
    for c in w_copies(e, wslot):
        c.wait()
    for r in range(4):
        rs = slice(256 * r, 256 * (r + 1))
        w1_s[rs, :] = w1_st[wslot, rs, :].astype(BF16)
        w2_s[rs, :] = w2_st[wslot, rs, :].astype(BF16)

    def tile(t, carry):
        slot = t % 2

        @pl.when(t + 1 < n_valid)
        def _():
            x_copy(t + 1, 1 - slot).start()

        x_copy(t, slot).wait()

        @pl.when(t >= 2)
        def _():
            y_copy(t - 2, slot).wait()

        hdn = _bdot(_unpack_pairs(xbuf[slot]), w1_s[...]) + b1_ref[0]
        gt = jnp.minimum(hdn[:, :D_FF], SWIGLU_LIMIT)
        up = jnp.clip(hdn[:, D_FF:], -SWIGLU_LIMIT, SWIGLU_LIMIT)
        act = (up + 1.0) * gt * _sigmoid(SWIGLU_ALPHA * gt)
        ybuf[slot] = _pack_pairs(_bdot(act, w2_s[...]) + b2_ref[0])
        y_copy(t, slot).start()
        return carry

    lax.fori_loop(ts_ref[e], ts_ref[e + 1], tile, 0)

    @pl.when(e == N_EXPERTS - 1)
    def _():
        @pl.when(n_valid >= 2)
        def _():
            y_copy(n_valid - 2, n_valid % 2).wait()

        y_copy(n_valid - 1, (n_valid - 1) % 2).wait()


def _experts(tile_start, xs, w_up, b_up, w_down, b_down):
    wsel = lambda e, ts: (e, 0, 0)
    hbm = pl.BlockSpec(memory_space=pl.ANY)
    grid_spec = pltpu.PrefetchScalarGridSpec(
        num_scalar_prefetch=1, grid=(N_EXPERTS,),
        in_specs=[hbm, hbm, pl.BlockSpec((1, 1, 2 * D_FF), wsel), hbm, pl.BlockSpec((1, 1, D_MODEL), wsel)],
        out_specs=hbm,
        scratch_shapes=[pltpu.VMEM((2, D_MODEL, 2 * D_FF), F32), pltpu.VMEM((2, D_FF, D_MODEL), F32),
                        pltpu.VMEM((D_MODEL, 2 * D_FF), BF16), pltpu.VMEM((D_FF, D_MODEL), BF16),
                        pltpu.VMEM((2, TM, D_PACK), jnp.int32), pltpu.VMEM((2, TM, D_PACK), jnp.int32),
                        pltpu.SemaphoreType.DMA((2, 2)), pltpu.SemaphoreType.DMA((2,)),
                        pltpu.SemaphoreType.DMA((2,))],
    )
    return pl.pallas_call(
        _experts_kernel, grid_spec=grid_spec, out_shape=jax.ShapeDtypeStruct((P_ROWS, D_PACK), jnp.int32),
        name="experts",
        compiler_params=pltpu.CompilerParams(dimension_semantics=("arbitrary",), vmem_limit_bytes=VMEM_LIMIT),
    )(tile_start, xs, w_up, b_up.reshape(N_EXPERTS, 1, 2 * D_FF), w_down, b_down.reshape(N_EXPERTS, 1, D_MODEL))


def _final_kernel(h_ref, yg_ref, gw_ref, p_ref, gple_ref, wg_ref, wp_ref, gfin_ref, out_ref):
    rows = h_ref.shape[0]
    gw = gw_ref[...]
    h = h_ref[...]
    for k in range(TOP_K):
        h = h + gw[:, k:k + 1] * _unpack_pairs(yg_ref[k])
    gate = _sigmoid(_bdot(_rms(h, gple_ref[...]), wg_ref[...]))
    h = h + gate * _bdot(p_ref[...].reshape(rows, PLE_DIM), wp_ref[...])
    out_ref[...] = _rms(h, gfin_ref[...]).reshape(out_ref.shape)


def _final_prompt(hmid, yg, gwt, p_prompt, gple, wg, wp, gfin):
    nb = 4
    rows = nb * CHUNK
    rb = lambda c, b: (c * (BATCH // nb) + b, 0)
    return pl.pallas_call(
        _final_kernel, grid=(N_CHUNKS, BATCH // nb),
        in_specs=[pl.BlockSpec((rows, D_MODEL), rb),
                  pl.BlockSpec((TOP_K, rows, D_PACK), lambda c, b: (0, c * (BATCH // nb) + b, 0)),
                  pl.BlockSpec((rows, TOP_K), rb),
                  pl.BlockSpec((nb, CHUNK, PLE_DIM), lambda c, b: (b, c, 0)),
                  _full((1, D_MODEL)), _full((D_MODEL, D_MODEL)), _full((PLE_DIM, D_MODEL)), _full((1, D_MODEL))],
        out_specs=pl.BlockSpec((nb, CHUNK, D_MODEL), lambda c, b: (b, c, 0)),
        out_shape=jax.ShapeDtypeStruct((BATCH, SEQ, D_MODEL), F32), name="final_prompt",
        compiler_params=pltpu.CompilerParams(dimension_semantics=("arbitrary", "arbitrary"),
                                             vmem_limit_bytes=VMEM_LIMIT),
    )(hmid, yg, gwt, p_prompt, gple, wg, wp, gfin)


def _final_sample(hmid, yg, gwt, p_sample, gple, wg, wp, gfin):
    blk = SAMPLE_ROW0 // DEC_BATCH
    return pl.pallas_call(
        _final_kernel, grid=(1,),
        in_specs=[pl.BlockSpec((DEC_BATCH, D_MODEL), lambda i: (blk, 0)),
                  pl.BlockSpec((TOP_K, DEC_BATCH, D_PACK), lambda i: (0, blk, 0)),
                  pl.BlockSpec((DEC_BATCH, TOP_K), lambda i: (blk, 0)),
                  _full((DEC_BATCH, PLE_DIM)),
                  _full((1, D_MODEL)), _full((D_MODEL, D_MODEL)), _full((PLE_DIM, D_MODEL)), _full((1, D_MODEL))],
        out_specs=_full((DEC_BATCH, D_MODEL)),
        out_shape=jax.ShapeDtypeStruct((DEC_BATCH, D_MODEL), F32), name="final_sample",
        compiler_params=pltpu.CompilerParams(dimension_semantics=("arbitrary",), vmem_limit_bytes=VMEM_LIMIT),
    )(hmid, yg, gwt, p_sample, gple, wg, wp, gfin)


def _alibi_tables():
    slopes = 2.0 ** (-8.0 * (np.arange(N_HEADS, dtype=np.float64) + 1.0) / N_HEADS)
    i = np.arange(CHUNK)[:, None]
    j = np.arange(2 * CHUNK)[None, :]
    dist = i + CHUNK - j
    valid = (dist >= 0) & (dist <= WINDOW)
    tabs = []
    for has_prev in (False, True):
        ok = valid & ((j >= CHUNK) | has_prev)
        tabs.append(np.where(ok[None], -slopes[:, None, None] * dist[None], NEG))
    prompt = np.stack(tabs).astype(np.float32)
    wb = min(WINDOW, PAST_LEN)
    sample = (-slopes[:, None] * (wb - np.arange(wb))[None, :]).astype(np.float32)
    return prompt, sample


def kernel(x_prompt, x_sample, cache_k_win, cache_v_win, state_ssm_re, state_ssm_im, p_prompt, p_sample, norm_mix, w_in, sinks, ssm_lam_re, ssm_lam_im, ssm_log_step, ssm_b_re, ssm_b_im, ssm_c_re, ssm_c_im, ssm_d, w_glu, b_glu, norm_attn_out, norm_ssm_out, w_out, norm_ffn, w_router, b_router, w_up, b_up, w_down, b_down, norm_ple, w_ple_gate, w_ple_proj, norm_final):
    nst = SSM_G * SSM_P
    bias_np, sbias_np = _alibi_tables()
    bias = jnp.asarray(bias_np)
    sbias = jnp.asarray(sbias_np)

    lbr, lbi, bbr, bbi = _prep(ssm_lam_re[0], ssm_lam_im[0], ssm_log_step[0], ssm_b_re[0], ssm_b_im[0])
    bblk, lamr, lami, cblk = _s5_blocks(lbr, lbi, bbr, bbi, ssm_c_re[0], ssm_c_im[0])

    gmix = norm_mix[0].reshape(1, D_MODEL)
    win = w_in[0].astype(BF16)
    dskip = ssm_d[0].reshape(1, D_SSM)
    wglu = w_glu[0].astype(BF16)
    bglu = b_glu[0].reshape(1, D_SSM)
    ga = norm_attn_out[0].reshape(1, D_ATTN)
    gs = norm_ssm_out[0].reshape(1, D_SSM)
    wout = w_out[0].astype(BF16)
    sink = sinks[0]

    hbuf, k_last, v_last, re_p, im_p = _mixer_prompt(
        x_prompt, sink, gmix, win, bias, bblk, lamr, lami, cblk, dskip, wglu, bglu, ga, gs, wout)

    ck = cache_k_win[0].reshape(DEC_BATCH, WINDOW, D_KV)
    cv = cache_v_win[0].reshape(DEC_BATCH, WINDOW, D_KV)
    hmid, k_new, v_new, re_s, im_s = _mixer_sample(
        x_sample.reshape(DEC_BATCH, D_MODEL), gmix, win, ck, cv, sbias, sink.reshape(N_HEADS, 1), bblk, lamr, lami,
        cblk, dskip, wglu, bglu, ga, gs, wout, state_ssm_re[0].reshape(DEC_BATCH, nst),
        state_ssm_im[0].reshape(DEC_BATCH, nst), hbuf)

    hn, idx, gw, rank, counts = _router(hmid, norm_ffn[0].reshape(1, D_MODEL), w_router[0].T,
                                        b_router[0].reshape(N_EXPERTS, 1))

    tiles_per = (counts[:, 0] + (TM - 1)) // TM
    tile_end = jnp.cumsum(tiles_per)
    offsets = ((tile_end - tiles_per) * TM).astype(jnp.int32)
    tile_start = jnp.concatenate([jnp.zeros((1,), jnp.int32), tile_end.astype(jnp.int32)])

    pos = _place(offsets, idx, rank)
    xs = _sc_dispatch(hn, pos)
    ys = _experts(tile_start, xs, w_up[0], b_up[0], w_down[0], b_down[0])
    yg = _sc_combine(ys, pos.reshape(TOP_K * T_PAD)).reshape(TOP_K, T_PAD, D_PACK)

    gwt = gw.T
    gple = norm_ple[0].reshape(1, D_MODEL)
    wg = w_ple_gate[0].astype(BF16)
    wp = w_ple_proj[0].astype(BF16)
    gfin = norm_final.reshape(1, D_MODEL)
    y_prompt = _final_prompt(hmid, yg, gwt, p_prompt[0], gple, wg, wp, gfin)
    y_sample = _final_sample(hmid, yg, gwt, p_sample[0].reshape(DEC_BATCH, PLE_DIM), gple, wg, wp, gfin)

    k_win_s = jnp.concatenate([ck[:, 1:], k_new[:, None, :]], axis=1)
    v_win_s = jnp.concatenate([cv[:, 1:], v_new[:, None, :]], axis=1)
    kv5 = (1, BATCH, CHUNK, N_KV, HEAD_DIM)
    skv5 = (1, DEC_BATCH, WINDOW, N_KV, HEAD_DIM)
    return (y_prompt, y_sample.reshape(DEC_BATCH, 1, D_MODEL),
            k_last.reshape(kv5), v_last.reshape(kv5),
            re_p.reshape(1, BATCH, SSM_G, SSM_P), im_p.reshape(1, BATCH, SSM_G, SSM_P),
            k_win_s.reshape(skv5), v_win_s.reshape(skv5),
            re_s.reshape(1, DEC_BATCH, SSM_G, SSM_P), im_s.reshape(1, DEC_BATCH, SSM_G, SSM_P))
```

```python
import functools

import numpy as np
import jax
import jax.numpy as jnp
from jax import lax
from jax.experimental import pallas as pl
from jax.experimental.pallas import tpu as pltpu
from jax.experimental.pallas import tpu_sc as plsc

F32 = jnp.float32
BF16 = jnp.bfloat16

D_MODEL = 1024
BATCH = 8
SEQ = 2048
DEC_BATCH = 128
PAST_LEN = 16384
HEAD_DIM = 64
D_ATTN = 512
N_HEADS = 8
N_KV = 2
D_KV = N_KV * HEAD_DIM
WINDOW = 128
D_SSM = 512
SSM_H = 16
SSM_G = 32
SSM_P = 64
D_IN = D_ATTN + 2 * D_KV + D_SSM
N_EXPERTS = 32
TOP_K = 4
D_FF = 1024
SWIGLU_LIMIT = 7.0
SWIGLU_ALPHA = 1.702
PLE_DIM = 256
EPS = 1e-5
NEG = -1e30

CHUNK = 128
N_CHUNKS = SEQ // CHUNK
ROWS = BATCH * CHUNK
PITCH = CHUNK + 8
NAT_HEADS = (0, 2, 5, 7)
ROT_HEADS = (1, 3, 4, 6)
SUB_T = 32
SUB_ROWS = SUB_T * BATCH
N_SBLK = 4
SBLK = 512
T_REAL = BATCH * SEQ + DEC_BATCH
T_PAD = T_REAL + 128
SAMPLE_ROW0 = BATCH * SEQ
ROUTER_TILE = 640
TM = 512
N_TILES = (T_PAD * TOP_K + N_EXPERTS * (TM - 1)) // TM
P_ROWS = N_TILES * TM
SC_ROWS = 40
D_PACK = D_MODEL // 2
VMEM_LIMIT = 56 * 1024 * 1024


def _rms(x, g):
    return x * lax.rsqrt(jnp.mean(x * x, axis=-1, keepdims=True) + EPS) * g


def _sigmoid(x):
    return 1.0 / (1.0 + jnp.exp(-x))


def _gelu_tanh(x):
    c = np.float32(np.sqrt(2.0 / np.pi))
    return 0.5 * x * (1.0 + jnp.tanh(c * (x + 0.044715 * (x * x * x))))


def _bdot(a, b):
    return jnp.dot(a.astype(BF16), b, preferred_element_type=F32)


def _pack_pairs(x):
    n = x.shape[1] // 2
    lo = lax.bitcast_convert_type(x[:, :n].astype(BF16).astype(F32), jnp.int32)
    hi = lax.bitcast_convert_type(x[:, n:].astype(BF16).astype(F32), jnp.int32)
    return lax.shift_right_logical(lo, 16) | hi


def _unpack_pairs(w):
    lo = lax.bitcast_convert_type(lax.shift_left(w, 16), F32)
    hi = lax.bitcast_convert_type(w & jnp.int32(-65536), F32)
    return jnp.concatenate([lo, hi], axis=1)


def _full(shape):
    n = len(shape)
    return pl.BlockSpec(shape, lambda *_: (0,) * n)


def _prep_kernel(lr_ref, li_ref, ls_ref, br_ref, bi_ref, lbr_ref, lbi_ref, bbr_ref, bbi_ref):
    lr = lr_ref[...]
    li = li_ref[...]
    step = jnp.exp(ls_ref[...])
    zr = lr * step
    zi = li * step
    mag = jnp.exp(zr)
    lbr = mag * jnp.cos(zi)
    lbi = mag * jnp.sin(zi)
    lbr_ref[...] = lbr
    lbi_ref[...] = lbi
    nr = lbr - 1.0
    den = lr * lr + li * li
    cr = (nr * lr + lbi * li) / den
    ci = (lbi * lr - nr * li) / den
    br = br_ref[...]
    bi = bi_ref[...]
    bbr_ref[...] = cr * br - ci * bi
    bbi_ref[...] = cr * bi + ci * br


def _prep(lam_re, lam_im, log_step, b_re, b_im):
    g, p, h = SSM_G, SSM_P, SSM_H
    out = pl.pallas_call(
        _prep_kernel,
        out_shape=[jax.ShapeDtypeStruct((g, 1, p), F32), jax.ShapeDtypeStruct((g, 1, p), F32),
                   jax.ShapeDtypeStruct((g, h, p), F32), jax.ShapeDtypeStruct((g, h, p), F32)],
        name="s5_prep",
    )(lam_re.reshape(g, 1, p), lam_im.reshape(g, 1, p), log_step.reshape(g, 1, 1),
      jnp.transpose(b_re, (0, 2, 1)), jnp.transpose(b_im, (0, 2, 1)))
    return out


def _s5_blocks(lbr, lbi, bbr, bbi, c_re, c_im):
    eye = jnp.eye(8, dtype=F32)
    shp = (N_SBLK, 8, SSM_H, SSM_P)
    b_r = jnp.einsum('jghp,gk->jghkp', bbr.reshape(shp), eye).reshape(N_SBLK, 128, SBLK)
    b_i = jnp.einsum('jghp,gk->jghkp', bbi.reshape(shp), eye).reshape(N_SBLK, 128, SBLK)
    bblk = jnp.concatenate([b_r, b_i], axis=-1).astype(BF16)
    c_r = jnp.einsum('jghp,gk->jgpkh', c_re.reshape(shp), eye).reshape(N_SBLK, SBLK, 128)
    c_i = jnp.einsum('jghp,gk->jgpkh', c_im.reshape(shp), eye).reshape(N_SBLK, SBLK, 128)
    cblk = jnp.concatenate([c_r, -c_i], axis=1).astype(BF16)
    return bblk, lbr.reshape(N_SBLK, 1, SBLK), lbi.reshape(N_SBLK, 1, SBLK), cblk


def _ssm_post(y_lin, u, dskip, wglu, bglu, gs):
    y = _gelu_tanh(y_lin + dskip * u)
    y = y * _sigmoid(_bdot(y, wglu) + bglu)
    return _rms(y, gs)


def _mixer_prompt_kernel(sinks_ref, x_ref, gmix_ref, win_ref, bias_ref, bblk_ref, lamr_ref, lami_ref,
                         cblk_ref, dskip_ref, wglu_ref, bglu_ref, ga_ref, gs_ref, wout_ref,
                         hmid_ref, klast_ref, vlast_ref, sre_ref, sim_ref,
                         proj_s, u_s, kprev_s, kprevr_s, vprev_s, vprevr_s, attn_s, ssm_s, utb_s, bu_s, xs_s, ytb_s):
    c = pl.program_id(0)

    @pl.when(c == 0)
    def _():
        zkv = jnp.zeros(kprev_s.shape, BF16)
        kprev_s[...] = zkv
        kprevr_s[...] = zkv
        vprev_s[...] = zkv
        vprevr_s[...] = zkv
        sre_ref[...] = jnp.zeros(sre_ref.shape, F32)
        sim_ref[...] = jnp.zeros(sim_ref.shape, F32)

    x = x_ref[...].reshape(ROWS, D_MODEL)
    proj = _bdot(_rms(x, gmix_ref[...]), win_ref[...])
    u0 = D_ATTN + 2 * D_KV
    proj_s[...] = proj[:, 0:u0]
    for jb in range(D_SSM // 128):
        for b in range(BATCH):
            u_s[jb, b * PITCH:b * PITCH + CHUNK, :] = proj[b * CHUNK:(b + 1) * CHUNK, u0 + 128 * jb:u0 + 128 * (jb + 1)]

    lo = lax.broadcasted_iota(jnp.int32, (CHUNK, 128), 1) < HEAD_DIM
    hi = jnp.logical_not(lo)
    table = jnp.minimum(c, 1)
    hrow = lax.broadcasted_iota(jnp.int32, (4 * CHUNK, 1), 0) // CHUNK

    def sink_col(heads):
        col = jnp.full((4 * CHUNK, 1), sinks_ref[heads[3]], F32)
        for n in (2, 1, 0):
            col = jnp.where(hrow == n, sinks_ref[heads[n]], col)
        return col

    sink_nat = sink_col(NAT_HEADS)
    sink_rot = sink_col(ROT_HEADS)

    def attend(q, k, v, bias, sink):
        s = lax.dot_general(q, k, (((1,), (1,)), ((), ())), preferred_element_type=F32) + bias
        m = jnp.maximum(jnp.max(s, axis=-1, keepdims=True), sink)
        p = jnp.exp(s - m)
        den = jnp.sum(p, axis=-1, keepdims=True) + jnp.exp(sink - m)
        return jnp.dot(p.astype(BF16), v, preferred_element_type=F32) / den

    def attn_body(b, carry):
        r0 = pl.multiple_of(b * CHUNK, CHUNK)
        rows = pl.ds(r0, CHUNK)
        kb = proj_s[rows, D_ATTN:D_ATTN + D_KV]
        vb = proj_s[rows, D_ATTN + D_KV:D_ATTN + 2 * D_KV]
        kb16 = kb.astype(BF16)
        vb16 = vb.astype(BF16)
        kbr16 = pltpu.roll(kb, HEAD_DIM, 1).astype(BF16)
        vbr16 = pltpu.roll(vb, HEAD_DIM, 1).astype(BF16)
        k_nat = jnp.concatenate([kprev_s[b], kb16], axis=0)
        k_rot = jnp.concatenate([kprevr_s[b], kbr16], axis=0)
        v_nat = jnp.concatenate([vprev_s[b], vb16], axis=0)
        v_rot = jnp.concatenate([vprevr_s[b], vbr16], axis=0)
        q2 = [proj_s[rows, 128 * jq:128 * (jq + 1)] * (HEAD_DIM ** -0.5) for jq in range(N_HEADS // 2)]
        q_nat = jnp.concatenate([jnp.where(lo if h % 2 == 0 else hi, q2[h // 2], 0.0) for h in NAT_HEADS],
                                axis=0).astype(BF16)
        q_rot = jnp.concatenate([jnp.where(lo if h % 2 == 0 else hi, q2[h // 2], 0.0) for h in ROT_HEADS],
                                axis=0).astype(BF16)
        o_nat = attend(q_nat, k_nat, v_nat, bias_ref[table, 0], sink_nat)
        o_rot = attend(q_rot, k_rot, v_rot, bias_ref[table, 1], sink_rot)
        for jq in range(N_HEADS // 2):
            blk = slice(CHUNK * jq, CHUNK * (jq + 1))
            even, odd = (o_nat, o_rot) if jq < 2 else (o_rot, o_nat)
            attn_s[rows, 128 * jq:128 * (jq + 1)] = jnp.where(lo, even[blk], odd[blk])
        kprev_s[b] = kb16
        kprevr_s[b] = kbr16
        vprev_s[b] = vb16
        vprevr_s[b] = vbr16
        return carry

    lax.fori_loop(0, BATCH, attn_body, 0)

    for sc in range(CHUNK // SUB_T):
        t0 = sc * SUB_T
        for i in range(SUB_T):
            for jb in range(D_SSM // 128):
                utb_s[i * BATCH:(i + 1) * BATCH, 128 * jb:128 * (jb + 1)] = (
                    u_s[jb, pl.ds(t0 + i, BATCH, stride=PITCH), :])
        u_tb = utb_s[...]
        for j in range(N_SBLK):
            bu_s[...] = _bdot(u_tb[:, 128 * j:128 * (j + 1)], bblk_ref[j])
            lr = jnp.broadcast_to(lamr_ref[j], (BATCH, SBLK))
            li = jnp.broadcast_to(lami_ref[j], (BATCH, SBLK))
            cols = slice(SBLK * j, SBLK * (j + 1))

            def step(i, carry):
                sr, si = carry
                r = pl.ds(pl.multiple_of(i * BATCH, BATCH), BATCH)
                nr = lr * sr - li * si + bu_s[r, 0:SBLK]
                ni = lr * si + li * sr + bu_s[r, SBLK:2 * SBLK]
                xs_s[r, 0:SBLK] = nr
                xs_s[r, SBLK:2 * SBLK] = ni
                return nr, ni

            sr, si = lax.fori_loop(0, SUB_T, step, (sre_ref[:, cols], sim_ref[:, cols]), unroll=True)
            sre_ref[:, cols] = sr
            sim_ref[:, cols] = si
            ytb_s[:, 128 * j:128 * (j + 1)] = _bdot(xs_s[...], cblk_ref[j])
        yn = _ssm_post(ytb_s[...], u_tb, dskip_ref[...], wglu_ref[...], bglu_ref[...], gs_ref[...])
        for i in range(SUB_T):
            for jb in range(D_SSM // 128):
                ssm_s[jb, pl.ds(t0 + i, BATCH, stride=PITCH), :] = (
                    yn[i * BATCH:(i + 1) * BATCH, 128 * jb:128 * (jb + 1)])

    an = _rms(attn_s[...], ga_ref[...])
    sn = jnp.concatenate(
        [jnp.concatenate([ssm_s[jb, b * PITCH:b * PITCH + CHUNK, :] for b in range(BATCH)], axis=0)
         for jb in range(D_SSM // 128)], axis=1)
    hmid_ref[...] = x + _bdot(an, wout_ref[0:D_ATTN, :]) + _bdot(sn, wout_ref[D_ATTN:2 * D_ATTN, :])

    @pl.when(c == N_CHUNKS - 1)
    def _():
        klast_ref[...] = proj_s[:, D_ATTN:D_ATTN + D_KV].reshape(BATCH, CHUNK, D_KV)
        vlast_ref[...] = proj_s[:, D_ATTN + D_KV:D_ATTN + 2 * D_KV].reshape(BATCH, CHUNK, D_KV)


def _mixer_prompt(x_prompt, sinks, gmix, win, bias, bblk, lamr, lami, cblk, dskip, wglu, bglu, ga, gs, wout):
    smem = pl.BlockSpec(memory_space=pltpu.SMEM)
    in_specs = [
        smem,
        pl.BlockSpec((BATCH, CHUNK, D_MODEL), lambda c: (0, c, 0)),
        _full((1, D_MODEL)), _full((D_MODEL, D_IN)), _full((2, 2, 4 * CHUNK, 2 * CHUNK)),
        _full((N_SBLK, 128, 2 * SBLK)), _full((N_SBLK, 1, SBLK)), _full((N_SBLK, 1, SBLK)),
        _full((N_SBLK, 2 * SBLK, 128)), _full((1, D_SSM)), _full((D_SSM, D_SSM)), _full((1, D_SSM)),
        _full((1, D_ATTN)), _full((1, D_SSM)), _full((D_MODEL, D_MODEL)),
    ]
    out_shape = [
        jax.ShapeDtypeStruct((T_PAD, D_MODEL), F32),
        jax.ShapeDtypeStruct((BATCH, CHUNK, D_KV), F32),
        jax.ShapeDtypeStruct((BATCH, CHUNK, D_KV), F32),
        jax.ShapeDtypeStruct((BATCH, SSM_G * SSM_P), F32),
        jax.ShapeDtypeStruct((BATCH, SSM_G * SSM_P), F32),
    ]
    out_specs = [
        pl.BlockSpec((ROWS, D_MODEL), lambda c: (c, 0)),
        _full((BATCH, CHUNK, D_KV)), _full((BATCH, CHUNK, D_KV)),
        _full((BATCH, SSM_G * SSM_P)), _full((BATCH, SSM_G * SSM_P)),
    ]
    kv_scr = pltpu.VMEM((BATCH, CHUNK, D_KV), BF16)
    scratch = [
        pltpu.VMEM((ROWS, D_ATTN + 2 * D_KV), F32), pltpu.VMEM((D_SSM // 128, BATCH * PITCH, 128), F32),
        kv_scr, kv_scr, kv_scr, kv_scr,
        pltpu.VMEM((ROWS, D_ATTN), F32), pltpu.VMEM((D_SSM // 128, BATCH * PITCH, 128), F32),
        pltpu.VMEM((SUB_ROWS, D_SSM), F32), pltpu.VMEM((SUB_ROWS, 2 * SBLK), F32),
        pltpu.VMEM((SUB_ROWS, 2 * SBLK), F32), pltpu.VMEM((SUB_ROWS, D_SSM), F32),
    ]
    return pl.pallas_call(
        _mixer_prompt_kernel, grid=(N_CHUNKS,), in_specs=in_specs, out_specs=out_specs, out_shape=out_shape,
        scratch_shapes=scratch, name="mixer_prompt",
        compiler_params=pltpu.CompilerParams(dimension_semantics=("arbitrary",), vmem_limit_bytes=VMEM_LIMIT),
    )(sinks, x_prompt, gmix, win, bias, bblk, lamr, lami, cblk, dskip, wglu, bglu, ga, gs, wout)


SGRP = 16
SPITCH = DEC_BATCH + 8
N_SGRP = DEC_BATCH // SGRP


def _mixer_sample_kernel(x_ref, gmix_ref, win_ref, ck_ref, cv_ref, sbias_ref, sinkc_ref, bblk_ref, lamr_ref,
                         lami_ref, cblk_ref, dskip_ref, wglu_ref, bglu_ref, ga_ref, gs_ref, wout_ref,
                         x0r_ref, x0i_ref, hbuf_ref,
                         hmid_ref, knew_ref, vnew_ref, sre_ref, sim_ref,
                         proj_s, qall_s, oall_s, kn8_s, vn8_s):
    del hbuf_ref
    g = pl.program_id(0)
    lo = lax.broadcasted_iota(jnp.int32, (DEC_BATCH, 128), 1) < HEAD_DIM

    @pl.when(g == 0)
    def _():
        proj = _bdot(_rms(x_ref[...], gmix_ref[...]), win_ref[...])
        proj_s[...] = proj
        for h in range(N_HEADS):
            jq, half, kv = h // 2, h % 2, h // 4
            q2 = proj[:, 128 * jq:128 * (jq + 1)] * (HEAD_DIM ** -0.5)
            if half != kv:
                q2 = pltpu.roll(q2, HEAD_DIM, 1)
            hr = slice(h * SPITCH, h * SPITCH + DEC_BATCH)
            qall_s[hr, :] = jnp.where(lo if kv == 0 else jnp.logical_not(lo), q2, 0.0)
            kn8_s[hr, :] = proj[:, D_ATTN:D_ATTN + D_KV]
            vn8_s[hr, :] = proj[:, D_ATTN + D_KV:D_ATTN + 2 * D_KV]

    def head_rows(ref):
        return jnp.stack([ref[pl.ds(g * SGRP + ii, N_HEADS, stride=SPITCH), :] for ii in range(SGRP)], axis=0)

    qe = head_rows(qall_s)
    knew = head_rows(kn8_s)
    vnew = head_rows(vn8_s)
    s = jnp.einsum('bhd,bkd->bhk', qe.astype(BF16), ck_ref[...].astype(BF16),
                   preferred_element_type=F32) + sbias_ref[...][None]
    s_new = jnp.sum(qe * knew, axis=-1, keepdims=True)
    sink = sinkc_ref[...][None]
    m = jnp.maximum(jnp.maximum(jnp.max(s, axis=-1, keepdims=True), s_new), sink)
    p = jnp.exp(s - m)
    p_new = jnp.exp(s_new - m)
    den = jnp.sum(p, axis=-1, keepdims=True) + p_new + jnp.exp(sink - m)
    o = (jnp.einsum('bhk,bkd->bhd', p.astype(BF16), cv_ref[...].astype(BF16),
                    preferred_element_type=F32) + p_new * vnew) / den
    for ii in range(SGRP):
        oall_s[pl.ds(g * SGRP + ii, N_HEADS, stride=SPITCH), :] = o[ii]

    @pl.when(g == N_SGRP - 1)
    def _():
        proj = proj_s[...]
        blocks = []
        for jq in range(N_HEADS // 2):
            oa = oall_s[(2 * jq) * SPITCH:(2 * jq) * SPITCH + DEC_BATCH, :]
            ob = oall_s[(2 * jq + 1) * SPITCH:(2 * jq + 1) * SPITCH + DEC_BATCH, :]
            if jq // 2 == 0:
                blocks.append(jnp.where(lo, oa, pltpu.roll(ob, HEAD_DIM, 1)))
            else:
                blocks.append(jnp.where(lo, pltpu.roll(oa, HEAD_DIM, 1), ob))
        attn = jnp.concatenate(blocks, axis=1)
        u = proj[:, D_ATTN + 2 * D_KV:]
        ys = []
        for j in range(N_SBLK):
            bu = _bdot(u[:, 128 * j:128 * (j + 1)], bblk_ref[j])
            lr = lamr_ref[j]
            li = lami_ref[j]
            cols = slice(SBLK * j, SBLK * (j + 1))
            sr = x0r_ref[:, cols]
            si = x0i_ref[:, cols]
            nr = lr * sr - li * si + bu[:, 0:SBLK]
            ni = lr * si + li * sr + bu[:, SBLK:2 * SBLK]
            sre_ref[:, cols] = nr
            sim_ref[:, cols] = ni
            ys.append(_bdot(jnp.concatenate([nr, ni], axis=1), cblk_ref[j]))
        sn = _ssm_post(jnp.concatenate(ys, axis=1), u, dskip_ref[...], wglu_ref[...], bglu_ref[...], gs_ref[...])
        an = _rms(attn, ga_ref[...])
        h = x_ref[...] + _bdot(an, wout_ref[0:D_ATTN, :]) + _bdot(sn, wout_ref[D_ATTN:2 * D_ATTN, :])
        hmid_ref[0:DEC_BATCH, :] = h
        hmid_ref[DEC_BATCH:2 * DEC_BATCH, :] = jnp.zeros((DEC_BATCH, D_MODEL), F32)
        knew_ref[...] = proj[:, D_ATTN:D_ATTN + D_KV]
        vnew_ref[...] = proj[:, D_ATTN + D_KV:D_ATTN + 2 * D_KV]


def _mixer_sample(x_s, gmix, win, ck, cv, sbias, sinkc, bblk, lamr, lami, cblk, dskip, wglu, bglu, ga, gs, wout,
                  x0r, x0i, hbuf):
    nst = SSM_G * SSM_P
    in_specs = [
        _full((DEC_BATCH, D_MODEL)), _full((1, D_MODEL)), _full((D_MODEL, D_IN)),
        pl.BlockSpec((SGRP, WINDOW, D_KV), lambda g: (g, 0, 0)),
        pl.BlockSpec((SGRP, WINDOW, D_KV), lambda g: (g, 0, 0)),
        _full((N_HEADS, WINDOW)), _full((N_HEADS, 1)),
        _full((N_SBLK, 128, 2 * SBLK)), _full((N_SBLK, 1, SBLK)), _full((N_SBLK, 1, SBLK)),
        _full((N_SBLK, 2 * SBLK, 128)), _full((1, D_SSM)), _full((D_SSM, D_SSM)), _full((1, D_SSM)),
        _full((1, D_ATTN)), _full((1, D_SSM)), _full((D_MODEL, D_MODEL)),
        _full((DEC_BATCH, nst)), _full((DEC_BATCH, nst)),
        pl.BlockSpec(memory_space=pl.ANY),
    ]
    out_shape = [
        jax.ShapeDtypeStruct((T_PAD, D_MODEL), F32),
        jax.ShapeDtypeStruct((DEC_BATCH, D_KV), F32), jax.ShapeDtypeStruct((DEC_BATCH, D_KV), F32),
        jax.ShapeDtypeStruct((DEC_BATCH, nst), F32), jax.ShapeDtypeStruct((DEC_BATCH, nst), F32),
    ]
    out_specs = [
        pl.BlockSpec((2 * DEC_BATCH, D_MODEL), lambda g: (SAMPLE_ROW0 // (2 * DEC_BATCH), 0)),
        _full((DEC_BATCH, D_KV)), _full((DEC_BATCH, D_KV)),
        _full((DEC_BATCH, nst)), _full((DEC_BATCH, nst)),
    ]
    head_rows = pltpu.VMEM((N_HEADS * SPITCH, 128), F32)
    scratch = [pltpu.VMEM((DEC_BATCH, D_IN), F32), head_rows, head_rows, head_rows, head_rows]
    return pl.pallas_call(
        _mixer_sample_kernel, grid=(N_SGRP,), in_specs=in_specs, out_specs=out_specs, out_shape=out_shape,
        scratch_shapes=scratch, input_output_aliases={19: 0}, name="mixer_sample",
        compiler_params=pltpu.CompilerParams(dimension_semantics=("arbitrary",), vmem_limit_bytes=VMEM_LIMIT),
    )(x_s, gmix, win, ck, cv, sbias, sinkc, bblk, lamr, lami, cblk, dskip, wglu, bglu, ga, gs, wout, x0r, x0i, hbuf)


def _router_kernel(h_ref, g_ref, wrt_ref, br_ref, tri_ref, hn_ref, idx_ref, gw_ref, rank_ref, cnt_ref):
    i = pl.program_id(0)

    @pl.when(i == 0)
    def _():
        cnt_ref[...] = jnp.zeros(cnt_ref.shape, jnp.int32)

    hn = _rms(h_ref[...], g_ref[...])
    hn_ref[...] = _pack_pairs(hn)
    logits = lax.dot_general(wrt_ref[...], hn, (((1,), (1,)), ((), ())), precision=lax.Precision.HIGHEST,
                             preferred_element_type=F32) + br_ref[...]
    eidx = lax.broadcasted_iota(jnp.int32, logits.shape, 0)
    vals, onehots = [], []
    l = logits
    for k in range(TOP_K):
        m = jnp.max(l, axis=0, keepdims=True)
        ik = jnp.min(jnp.where(l == m, eidx, N_EXPERTS), axis=0, keepdims=True)
        oh = eidx == ik
        idx_ref[k:k + 1, :] = ik
        vals.append(m)
        onehots.append(oh)
        l = jnp.where(oh, -jnp.inf, l)
    exps = [jnp.exp(v - vals[0]) for v in vals]
    den = exps[0] + exps[1] + exps[2] + exps[3]
    for k in range(TOP_K):
        gw_ref[k:k + 1, :] = exps[k] / den
    member = jnp.zeros(logits.shape, F32)
    for oh in onehots:
        member = member + jnp.where(oh, 1.0, 0.0)
    before = jnp.dot(member.astype(BF16), tri_ref[...], preferred_element_type=F32) + cnt_ref[...].astype(F32)
    for k in range(TOP_K):
        rank_ref[k:k + 1, :] = jnp.sum(jnp.where(onehots[k], before, 0.0), axis=0, keepdims=True).astype(jnp.int32)
    cnt_ref[...] += jnp.sum(member, axis=1, keepdims=True).astype(jnp.int32)


def _router(hmid, g_ffn, w_router_t, b_router_c):
    rt = ROUTER_TILE
    tri = jnp.asarray(np.triu(np.ones((rt, rt), np.float32), 1), BF16)
    tok = lambda i: (0, i)
    return pl.pallas_call(
        _router_kernel, grid=(T_PAD // rt,),
        in_specs=[pl.BlockSpec((rt, D_MODEL), lambda i: (i, 0)), _full((1, D_MODEL)),
                  _full((N_EXPERTS, D_MODEL)), _full((N_EXPERTS, 1)), _full((rt, rt))],
        out_specs=[pl.BlockSpec((rt, D_PACK), lambda i: (i, 0)), pl.BlockSpec((TOP_K, rt), tok),
                   pl.BlockSpec((TOP_K, rt), tok), pl.BlockSpec((TOP_K, rt), tok), _full((N_EXPERTS, 1))],
        out_shape=[jax.ShapeDtypeStruct((T_PAD, D_PACK), jnp.int32), jax.ShapeDtypeStruct((TOP_K, T_PAD), jnp.int32),
                   jax.ShapeDtypeStruct((TOP_K, T_PAD), F32), jax.ShapeDtypeStruct((TOP_K, T_PAD), jnp.int32),
                   jax.ShapeDtypeStruct((N_EXPERTS, 1), jnp.int32)],
        name="router",
        compiler_params=pltpu.CompilerParams(dimension_semantics=("arbitrary",), vmem_limit_bytes=VMEM_LIMIT),
    )(hmid, g_ffn, w_router_t, b_router_c, tri)


def _place_kernel(off_ref, idx_ref, rank_ref, pos_ref):
    idx = idx_ref[...]
    pos = rank_ref[...]
    for e in range(N_EXPERTS):
        pos = pos + jnp.where(idx == e, off_ref[e], 0)
    pos_ref[...] = pos


def _place(offsets, idx, rank):
    return pl.pallas_call(
        _place_kernel,
        in_specs=[pl.BlockSpec(memory_space=pltpu.SMEM), pl.BlockSpec(memory_space=pltpu.VMEM),
                  pl.BlockSpec(memory_space=pltpu.VMEM)],
        out_specs=pl.BlockSpec(memory_space=pltpu.VMEM),
        out_shape=jax.ShapeDtypeStruct((TOP_K, T_PAD), jnp.int32), name="place",
    )(offsets, idx, rank)


def _sc_mesh():
    return plsc.VectorSubcoreMesh(core_axis_name="core", subcore_axis_name="subcore")


def _sc_dispatch(rows, pos):
    n, d = rows.shape
    nblk = n // SC_ROWS
    pos_w = pos.reshape(TOP_K, nblk, SC_ROWS).transpose(1, 0, 2)

    @functools.partial(pl.kernel, out_type=jax.ShapeDtypeStruct((P_ROWS, d), rows.dtype), mesh=_sc_mesh(),
                       scratch_types=[], name="dispatch")
    def run(x_hbm, i_hbm, o_hbm):
        def body(x_vmem, i_vmem):
            for k in range(TOP_K):
                pltpu.sync_copy(x_vmem, o_hbm.at[i_vmem.at[0, k]])

        pltpu.emit_pipeline(
            body, grid=(nblk,),
            in_specs=[pl.BlockSpec((SC_ROWS, d), lambda i: (i, 0)),
                      pl.BlockSpec((1, TOP_K, SC_ROWS), lambda i: (i, 0, 0))],
            out_specs=[], core_axis_name=("core", "subcore"), dimension_semantics=(pltpu.PARALLEL,),
        )(x_hbm, i_hbm)

    return run(rows, pos_w)


def _sc_combine(rows, flat_pos):
    _, d = rows.shape
    n = flat_pos.shape[0]
    nblk = n // SC_ROWS
    pos_w = flat_pos.reshape(nblk, 1, SC_ROWS)

    @functools.partial(pl.kernel, out_type=jax.ShapeDtypeStruct((n, d), rows.dtype), mesh=_sc_mesh(),
                       scratch_types=[], name="combine")
    def run(x_hbm, i_hbm, o_hbm):
        def body(i_vmem, o_vmem):
            pltpu.sync_copy(x_hbm.at[i_vmem.at[0, 0]], o_vmem)

        pltpu.emit_pipeline(
            body, grid=(nblk,),
            in_specs=[pl.BlockSpec((1, 1, SC_ROWS), lambda i: (i, 0, 0))],
            out_specs=[pl.BlockSpec((SC_ROWS, d), lambda i: (i, 0))],
            core_axis_name=("core", "subcore"), dimension_semantics=(pltpu.PARALLEL,),
        )(i_hbm, o_hbm)

    return run(rows, pos_w)


def _experts_kernel(ts_ref, xs_hbm, w1_hbm, b1_ref, w2_hbm, b2_ref, ys_hbm,
                    w1_st, w2_st, w1_s, w2_s, xbuf, ybuf, wsem, xsem, ysem):
    e = pl.program_id(0)
    n_valid = ts_ref[N_EXPERTS]

    def w_copies(ex, slot):
        return (pltpu.make_async_copy(w1_hbm.at[ex], w1_st.at[slot], wsem.at[0, slot]),
                pltpu.make_async_copy(w2_hbm.at[ex], w2_st.at[slot], wsem.at[1, slot]))

    def x_copy(t, slot):
        return pltpu.make_async_copy(xs_hbm.at[pl.ds(t * TM, TM)], xbuf.at[slot], xsem.at[slot])

    def y_copy(t, slot):
        return pltpu.make_async_copy(ybuf.at[slot], ys_hbm.at[pl.ds(t * TM, TM)], ysem.at[slot])

    @pl.when(e == 0)
    def _():
        for c in w_copies(0, 0):
            c.start()
        x_copy(0, 0).start()

    @pl.when(e + 1 < N_EXPERTS)
    def _():
        for c in w_copies(e + 1, (e + 1) % 2):
            c.start()

    wslot = e % 2
    for c in w_copies(e, wslot):
        c.wait()
    for r in range(4):
        rs = slice(256 * r, 256 * (r + 1))
        w1_s[rs, :] = w1_st[wslot, rs, :].astype(BF16)
        w2_s[rs, :] = w2_st[wslot, rs, :].astype(BF16)

    def tile(t, carry):
        slot = t % 2

        @pl.when(t + 1 < n_valid)
        def _():
            x_copy(t + 1, 1 - slot).start()

        x_copy(t, slot).wait()

        @pl.when(t >= 2)
        def _():
            y_copy(t - 2, slot).wait()

        hdn = _bdot(_unpack_pairs(xbuf[slot]), w1_s[...]) + b1_ref[0]
        gt = jnp.minimum(hdn[:, :D_FF], SWIGLU_LIMIT)
        up = jnp.clip(hdn[:, D_FF:], -SWIGLU_LIMIT, SWIGLU_LIMIT)
        act = (up + 1.0) * gt * _sigmoid(SWIGLU_ALPHA * gt)
        ybuf[slot] = _pack_pairs(_bdot(act, w2_s[...]) + b2_ref[0])
        y_copy(t, slot).start()
        return carry

    lax.fori_loop(ts_ref[e], ts_ref[e + 1], tile, 0)

    @pl.when(e == N_EXPERTS - 1)
    def _():
        @pl.when(n_valid >= 2)
        def _():
            y_copy(n_valid - 2, n_valid % 2).wait()

        y_copy(n_valid - 1, (n_valid - 1) % 2).wait()


def _experts(tile_start, xs, w_up, b_up, w_down, b_down):
    wsel = lambda e, ts: (e, 0, 0)
    hbm = pl.BlockSpec(memory_space=pl.ANY)
    grid_spec = pltpu.PrefetchScalarGridSpec(
        num_scalar_prefetch=1, grid=(N_EXPERTS,),
        in_specs=[hbm, hbm, pl.BlockSpec((1, 1, 2 * D_FF), wsel), hbm, pl.BlockSpec((1, 1, D_MODEL), wsel)],
        out_specs=hbm,
        scratch_shapes=[pltpu.VMEM((2, D_MODEL, 2 * D_FF), F32), pltpu.VMEM((2, D_FF, D_MODEL), F32),
                        pltpu.VMEM((D_MODEL, 2 * D_FF), BF16), pltpu.VMEM((D_FF, D_MODEL), BF16),
                        pltpu.VMEM((2, TM, D_PACK), jnp.int32), pltpu.VMEM((2, TM, D_PACK), jnp.int32),
                        pltpu.SemaphoreType.DMA((2, 2)), pltpu.SemaphoreType.DMA((2,)),
                        pltpu.SemaphoreType.DMA((2,))],
    )
    return pl.pallas_call(
        _experts_kernel, grid_spec=grid_spec, out_shape=jax.ShapeDtypeStruct((P_ROWS, D_PACK), jnp.int32),
        name="experts",
        compiler_params=pltpu.CompilerParams(dimension_semantics=("arbitrary",), vmem_limit_bytes=VMEM_LIMIT),
    )(tile_start, xs, w_up, b_up.reshape(N_EXPERTS, 1, 2 * D_FF), w_down, b_down.reshape(N_EXPERTS, 1, D_MODEL))


def _final_kernel(h_ref, yg_ref, gw_ref, p_ref, gple_ref, wg_ref, wp_ref, gfin_ref, out_ref):
    rows = h_ref.shape[0]
    gw = gw_ref[...]
    h = h_ref[...]
    for k in range(TOP_K):
        h = h + gw[:, k:k + 1] * _unpack_pairs(yg_ref[k])
    gate = _sigmoid(_bdot(_rms(h, gple_ref[...]), wg_ref[...]))
    h = h + gate * _bdot(p_ref[...].reshape(rows, PLE_DIM), wp_ref[...])
    out_ref[...] = _rms(h, gfin_ref[...]).reshape(out_ref.shape)


def _final_prompt(hmid, yg, gwt, p_prompt, gple, wg, wp, gfin):
    nb = 4
    rows = nb * CHUNK
    rb = lambda c, b: (c * (BATCH // nb) + b, 0)
    return pl.pallas_call(
        _final_kernel, grid=(N_CHUNKS, BATCH // nb),
        in_specs=[pl.BlockSpec((rows, D_MODEL), rb),
                  pl.BlockSpec((TOP_K, rows, D_PACK), lambda c, b: (0, c * (BATCH // nb) + b, 0)),
                  pl.BlockSpec((rows, TOP_K), rb),
                  pl.BlockSpec((nb, CHUNK, PLE_DIM), lambda c, b: (b, c, 0)),
                  _full((1, D_MODEL)), _full((D_MODEL, D_MODEL)), _full((PLE_DIM, D_MODEL)), _full((1, D_MODEL))],
        out_specs=pl.BlockSpec((nb, CHUNK, D_MODEL), lambda c, b: (b, c, 0)),
        out_shape=jax.ShapeDtypeStruct((BATCH, SEQ, D_MODEL), F32), name="final_prompt",
        compiler_params=pltpu.CompilerParams(dimension_semantics=("arbitrary", "arbitrary"),
                                             vmem_limit_bytes=VMEM_LIMIT),
    )(hmid, yg, gwt, p_prompt, gple, wg, wp, gfin)


def _final_sample(hmid, yg, gwt, p_sample, gple, wg, wp, gfin):
    blk = SAMPLE_ROW0 // DEC_BATCH
    return pl.pallas_call(
        _final_kernel, grid=(1,),
        in_specs=[pl.BlockSpec((DEC_BATCH, D_MODEL), lambda i: (blk, 0)),
                  pl.BlockSpec((TOP_K, DEC_BATCH, D_PACK), lambda i: (0, blk, 0)),
                  pl.BlockSpec((DEC_BATCH, TOP_K), lambda i: (blk, 0)),
                  _full((DEC_BATCH, PLE_DIM)),
                  _full((1, D_MODEL)), _full((D_MODEL, D_MODEL)), _full((PLE_DIM, D_MODEL)), _full((1, D_MODEL))],
        out_specs=_full((DEC_BATCH, D_MODEL)),
        out_shape=jax.ShapeDtypeStruct((DEC_BATCH, D_MODEL), F32), name="final_sample",
        compiler_params=pltpu.CompilerParams(dimension_semantics=("arbitrary",), vmem_limit_bytes=VMEM_LIMIT),
    )(hmid, yg, gwt, p_sample, gple, wg, wp, gfin)


def _alibi_tables():
    slopes = 2.0 ** (-8.0 * (np.arange(N_HEADS, dtype=np.float64) + 1.0) / N_HEADS)
    i = np.arange(CHUNK)[:, None]
    j = np.arange(2 * CHUNK)[None, :]
    dist = i + CHUNK - j
    valid = (dist >= 0) & (dist <= WINDOW)
    tabs = []
    for has_prev in (False, True):
        ok = valid & ((j >= CHUNK) | has_prev)
        tabs.append(np.where(ok[None], -slopes[:, None, None] * dist[None], NEG))
    prompt = np.stack([np.stack([np.concatenate([t[h] for h in grp], axis=0) for grp in (NAT_HEADS, ROT_HEADS)])
                       for t in tabs]).astype(np.float32)
    wb = min(WINDOW, PAST_LEN)
    sample = (-slopes[:, None] * (wb - np.arange(wb))[None, :]).astype(np.float32)
    return prompt, sample


def kernel(x_prompt, x_sample, cache_k_win, cache_v_win, state_ssm_re, state_ssm_im, p_prompt, p_sample, norm_mix, w_in, sinks, ssm_lam_re, ssm_lam_im, ssm_log_step, ssm_b_re, ssm_b_im, ssm_c_re, ssm_c_im, ssm_d, w_glu, b_glu, norm_attn_out, norm_ssm_out, w_out, norm_ffn, w_router, b_router, w_up, b_up, w_down, b_down, norm_ple, w_ple_gate, w_ple_proj, norm_final):
    nst = SSM_G * SSM_P
    bias_np, sbias_np = _alibi_tables()
    bias = jnp.asarray(bias_np)
    sbias = jnp.asarray(sbias_np)

    lbr, lbi, bbr, bbi = _prep(ssm_lam_re[0], ssm_lam_im[0], ssm_log_step[0], ssm_b_re[0], ssm_b_im[0])
    bblk, lamr, lami, cblk = _s5_blocks(lbr, lbi, bbr, bbi, ssm_c_re[0], ssm_c_im[0])

    gmix = norm_mix[0].reshape(1, D_MODEL)
    win = w_in[0].astype(BF16)
    dskip = ssm_d[0].reshape(1, D_SSM)
    wglu = w_glu[0].astype(BF16)
    bglu = b_glu[0].reshape(1, D_SSM)
    ga = norm_attn_out[0].reshape(1, D_ATTN)
    gs = norm_ssm_out[0].reshape(1, D_SSM)
    wout = w_out[0].astype(BF16)
    sink = sinks[0]

    hbuf, k_last, v_last, re_p, im_p = _mixer_prompt(
        x_prompt, sink, gmix, win, bias, bblk, lamr, lami, cblk, dskip, wglu, bglu, ga, gs, wout)

    ck = cache_k_win[0].reshape(DEC_BATCH, WINDOW, D_KV)
    cv = cache_v_win[0].reshape(DEC_BATCH, WINDOW, D_KV)
    hmid, k_new, v_new, re_s, im_s = _mixer_sample(
        x_sample.reshape(DEC_BATCH, D_MODEL), gmix, win, ck, cv, sbias, sink.reshape(N_HEADS, 1), bblk, lamr, lami,
        cblk, dskip, wglu, bglu, ga, gs, wout, state_ssm_re[0].reshape(DEC_BATCH, nst),
        state_ssm_im[0].reshape(DEC_BATCH, nst), hbuf)

    hn, idx, gw, rank, counts = _router(hmid, norm_ffn[0].reshape(1, D_MODEL), w_router[0].T,
                                        b_router[0].reshape(N_EXPERTS, 1))

    tiles_per = (counts[:, 0] + (TM - 1)) // TM
    tile_end = jnp.cumsum(tiles_per)
    offsets = ((tile_end - tiles_per) * TM).astype(jnp.int32)
    tile_start = jnp.concatenate([jnp.zeros((1,), jnp.int32), tile_end.astype(jnp.int32)])

    pos = _place(offsets, idx, rank)
    xs = _sc_dispatch(hn, pos)
    ys = _experts(tile_start, xs, w_up[0], b_up[0], w_down[0], b_down[0])
    yg = _sc_combine(ys, pos.reshape(TOP_K * T_PAD)).reshape(TOP_K, T_PAD, D_PACK)

    gwt = gw.T
    gple = norm_ple[0].reshape(1, D_MODEL)
    wg = w_ple_gate[0].astype(BF16)
    wp = w_ple_proj[0].astype(BF16)
    gfin = norm_final.reshape(1, D_MODEL)
    y_prompt = _final_prompt(hmid, yg, gwt, p_prompt[0], gple, wg, wp, gfin)
    y_sample = _final_sample(hmid, yg, gwt, p_sample[0].reshape(DEC_BATCH, PLE_DIM), gple, wg, wp, gfin)

    k_win_s = jnp.concatenate([ck[:, 1:], k_new[:, None, :]], axis=1)
    v_win_s = jnp.concatenate([cv[:, 1:], v_new[:, None, :]], axis=1)
    kv5 = (1, BATCH, CHUNK, N_KV, HEAD_DIM)
    skv5 = (1, DEC_BATCH, WINDOW, N_KV, HEAD_DIM)
    return (y_prompt, y_sample.reshape(DEC_BATCH, 1, D_MODEL),
            k_last.reshape(kv5), v_last.reshape(kv5),
            re_p.reshape(1, BATCH, SSM_G, SSM_P), im_p.reshape(1, BATCH, SSM_G, SSM_P),
            k_win_s.reshape(skv5), v_win_s.reshape(skv5),
            re_s.reshape(1, DEC_BATCH, SSM_G, SSM_P), im_s.reshape(1, DEC_BATCH, SSM_G, SSM_P))
```

```python
import functools

import numpy as np
import jax
import jax.numpy as jnp
from jax import lax
from jax.experimental import pallas as pl
from jax.experimental.pallas import tpu as pltpu
from jax.experimental.pallas import tpu_sc as plsc

F32 = jnp.float32
BF16 = jnp.bfloat16

D_MODEL = 1024
BATCH = 8
SEQ = 2048
DEC_BATCH = 128
PAST_LEN = 16384
HEAD_DIM = 64
D_ATTN = 512
N_HEADS = 8
N_KV = 2
D_KV = N_KV * HEAD_DIM
WINDOW = 128
D_SSM = 512
SSM_H = 16
SSM_G = 32
SSM_P = 64
D_IN = D_ATTN + 2 * D_KV + D_SSM
N_EXPERTS = 32
TOP_K = 4
D_FF = 1024
SWIGLU_LIMIT = 7.0
SWIGLU_ALPHA = 1.702
PLE_DIM = 256
EPS = 1e-5
NEG = -1e30

CHUNK = 128
N_CHUNKS = SEQ // CHUNK
ROWS = BATCH * CHUNK
PITCH = CHUNK + 8
NAT_HEADS = (0, 2, 5, 7)
ROT_HEADS = (1, 3, 4, 6)
SUB_T = 32
SUB_ROWS = SUB_T * BATCH
N_SBLK = 4
SBLK = 512
T_REAL = BATCH * SEQ + DEC_BATCH
T_PAD = T_REAL + 128
SAMPLE_ROW0 = BATCH * SEQ
ROUTER_TILE = 640
TM = 512
N_TILES = (T_PAD * TOP_K + N_EXPERTS * (TM - 1)) // TM
P_ROWS = N_TILES * TM
SC_ROWS = 40
SC_ROWS_COMBINE = 32
N_PARTS = 4
PART_CHUNKS = N_CHUNKS // N_PARTS
D_PACK = D_MODEL // 2
VMEM_LIMIT = 56 * 1024 * 1024


def _rms(x, g):
    return x * lax.rsqrt(jnp.mean(x * x, axis=-1, keepdims=True) + EPS) * g


def _sigmoid(x):
    return 1.0 / (1.0 + jnp.exp(-x))


def _gelu_tanh(x):
    c = np.float32(np.sqrt(2.0 / np.pi))
    return 0.5 * x * (1.0 + jnp.tanh(c * (x + 0.044715 * (x * x * x))))


def _bdot(a, b):
    return jnp.dot(a.astype(BF16), b, preferred_element_type=F32)


def _pack_pairs(x):
    n = x.shape[1] // 2
    lo = lax.bitcast_convert_type(x[:, :n].astype(BF16).astype(F32), jnp.int32)
    hi = lax.bitcast_convert_type(x[:, n:].astype(BF16).astype(F32), jnp.int32)
    return lax.shift_right_logical(lo, 16) | hi


def _unpack_pairs(w):
    lo = lax.bitcast_convert_type(lax.shift_left(w, 16), F32)
    hi = lax.bitcast_convert_type(w & jnp.int32(-65536), F32)
    return jnp.concatenate([lo, hi], axis=1)


def _full(shape):
    n = len(shape)
    return pl.BlockSpec(shape, lambda *_: (0,) * n)


def _prep_kernel(lr_ref, li_ref, ls_ref, br_ref, bi_ref, lbr_ref, lbi_ref, bbr_ref, bbi_ref):
    lr = lr_ref[...]
    li = li_ref[...]
    step = jnp.exp(ls_ref[...])
    zr = lr * step
    zi = li * step
    mag = jnp.exp(zr)
    lbr = mag * jnp.cos(zi)
    lbi = mag * jnp.sin(zi)
    lbr_ref[...] = lbr
    lbi_ref[...] = lbi
    nr = lbr - 1.0
    den = lr * lr + li * li
    cr = (nr * lr + lbi * li) / den
    ci = (lbi * lr - nr * li) / den
    br = br_ref[...]
    bi = bi_ref[...]
    bbr_ref[...] = cr * br - ci * bi
    bbi_ref[...] = cr * bi + ci * br


def _prep(lam_re, lam_im, log_step, b_re, b_im):
    g, p, h = SSM_G, SSM_P, SSM_H
    out = pl.pallas_call(
        _prep_kernel,
        out_shape=[jax.ShapeDtypeStruct((g, 1, p), F32), jax.ShapeDtypeStruct((g, 1, p), F32),
                   jax.ShapeDtypeStruct((g, h, p), F32), jax.ShapeDtypeStruct((g, h, p), F32)],
        name="s5_prep",
    )(lam_re.reshape(g, 1, p), lam_im.reshape(g, 1, p), log_step.reshape(g, 1, 1),
      jnp.transpose(b_re, (0, 2, 1)), jnp.transpose(b_im, (0, 2, 1)))
    return out


def _s5_blocks(lbr, lbi, bbr, bbi, c_re, c_im):
    eye = jnp.eye(8, dtype=F32)
    shp = (N_SBLK, 8, SSM_H, SSM_P)
    b_r = jnp.einsum('jghp,gk->jghkp', bbr.reshape(shp), eye).reshape(N_SBLK, 128, SBLK)
    b_i = jnp.einsum('jghp,gk->jghkp', bbi.reshape(shp), eye).reshape(N_SBLK, 128, SBLK)
    bblk = jnp.concatenate([b_r, b_i], axis=-1).astype(BF16)
    c_r = jnp.einsum('jghp,gk->jgpkh', c_re.reshape(shp), eye).reshape(N_SBLK, SBLK, 128)
    c_i = jnp.einsum('jghp,gk->jgpkh', c_im.reshape(shp), eye).reshape(N_SBLK, SBLK, 128)
    cblk = jnp.concatenate([c_r, -c_i], axis=1).astype(BF16)
    return bblk, lbr.reshape(N_SBLK, 1, SBLK), lbi.reshape(N_SBLK, 1, SBLK), cblk


def _ssm_post(y_lin, u, dskip, wglu, bglu, gs):
    y = _gelu_tanh(y_lin + dskip * u)
    y = y * _sigmoid(_bdot(y, wglu) + bglu)
    return _rms(y, gs)


def _mixer_prompt_kernel(sinks_ref, x_ref, gmix_ref, win_ref, bias_ref, bblk_ref, lamr_ref, lami_ref,
                         cblk_ref, dskip_ref, wglu_ref, bglu_ref, ga_ref, gs_ref, wout_ref,
                         hmid_ref, klast_ref, vlast_ref, sre_ref, sim_ref,
                         proj_s, u_s, kprev_s, kprevr_s, vprev_s, vprevr_s, attn_s, ssm_s, utb_s, bu_s, xs_s, ytb_s):
    c = pl.program_id(0)

    @pl.when(c == 0)
    def _():
        zkv = jnp.zeros(kprev_s.shape, BF16)
        kprev_s[...] = zkv
        kprevr_s[...] = zkv
        vprev_s[...] = zkv
        vprevr_s[...] = zkv
        sre_ref[...] = jnp.zeros(sre_ref.shape, F32)
        sim_ref[...] = jnp.zeros(sim_ref.shape, F32)

    x = x_ref[...].reshape(ROWS, D_MODEL)
    proj = _bdot(_rms(x, gmix_ref[...]), win_ref[...])
    u0 = D_ATTN + 2 * D_KV
    proj_s[...] = proj[:, 0:u0]
    for jb in range(D_SSM // 128):
        for b in range(BATCH):
            u_s[jb, b * PITCH:b * PITCH + CHUNK, :] = proj[b * CHUNK:(b + 1) * CHUNK, u0 + 128 * jb:u0 + 128 * (jb + 1)]

    lo = lax.broadcasted_iota(jnp.int32, (CHUNK, 128), 1) < HEAD_DIM
    hi = jnp.logical_not(lo)
    table = jnp.minimum(c, 1)
    hrow = lax.broadcasted_iota(jnp.int32, (4 * CHUNK, 1), 0) // CHUNK

    def sink_col(heads):
        col = jnp.full((4 * CHUNK, 1), sinks_ref[heads[3]], F32)
        for n in (2, 1, 0):
            col = jnp.where(hrow == n, sinks_ref[heads[n]], col)
        return col

    sink_nat = sink_col(NAT_HEADS)
    sink_rot = sink_col(ROT_HEADS)

    def attend(q, k, v, bias, sink):
        s = lax.dot_general(q, k, (((1,), (1,)), ((), ())), preferred_element_type=F32) + bias
        m = jnp.maximum(jnp.max(s, axis=-1, keepdims=True), sink)
        p = jnp.exp(s - m)
        den = jnp.sum(p, axis=-1, keepdims=True) + jnp.exp(sink - m)
        return jnp.dot(p.astype(BF16), v, preferred_element_type=F32) / den

    def attn_body(b, carry):
        r0 = pl.multiple_of(b * CHUNK, CHUNK)
        rows = pl.ds(r0, CHUNK)
        kb = proj_s[rows, D_ATTN:D_ATTN + D_KV]
        vb = proj_s[rows, D_ATTN + D_KV:D_ATTN + 2 * D_KV]
        kb16 = kb.astype(BF16)
        vb16 = vb.astype(BF16)
        kbr16 = pltpu.roll(kb, HEAD_DIM, 1).astype(BF16)
        vbr16 = pltpu.roll(vb, HEAD_DIM, 1).astype(BF16)
        k_nat = jnp.concatenate([kprev_s[b], kb16], axis=0)
        k_rot = jnp.concatenate([kprevr_s[b], kbr16], axis=0)
        v_nat = jnp.concatenate([vprev_s[b], vb16], axis=0)
        v_rot = jnp.concatenate([vprevr_s[b], vbr16], axis=0)
        q2 = [proj_s[rows, 128 * jq:128 * (jq + 1)] * (HEAD_DIM ** -0.5) for jq in range(N_HEADS // 2)]
        q_nat = jnp.concatenate([jnp.where(lo if h % 2 == 0 else hi, q2[h // 2], 0.0) for h in NAT_HEADS],
                                axis=0).astype(BF16)
        q_rot = jnp.concatenate([jnp.where(lo if h % 2 == 0 else hi, q2[h // 2], 0.0) for h in ROT_HEADS],
                                axis=0).astype(BF16)
        o_nat = attend(q_nat, k_nat, v_nat, bias_ref[table, 0], sink_nat)
        o_rot = attend(q_rot, k_rot, v_rot, bias_ref[table, 1], sink_rot)
        for jq in range(N_HEADS // 2):
            blk = slice(CHUNK * jq, CHUNK * (jq + 1))
            even, odd = (o_nat, o_rot) if jq < 2 else (o_rot, o_nat)
            attn_s[rows, 128 * jq:128 * (jq + 1)] = jnp.where(lo, even[blk], odd[blk])
        kprev_s[b] = kb16
        kprevr_s[b] = kbr16
        vprev_s[b] = vb16
        vprevr_s[b] = vbr16
        return carry

    lax.fori_loop(0, BATCH, attn_body, 0)

    for sc in range(CHUNK // SUB_T):
        t0 = sc * SUB_T
        for i in range(SUB_T):
            for jb in range(D_SSM // 128):
                utb_s[i * BATCH:(i + 1) * BATCH, 128 * jb:128 * (jb + 1)] = (
                    u_s[jb, pl.ds(t0 + i, BATCH, stride=PITCH), :])
        u_tb = utb_s[...]
        for j in range(N_SBLK):
            bu_s[...] = _bdot(u_tb[:, 128 * j:128 * (j + 1)], bblk_ref[j])
            lr = jnp.broadcast_to(lamr_ref[j], (BATCH, SBLK))
            li = jnp.broadcast_to(lami_ref[j], (BATCH, SBLK))
            cols = slice(SBLK * j, SBLK * (j + 1))

            def step(i, carry):
                sr, si = carry
                r = pl.ds(pl.multiple_of(i * BATCH, BATCH), BATCH)
                nr = lr * sr - li * si + bu_s[r, 0:SBLK]
                ni = lr * si + li * sr + bu_s[r, SBLK:2 * SBLK]
                xs_s[r, 0:SBLK] = nr
                xs_s[r, SBLK:2 * SBLK] = ni
                return nr, ni

            sr, si = lax.fori_loop(0, SUB_T, step, (sre_ref[:, cols], sim_ref[:, cols]), unroll=True)
            sre_ref[:, cols] = sr
            sim_ref[:, cols] = si
            ytb_s[:, 128 * j:128 * (j + 1)] = _bdot(xs_s[...], cblk_ref[j])
        yn = _ssm_post(ytb_s[...], u_tb, dskip_ref[...], wglu_ref[...], bglu_ref[...], gs_ref[...])
        for i in range(SUB_T):
            for jb in range(D_SSM // 128):
                ssm_s[jb, pl.ds(t0 + i, BATCH, stride=PITCH), :] = (
                    yn[i * BATCH:(i + 1) * BATCH, 128 * jb:128 * (jb + 1)])

    an = _rms(attn_s[...], ga_ref[...])
    sn = jnp.concatenate(
        [jnp.concatenate([ssm_s[jb, b * PITCH:b * PITCH + CHUNK, :] for b in range(BATCH)], axis=0)
         for jb in range(D_SSM // 128)], axis=1)
    hmid_ref[...] = x + _bdot(an, wout_ref[0:D_ATTN, :]) + _bdot(sn, wout_ref[D_ATTN:2 * D_ATTN, :])

    @pl.when(c == N_CHUNKS - 1)
    def _():
        klast_ref[...] = proj_s[:, D_ATTN:D_ATTN + D_KV].reshape(BATCH, CHUNK, D_KV)
        vlast_ref[...] = proj_s[:, D_ATTN + D_KV:D_ATTN + 2 * D_KV].reshape(BATCH, CHUNK, D_KV)


def _mixer_prompt(x_prompt, sinks, gmix, win, bias, bblk, lamr, lami, cblk, dskip, wglu, bglu, ga, gs, wout):
    smem = pl.BlockSpec(memory_space=pltpu.SMEM)
    in_specs = [
        smem,
        pl.BlockSpec((BATCH, CHUNK, D_MODEL), lambda c: (0, c, 0)),
        _full((1, D_MODEL)), _full((D_MODEL, D_IN)), _full((2, 2, 4 * CHUNK, 2 * CHUNK)),
        _full((N_SBLK, 128, 2 * SBLK)), _full((N_SBLK, 1, SBLK)), _full((N_SBLK, 1, SBLK)),
        _full((N_SBLK, 2 * SBLK, 128)), _full((1, D_SSM)), _full((D_SSM, D_SSM)), _full((1, D_SSM)),
        _full((1, D_ATTN)), _full((1, D_SSM)), _full((D_MODEL, D_MODEL)),
    ]
    out_shape = [
        jax.ShapeDtypeStruct((T_PAD, D_MODEL), F32),
        jax.ShapeDtypeStruct((BATCH, CHUNK, D_KV), F32),
        jax.ShapeDtypeStruct((BATCH, CHUNK, D_KV), F32),
        jax.ShapeDtypeStruct((BATCH, SSM_G * SSM_P), F32),
        jax.ShapeDtypeStruct((BATCH, SSM_G * SSM_P), F32),
    ]
    out_specs = [
        pl.BlockSpec((ROWS, D_MODEL), lambda c: (c, 0)),
        _full((BATCH, CHUNK, D_KV)), _full((BATCH, CHUNK, D_KV)),
        _full((BATCH, SSM_G * SSM_P)), _full((BATCH, SSM_G * SSM_P)),
    ]
    kv_scr = pltpu.VMEM((BATCH, CHUNK, D_KV), BF16)
    scratch = [
        pltpu.VMEM((ROWS, D_ATTN + 2 * D_KV), F32), pltpu.VMEM((D_SSM // 128, BATCH * PITCH, 128), F32),
        kv_scr, kv_scr, kv_scr, kv_scr,
        pltpu.VMEM((ROWS, D_ATTN), F32), pltpu.VMEM((D_SSM // 128, BATCH * PITCH, 128), F32),
        pltpu.VMEM((SUB_ROWS, D_SSM), F32), pltpu.VMEM((SUB_ROWS, 2 * SBLK), F32),
        pltpu.VMEM((SUB_ROWS, 2 * SBLK), F32), pltpu.VMEM((SUB_ROWS, D_SSM), F32),
    ]
    return pl.pallas_call(
        _mixer_prompt_kernel, grid=(N_CHUNKS,), in_specs=in_specs, out_specs=out_specs, out_shape=out_shape,
        scratch_shapes=scratch, name="mixer_prompt",
        compiler_params=pltpu.CompilerParams(dimension_semantics=("arbitrary",), vmem_limit_bytes=VMEM_LIMIT),
    )(sinks, x_prompt, gmix, win, bias, bblk, lamr, lami, cblk, dskip, wglu, bglu, ga, gs, wout)


SGRP = 16
SPITCH = DEC_BATCH + 8
N_SGRP = DEC_BATCH // SGRP


def _mixer_sample_kernel(x_ref, gmix_ref, win_ref, ck_ref, cv_ref, sbias_ref, sinkc_ref, bblk_ref, lamr_ref,
                         lami_ref, cblk_ref, dskip_ref, wglu_ref, bglu_ref, ga_ref, gs_ref, wout_ref,
                         x0r_ref, x0i_ref, hbuf_ref,
                         hmid_ref, knew_ref, vnew_ref, sre_ref, sim_ref,
                         proj_s, qall_s, oall_s, kn8_s, vn8_s):
    del hbuf_ref
    g = pl.program_id(0)
    lo = lax.broadcasted_iota(jnp.int32, (DEC_BATCH, 128), 1) < HEAD_DIM

    @pl.when(g == 0)
    def _():
        proj = _bdot(_rms(x_ref[...], gmix_ref[...]), win_ref[...])
        proj_s[...] = proj
        for h in range(N_HEADS):
            jq, half, kv = h // 2, h % 2, h // 4
            q2 = proj[:, 128 * jq:128 * (jq + 1)] * (HEAD_DIM ** -0.5)
            if half != kv:
                q2 = pltpu.roll(q2, HEAD_DIM, 1)
            hr = slice(h * SPITCH, h * SPITCH + DEC_BATCH)
            qall_s[hr, :] = jnp.where(lo if kv == 0 else jnp.logical_not(lo), q2, 0.0)
            kn8_s[hr, :] = proj[:, D_ATTN:D_ATTN + D_KV]
            vn8_s[hr, :] = proj[:, D_ATTN + D_KV:D_ATTN + 2 * D_KV]

    def head_rows(ref):
        return jnp.stack([ref[pl.ds(g * SGRP + ii, N_HEADS, stride=SPITCH), :] for ii in range(SGRP)], axis=0)

    qe = head_rows(qall_s)
    knew = head_rows(kn8_s)
    vnew = head_rows(vn8_s)
    s = jnp.einsum('bhd,bkd->bhk', qe.astype(BF16), ck_ref[...].astype(BF16),
                   preferred_element_type=F32) + sbias_ref[...][None]
    s_new = jnp.sum(qe * knew, axis=-1, keepdims=True)
    sink = sinkc_ref[...][None]
    m = jnp.maximum(jnp.maximum(jnp.max(s, axis=-1, keepdims=True), s_new), sink)
    p = jnp.exp(s - m)
    p_new = jnp.exp(s_new - m)
    den = jnp.sum(p, axis=-1, keepdims=True) + p_new + jnp.exp(sink - m)
    o = (jnp.einsum('bhk,bkd->bhd', p.astype(BF16), cv_ref[...].astype(BF16),
                    preferred_element_type=F32) + p_new * vnew) / den
    for ii in range(SGRP):
        oall_s[pl.ds(g * SGRP + ii, N_HEADS, stride=SPITCH), :] = o[ii]

    @pl.when(g == N_SGRP - 1)
    def _():
        proj = proj_s[...]
        blocks = []
        for jq in range(N_HEADS // 2):
            oa = oall_s[(2 * jq) * SPITCH:(2 * jq) * SPITCH + DEC_BATCH, :]
            ob = oall_s[(2 * jq + 1) * SPITCH:(2 * jq + 1) * SPITCH + DEC_BATCH, :]
            if jq // 2 == 0:
                blocks.append(jnp.where(lo, oa, pltpu.roll(ob, HEAD_DIM, 1)))
            else:
                blocks.append(jnp.where(lo, pltpu.roll(oa, HEAD_DIM, 1), ob))
        attn = jnp.concatenate(blocks, axis=1)
        u = proj[:, D_ATTN + 2 * D_KV:]
        ys = []
        for j in range(N_SBLK):
            bu = _bdot(u[:, 128 * j:128 * (j + 1)], bblk_ref[j])
            lr = lamr_ref[j]
            li = lami_ref[j]
            cols = slice(SBLK * j, SBLK * (j + 1))
            sr = x0r_ref[:, cols]
            si = x0i_ref[:, cols]
            nr = lr * sr - li * si + bu[:, 0:SBLK]
            ni = lr * si + li * sr + bu[:, SBLK:2 * SBLK]
            sre_ref[:, cols] = nr
            sim_ref[:, cols] = ni
            ys.append(_bdot(jnp.concatenate([nr, ni], axis=1), cblk_ref[j]))
        sn = _ssm_post(jnp.concatenate(ys, axis=1), u, dskip_ref[...], wglu_ref[...], bglu_ref[...], gs_ref[...])
        an = _rms(attn, ga_ref[...])
        h = x_ref[...] + _bdot(an, wout_ref[0:D_ATTN, :]) + _bdot(sn, wout_ref[D_ATTN:2 * D_ATTN, :])
        hmid_ref[0:DEC_BATCH, :] = h
        hmid_ref[DEC_BATCH:2 * DEC_BATCH, :] = jnp.zeros((DEC_BATCH, D_MODEL), F32)
        knew_ref[...] = proj[:, D_ATTN:D_ATTN + D_KV]
        vnew_ref[...] = proj[:, D_ATTN + D_KV:D_ATTN + 2 * D_KV]


def _mixer_sample(x_s, gmix, win, ck, cv, sbias, sinkc, bblk, lamr, lami, cblk, dskip, wglu, bglu, ga, gs, wout,
                  x0r, x0i, hbuf):
    nst = SSM_G * SSM_P
    in_specs = [
        _full((DEC_BATCH, D_MODEL)), _full((1, D_MODEL)), _full((D_MODEL, D_IN)),
        pl.BlockSpec((SGRP, WINDOW, D_KV), lambda g: (g, 0, 0)),
        pl.BlockSpec((SGRP, WINDOW, D_KV), lambda g: (g, 0, 0)),
        _full((N_HEADS, WINDOW)), _full((N_HEADS, 1)),
        _full((N_SBLK, 128, 2 * SBLK)), _full((N_SBLK, 1, SBLK)), _full((N_SBLK, 1, SBLK)),
        _full((N_SBLK, 2 * SBLK, 128)), _full((1, D_SSM)), _full((D_SSM, D_SSM)), _full((1, D_SSM)),
        _full((1, D_ATTN)), _full((1, D_SSM)), _full((D_MODEL, D_MODEL)),
        _full((DEC_BATCH, nst)), _full((DEC_BATCH, nst)),
        pl.BlockSpec(memory_space=pl.ANY),
    ]
    out_shape = [
        jax.ShapeDtypeStruct((T_PAD, D_MODEL), F32),
        jax.ShapeDtypeStruct((DEC_BATCH, D_KV), F32), jax.ShapeDtypeStruct((DEC_BATCH, D_KV), F32),
        jax.ShapeDtypeStruct((DEC_BATCH, nst), F32), jax.ShapeDtypeStruct((DEC_BATCH, nst), F32),
    ]
    out_specs = [
        pl.BlockSpec((2 * DEC_BATCH, D_MODEL), lambda g: (SAMPLE_ROW0 // (2 * DEC_BATCH), 0)),
        _full((DEC_BATCH, D_KV)), _full((DEC_BATCH, D_KV)),
        _full((DEC_BATCH, nst)), _full((DEC_BATCH, nst)),
    ]
    head_rows = pltpu.VMEM((N_HEADS * SPITCH, 128), F32)
    scratch = [pltpu.VMEM((DEC_BATCH, D_IN), F32), head_rows, head_rows, head_rows, head_rows]
    return pl.pallas_call(
        _mixer_sample_kernel, grid=(N_SGRP,), in_specs=in_specs, out_specs=out_specs, out_shape=out_shape,
        scratch_shapes=scratch, input_output_aliases={19: 0}, name="mixer_sample",
        compiler_params=pltpu.CompilerParams(dimension_semantics=("arbitrary",), vmem_limit_bytes=VMEM_LIMIT),
    )(x_s, gmix, win, ck, cv, sbias, sinkc, bblk, lamr, lami, cblk, dskip, wglu, bglu, ga, gs, wout, x0r, x0i, hbuf)


def _router_kernel(h_ref, g_ref, wrt_ref, br_ref, tri_ref, hn_ref, idx_ref, gw_ref, rank_ref, cnt_ref):
    i = pl.program_id(0)

    @pl.when(i == 0)
    def _():
        cnt_ref[...] = jnp.zeros(cnt_ref.shape, jnp.int32)

    hn = _rms(h_ref[...], g_ref[...])
    hn_ref[...] = _pack_pairs(hn)
    logits = lax.dot_general(wrt_ref[...], hn, (((1,), (1,)), ((), ())), precision=lax.Precision.HIGHEST,
                             preferred_element_type=F32) + br_ref[...]
    eidx = lax.broadcasted_iota(jnp.int32, logits.shape, 0)
    vals, onehots = [], []
    l = logits
    for k in range(TOP_K):
        m = jnp.max(l, axis=0, keepdims=True)
        ik = jnp.min(jnp.where(l == m, eidx, N_EXPERTS), axis=0, keepdims=True)
        oh = eidx == ik
        idx_ref[k:k + 1, :] = ik
        vals.append(m)
        onehots.append(oh)
        l = jnp.where(oh, -jnp.inf, l)
    exps = [jnp.exp(v - vals[0]) for v in vals]
    den = exps[0] + exps[1] + exps[2] + exps[3]
    for k in range(TOP_K):
        gw_ref[k:k + 1, :] = exps[k] / den
    member = jnp.zeros(logits.shape, F32)
    for oh in onehots:
        member = member + jnp.where(oh, 1.0, 0.0)
    before = jnp.dot(member.astype(BF16), tri_ref[...], preferred_element_type=F32) + cnt_ref[...].astype(F32)
    for k in range(TOP_K):
        rank_ref[k:k + 1, :] = jnp.sum(jnp.where(onehots[k], before, 0.0), axis=0, keepdims=True).astype(jnp.int32)
    cnt_ref[...] += jnp.sum(member, axis=1, keepdims=True).astype(jnp.int32)


def _router(hmid, g_ffn, w_router_t, b_router_c):
    rt = ROUTER_TILE
    tri = jnp.asarray(np.triu(np.ones((rt, rt), np.float32), 1), BF16)
    tok = lambda i: (0, i)
    return pl.pallas_call(
        _router_kernel, grid=(T_PAD // rt,),
        in_specs=[pl.BlockSpec((rt, D_MODEL), lambda i: (i, 0)), _full((1, D_MODEL)),
                  _full((N_EXPERTS, D_MODEL)), _full((N_EXPERTS, 1)), _full((rt, rt))],
        out_specs=[pl.BlockSpec((rt, D_PACK), lambda i: (i, 0)), pl.BlockSpec((TOP_K, rt), tok),
                   pl.BlockSpec((TOP_K, rt), tok), pl.BlockSpec((TOP_K, rt), tok), _full((N_EXPERTS, 1))],
        out_shape=[jax.ShapeDtypeStruct((T_PAD, D_PACK), jnp.int32), jax.ShapeDtypeStruct((TOP_K, T_PAD), jnp.int32),
                   jax.ShapeDtypeStruct((TOP_K, T_PAD), F32), jax.ShapeDtypeStruct((TOP_K, T_PAD), jnp.int32),
                   jax.ShapeDtypeStruct((N_EXPERTS, 1), jnp.int32)],
        name="router",
        compiler_params=pltpu.CompilerParams(dimension_semantics=("arbitrary",), vmem_limit_bytes=VMEM_LIMIT),
    )(hmid, g_ffn, w_router_t, b_router_c, tri)


def _place_kernel(off_ref, idx_ref, rank_ref, pos_ref):
    idx = idx_ref[...]
    pos = rank_ref[...]
    for e in range(N_EXPERTS):
        pos = pos + jnp.where(idx == e, off_ref[e], 0)
    pos_ref[...] = pos


def _place(offsets, idx, rank):
    return pl.pallas_call(
        _place_kernel,
        in_specs=[pl.BlockSpec(memory_space=pltpu.SMEM), pl.BlockSpec(memory_space=pltpu.VMEM),
                  pl.BlockSpec(memory_space=pltpu.VMEM)],
        out_specs=pl.BlockSpec(memory_space=pltpu.VMEM),
        out_shape=jax.ShapeDtypeStruct((TOP_K, T_PAD), jnp.int32), name="place",
    )(offsets, idx, rank)


def _sc_mesh():
    return plsc.VectorSubcoreMesh(core_axis_name="core", subcore_axis_name="subcore")


def _sc_dispatch(rows, pos):
    n, d = rows.shape
    nblk = n // SC_ROWS
    pos_w = pos.reshape(TOP_K, nblk, SC_ROWS).transpose(1, 0, 2)

    @functools.partial(pl.kernel, out_type=jax.ShapeDtypeStruct((P_ROWS, d), rows.dtype), mesh=_sc_mesh(),
                       scratch_types=[], name="dispatch")
    def run(x_hbm, i_hbm, o_hbm):
        def body(x_vmem, i_vmem):
            for k in range(TOP_K):
                pltpu.sync_copy(x_vmem, o_hbm.at[i_vmem.at[0, k]])

        pltpu.emit_pipeline(
            body, grid=(nblk,),
            in_specs=[pl.BlockSpec((SC_ROWS, d), lambda i: (i, 0)),
                      pl.BlockSpec((1, TOP_K, SC_ROWS), lambda i: (i, 0, 0))],
            out_specs=[], core_axis_name=("core", "subcore"), dimension_semantics=(pltpu.PARALLEL,),
        )(x_hbm, i_hbm)

    return run(rows, pos_w)


def _sc_combine(rows, flat_pos):
    _, d = rows.shape
    n = flat_pos.shape[0]
    w = SC_ROWS_COMBINE
    nblk = n // w
    pos_w = flat_pos.reshape(nblk, 1, w)

    @functools.partial(pl.kernel, out_type=jax.ShapeDtypeStruct((n, d), rows.dtype), mesh=_sc_mesh(),
                       scratch_types=[], name="combine")
    def run(x_hbm, i_hbm, o_hbm):
        def body(i_vmem, o_vmem):
            pltpu.sync_copy(x_hbm.at[i_vmem.at[0, 0]], o_vmem)

        pltpu.emit_pipeline(
            body, grid=(nblk,),
            in_specs=[pl.BlockSpec((1, 1, w), lambda i: (i, 0, 0))],
            out_specs=[pl.BlockSpec((w, d), lambda i: (i, 0))],
            core_axis_name=("core", "subcore"), dimension_semantics=(pltpu.PARALLEL,),
        )(i_hbm, o_hbm)

    return run(rows, pos_w)


def _experts_kernel(ts_ref, xs_hbm, w1_hbm, b1_ref, w2_hbm, b2_ref, ys_hbm,
                    w1_st, w2_st, w1_s, w2_s, xbuf, ybuf, wsem, xsem, ysem):
    e = pl.program_id(0)
    n_valid = ts_ref[N_EXPERTS]

    def w_copies(ex, slot):
        return (pltpu.make_async_copy(w1_hbm.at[ex], w1_st.at[slot], wsem.at[0, slot]),
                pltpu.make_async_copy(w2_hbm.at[ex], w2_st.at[slot], wsem.at[1, slot]))

    def x_copy(t, slot):
        return pltpu.make_async_copy(xs_hbm.at[pl.ds(t * TM, TM)], xbuf.at[slot], xsem.at[slot])

    def y_copy(t, slot):
        return pltpu.make_async_copy(ybuf.at[slot], ys_hbm.at[pl.ds(t * TM, TM)], ysem.at[slot])

    @pl.when(e == 0)
    def _():
        for c in w_copies(0, 0):
            c.start()
        x_copy(0, 0).start()

    @pl.when(e + 1 < N_EXPERTS)
    def _():
        for c in w_copies(e + 1, (e + 1) % 2):
            c.start()

    wslot = e % 2
    for c in w_copies(e, wslot):
        c.wait()
    for r in range(4):
        rs = slice(256 * r, 256 * (r + 1))
        w1_s[rs, :] = w1_st[wslot, rs, :].astype(BF16)
        w2_s[rs, :] = w2_st[wslot, rs, :].astype(BF16)

    def tile(t, carry):
        slot = t % 2

        @pl.when(t + 1 < n_valid)
        def _():
            x_copy(t + 1, 1 - slot).start()

        x_copy(t, slot).wait()

        @pl.when(t >= 2)
        def _():
            y_copy(t - 2, slot).wait()

        hdn = _bdot(_unpack_pairs(xbuf[slot]), w1_s[...]) + b1_ref[0]
        gt = jnp.minimum(hdn[:, :D_FF], SWIGLU_LIMIT)
        up = jnp.clip(hdn[:, D_FF:], -SWIGLU_LIMIT, SWIGLU_LIMIT)
        act = (up + 1.0) * gt * _sigmoid(SWIGLU_ALPHA * gt)
        ybuf[slot] = _pack_pairs(_bdot(act, w2_s[...]) + b2_ref[0])
        y_copy(t, slot).start()
        return carry

    lax.fori_loop(ts_ref[e], ts_ref[e + 1], tile, 0)

    @pl.when(e == N_EXPERTS - 1)
    def _():
        @pl.when(n_valid >= 2)
        def _():
            y_copy(n_valid - 2, n_valid % 2).wait()

        y_copy(n_valid - 1, (n_valid - 1) % 2).wait()


def _experts(tile_start, xs, w_up, b_up, w_down, b_down):
    wsel = lambda e, ts: (e, 0, 0)
    hbm = pl.BlockSpec(memory_space=pl.ANY)
    grid_spec = pltpu.PrefetchScalarGridSpec(
        num_scalar_prefetch=1, grid=(N_EXPERTS,),
        in_specs=[hbm, hbm, pl.BlockSpec((1, 1, 2 * D_FF), wsel), hbm, pl.BlockSpec((1, 1, D_MODEL), wsel)],
        out_specs=hbm,
        scratch_shapes=[pltpu.VMEM((2, D_MODEL, 2 * D_FF), F32), pltpu.VMEM((2, D_FF, D_MODEL), F32),
                        pltpu.VMEM((D_MODEL, 2 * D_FF), BF16), pltpu.VMEM((D_FF, D_MODEL), BF16),
                        pltpu.VMEM((2, TM, D_PACK), jnp.int32), pltpu.VMEM((2, TM, D_PACK), jnp.int32),
                        pltpu.SemaphoreType.DMA((2, 2)), pltpu.SemaphoreType.DMA((2,)),
                        pltpu.SemaphoreType.DMA((2,))],
    )
    return pl.pallas_call(
        _experts_kernel, grid_spec=grid_spec, out_shape=jax.ShapeDtypeStruct((P_ROWS, D_PACK), jnp.int32),
        name="experts",
        compiler_params=pltpu.CompilerParams(dimension_semantics=("arbitrary",), vmem_limit_bytes=VMEM_LIMIT),
    )(tile_start, xs, w_up, b_up.reshape(N_EXPERTS, 1, 2 * D_FF), w_down, b_down.reshape(N_EXPERTS, 1, D_MODEL))


def _final_kernel(h_ref, yg_ref, gw_ref, p_ref, gple_ref, wg_ref, wp_ref, gfin_ref, *rest):
    out_ref = rest[-1]
    rows = h_ref.shape[0]
    gw = gw_ref[...]
    h = h_ref[...]
    for k in range(TOP_K):
        h = h + gw[:, k:k + 1] * _unpack_pairs(yg_ref[k])
    gate = _sigmoid(_bdot(_rms(h, gple_ref[...]), wg_ref[...]))
    h = h + gate * _bdot(p_ref[...].reshape(rows, PLE_DIM), wp_ref[...])
    out_ref[...] = _rms(h, gfin_ref[...]).reshape(out_ref.shape)


def _final_prompt(part, hmid, yg, gwt, p_prompt, gple, wg, wp, gfin, y_prev):
    nb = 4
    rows = nb * CHUNK
    nbh = BATCH // nb
    c0 = part * PART_CHUNKS
    rb = lambda c, b: ((c0 + c) * nbh + b, 0)
    in_specs = [pl.BlockSpec((rows, D_MODEL), rb),
                pl.BlockSpec((TOP_K, rows, D_PACK), lambda c, b: (0, c * nbh + b, 0)),
                pl.BlockSpec((rows, TOP_K), rb),
                pl.BlockSpec((nb, CHUNK, PLE_DIM), lambda c, b: (b, c0 + c, 0)),
                _full((1, D_MODEL)), _full((D_MODEL, D_MODEL)), _full((PLE_DIM, D_MODEL)), _full((1, D_MODEL))]
    args = [hmid, yg, gwt, p_prompt, gple, wg, wp, gfin]
    aliases = {}
    if y_prev is not None:
        in_specs.append(pl.BlockSpec(memory_space=pl.ANY))
        args.append(y_prev)
        aliases = {len(args) - 1: 0}
    return pl.pallas_call(
        _final_kernel, grid=(PART_CHUNKS, nbh), in_specs=in_specs,
        out_specs=pl.BlockSpec((nb, CHUNK, D_MODEL), lambda c, b: (b, c0 + c, 0)),
        out_shape=jax.ShapeDtypeStruct((BATCH, SEQ, D_MODEL), F32), name="final_prompt",
        input_output_aliases=aliases,
        compiler_params=pltpu.CompilerParams(dimension_semantics=("arbitrary", "arbitrary"),
                                             vmem_limit_bytes=VMEM_LIMIT),
    )(*args)


def _final_sample(hmid, yg, gwt, p_sample, gple, wg, wp, gfin):
    blk = SAMPLE_ROW0 // DEC_BATCH
    return pl.pallas_call(
        _final_kernel, grid=(1,),
        in_specs=[pl.BlockSpec((DEC_BATCH, D_MODEL), lambda i: (blk, 0)),
                  pl.BlockSpec((TOP_K, DEC_BATCH, D_PACK), lambda i: (0, PART_CHUNKS * ROWS // DEC_BATCH, 0)),
                  pl.BlockSpec((DEC_BATCH, TOP_K), lambda i: (blk, 0)),
                  _full((DEC_BATCH, PLE_DIM)),
                  _full((1, D_MODEL)), _full((D_MODEL, D_MODEL)), _full((PLE_DIM, D_MODEL)), _full((1, D_MODEL))],
        out_specs=_full((DEC_BATCH, D_MODEL)),
        out_shape=jax.ShapeDtypeStruct((DEC_BATCH, D_MODEL), F32), name="final_sample",
        compiler_params=pltpu.CompilerParams(dimension_semantics=("arbitrary",), vmem_limit_bytes=VMEM_LIMIT),
    )(hmid, yg, gwt, p_sample, gple, wg, wp, gfin)


def _alibi_tables():
    slopes = 2.0 ** (-8.0 * (np.arange(N_HEADS, dtype=np.float64) + 1.0) / N_HEADS)
    i = np.arange(CHUNK)[:, None]
    j = np.arange(2 * CHUNK)[None, :]
    dist = i + CHUNK - j
    valid = (dist >= 0) & (dist <= WINDOW)
    tabs = []
    for has_prev in (False, True):
        ok = valid & ((j >= CHUNK) | has_prev)
        tabs.append(np.where(ok[None], -slopes[:, None, None] * dist[None], NEG))
    prompt = np.stack([np.stack([np.concatenate([t[h] for h in grp], axis=0) for grp in (NAT_HEADS, ROT_HEADS)])
                       for t in tabs]).astype(np.float32)
    wb = min(WINDOW, PAST_LEN)
    sample = (-slopes[:, None] * (wb - np.arange(wb))[None, :]).astype(np.float32)
    return prompt, sample


def kernel(x_prompt, x_sample, cache_k_win, cache_v_win, state_ssm_re, state_ssm_im, p_prompt, p_sample, norm_mix, w_in, sinks, ssm_lam_re, ssm_lam_im, ssm_log_step, ssm_b_re, ssm_b_im, ssm_c_re, ssm_c_im, ssm_d, w_glu, b_glu, norm_attn_out, norm_ssm_out, w_out, norm_ffn, w_router, b_router, w_up, b_up, w_down, b_down, norm_ple, w_ple_gate, w_ple_proj, norm_final):
    nst = SSM_G * SSM_P
    bias_np, sbias_np = _alibi_tables()
    bias = jnp.asarray(bias_np)
    sbias = jnp.asarray(sbias_np)

    lbr, lbi, bbr, bbi = _prep(ssm_lam_re[0], ssm_lam_im[0], ssm_log_step[0], ssm_b_re[0], ssm_b_im[0])
    bblk, lamr, lami, cblk = _s5_blocks(lbr, lbi, bbr, bbi, ssm_c_re[0], ssm_c_im[0])

    gmix = norm_mix[0].reshape(1, D_MODEL)
    win = w_in[0].astype(BF16)
    dskip = ssm_d[0].reshape(1, D_SSM)
    wglu = w_glu[0].astype(BF16)
    bglu = b_glu[0].reshape(1, D_SSM)
    ga = norm_attn_out[0].reshape(1, D_ATTN)
    gs = norm_ssm_out[0].reshape(1, D_SSM)
    wout = w_out[0].astype(BF16)
    sink = sinks[0]

    hbuf, k_last, v_last, re_p, im_p = _mixer_prompt(
        x_prompt, sink, gmix, win, bias, bblk, lamr, lami, cblk, dskip, wglu, bglu, ga, gs, wout)

    ck = cache_k_win[0].reshape(DEC_BATCH, WINDOW, D_KV)
    cv = cache_v_win[0].reshape(DEC_BATCH, WINDOW, D_KV)
    hmid, k_new, v_new, re_s, im_s = _mixer_sample(
        x_sample.reshape(DEC_BATCH, D_MODEL), gmix, win, ck, cv, sbias, sink.reshape(N_HEADS, 1), bblk, lamr, lami,
        cblk, dskip, wglu, bglu, ga, gs, wout, state_ssm_re[0].reshape(DEC_BATCH, nst),
        state_ssm_im[0].reshape(DEC_BATCH, nst), hbuf)

    hn, idx, gw, rank, counts = _router(hmid, norm_ffn[0].reshape(1, D_MODEL), w_router[0].T,
                                        b_router[0].reshape(N_EXPERTS, 1))

    tiles_per = (counts[:, 0] + (TM - 1)) // TM
    tile_end = jnp.cumsum(tiles_per)
    offsets = ((tile_end - tiles_per) * TM).astype(jnp.int32)
    tile_start = jnp.concatenate([jnp.zeros((1,), jnp.int32), tile_end.astype(jnp.int32)])

    pos = _place(offsets, idx, rank)
    xs = _sc_dispatch(hn, pos)
    ys = _experts(tile_start, xs, w_up[0], b_up[0], w_down[0], b_down[0])
    ygs = []
    for part in range(N_PARTS):
        r0 = part * PART_CHUNKS * ROWS
        r1 = T_PAD if part == N_PARTS - 1 else r0 + PART_CHUNKS * ROWS
        ygs.append(_sc_combine(ys, pos[:, r0:r1].reshape(TOP_K * (r1 - r0))).reshape(TOP_K, r1 - r0, D_PACK))

    gwt = gw.T
    gple = norm_ple[0].reshape(1, D_MODEL)
    wg = w_ple_gate[0].astype(BF16)
    wp = w_ple_proj[0].astype(BF16)
    gfin = norm_final.reshape(1, D_MODEL)
    y_prompt = None
    for part in range(N_PARTS):
        y_prompt = _final_prompt(part, hmid, ygs[part], gwt, p_prompt[0], gple, wg, wp, gfin, y_prompt)
    y_sample = _final_sample(hmid, ygs[-1], gwt, p_sample[0].reshape(DEC_BATCH, PLE_DIM), gple, wg, wp, gfin)

    k_win_s = jnp.concatenate([ck[:, 1:], k_new[:, None, :]], axis=1)
    v_win_s = jnp.concatenate([cv[:, 1:], v_new[:, None, :]], axis=1)
    kv5 = (1, BATCH, CHUNK, N_KV, HEAD_DIM)
    skv5 = (1, DEC_BATCH, WINDOW, N_KV, HEAD_DIM)
    return (y_prompt, y_sample.reshape(DEC_BATCH, 1, D_MODEL),
            k_last.reshape(kv5), v_last.reshape(kv5),
            re_p.reshape(1, BATCH, SSM_G, SSM_P), im_p.reshape(1, BATCH, SSM_G, SSM_P),
            k_win_s.reshape(skv5), v_win_s.reshape(skv5),
            re_s.reshape(1, DEC_BATCH, SSM_G, SSM_P), im_s.reshape(1, DEC_BATCH, SSM_G, SSM_P))
```

```python
import functools

import numpy as np
import jax
import jax.numpy as jnp
from jax import lax
from jax.experimental import pallas as pl
from jax.experimental.pallas import tpu as pltpu
from jax.experimental.pallas import tpu_sc as plsc

F32 = jnp.float32
BF16 = jnp.bfloat16

D_MODEL = 1024
BATCH = 8
SEQ = 2048
DEC_BATCH = 128
PAST_LEN = 16384
HEAD_DIM = 64
D_ATTN = 512
N_HEADS = 8
N_KV = 2
D_KV = N_KV * HEAD_DIM
WINDOW = 128
D_SSM = 512
SSM_H = 16
SSM_G = 32
SSM_P = 64
D_IN = D_ATTN + 2 * D_KV + D_SSM
N_EXPERTS = 32
TOP_K = 4
D_FF = 1024
SWIGLU_LIMIT = 7.0
SWIGLU_ALPHA = 1.702
PLE_DIM = 256
EPS = 1e-5
NEG = -1e30

CHUNK = 128
N_CHUNKS = SEQ // CHUNK
ROWS = BATCH * CHUNK
PITCH = CHUNK + 8
NAT_HEADS = (0, 2, 5, 7)
ROT_HEADS = (1, 3, 4, 6)
SUB_T = 32
SUB_ROWS = SUB_T * BATCH
N_SBLK = 4
SBLK = 512
T_REAL = BATCH * SEQ + DEC_BATCH
T_PAD = T_REAL + 128
SAMPLE_ROW0 = BATCH * SEQ
ROUTE_BLOCK = 512
TM = 512
N_TILES = (T_PAD * TOP_K + N_EXPERTS * (TM - 1)) // TM
P_ROWS = N_TILES * TM
SC_ROWS = 40
SC_ROWS_COMBINE = 32
N_PARTS = 4
PART_CHUNKS = N_CHUNKS // N_PARTS
D_PACK = D_MODEL // 2
VMEM_LIMIT = 56 * 1024 * 1024


def _rms(x, g):
    return x * lax.rsqrt(jnp.mean(x * x, axis=-1, keepdims=True) + EPS) * g


def _sigmoid(x):
    return 1.0 / (1.0 + jnp.exp(-x))


def _gelu_tanh(x):
    c = np.float32(np.sqrt(2.0 / np.pi))
    return 0.5 * x * (1.0 + jnp.tanh(c * (x + 0.044715 * (x * x * x))))


def _bdot(a, b):
    return jnp.dot(a.astype(BF16), b, preferred_element_type=F32)


def _pack_pairs(x):
    n = x.shape[1] // 2
    lo = lax.bitcast_convert_type(x[:, :n].astype(BF16).astype(F32), jnp.int32)
    hi = lax.bitcast_convert_type(x[:, n:].astype(BF16).astype(F32), jnp.int32)
    return lax.shift_right_logical(lo, 16) | hi


def _unpack_pairs(w):
    lo = lax.bitcast_convert_type(lax.shift_left(w, 16), F32)
    hi = lax.bitcast_convert_type(w & jnp.int32(-65536), F32)
    return jnp.concatenate([lo, hi], axis=1)


def _full(shape):
    n = len(shape)
    return pl.BlockSpec(shape, lambda *_: (0,) * n)


def _prep_kernel(lr_ref, li_ref, ls_ref, br_ref, bi_ref, lbr_ref, lbi_ref, bbr_ref, bbi_ref):
    lr = lr_ref[...]
    li = li_ref[...]
    step = jnp.exp(ls_ref[...])
    zr = lr * step
    zi = li * step
    mag = jnp.exp(zr)
    lbr = mag * jnp.cos(zi)
    lbi = mag * jnp.sin(zi)
    lbr_ref[...] = lbr
    lbi_ref[...] = lbi
    nr = lbr - 1.0
    den = lr * lr + li * li
    cr = (nr * lr + lbi * li) / den
    ci = (lbi * lr - nr * li) / den
    br = br_ref[...]
    bi = bi_ref[...]
    bbr_ref[...] = cr * br - ci * bi
    bbi_ref[...] = cr * bi + ci * br


def _prep(lam_re, lam_im, log_step, b_re, b_im):
    g, p, h = SSM_G, SSM_P, SSM_H
    out = pl.pallas_call(
        _prep_kernel,
        out_shape=[jax.ShapeDtypeStruct((g, 1, p), F32), jax.ShapeDtypeStruct((g, 1, p), F32),
                   jax.ShapeDtypeStruct((g, h, p), F32), jax.ShapeDtypeStruct((g, h, p), F32)],
        name="s5_prep",
    )(lam_re.reshape(g, 1, p), lam_im.reshape(g, 1, p), log_step.reshape(g, 1, 1),
      jnp.transpose(b_re, (0, 2, 1)), jnp.transpose(b_im, (0, 2, 1)))
    return out


def _s5_blocks(lbr, lbi, bbr, bbi, c_re, c_im):
    eye = jnp.eye(8, dtype=F32)
    shp = (N_SBLK, 8, SSM_H, SSM_P)
    b_r = jnp.einsum('jghp,gk->jghkp', bbr.reshape(shp), eye).reshape(N_SBLK, 128, SBLK)
    b_i = jnp.einsum('jghp,gk->jghkp', bbi.reshape(shp), eye).reshape(N_SBLK, 128, SBLK)
    bblk = jnp.concatenate([b_r, b_i], axis=-1).astype(BF16)
    c_r = jnp.einsum('jghp,gk->jgpkh', c_re.reshape(shp), eye).reshape(N_SBLK, SBLK, 128)
    c_i = jnp.einsum('jghp,gk->jgpkh', c_im.reshape(shp), eye).reshape(N_SBLK, SBLK, 128)
    cblk = jnp.concatenate([c_r, -c_i], axis=1).astype(BF16)
    return bblk, lbr.reshape(N_SBLK, 1, SBLK), lbi.reshape(N_SBLK, 1, SBLK), cblk


def _ssm_post(y_lin, u, dskip, wglu, bglu, gs):
    y = _gelu_tanh(y_lin + dskip * u)
    y = y * _sigmoid(_bdot(y, wglu) + bglu)
    return _rms(y, gs)


def _mixer_prompt_kernel(sinks_ref, x_ref, gmix_ref, win_ref, bias_ref, bblk_ref, lamr_ref, lami_ref,
                         cblk_ref, dskip_ref, wglu_ref, bglu_ref, ga_ref, gs_ref, wout_ref,
                         gffn_ref, wrt_ref, br_ref, tri_ref,
                         hmid_ref, klast_ref, vlast_ref, sre_ref, sim_ref, hn_ref, idx_ref, gw_ref, rank_ref, cnt_ref,
                         proj_s, u_s, kprev_s, kprevr_s, vprev_s, vprevr_s, attn_s, ssm_s, utb_s, bu_s, xs_s, ytb_s):
    c = pl.program_id(0)

    @pl.when(c == 0)
    def _():
        zkv = jnp.zeros(kprev_s.shape, BF16)
        kprev_s[...] = zkv
        kprevr_s[...] = zkv
        vprev_s[...] = zkv
        vprevr_s[...] = zkv
        sre_ref[...] = jnp.zeros(sre_ref.shape, F32)
        sim_ref[...] = jnp.zeros(sim_ref.shape, F32)
        cnt_ref[...] = jnp.zeros(cnt_ref.shape, jnp.int32)

    x = x_ref[...].reshape(ROWS, D_MODEL)
    proj = _bdot(_rms(x, gmix_ref[...]), win_ref[...])
    u0 = D_ATTN + 2 * D_KV
    proj_s[...] = proj[:, 0:u0]
    for jb in range(D_SSM // 128):
        for b in range(BATCH):
            u_s[jb, b * PITCH:b * PITCH + CHUNK, :] = proj[b * CHUNK:(b + 1) * CHUNK, u0 + 128 * jb:u0 + 128 * (jb + 1)]

    lo = lax.broadcasted_iota(jnp.int32, (CHUNK, 128), 1) < HEAD_DIM
    hi = jnp.logical_not(lo)
    table = jnp.minimum(c, 1)
    hrow = lax.broadcasted_iota(jnp.int32, (4 * CHUNK, 1), 0) // CHUNK

    def sink_col(heads):
        col = jnp.full((4 * CHUNK, 1), sinks_ref[heads[3]], F32)
        for n in (2, 1, 0):
            col = jnp.where(hrow == n, sinks_ref[heads[n]], col)
        return col

    sink_nat = sink_col(NAT_HEADS)
    sink_rot = sink_col(ROT_HEADS)

    def attend(q, k, v, bias, sink):
        s = lax.dot_general(q, k, (((1,), (1,)), ((), ())), preferred_element_type=F32) + bias
        m = jnp.maximum(jnp.max(s, axis=-1, keepdims=True), sink)
        p = jnp.exp(s - m)
        den = jnp.sum(p, axis=-1, keepdims=True) + jnp.exp(sink - m)
        return jnp.dot(p.astype(BF16), v, preferred_element_type=F32) / den

    def attn_body(b, carry):
        r0 = pl.multiple_of(b * CHUNK, CHUNK)
        rows = pl.ds(r0, CHUNK)
        kb = proj_s[rows, D_ATTN:D_ATTN + D_KV]
        vb = proj_s[rows, D_ATTN + D_KV:D_ATTN + 2 * D_KV]
        kb16 = kb.astype(BF16)
        vb16 = vb.astype(BF16)
        kbr16 = pltpu.roll(kb, HEAD_DIM, 1).astype(BF16)
        vbr16 = pltpu.roll(vb, HEAD_DIM, 1).astype(BF16)
        k_nat = jnp.concatenate([kprev_s[b], kb16], axis=0)
        k_rot = jnp.concatenate([kprevr_s[b], kbr16], axis=0)
        v_nat = jnp.concatenate([vprev_s[b], vb16], axis=0)
        v_rot = jnp.concatenate([vprevr_s[b], vbr16], axis=0)
        q2 = [proj_s[rows, 128 * jq:128 * (jq + 1)] * (HEAD_DIM ** -0.5) for jq in range(N_HEADS // 2)]
        q_nat = jnp.concatenate([jnp.where(lo if h % 2 == 0 else hi, q2[h // 2], 0.0) for h in NAT_HEADS],
                                axis=0).astype(BF16)
        q_rot = jnp.concatenate([jnp.where(lo if h % 2 == 0 else hi, q2[h // 2], 0.0) for h in ROT_HEADS],
                                axis=0).astype(BF16)
        o_nat = attend(q_nat, k_nat, v_nat, bias_ref[table, 0], sink_nat)
        o_rot = attend(q_rot, k_rot, v_rot, bias_ref[table, 1], sink_rot)
        for jq in range(N_HEADS // 2):
            blk = slice(CHUNK * jq, CHUNK * (jq + 1))
            even, odd = (o_nat, o_rot) if jq < 2 else (o_rot, o_nat)
            attn_s[rows, 128 * jq:128 * (jq + 1)] = jnp.where(lo, even[blk], odd[blk])
        kprev_s[b] = kb16
        kprevr_s[b] = kbr16
        vprev_s[b] = vb16
        vprevr_s[b] = vbr16
        return carry

    lax.fori_loop(0, BATCH, attn_body, 0)

    for sc in range(CHUNK // SUB_T):
        t0 = sc * SUB_T
        for i in range(SUB_T):
            for jb in range(D_SSM // 128):
                utb_s[i * BATCH:(i + 1) * BATCH, 128 * jb:128 * (jb + 1)] = (
                    u_s[jb, pl.ds(t0 + i, BATCH, stride=PITCH), :])
        u_tb = utb_s[...]
        for j in range(N_SBLK):
            bu_s[...] = _bdot(u_tb[:, 128 * j:128 * (j + 1)], bblk_ref[j])
            lr = jnp.broadcast_to(lamr_ref[j], (BATCH, SBLK))
            li = jnp.broadcast_to(lami_ref[j], (BATCH, SBLK))
            cols = slice(SBLK * j, SBLK * (j + 1))

            def step(i, carry):
                sr, si = carry
                r = pl.ds(pl.multiple_of(i * BATCH, BATCH), BATCH)
                nr = lr * sr - li * si + bu_s[r, 0:SBLK]
                ni = lr * si + li * sr + bu_s[r, SBLK:2 * SBLK]
                xs_s[r, 0:SBLK] = nr
                xs_s[r, SBLK:2 * SBLK] = ni
                return nr, ni

            sr, si = lax.fori_loop(0, SUB_T, step, (sre_ref[:, cols], sim_ref[:, cols]), unroll=True)
            sre_ref[:, cols] = sr
            sim_ref[:, cols] = si
            ytb_s[:, 128 * j:128 * (j + 1)] = _bdot(xs_s[...], cblk_ref[j])
        yn = _ssm_post(ytb_s[...], u_tb, dskip_ref[...], wglu_ref[...], bglu_ref[...], gs_ref[...])
        for i in range(SUB_T):
            for jb in range(D_SSM // 128):
                ssm_s[jb, pl.ds(t0 + i, BATCH, stride=PITCH), :] = (
                    yn[i * BATCH:(i + 1) * BATCH, 128 * jb:128 * (jb + 1)])

    an = _rms(attn_s[...], ga_ref[...])
    sn = jnp.concatenate(
        [jnp.concatenate([ssm_s[jb, b * PITCH:b * PITCH + CHUNK, :] for b in range(BATCH)], axis=0)
         for jb in range(D_SSM // 128)], axis=1)
    h = x + _bdot(an, wout_ref[0:D_ATTN, :]) + _bdot(sn, wout_ref[D_ATTN:2 * D_ATTN, :])
    hmid_ref[...] = h
    _route(h, gffn_ref, wrt_ref, br_ref, tri_ref, cnt_ref, hn_ref, idx_ref, gw_ref, rank_ref)

    @pl.when(c == N_CHUNKS - 1)
    def _():
        klast_ref[...] = proj_s[:, D_ATTN:D_ATTN + D_KV].reshape(BATCH, CHUNK, D_KV)
        vlast_ref[...] = proj_s[:, D_ATTN + D_KV:D_ATTN + 2 * D_KV].reshape(BATCH, CHUNK, D_KV)


def _mixer_prompt(x_prompt, sinks, gmix, win, bias, bblk, lamr, lami, cblk, dskip, wglu, bglu, ga, gs, wout,
                  gffn, wrt, br):
    smem = pl.BlockSpec(memory_space=pltpu.SMEM)
    in_specs = [
        smem,
        pl.BlockSpec((BATCH, CHUNK, D_MODEL), lambda c: (0, c, 0)),
        _full((1, D_MODEL)), _full((D_MODEL, D_IN)), _full((2, 2, 4 * CHUNK, 2 * CHUNK)),
        _full((N_SBLK, 128, 2 * SBLK)), _full((N_SBLK, 1, SBLK)), _full((N_SBLK, 1, SBLK)),
        _full((N_SBLK, 2 * SBLK, 128)), _full((1, D_SSM)), _full((D_SSM, D_SSM)), _full((1, D_SSM)),
        _full((1, D_ATTN)), _full((1, D_SSM)), _full((D_MODEL, D_MODEL)),
        _full((1, D_MODEL)), _full((N_EXPERTS, D_MODEL)), _full((N_EXPERTS, 1)), _full((ROUTE_BLOCK, ROUTE_BLOCK)),
    ]
    route_specs, route_shapes = _route_specs(ROWS, lambda c: c)
    out_shape = [
        jax.ShapeDtypeStruct((T_PAD, D_MODEL), F32),
        jax.ShapeDtypeStruct((BATCH, CHUNK, D_KV), F32),
        jax.ShapeDtypeStruct((BATCH, CHUNK, D_KV), F32),
        jax.ShapeDtypeStruct((BATCH, SSM_G * SSM_P), F32),
        jax.ShapeDtypeStruct((BATCH, SSM_G * SSM_P), F32),
    ] + route_shapes
    out_specs = [
        pl.BlockSpec((ROWS, D_MODEL), lambda c: (c, 0)),
        _full((BATCH, CHUNK, D_KV)), _full((BATCH, CHUNK, D_KV)),
        _full((BATCH, SSM_G * SSM_P)), _full((BATCH, SSM_G * SSM_P)),
    ] + route_specs
    kv_scr = pltpu.VMEM((BATCH, CHUNK, D_KV), BF16)
    scratch = [
        pltpu.VMEM((ROWS, D_ATTN + 2 * D_KV), F32), pltpu.VMEM((D_SSM // 128, BATCH * PITCH, 128), F32),
        kv_scr, kv_scr, kv_scr, kv_scr,
        pltpu.VMEM((ROWS, D_ATTN), F32), pltpu.VMEM((D_SSM // 128, BATCH * PITCH, 128), F32),
        pltpu.VMEM((SUB_ROWS, D_SSM), F32), pltpu.VMEM((SUB_ROWS, 2 * SBLK), F32),
        pltpu.VMEM((SUB_ROWS, 2 * SBLK), F32), pltpu.VMEM((SUB_ROWS, D_SSM), F32),
    ]
    return pl.pallas_call(
        _mixer_prompt_kernel, grid=(N_CHUNKS,), in_specs=in_specs, out_specs=out_specs, out_shape=out_shape,
        scratch_shapes=scratch, name="mixer_prompt",
        compiler_params=pltpu.CompilerParams(dimension_semantics=("arbitrary",), vmem_limit_bytes=VMEM_LIMIT),
    )(sinks, x_prompt, gmix, win, bias, bblk, lamr, lami, cblk, dskip, wglu, bglu, ga, gs, wout,
      gffn, wrt, br, _tri(ROUTE_BLOCK))


SGRP = 16
SPITCH = DEC_BATCH + 8
N_SGRP = DEC_BATCH // SGRP


def _mixer_sample_kernel(x_ref, gmix_ref, win_ref, ck_ref, cv_ref, sbias_ref, sinkc_ref, bblk_ref, lamr_ref,
                         lami_ref, cblk_ref, dskip_ref, wglu_ref, bglu_ref, ga_ref, gs_ref, wout_ref,
                         x0r_ref, x0i_ref, gffn_ref, wrt_ref, br_ref, tri_ref, cnt_in_ref,
                         hbuf_ref, hnbuf_ref, idxbuf_ref, gwbuf_ref, rankbuf_ref,
                         hmid_ref, knew_ref, vnew_ref, sre_ref, sim_ref, hn_ref, idx_ref, gw_ref, rank_ref, cnt_ref,
                         proj_s, qall_s, oall_s, kn8_s, vn8_s):
    del hbuf_ref, hnbuf_ref, idxbuf_ref, gwbuf_ref, rankbuf_ref
    g = pl.program_id(0)
    lo = lax.broadcasted_iota(jnp.int32, (DEC_BATCH, 128), 1) < HEAD_DIM

    @pl.when(g == 0)
    def _():
        proj = _bdot(_rms(x_ref[...], gmix_ref[...]), win_ref[...])
        proj_s[...] = proj
        for h in range(N_HEADS):
            jq, half, kv = h // 2, h % 2, h // 4
            q2 = proj[:, 128 * jq:128 * (jq + 1)] * (HEAD_DIM ** -0.5)
            if half != kv:
                q2 = pltpu.roll(q2, HEAD_DIM, 1)
            hr = slice(h * SPITCH, h * SPITCH + DEC_BATCH)
            qall_s[hr, :] = jnp.where(lo if kv == 0 else jnp.logical_not(lo), q2, 0.0)
            kn8_s[hr, :] = proj[:, D_ATTN:D_ATTN + D_KV]
            vn8_s[hr, :] = proj[:, D_ATTN + D_KV:D_ATTN + 2 * D_KV]

    def head_rows(ref):
        return jnp.stack([ref[pl.ds(g * SGRP + ii, N_HEADS, stride=SPITCH), :] for ii in range(SGRP)], axis=0)

    qe = head_rows(qall_s)
    knew = head_rows(kn8_s)
    vnew = head_rows(vn8_s)
    s = jnp.einsum('bhd,bkd->bhk', qe.astype(BF16), ck_ref[...].astype(BF16),
                   preferred_element_type=F32) + sbias_ref[...][None]
    s_new = jnp.sum(qe * knew, axis=-1, keepdims=True)
    sink = sinkc_ref[...][None]
    m = jnp.maximum(jnp.maximum(jnp.max(s, axis=-1, keepdims=True), s_new), sink)
    p = jnp.exp(s - m)
    p_new = jnp.exp(s_new - m)
    den = jnp.sum(p, axis=-1, keepdims=True) + p_new + jnp.exp(sink - m)
    o = (jnp.einsum('bhk,bkd->bhd', p.astype(BF16), cv_ref[...].astype(BF16),
                    preferred_element_type=F32) + p_new * vnew) / den
    for ii in range(SGRP):
        oall_s[pl.ds(g * SGRP + ii, N_HEADS, stride=SPITCH), :] = o[ii]

    @pl.when(g == N_SGRP - 1)
    def _():
        proj = proj_s[...]
        blocks = []
        for jq in range(N_HEADS // 2):
            oa = oall_s[(2 * jq) * SPITCH:(2 * jq) * SPITCH + DEC_BATCH, :]
            ob = oall_s[(2 * jq + 1) * SPITCH:(2 * jq + 1) * SPITCH + DEC_BATCH, :]
            if jq // 2 == 0:
                blocks.append(jnp.where(lo, oa, pltpu.roll(ob, HEAD_DIM, 1)))
            else:
                blocks.append(jnp.where(lo, pltpu.roll(oa, HEAD_DIM, 1), ob))
        attn = jnp.concatenate(blocks, axis=1)
        u = proj[:, D_ATTN + 2 * D_KV:]
        ys = []
        for j in range(N_SBLK):
            bu = _bdot(u[:, 128 * j:128 * (j + 1)], bblk_ref[j])
            lr = lamr_ref[j]
            li = lami_ref[j]
            cols = slice(SBLK * j, SBLK * (j + 1))
            sr = x0r_ref[:, cols]
            si = x0i_ref[:, cols]
            nr = lr * sr - li * si + bu[:, 0:SBLK]
            ni = lr * si + li * sr + bu[:, SBLK:2 * SBLK]
            sre_ref[:, cols] = nr
            sim_ref[:, cols] = ni
            ys.append(_bdot(jnp.concatenate([nr, ni], axis=1), cblk_ref[j]))
        sn = _ssm_post(jnp.concatenate(ys, axis=1), u, dskip_ref[...], wglu_ref[...], bglu_ref[...], gs_ref[...])
        an = _rms(attn, ga_ref[...])
        h = x_ref[...] + _bdot(an, wout_ref[0:D_ATTN, :]) + _bdot(sn, wout_ref[D_ATTN:2 * D_ATTN, :])
        h = jnp.concatenate([h, jnp.zeros((DEC_BATCH, D_MODEL), F32)], axis=0)
        hmid_ref[...] = h
        cnt_ref[...] = cnt_in_ref[...]
        _route(h, gffn_ref, wrt_ref, br_ref, tri_ref, cnt_ref, hn_ref, idx_ref, gw_ref, rank_ref)
        knew_ref[...] = proj[:, D_ATTN:D_ATTN + D_KV]
        vnew_ref[...] = proj[:, D_ATTN + D_KV:D_ATTN + 2 * D_KV]


def _mixer_sample(x_s, gmix, win, ck, cv, sbias, sinkc, bblk, lamr, lami, cblk, dskip, wglu, bglu, ga, gs, wout,
                  x0r, x0i, gffn, wrt, br, cnt_in, hbuf, hnbuf, idxbuf, gwbuf, rankbuf):
    nst = SSM_G * SSM_P
    in_specs = [
        _full((DEC_BATCH, D_MODEL)), _full((1, D_MODEL)), _full((D_MODEL, D_IN)),
        pl.BlockSpec((SGRP, WINDOW, D_KV), lambda g: (g, 0, 0)),
        pl.BlockSpec((SGRP, WINDOW, D_KV), lambda g: (g, 0, 0)),
        _full((N_HEADS, WINDOW)), _full((N_HEADS, 1)),
        _full((N_SBLK, 128, 2 * SBLK)), _full((N_SBLK, 1, SBLK)), _full((N_SBLK, 1, SBLK)),
        _full((N_SBLK, 2 * SBLK, 128)), _full((1, D_SSM)), _full((D_SSM, D_SSM)), _full((1, D_SSM)),
        _full((1, D_ATTN)), _full((1, D_SSM)), _full((D_MODEL, D_MODEL)),
        _full((DEC_BATCH, nst)), _full((DEC_BATCH, nst)),
        _full((1, D_MODEL)), _full((N_EXPERTS, D_MODEL)), _full((N_EXPERTS, 1)),
        _full((2 * DEC_BATCH, 2 * DEC_BATCH)), _full((N_EXPERTS, 1)),
    ] + [pl.BlockSpec(memory_space=pl.ANY)] * 5
    tail_block = SAMPLE_ROW0 // (2 * DEC_BATCH)
    route_specs, route_shapes = _route_specs(2 * DEC_BATCH, lambda g: tail_block)
    out_shape = [
        jax.ShapeDtypeStruct((T_PAD, D_MODEL), F32),
        jax.ShapeDtypeStruct((DEC_BATCH, D_KV), F32), jax.ShapeDtypeStruct((DEC_BATCH, D_KV), F32),
        jax.ShapeDtypeStruct((DEC_BATCH, nst), F32), jax.ShapeDtypeStruct((DEC_BATCH, nst), F32),
    ] + route_shapes
    out_specs = [
        pl.BlockSpec((2 * DEC_BATCH, D_MODEL), lambda g: (tail_block, 0)),
        _full((DEC_BATCH, D_KV)), _full((DEC_BATCH, D_KV)),
        _full((DEC_BATCH, nst)), _full((DEC_BATCH, nst)),
    ] + route_specs
    head_rows = pltpu.VMEM((N_HEADS * SPITCH, 128), F32)
    scratch = [pltpu.VMEM((DEC_BATCH, D_IN), F32), head_rows, head_rows, head_rows, head_rows]
    return pl.pallas_call(
        _mixer_sample_kernel, grid=(N_SGRP,), in_specs=in_specs, out_specs=out_specs, out_shape=out_shape,
        scratch_shapes=scratch, input_output_aliases={24: 0, 25: 5, 26: 6, 27: 7, 28: 8}, name="mixer_sample",
        compiler_params=pltpu.CompilerParams(dimension_semantics=("arbitrary",), vmem_limit_bytes=VMEM_LIMIT),
    )(x_s, gmix, win, ck, cv, sbias, sinkc, bblk, lamr, lami, cblk, dskip, wglu, bglu, ga, gs, wout, x0r, x0i,
      gffn, wrt, br, _tri(2 * DEC_BATCH), cnt_in, hbuf, hnbuf, idxbuf, gwbuf, rankbuf)


def _route(h, g_ref, wrt_ref, br_ref, tri_ref, cnt_ref, hn_ref, idx_ref, gw_ref, rank_ref):
    hn = _rms(h, g_ref[...])
    hn_ref[...] = _pack_pairs(hn)
    hn_hi = hn.astype(BF16)
    hn_lo = (hn - hn_hi.astype(F32)).astype(BF16)
    w = wrt_ref[...]
    w_hi = w.astype(BF16)
    w_lo = (w - w_hi.astype(F32)).astype(BF16)
    nt = (((1,), (1,)), ((), ()))
    logits = (lax.dot_general(w_hi, hn_hi, nt, preferred_element_type=F32)
              + lax.dot_general(w_lo, hn_hi, nt, preferred_element_type=F32)
              + lax.dot_general(w_hi, hn_lo, nt, preferred_element_type=F32)) + br_ref[...]
    eidx = lax.broadcasted_iota(jnp.int32, logits.shape, 0)
    vals, onehots = [], []
    l = logits
    for k in range(TOP_K):
        m = jnp.max(l, axis=0, keepdims=True)
        ik = jnp.min(jnp.where(l == m, eidx, N_EXPERTS), axis=0, keepdims=True)
        oh = eidx == ik
        idx_ref[k:k + 1, :] = ik
        vals.append(m)
        onehots.append(oh)
        l = jnp.where(oh, -jnp.inf, l)
    exps = [jnp.exp(v - vals[0]) for v in vals]
    den = exps[0] + exps[1] + exps[2] + exps[3]
    for k in range(TOP_K):
        gw_ref[k:k + 1, :] = exps[k] / den
    member = jnp.zeros(logits.shape, F32)
    for oh in onehots:
        member = member + jnp.where(oh, 1.0, 0.0)
    wblk = tri_ref.shape[0]
    base = cnt_ref[...].astype(F32)
    befores = []
    for cb in range(h.shape[0] // wblk):
        mblk = member[:, cb * wblk:(cb + 1) * wblk]
        befores.append(jnp.dot(mblk.astype(BF16), tri_ref[...], preferred_element_type=F32) + base)
        base = base + jnp.sum(mblk, axis=1, keepdims=True)
    before = jnp.concatenate(befores, axis=1)
    for k in range(TOP_K):
        rank_ref[k:k + 1, :] = jnp.sum(jnp.where(onehots[k], before, 0.0), axis=0, keepdims=True).astype(jnp.int32)
    cnt_ref[...] = base.astype(jnp.int32)


def _route_specs(rows, block):
    specs = [pl.BlockSpec((rows, D_PACK), lambda i: (block(i), 0)), pl.BlockSpec((TOP_K, rows), lambda i: (0, block(i))),
             pl.BlockSpec((TOP_K, rows), lambda i: (0, block(i))), pl.BlockSpec((TOP_K, rows), lambda i: (0, block(i))),
             _full((N_EXPERTS, 1))]
    shapes = [jax.ShapeDtypeStruct((T_PAD, D_PACK), jnp.int32), jax.ShapeDtypeStruct((TOP_K, T_PAD), jnp.int32),
              jax.ShapeDtypeStruct((TOP_K, T_PAD), F32), jax.ShapeDtypeStruct((TOP_K, T_PAD), jnp.int32),
              jax.ShapeDtypeStruct((N_EXPERTS, 1), jnp.int32)]
    return specs, shapes


def _tri(n):
    return jnp.asarray(np.triu(np.ones((n, n), np.float32), 1), BF16)


def _place_kernel(off_ref, idx_ref, rank_ref, pos_ref):
    idx = idx_ref[...]
    pos = rank_ref[...]
    for e in range(N_EXPERTS):
        pos = pos + jnp.where(idx == e, off_ref[e], 0)
    pos_ref[...] = pos


def _place(offsets, idx, rank):
    return pl.pallas_call(
        _place_kernel,
        in_specs=[pl.BlockSpec(memory_space=pltpu.SMEM), pl.BlockSpec(memory_space=pltpu.VMEM),
                  pl.BlockSpec(memory_space=pltpu.VMEM)],
        out_specs=pl.BlockSpec(memory_space=pltpu.VMEM),
        out_shape=jax.ShapeDtypeStruct((TOP_K, T_PAD), jnp.int32), name="place",
    )(offsets, idx, rank)


def _sc_mesh():
    return plsc.VectorSubcoreMesh(core_axis_name="core", subcore_axis_name="subcore")


def _sc_dispatch(rows, pos):
    n, d = rows.shape
    nblk = n // SC_ROWS
    pos_w = pos.reshape(TOP_K, nblk, SC_ROWS).transpose(1, 0, 2)

    @functools.partial(pl.kernel, out_type=jax.ShapeDtypeStruct((P_ROWS, d), rows.dtype), mesh=_sc_mesh(),
                       scratch_types=[], name="dispatch")
    def run(x_hbm, i_hbm, o_hbm):
        def body(x_vmem, i_vmem):
            for k in range(TOP_K):
                pltpu.sync_copy(x_vmem, o_hbm.at[i_vmem.at[0, k]])

        pltpu.emit_pipeline(
            body, grid=(nblk,),
            in_specs=[pl.BlockSpec((SC_ROWS, d), lambda i: (i, 0)),
                      pl.BlockSpec((1, TOP_K, SC_ROWS), lambda i: (i, 0, 0))],
            out_specs=[], core_axis_name=("core", "subcore"), dimension_semantics=(pltpu.PARALLEL,),
        )(x_hbm, i_hbm)

    return run(rows, pos_w)


def _sc_combine(rows, flat_pos):
    _, d = rows.shape
    n = flat_pos.shape[0]
    w = SC_ROWS_COMBINE
    nblk = n // w
    pos_w = flat_pos.reshape(nblk, 1, w)

    @functools.partial(pl.kernel, out_type=jax.ShapeDtypeStruct((n, d), rows.dtype), mesh=_sc_mesh(),
                       scratch_types=[], name="combine")
    def run(x_hbm, i_hbm, o_hbm):
        def body(i_vmem, o_vmem):
            pltpu.sync_copy(x_hbm.at[i_vmem.at[0, 0]], o_vmem)

        pltpu.emit_pipeline(
            body, grid=(nblk,),
            in_specs=[pl.BlockSpec((1, 1, w), lambda i: (i, 0, 0))],
            out_specs=[pl.BlockSpec((w, d), lambda i: (i, 0))],
            core_axis_name=("core", "subcore"), dimension_semantics=(pltpu.PARALLEL,),
        )(i_hbm, o_hbm)

    return run(rows, pos_w)


def _experts_kernel(ts_ref, xs_hbm, w1_hbm, b1_ref, w2_hbm, b2_ref, ys_hbm,
                    w1_st, w2_st, w1_s, w2_s, xbuf, ybuf, wsem, xsem, ysem):
    e = pl.program_id(0)
    n_valid = ts_ref[N_EXPERTS]

    def w_copies(ex, slot):
        return (pltpu.make_async_copy(w1_hbm.at[ex], w1_st.at[slot], wsem.at[0, slot]),
                pltpu.make_async_copy(w2_hbm.at[ex], w2_st.at[slot], wsem.at[1, slot]))

    def x_copy(t, slot):
        return pltpu.make_async_copy(xs_hbm.at[pl.ds(t * TM, TM)], xbuf.at[slot], xsem.at[slot])

    def y_copy(t, slot):
        return pltpu.make_async_copy(ybuf.at[slot], ys_hbm.at[pl.ds(t * TM, TM)], ysem.at[slot])

    @pl.when(e == 0)
    def _():
        for c in w_copies(0, 0):
            c.start()
        x_copy(0, 0).start()

    @pl.when(e + 1 < N_EXPERTS)
    def _():
        for c in w_copies(e + 1, (e + 1) % 2):
            c.start()

    wslot = e % 2
    for c in w_copies(e, wslot):
        c.wait()
    for r in range(4):
        rs = slice(256 * r, 256 * (r + 1))
        w1_s[rs, :] = w1_st[wslot, rs, :].astype(BF16)
        w2_s[rs, :] = w2_st[wslot, rs, :].astype(BF16)

    def tile(t, carry):
        slot = t % 2

        @pl.when(t + 1 < n_valid)
        def _():
            x_copy(t + 1, 1 - slot).start()

        x_copy(t, slot).wait()

        @pl.when(t >= 2)
        def _():
            y_copy(t - 2, slot).wait()

        hdn = _bdot(_unpack_pairs(xbuf[slot]), w1_s[...]) + b1_ref[0]
        gt = jnp.minimum(hdn[:, :D_FF], SWIGLU_LIMIT)
        up = jnp.clip(hdn[:, D_FF:], -SWIGLU_LIMIT, SWIGLU_LIMIT)
        act = (up + 1.0) * gt * _sigmoid(SWIGLU_ALPHA * gt)
        ybuf[slot] = _pack_pairs(_bdot(act, w2_s[...]) + b2_ref[0])
        y_copy(t, slot).start()
        return carry

    lax.fori_loop(ts_ref[e], ts_ref[e + 1], tile, 0)

    @pl.when(e == N_EXPERTS - 1)
    def _():
        @pl.when(n_valid >= 2)
        def _():
            y_copy(n_valid - 2, n_valid % 2).wait()

        y_copy(n_valid - 1, (n_valid - 1) % 2).wait()


def _experts(tile_start, xs, w_up, b_up, w_down, b_down):
    wsel = lambda e, ts: (e, 0, 0)
    hbm = pl.BlockSpec(memory_space=pl.ANY)
    grid_spec = pltpu.PrefetchScalarGridSpec(
        num_scalar_prefetch=1, grid=(N_EXPERTS,),
        in_specs=[hbm, hbm, pl.BlockSpec((1, 1, 2 * D_FF), wsel), hbm, pl.BlockSpec((1, 1, D_MODEL), wsel)],
        out_specs=hbm,
        scratch_shapes=[pltpu.VMEM((2, D_MODEL, 2 * D_FF), F32), pltpu.VMEM((2, D_FF, D_MODEL), F32),
                        pltpu.VMEM((D_MODEL, 2 * D_FF), BF16), pltpu.VMEM((D_FF, D_MODEL), BF16),
                        pltpu.VMEM((2, TM, D_PACK), jnp.int32), pltpu.VMEM((2, TM, D_PACK), jnp.int32),
                        pltpu.SemaphoreType.DMA((2, 2)), pltpu.SemaphoreType.DMA((2,)),
                        pltpu.SemaphoreType.DMA((2,))],
    )
    return pl.pallas_call(
        _experts_kernel, grid_spec=grid_spec, out_shape=jax.ShapeDtypeStruct((P_ROWS, D_PACK), jnp.int32),
        name="experts",
        compiler_params=pltpu.CompilerParams(dimension_semantics=("arbitrary",), vmem_limit_bytes=VMEM_LIMIT),
    )(tile_start, xs, w_up, b_up.reshape(N_EXPERTS, 1, 2 * D_FF), w_down, b_down.reshape(N_EXPERTS, 1, D_MODEL))


def _final_kernel(h_ref, yg_ref, gw_ref, p_ref, gple_ref, wg_ref, wp_ref, gfin_ref, *rest):
    out_ref = rest[-1]
    rows = h_ref.shape[0]
    gw = gw_ref[...]
    h = h_ref[...]
    for k in range(TOP_K):
        h = h + gw[:, k:k + 1] * _unpack_pairs(yg_ref[k])
    gate = _sigmoid(_bdot(_rms(h, gple_ref[...]), wg_ref[...]))
    h = h + gate * _bdot(p_ref[...].reshape(rows, PLE_DIM), wp_ref[...])
    out_ref[...] = _rms(h, gfin_ref[...]).reshape(out_ref.shape)


def _final_prompt(part, hmid, yg, gwt, p_prompt, gple, wg, wp, gfin, y_prev):
    nb = 4
    rows = nb * CHUNK
    nbh = BATCH // nb
    c0 = part * PART_CHUNKS
    rb = lambda c, b: ((c0 + c) * nbh + b, 0)
    in_specs = [pl.BlockSpec((rows, D_MODEL), rb),
                pl.BlockSpec((TOP_K, rows, D_PACK), lambda c, b: (0, c * nbh + b, 0)),
                pl.BlockSpec((rows, TOP_K), rb),
                pl.BlockSpec((nb, CHUNK, PLE_DIM), lambda c, b: (b, c0 + c, 0)),
                _full((1, D_MODEL)), _full((D_MODEL, D_MODEL)), _full((PLE_DIM, D_MODEL)), _full((1, D_MODEL))]
    args = [hmid, yg, gwt, p_prompt, gple, wg, wp, gfin]
    aliases = {}
    if y_prev is not None:
        in_specs.append(pl.BlockSpec(memory_space=pl.ANY))
        args.append(y_prev)
        aliases = {len(args) - 1: 0}
    return pl.pallas_call(
        _final_kernel, grid=(PART_CHUNKS, nbh), in_specs=in_specs,
        out_specs=pl.BlockSpec((nb, CHUNK, D_MODEL), lambda c, b: (b, c0 + c, 0)),
        out_shape=jax.ShapeDtypeStruct((BATCH, SEQ, D_MODEL), F32), name="final_prompt",
        input_output_aliases=aliases,
        compiler_params=pltpu.CompilerParams(dimension_semantics=("arbitrary", "arbitrary"),
                                             vmem_limit_bytes=VMEM_LIMIT),
    )(*args)


def _final_sample(hmid, yg, gwt, p_sample, gple, wg, wp, gfin):
    blk = SAMPLE_ROW0 // DEC_BATCH
    return pl.pallas_call(
        _final_kernel, grid=(1,),
        in_specs=[pl.BlockSpec((DEC_BATCH, D_MODEL), lambda i: (blk, 0)),
                  pl.BlockSpec((TOP_K, DEC_BATCH, D_PACK), lambda i: (0, PART_CHUNKS * ROWS // DEC_BATCH, 0)),
                  pl.BlockSpec((DEC_BATCH, TOP_K), lambda i: (blk, 0)),
                  _full((DEC_BATCH, PLE_DIM)),
                  _full((1, D_MODEL)), _full((D_MODEL, D_MODEL)), _full((PLE_DIM, D_MODEL)), _full((1, D_MODEL))],
        out_specs=_full((DEC_BATCH, D_MODEL)),
        out_shape=jax.ShapeDtypeStruct((DEC_BATCH, D_MODEL), F32), name="final_sample",
        compiler_params=pltpu.CompilerParams(dimension_semantics=("arbitrary",), vmem_limit_bytes=VMEM_LIMIT),
    )(hmid, yg, gwt, p_sample, gple, wg, wp, gfin)


def _alibi_tables():
    slopes = 2.0 ** (-8.0 * (np.arange(N_HEADS, dtype=np.float64) + 1.0) / N_HEADS)
    i = np.arange(CHUNK)[:, None]
    j = np.arange(2 * CHUNK)[None, :]
    dist = i + CHUNK - j
    valid = (dist >= 0) & (dist <= WINDOW)
    tabs = []
    for has_prev in (False, True):
        ok = valid & ((j >= CHUNK) | has_prev)
        tabs.append(np.where(ok[None], -slopes[:, None, None] * dist[None], NEG))
    prompt = np.stack([np.stack([np.concatenate([t[h] for h in grp], axis=0) for grp in (NAT_HEADS, ROT_HEADS)])
                       for t in tabs]).astype(np.float32)
    wb = min(WINDOW, PAST_LEN)
    sample = (-slopes[:, None] * (wb - np.arange(wb))[None, :]).astype(np.float32)
    return prompt, sample


def kernel(x_prompt, x_sample, cache_k_win, cache_v_win, state_ssm_re, state_ssm_im, p_prompt, p_sample, norm_mix, w_in, sinks, ssm_lam_re, ssm_lam_im, ssm_log_step, ssm_b_re, ssm_b_im, ssm_c_re, ssm_c_im, ssm_d, w_glu, b_glu, norm_attn_out, norm_ssm_out, w_out, norm_ffn, w_router, b_router, w_up, b_up, w_down, b_down, norm_ple, w_ple_gate, w_ple_proj, norm_final):
    nst = SSM_G * SSM_P
    bias_np, sbias_np = _alibi_tables()
    bias = jnp.asarray(bias_np)
    sbias = jnp.asarray(sbias_np)

    lbr, lbi, bbr, bbi = _prep(ssm_lam_re[0], ssm_lam_im[0], ssm_log_step[0], ssm_b_re[0], ssm_b_im[0])
    bblk, lamr, lami, cblk = _s5_blocks(lbr, lbi, bbr, bbi, ssm_c_re[0], ssm_c_im[0])

    gmix = norm_mix[0].reshape(1, D_MODEL)
    win = w_in[0].astype(BF16)
    dskip = ssm_d[0].reshape(1, D_SSM)
    wglu = w_glu[0].astype(BF16)
    bglu = b_glu[0].reshape(1, D_SSM)
    ga = norm_attn_out[0].reshape(1, D_ATTN)
    gs = norm_ssm_out[0].reshape(1, D_SSM)
    wout = w_out[0].astype(BF16)
    sink = sinks[0]

    gffn = norm_ffn[0].reshape(1, D_MODEL)
    wrt = w_router[0].T
    br = b_router[0].reshape(N_EXPERTS, 1)
    hbuf, k_last, v_last, re_p, im_p, hnbuf, idxbuf, gwbuf, rankbuf, cnt_p = _mixer_prompt(
        x_prompt, sink, gmix, win, bias, bblk, lamr, lami, cblk, dskip, wglu, bglu, ga, gs, wout, gffn, wrt, br)

    ck = cache_k_win[0].reshape(DEC_BATCH, WINDOW, D_KV)
    cv = cache_v_win[0].reshape(DEC_BATCH, WINDOW, D_KV)
    hmid, k_new, v_new, re_s, im_s, hn, idx, gw, rank, counts = _mixer_sample(
        x_sample.reshape(DEC_BATCH, D_MODEL), gmix, win, ck, cv, sbias, sink.reshape(N_HEADS, 1), bblk, lamr, lami,
        cblk, dskip, wglu, bglu, ga, gs, wout, state_ssm_re[0].reshape(DEC_BATCH, nst),
        state_ssm_im[0].reshape(DEC_BATCH, nst), gffn, wrt, br, cnt_p, hbuf, hnbuf, idxbuf, gwbuf, rankbuf)

    tiles_per = (counts[:, 0] + (TM - 1)) // TM
    tile_end = jnp.cumsum(tiles_per)
    offsets = ((tile_end - tiles_per) * TM).astype(jnp.int32)
    tile_start = jnp.concatenate([jnp.zeros((1,), jnp.int32), tile_end.astype(jnp.int32)])

    pos = _place(offsets, idx, rank)
    xs = _sc_dispatch(hn, pos)
    ys = _experts(tile_start, xs, w_up[0], b_up[0], w_down[0], b_down[0])
    ygs = []
    for part in range(N_PARTS):
        r0 = part * PART_CHUNKS * ROWS
        r1 = T_PAD if part == N_PARTS - 1 else r0 + PART_CHUNKS * ROWS
        ygs.append(_sc_combine(ys, pos[:, r0:r1].reshape(TOP_K * (r1 - r0))).reshape(TOP_K, r1 - r0, D_PACK))

    gwt = gw.T
    gple = norm_ple[0].reshape(1, D_MODEL)
    wg = w_ple_gate[0].astype(BF16)
    wp = w_ple_proj[0].astype(BF16)
    gfin = norm_final.reshape(1, D_MODEL)
    y_prompt = None
    for part in range(N_PARTS):
        y_prompt = _final_prompt(part, hmid, ygs[part], gwt, p_prompt[0], gple, wg, wp, gfin, y_prompt)
    y_sample = _final_sample(hmid, ygs[-1], gwt, p_sample[0].reshape(DEC_BATCH, PLE_DIM), gple, wg, wp, gfin)

    k_win_s = jnp.concatenate([ck[:, 1:], k_new[:, None, :]], axis=1)
    v_win_s = jnp.concatenate([cv[:, 1:], v_new[:, None, :]], axis=1)
    kv5 = (1, BATCH, CHUNK, N_KV, HEAD_DIM)
    skv5 = (1, DEC_BATCH, WINDOW, N_KV, HEAD_DIM)
    return (y_prompt, y_sample.reshape(DEC_BATCH, 1, D_MODEL),
            k_last.reshape(kv5), v_last.reshape(kv5),
            re_p.reshape(1, BATCH, SSM_G, SSM_P), im_p.reshape(1, BATCH, SSM_G, SSM_P),
            k_win_s.reshape(skv5), v_win_s.reshape(skv5),
            re_s.reshape(1, DEC_BATCH, SSM_G, SSM_P), im_s.reshape(1, DEC_BATCH, SSM_G, SSM_P))
```

```python
import functools

import numpy as np
import jax
import jax.numpy as jnp
from jax import lax
from jax.experimental import pallas as pl
from jax.experimental.pallas import tpu as pltpu
from jax.experimental.pallas import tpu_sc as plsc

F32 = jnp.float32
BF16 = jnp.bfloat16

D_MODEL = 1024
BATCH = 8
SEQ = 2048
DEC_BATCH = 128
PAST_LEN = 16384
HEAD_DIM = 64
D_ATTN = 512
N_HEADS = 8
N_KV = 2
D_KV = N_KV * HEAD_DIM
WINDOW = 128
D_SSM = 512
SSM_H = 16
SSM_G = 32
SSM_P = 64
D_IN = D_ATTN + 2 * D_KV + D_SSM
N_EXPERTS = 32
TOP_K = 4
D_FF = 1024
SWIGLU_LIMIT = 7.0
SWIGLU_ALPHA = 1.702
PLE_DIM = 256
EPS = 1e-5
NEG = -1e30

CHUNK = 128
N_CHUNKS = SEQ // CHUNK
ROWS = BATCH * CHUNK
PITCH = CHUNK + 8
NAT_HEADS = (0, 2, 5, 7)
ROT_HEADS = (1, 3, 4, 6)
SUB_T = 32
SUB_ROWS = SUB_T * BATCH
N_SBLK = 4
SBLK = 512
T_REAL = BATCH * SEQ + DEC_BATCH
T_PAD = T_REAL + 128
SAMPLE_ROW0 = BATCH * SEQ
ROUTE_BLOCK = 512
TM = 512
TM_HALF = 256
P_ROWS = (T_PAD * TOP_K + N_EXPERTS * (TM_HALF - 1)) // TM_HALF * TM_HALF
MAX_TILES = P_ROWS // TM + N_EXPERTS + 1
SC_ROWS = 40
SC_ROWS_COMBINE = 32
N_PARTS = 4
PART_CHUNKS = N_CHUNKS // N_PARTS
D_PACK = D_MODEL // 2
VMEM_LIMIT = 56 * 1024 * 1024


def _rms(x, g):
    return x * lax.rsqrt(jnp.mean(x * x, axis=-1, keepdims=True) + EPS) * g


def _sigmoid(x):
    return 1.0 / (1.0 + jnp.exp(-x))


def _gelu_tanh(x):
    c = np.float32(np.sqrt(2.0 / np.pi))
    return 0.5 * x * (1.0 + jnp.tanh(c * (x + 0.044715 * (x * x * x))))


def _bdot(a, b):
    return jnp.dot(a.astype(BF16), b, preferred_element_type=F32)


def _pack_pairs(x):
    n = x.shape[1] // 2
    lo = lax.bitcast_convert_type(x[:, :n].astype(BF16).astype(F32), jnp.int32)
    hi = lax.bitcast_convert_type(x[:, n:].astype(BF16).astype(F32), jnp.int32)
    return lax.shift_right_logical(lo, 16) | hi


def _unpack_pairs(w):
    lo = lax.bitcast_convert_type(lax.shift_left(w, 16), F32)
    hi = lax.bitcast_convert_type(w & jnp.int32(-65536), F32)
    return jnp.concatenate([lo, hi], axis=1)


def _full(shape):
    n = len(shape)
    return pl.BlockSpec(shape, lambda *_: (0,) * n)


def _prep_kernel(lr_ref, li_ref, ls_ref, br_ref, bi_ref, lbr_ref, lbi_ref, bbr_ref, bbi_ref):
    lr = lr_ref[...]
    li = li_ref[...]
    step = jnp.exp(ls_ref[...])
    zr = lr * step
    zi = li * step
    mag = jnp.exp(zr)
    lbr = mag * jnp.cos(zi)
    lbi = mag * jnp.sin(zi)
    lbr_ref[...] = lbr
    lbi_ref[...] = lbi
    nr = lbr - 1.0
    den = lr * lr + li * li
    cr = (nr * lr + lbi * li) / den
    ci = (lbi * lr - nr * li) / den
    br = br_ref[...]
    bi = bi_ref[...]
    bbr_ref[...] = cr * br - ci * bi
    bbi_ref[...] = cr * bi + ci * br


def _prep(lam_re, lam_im, log_step, b_re, b_im):
    g, p, h = SSM_G, SSM_P, SSM_H
    out = pl.pallas_call(
        _prep_kernel,
        out_shape=[jax.ShapeDtypeStruct((g, 1, p), F32), jax.ShapeDtypeStruct((g, 1, p), F32),
                   jax.ShapeDtypeStruct((g, h, p), F32), jax.ShapeDtypeStruct((g, h, p), F32)],
        name="s5_prep",
    )(lam_re.reshape(g, 1, p), lam_im.reshape(g, 1, p), log_step.reshape(g, 1, 1),
      jnp.transpose(b_re, (0, 2, 1)), jnp.transpose(b_im, (0, 2, 1)))
    return out


def _s5_blocks(lbr, lbi, bbr, bbi, c_re, c_im):
    eye = jnp.eye(8, dtype=F32)
    shp = (N_SBLK, 8, SSM_H, SSM_P)
    b_r = jnp.einsum('jghp,gk->jghkp', bbr.reshape(shp), eye).reshape(N_SBLK, 128, SBLK)
    b_i = jnp.einsum('jghp,gk->jghkp', bbi.reshape(shp), eye).reshape(N_SBLK, 128, SBLK)
    bblk = jnp.concatenate([b_r, b_i], axis=-1).astype(BF16)
    c_r = jnp.einsum('jghp,gk->jgpkh', c_re.reshape(shp), eye).reshape(N_SBLK, SBLK, 128)
    c_i = jnp.einsum('jghp,gk->jgpkh', c_im.reshape(shp), eye).reshape(N_SBLK, SBLK, 128)
    cblk = jnp.concatenate([c_r, -c_i], axis=1).astype(BF16)
    return bblk, lbr.reshape(N_SBLK, 1, SBLK), lbi.reshape(N_SBLK, 1, SBLK), cblk


def _ssm_post(y_lin, u, dskip, wglu, bglu, gs):
    y = _gelu_tanh(y_lin + dskip * u)
    y = y * _sigmoid(_bdot(y, wglu) + bglu)
    return _rms(y, gs)


def _mixer_prompt_kernel(sinks_ref, x_ref, gmix_ref, win_ref, bias_ref, bblk_ref, lamr_ref, lami_ref,
                         cblk_ref, dskip_ref, wglu_ref, bglu_ref, ga_ref, gs_ref, wout_ref,
                         gffn_ref, wrt_ref, br_ref, tri_ref,
                         hmid_ref, klast_ref, vlast_ref, sre_ref, sim_ref, hn_ref, idx_ref, gw_ref, rank_ref, cnt_ref,
                         proj_s, u_s, kprev_s, kprevr_s, vprev_s, vprevr_s, attn_s, ssm_s, utb_s, bu_s, xs_s, ytb_s):
    c = pl.program_id(0)

    @pl.when(c == 0)
    def _():
        zkv = jnp.zeros(kprev_s.shape, BF16)
        kprev_s[...] = zkv
        kprevr_s[...] = zkv
        vprev_s[...] = zkv
        vprevr_s[...] = zkv
        sre_ref[...] = jnp.zeros(sre_ref.shape, F32)
        sim_ref[...] = jnp.zeros(sim_ref.shape, F32)
        cnt_ref[...] = jnp.zeros(cnt_ref.shape, jnp.int32)

    x = x_ref[...].reshape(ROWS, D_MODEL)
    proj = _bdot(_rms(x, gmix_ref[...]), win_ref[...])
    u0 = D_ATTN + 2 * D_KV
    proj_s[...] = proj[:, 0:u0]
    for jb in range(D_SSM // 128):
        for b in range(BATCH):
            u_s[jb, b * PITCH:b * PITCH + CHUNK, :] = proj[b * CHUNK:(b + 1) * CHUNK, u0 + 128 * jb:u0 + 128 * (jb + 1)]

    lo = lax.broadcasted_iota(jnp.int32, (CHUNK, 128), 1) < HEAD_DIM
    hi = jnp.logical_not(lo)
    table = jnp.minimum(c, 1)
    hrow = lax.broadcasted_iota(jnp.int32, (4 * CHUNK, 1), 0) // CHUNK

    def sink_col(heads):
        col = jnp.full((4 * CHUNK, 1), sinks_ref[heads[3]], F32)
        for n in (2, 1, 0):
            col = jnp.where(hrow == n, sinks_ref[heads[n]], col)
        return col

    sink_nat = sink_col(NAT_HEADS)
    sink_rot = sink_col(ROT_HEADS)

    def attend(q, k, v, bias, sink):
        s = lax.dot_general(q, k, (((1,), (1,)), ((), ())), preferred_element_type=F32) + bias
        m = jnp.maximum(jnp.max(s, axis=-1, keepdims=True), sink)
        p = jnp.exp(s - m)
        den = jnp.sum(p, axis=-1, keepdims=True) + jnp.exp(sink - m)
        return jnp.dot(p.astype(BF16), v, preferred_element_type=F32) / den

    def attn_body(b, carry):
        r0 = pl.multiple_of(b * CHUNK, CHUNK)
        rows = pl.ds(r0, CHUNK)
        kb = proj_s[rows, D_ATTN:D_ATTN + D_KV]
        vb = proj_s[rows, D_ATTN + D_KV:D_ATTN + 2 * D_KV]
        kb16 = kb.astype(BF16)
        vb16 = vb.astype(BF16)
        kbr16 = pltpu.roll(kb, HEAD_DIM, 1).astype(BF16)
        vbr16 = pltpu.roll(vb, HEAD_DIM, 1).astype(BF16)
        k_nat = jnp.concatenate([kprev_s[b], kb16], axis=0)
        k_rot = jnp.concatenate([kprevr_s[b], kbr16], axis=0)
        v_nat = jnp.concatenate([vprev_s[b], vb16], axis=0)
        v_rot = jnp.concatenate([vprevr_s[b], vbr16], axis=0)
        q2 = [proj_s[rows, 128 * jq:128 * (jq + 1)] * (HEAD_DIM ** -0.5) for jq in range(N_HEADS // 2)]
        q_nat = jnp.concatenate([jnp.where(lo if h % 2 == 0 else hi, q2[h // 2], 0.0) for h in NAT_HEADS],
                                axis=0).astype(BF16)
        q_rot = jnp.concatenate([jnp.where(lo if h % 2 == 0 else hi, q2[h // 2], 0.0) for h in ROT_HEADS],
                                axis=0).astype(BF16)
        o_nat = attend(q_nat, k_nat, v_nat, bias_ref[table, 0], sink_nat)
        o_rot = attend(q_rot, k_rot, v_rot, bias_ref[table, 1], sink_rot)
        for jq in range(N_HEADS // 2):
            blk = slice(CHUNK * jq, CHUNK * (jq + 1))
            even, odd = (o_nat, o_rot) if jq < 2 else (o_rot, o_nat)
            attn_s[rows, 128 * jq:128 * (jq + 1)] = jnp.where(lo, even[blk], odd[blk])
        kprev_s[b] = kb16
        kprevr_s[b] = kbr16
        vprev_s[b] = vb16
        vprevr_s[b] = vbr16
        return carry

    lax.fori_loop(0, BATCH, attn_body, 0)

    for sc in range(CHUNK // SUB_T):
        t0 = sc * SUB_T
        for i in range(SUB_T):
            for jb in range(D_SSM // 128):
                utb_s[i * BATCH:(i + 1) * BATCH, 128 * jb:128 * (jb + 1)] = (
                    u_s[jb, pl.ds(t0 + i, BATCH, stride=PITCH), :])
        u_tb = utb_s[...]
        for j in range(N_SBLK):
            bu_s[...] = _bdot(u_tb[:, 128 * j:128 * (j + 1)], bblk_ref[j])
            lr = jnp.broadcast_to(lamr_ref[j], (BATCH, SBLK))
            li = jnp.broadcast_to(lami_ref[j], (BATCH, SBLK))
            cols = slice(SBLK * j, SBLK * (j + 1))

            def step(i, carry):
                sr, si = carry
                r = pl.ds(pl.multiple_of(i * BATCH, BATCH), BATCH)
                nr = lr * sr - li * si + bu_s[r, 0:SBLK]
                ni = lr * si + li * sr + bu_s[r, SBLK:2 * SBLK]
                xs_s[r, 0:SBLK] = nr
                xs_s[r, SBLK:2 * SBLK] = ni
                return nr, ni

            sr, si = lax.fori_loop(0, SUB_T, step, (sre_ref[:, cols], sim_ref[:, cols]), unroll=True)
            sre_ref[:, cols] = sr
            sim_ref[:, cols] = si
            ytb_s[:, 128 * j:128 * (j + 1)] = _bdot(xs_s[...], cblk_ref[j])
        yn = _ssm_post(ytb_s[...], u_tb, dskip_ref[...], wglu_ref[...], bglu_ref[...], gs_ref[...])
        for i in range(SUB_T):
            for jb in range(D_SSM // 128):
                ssm_s[jb, pl.ds(t0 + i, BATCH, stride=PITCH), :] = (
                    yn[i * BATCH:(i + 1) * BATCH, 128 * jb:128 * (jb + 1)])

    an = _rms(attn_s[...], ga_ref[...])
    sn = jnp.concatenate(
        [jnp.concatenate([ssm_s[jb, b * PITCH:b * PITCH + CHUNK, :] for b in range(BATCH)], axis=0)
         for jb in range(D_SSM // 128)], axis=1)
    h = x + _bdot(an, wout_ref[0:D_ATTN, :]) + _bdot(sn, wout_ref[D_ATTN:2 * D_ATTN, :])
    hmid_ref[...] = h
    _route(h, gffn_ref, wrt_ref, br_ref, tri_ref, cnt_ref, hn_ref, idx_ref, gw_ref, rank_ref)

    @pl.when(c == N_CHUNKS - 1)
    def _():
        klast_ref[...] = proj_s[:, D_ATTN:D_ATTN + D_KV].reshape(BATCH, CHUNK, D_KV)
        vlast_ref[...] = proj_s[:, D_ATTN + D_KV:D_ATTN + 2 * D_KV].reshape(BATCH, CHUNK, D_KV)


def _mixer_prompt(x_prompt, sinks, gmix, win, bias, bblk, lamr, lami, cblk, dskip, wglu, bglu, ga, gs, wout,
                  gffn, wrt, br):
    smem = pl.BlockSpec(memory_space=pltpu.SMEM)
    in_specs = [
        smem,
        pl.BlockSpec((BATCH, CHUNK, D_MODEL), lambda c: (0, c, 0)),
        _full((1, D_MODEL)), _full((D_MODEL, D_IN)), _full((2, 2, 4 * CHUNK, 2 * CHUNK)),
        _full((N_SBLK, 128, 2 * SBLK)), _full((N_SBLK, 1, SBLK)), _full((N_SBLK, 1, SBLK)),
        _full((N_SBLK, 2 * SBLK, 128)), _full((1, D_SSM)), _full((D_SSM, D_SSM)), _full((1, D_SSM)),
        _full((1, D_ATTN)), _full((1, D_SSM)), _full((D_MODEL, D_MODEL)),
        _full((1, D_MODEL)), _full((N_EXPERTS, D_MODEL)), _full((N_EXPERTS, 1)), _full((ROUTE_BLOCK, ROUTE_BLOCK)),
    ]
    route_specs, route_shapes = _route_specs(ROWS, lambda c: c)
    out_shape = [
        jax.ShapeDtypeStruct((T_PAD, D_MODEL), F32),
        jax.ShapeDtypeStruct((BATCH, CHUNK, D_KV), F32),
        jax.ShapeDtypeStruct((BATCH, CHUNK, D_KV), F32),
        jax.ShapeDtypeStruct((BATCH, SSM_G * SSM_P), F32),
        jax.ShapeDtypeStruct((BATCH, SSM_G * SSM_P), F32),
    ] + route_shapes
    out_specs = [
        pl.BlockSpec((ROWS, D_MODEL), lambda c: (c, 0)),
        _full((BATCH, CHUNK, D_KV)), _full((BATCH, CHUNK, D_KV)),
        _full((BATCH, SSM_G * SSM_P)), _full((BATCH, SSM_G * SSM_P)),
    ] + route_specs
    kv_scr = pltpu.VMEM((BATCH, CHUNK, D_KV), BF16)
    scratch = [
        pltpu.VMEM((ROWS, D_ATTN + 2 * D_KV), F32), pltpu.VMEM((D_SSM // 128, BATCH * PITCH, 128), F32),
        kv_scr, kv_scr, kv_scr, kv_scr,
        pltpu.VMEM((ROWS, D_ATTN), F32), pltpu.VMEM((D_SSM // 128, BATCH * PITCH, 128), F32),
        pltpu.VMEM((SUB_ROWS, D_SSM), F32), pltpu.VMEM((SUB_ROWS, 2 * SBLK), F32),
        pltpu.VMEM((SUB_ROWS, 2 * SBLK), F32), pltpu.VMEM((SUB_ROWS, D_SSM), F32),
    ]
    return pl.pallas_call(
        _mixer_prompt_kernel, grid=(N_CHUNKS,), in_specs=in_specs, out_specs=out_specs, out_shape=out_shape,
        scratch_shapes=scratch, name="mixer_prompt",
        compiler_params=pltpu.CompilerParams(dimension_semantics=("arbitrary",), vmem_limit_bytes=VMEM_LIMIT),
    )(sinks, x_prompt, gmix, win, bias, bblk, lamr, lami, cblk, dskip, wglu, bglu, ga, gs, wout,
      gffn, wrt, br, _tri(ROUTE_BLOCK))


SGRP = 16
SPITCH = DEC_BATCH + 8
N_SGRP = DEC_BATCH // SGRP


def _mixer_sample_kernel(x_ref, gmix_ref, win_ref, ck_ref, cv_ref, sbias_ref, sinkc_ref, bblk_ref, lamr_ref,
                         lami_ref, cblk_ref, dskip_ref, wglu_ref, bglu_ref, ga_ref, gs_ref, wout_ref,
                         x0r_ref, x0i_ref, gffn_ref, wrt_ref, br_ref, tri_ref, cnt_in_ref,
                         hbuf_ref, hnbuf_ref, idxbuf_ref, gwbuf_ref, rankbuf_ref,
                         hmid_ref, knew_ref, vnew_ref, sre_ref, sim_ref, hn_ref, idx_ref, gw_ref, rank_ref, cnt_ref,
                         proj_s, qall_s, oall_s, kn8_s, vn8_s):
    del hbuf_ref, hnbuf_ref, idxbuf_ref, gwbuf_ref, rankbuf_ref
    g = pl.program_id(0)
    lo = lax.broadcasted_iota(jnp.int32, (DEC_BATCH, 128), 1) < HEAD_DIM

    @pl.when(g == 0)
    def _():
        proj = _bdot(_rms(x_ref[...], gmix_ref[...]), win_ref[...])
        proj_s[...] = proj
        for h in range(N_HEADS):
            jq, half, kv = h // 2, h % 2, h // 4
            q2 = proj[:, 128 * jq:128 * (jq + 1)] * (HEAD_DIM ** -0.5)
            if half != kv:
                q2 = pltpu.roll(q2, HEAD_DIM, 1)
            hr = slice(h * SPITCH, h * SPITCH + DEC_BATCH)
            qall_s[hr, :] = jnp.where(lo if kv == 0 else jnp.logical_not(lo), q2, 0.0)
            kn8_s[hr, :] = proj[:, D_ATTN:D_ATTN + D_KV]
            vn8_s[hr, :] = proj[:, D_ATTN + D_KV:D_ATTN + 2 * D_KV]

    def head_rows(ref):
        return jnp.stack([ref[pl.ds(g * SGRP + ii, N_HEADS, stride=SPITCH), :] for ii in range(SGRP)], axis=0)

    qe = head_rows(qall_s)
    knew = head_rows(kn8_s)
    vnew = head_rows(vn8_s)
    s = jnp.einsum('bhd,bkd->bhk', qe.astype(BF16), ck_ref[...].astype(BF16),
                   preferred_element_type=F32) + sbias_ref[...][None]
    s_new = jnp.sum(qe * knew, axis=-1, keepdims=True)
    sink = sinkc_ref[...][None]
    m = jnp.maximum(jnp.maximum(jnp.max(s, axis=-1, keepdims=True), s_new), sink)
    p = jnp.exp(s - m)
    p_new = jnp.exp(s_new - m)
    den = jnp.sum(p, axis=-1, keepdims=True) + p_new + jnp.exp(sink - m)
    o = (jnp.einsum('bhk,bkd->bhd', p.astype(BF16), cv_ref[...].astype(BF16),
                    preferred_element_type=F32) + p_new * vnew) / den
    for ii in range(SGRP):
        oall_s[pl.ds(g * SGRP + ii, N_HEADS, stride=SPITCH), :] = o[ii]

    @pl.when(g == N_SGRP - 1)
    def _():
        proj = proj_s[...]
        blocks = []
        for jq in range(N_HEADS // 2):
            oa = oall_s[(2 * jq) * SPITCH:(2 * jq) * SPITCH + DEC_BATCH, :]
            ob = oall_s[(2 * jq + 1) * SPITCH:(2 * jq + 1) * SPITCH + DEC_BATCH, :]
            if jq // 2 == 0:
                blocks.append(jnp.where(lo, oa, pltpu.roll(ob, HEAD_DIM, 1)))
            else:
                blocks.append(jnp.where(lo, pltpu.roll(oa, HEAD_DIM, 1), ob))
        attn = jnp.concatenate(blocks, axis=1)
        u = proj[:, D_ATTN + 2 * D_KV:]
        ys = []
        for j in range(N_SBLK):
            bu = _bdot(u[:, 128 * j:128 * (j + 1)], bblk_ref[j])
            lr = lamr_ref[j]
            li = lami_ref[j]
            cols = slice(SBLK * j, SBLK * (j + 1))
            sr = x0r_ref[:, cols]
            si = x0i_ref[:, cols]
            nr = lr * sr - li * si + bu[:, 0:SBLK]
            ni = lr * si + li * sr + bu[:, SBLK:2 * SBLK]
            sre_ref[:, cols] = nr
            sim_ref[:, cols] = ni
            ys.append(_bdot(jnp.concatenate([nr, ni], axis=1), cblk_ref[j]))
        sn = _ssm_post(jnp.concatenate(ys, axis=1), u, dskip_ref[...], wglu_ref[...], bglu_ref[...], gs_ref[...])
        an = _rms(attn, ga_ref[...])
        h = x_ref[...] + _bdot(an, wout_ref[0:D_ATTN, :]) + _bdot(sn, wout_ref[D_ATTN:2 * D_ATTN, :])
        h = jnp.concatenate([h, jnp.zeros((DEC_BATCH, D_MODEL), F32)], axis=0)
        hmid_ref[...] = h
        cnt_ref[...] = cnt_in_ref[...]
        _route(h, gffn_ref, wrt_ref, br_ref, tri_ref, cnt_ref, hn_ref, idx_ref, gw_ref, rank_ref)
        knew_ref[...] = proj[:, D_ATTN:D_ATTN + D_KV]
        vnew_ref[...] = proj[:, D_ATTN + D_KV:D_ATTN + 2 * D_KV]


def _mixer_sample(x_s, gmix, win, ck, cv, sbias, sinkc, bblk, lamr, lami, cblk, dskip, wglu, bglu, ga, gs, wout,
                  x0r, x0i, gffn, wrt, br, cnt_in, hbuf, hnbuf, idxbuf, gwbuf, rankbuf):
    nst = SSM_G * SSM_P
    in_specs = [
        _full((DEC_BATCH, D_MODEL)), _full((1, D_MODEL)), _full((D_MODEL, D_IN)),
        pl.BlockSpec((SGRP, WINDOW, D_KV), lambda g: (g, 0, 0)),
        pl.BlockSpec((SGRP, WINDOW, D_KV), lambda g: (g, 0, 0)),
        _full((N_HEADS, WINDOW)), _full((N_HEADS, 1)),
        _full((N_SBLK, 128, 2 * SBLK)), _full((N_SBLK, 1, SBLK)), _full((N_SBLK, 1, SBLK)),
        _full((N_SBLK, 2 * SBLK, 128)), _full((1, D_SSM)), _full((D_SSM, D_SSM)), _full((1, D_SSM)),
        _full((1, D_ATTN)), _full((1, D_SSM)), _full((D_MODEL, D_MODEL)),
        _full((DEC_BATCH, nst)), _full((DEC_BATCH, nst)),
        _full((1, D_MODEL)), _full((N_EXPERTS, D_MODEL)), _full((N_EXPERTS, 1)),
        _full((2 * DEC_BATCH, 2 * DEC_BATCH)), _full((N_EXPERTS, 1)),
    ] + [pl.BlockSpec(memory_space=pl.ANY)] * 5
    tail_block = SAMPLE_ROW0 // (2 * DEC_BATCH)
    route_specs, route_shapes = _route_specs(2 * DEC_BATCH, lambda g: tail_block)
    out_shape = [
        jax.ShapeDtypeStruct((T_PAD, D_MODEL), F32),
        jax.ShapeDtypeStruct((DEC_BATCH, D_KV), F32), jax.ShapeDtypeStruct((DEC_BATCH, D_KV), F32),
        jax.ShapeDtypeStruct((DEC_BATCH, nst), F32), jax.ShapeDtypeStruct((DEC_BATCH, nst), F32),
    ] + route_shapes
    out_specs = [
        pl.BlockSpec((2 * DEC_BATCH, D_MODEL), lambda g: (tail_block, 0)),
        _full((DEC_BATCH, D_KV)), _full((DEC_BATCH, D_KV)),
        _full((DEC_BATCH, nst)), _full((DEC_BATCH, nst)),
    ] + route_specs
    head_rows = pltpu.VMEM((N_HEADS * SPITCH, 128), F32)
    scratch = [pltpu.VMEM((DEC_BATCH, D_IN), F32), head_rows, head_rows, head_rows, head_rows]
    return pl.pallas_call(
        _mixer_sample_kernel, grid=(N_SGRP,), in_specs=in_specs, out_specs=out_specs, out_shape=out_shape,
        scratch_shapes=scratch, input_output_aliases={24: 0, 25: 5, 26: 6, 27: 7, 28: 8}, name="mixer_sample",
        compiler_params=pltpu.CompilerParams(dimension_semantics=("arbitrary",), vmem_limit_bytes=VMEM_LIMIT),
    )(x_s, gmix, win, ck, cv, sbias, sinkc, bblk, lamr, lami, cblk, dskip, wglu, bglu, ga, gs, wout, x0r, x0i,
      gffn, wrt, br, _tri(2 * DEC_BATCH), cnt_in, hbuf, hnbuf, idxbuf, gwbuf, rankbuf)


def _route(h, g_ref, wrt_ref, br_ref, tri_ref, cnt_ref, hn_ref, idx_ref, gw_ref, rank_ref):
    hn = _rms(h, g_ref[...])
    hn_ref[...] = _pack_pairs(hn)
    hn_hi = hn.astype(BF16)
    hn_lo = (hn - hn_hi.astype(F32)).astype(BF16)
    w = wrt_ref[...]
    w_hi = w.astype(BF16)
    w_lo = (w - w_hi.astype(F32)).astype(BF16)
    nt = (((1,), (1,)), ((), ()))
    logits = (lax.dot_general(w_hi, hn_hi, nt, preferred_element_type=F32)
              + lax.dot_general(w_lo, hn_hi, nt, preferred_element_type=F32)
              + lax.dot_general(w_hi, hn_lo, nt, preferred_element_type=F32)) + br_ref[...]
    eidx = lax.broadcasted_iota(jnp.int32, logits.shape, 0)
    vals, onehots = [], []
    l = logits
    for k in range(TOP_K):
        m = jnp.max(l, axis=0, keepdims=True)
        ik = jnp.min(jnp.where(l == m, eidx, N_EXPERTS), axis=0, keepdims=True)
        oh = eidx == ik
        idx_ref[k:k + 1, :] = ik
        vals.append(m)
        onehots.append(oh)
        l = jnp.where(oh, -jnp.inf, l)
    exps = [jnp.exp(v - vals[0]) for v in vals]
    den = exps[0] + exps[1] + exps[2] + exps[3]
    for k in range(TOP_K):
        gw_ref[k:k + 1, :] = exps[k] / den
    member = jnp.zeros(logits.shape, F32)
    for oh in onehots:
        member = member + jnp.where(oh, 1.0, 0.0)
    wblk = tri_ref.shape[0]
    base = cnt_ref[...].astype(F32)
    befores = []
    for cb in range(h.shape[0] // wblk):
        mblk = member[:, cb * wblk:(cb + 1) * wblk]
        befores.append(jnp.dot(mblk.astype(BF16), tri_ref[...], preferred_element_type=F32) + base)
        base = base + jnp.sum(mblk, axis=1, keepdims=True)
    before = jnp.concatenate(befores, axis=1)
    for k in range(TOP_K):
        rank_ref[k:k + 1, :] = jnp.sum(jnp.where(onehots[k], before, 0.0), axis=0, keepdims=True).astype(jnp.int32)
    cnt_ref[...] = base.astype(jnp.int32)


def _route_specs(rows, block):
    specs = [pl.BlockSpec((rows, D_PACK), lambda i: (block(i), 0)), pl.BlockSpec((TOP_K, rows), lambda i: (0, block(i))),
             pl.BlockSpec((TOP_K, rows), lambda i: (0, block(i))), pl.BlockSpec((TOP_K, rows), lambda i: (0, block(i))),
             _full((N_EXPERTS, 1))]
    shapes = [jax.ShapeDtypeStruct((T_PAD, D_PACK), jnp.int32), jax.ShapeDtypeStruct((TOP_K, T_PAD), jnp.int32),
              jax.ShapeDtypeStruct((TOP_K, T_PAD), F32), jax.ShapeDtypeStruct((TOP_K, T_PAD), jnp.int32),
              jax.ShapeDtypeStruct((N_EXPERTS, 1), jnp.int32)]
    return specs, shapes


def _tri(n):
    return jnp.asarray(np.triu(np.ones((n, n), np.float32), 1), BF16)


def _place_kernel(off_ref, idx_ref, rank_ref, pos_ref):
    idx = idx_ref[...]
    pos = rank_ref[...]
    for e in range(N_EXPERTS):
        pos = pos + jnp.where(idx == e, off_ref[e], 0)
    pos_ref[...] = pos


def _place(offsets, idx, rank):
    return pl.pallas_call(
        _place_kernel,
        in_specs=[pl.BlockSpec(memory_space=pltpu.SMEM), pl.BlockSpec(memory_space=pltpu.VMEM),
                  pl.BlockSpec(memory_space=pltpu.VMEM)],
        out_specs=pl.BlockSpec(memory_space=pltpu.VMEM),
        out_shape=jax.ShapeDtypeStruct((TOP_K, T_PAD), jnp.int32), name="place",
    )(offsets, idx, rank)


def _sc_mesh():
    return plsc.VectorSubcoreMesh(core_axis_name="core", subcore_axis_name="subcore")


def _sc_dispatch(rows, pos):
    n, d = rows.shape
    nblk = n // SC_ROWS
    pos_w = pos.reshape(TOP_K, nblk, SC_ROWS).transpose(1, 0, 2)

    @functools.partial(pl.kernel, out_type=jax.ShapeDtypeStruct((P_ROWS, d), rows.dtype), mesh=_sc_mesh(),
                       scratch_types=[], name="dispatch")
    def run(x_hbm, i_hbm, o_hbm):
        def body(x_vmem, i_vmem):
            for k in range(TOP_K):
                pltpu.sync_copy(x_vmem, o_hbm.at[i_vmem.at[0, k]])

        pltpu.emit_pipeline(
            body, grid=(nblk,),
            in_specs=[pl.BlockSpec((SC_ROWS, d), lambda i: (i, 0)),
                      pl.BlockSpec((1, TOP_K, SC_ROWS), lambda i: (i, 0, 0))],
            out_specs=[], core_axis_name=("core", "subcore"), dimension_semantics=(pltpu.PARALLEL,),
        )(x_hbm, i_hbm)

    return run(rows, pos_w)


def _sc_combine(rows, flat_pos):
    _, d = rows.shape
    n = flat_pos.shape[0]
    w = SC_ROWS_COMBINE
    nblk = n // w
    pos_w = flat_pos.reshape(nblk, 1, w)

    @functools.partial(pl.kernel, out_type=jax.ShapeDtypeStruct((n, d), rows.dtype), mesh=_sc_mesh(),
                       scratch_types=[], name="combine")
    def run(x_hbm, i_hbm, o_hbm):
        def body(i_vmem, o_vmem):
            pltpu.sync_copy(x_hbm.at[i_vmem.at[0, 0]], o_vmem)

        pltpu.emit_pipeline(
            body, grid=(nblk,),
            in_specs=[pl.BlockSpec((1, 1, w), lambda i: (i, 0, 0))],
            out_specs=[pl.BlockSpec((w, d), lambda i: (i, 0))],
            core_axis_name=("core", "subcore"), dimension_semantics=(pltpu.PARALLEL,),
        )(i_hbm, o_hbm)

    return run(rows, pos_w)


def _experts_kernel(ts_ref, trow_ref, tsz_ref, xs_hbm, w1_hbm, b1_ref, w2_hbm, b2_ref, ys_hbm,
                    w1_st, w2_st, w1_s, w2_s, xbuf, ybuf, wsem, xsem, ysem):
    e = pl.program_id(0)
    n_valid = ts_ref[N_EXPERTS]

    def w_copies(ex, slot):
        return (pltpu.make_async_copy(w1_hbm.at[ex], w1_st.at[slot], wsem.at[0, slot]),
                pltpu.make_async_copy(w2_hbm.at[ex], w2_st.at[slot], wsem.at[1, slot]))

    def x_copy(t, slot, rows):
        src = xs_hbm.at[pl.ds(pl.multiple_of(trow_ref[t], TM_HALF), rows)]
        return pltpu.make_async_copy(src, xbuf.at[slot, pl.ds(0, rows)], xsem.at[slot])

    def y_copy(t, slot, rows):
        dst = ys_hbm.at[pl.ds(pl.multiple_of(trow_ref[t], TM_HALF), rows)]
        return pltpu.make_async_copy(ybuf.at[slot, pl.ds(0, rows)], dst, ysem.at[slot])

    def by_size(t, fn):
        for rows in (TM, TM_HALF):
            @pl.when(tsz_ref[t] == rows)
            def _():
                fn(rows)

    @pl.when(e == 0)
    def _():
        for c in w_copies(0, 0):
            c.start()
        by_size(0, lambda rows: x_copy(0, 0, rows).start())

    @pl.when(e + 1 < N_EXPERTS)
    def _():
        for c in w_copies(e + 1, (e + 1) % 2):
            c.start()

    wslot = e % 2
    for c in w_copies(e, wslot):
        c.wait()
    for r in range(4):
        rs = slice(256 * r, 256 * (r + 1))
        w1_s[rs, :] = w1_st[wslot, rs, :].astype(BF16)
        w2_s[rs, :] = w2_st[wslot, rs, :].astype(BF16)

    def tile(t, carry):
        slot = t % 2

        @pl.when(t + 1 < n_valid)
        def _():
            by_size(t + 1, lambda rows: x_copy(t + 1, 1 - slot, rows).start())

        by_size(t, lambda rows: x_copy(t, slot, rows).wait())

        @pl.when(t >= 2)
        def _():
            by_size(t - 2, lambda rows: y_copy(t - 2, slot, rows).wait())

        def compute(rows):
            hdn = _bdot(_unpack_pairs(xbuf[slot, 0:rows]), w1_s[...]) + b1_ref[0]
            gt = jnp.minimum(hdn[:, :D_FF], SWIGLU_LIMIT)
            up = jnp.clip(hdn[:, D_FF:], -SWIGLU_LIMIT, SWIGLU_LIMIT)
            act = (up + 1.0) * gt * _sigmoid(SWIGLU_ALPHA * gt)
            ybuf[slot, 0:rows] = _pack_pairs(_bdot(act, w2_s[...]) + b2_ref[0])
            y_copy(t, slot, rows).start()

        by_size(t, compute)
        return carry

    lax.fori_loop(ts_ref[e], ts_ref[e + 1], tile, 0)

    @pl.when(e == N_EXPERTS - 1)
    def _():
        @pl.when(n_valid >= 2)
        def _():
            by_size(n_valid - 2, lambda rows: y_copy(n_valid - 2, n_valid % 2, rows).wait())

        by_size(n_valid - 1, lambda rows: y_copy(n_valid - 1, (n_valid - 1) % 2, rows).wait())


def _tile_plan(counts):
    units = (counts + (TM_HALF - 1)) // TM_HALF
    unit_end = jnp.cumsum(units)
    offsets = ((unit_end - units) * TM_HALF).astype(jnp.int32)
    n_full = units // 2
    tiles_per = n_full + units % 2
    tile_end = jnp.cumsum(tiles_per)
    tile_start = jnp.concatenate([jnp.zeros((1,), jnp.int32), tile_end.astype(jnp.int32)])
    t = jnp.arange(MAX_TILES, dtype=jnp.int32)
    e_of_t = jnp.minimum(jnp.sum((tile_end[None, :] <= t[:, None]).astype(jnp.int32), axis=1), N_EXPERTS - 1)
    j = t - tile_start[e_of_t]
    valid = t < tile_end[-1]
    tile_row = jnp.where(valid, offsets[e_of_t] + j * TM, 0).astype(jnp.int32)
    tile_rows = jnp.where(valid, jnp.where(j < n_full[e_of_t], TM, TM_HALF), 0).astype(jnp.int32)
    return offsets, tile_start, tile_row, tile_rows


def _experts(tile_start, tile_row, tile_rows, xs, w_up, b_up, w_down, b_down):
    wsel = lambda e, *_: (e, 0, 0)
    hbm = pl.BlockSpec(memory_space=pl.ANY)
    grid_spec = pltpu.PrefetchScalarGridSpec(
        num_scalar_prefetch=3, grid=(N_EXPERTS,),
        in_specs=[hbm, hbm, pl.BlockSpec((1, 1, 2 * D_FF), wsel), hbm, pl.BlockSpec((1, 1, D_MODEL), wsel)],
        out_specs=hbm,
        scratch_shapes=[pltpu.VMEM((2, D_MODEL, 2 * D_FF), F32), pltpu.VMEM((2, D_FF, D_MODEL), F32),
                        pltpu.VMEM((D_MODEL, 2 * D_FF), BF16), pltpu.VMEM((D_FF, D_MODEL), BF16),
                        pltpu.VMEM((2, TM, D_PACK), jnp.int32), pltpu.VMEM((2, TM, D_PACK), jnp.int32),
                        pltpu.SemaphoreType.DMA((2, 2)), pltpu.SemaphoreType.DMA((2,)),
                        pltpu.SemaphoreType.DMA((2,))],
    )
    return pl.pallas_call(
        _experts_kernel, grid_spec=grid_spec, out_shape=jax.ShapeDtypeStruct((P_ROWS, D_PACK), jnp.int32),
        name="experts",
        compiler_params=pltpu.CompilerParams(dimension_semantics=("arbitrary",), vmem_limit_bytes=VMEM_LIMIT),
    )(tile_start, tile_row, tile_rows, xs, w_up, b_up.reshape(N_EXPERTS, 1, 2 * D_FF), w_down,
      b_down.reshape(N_EXPERTS, 1, D_MODEL))


def _final_kernel(h_ref, yg_ref, gw_ref, p_ref, gple_ref, wg_ref, wp_ref, gfin_ref, *rest):
    out_ref = rest[-1]
    rows = h_ref.shape[0]
    gw = gw_ref[...]
    h = h_ref[...]
    for k in range(TOP_K):
        h = h + gw[:, k:k + 1] * _unpack_pairs(yg_ref[k])
    gate = _sigmoid(_bdot(_rms(h, gple_ref[...]), wg_ref[...]))
    h = h + gate * _bdot(p_ref[...].reshape(rows, PLE_DIM), wp_ref[...])
    out_ref[...] = _rms(h, gfin_ref[...]).reshape(out_ref.shape)


def _final_prompt(part, hmid, yg, gwt, p_prompt, gple, wg, wp, gfin, y_prev):
    nb = 4
    rows = nb * CHUNK
    nbh = BATCH // nb
    c0 = part * PART_CHUNKS
    rb = lambda c, b: ((c0 + c) * nbh + b, 0)
    in_specs = [pl.BlockSpec((rows, D_MODEL), rb),
                pl.BlockSpec((TOP_K, rows, D_PACK), lambda c, b: (0, c * nbh + b, 0)),
                pl.BlockSpec((rows, TOP_K), rb),
                pl.BlockSpec((nb, CHUNK, PLE_DIM), lambda c, b: (b, c0 + c, 0)),
                _full((1, D_MODEL)), _full((D_MODEL, D_MODEL)), _full((PLE_DIM, D_MODEL)), _full((1, D_MODEL))]
    args = [hmid, yg, gwt, p_prompt, gple, wg, wp, gfin]
    aliases = {}
    if y_prev is not None:
        in_specs.append(pl.BlockSpec(memory_space=pl.ANY))
        args.append(y_prev)
        aliases = {len(args) - 1: 0}
    return pl.pallas_call(
        _final_kernel, grid=(PART_CHUNKS, nbh), in_specs=in_specs,
        out_specs=pl.BlockSpec((nb, CHUNK, D_MODEL), lambda c, b: (b, c0 + c, 0)),
        out_shape=jax.ShapeDtypeStruct((BATCH, SEQ, D_MODEL), F32), name="final_prompt",
        input_output_aliases=aliases,
        compiler_params=pltpu.CompilerParams(dimension_semantics=("arbitrary", "arbitrary"),
                                             vmem_limit_bytes=VMEM_LIMIT),
    )(*args)


def _final_sample(hmid, yg, gwt, p_sample, gple, wg, wp, gfin):
    blk = SAMPLE_ROW0 // DEC_BATCH
    return pl.pallas_call(
        _final_kernel, grid=(1,),
        in_specs=[pl.BlockSpec((DEC_BATCH, D_MODEL), lambda i: (blk, 0)),
                  pl.BlockSpec((TOP_K, DEC_BATCH, D_PACK), lambda i: (0, PART_CHUNKS * ROWS // DEC_BATCH, 0)),
                  pl.BlockSpec((DEC_BATCH, TOP_K), lambda i: (blk, 0)),
                  _full((DEC_BATCH, PLE_DIM)),
                  _full((1, D_MODEL)), _full((D_MODEL, D_MODEL)), _full((PLE_DIM, D_MODEL)), _full((1, D_MODEL))],
        out_specs=_full((DEC_BATCH, D_MODEL)),
        out_shape=jax.ShapeDtypeStruct((DEC_BATCH, D_MODEL), F32), name="final_sample",
        compiler_params=pltpu.CompilerParams(dimension_semantics=("arbitrary",), vmem_limit_bytes=VMEM_LIMIT),
    )(hmid, yg, gwt, p_sample, gple, wg, wp, gfin)


def _alibi_tables():
    slopes = 2.0 ** (-8.0 * (np.arange(N_HEADS, dtype=np.float64) + 1.0) / N_HEADS)
    i = np.arange(CHUNK)[:, None]
    j = np.arange(2 * CHUNK)[None, :]
    dist = i + CHUNK - j
    valid = (dist >= 0) & (dist <= WINDOW)
    tabs = []
    for has_prev in (False, True):
        ok = valid & ((j >= CHUNK) | has_prev)
        tabs.append(np.where(ok[None], -slopes[:, None, None] * dist[None], NEG))
    prompt = np.stack([np.stack([np.concatenate([t[h] for h in grp], axis=0) for grp in (NAT_HEADS, ROT_HEADS)])
                       for t in tabs]).astype(np.float32)
    wb = min(WINDOW, PAST_LEN)
    sample = (-slopes[:, None] * (wb - np.arange(wb))[None, :]).astype(np.float32)
    return prompt, sample


def kernel(x_prompt, x_sample, cache_k_win, cache_v_win, state_ssm_re, state_ssm_im, p_prompt, p_sample, norm_mix, w_in, sinks, ssm_lam_re, ssm_lam_im, ssm_log_step, ssm_b_re, ssm_b_im, ssm_c_re, ssm_c_im, ssm_d, w_glu, b_glu, norm_attn_out, norm_ssm_out, w_out, norm_ffn, w_router, b_router, w_up, b_up, w_down, b_down, norm_ple, w_ple_gate, w_ple_proj, norm_final):
    nst = SSM_G * SSM_P
    bias_np, sbias_np = _alibi_tables()
    bias = jnp.asarray(bias_np)
    sbias = jnp.asarray(sbias_np)

    lbr, lbi, bbr, bbi = _prep(ssm_lam_re[0], ssm_lam_im[0], ssm_log_step[0], ssm_b_re[0], ssm_b_im[0])
    bblk, lamr, lami, cblk = _s5_blocks(lbr, lbi, bbr, bbi, ssm_c_re[0], ssm_c_im[0])

    gmix = norm_mix[0].reshape(1, D_MODEL)
    win = w_in[0].astype(BF16)
    dskip = ssm_d[0].reshape(1, D_SSM)
    wglu = w_glu[0].astype(BF16)
    bglu = b_glu[0].reshape(1, D_SSM)
    ga = norm_attn_out[0].reshape(1, D_ATTN)
    gs = norm_ssm_out[0].reshape(1, D_SSM)
    wout = w_out[0].astype(BF16)
    sink = sinks[0]

    gffn = norm_ffn[0].reshape(1, D_MODEL)
    wrt = w_router[0].T
    br = b_router[0].reshape(N_EXPERTS, 1)
    hbuf, k_last, v_last, re_p, im_p, hnbuf, idxbuf, gwbuf, rankbuf, cnt_p = _mixer_prompt(
        x_prompt, sink, gmix, win, bias, bblk, lamr, lami, cblk, dskip, wglu, bglu, ga, gs, wout, gffn, wrt, br)

    ck = cache_k_win[0].reshape(DEC_BATCH, WINDOW, D_KV)
    cv = cache_v_win[0].reshape(DEC_BATCH, WINDOW, D_KV)
    hmid, k_new, v_new, re_s, im_s, hn, idx, gw, rank, counts = _mixer_sample(
        x_sample.reshape(DEC_BATCH, D_MODEL), gmix, win, ck, cv, sbias, sink.reshape(N_HEADS, 1), bblk, lamr, lami,
        cblk, dskip, wglu, bglu, ga, gs, wout, state_ssm_re[0].reshape(DEC_BATCH, nst),
        state_ssm_im[0].reshape(DEC_BATCH, nst), gffn, wrt, br, cnt_p, hbuf, hnbuf, idxbuf, gwbuf, rankbuf)

    offsets, tile_start, tile_row, tile_rows = _tile_plan(counts[:, 0])

    pos = _place(offsets, idx, rank)
    xs = _sc_dispatch(hn, pos)
    ys = _experts(tile_start, tile_row, tile_rows, xs, w_up[0], b_up[0], w_down[0], b_down[0])
    ygs = []
    for part in range(N_PARTS):
        r0 = part * PART_CHUNKS * ROWS
        r1 = T_PAD if part == N_PARTS - 1 else r0 + PART_CHUNKS * ROWS
        ygs.append(_sc_combine(ys, pos[:, r0:r1].reshape(TOP_K * (r1 - r0))).reshape(TOP_K, r1 - r0, D_PACK))

    gwt = gw.T
    gple = norm_ple[0].reshape(1, D_MODEL)
    wg = w_ple_gate[0].astype(BF16)
    wp = w_ple_proj[0].astype(BF16)
    gfin = norm_final.reshape(1, D_MODEL)
    y_prompt = None
    for part in range(N_PARTS):
        y_prompt = _final_prompt(part, hmid, ygs[part], gwt, p_prompt[0], gple, wg, wp, gfin, y_prompt)
    y_sample = _final_sample(hmid, ygs[-1], gwt, p_sample[0].reshape(DEC_BATCH, PLE_DIM), gple, wg, wp, gfin)

    k_win_s = jnp.concatenate([ck[:, 1:], k_new[:, None, :]], axis=1)
    v_win_s = jnp.concatenate([cv[:, 1:], v_new[:, None, :]], axis=1)
    kv5 = (1, BATCH, CHUNK, N_KV, HEAD_DIM)
    skv5 = (1, DEC_BATCH, WINDOW, N_KV, HEAD_DIM)
    return (y_prompt, y_sample.reshape(DEC_BATCH, 1, D_MODEL),
            k_last.reshape(kv5), v_last.reshape(kv5),
            re_p.reshape(1, BATCH, SSM_G, SSM_P), im_p.reshape(1, BATCH, SSM_G, SSM_P),
            k_win_s.reshape(skv5), v_win_s.reshape(skv5),
            re_s.reshape(1, DEC_BATCH, SSM_G, SSM_P), im_s.reshape(1, DEC_BATCH, SSM_G, SSM_P))
```

```python
import functools

import numpy as np
import jax
import jax.numpy as jnp
from jax import lax
from jax.experimental import pallas as pl
from jax.experimental.pallas import tpu as pltpu
from jax.experimental.pallas import tpu_sc as plsc

F32 = jnp.float32
BF16 = jnp.bfloat16

D_MODEL = 1024
BATCH = 8
SEQ = 2048
DEC_BATCH = 128
PAST_LEN = 16384
HEAD_DIM = 64
D_ATTN = 512
N_HEADS = 8
N_KV = 2
D_KV = N_KV * HEAD_DIM
WINDOW = 128
D_SSM = 512
SSM_H = 16
SSM_G = 32
SSM_P = 64
D_IN = D_ATTN + 2 * D_KV + D_SSM
N_EXPERTS = 32
TOP_K = 4
D_FF = 1024
SWIGLU_LIMIT = 7.0
SWIGLU_ALPHA = 1.702
PLE_DIM = 256
EPS = 1e-5
NEG = -1e30

CHUNK = 128
N_CHUNKS = SEQ // CHUNK
ROWS = BATCH * CHUNK
PITCH = CHUNK + 8
NAT_HEADS = (0, 2, 5, 7)
ROT_HEADS = (1, 3, 4, 6)
SUB_T = 32
SUB_ROWS = SUB_T * BATCH
N_SBLK = 4
SBLK = 512
T_REAL = BATCH * SEQ + DEC_BATCH
T_PAD = T_REAL + 128
SAMPLE_ROW0 = BATCH * SEQ
ROUTE_BLOCK = 512
TM = 512
TM_HALF = 256
P_ROWS = (T_PAD * TOP_K + N_EXPERTS * (TM_HALF - 1)) // TM_HALF * TM_HALF
MAX_TILES = P_ROWS // TM + N_EXPERTS + 1
SC_ROWS = 40
SC_ROWS_COMBINE = 32
N_PARTS = 4
PART_CHUNKS = N_CHUNKS // N_PARTS
D_PACK = D_MODEL // 2
VMEM_LIMIT = 56 * 1024 * 1024


def _rms(x, g):
    return x * lax.rsqrt(jnp.mean(x * x, axis=-1, keepdims=True) + EPS) * g


def _sigmoid(x):
    return 1.0 / (1.0 + jnp.exp(-x))


def _gelu_tanh(x):
    c = np.float32(np.sqrt(2.0 / np.pi))
    return 0.5 * x * (1.0 + jnp.tanh(c * (x + 0.044715 * (x * x * x))))


def _bdot(a, b):
    return jnp.dot(a.astype(BF16), b, preferred_element_type=F32)


def _pack_pairs(x):
    n = x.shape[1] // 2
    lo = lax.bitcast_convert_type(x[:, :n].astype(BF16).astype(F32), jnp.int32)
    hi = lax.bitcast_convert_type(x[:, n:].astype(BF16).astype(F32), jnp.int32)
    return lax.shift_right_logical(lo, 16) | hi


def _unpack_pairs(w):
    lo = lax.bitcast_convert_type(lax.shift_left(w, 16), F32)
    hi = lax.bitcast_convert_type(w & jnp.int32(-65536), F32)
    return jnp.concatenate([lo, hi], axis=1)


def _full(shape):
    n = len(shape)
    return pl.BlockSpec(shape, lambda *_: (0,) * n)


def _prep_kernel(lr_ref, li_ref, ls_ref, br_ref, bi_ref, lbr_ref, lbi_ref, bbr_ref, bbi_ref):
    lr = lr_ref[...]
    li = li_ref[...]
    step = jnp.exp(ls_ref[...])
    zr = lr * step
    zi = li * step
    mag = jnp.exp(zr)
    lbr = mag * jnp.cos(zi)
    lbi = mag * jnp.sin(zi)
    lbr_ref[...] = lbr
    lbi_ref[...] = lbi
    nr = lbr - 1.0
    den = lr * lr + li * li
    cr = (nr * lr + lbi * li) / den
    ci = (lbi * lr - nr * li) / den
    br = br_ref[...]
    bi = bi_ref[...]
    bbr_ref[...] = cr * br - ci * bi
    bbi_ref[...] = cr * bi + ci * br


def _prep(lam_re, lam_im, log_step, b_re, b_im):
    g, p, h = SSM_G, SSM_P, SSM_H
    out = pl.pallas_call(
        _prep_kernel,
        out_shape=[jax.ShapeDtypeStruct((g, 1, p), F32), jax.ShapeDtypeStruct((g, 1, p), F32),
                   jax.ShapeDtypeStruct((g, h, p), F32), jax.ShapeDtypeStruct((g, h, p), F32)],
        name="s5_prep",
    )(lam_re.reshape(g, 1, p), lam_im.reshape(g, 1, p), log_step.reshape(g, 1, 1),
      jnp.transpose(b_re, (0, 2, 1)), jnp.transpose(b_im, (0, 2, 1)))
    return out


def _s5_blocks(lbr, lbi, bbr, bbi, c_re, c_im):
    eye = jnp.eye(8, dtype=F32)
    shp = (N_SBLK, 8, SSM_H, SSM_P)

    def in_map(b):
        return (b.reshape(shp)[:, :, :, None, :] * eye[None, :, None, :, None]).reshape(N_SBLK, 128, SBLK)

    def out_map(c):
        ct = jnp.transpose(c.reshape(shp), (0, 1, 3, 2))
        return (ct[:, :, :, None, :] * eye[None, :, None, :, None]).reshape(N_SBLK, SBLK, 128)

    bblk = jnp.concatenate([in_map(bbr), in_map(bbi)], axis=-1).astype(BF16)
    cblk = jnp.concatenate([out_map(c_re), -out_map(c_im)], axis=1).astype(BF16)
    return bblk, lbr.reshape(N_SBLK, 1, SBLK), lbi.reshape(N_SBLK, 1, SBLK), cblk


def _ssm_post(y_lin, u, dskip, wglu, bglu, gs):
    y = _gelu_tanh(y_lin + dskip * u)
    y = y * _sigmoid(_bdot(y, wglu) + bglu)
    return _rms(y, gs)


def _mixer_prompt_kernel(sinks_ref, x_ref, gmix_ref, win_ref, bias_ref, bblk_ref, lamr_ref, lami_ref,
                         cblk_ref, dskip_ref, wglu_ref, bglu_ref, ga_ref, gs_ref, wout_ref,
                         gffn_ref, wrt_ref, br_ref, tri_ref,
                         hmid_ref, klast_ref, vlast_ref, sre_ref, sim_ref, hn_ref, idx_ref, gw_ref, rank_ref, cnt_ref,
                         proj_s, u_s, kprev_s, kprevr_s, vprev_s, vprevr_s, attn_s, ssm_s, utb_s, bu_s, xs_s, ytb_s):
    c = pl.program_id(0)

    @pl.when(c == 0)
    def _():
        zkv = jnp.zeros(kprev_s.shape, BF16)
        kprev_s[...] = zkv
        kprevr_s[...] = zkv
        vprev_s[...] = zkv
        vprevr_s[...] = zkv
        sre_ref[...] = jnp.zeros(sre_ref.shape, F32)
        sim_ref[...] = jnp.zeros(sim_ref.shape, F32)
        cnt_ref[...] = jnp.zeros(cnt_ref.shape, jnp.int32)

    x = x_ref[...].reshape(ROWS, D_MODEL)
    proj = _bdot(_rms(x, gmix_ref[...]), win_ref[...])
    u0 = D_ATTN + 2 * D_KV
    proj_s[...] = proj[:, 0:u0]
    for jb in range(D_SSM // 128):
        for b in range(BATCH):
            u_s[jb, b * PITCH:b * PITCH + CHUNK, :] = proj[b * CHUNK:(b + 1) * CHUNK, u0 + 128 * jb:u0 + 128 * (jb + 1)]

    lo = lax.broadcasted_iota(jnp.int32, (CHUNK, 128), 1) < HEAD_DIM
    hi = jnp.logical_not(lo)
    table = jnp.minimum(c, 1)
    hrow = lax.broadcasted_iota(jnp.int32, (4 * CHUNK, 1), 0) // CHUNK

    def sink_col(heads):
        col = jnp.full((4 * CHUNK, 1), sinks_ref[heads[3]], F32)
        for n in (2, 1, 0):
            col = jnp.where(hrow == n, sinks_ref[heads[n]], col)
        return col

    sink_nat = sink_col(NAT_HEADS)
    sink_rot = sink_col(ROT_HEADS)

    def attend(q, k, v, bias, sink):
        s = lax.dot_general(q, k, (((1,), (1,)), ((), ())), preferred_element_type=F32) + bias
        m = jnp.maximum(jnp.max(s, axis=-1, keepdims=True), sink)
        p = jnp.exp(s - m)
        den = jnp.sum(p, axis=-1, keepdims=True) + jnp.exp(sink - m)
        return jnp.dot(p.astype(BF16), v, preferred_element_type=F32) / den

    def attn_body(b, carry):
        r0 = pl.multiple_of(b * CHUNK, CHUNK)
        rows = pl.ds(r0, CHUNK)
        kb = proj_s[rows, D_ATTN:D_ATTN + D_KV]
        vb = proj_s[rows, D_ATTN + D_KV:D_ATTN + 2 * D_KV]
        kb16 = kb.astype(BF16)
        vb16 = vb.astype(BF16)
        kbr16 = pltpu.roll(kb, HEAD_DIM, 1).astype(BF16)
        vbr16 = pltpu.roll(vb, HEAD_DIM, 1).astype(BF16)
        k_nat = jnp.concatenate([kprev_s[b], kb16], axis=0)
        k_rot = jnp.concatenate([kprevr_s[b], kbr16], axis=0)
        v_nat = jnp.concatenate([vprev_s[b], vb16], axis=0)
        v_rot = jnp.concatenate([vprevr_s[b], vbr16], axis=0)
        q2 = [proj_s[rows, 128 * jq:128 * (jq + 1)] * (HEAD_DIM ** -0.5) for jq in range(N_HEADS // 2)]
        q_nat = jnp.concatenate([jnp.where(lo if h % 2 == 0 else hi, q2[h // 2], 0.0) for h in NAT_HEADS],
                                axis=0).astype(BF16)
        q_rot = jnp.concatenate([jnp.where(lo if h % 2 == 0 else hi, q2[h // 2], 0.0) for h in ROT_HEADS],
                                axis=0).astype(BF16)
        o_nat = attend(q_nat, k_nat, v_nat, bias_ref[table, 0], sink_nat)
        o_rot = attend(q_rot, k_rot, v_rot, bias_ref[table, 1], sink_rot)
        for jq in range(N_HEADS // 2):
            blk = slice(CHUNK * jq, CHUNK * (jq + 1))
            even, odd = (o_nat, o_rot) if jq < 2 else (o_rot, o_nat)
            attn_s[rows, 128 * jq:128 * (jq + 1)] = jnp.where(lo, even[blk], odd[blk])
        kprev_s[b] = kb16
        kprevr_s[b] = kbr16
        vprev_s[b] = vb16
        vprevr_s[b] = vbr16
        return carry

    lax.fori_loop(0, BATCH, attn_body, 0)

    for sc in range(CHUNK // SUB_T):
        t0 = sc * SUB_T
        for i in range(SUB_T):
            for jb in range(D_SSM // 128):
                utb_s[i * BATCH:(i + 1) * BATCH, 128 * jb:128 * (jb + 1)] = (
                    u_s[jb, pl.ds(t0 + i, BATCH, stride=PITCH), :])
        u_tb = utb_s[...]
        for j in range(N_SBLK):
            bu_s[...] = _bdot(u_tb[:, 128 * j:128 * (j + 1)], bblk_ref[j])
            lr = jnp.broadcast_to(lamr_ref[j], (BATCH, SBLK))
            li = jnp.broadcast_to(lami_ref[j], (BATCH, SBLK))
            cols = slice(SBLK * j, SBLK * (j + 1))

            def step(i, carry):
                sr, si = carry
                r = pl.ds(pl.multiple_of(i * BATCH, BATCH), BATCH)
                nr = lr * sr - li * si + bu_s[r, 0:SBLK]
                ni = lr * si + li * sr + bu_s[r, SBLK:2 * SBLK]
                xs_s[r, 0:SBLK] = nr
                xs_s[r, SBLK:2 * SBLK] = ni
                return nr, ni

            sr, si = lax.fori_loop(0, SUB_T, step, (sre_ref[:, cols], sim_ref[:, cols]), unroll=True)
            sre_ref[:, cols] = sr
            sim_ref[:, cols] = si
            ytb_s[:, 128 * j:128 * (j + 1)] = _bdot(xs_s[...], cblk_ref[j])
        yn = _ssm_post(ytb_s[...], u_tb, dskip_ref[...], wglu_ref[...], bglu_ref[...], gs_ref[...])
        for i in range(SUB_T):
            for jb in range(D_SSM // 128):
                ssm_s[jb, pl.ds(t0 + i, BATCH, stride=PITCH), :] = (
                    yn[i * BATCH:(i + 1) * BATCH, 128 * jb:128 * (jb + 1)])

    an = _rms(attn_s[...], ga_ref[...])
    sn = jnp.concatenate(
        [jnp.concatenate([ssm_s[jb, b * PITCH:b * PITCH + CHUNK, :] for b in range(BATCH)], axis=0)
         for jb in range(D_SSM // 128)], axis=1)
    h = x + _bdot(an, wout_ref[0:D_ATTN, :]) + _bdot(sn, wout_ref[D_ATTN:2 * D_ATTN, :])
    hmid_ref[...] = h
    _route(h, gffn_ref, wrt_ref, br_ref, tri_ref, cnt_ref, hn_ref, idx_ref, gw_ref, rank_ref)

    @pl.when(c == N_CHUNKS - 1)
    def _():
        klast_ref[...] = proj_s[:, D_ATTN:D_ATTN + D_KV].reshape(BATCH, CHUNK, D_KV)
        vlast_ref[...] = proj_s[:, D_ATTN + D_KV:D_ATTN + 2 * D_KV].reshape(BATCH, CHUNK, D_KV)


def _mixer_prompt(x_prompt, sinks, gmix, win, bias, bblk, lamr, lami, cblk, dskip, wglu, bglu, ga, gs, wout,
                  gffn, wrt, br):
    smem = pl.BlockSpec(memory_space=pltpu.SMEM)
    in_specs = [
        smem,
        pl.BlockSpec((BATCH, CHUNK, D_MODEL), lambda c: (0, c, 0)),
        _full((1, D_MODEL)), _full((D_MODEL, D_IN)), _full((2, 2, 4 * CHUNK, 2 * CHUNK)),
        _full((N_SBLK, 128, 2 * SBLK)), _full((N_SBLK, 1, SBLK)), _full((N_SBLK, 1, SBLK)),
        _full((N_SBLK, 2 * SBLK, 128)), _full((1, D_SSM)), _full((D_SSM, D_SSM)), _full((1, D_SSM)),
        _full((1, D_ATTN)), _full((1, D_SSM)), _full((D_MODEL, D_MODEL)),
        _full((1, D_MODEL)), _full((N_EXPERTS, D_MODEL)), _full((N_EXPERTS, 1)), _full((ROUTE_BLOCK, ROUTE_BLOCK)),
    ]
    route_specs, route_shapes = _route_specs(ROWS, lambda c: c)
    out_shape = [
        jax.ShapeDtypeStruct((T_PAD, D_MODEL), F32),
        jax.ShapeDtypeStruct((BATCH, CHUNK, D_KV), F32),
        jax.ShapeDtypeStruct((BATCH, CHUNK, D_KV), F32),
        jax.ShapeDtypeStruct((BATCH, SSM_G * SSM_P), F32),
        jax.ShapeDtypeStruct((BATCH, SSM_G * SSM_P), F32),
    ] + route_shapes
    out_specs = [
        pl.BlockSpec((ROWS, D_MODEL), lambda c: (c, 0)),
        _full((BATCH, CHUNK, D_KV)), _full((BATCH, CHUNK, D_KV)),
        _full((BATCH, SSM_G * SSM_P)), _full((BATCH, SSM_G * SSM_P)),
    ] + route_specs
    kv_scr = pltpu.VMEM((BATCH, CHUNK, D_KV), BF16)
    scratch = [
        pltpu.VMEM((ROWS, D_ATTN + 2 * D_KV), F32), pltpu.VMEM((D_SSM // 128, BATCH * PITCH, 128), F32),
        kv_scr, kv_scr, kv_scr, kv_scr,
        pltpu.VMEM((ROWS, D_ATTN), F32), pltpu.VMEM((D_SSM // 128, BATCH * PITCH, 128), F32),
        pltpu.VMEM((SUB_ROWS, D_SSM), F32), pltpu.VMEM((SUB_ROWS, 2 * SBLK), F32),
        pltpu.VMEM((SUB_ROWS, 2 * SBLK), F32), pltpu.VMEM((SUB_ROWS, D_SSM), F32),
    ]
    return pl.pallas_call(
        _mixer_prompt_kernel, grid=(N_CHUNKS,), in_specs=in_specs, out_specs=out_specs, out_shape=out_shape,
        scratch_shapes=scratch, name="mixer_prompt",
        compiler_params=pltpu.CompilerParams(dimension_semantics=("arbitrary",), vmem_limit_bytes=VMEM_LIMIT),
    )(sinks, x_prompt, gmix, win, bias, bblk, lamr, lami, cblk, dskip, wglu, bglu, ga, gs, wout,
      gffn, wrt, br, _tri(ROUTE_BLOCK))


SGRP = 16
SPITCH = DEC_BATCH + 8
N_SGRP = DEC_BATCH // SGRP


def _mixer_sample_kernel(x_ref, gmix_ref, win_ref, ck_ref, cv_ref, sbias_ref, sinkc_ref, bblk_ref, lamr_ref,
                         lami_ref, cblk_ref, dskip_ref, wglu_ref, bglu_ref, ga_ref, gs_ref, wout_ref,
                         x0r_ref, x0i_ref, gffn_ref, wrt_ref, br_ref, tri_ref, cnt_in_ref,
                         hbuf_ref, hnbuf_ref, idxbuf_ref, gwbuf_ref, rankbuf_ref,
                         hmid_ref, knew_ref, vnew_ref, sre_ref, sim_ref, hn_ref, idx_ref, gw_ref, rank_ref, cnt_ref,
                         proj_s, qall_s, oall_s, kn8_s, vn8_s):
    del hbuf_ref, hnbuf_ref, idxbuf_ref, gwbuf_ref, rankbuf_ref
    g = pl.program_id(0)
    lo = lax.broadcasted_iota(jnp.int32, (DEC_BATCH, 128), 1) < HEAD_DIM

    @pl.when(g == 0)
    def _():
        proj = _bdot(_rms(x_ref[...], gmix_ref[...]), win_ref[...])
        proj_s[...] = proj
        for h in range(N_HEADS):
            jq, half, kv = h // 2, h % 2, h // 4
            q2 = proj[:, 128 * jq:128 * (jq + 1)] * (HEAD_DIM ** -0.5)
            if half != kv:
                q2 = pltpu.roll(q2, HEAD_DIM, 1)
            hr = slice(h * SPITCH, h * SPITCH + DEC_BATCH)
            qall_s[hr, :] = jnp.where(lo if kv == 0 else jnp.logical_not(lo), q2, 0.0)
            kn8_s[hr, :] = proj[:, D_ATTN:D_ATTN + D_KV]
            vn8_s[hr, :] = proj[:, D_ATTN + D_KV:D_ATTN + 2 * D_KV]

    def head_rows(ref):
        return jnp.stack([ref[pl.ds(g * SGRP + ii, N_HEADS, stride=SPITCH), :] for ii in range(SGRP)], axis=0)

    qe = head_rows(qall_s)
    knew = head_rows(kn8_s)
    vnew = head_rows(vn8_s)
    s = jnp.einsum('bhd,bkd->bhk', qe.astype(BF16), ck_ref[...].astype(BF16),
                   preferred_element_type=F32) + sbias_ref[...][None]
    s_new = jnp.sum(qe * knew, axis=-1, keepdims=True)
    sink = sinkc_ref[...][None]
    m = jnp.maximum(jnp.maximum(jnp.max(s, axis=-1, keepdims=True), s_new), sink)
    p = jnp.exp(s - m)
    p_new = jnp.exp(s_new - m)
    den = jnp.sum(p, axis=-1, keepdims=True) + p_new + jnp.exp(sink - m)
    o = (jnp.einsum('bhk,bkd->bhd', p.astype(BF16), cv_ref[...].astype(BF16),
                    preferred_element_type=F32) + p_new * vnew) / den
    for ii in range(SGRP):
        oall_s[pl.ds(g * SGRP + ii, N_HEADS, stride=SPITCH), :] = o[ii]

    @pl.when(g == N_SGRP - 1)
    def _():
        proj = proj_s[...]
        blocks = []
        for jq in range(N_HEADS // 2):
            oa = oall_s[(2 * jq) * SPITCH:(2 * jq) * SPITCH + DEC_BATCH, :]
            ob = oall_s[(2 * jq + 1) * SPITCH:(2 * jq + 1) * SPITCH + DEC_BATCH, :]
            if jq // 2 == 0:
                blocks.append(jnp.where(lo, oa, pltpu.roll(ob, HEAD_DIM, 1)))
            else:
                blocks.append(jnp.where(lo, pltpu.roll(oa, HEAD_DIM, 1), ob))
        attn = jnp.concatenate(blocks, axis=1)
        u = proj[:, D_ATTN + 2 * D_KV:]
        ys = []
        for j in range(N_SBLK):
            bu = _bdot(u[:, 128 * j:128 * (j + 1)], bblk_ref[j])
            lr = lamr_ref[j]
            li = lami_ref[j]
            cols = slice(SBLK * j, SBLK * (j + 1))
            sr = x0r_ref[:, cols]
            si = x0i_ref[:, cols]
            nr = lr * sr - li * si + bu[:, 0:SBLK]
            ni = lr * si + li * sr + bu[:, SBLK:2 * SBLK]
            sre_ref[:, cols] = nr
            sim_ref[:, cols] = ni
            ys.append(_bdot(jnp.concatenate([nr, ni], axis=1), cblk_ref[j]))
        sn = _ssm_post(jnp.concatenate(ys, axis=1), u, dskip_ref[...], wglu_ref[...], bglu_ref[...], gs_ref[...])
        an = _rms(attn, ga_ref[...])
        h = x_ref[...] + _bdot(an, wout_ref[0:D_ATTN, :]) + _bdot(sn, wout_ref[D_ATTN:2 * D_ATTN, :])
        h = jnp.concatenate([h, jnp.zeros((DEC_BATCH, D_MODEL), F32)], axis=0)
        hmid_ref[...] = h
        cnt_ref[...] = cnt_in_ref[...]
        _route(h, gffn_ref, wrt_ref, br_ref, tri_ref, cnt_ref, hn_ref, idx_ref, gw_ref, rank_ref)
        knew_ref[...] = proj[:, D_ATTN:D_ATTN + D_KV]
        vnew_ref[...] = proj[:, D_ATTN + D_KV:D_ATTN + 2 * D_KV]


def _mixer_sample(x_s, gmix, win, ck, cv, sbias, sinkc, bblk, lamr, lami, cblk, dskip, wglu, bglu, ga, gs, wout,
                  x0r, x0i, gffn, wrt, br, cnt_in, hbuf, hnbuf, idxbuf, gwbuf, rankbuf):
    nst = SSM_G * SSM_P
    in_specs = [
        _full((DEC_BATCH, D_MODEL)), _full((1, D_MODEL)), _full((D_MODEL, D_IN)),
        pl.BlockSpec((SGRP, WINDOW, D_KV), lambda g: (g, 0, 0)),
        pl.BlockSpec((SGRP, WINDOW, D_KV), lambda g: (g, 0, 0)),
        _full((N_HEADS, WINDOW)), _full((N_HEADS, 1)),
        _full((N_SBLK, 128, 2 * SBLK)), _full((N_SBLK, 1, SBLK)), _full((N_SBLK, 1, SBLK)),
        _full((N_SBLK, 2 * SBLK, 128)), _full((1, D_SSM)), _full((D_SSM, D_SSM)), _full((1, D_SSM)),
        _full((1, D_ATTN)), _full((1, D_SSM)), _full((D_MODEL, D_MODEL)),
        _full((DEC_BATCH, nst)), _full((DEC_BATCH, nst)),
        _full((1, D_MODEL)), _full((N_EXPERTS, D_MODEL)), _full((N_EXPERTS, 1)),
        _full((2 * DEC_BATCH, 2 * DEC_BATCH)), _full((N_EXPERTS, 1)),
    ] + [pl.BlockSpec(memory_space=pl.ANY)] * 5
    tail_block = SAMPLE_ROW0 // (2 * DEC_BATCH)
    route_specs, route_shapes = _route_specs(2 * DEC_BATCH, lambda g: tail_block)
    out_shape = [
        jax.ShapeDtypeStruct((T_PAD, D_MODEL), F32),
        jax.ShapeDtypeStruct((DEC_BATCH, D_KV), F32), jax.ShapeDtypeStruct((DEC_BATCH, D_KV), F32),
        jax.ShapeDtypeStruct((DEC_BATCH, nst), F32), jax.ShapeDtypeStruct((DEC_BATCH, nst), F32),
    ] + route_shapes
    out_specs = [
        pl.BlockSpec((2 * DEC_BATCH, D_MODEL), lambda g: (tail_block, 0)),
        _full((DEC_BATCH, D_KV)), _full((DEC_BATCH, D_KV)),
        _full((DEC_BATCH, nst)), _full((DEC_BATCH, nst)),
    ] + route_specs
    head_rows = pltpu.VMEM((N_HEADS * SPITCH, 128), F32)
    scratch = [pltpu.VMEM((DEC_BATCH, D_IN), F32), head_rows, head_rows, head_rows, head_rows]
    return pl.pallas_call(
        _mixer_sample_kernel, grid=(N_SGRP,), in_specs=in_specs, out_specs=out_specs, out_shape=out_shape,
        scratch_shapes=scratch, input_output_aliases={24: 0, 25: 5, 26: 6, 27: 7, 28: 8}, name="mixer_sample",
        compiler_params=pltpu.CompilerParams(dimension_semantics=("arbitrary",), vmem_limit_bytes=VMEM_LIMIT),
    )(x_s, gmix, win, ck, cv, sbias, sinkc, bblk, lamr, lami, cblk, dskip, wglu, bglu, ga, gs, wout, x0r, x0i,
      gffn, wrt, br, _tri(2 * DEC_BATCH), cnt_in, hbuf, hnbuf, idxbuf, gwbuf, rankbuf)


def _route(h, g_ref, wrt_ref, br_ref, tri_ref, cnt_ref, hn_ref, idx_ref, gw_ref, rank_ref):
    hn = _rms(h, g_ref[...])
    hn_ref[...] = _pack_pairs(hn)
    hn_hi = hn.astype(BF16)
    hn_lo = (hn - hn_hi.astype(F32)).astype(BF16)
    w = wrt_ref[...]
    w_hi = w.astype(BF16)
    w_lo = (w - w_hi.astype(F32)).astype(BF16)
    nt = (((1,), (1,)), ((), ()))
    logits = (lax.dot_general(w_hi, hn_hi, nt, preferred_element_type=F32)
              + lax.dot_general(w_lo, hn_hi, nt, preferred_element_type=F32)
              + lax.dot_general(w_hi, hn_lo, nt, preferred_element_type=F32)) + br_ref[...]
    eidx = lax.broadcasted_iota(jnp.int32, logits.shape, 0)
    vals, onehots = [], []
    l = logits
    for k in range(TOP_K):
        m = jnp.max(l, axis=0, keepdims=True)
        ik = jnp.min(jnp.where(l == m, eidx, N_EXPERTS), axis=0, keepdims=True)
        oh = eidx == ik
        idx_ref[k:k + 1, :] = ik
        vals.append(m)
        onehots.append(oh)
        l = jnp.where(oh, -jnp.inf, l)
    exps = [jnp.exp(v - vals[0]) for v in vals]
    den = exps[0] + exps[1] + exps[2] + exps[3]
    for k in range(TOP_K):
        gw_ref[k:k + 1, :] = exps[k] / den
    member = jnp.zeros(logits.shape, F32)
    for oh in onehots:
        member = member + jnp.where(oh, 1.0, 0.0)
    wblk = tri_ref.shape[0]
    base = cnt_ref[...].astype(F32)
    befores = []
    for cb in range(h.shape[0] // wblk):
        mblk = member[:, cb * wblk:(cb + 1) * wblk]
        befores.append(jnp.dot(mblk.astype(BF16), tri_ref[...], preferred_element_type=F32) + base)
        base = base + jnp.sum(mblk, axis=1, keepdims=True)
    before = jnp.concatenate(befores, axis=1)
    for k in range(TOP_K):
        rank_ref[k:k + 1, :] = jnp.sum(jnp.where(onehots[k], before, 0.0), axis=0, keepdims=True).astype(jnp.int32)
    cnt_ref[...] = base.astype(jnp.int32)


def _route_specs(rows, block):
    specs = [pl.BlockSpec((rows, D_PACK), lambda i: (block(i), 0)), pl.BlockSpec((TOP_K, rows), lambda i: (0, block(i))),
             pl.BlockSpec((TOP_K, rows), lambda i: (0, block(i))), pl.BlockSpec((TOP_K, rows), lambda i: (0, block(i))),
             _full((N_EXPERTS, 1))]
    shapes = [jax.ShapeDtypeStruct((T_PAD, D_PACK), jnp.int32), jax.ShapeDtypeStruct((TOP_K, T_PAD), jnp.int32),
              jax.ShapeDtypeStruct((TOP_K, T_PAD), F32), jax.ShapeDtypeStruct((TOP_K, T_PAD), jnp.int32),
              jax.ShapeDtypeStruct((N_EXPERTS, 1), jnp.int32)]
    return specs, shapes


def _tri(n):
    return jnp.asarray(np.triu(np.ones((n, n), np.float32), 1), BF16)


def _place_kernel(off_ref, idx_ref, rank_ref, pos_ref):
    idx = idx_ref[...]
    pos = rank_ref[...]
    for e in range(N_EXPERTS):
        pos = pos + jnp.where(idx == e, off_ref[e], 0)
    pos_ref[...] = pos


def _place(offsets, idx, rank):
    return pl.pallas_call(
        _place_kernel,
        in_specs=[pl.BlockSpec(memory_space=pltpu.SMEM), pl.BlockSpec(memory_space=pltpu.VMEM),
                  pl.BlockSpec(memory_space=pltpu.VMEM)],
        out_specs=pl.BlockSpec(memory_space=pltpu.VMEM),
        out_shape=jax.ShapeDtypeStruct((TOP_K, T_PAD), jnp.int32), name="place",
    )(offsets, idx, rank)


def _sc_mesh():
    return plsc.VectorSubcoreMesh(core_axis_name="core", subcore_axis_name="subcore")


def _sc_dispatch(rows, pos):
    n, d = rows.shape
    nblk = n // SC_ROWS
    pos_w = pos.reshape(TOP_K, nblk, SC_ROWS).transpose(1, 0, 2)

    @functools.partial(pl.kernel, out_type=jax.ShapeDtypeStruct((P_ROWS, d), rows.dtype), mesh=_sc_mesh(),
                       scratch_types=[], name="dispatch")
    def run(x_hbm, i_hbm, o_hbm):
        def body(x_vmem, i_vmem):
            for k in range(TOP_K):
                pltpu.sync_copy(x_vmem, o_hbm.at[i_vmem.at[0, k]])

        pltpu.emit_pipeline(
            body, grid=(nblk,),
            in_specs=[pl.BlockSpec((SC_ROWS, d), lambda i: (i, 0)),
                      pl.BlockSpec((1, TOP_K, SC_ROWS), lambda i: (i, 0, 0))],
            out_specs=[], core_axis_name=("core", "subcore"), dimension_semantics=(pltpu.PARALLEL,),
        )(x_hbm, i_hbm)

    return run(rows, pos_w)


def _sc_combine(rows, flat_pos):
    _, d = rows.shape
    n = flat_pos.shape[0]
    w = SC_ROWS_COMBINE
    nblk = n // w
    pos_w = flat_pos.reshape(nblk, 1, w)

    @functools.partial(pl.kernel, out_type=jax.ShapeDtypeStruct((n, d), rows.dtype), mesh=_sc_mesh(),
                       scratch_types=[], name="combine")
    def run(x_hbm, i_hbm, o_hbm):
        def body(i_vmem, o_vmem):
            pltpu.sync_copy(x_hbm.at[i_vmem.at[0, 0]], o_vmem)

        pltpu.emit_pipeline(
            body, grid=(nblk,),
            in_specs=[pl.BlockSpec((1, 1, w), lambda i: (i, 0, 0))],
            out_specs=[pl.BlockSpec((w, d), lambda i: (i, 0))],
            core_axis_name=("core", "subcore"), dimension_semantics=(pltpu.PARALLEL,),
        )(i_hbm, o_hbm)

    return run(rows, pos_w)


def _experts_kernel(ts_ref, trow_ref, tsz_ref, xs_hbm, w1_hbm, b1_ref, w2_hbm, b2_ref, ys_hbm,
                    w1_st, w2_st, w1_s, w2_s, xbuf, ybuf, wsem, xsem, ysem):
    e = pl.program_id(0)
    n_valid = ts_ref[N_EXPERTS]

    def w_copies(ex, slot):
        return (pltpu.make_async_copy(w1_hbm.at[ex], w1_st.at[slot], wsem.at[0, slot]),
                pltpu.make_async_copy(w2_hbm.at[ex], w2_st.at[slot], wsem.at[1, slot]))

    def x_copy(t, slot, rows):
        src = xs_hbm.at[pl.ds(pl.multiple_of(trow_ref[t], TM_HALF), rows)]
        return pltpu.make_async_copy(src, xbuf.at[slot, pl.ds(0, rows)], xsem.at[slot])

    def y_copy(t, slot, rows):
        dst = ys_hbm.at[pl.ds(pl.multiple_of(trow_ref[t], TM_HALF), rows)]
        return pltpu.make_async_copy(ybuf.at[slot, pl.ds(0, rows)], dst, ysem.at[slot])

    def by_size(t, fn):
        for rows in (TM, TM_HALF):
            @pl.when(tsz_ref[t] == rows)
            def _():
                fn(rows)

    @pl.when(e == 0)
    def _():
        for c in w_copies(0, 0):
            c.start()
        by_size(0, lambda rows: x_copy(0, 0, rows).start())

    @pl.when(e + 1 < N_EXPERTS)
    def _():
        for c in w_copies(e + 1, (e + 1) % 2):
            c.start()

    wslot = e % 2
    for c in w_copies(e, wslot):
        c.wait()
    for r in range(4):
        rs = slice(256 * r, 256 * (r + 1))
        w1_s[rs, :] = w1_st[wslot, rs, :].astype(BF16)
        w2_s[rs, :] = w2_st[wslot, rs, :].astype(BF16)

    def tile(t, carry):
        slot = t % 2

        @pl.when(t + 1 < n_valid)
        def _():
            by_size(t + 1, lambda rows: x_copy(t + 1, 1 - slot, rows).start())

        by_size(t, lambda rows: x_copy(t, slot, rows).wait())

        @pl.when(t >= 2)
        def _():
            by_size(t - 2, lambda rows: y_copy(t - 2, slot, rows).wait())

        def compute(rows):
            hdn = _bdot(_unpack_pairs(xbuf[slot, 0:rows]), w1_s[...]) + b1_ref[0]
            gt = jnp.minimum(hdn[:, :D_FF], SWIGLU_LIMIT)
            up = jnp.clip(hdn[:, D_FF:], -SWIGLU_LIMIT, SWIGLU_LIMIT)
            act = (up + 1.0) * gt * _sigmoid(SWIGLU_ALPHA * gt)
            ybuf[slot, 0:rows] = _pack_pairs(_bdot(act, w2_s[...]) + b2_ref[0])
            y_copy(t, slot, rows).start()

        by_size(t, compute)
        return carry

    lax.fori_loop(ts_ref[e], ts_ref[e + 1], tile, 0)

    @pl.when(e == N_EXPERTS - 1)
    def _():
        @pl.when(n_valid >= 2)
        def _():
            by_size(n_valid - 2, lambda rows: y_copy(n_valid - 2, n_valid % 2, rows).wait())

        by_size(n_valid - 1, lambda rows: y_copy(n_valid - 1, (n_valid - 1) % 2, rows).wait())


def _tile_plan(counts):
    units = (counts + (TM_HALF - 1)) // TM_HALF
    unit_end = jnp.cumsum(units)
    offsets = ((unit_end - units) * TM_HALF).astype(jnp.int32)
    n_full = units // 2
    tiles_per = n_full + units % 2
    tile_end = jnp.cumsum(tiles_per)
    tile_start = jnp.concatenate([jnp.zeros((1,), jnp.int32), tile_end.astype(jnp.int32)])
    t = jnp.arange(MAX_TILES, dtype=jnp.int32)[:, None]
    done = tile_end[None, :] <= t
    first_tile = jnp.max(jnp.where(done, tile_end[None, :], 0), axis=1)
    first_row = jnp.max(jnp.where(done, unit_end[None, :] * TM_HALF, 0), axis=1)
    is_half = jnp.any((units[None, :] % 2 == 1) & (t == tile_end[None, :] - 1), axis=1)
    valid = t[:, 0] < tile_end[-1]
    tile_row = jnp.where(valid, first_row + (t[:, 0] - first_tile) * TM, 0).astype(jnp.int32)
    tile_rows = jnp.where(valid, jnp.where(is_half, TM_HALF, TM), 0).astype(jnp.int32)
    return offsets, tile_start, tile_row, tile_rows


def _experts(tile_start, tile_row, tile_rows, xs, w_up, b_up, w_down, b_down):
    wsel = lambda e, *_: (e, 0, 0)
    hbm = pl.BlockSpec(memory_space=pl.ANY)
    grid_spec = pltpu.PrefetchScalarGridSpec(
        num_scalar_prefetch=3, grid=(N_EXPERTS,),
        in_specs=[hbm, hbm, pl.BlockSpec((1, 1, 2 * D_FF), wsel), hbm, pl.BlockSpec((1, 1, D_MODEL), wsel)],
        out_specs=hbm,
        scratch_shapes=[pltpu.VMEM((2, D_MODEL, 2 * D_FF), F32), pltpu.VMEM((2, D_FF, D_MODEL), F32),
                        pltpu.VMEM((D_MODEL, 2 * D_FF), BF16), pltpu.VMEM((D_FF, D_MODEL), BF16),
                        pltpu.VMEM((2, TM, D_PACK), jnp.int32), pltpu.VMEM((2, TM, D_PACK), jnp.int32),
                        pltpu.SemaphoreType.DMA((2, 2)), pltpu.SemaphoreType.DMA((2,)),
                        pltpu.SemaphoreType.DMA((2,))],
    )
    return pl.pallas_call(
        _experts_kernel, grid_spec=grid_spec, out_shape=jax.ShapeDtypeStruct((P_ROWS, D_PACK), jnp.int32),
        name="experts",
        compiler_params=pltpu.CompilerParams(dimension_semantics=("arbitrary",), vmem_limit_bytes=VMEM_LIMIT),
    )(tile_start, tile_row, tile_rows, xs, w_up, b_up.reshape(N_EXPERTS, 1, 2 * D_FF), w_down,
      b_down.reshape(N_EXPERTS, 1, D_MODEL))


def _final_kernel(h_ref, yg_ref, gw_ref, p_ref, gple_ref, wg_ref, wp_ref, gfin_ref, *rest):
    out_ref = rest[-1]
    rows = h_ref.shape[0]
    gw = gw_ref[...]
    h = h_ref[...]
    for k in range(TOP_K):
        h = h + gw[:, k:k + 1] * _unpack_pairs(yg_ref[k])
    gate = _sigmoid(_bdot(_rms(h, gple_ref[...]), wg_ref[...]))
    h = h + gate * _bdot(p_ref[...].reshape(rows, PLE_DIM), wp_ref[...])
    out_ref[...] = _rms(h, gfin_ref[...]).reshape(out_ref.shape)


def _final_prompt(part, hmid, yg, gwt, p_prompt, gple, wg, wp, gfin, y_prev):
    nb = 4
    rows = nb * CHUNK
    nbh = BATCH // nb
    c0 = part * PART_CHUNKS
    rb = lambda c, b: ((c0 + c) * nbh + b, 0)
    in_specs = [pl.BlockSpec((rows, D_MODEL), rb),
                pl.BlockSpec((TOP_K, rows, D_PACK), lambda c, b: (0, c * nbh + b, 0)),
                pl.BlockSpec((rows, TOP_K), rb),
                pl.BlockSpec((nb, CHUNK, PLE_DIM), lambda c, b: (b, c0 + c, 0)),
                _full((1, D_MODEL)), _full((D_MODEL, D_MODEL)), _full((PLE_DIM, D_MODEL)), _full((1, D_MODEL))]
    args = [hmid, yg, gwt, p_prompt, gple, wg, wp, gfin]
    aliases = {}
    if y_prev is not None:
        in_specs.append(pl.BlockSpec(memory_space=pl.ANY))
        args.append(y_prev)
        aliases = {len(args) - 1: 0}
    return pl.pallas_call(
        _final_kernel, grid=(PART_CHUNKS, nbh), in_specs=in_specs,
        out_specs=pl.BlockSpec((nb, CHUNK, D_MODEL), lambda c, b: (b, c0 + c, 0)),
        out_shape=jax.ShapeDtypeStruct((BATCH, SEQ, D_MODEL), F32), name="final_prompt",
        input_output_aliases=aliases,
        compiler_params=pltpu.CompilerParams(dimension_semantics=("arbitrary", "arbitrary"),
                                             vmem_limit_bytes=VMEM_LIMIT),
    )(*args)


def _final_sample(hmid, yg, gwt, p_sample, gple, wg, wp, gfin):
    blk = SAMPLE_ROW0 // DEC_BATCH
    return pl.pallas_call(
        _final_kernel, grid=(1,),
        in_specs=[pl.BlockSpec((DEC_BATCH, D_MODEL), lambda i: (blk, 0)),
                  pl.BlockSpec((TOP_K, DEC_BATCH, D_PACK), lambda i: (0, PART_CHUNKS * ROWS // DEC_BATCH, 0)),
                  pl.BlockSpec((DEC_BATCH, TOP_K), lambda i: (blk, 0)),
                  _full((DEC_BATCH, PLE_DIM)),
                  _full((1, D_MODEL)), _full((D_MODEL, D_MODEL)), _full((PLE_DIM, D_MODEL)), _full((1, D_MODEL))],
        out_specs=_full((DEC_BATCH, D_MODEL)),
        out_shape=jax.ShapeDtypeStruct((DEC_BATCH, D_MODEL), F32), name="final_sample",
        compiler_params=pltpu.CompilerParams(dimension_semantics=("arbitrary",), vmem_limit_bytes=VMEM_LIMIT),
    )(hmid, yg, gwt, p_sample, gple, wg, wp, gfin)


def _alibi_tables():
    slopes = 2.0 ** (-8.0 * (np.arange(N_HEADS, dtype=np.float64) + 1.0) / N_HEADS)
    i = np.arange(CHUNK)[:, None]
    j = np.arange(2 * CHUNK)[None, :]
    dist = i + CHUNK - j
    valid = (dist >= 0) & (dist <= WINDOW)
    tabs = []
    for has_prev in (False, True):
        ok = valid & ((j >= CHUNK) | has_prev)
        tabs.append(np.where(ok[None], -slopes[:, None, None] * dist[None], NEG))
    prompt = np.stack([np.stack([np.concatenate([t[h] for h in grp], axis=0) for grp in (NAT_HEADS, ROT_HEADS)])
                       for t in tabs]).astype(np.float32)
    wb = min(WINDOW, PAST_LEN)
    sample = (-slopes[:, None] * (wb - np.arange(wb))[None, :]).astype(np.float32)
    return prompt, sample


def kernel(x_prompt, x_sample, cache_k_win, cache_v_win, state_ssm_re, state_ssm_im, p_prompt, p_sample, norm_mix, w_in, sinks, ssm_lam_re, ssm_lam_im, ssm_log_step, ssm_b_re, ssm_b_im, ssm_c_re, ssm_c_im, ssm_d, w_glu, b_glu, norm_attn_out, norm_ssm_out, w_out, norm_ffn, w_router, b_router, w_up, b_up, w_down, b_down, norm_ple, w_ple_gate, w_ple_proj, norm_final):
    nst = SSM_G * SSM_P
    bias_np, sbias_np = _alibi_tables()
    bias = jnp.asarray(bias_np)
    sbias = jnp.asarray(sbias_np)

    lbr, lbi, bbr, bbi = _prep(ssm_lam_re[0], ssm_lam_im[0], ssm_log_step[0], ssm_b_re[0], ssm_b_im[0])
    bblk, lamr, lami, cblk = _s5_blocks(lbr, lbi, bbr, bbi, ssm_c_re[0], ssm_c_im[0])

    gmix = norm_mix[0].reshape(1, D_MODEL)
    win = w_in[0].astype(BF16)
    dskip = ssm_d[0].reshape(1, D_SSM)
    wglu = w_glu[0].astype(BF16)
    bglu = b_glu[0].reshape(1, D_SSM)
    ga = norm_attn_out[0].reshape(1, D_ATTN)
    gs = norm_ssm_out[0].reshape(1, D_SSM)
    wout = w_out[0].astype(BF16)
    sink = sinks[0]

    gffn = norm_ffn[0].reshape(1, D_MODEL)
    wrt = w_router[0].T
    br = b_router[0].reshape(N_EXPERTS, 1)
    hbuf, k_last, v_last, re_p, im_p, hnbuf, idxbuf, gwbuf, rankbuf, cnt_p = _mixer_prompt(
        x_prompt, sink, gmix, win, bias, bblk, lamr, lami, cblk, dskip, wglu, bglu, ga, gs, wout, gffn, wrt, br)

    ck = cache_k_win[0].reshape(DEC_BATCH, WINDOW, D_KV)
    cv = cache_v_win[0].reshape(DEC_BATCH, WINDOW, D_KV)
    hmid, k_new, v_new, re_s, im_s, hn, idx, gw, rank, counts = _mixer_sample(
        x_sample.reshape(DEC_BATCH, D_MODEL), gmix, win, ck, cv, sbias, sink.reshape(N_HEADS, 1), bblk, lamr, lami,
        cblk, dskip, wglu, bglu, ga, gs, wout, state_ssm_re[0].reshape(DEC_BATCH, nst),
        state_ssm_im[0].reshape(DEC_BATCH, nst), gffn, wrt, br, cnt_p, hbuf, hnbuf, idxbuf, gwbuf, rankbuf)

    offsets, tile_start, tile_row, tile_rows = _tile_plan(counts[:, 0])

    pos = _place(offsets, idx, rank)
    xs = _sc_dispatch(hn, pos)
    ys = _experts(tile_start, tile_row, tile_rows, xs, w_up[0], b_up[0], w_down[0], b_down[0])
    ygs = []
    for part in range(N_PARTS):
        r0 = part * PART_CHUNKS * ROWS
        r1 = T_PAD if part == N_PARTS - 1 else r0 + PART_CHUNKS * ROWS
        ygs.append(_sc_combine(ys, pos[:, r0:r1].reshape(TOP_K * (r1 - r0))).reshape(TOP_K, r1 - r0, D_PACK))

    gwt = gw.T
    gple = norm_ple[0].reshape(1, D_MODEL)
    wg = w_ple_gate[0].astype(BF16)
    wp = w_ple_proj[0].astype(BF16)
    gfin = norm_final.reshape(1, D_MODEL)
    y_prompt = None
    for part in range(N_PARTS):
        y_prompt = _final_prompt(part, hmid, ygs[part], gwt, p_prompt[0], gple, wg, wp, gfin, y_prompt)
    y_sample = _final_sample(hmid, ygs[-1], gwt, p_sample[0].reshape(DEC_BATCH, PLE_DIM), gple, wg, wp, gfin)

    k_win_s = jnp.concatenate([ck[:, 1:], k_new[:, None, :]], axis=1)
    v_win_s = jnp.concatenate([cv[:, 1:], v_new[:, None, :]], axis=1)
    kv5 = (1, BATCH, CHUNK, N_KV, HEAD_DIM)
    skv5 = (1, DEC_BATCH, WINDOW, N_KV, HEAD_DIM)
    return (y_prompt, y_sample.reshape(DEC_BATCH, 1, D_MODEL),
            k_last.reshape(kv5), v_last.reshape(kv5),
            re_p.reshape(1, BATCH, SSM_G, SSM_P), im_p.reshape(1, BATCH, SSM_G, SSM_P),
            k_win_s.reshape(skv5), v_win_s.reshape(skv5),
            re_s.reshape(1, DEC_BATCH, SSM_G, SSM_P), im_s.reshape(1, DEC_BATCH, SSM_G, SSM_P))
```

```python
import functools

import numpy as np
import jax
import jax.numpy as jnp
from jax import lax
from jax.experimental import pallas as pl
from jax.experimental.pallas import tpu as pltpu
from jax.experimental.pallas import tpu_sc as plsc
from jax._src.pallas import mpmd

F32 = jnp.float32
BF16 = jnp.bfloat16

D_MODEL = 1024
BATCH = 8
SEQ = 2048
DEC_BATCH = 128
PAST_LEN = 16384
HEAD_DIM = 64
D_ATTN = 512
N_HEADS = 8
N_KV = 2
D_KV = N_KV * HEAD_DIM
WINDOW = 128
D_SSM = 512
SSM_H = 16
SSM_G = 32
SSM_P = 64
D_IN = D_ATTN + 2 * D_KV + D_SSM
N_EXPERTS = 32
TOP_K = 4
D_FF = 1024
SWIGLU_LIMIT = 7.0
SWIGLU_ALPHA = 1.702
PLE_DIM = 256
EPS = 1e-5
NEG = -1e30

CHUNK = 128
N_CHUNKS = SEQ // CHUNK
ROWS = BATCH * CHUNK
PITCH = CHUNK + 8
NAT_HEADS = (0, 2, 5, 7)
ROT_HEADS = (1, 3, 4, 6)
SUB_T = 32
SUB_ROWS = SUB_T * BATCH
N_SBLK = 4
SBLK = 512
T_REAL = BATCH * SEQ + DEC_BATCH
T_PAD = T_REAL + 128
SAMPLE_ROW0 = BATCH * SEQ
ROUTE_BLOCK = 512
TM = 512
TM_HALF = 256
P_ROWS = (T_PAD * TOP_K + N_EXPERTS * (TM_HALF - 1)) // TM_HALF * TM_HALF
MAX_TILES = P_ROWS // TM + N_EXPERTS + 1
N_ROW_CHUNKS = 4
CHUNK_ROWS = 19456
P_ALLOC = N_ROW_CHUNKS * CHUNK_ROWS
DEST_SLICE = P_ALLOC // 32
K_STRIDE = 16896
TRASH_ROW = TOP_K * K_STRIDE
YG_ROWS = TRASH_ROW + 8
SC_ROWS = 40
SC_ROWS_COMBINE = 32
D_PACK = D_MODEL // 2
VMEM_LIMIT = 56 * 1024 * 1024


def _rms(x, g):
    return x * lax.rsqrt(jnp.mean(x * x, axis=-1, keepdims=True) + EPS) * g


def _sigmoid(x):
    return 1.0 / (1.0 + jnp.exp(-x))


def _gelu_tanh(x):
    c = np.float32(np.sqrt(2.0 / np.pi))
    return 0.5 * x * (1.0 + jnp.tanh(c * (x + 0.044715 * (x * x * x))))


def _bdot(a, b):
    return jnp.dot(a.astype(BF16), b, preferred_element_type=F32)


def _pack_pairs(x):
    n = x.shape[1] // 2
    lo = lax.bitcast_convert_type(x[:, :n].astype(BF16).astype(F32), jnp.int32)
    hi = lax.bitcast_convert_type(x[:, n:].astype(BF16).astype(F32), jnp.int32)
    return lax.shift_right_logical(lo, 16) | hi


def _unpack_pairs(w):
    lo = lax.bitcast_convert_type(lax.shift_left(w, 16), F32)
    hi = lax.bitcast_convert_type(w & jnp.int32(-65536), F32)
    return jnp.concatenate([lo, hi], axis=1)


def _full(shape):
    n = len(shape)
    return pl.BlockSpec(shape, lambda *_: (0,) * n)


def _prep_kernel(lr_ref, li_ref, ls_ref, br_ref, bi_ref, lbr_ref, lbi_ref, bbr_ref, bbi_ref):
    lr = lr_ref[...]
    li = li_ref[...]
    step = jnp.exp(ls_ref[...])
    zr = lr * step
    zi = li * step
    mag = jnp.exp(zr)
    lbr = mag * jnp.cos(zi)
    lbi = mag * jnp.sin(zi)
    lbr_ref[...] = lbr
    lbi_ref[...] = lbi
    nr = lbr - 1.0
    den = lr * lr + li * li
    cr = (nr * lr + lbi * li) / den
    ci = (lbi * lr - nr * li) / den
    br = br_ref[...]
    bi = bi_ref[...]
    bbr_ref[...] = cr * br - ci * bi
    bbi_ref[...] = cr * bi + ci * br


def _prep(lam_re, lam_im, log_step, b_re, b_im):
    g, p, h = SSM_G, SSM_P, SSM_H
    out = pl.pallas_call(
        _prep_kernel,
        out_shape=[jax.ShapeDtypeStruct((g, 1, p), F32), jax.ShapeDtypeStruct((g, 1, p), F32),
                   jax.ShapeDtypeStruct((g, h, p), F32), jax.ShapeDtypeStruct((g, h, p), F32)],
        name="s5_prep",
    )(lam_re.reshape(g, 1, p), lam_im.reshape(g, 1, p), log_step.reshape(g, 1, 1),
      jnp.transpose(b_re, (0, 2, 1)), jnp.transpose(b_im, (0, 2, 1)))
    return out


def _s5_blocks(lbr, lbi, bbr, bbi, c_re, c_im):
    eye = jnp.eye(8, dtype=F32)
    shp = (N_SBLK, 8, SSM_H, SSM_P)

    def in_map(b):
        return (b.reshape(shp)[:, :, :, None, :] * eye[None, :, None, :, None]).reshape(N_SBLK, 128, SBLK)

    def out_map(c):
        ct = jnp.transpose(c.reshape(shp), (0, 1, 3, 2))
        return (ct[:, :, :, None, :] * eye[None, :, None, :, None]).reshape(N_SBLK, SBLK, 128)

    bblk = jnp.concatenate([in_map(bbr), in_map(bbi)], axis=-1).astype(BF16)
    cblk = jnp.concatenate([out_map(c_re), -out_map(c_im)], axis=1).astype(BF16)
    return bblk, lbr.reshape(N_SBLK, 1, SBLK), lbi.reshape(N_SBLK, 1, SBLK), cblk


def _ssm_post(y_lin, u, dskip, wglu, bglu, gs):
    y = _gelu_tanh(y_lin + dskip * u)
    y = y * _sigmoid(_bdot(y, wglu) + bglu)
    return _rms(y, gs)


def _mixer_prompt_kernel(sinks_ref, x_ref, gmix_ref, win_ref, bias_ref, bblk_ref, lamr_ref, lami_ref,
                         cblk_ref, dskip_ref, wglu_ref, bglu_ref, ga_ref, gs_ref, wout_ref,
                         gffn_ref, wrt_ref, br_ref, tri_ref,
                         hmid_ref, klast_ref, vlast_ref, sre_ref, sim_ref, hn_ref, idx_ref, gw_ref, rank_ref, cnt_ref,
                         proj_s, u_s, kprev_s, kprevr_s, vprev_s, vprevr_s, attn_s, ssm_s, utb_s, bu_s, xs_s, ytb_s):
    c = pl.program_id(0)

    @pl.when(c == 0)
    def _():
        zkv = jnp.zeros(kprev_s.shape, BF16)
        kprev_s[...] = zkv
        kprevr_s[...] = zkv
        vprev_s[...] = zkv
        vprevr_s[...] = zkv
        sre_ref[...] = jnp.zeros(sre_ref.shape, F32)
        sim_ref[...] = jnp.zeros(sim_ref.shape, F32)
        cnt_ref[...] = jnp.zeros(cnt_ref.shape, jnp.int32)

    x = x_ref[...].reshape(ROWS, D_MODEL)
    proj = _bdot(_rms(x, gmix_ref[...]), win_ref[...])
    u0 = D_ATTN + 2 * D_KV
    proj_s[...] = proj[:, 0:u0]
    for jb in range(D_SSM // 128):
        for b in range(BATCH):
            u_s[jb, b * PITCH:b * PITCH + CHUNK, :] = proj[b * CHUNK:(b + 1) * CHUNK, u0 + 128 * jb:u0 + 128 * (jb + 1)]

    lo = lax.broadcasted_iota(jnp.int32, (CHUNK, 128), 1) < HEAD_DIM
    hi = jnp.logical_not(lo)
    table = jnp.minimum(c, 1)
    hrow = lax.broadcasted_iota(jnp.int32, (4 * CHUNK, 1), 0) // CHUNK

    def sink_col(heads):
        col = jnp.full((4 * CHUNK, 1), sinks_ref[heads[3]], F32)
        for n in (2, 1, 0):
            col = jnp.where(hrow == n, sinks_ref[heads[n]], col)
        return col

    sink_nat = sink_col(NAT_HEADS)
    sink_rot = sink_col(ROT_HEADS)

    def attend(q, k, v, bias, sink):
        s = lax.dot_general(q, k, (((1,), (1,)), ((), ())), preferred_element_type=F32) + bias
        m = jnp.maximum(jnp.max(s, axis=-1, keepdims=True), sink)
        p = jnp.exp(s - m)
        den = jnp.sum(p, axis=-1, keepdims=True) + jnp.exp(sink - m)
        return jnp.dot(p.astype(BF16), v, preferred_element_type=F32) / den

    def attn_body(b, carry):
        r0 = pl.multiple_of(b * CHUNK, CHUNK)
        rows = pl.ds(r0, CHUNK)
        kb = proj_s[rows, D_ATTN:D_ATTN + D_KV]
        vb = proj_s[rows, D_ATTN + D_KV:D_ATTN + 2 * D_KV]
        kb16 = kb.astype(BF16)
        vb16 = vb.astype(BF16)
        kbr16 = pltpu.roll(kb, HEAD_DIM, 1).astype(BF16)
        vbr16 = pltpu.roll(vb, HEAD_DIM, 1).astype(BF16)
        k_nat = jnp.concatenate([kprev_s[b], kb16], axis=0)
        k_rot = jnp.concatenate([kprevr_s[b], kbr16], axis=0)
        v_nat = jnp.concatenate([vprev_s[b], vb16], axis=0)
        v_rot = jnp.concatenate([vprevr_s[b], vbr16], axis=0)
        q2 = [proj_s[rows, 128 * jq:128 * (jq + 1)] * (HEAD_DIM ** -0.5) for jq in range(N_HEADS // 2)]
        q_nat = jnp.concatenate([jnp.where(lo if h % 2 == 0 else hi, q2[h // 2], 0.0) for h in NAT_HEADS],
                                axis=0).astype(BF16)
        q_rot = jnp.concatenate([jnp.where(lo if h % 2 == 0 else hi, q2[h // 2], 0.0) for h in ROT_HEADS],
                                axis=0).astype(BF16)
        o_nat = attend(q_nat, k_nat, v_nat, bias_ref[table, 0], sink_nat)
        o_rot = attend(q_rot, k_rot, v_rot, bias_ref[table, 1], sink_rot)
        for jq in range(N_HEADS // 2):
            blk = slice(CHUNK * jq, CHUNK * (jq + 1))
            even, odd = (o_nat, o_rot) if jq < 2 else (o_rot, o_nat)
            attn_s[rows, 128 * jq:128 * (jq + 1)] = jnp.where(lo, even[blk], odd[blk])
        kprev_s[b] = kb16
        kprevr_s[b] = kbr16
        vprev_s[b] = vb16
        vprevr_s[b] = vbr16
        return carry

    lax.fori_loop(0, BATCH, attn_body, 0)

    for sc in range(CHUNK // SUB_T):
        t0 = sc * SUB_T
        for i in range(SUB_T):
            for jb in range(D_SSM // 128):
                utb_s[i * BATCH:(i + 1) * BATCH, 128 * jb:128 * (jb + 1)] = (
                    u_s[jb, pl.ds(t0 + i, BATCH, stride=PITCH), :])
        u_tb = utb_s[...]
        for j in range(N_SBLK):
            bu_s[...] = _bdot(u_tb[:, 128 * j:128 * (j + 1)], bblk_ref[j])
            lr = jnp.broadcast_to(lamr_ref[j], (BATCH, SBLK))
            li = jnp.broadcast_to(lami_ref[j], (BATCH, SBLK))
            cols = slice(SBLK * j, SBLK * (j + 1))

            def step(i, carry):
                sr, si = carry
                r = pl.ds(pl.multiple_of(i * BATCH, BATCH), BATCH)
                nr = lr * sr - li * si + bu_s[r, 0:SBLK]
                ni = lr * si + li * sr + bu_s[r, SBLK:2 * SBLK]
                xs_s[r, 0:SBLK] = nr
                xs_s[r, SBLK:2 * SBLK] = ni
                return nr, ni

            sr, si = lax.fori_loop(0, SUB_T, step, (sre_ref[:, cols], sim_ref[:, cols]), unroll=True)
            sre_ref[:, cols] = sr
            sim_ref[:, cols] = si
            ytb_s[:, 128 * j:128 * (j + 1)] = _bdot(xs_s[...], cblk_ref[j])
        yn = _ssm_post(ytb_s[...], u_tb, dskip_ref[...], wglu_ref[...], bglu_ref[...], gs_ref[...])
        for i in range(SUB_T):
            for jb in range(D_SSM // 128):
                ssm_s[jb, pl.ds(t0 + i, BATCH, stride=PITCH), :] = (
                    yn[i * BATCH:(i + 1) * BATCH, 128 * jb:128 * (jb + 1)])

    an = _rms(attn_s[...], ga_ref[...])
    sn = jnp.concatenate(
        [jnp.concatenate([ssm_s[jb, b * PITCH:b * PITCH + CHUNK, :] for b in range(BATCH)], axis=0)
         for jb in range(D_SSM // 128)], axis=1)
    h = x + _bdot(an, wout_ref[0:D_ATTN, :]) + _bdot(sn, wout_ref[D_ATTN:2 * D_ATTN, :])
    hmid_ref[...] = h
    _route(h, gffn_ref, wrt_ref, br_ref, tri_ref, cnt_ref, hn_ref, idx_ref, gw_ref, rank_ref)

    @pl.when(c == N_CHUNKS - 1)
    def _():
        klast_ref[...] = proj_s[:, D_ATTN:D_ATTN + D_KV].reshape(BATCH, CHUNK, D_KV)
        vlast_ref[...] = proj_s[:, D_ATTN + D_KV:D_ATTN + 2 * D_KV].reshape(BATCH, CHUNK, D_KV)


def _mixer_prompt(x_prompt, sinks, gmix, win, bias, bblk, lamr, lami, cblk, dskip, wglu, bglu, ga, gs, wout,
                  gffn, wrt, br):
    smem = pl.BlockSpec(memory_space=pltpu.SMEM)
    in_specs = [
        smem,
        pl.BlockSpec((BATCH, CHUNK, D_MODEL), lambda c: (0, c, 0)),
        _full((1, D_MODEL)), _full((D_MODEL, D_IN)), _full((2, 2, 4 * CHUNK, 2 * CHUNK)),
        _full((N_SBLK, 128, 2 * SBLK)), _full((N_SBLK, 1, SBLK)), _full((N_SBLK, 1, SBLK)),
        _full((N_SBLK, 2 * SBLK, 128)), _full((1, D_SSM)), _full((D_SSM, D_SSM)), _full((1, D_SSM)),
        _full((1, D_ATTN)), _full((1, D_SSM)), _full((D_MODEL, D_MODEL)),
        _full((1, D_MODEL)), _full((N_EXPERTS, D_MODEL)), _full((N_EXPERTS, 1)), _full((ROUTE_BLOCK, ROUTE_BLOCK)),
    ]
    route_specs, route_shapes = _route_specs(ROWS, lambda c: c)
    out_shape = [
        jax.ShapeDtypeStruct((T_PAD, D_MODEL), F32),
        jax.ShapeDtypeStruct((BATCH, CHUNK, D_KV), F32),
        jax.ShapeDtypeStruct((BATCH, CHUNK, D_KV), F32),
        jax.ShapeDtypeStruct((BATCH, SSM_G * SSM_P), F32),
        jax.ShapeDtypeStruct((BATCH, SSM_G * SSM_P), F32),
    ] + route_shapes
    out_specs = [
        pl.BlockSpec((ROWS, D_MODEL), lambda c: (c, 0)),
        _full((BATCH, CHUNK, D_KV)), _full((BATCH, CHUNK, D_KV)),
        _full((BATCH, SSM_G * SSM_P)), _full((BATCH, SSM_G * SSM_P)),
    ] + route_specs
    kv_scr = pltpu.VMEM((BATCH, CHUNK, D_KV), BF16)
    scratch = [
        pltpu.VMEM((ROWS, D_ATTN + 2 * D_KV), F32), pltpu.VMEM((D_SSM // 128, BATCH * PITCH, 128), F32),
        kv_scr, kv_scr, kv_scr, kv_scr,
        pltpu.VMEM((ROWS, D_ATTN), F32), pltpu.VMEM((D_SSM // 128, BATCH * PITCH, 128), F32),
        pltpu.VMEM((SUB_ROWS, D_SSM), F32), pltpu.VMEM((SUB_ROWS, 2 * SBLK), F32),
        pltpu.VMEM((SUB_ROWS, 2 * SBLK), F32), pltpu.VMEM((SUB_ROWS, D_SSM), F32),
    ]
    return pl.pallas_call(
        _mixer_prompt_kernel, grid=(N_CHUNKS,), in_specs=in_specs, out_specs=out_specs, out_shape=out_shape,
        scratch_shapes=scratch, name="mixer_prompt",
        compiler_params=pltpu.CompilerParams(dimension_semantics=("arbitrary",), vmem_limit_bytes=VMEM_LIMIT),
    )(sinks, x_prompt, gmix, win, bias, bblk, lamr, lami, cblk, dskip, wglu, bglu, ga, gs, wout,
      gffn, wrt, br, _tri(ROUTE_BLOCK))


SGRP = 16
SPITCH = DEC_BATCH + 8
N_SGRP = DEC_BATCH // SGRP


def _mixer_sample_kernel(x_ref, gmix_ref, win_ref, ck_ref, cv_ref, sbias_ref, sinkc_ref, bblk_ref, lamr_ref,
                         lami_ref, cblk_ref, dskip_ref, wglu_ref, bglu_ref, ga_ref, gs_ref, wout_ref,
                         x0r_ref, x0i_ref, gffn_ref, wrt_ref, br_ref, tri_ref, cnt_in_ref,
                         hbuf_ref, hnbuf_ref, idxbuf_ref, gwbuf_ref, rankbuf_ref,
                         hmid_ref, knew_ref, vnew_ref, sre_ref, sim_ref, hn_ref, idx_ref, gw_ref, rank_ref, cnt_ref,
                         proj_s, qall_s, oall_s, kn8_s, vn8_s):
    del hbuf_ref, hnbuf_ref, idxbuf_ref, gwbuf_ref, rankbuf_ref
    g = pl.program_id(0)
    lo = lax.broadcasted_iota(jnp.int32, (DEC_BATCH, 128), 1) < HEAD_DIM

    @pl.when(g == 0)
    def _():
        proj = _bdot(_rms(x_ref[...], gmix_ref[...]), win_ref[...])
        proj_s[...] = proj
        for h in range(N_HEADS):
            jq, half, kv = h // 2, h % 2, h // 4
            q2 = proj[:, 128 * jq:128 * (jq + 1)] * (HEAD_DIM ** -0.5)
            if half != kv:
                q2 = pltpu.roll(q2, HEAD_DIM, 1)
            hr = slice(h * SPITCH, h * SPITCH + DEC_BATCH)
            qall_s[hr, :] = jnp.where(lo if kv == 0 else jnp.logical_not(lo), q2, 0.0)
            kn8_s[hr, :] = proj[:, D_ATTN:D_ATTN + D_KV]
            vn8_s[hr, :] = proj[:, D_ATTN + D_KV:D_ATTN + 2 * D_KV]

    def head_rows(ref):
        return jnp.stack([ref[pl.ds(g * SGRP + ii, N_HEADS, stride=SPITCH), :] for ii in range(SGRP)], axis=0)

    qe = head_rows(qall_s)
    knew = head_rows(kn8_s)
    vnew = head_rows(vn8_s)
    s = jnp.einsum('bhd,bkd->bhk', qe.astype(BF16), ck_ref[...].astype(BF16),
                   preferred_element_type=F32) + sbias_ref[...][None]
    s_new = jnp.sum(qe * knew, axis=-1, keepdims=True)
    sink = sinkc_ref[...][None]
    m = jnp.maximum(jnp.maximum(jnp.max(s, axis=-1, keepdims=True), s_new), sink)
    p = jnp.exp(s - m)
    p_new = jnp.exp(s_new - m)
    den = jnp.sum(p, axis=-1, keepdims=True) + p_new + jnp.exp(sink - m)
    o = (jnp.einsum('bhk,bkd->bhd', p.astype(BF16), cv_ref[...].astype(BF16),
                    preferred_element_type=F32) + p_new * vnew) / den
    for ii in range(SGRP):
        oall_s[pl.ds(g * SGRP + ii, N_HEADS, stride=SPITCH), :] = o[ii]

    @pl.when(g == N_SGRP - 1)
    def _():
        proj = proj_s[...]
        blocks = []
        for jq in range(N_HEADS // 2):
            oa = oall_s[(2 * jq) * SPITCH:(2 * jq) * SPITCH + DEC_BATCH, :]
            ob = oall_s[(2 * jq + 1) * SPITCH:(2 * jq + 1) * SPITCH + DEC_BATCH, :]
            if jq // 2 == 0:
                blocks.append(jnp.where(lo, oa, pltpu.roll(ob, HEAD_DIM, 1)))
            else:
                blocks.append(jnp.where(lo, pltpu.roll(oa, HEAD_DIM, 1), ob))
        attn = jnp.concatenate(blocks, axis=1)
        u = proj[:, D_ATTN + 2 * D_KV:]
        ys = []
        for j in range(N_SBLK):
            bu = _bdot(u[:, 128 * j:128 * (j + 1)], bblk_ref[j])
            lr = lamr_ref[j]
            li = lami_ref[j]
            cols = slice(SBLK * j, SBLK * (j + 1))
            sr = x0r_ref[:, cols]
            si = x0i_ref[:, cols]
            nr = lr * sr - li * si + bu[:, 0:SBLK]
            ni = lr * si + li * sr + bu[:, SBLK:2 * SBLK]
            sre_ref[:, cols] = nr
            sim_ref[:, cols] = ni
            ys.append(_bdot(jnp.concatenate([nr, ni], axis=1), cblk_ref[j]))
        sn = _ssm_post(jnp.concatenate(ys, axis=1), u, dskip_ref[...], wglu_ref[...], bglu_ref[...], gs_ref[...])
        an = _rms(attn, ga_ref[...])
        h = x_ref[...] + _bdot(an, wout_ref[0:D_ATTN, :]) + _bdot(sn, wout_ref[D_ATTN:2 * D_ATTN, :])
        h = jnp.concatenate([h, jnp.zeros((DEC_BATCH, D_MODEL), F32)], axis=0)
        hmid_ref[...] = h
        cnt_ref[...] = cnt_in_ref[...]
        _route(h, gffn_ref, wrt_ref, br_ref, tri_ref, cnt_ref, hn_ref, idx_ref, gw_ref, rank_ref)
        knew_ref[...] = proj[:, D_ATTN:D_ATTN + D_KV]
        vnew_ref[...] = proj[:, D_ATTN + D_KV:D_ATTN + 2 * D_KV]


def _mixer_sample(x_s, gmix, win, ck, cv, sbias, sinkc, bblk, lamr, lami, cblk, dskip, wglu, bglu, ga, gs, wout,
                  x0r, x0i, gffn, wrt, br, cnt_in, hbuf, hnbuf, idxbuf, gwbuf, rankbuf):
    nst = SSM_G * SSM_P
    in_specs = [
        _full((DEC_BATCH, D_MODEL)), _full((1, D_MODEL)), _full((D_MODEL, D_IN)),
        pl.BlockSpec((SGRP, WINDOW, D_KV), lambda g: (g, 0, 0)),
        pl.BlockSpec((SGRP, WINDOW, D_KV), lambda g: (g, 0, 0)),
        _full((N_HEADS, WINDOW)), _full((N_HEADS, 1)),
        _full((N_SBLK, 128, 2 * SBLK)), _full((N_SBLK, 1, SBLK)), _full((N_SBLK, 1, SBLK)),
        _full((N_SBLK, 2 * SBLK, 128)), _full((1, D_SSM)), _full((D_SSM, D_SSM)), _full((1, D_SSM)),
        _full((1, D_ATTN)), _full((1, D_SSM)), _full((D_MODEL, D_MODEL)),
        _full((DEC_BATCH, nst)), _full((DEC_BATCH, nst)),
        _full((1, D_MODEL)), _full((N_EXPERTS, D_MODEL)), _full((N_EXPERTS, 1)),
        _full((2 * DEC_BATCH, 2 * DEC_BATCH)), _full((N_EXPERTS, 1)),
    ] + [pl.BlockSpec(memory_space=pl.ANY)] * 5
    tail_block = SAMPLE_ROW0 // (2 * DEC_BATCH)
    route_specs, route_shapes = _route_specs(2 * DEC_BATCH, lambda g: tail_block)
    out_shape = [
        jax.ShapeDtypeStruct((T_PAD, D_MODEL), F32),
        jax.ShapeDtypeStruct((DEC_BATCH, D_KV), F32), jax.ShapeDtypeStruct((DEC_BATCH, D_KV), F32),
        jax.ShapeDtypeStruct((DEC_BATCH, nst), F32), jax.ShapeDtypeStruct((DEC_BATCH, nst), F32),
    ] + route_shapes
    out_specs = [
        pl.BlockSpec((2 * DEC_BATCH, D_MODEL), lambda g: (tail_block, 0)),
        _full((DEC_BATCH, D_KV)), _full((DEC_BATCH, D_KV)),
        _full((DEC_BATCH, nst)), _full((DEC_BATCH, nst)),
    ] + route_specs
    head_rows = pltpu.VMEM((N_HEADS * SPITCH, 128), F32)
    scratch = [pltpu.VMEM((DEC_BATCH, D_IN), F32), head_rows, head_rows, head_rows, head_rows]
    return pl.pallas_call(
        _mixer_sample_kernel, grid=(N_SGRP,), in_specs=in_specs, out_specs=out_specs, out_shape=out_shape,
        scratch_shapes=scratch, input_output_aliases={24: 0, 25: 5, 26: 6, 27: 7, 28: 8}, name="mixer_sample",
        compiler_params=pltpu.CompilerParams(dimension_semantics=("arbitrary",), vmem_limit_bytes=VMEM_LIMIT),
    )(x_s, gmix, win, ck, cv, sbias, sinkc, bblk, lamr, lami, cblk, dskip, wglu, bglu, ga, gs, wout, x0r, x0i,
      gffn, wrt, br, _tri(2 * DEC_BATCH), cnt_in, hbuf, hnbuf, idxbuf, gwbuf, rankbuf)


def _route(h, g_ref, wrt_ref, br_ref, tri_ref, cnt_ref, hn_ref, idx_ref, gw_ref, rank_ref):
    hn = _rms(h, g_ref[...])
    hn_ref[...] = _pack_pairs(hn)
    hn_hi = hn.astype(BF16)
    hn_lo = (hn - hn_hi.astype(F32)).astype(BF16)
    w = wrt_ref[...]
    w_hi = w.astype(BF16)
    w_lo = (w - w_hi.astype(F32)).astype(BF16)
    nt = (((1,), (1,)), ((), ()))
    logits = (lax.dot_general(w_hi, hn_hi, nt, preferred_element_type=F32)
              + lax.dot_general(w_lo, hn_hi, nt, preferred_element_type=F32)
              + lax.dot_general(w_hi, hn_lo, nt, preferred_element_type=F32)) + br_ref[...]
    eidx = lax.broadcasted_iota(jnp.int32, logits.shape, 0)
    vals, onehots = [], []
    l = logits
    for k in range(TOP_K):
        m = jnp.max(l, axis=0, keepdims=True)
        ik = jnp.min(jnp.where(l == m, eidx, N_EXPERTS), axis=0, keepdims=True)
        oh = eidx == ik
        idx_ref[k:k + 1, :] = ik
        vals.append(m)
        onehots.append(oh)
        l = jnp.where(oh, -jnp.inf, l)
    exps = [jnp.exp(v - vals[0]) for v in vals]
    den = exps[0] + exps[1] + exps[2] + exps[3]
    for k in range(TOP_K):
        gw_ref[k:k + 1, :] = exps[k] / den
    member = jnp.zeros(logits.shape, F32)
    for oh in onehots:
        member = member + jnp.where(oh, 1.0, 0.0)
    wblk = tri_ref.shape[0]
    base = cnt_ref[...].astype(F32)
    befores = []
    for cb in range(h.shape[0] // wblk):
        mblk = member[:, cb * wblk:(cb + 1) * wblk]
        befores.append(jnp.dot(mblk.astype(BF16), tri_ref[...], preferred_element_type=F32) + base)
        base = base + jnp.sum(mblk, axis=1, keepdims=True)
    before = jnp.concatenate(befores, axis=1)
    for k in range(TOP_K):
        rank_ref[k:k + 1, :] = jnp.sum(jnp.where(onehots[k], before, 0.0), axis=0, keepdims=True).astype(jnp.int32)
    cnt_ref[...] = base.astype(jnp.int32)


def _route_specs(rows, block):
    specs = [pl.BlockSpec((rows, D_PACK), lambda i: (block(i), 0)), pl.BlockSpec((TOP_K, rows), lambda i: (0, block(i))),
             pl.BlockSpec((TOP_K, rows), lambda i: (0, block(i))), pl.BlockSpec((TOP_K, rows), lambda i: (0, block(i))),
             _full((N_EXPERTS, 1))]
    shapes = [jax.ShapeDtypeStruct((T_PAD, D_PACK), jnp.int32), jax.ShapeDtypeStruct((TOP_K, T_PAD), jnp.int32),
              jax.ShapeDtypeStruct((TOP_K, T_PAD), F32), jax.ShapeDtypeStruct((TOP_K, T_PAD), jnp.int32),
              jax.ShapeDtypeStruct((N_EXPERTS, 1), jnp.int32)]
    return specs, shapes


def _tri(n):
    return jnp.asarray(np.triu(np.ones((n, n), np.float32), 1), BF16)


def _place_kernel(off_ref, idx_ref, rank_ref, pos_ref):
    idx = idx_ref[...]
    pos = rank_ref[...]
    for e in range(N_EXPERTS):
        pos = pos + jnp.where(idx == e, off_ref[e], 0)
    pos_ref[...] = pos


def _place(offsets, idx, rank):
    return pl.pallas_call(
        _place_kernel,
        in_specs=[pl.BlockSpec(memory_space=pltpu.SMEM), pl.BlockSpec(memory_space=pltpu.VMEM),
                  pl.BlockSpec(memory_space=pltpu.VMEM)],
        out_specs=pl.BlockSpec(memory_space=pltpu.VMEM),
        out_shape=jax.ShapeDtypeStruct((TOP_K, T_PAD), jnp.int32), name="place",
    )(offsets, idx, rank)


def _sc_mesh():
    return plsc.VectorSubcoreMesh(core_axis_name="core", subcore_axis_name="subcore")


def _sc_dispatch(rows, pos):
    n, d = rows.shape
    nblk = n // SC_ROWS
    pos_w = pos.reshape(TOP_K, nblk, SC_ROWS).transpose(1, 0, 2)

    @functools.partial(pl.kernel, out_type=jax.ShapeDtypeStruct((P_ALLOC, d), rows.dtype), mesh=_sc_mesh(),
                       scratch_types=[], name="dispatch")
    def run(x_hbm, i_hbm, o_hbm):
        def body(x_vmem, i_vmem):
            for k in range(TOP_K):
                pltpu.sync_copy(x_vmem, o_hbm.at[i_vmem.at[0, k]])

        pltpu.emit_pipeline(
            body, grid=(nblk,),
            in_specs=[pl.BlockSpec((SC_ROWS, d), lambda i: (i, 0)),
                      pl.BlockSpec((1, TOP_K, SC_ROWS), lambda i: (i, 0, 0))],
            out_specs=[], core_axis_name=("core", "subcore"), dimension_semantics=(pltpu.PARALLEL,),
        )(x_hbm, i_hbm)

    return run(rows, pos_w)


def _sc_build_dest(flat_pos, flat_ids):
    n = flat_pos.shape[0]
    stage = n // 32
    lanes = 16
    params = pltpu.CompilerParams(needs_layout_passes=False)

    @functools.partial(pl.kernel, out_type=jax.ShapeDtypeStruct((P_ALLOC,), jnp.int32), mesh=_sc_mesh(),
                       scratch_types=[pltpu.VMEM((DEST_SLICE,), jnp.int32), pltpu.VMEM((stage,), jnp.int32),
                                      pltpu.VMEM((stage,), jnp.int32)],
                       compiler_params=params, name="build_dest")
    def run(pos_hbm, ids_hbm, o_hbm, local, pbuf, ibuf):
        lo = (lax.axis_index("core") * 16 + lax.axis_index("subcore")) * DEST_SLICE

        @pl.loop(0, DEST_SLICE, step=lanes)
        def _(i):
            local[pl.ds(i, lanes)] = jnp.full((lanes,), TRASH_ROW, jnp.int32)

        @pl.loop(0, n // stage)
        def _(c):
            pltpu.sync_copy(pos_hbm.at[pl.ds(c * stage, stage)], pbuf)
            pltpu.sync_copy(ids_hbm.at[pl.ds(c * stage, stage)], ibuf)

            @pl.loop(0, stage, step=lanes)
            def _(j):
                rel = pbuf[pl.ds(j, lanes)] - lo
                plsc.store_scatter(local, [rel], ibuf[pl.ds(j, lanes)], mask=(rel >= 0) & (rel < DEST_SLICE))

        pltpu.sync_copy(local, o_hbm.at[pl.ds(lo, DEST_SLICE)])

    return run(flat_pos, flat_ids)


def _sc_combine_scatter(rows, dest, yg_prev, chunk):
    w = SC_ROWS_COMBINE
    nblk = CHUNK_ROWS // w
    b0 = chunk * nblk
    dest_w = dest.reshape(P_ALLOC // w, 1, w)

    def body_fn(x_hbm, i_hbm, *rest):
        o_hbm = rest[-1]

        def body(x_vmem, i_vmem):
            pltpu.sync_copy(x_vmem, o_hbm.at[i_vmem.at[0, 0]])

        pltpu.emit_pipeline(
            body, grid=(nblk,),
            in_specs=[pl.BlockSpec((w, D_PACK), lambda i: (b0 + i, 0)),
                      pl.BlockSpec((1, 1, w), lambda i: (b0 + i, 0, 0))],
            out_specs=[], core_axis_name=("core", "subcore"), dimension_semantics=(pltpu.PARALLEL,),
        )(x_hbm, i_hbm)

    out_type = jax.ShapeDtypeStruct((YG_ROWS, D_PACK), rows.dtype)
    if yg_prev is None:
        return pl.kernel(body_fn, out_type=out_type, mesh=_sc_mesh(), scratch_types=[],
                         name="combine")(rows, dest_w)
    return mpmd._mpmd_map([(_sc_mesh(), body_fn)], out_type, input_output_aliases={2: 0}, scratch_types=(),
                          name="combine")(rows, dest_w, yg_prev)


def _experts_kernel(ts_ref, trow_ref, tsz_ref, rng_ref, xs_hbm, w1_hbm, b1_hbm, w2_hbm, b2_hbm, ys_hbm,
                    w1_st, w2_st, b1_st, b2_st, w1_s, w2_s, xbuf, ybuf, wsem, xsem, ysem):
    elo, ehi, tlo, thi = rng_ref[0], rng_ref[1], rng_ref[2], rng_ref[3]

    def w_copies(ex, slot):
        return (pltpu.make_async_copy(w1_hbm.at[ex], w1_st.at[slot], wsem.at[0, slot]),
                pltpu.make_async_copy(w2_hbm.at[ex], w2_st.at[slot], wsem.at[1, slot]),
                pltpu.make_async_copy(b1_hbm.at[ex], b1_st.at[slot], wsem.at[2, slot]),
                pltpu.make_async_copy(b2_hbm.at[ex], b2_st.at[slot], wsem.at[3, slot]))

    def x_copy(t, slot, rows):
        src = xs_hbm.at[pl.ds(pl.multiple_of(trow_ref[t], TM_HALF), rows)]
        return pltpu.make_async_copy(src, xbuf.at[slot, pl.ds(0, rows)], xsem.at[slot])

    def y_copy(t, slot, rows):
        dst = ys_hbm.at[pl.ds(pl.multiple_of(trow_ref[t], TM_HALF), rows)]
        return pltpu.make_async_copy(ybuf.at[slot, pl.ds(0, rows)], dst, ysem.at[slot])

    def by_size(t, fn):
        for rows in (TM, TM_HALF):
            @pl.when(tsz_ref[t] == rows)
            def _():
                fn(rows)

    @pl.when(tlo < thi)
    def _():
        for c in w_copies(elo, 0):
            c.start()
        by_size(tlo, lambda rows: x_copy(tlo, tlo % 2, rows).start())

    def expert(e, carry):
        wslot = (e - elo) % 2

        @pl.when(e + 1 < ehi)
        def _():
            for c in w_copies(e + 1, 1 - wslot):
                c.start()

        for c in w_copies(e, wslot):
            c.wait()
        for r in range(4):
            rs = slice(256 * r, 256 * (r + 1))
            w1_s[rs, :] = w1_st[wslot, rs, :].astype(BF16)
            w2_s[rs, :] = w2_st[wslot, rs, :].astype(BF16)
        b1 = b1_st[wslot]
        b2 = b2_st[wslot]

        def tile(t, carry):
            slot = t % 2

            @pl.when(t + 1 < thi)
            def _():
                by_size(t + 1, lambda rows: x_copy(t + 1, 1 - slot, rows).start())

            by_size(t, lambda rows: x_copy(t, slot, rows).wait())

            @pl.when(t - 2 >= tlo)
            def _():
                by_size(t - 2, lambda rows: y_copy(t - 2, slot, rows).wait())

            def compute(rows):
                hdn = _bdot(_unpack_pairs(xbuf[slot, 0:rows]), w1_s[...]) + b1
                gt = jnp.minimum(hdn[:, :D_FF], SWIGLU_LIMIT)
                up = jnp.clip(hdn[:, D_FF:], -SWIGLU_LIMIT, SWIGLU_LIMIT)
                act = (up + 1.0) * gt * _sigmoid(SWIGLU_ALPHA * gt)
                ybuf[slot, 0:rows] = _pack_pairs(_bdot(act, w2_s[...]) + b2)
                y_copy(t, slot, rows).start()

            by_size(t, compute)
            return carry

        lax.fori_loop(jnp.maximum(ts_ref[e], tlo), jnp.minimum(ts_ref[e + 1], thi), tile, 0)
        return carry

    lax.fori_loop(elo, ehi, expert, 0)

    @pl.when(thi - tlo >= 2)
    def _():
        by_size(thi - 2, lambda rows: y_copy(thi - 2, thi % 2, rows).wait())

    @pl.when(thi - tlo >= 1)
    def _():
        by_size(thi - 1, lambda rows: y_copy(thi - 1, (thi - 1) % 2, rows).wait())


def _tile_plan(counts):
    units = (counts + (TM_HALF - 1)) // TM_HALF
    unit_end = jnp.cumsum(units)
    offsets = ((unit_end - units) * TM_HALF).astype(jnp.int32)
    n_full = units // 2
    tiles_per = n_full + units % 2
    tile_end = jnp.cumsum(tiles_per)
    tile_start = jnp.concatenate([jnp.zeros((1,), jnp.int32), tile_end.astype(jnp.int32)])
    t = jnp.arange(MAX_TILES, dtype=jnp.int32)[:, None]
    done = tile_end[None, :] <= t
    first_tile = jnp.max(jnp.where(done, tile_end[None, :], 0), axis=1)
    first_row = jnp.max(jnp.where(done, unit_end[None, :] * TM_HALF, 0), axis=1)
    is_half = jnp.any((units[None, :] % 2 == 1) & (t == tile_end[None, :] - 1), axis=1)
    valid = t[:, 0] < tile_end[-1]
    tile_row = jnp.where(valid, first_row + (t[:, 0] - first_tile) * TM, 0).astype(jnp.int32)
    tile_rows = jnp.where(valid, jnp.where(is_half, TM_HALF, TM), 0).astype(jnp.int32)
    return offsets, tile_start, tile_row, tile_rows


def _chunk_ranges(tile_start, tile_row, tile_rows):
    valid = tile_rows > 0
    tile_end = tile_start[1:]
    out = []
    for c in range(N_ROW_CHUNKS):
        r0, r1 = c * CHUNK_ROWS, (c + 1) * CHUNK_ROWS
        tlo = jnp.sum((valid & (tile_row + tile_rows <= r0)).astype(jnp.int32))
        thi = jnp.sum((valid & (tile_row < r1)).astype(jnp.int32))
        elo = jnp.sum((tile_end <= tlo).astype(jnp.int32))
        ehi = jnp.where(thi > tlo, jnp.sum((tile_end <= thi - 1).astype(jnp.int32)) + 1, elo)
        out.append(jnp.stack([elo, ehi, tlo, thi]).astype(jnp.int32))
    return out


def _experts(tile_start, tile_row, tile_rows, chunk_range, xs, w_up, b_up, w_down, b_down):
    hbm = pl.BlockSpec(memory_space=pl.ANY)
    grid_spec = pltpu.PrefetchScalarGridSpec(
        num_scalar_prefetch=4, grid=(1,),
        in_specs=[hbm, hbm, hbm, hbm, hbm],
        out_specs=hbm,
        scratch_shapes=[pltpu.VMEM((2, D_MODEL, 2 * D_FF), F32), pltpu.VMEM((2, D_FF, D_MODEL), F32),
                        pltpu.VMEM((2, 1, 2 * D_FF), F32), pltpu.VMEM((2, 1, D_MODEL), F32),
                        pltpu.VMEM((D_MODEL, 2 * D_FF), BF16), pltpu.VMEM((D_FF, D_MODEL), BF16),
                        pltpu.VMEM((2, TM, D_PACK), jnp.int32), pltpu.VMEM((2, TM, D_PACK), jnp.int32),
                        pltpu.SemaphoreType.DMA((4, 2)), pltpu.SemaphoreType.DMA((2,)),
                        pltpu.SemaphoreType.DMA((2,))],
    )
    return pl.pallas_call(
        _experts_kernel, grid_spec=grid_spec, out_shape=jax.ShapeDtypeStruct((P_ALLOC, D_PACK), jnp.int32),
        name="experts",
        compiler_params=pltpu.CompilerParams(dimension_semantics=("arbitrary",), vmem_limit_bytes=VMEM_LIMIT),
    )(tile_start, tile_row, tile_rows, chunk_range, xs, w_up, b_up.reshape(N_EXPERTS, 1, 2 * D_FF), w_down,
      b_down.reshape(N_EXPERTS, 1, D_MODEL))


def _final_kernel(h_ref, yg0_ref, yg1_ref, yg2_ref, yg3_ref, gw_ref, p_ref, gple_ref, wg_ref, wp_ref, gfin_ref,
                  out_ref):
    rows = h_ref.shape[0]
    gw = gw_ref[...]
    h = h_ref[...]
    for k, yg_ref in enumerate((yg0_ref, yg1_ref, yg2_ref, yg3_ref)):
        h = h + gw[:, k:k + 1] * _unpack_pairs(yg_ref[...])
    gate = _sigmoid(_bdot(_rms(h, gple_ref[...]), wg_ref[...]))
    h = h + gate * _bdot(p_ref[...].reshape(rows, PLE_DIM), wp_ref[...])
    out_ref[...] = _rms(h, gfin_ref[...]).reshape(out_ref.shape)


def _final_prompt(hmid, yg, gwt, p_prompt, gple, wg, wp, gfin):
    nb = 4
    rows = nb * CHUNK
    nbh = BATCH // nb
    rb = lambda c, b: (c * nbh + b, 0)
    slot = lambda k: pl.BlockSpec((rows, D_PACK), lambda c, b: (k * (K_STRIDE // rows) + c * nbh + b, 0))
    return pl.pallas_call(
        _final_kernel, grid=(N_CHUNKS, nbh),
        in_specs=[pl.BlockSpec((rows, D_MODEL), rb), slot(0), slot(1), slot(2), slot(3),
                  pl.BlockSpec((rows, TOP_K), rb),
                  pl.BlockSpec((nb, CHUNK, PLE_DIM), lambda c, b: (b, c, 0)),
                  _full((1, D_MODEL)), _full((D_MODEL, D_MODEL)), _full((PLE_DIM, D_MODEL)), _full((1, D_MODEL))],
        out_specs=pl.BlockSpec((nb, CHUNK, D_MODEL), lambda c, b: (b, c, 0)),
        out_shape=jax.ShapeDtypeStruct((BATCH, SEQ, D_MODEL), F32), name="final_prompt",
        compiler_params=pltpu.CompilerParams(dimension_semantics=("arbitrary", "arbitrary"),
                                             vmem_limit_bytes=VMEM_LIMIT),
    )(hmid, yg, yg, yg, yg, gwt, p_prompt, gple, wg, wp, gfin)


def _final_sample(hmid, yg, gwt, p_sample, gple, wg, wp, gfin):
    blk = SAMPLE_ROW0 // DEC_BATCH
    slot = lambda k: pl.BlockSpec((DEC_BATCH, D_PACK), lambda i: (k * (K_STRIDE // DEC_BATCH) + blk, 0))
    return pl.pallas_call(
        _final_kernel, grid=(1,),
        in_specs=[pl.BlockSpec((DEC_BATCH, D_MODEL), lambda i: (blk, 0)), slot(0), slot(1), slot(2), slot(3),
                  pl.BlockSpec((DEC_BATCH, TOP_K), lambda i: (blk, 0)),
                  _full((DEC_BATCH, PLE_DIM)),
                  _full((1, D_MODEL)), _full((D_MODEL, D_MODEL)), _full((PLE_DIM, D_MODEL)), _full((1, D_MODEL))],
        out_specs=_full((DEC_BATCH, D_MODEL)),
        out_shape=jax.ShapeDtypeStruct((DEC_BATCH, D_MODEL), F32), name="final_sample",
        compiler_params=pltpu.CompilerParams(dimension_semantics=("arbitrary",), vmem_limit_bytes=VMEM_LIMIT),
    )(hmid, yg, yg, yg, yg, gwt, p_sample, gple, wg, wp, gfin)


def _alibi_tables():
    slopes = 2.0 ** (-8.0 * (np.arange(N_HEADS, dtype=np.float64) + 1.0) / N_HEADS)
    i = np.arange(CHUNK)[:, None]
    j = np.arange(2 * CHUNK)[None, :]
    dist = i + CHUNK - j
    valid = (dist >= 0) & (dist <= WINDOW)
    tabs = []
    for has_prev in (False, True):
        ok = valid & ((j >= CHUNK) | has_prev)
        tabs.append(np.where(ok[None], -slopes[:, None, None] * dist[None], NEG))
    prompt = np.stack([np.stack([np.concatenate([t[h] for h in grp], axis=0) for grp in (NAT_HEADS, ROT_HEADS)])
                       for t in tabs]).astype(np.float32)
    wb = min(WINDOW, PAST_LEN)
    sample = (-slopes[:, None] * (wb - np.arange(wb))[None, :]).astype(np.float32)
    return prompt, sample


def kernel(x_prompt, x_sample, cache_k_win, cache_v_win, state_ssm_re, state_ssm_im, p_prompt, p_sample, norm_mix, w_in, sinks, ssm_lam_re, ssm_lam_im, ssm_log_step, ssm_b_re, ssm_b_im, ssm_c_re, ssm_c_im, ssm_d, w_glu, b_glu, norm_attn_out, norm_ssm_out, w_out, norm_ffn, w_router, b_router, w_up, b_up, w_down, b_down, norm_ple, w_ple_gate, w_ple_proj, norm_final):
    nst = SSM_G * SSM_P
    bias_np, sbias_np = _alibi_tables()
    bias = jnp.asarray(bias_np)
    sbias = jnp.asarray(sbias_np)

    lbr, lbi, bbr, bbi = _prep(ssm_lam_re[0], ssm_lam_im[0], ssm_log_step[0], ssm_b_re[0], ssm_b_im[0])
    bblk, lamr, lami, cblk = _s5_blocks(lbr, lbi, bbr, bbi, ssm_c_re[0], ssm_c_im[0])

    gmix = norm_mix[0].reshape(1, D_MODEL)
    win = w_in[0].astype(BF16)
    dskip = ssm_d[0].reshape(1, D_SSM)
    wglu = w_glu[0].astype(BF16)
    bglu = b_glu[0].reshape(1, D_SSM)
    ga = norm_attn_out[0].reshape(1, D_ATTN)
    gs = norm_ssm_out[0].reshape(1, D_SSM)
    wout = w_out[0].astype(BF16)
    sink = sinks[0]

    gffn = norm_ffn[0].reshape(1, D_MODEL)
    wrt = w_router[0].T
    br = b_router[0].reshape(N_EXPERTS, 1)
    hbuf, k_last, v_last, re_p, im_p, hnbuf, idxbuf, gwbuf, rankbuf, cnt_p = _mixer_prompt(
        x_prompt, sink, gmix, win, bias, bblk, lamr, lami, cblk, dskip, wglu, bglu, ga, gs, wout, gffn, wrt, br)

    ck = cache_k_win[0].reshape(DEC_BATCH, WINDOW, D_KV)
    cv = cache_v_win[0].reshape(DEC_BATCH, WINDOW, D_KV)
    hmid, k_new, v_new, re_s, im_s, hn, idx, gw, rank, counts = _mixer_sample(
        x_sample.reshape(DEC_BATCH, D_MODEL), gmix, win, ck, cv, sbias, sink.reshape(N_HEADS, 1), bblk, lamr, lami,
        cblk, dskip, wglu, bglu, ga, gs, wout, state_ssm_re[0].reshape(DEC_BATCH, nst),
        state_ssm_im[0].reshape(DEC_BATCH, nst), gffn, wrt, br, cnt_p, hbuf, hnbuf, idxbuf, gwbuf, rankbuf)

    offsets, tile_start, tile_row, tile_rows = _tile_plan(counts[:, 0])

    pos = _place(offsets, idx, rank)
    xs = _sc_dispatch(hn, pos)
    pair_ids = (jnp.arange(TOP_K, dtype=jnp.int32)[:, None] * K_STRIDE + jnp.arange(T_PAD, dtype=jnp.int32)[None, :])
    dest = _sc_build_dest(pos.reshape(TOP_K * T_PAD), pair_ids.reshape(TOP_K * T_PAD))
    yg = None
    for c, chunk_range in enumerate(_chunk_ranges(tile_start, tile_row, tile_rows)):
        ys = _experts(tile_start, tile_row, tile_rows, chunk_range, xs, w_up[0], b_up[0], w_down[0], b_down[0])
        yg = _sc_combine_scatter(ys, dest, yg, c)

    gwt = gw.T
    gple = norm_ple[0].reshape(1, D_MODEL)
    wg = w_ple_gate[0].astype(BF16)
    wp = w_ple_proj[0].astype(BF16)
    gfin = norm_final.reshape(1, D_MODEL)
    y_prompt = _final_prompt(hmid, yg, gwt, p_prompt[0], gple, wg, wp, gfin)
    y_sample = _final_sample(hmid, yg, gwt, p_sample[0].reshape(DEC_BATCH, PLE_DIM), gple, wg, wp, gfin)

    k_win_s = jnp.concatenate([ck[:, 1:], k_new[:, None, :]], axis=1)
    v_win_s = jnp.concatenate([cv[:, 1:], v_new[:, None, :]], axis=1)
    kv5 = (1, BATCH, CHUNK, N_KV, HEAD_DIM)
    skv5 = (1, DEC_BATCH, WINDOW, N_KV, HEAD_DIM)
    return (y_prompt, y_sample.reshape(DEC_BATCH, 1, D_MODEL),
            k_last.reshape(kv5), v_last.reshape(kv5),
            re_p.reshape(1, BATCH, SSM_G, SSM_P), im_p.reshape(1, BATCH, SSM_G, SSM_P),
            k_win_s.reshape(skv5), v_win_s.reshape(skv5),
            re_s.reshape(1, DEC_BATCH, SSM_G, SSM_P), im_s.reshape(1, DEC_BATCH, SSM_G, SSM_P))
```

```python
import functools

import numpy as np
import jax
import jax.numpy as jnp
from jax import lax
from jax.experimental import pallas as pl
from jax.experimental.pallas import tpu as pltpu
from jax.experimental.pallas import tpu_sc as plsc
from jax._src.pallas import mpmd

F32 = jnp.float32
BF16 = jnp.bfloat16

D_MODEL = 1024
BATCH = 8
SEQ = 2048
DEC_BATCH = 128
PAST_LEN = 16384
HEAD_DIM = 64
D_ATTN = 512
N_HEADS = 8
N_KV = 2
D_KV = N_KV * HEAD_DIM
WINDOW = 128
D_SSM = 512
SSM_H = 16
SSM_G = 32
SSM_P = 64
D_IN = D_ATTN + 2 * D_KV + D_SSM
N_EXPERTS = 32
TOP_K = 4
D_FF = 1024
SWIGLU_LIMIT = 7.0
SWIGLU_ALPHA = 1.702
PLE_DIM = 256
EPS = 1e-5
NEG = -1e30

CHUNK = 128
N_CHUNKS = SEQ // CHUNK
ROWS = BATCH * CHUNK
PITCH = CHUNK + 8
NAT_HEADS = (0, 2, 5, 7)
ROT_HEADS = (1, 3, 4, 6)
SUB_T = 32
SUB_ROWS = SUB_T * BATCH
N_SBLK = 4
SBLK = 512
T_REAL = BATCH * SEQ + DEC_BATCH
T_PAD = T_REAL + 128
SAMPLE_ROW0 = BATCH * SEQ
ROUTE_BLOCK = 512
TM = 512
TM_HALF = 256
P_ROWS = (T_PAD * TOP_K + N_EXPERTS * (TM_HALF - 1)) // TM_HALF * TM_HALF
MAX_TILES = P_ROWS // TM + N_EXPERTS + 1
CHUNK_UNIT = 1024
CHUNK_UNITS = (26, 24, 18, 8)
CHUNK_BOUNDS = tuple(CHUNK_UNIT * sum(CHUNK_UNITS[:c]) for c in range(len(CHUNK_UNITS) + 1))
P_ALLOC = CHUNK_BOUNDS[-1]
DEST_SLICE = P_ALLOC // 32
K_STRIDE = 16896
TRASH_ROW = TOP_K * K_STRIDE
N_TRASH = 2048
YG_ROWS = TRASH_ROW + N_TRASH
SC_ROWS = 40
SC_ROWS_COMBINE = 32
D_PACK = D_MODEL // 2
VMEM_LIMIT = 56 * 1024 * 1024


def _rms(x, g):
    return x * lax.rsqrt(jnp.mean(x * x, axis=-1, keepdims=True) + EPS) * g


def _sigmoid(x):
    return 1.0 / (1.0 + jnp.exp(-x))


def _gelu_tanh(x):
    c = np.float32(np.sqrt(2.0 / np.pi))
    return 0.5 * x * (1.0 + jnp.tanh(c * (x + 0.044715 * (x * x * x))))


def _bdot(a, b):
    return jnp.dot(a.astype(BF16), b, preferred_element_type=F32)


def _pack_pairs(x):
    n = x.shape[1] // 2
    lo = lax.bitcast_convert_type(x[:, :n].astype(BF16).astype(F32), jnp.int32)
    hi = lax.bitcast_convert_type(x[:, n:].astype(BF16).astype(F32), jnp.int32)
    return lax.shift_right_logical(lo, 16) | hi


def _unpack_pairs(w):
    lo = lax.bitcast_convert_type(lax.shift_left(w, 16), F32)
    hi = lax.bitcast_convert_type(w & jnp.int32(-65536), F32)
    return jnp.concatenate([lo, hi], axis=1)


def _full(shape):
    n = len(shape)
    return pl.BlockSpec(shape, lambda *_: (0,) * n)


def _prep_kernel(lr_ref, li_ref, ls_ref, br_ref, bi_ref, lbr_ref, lbi_ref, bbr_ref, bbi_ref):
    lr = lr_ref[...]
    li = li_ref[...]
    step = jnp.exp(ls_ref[...])
    zr = lr * step
    zi = li * step
    mag = jnp.exp(zr)
    lbr = mag * jnp.cos(zi)
    lbi = mag * jnp.sin(zi)
    lbr_ref[...] = lbr
    lbi_ref[...] = lbi
    nr = lbr - 1.0
    den = lr * lr + li * li
    cr = (nr * lr + lbi * li) / den
    ci = (lbi * lr - nr * li) / den
    br = br_ref[...]
    bi = bi_ref[...]
    bbr_ref[...] = cr * br - ci * bi
    bbi_ref[...] = cr * bi + ci * br


def _prep(lam_re, lam_im, log_step, b_re, b_im):
    g, p, h = SSM_G, SSM_P, SSM_H
    out = pl.pallas_call(
        _prep_kernel,
        out_shape=[jax.ShapeDtypeStruct((g, 1, p), F32), jax.ShapeDtypeStruct((g, 1, p), F32),
                   jax.ShapeDtypeStruct((g, h, p), F32), jax.ShapeDtypeStruct((g, h, p), F32)],
        name="s5_prep",
    )(lam_re.reshape(g, 1, p), lam_im.reshape(g, 1, p), log_step.reshape(g, 1, 1),
      jnp.transpose(b_re, (0, 2, 1)), jnp.transpose(b_im, (0, 2, 1)))
    return out


def _s5_blocks(lbr, lbi, bbr, bbi, c_re, c_im):
    eye = jnp.eye(8, dtype=F32)
    shp = (N_SBLK, 8, SSM_H, SSM_P)

    def in_map(b):
        return (b.reshape(shp)[:, :, :, None, :] * eye[None, :, None, :, None]).reshape(N_SBLK, 128, SBLK)

    def out_map(c):
        ct = jnp.transpose(c.reshape(shp), (0, 1, 3, 2))
        return (ct[:, :, :, None, :] * eye[None, :, None, :, None]).reshape(N_SBLK, SBLK, 128)

    bblk = jnp.concatenate([in_map(bbr), in_map(bbi)], axis=-1).astype(BF16)
    cblk = jnp.concatenate([out_map(c_re), -out_map(c_im)], axis=1).astype(BF16)
    return bblk, lbr.reshape(N_SBLK, 1, SBLK), lbi.reshape(N_SBLK, 1, SBLK), cblk


def _ssm_post(y_lin, u, dskip, wglu, bglu, gs):
    y = _gelu_tanh(y_lin + dskip * u)
    y = y * _sigmoid(_bdot(y, wglu) + bglu)
    return _rms(y, gs)


def _mixer_prompt_kernel(sinks_ref, x_ref, gmix_ref, win_ref, bias_ref, bblk_ref, lamr_ref, lami_ref,
                         cblk_ref, dskip_ref, wglu_ref, bglu_ref, ga_ref, gs_ref, wout_ref,
                         gffn_ref, wrt_ref, br_ref, tri_ref,
                         hmid_ref, klast_ref, vlast_ref, sre_ref, sim_ref, hn_ref, idx_ref, gw_ref, rank_ref, cnt_ref,
                         proj_s, u_s, kprev_s, kprevr_s, vprev_s, vprevr_s, attn_s, ssm_s, utb_s, bu_s, xs_s, ytb_s):
    c = pl.program_id(0)

    @pl.when(c == 0)
    def _():
        zkv = jnp.zeros(kprev_s.shape, BF16)
        kprev_s[...] = zkv
        kprevr_s[...] = zkv
        vprev_s[...] = zkv
        vprevr_s[...] = zkv
        sre_ref[...] = jnp.zeros(sre_ref.shape, F32)
        sim_ref[...] = jnp.zeros(sim_ref.shape, F32)
        cnt_ref[...] = jnp.zeros(cnt_ref.shape, jnp.int32)

    x = x_ref[...].reshape(ROWS, D_MODEL)
    proj = _bdot(_rms(x, gmix_ref[...]), win_ref[...])
    u0 = D_ATTN + 2 * D_KV
    proj_s[...] = proj[:, 0:u0]
    for jb in range(D_SSM // 128):
        for b in range(BATCH):
            u_s[jb, b * PITCH:b * PITCH + CHUNK, :] = proj[b * CHUNK:(b + 1) * CHUNK, u0 + 128 * jb:u0 + 128 * (jb + 1)]

    lo = lax.broadcasted_iota(jnp.int32, (CHUNK, 128), 1) < HEAD_DIM
    hi = jnp.logical_not(lo)
    table = jnp.minimum(c, 1)
    hrow = lax.broadcasted_iota(jnp.int32, (4 * CHUNK, 1), 0) // CHUNK

    def sink_col(heads):
        col = jnp.full((4 * CHUNK, 1), sinks_ref[heads[3]], F32)
        for n in (2, 1, 0):
            col = jnp.where(hrow == n, sinks_ref[heads[n]], col)
        return col

    sink_nat = sink_col(NAT_HEADS)
    sink_rot = sink_col(ROT_HEADS)

    def attend(q, k, v, bias, sink):
        s = lax.dot_general(q, k, (((1,), (1,)), ((), ())), preferred_element_type=F32) + bias
        m = jnp.maximum(jnp.max(s, axis=-1, keepdims=True), sink)
        p = jnp.exp(s - m)
        den = jnp.sum(p, axis=-1, keepdims=True) + jnp.exp(sink - m)
        return jnp.dot(p.astype(BF16), v, preferred_element_type=F32) / den

    def attn_body(b, carry):
        r0 = pl.multiple_of(b * CHUNK, CHUNK)
        rows = pl.ds(r0, CHUNK)
        kb = proj_s[rows, D_ATTN:D_ATTN + D_KV]
        vb = proj_s[rows, D_ATTN + D_KV:D_ATTN + 2 * D_KV]
        kb16 = kb.astype(BF16)
        vb16 = vb.astype(BF16)
        kbr16 = pltpu.roll(kb, HEAD_DIM, 1).astype(BF16)
        vbr16 = pltpu.roll(vb, HEAD_DIM, 1).astype(BF16)
        k_nat = jnp.concatenate([kprev_s[b], kb16], axis=0)
        k_rot = jnp.concatenate([kprevr_s[b], kbr16], axis=0)
        v_nat = jnp.concatenate([vprev_s[b], vb16], axis=0)
        v_rot = jnp.concatenate([vprevr_s[b], vbr16], axis=0)
        q2 = [proj_s[rows, 128 * jq:128 * (jq + 1)] * (HEAD_DIM ** -0.5) for jq in range(N_HEADS // 2)]
        q_nat = jnp.concatenate([jnp.where(lo if h % 2 == 0 else hi, q2[h // 2], 0.0) for h in NAT_HEADS],
                                axis=0).astype(BF16)
        q_rot = jnp.concatenate([jnp.where(lo if h % 2 == 0 else hi, q2[h // 2], 0.0) for h in ROT_HEADS],
                                axis=0).astype(BF16)
        o_nat = attend(q_nat, k_nat, v_nat, bias_ref[table, 0], sink_nat)
        o_rot = attend(q_rot, k_rot, v_rot, bias_ref[table, 1], sink_rot)
        for jq in range(N_HEADS // 2):
            blk = slice(CHUNK * jq, CHUNK * (jq + 1))
            even, odd = (o_nat, o_rot) if jq < 2 else (o_rot, o_nat)
            attn_s[rows, 128 * jq:128 * (jq + 1)] = jnp.where(lo, even[blk], odd[blk])
        kprev_s[b] = kb16
        kprevr_s[b] = kbr16
        vprev_s[b] = vb16
        vprevr_s[b] = vbr16
        return carry

    lax.fori_loop(0, BATCH, attn_body, 0)

    for sc in range(CHUNK // SUB_T):
        t0 = sc * SUB_T
        for i in range(SUB_T):
            for jb in range(D_SSM // 128):
                utb_s[i * BATCH:(i + 1) * BATCH, 128 * jb:128 * (jb + 1)] = (
                    u_s[jb, pl.ds(t0 + i, BATCH, stride=PITCH), :])
        u_tb = utb_s[...]
        for j in range(N_SBLK):
            bu_s[...] = _bdot(u_tb[:, 128 * j:128 * (j + 1)], bblk_ref[j])
            lr = jnp.broadcast_to(lamr_ref[j], (BATCH, SBLK))
            li = jnp.broadcast_to(lami_ref[j], (BATCH, SBLK))
            cols = slice(SBLK * j, SBLK * (j + 1))

            def step(i, carry):
                sr, si = carry
                r = pl.ds(pl.multiple_of(i * BATCH, BATCH), BATCH)
                nr = lr * sr - li * si + bu_s[r, 0:SBLK]
                ni = lr * si + li * sr + bu_s[r, SBLK:2 * SBLK]
                xs_s[r, 0:SBLK] = nr
                xs_s[r, SBLK:2 * SBLK] = ni
                return nr, ni

            sr, si = lax.fori_loop(0, SUB_T, step, (sre_ref[:, cols], sim_ref[:, cols]), unroll=True)
            sre_ref[:, cols] = sr
            sim_ref[:, cols] = si
            ytb_s[:, 128 * j:128 * (j + 1)] = _bdot(xs_s[...], cblk_ref[j])
        yn = _ssm_post(ytb_s[...], u_tb, dskip_ref[...], wglu_ref[...], bglu_ref[...], gs_ref[...])
        for i in range(SUB_T):
            for jb in range(D_SSM // 128):
                ssm_s[jb, pl.ds(t0 + i, BATCH, stride=PITCH), :] = (
                    yn[i * BATCH:(i + 1) * BATCH, 128 * jb:128 * (jb + 1)])

    an = _rms(attn_s[...], ga_ref[...])
    sn = jnp.concatenate(
        [jnp.concatenate([ssm_s[jb, b * PITCH:b * PITCH + CHUNK, :] for b in range(BATCH)], axis=0)
         for jb in range(D_SSM // 128)], axis=1)
    h = x + _bdot(an, wout_ref[0:D_ATTN, :]) + _bdot(sn, wout_ref[D_ATTN:2 * D_ATTN, :])
    hmid_ref[...] = h
    _route(h, gffn_ref, wrt_ref, br_ref, tri_ref, cnt_ref, hn_ref, idx_ref, gw_ref, rank_ref)

    @pl.when(c == N_CHUNKS - 1)
    def _():
        klast_ref[...] = proj_s[:, D_ATTN:D_ATTN + D_KV].reshape(BATCH, CHUNK, D_KV)
        vlast_ref[...] = proj_s[:, D_ATTN + D_KV:D_ATTN + 2 * D_KV].reshape(BATCH, CHUNK, D_KV)


def _mixer_prompt(x_prompt, sinks, gmix, win, bias, bblk, lamr, lami, cblk, dskip, wglu, bglu, ga, gs, wout,
                  gffn, wrt, br):
    smem = pl.BlockSpec(memory_space=pltpu.SMEM)
    in_specs = [
        smem,
        pl.BlockSpec((BATCH, CHUNK, D_MODEL), lambda c: (0, c, 0)),
        _full((1, D_MODEL)), _full((D_MODEL, D_IN)), _full((2, 2, 4 * CHUNK, 2 * CHUNK)),
        _full((N_SBLK, 128, 2 * SBLK)), _full((N_SBLK, 1, SBLK)), _full((N_SBLK, 1, SBLK)),
        _full((N_SBLK, 2 * SBLK, 128)), _full((1, D_SSM)), _full((D_SSM, D_SSM)), _full((1, D_SSM)),
        _full((1, D_ATTN)), _full((1, D_SSM)), _full((D_MODEL, D_MODEL)),
        _full((1, D_MODEL)), _full((N_EXPERTS, D_MODEL)), _full((N_EXPERTS, 1)), _full((ROUTE_BLOCK, ROUTE_BLOCK)),
    ]
    route_specs, route_shapes = _route_specs(ROWS, lambda c: c)
    out_shape = [
        jax.ShapeDtypeStruct((T_PAD, D_MODEL), F32),
        jax.ShapeDtypeStruct((BATCH, CHUNK, D_KV), F32),
        jax.ShapeDtypeStruct((BATCH, CHUNK, D_KV), F32),
        jax.ShapeDtypeStruct((BATCH, SSM_G * SSM_P), F32),
        jax.ShapeDtypeStruct((BATCH, SSM_G * SSM_P), F32),
    ] + route_shapes
    out_specs = [
        pl.BlockSpec((ROWS, D_MODEL), lambda c: (c, 0)),
        _full((BATCH, CHUNK, D_KV)), _full((BATCH, CHUNK, D_KV)),
        _full((BATCH, SSM_G * SSM_P)), _full((BATCH, SSM_G * SSM_P)),
    ] + route_specs
    kv_scr = pltpu.VMEM((BATCH, CHUNK, D_KV), BF16)
    scratch = [
        pltpu.VMEM((ROWS, D_ATTN + 2 * D_KV), F32), pltpu.VMEM((D_SSM // 128, BATCH * PITCH, 128), F32),
        kv_scr, kv_scr, kv_scr, kv_scr,
        pltpu.VMEM((ROWS, D_ATTN), F32), pltpu.VMEM((D_SSM // 128, BATCH * PITCH, 128), F32),
        pltpu.VMEM((SUB_ROWS, D_SSM), F32), pltpu.VMEM((SUB_ROWS, 2 * SBLK), F32),
        pltpu.VMEM((SUB_ROWS, 2 * SBLK), F32), pltpu.VMEM((SUB_ROWS, D_SSM), F32),
    ]
    return pl.pallas_call(
        _mixer_prompt_kernel, grid=(N_CHUNKS,), in_specs=in_specs, out_specs=out_specs, out_shape=out_shape,
        scratch_shapes=scratch, name="mixer_prompt",
        compiler_params=pltpu.CompilerParams(dimension_semantics=("arbitrary",), vmem_limit_bytes=VMEM_LIMIT),
    )(sinks, x_prompt, gmix, win, bias, bblk, lamr, lami, cblk, dskip, wglu, bglu, ga, gs, wout,
      gffn, wrt, br, _tri(ROUTE_BLOCK))


SGRP = 16
SPITCH = DEC_BATCH + 8
N_SGRP = DEC_BATCH // SGRP


def _mixer_sample_kernel(x_ref, gmix_ref, win_ref, ck_ref, cv_ref, sbias_ref, sinkc_ref, bblk_ref, lamr_ref,
                         lami_ref, cblk_ref, dskip_ref, wglu_ref, bglu_ref, ga_ref, gs_ref, wout_ref,
                         x0r_ref, x0i_ref, gffn_ref, wrt_ref, br_ref, tri_ref, cnt_in_ref,
                         hbuf_ref, hnbuf_ref, idxbuf_ref, gwbuf_ref, rankbuf_ref,
                         hmid_ref, knew_ref, vnew_ref, sre_ref, sim_ref, hn_ref, idx_ref, gw_ref, rank_ref, cnt_ref,
                         proj_s, qall_s, oall_s, kn8_s, vn8_s):
    del hbuf_ref, hnbuf_ref, idxbuf_ref, gwbuf_ref, rankbuf_ref
    g = pl.program_id(0)
    lo = lax.broadcasted_iota(jnp.int32, (DEC_BATCH, 128), 1) < HEAD_DIM

    @pl.when(g == 0)
    def _():
        proj = _bdot(_rms(x_ref[...], gmix_ref[...]), win_ref[...])
        proj_s[...] = proj
        for h in range(N_HEADS):
            jq, half, kv = h // 2, h % 2, h // 4
            q2 = proj[:, 128 * jq:128 * (jq + 1)] * (HEAD_DIM ** -0.5)
            if half != kv:
                q2 = pltpu.roll(q2, HEAD_DIM, 1)
            hr = slice(h * SPITCH, h * SPITCH + DEC_BATCH)
            qall_s[hr, :] = jnp.where(lo if kv == 0 else jnp.logical_not(lo), q2, 0.0)
            kn8_s[hr, :] = proj[:, D_ATTN:D_ATTN + D_KV]
            vn8_s[hr, :] = proj[:, D_ATTN + D_KV:D_ATTN + 2 * D_KV]

    def head_rows(ref):
        return jnp.stack([ref[pl.ds(g * SGRP + ii, N_HEADS, stride=SPITCH), :] for ii in range(SGRP)], axis=0)

    qe = head_rows(qall_s)
    knew = head_rows(kn8_s)
    vnew = head_rows(vn8_s)
    s = jnp.einsum('bhd,bkd->bhk', qe.astype(BF16), ck_ref[...].astype(BF16),
                   preferred_element_type=F32) + sbias_ref[...][None]
    s_new = jnp.sum(qe * knew, axis=-1, keepdims=True)
    sink = sinkc_ref[...][None]
    m = jnp.maximum(jnp.maximum(jnp.max(s, axis=-1, keepdims=True), s_new), sink)
    p = jnp.exp(s - m)
    p_new = jnp.exp(s_new - m)
    den = jnp.sum(p, axis=-1, keepdims=True) + p_new + jnp.exp(sink - m)
    o = (jnp.einsum('bhk,bkd->bhd', p.astype(BF16), cv_ref[...].astype(BF16),
                    preferred_element_type=F32) + p_new * vnew) / den
    for ii in range(SGRP):
        oall_s[pl.ds(g * SGRP + ii, N_HEADS, stride=SPITCH), :] = o[ii]

    @pl.when(g == N_SGRP - 1)
    def _():
        proj = proj_s[...]
        blocks = []
        for jq in range(N_HEADS // 2):
            oa = oall_s[(2 * jq) * SPITCH:(2 * jq) * SPITCH + DEC_BATCH, :]
            ob = oall_s[(2 * jq + 1) * SPITCH:(2 * jq + 1) * SPITCH + DEC_BATCH, :]
            if jq // 2 == 0:
                blocks.append(jnp.where(lo, oa, pltpu.roll(ob, HEAD_DIM, 1)))
            else:
                blocks.append(jnp.where(lo, pltpu.roll(oa, HEAD_DIM, 1), ob))
        attn = jnp.concatenate(blocks, axis=1)
        u = proj[:, D_ATTN + 2 * D_KV:]
        ys = []
        for j in range(N_SBLK):
            bu = _bdot(u[:, 128 * j:128 * (j + 1)], bblk_ref[j])
            lr = lamr_ref[j]
            li = lami_ref[j]
            cols = slice(SBLK * j, SBLK * (j + 1))
            sr = x0r_ref[:, cols]
            si = x0i_ref[:, cols]
            nr = lr * sr - li * si + bu[:, 0:SBLK]
            ni = lr * si + li * sr + bu[:, SBLK:2 * SBLK]
            sre_ref[:, cols] = nr
            sim_ref[:, cols] = ni
            ys.append(_bdot(jnp.concatenate([nr, ni], axis=1), cblk_ref[j]))
        sn = _ssm_post(jnp.concatenate(ys, axis=1), u, dskip_ref[...], wglu_ref[...], bglu_ref[...], gs_ref[...])
        an = _rms(attn, ga_ref[...])
        h = x_ref[...] + _bdot(an, wout_ref[0:D_ATTN, :]) + _bdot(sn, wout_ref[D_ATTN:2 * D_ATTN, :])
        h = jnp.concatenate([h, jnp.zeros((DEC_BATCH, D_MODEL), F32)], axis=0)
        hmid_ref[...] = h
        cnt_ref[...] = cnt_in_ref[...]
        _route(h, gffn_ref, wrt_ref, br_ref, tri_ref, cnt_ref, hn_ref, idx_ref, gw_ref, rank_ref)
        knew_ref[...] = proj[:, D_ATTN:D_ATTN + D_KV]
        vnew_ref[...] = proj[:, D_ATTN + D_KV:D_ATTN + 2 * D_KV]


def _mixer_sample(x_s, gmix, win, ck, cv, sbias, sinkc, bblk, lamr, lami, cblk, dskip, wglu, bglu, ga, gs, wout,
                  x0r, x0i, gffn, wrt, br, cnt_in, hbuf, hnbuf, idxbuf, gwbuf, rankbuf):
    nst = SSM_G * SSM_P
    in_specs = [
        _full((DEC_BATCH, D_MODEL)), _full((1, D_MODEL)), _full((D_MODEL, D_IN)),
        pl.BlockSpec((SGRP, WINDOW, D_KV), lambda g: (g, 0, 0)),
        pl.BlockSpec((SGRP, WINDOW, D_KV), lambda g: (g, 0, 0)),
        _full((N_HEADS, WINDOW)), _full((N_HEADS, 1)),
        _full((N_SBLK, 128, 2 * SBLK)), _full((N_SBLK, 1, SBLK)), _full((N_SBLK, 1, SBLK)),
        _full((N_SBLK, 2 * SBLK, 128)), _full((1, D_SSM)), _full((D_SSM, D_SSM)), _full((1, D_SSM)),
        _full((1, D_ATTN)), _full((1, D_SSM)), _full((D_MODEL, D_MODEL)),
        _full((DEC_BATCH, nst)), _full((DEC_BATCH, nst)),
        _full((1, D_MODEL)), _full((N_EXPERTS, D_MODEL)), _full((N_EXPERTS, 1)),
        _full((2 * DEC_BATCH, 2 * DEC_BATCH)), _full((N_EXPERTS, 1)),
    ] + [pl.BlockSpec(memory_space=pl.ANY)] * 5
    tail_block = SAMPLE_ROW0 // (2 * DEC_BATCH)
    route_specs, route_shapes = _route_specs(2 * DEC_BATCH, lambda g: tail_block)
    out_shape = [
        jax.ShapeDtypeStruct((T_PAD, D_MODEL), F32),
        jax.ShapeDtypeStruct((DEC_BATCH, D_KV), F32), jax.ShapeDtypeStruct((DEC_BATCH, D_KV), F32),
        jax.ShapeDtypeStruct((DEC_BATCH, nst), F32), jax.ShapeDtypeStruct((DEC_BATCH, nst), F32),
    ] + route_shapes
    out_specs = [
        pl.BlockSpec((2 * DEC_BATCH, D_MODEL), lambda g: (tail_block, 0)),
        _full((DEC_BATCH, D_KV)), _full((DEC_BATCH, D_KV)),
        _full((DEC_BATCH, nst)), _full((DEC_BATCH, nst)),
    ] + route_specs
    head_rows = pltpu.VMEM((N_HEADS * SPITCH, 128), F32)
    scratch = [pltpu.VMEM((DEC_BATCH, D_IN), F32), head_rows, head_rows, head_rows, head_rows]
    return pl.pallas_call(
        _mixer_sample_kernel, grid=(N_SGRP,), in_specs=in_specs, out_specs=out_specs, out_shape=out_shape,
        scratch_shapes=scratch, input_output_aliases={24: 0, 25: 5, 26: 6, 27: 7, 28: 8}, name="mixer_sample",
        compiler_params=pltpu.CompilerParams(dimension_semantics=("arbitrary",), vmem_limit_bytes=VMEM_LIMIT),
    )(x_s, gmix, win, ck, cv, sbias, sinkc, bblk, lamr, lami, cblk, dskip, wglu, bglu, ga, gs, wout, x0r, x0i,
      gffn, wrt, br, _tri(2 * DEC_BATCH), cnt_in, hbuf, hnbuf, idxbuf, gwbuf, rankbuf)


def _route(h, g_ref, wrt_ref, br_ref, tri_ref, cnt_ref, hn_ref, idx_ref, gw_ref, rank_ref):
    hn = _rms(h, g_ref[...])
    hn_ref[...] = _pack_pairs(hn)
    hn_hi = hn.astype(BF16)
    hn_lo = (hn - hn_hi.astype(F32)).astype(BF16)
    w = wrt_ref[...]
    w_hi = w.astype(BF16)
    w_lo = (w - w_hi.astype(F32)).astype(BF16)
    nt = (((1,), (1,)), ((), ()))
    logits = (lax.dot_general(w_hi, hn_hi, nt, preferred_element_type=F32)
              + lax.dot_general(w_lo, hn_hi, nt, preferred_element_type=F32)
              + lax.dot_general(w_hi, hn_lo, nt, preferred_element_type=F32)) + br_ref[...]
    eidx = lax.broadcasted_iota(jnp.int32, logits.shape, 0)
    vals, onehots = [], []
    l = logits
    for k in range(TOP_K):
        m = jnp.max(l, axis=0, keepdims=True)
        ik = jnp.min(jnp.where(l == m, eidx, N_EXPERTS), axis=0, keepdims=True)
        oh = eidx == ik
        idx_ref[k:k + 1, :] = ik
        vals.append(m)
        onehots.append(oh)
        l = jnp.where(oh, -jnp.inf, l)
    exps = [jnp.exp(v - vals[0]) for v in vals]
    den = exps[0] + exps[1] + exps[2] + exps[3]
    for k in range(TOP_K):
        gw_ref[k:k + 1, :] = exps[k] / den
    member = jnp.zeros(logits.shape, F32)
    for oh in onehots:
        member = member + jnp.where(oh, 1.0, 0.0)
    wblk = tri_ref.shape[0]
    base = cnt_ref[...].astype(F32)
    befores = []
    for cb in range(h.shape[0] // wblk):
        mblk = member[:, cb * wblk:(cb + 1) * wblk]
        befores.append(jnp.dot(mblk.astype(BF16), tri_ref[...], preferred_element_type=F32) + base)
        base = base + jnp.sum(mblk, axis=1, keepdims=True)
    before = jnp.concatenate(befores, axis=1)
    for k in range(TOP_K):
        rank_ref[k:k + 1, :] = jnp.sum(jnp.where(onehots[k], before, 0.0), axis=0, keepdims=True).astype(jnp.int32)
    cnt_ref[...] = base.astype(jnp.int32)


def _route_specs(rows, block):
    specs = [pl.BlockSpec((rows, D_PACK), lambda i: (block(i), 0)), pl.BlockSpec((TOP_K, rows), lambda i: (0, block(i))),
             pl.BlockSpec((TOP_K, rows), lambda i: (0, block(i))), pl.BlockSpec((TOP_K, rows), lambda i: (0, block(i))),
             _full((N_EXPERTS, 1))]
    shapes = [jax.ShapeDtypeStruct((T_PAD, D_PACK), jnp.int32), jax.ShapeDtypeStruct((TOP_K, T_PAD), jnp.int32),
              jax.ShapeDtypeStruct((TOP_K, T_PAD), F32), jax.ShapeDtypeStruct((TOP_K, T_PAD), jnp.int32),
              jax.ShapeDtypeStruct((N_EXPERTS, 1), jnp.int32)]
    return specs, shapes


def _tri(n):
    return jnp.asarray(np.triu(np.ones((n, n), np.float32), 1), BF16)


def _place_kernel(off_ref, idx_ref, rank_ref, pos_ref):
    idx = idx_ref[...]
    pos = rank_ref[...]
    for e in range(N_EXPERTS):
        pos = pos + jnp.where(idx == e, off_ref[e], 0)
    pos_ref[...] = pos


def _place(offsets, idx, rank):
    return pl.pallas_call(
        _place_kernel,
        in_specs=[pl.BlockSpec(memory_space=pltpu.SMEM), pl.BlockSpec(memory_space=pltpu.VMEM),
                  pl.BlockSpec(memory_space=pltpu.VMEM)],
        out_specs=pl.BlockSpec(memory_space=pltpu.VMEM),
        out_shape=jax.ShapeDtypeStruct((TOP_K, T_PAD), jnp.int32), name="place",
    )(offsets, idx, rank)


def _sc_mesh():
    return plsc.VectorSubcoreMesh(core_axis_name="core", subcore_axis_name="subcore")


def _sc_dispatch(rows, pos):
    n, d = rows.shape
    nblk = n // SC_ROWS
    pos_w = pos.reshape(TOP_K, nblk, SC_ROWS).transpose(1, 0, 2)

    @functools.partial(pl.kernel, out_type=jax.ShapeDtypeStruct((P_ALLOC, d), rows.dtype), mesh=_sc_mesh(),
                       scratch_types=[], name="dispatch")
    def run(x_hbm, i_hbm, o_hbm):
        def body(x_vmem, i_vmem):
            for k in range(TOP_K):
                pltpu.sync_copy(x_vmem, o_hbm.at[i_vmem.at[0, k]])

        pltpu.emit_pipeline(
            body, grid=(nblk,),
            in_specs=[pl.BlockSpec((SC_ROWS, d), lambda i: (i, 0)),
                      pl.BlockSpec((1, TOP_K, SC_ROWS), lambda i: (i, 0, 0))],
            out_specs=[], core_axis_name=("core", "subcore"), dimension_semantics=(pltpu.PARALLEL,),
        )(x_hbm, i_hbm)

    return run(rows, pos_w)


def _sc_build_dest(flat_pos, flat_ids, after):
    n = flat_pos.shape[0]
    stage = n // 32
    lanes = 16
    unroll = 5
    params = pltpu.CompilerParams(needs_layout_passes=False)

    @functools.partial(pl.kernel, out_type=jax.ShapeDtypeStruct((P_ALLOC,), jnp.int32), mesh=_sc_mesh(),
                       scratch_types=[pltpu.VMEM((DEST_SLICE,), jnp.int32), pltpu.VMEM((stage,), jnp.int32),
                                      pltpu.VMEM((stage,), jnp.int32)],
                       compiler_params=params, name="build_dest")
    def run(pos_hbm, ids_hbm, after_hbm, o_hbm, local, pbuf, ibuf):
        del after_hbm
        lo = (lax.axis_index("core") * 16 + lax.axis_index("subcore")) * DEST_SLICE

        @pl.loop(0, DEST_SLICE, step=lanes)
        def _(i):
            local[pl.ds(i, lanes)] = TRASH_ROW + ((lo + i + lax.iota(jnp.int32, lanes)) & (N_TRASH - 1))

        @pl.loop(0, n // stage)
        def _(c):
            pltpu.sync_copy(pos_hbm.at[pl.ds(c * stage, stage)], pbuf)
            pltpu.sync_copy(ids_hbm.at[pl.ds(c * stage, stage)], ibuf)

            @pl.loop(0, stage, step=lanes * unroll)
            def _(j0):
                for u in range(unroll):
                    j = j0 + u * lanes
                    rel = pbuf[pl.ds(j, lanes)] - lo
                    plsc.store_scatter(local, [rel], ibuf[pl.ds(j, lanes)], mask=(rel >= 0) & (rel < DEST_SLICE))

        pltpu.sync_copy(local, o_hbm.at[pl.ds(lo, DEST_SLICE)])

    return run(flat_pos, flat_ids, after)


def _sc_combine_scatter(rows, dest, yg_prev, chunk):
    w = SC_ROWS_COMBINE
    nblk = (CHUNK_BOUNDS[chunk + 1] - CHUNK_BOUNDS[chunk]) // w
    b0 = CHUNK_BOUNDS[chunk] // w
    dest_w = dest.reshape(P_ALLOC // w, 1, w)

    def body_fn(x_hbm, i_hbm, *rest):
        o_hbm = rest[-1]

        def body(x_vmem, i_vmem):
            pltpu.sync_copy(x_vmem, o_hbm.at[i_vmem.at[0, 0]])

        pltpu.emit_pipeline(
            body, grid=(nblk,),
            in_specs=[pl.BlockSpec((w, D_PACK), lambda i: (b0 + i, 0)),
                      pl.BlockSpec((1, 1, w), lambda i: (b0 + i, 0, 0))],
            out_specs=[], core_axis_name=("core", "subcore"), dimension_semantics=(pltpu.PARALLEL,),
        )(x_hbm, i_hbm)

    out_type = jax.ShapeDtypeStruct((YG_ROWS, D_PACK), rows.dtype)
    if yg_prev is None:
        return pl.kernel(body_fn, out_type=out_type, mesh=_sc_mesh(), scratch_types=[],
                         name="combine")(rows, dest_w)
    return mpmd._mpmd_map([(_sc_mesh(), body_fn)], out_type, input_output_aliases={2: 0}, scratch_types=(),
                          name="combine")(rows, dest_w, yg_prev)


def _experts_kernel(ts_ref, trow_ref, tsz_ref, rng_ref, xs_hbm, w1_hbm, b1_hbm, w2_hbm, b2_hbm, ys_hbm,
                    w1_st, w2_st, b1_st, b2_st, w1_s, w2_s, xbuf, ybuf, wsem, xsem, ysem):
    elo, ehi, tlo, thi = rng_ref[0], rng_ref[1], rng_ref[2], rng_ref[3]

    def w_copies(ex, slot):
        return (pltpu.make_async_copy(w1_hbm.at[ex], w1_st.at[slot], wsem.at[0, slot]),
                pltpu.make_async_copy(w2_hbm.at[ex], w2_st.at[slot], wsem.at[1, slot]),
                pltpu.make_async_copy(b1_hbm.at[ex], b1_st.at[slot], wsem.at[2, slot]),
                pltpu.make_async_copy(b2_hbm.at[ex], b2_st.at[slot], wsem.at[3, slot]))

    def x_copy(t, slot, rows):
        src = xs_hbm.at[pl.ds(pl.multiple_of(trow_ref[t], TM_HALF), rows)]
        return pltpu.make_async_copy(src, xbuf.at[slot, pl.ds(0, rows)], xsem.at[slot])

    def y_copy(t, slot, rows):
        dst = ys_hbm.at[pl.ds(pl.multiple_of(trow_ref[t], TM_HALF), rows)]
        return pltpu.make_async_copy(ybuf.at[slot, pl.ds(0, rows)], dst, ysem.at[slot])

    def by_size(t, fn):
        for rows in (TM, TM_HALF):
            @pl.when(tsz_ref[t] == rows)
            def _():
                fn(rows)

    @pl.when(tlo < thi)
    def _():
        for c in w_copies(elo, 0):
            c.start()
        by_size(tlo, lambda rows: x_copy(tlo, tlo % 2, rows).start())

    def expert(e, carry):
        wslot = (e - elo) % 2

        @pl.when(e + 1 < ehi)
        def _():
            for c in w_copies(e + 1, 1 - wslot):
                c.start()

        for c in w_copies(e, wslot):
            c.wait()
        for r in range(4):
            rs = slice(256 * r, 256 * (r + 1))
            w1_s[rs, :] = w1_st[wslot, rs, :].astype(BF16)
            w2_s[rs, :] = w2_st[wslot, rs, :].astype(BF16)
        b1 = b1_st[wslot]
        b2 = b2_st[wslot]

        def tile(t, carry):
            slot = t % 2

            @pl.when(t + 1 < thi)
            def _():
                by_size(t + 1, lambda rows: x_copy(t + 1, 1 - slot, rows).start())

            by_size(t, lambda rows: x_copy(t, slot, rows).wait())

            @pl.when(t - 2 >= tlo)
            def _():
                by_size(t - 2, lambda rows: y_copy(t - 2, slot, rows).wait())

            def compute(rows):
                hdn = _bdot(_unpack_pairs(xbuf[slot, 0:rows]), w1_s[...]) + b1
                gt = jnp.minimum(hdn[:, :D_FF], SWIGLU_LIMIT)
                up = jnp.clip(hdn[:, D_FF:], -SWIGLU_LIMIT, SWIGLU_LIMIT)
                act = (up + 1.0) * gt * _sigmoid(SWIGLU_ALPHA * gt)
                ybuf[slot, 0:rows] = _pack_pairs(_bdot(act, w2_s[...]) + b2)
                y_copy(t, slot, rows).start()

            by_size(t, compute)
            return carry

        lax.fori_loop(jnp.maximum(ts_ref[e], tlo), jnp.minimum(ts_ref[e + 1], thi), tile, 0)
        return carry

    lax.fori_loop(elo, ehi, expert, 0)

    @pl.when(thi - tlo >= 2)
    def _():
        by_size(thi - 2, lambda rows: y_copy(thi - 2, thi % 2, rows).wait())

    @pl.when(thi - tlo >= 1)
    def _():
        by_size(thi - 1, lambda rows: y_copy(thi - 1, (thi - 1) % 2, rows).wait())


def _tile_plan(counts):
    units = (counts + (TM_HALF - 1)) // TM_HALF
    unit_end = jnp.cumsum(units)
    offsets = ((unit_end - units) * TM_HALF).astype(jnp.int32)
    n_full = units // 2
    tiles_per = n_full + units % 2
    tile_end = jnp.cumsum(tiles_per)
    tile_start = jnp.concatenate([jnp.zeros((1,), jnp.int32), tile_end.astype(jnp.int32)])
    t = jnp.arange(MAX_TILES, dtype=jnp.int32)[:, None]
    done = tile_end[None, :] <= t
    first_tile = jnp.max(jnp.where(done, tile_end[None, :], 0), axis=1)
    first_row = jnp.max(jnp.where(done, unit_end[None, :] * TM_HALF, 0), axis=1)
    is_half = jnp.any((units[None, :] % 2 == 1) & (t == tile_end[None, :] - 1), axis=1)
    valid = t[:, 0] < tile_end[-1]
    tile_row = jnp.where(valid, first_row + (t[:, 0] - first_tile) * TM, 0).astype(jnp.int32)
    tile_rows = jnp.where(valid, jnp.where(is_half, TM_HALF, TM), 0).astype(jnp.int32)
    return offsets, tile_start, tile_row, tile_rows


def _chunk_ranges(tile_start, tile_row, tile_rows):
    valid = tile_rows > 0
    tile_end = tile_start[1:]
    out = []
    for c in range(len(CHUNK_UNITS)):
        r0, r1 = CHUNK_BOUNDS[c], CHUNK_BOUNDS[c + 1]
        tlo = jnp.sum((valid & (tile_row + tile_rows <= r0)).astype(jnp.int32))
        thi = jnp.sum((valid & (tile_row < r1)).astype(jnp.int32))
        elo = jnp.sum((tile_end <= tlo).astype(jnp.int32))
        ehi = jnp.where(thi > tlo, jnp.sum((tile_end <= thi - 1).astype(jnp.int32)) + 1, elo)
        out.append(jnp.stack([elo, ehi, tlo, thi]).astype(jnp.int32))
    return out


def _experts(tile_start, tile_row, tile_rows, chunk_range, xs, w_up, b_up, w_down, b_down):
    hbm = pl.BlockSpec(memory_space=pl.ANY)
    grid_spec = pltpu.PrefetchScalarGridSpec(
        num_scalar_prefetch=4, grid=(1,),
        in_specs=[hbm, hbm, hbm, hbm, hbm],
        out_specs=hbm,
        scratch_shapes=[pltpu.VMEM((2, D_MODEL, 2 * D_FF), F32), pltpu.VMEM((2, D_FF, D_MODEL), F32),
                        pltpu.VMEM((2, 1, 2 * D_FF), F32), pltpu.VMEM((2, 1, D_MODEL), F32),
                        pltpu.VMEM((D_MODEL, 2 * D_FF), BF16), pltpu.VMEM((D_FF, D_MODEL), BF16),
                        pltpu.VMEM((2, TM, D_PACK), jnp.int32), pltpu.VMEM((2, TM, D_PACK), jnp.int32),
                        pltpu.SemaphoreType.DMA((4, 2)), pltpu.SemaphoreType.DMA((2,)),
                        pltpu.SemaphoreType.DMA((2,))],
    )
    return pl.pallas_call(
        _experts_kernel, grid_spec=grid_spec, out_shape=jax.ShapeDtypeStruct((P_ALLOC, D_PACK), jnp.int32),
        name="experts",
        compiler_params=pltpu.CompilerParams(dimension_semantics=("arbitrary",), vmem_limit_bytes=VMEM_LIMIT),
    )(tile_start, tile_row, tile_rows, chunk_range, xs, w_up, b_up.reshape(N_EXPERTS, 1, 2 * D_FF), w_down,
      b_down.reshape(N_EXPERTS, 1, D_MODEL))


def _final_kernel(h_ref, yg0_ref, yg1_ref, yg2_ref, yg3_ref, gw_ref, p_ref, gple_ref, wg_ref, wp_ref, gfin_ref,
                  out_ref):
    rows = h_ref.shape[0]
    gw = gw_ref[...]
    h = h_ref[...]
    for k, yg_ref in enumerate((yg0_ref, yg1_ref, yg2_ref, yg3_ref)):
        h = h + gw[:, k:k + 1] * _unpack_pairs(yg_ref[...])
    gate = _sigmoid(_bdot(_rms(h, gple_ref[...]), wg_ref[...]))
    h = h + gate * _bdot(p_ref[...].reshape(rows, PLE_DIM), wp_ref[...])
    out_ref[...] = _rms(h, gfin_ref[...]).reshape(out_ref.shape)


def _final_prompt(hmid, yg, gwt, p_prompt, gple, wg, wp, gfin):
    nb = 4
    rows = nb * CHUNK
    nbh = BATCH // nb
    rb = lambda c, b: (c * nbh + b, 0)
    slot = lambda k: pl.BlockSpec((rows, D_PACK), lambda c, b: (k * (K_STRIDE // rows) + c * nbh + b, 0))
    return pl.pallas_call(
        _final_kernel, grid=(N_CHUNKS, nbh),
        in_specs=[pl.BlockSpec((rows, D_MODEL), rb), slot(0), slot(1), slot(2), slot(3),
                  pl.BlockSpec((rows, TOP_K), rb),
                  pl.BlockSpec((nb, CHUNK, PLE_DIM), lambda c, b: (b, c, 0)),
                  _full((1, D_MODEL)), _full((D_MODEL, D_MODEL)), _full((PLE_DIM, D_MODEL)), _full((1, D_MODEL))],
        out_specs=pl.BlockSpec((nb, CHUNK, D_MODEL), lambda c, b: (b, c, 0)),
        out_shape=jax.ShapeDtypeStruct((BATCH, SEQ, D_MODEL), F32), name="final_prompt",
        compiler_params=pltpu.CompilerParams(dimension_semantics=("arbitrary", "arbitrary"),
                                             vmem_limit_bytes=VMEM_LIMIT),
    )(hmid, yg, yg, yg, yg, gwt, p_prompt, gple, wg, wp, gfin)


def _final_sample(hmid, yg, gwt, p_sample, gple, wg, wp, gfin):
    blk = SAMPLE_ROW0 // DEC_BATCH
    slot = lambda k: pl.BlockSpec((DEC_BATCH, D_PACK), lambda i: (k * (K_STRIDE // DEC_BATCH) + blk, 0))
    return pl.pallas_call(
        _final_kernel, grid=(1,),
        in_specs=[pl.BlockSpec((DEC_BATCH, D_MODEL), lambda i: (blk, 0)), slot(0), slot(1), slot(2), slot(3),
                  pl.BlockSpec((DEC_BATCH, TOP_K), lambda i: (blk, 0)),
                  _full((DEC_BATCH, PLE_DIM)),
                  _full((1, D_MODEL)), _full((D_MODEL, D_MODEL)), _full((PLE_DIM, D_MODEL)), _full((1, D_MODEL))],
        out_specs=_full((DEC_BATCH, D_MODEL)),
        out_shape=jax.ShapeDtypeStruct((DEC_BATCH, D_MODEL), F32), name="final_sample",
        compiler_params=pltpu.CompilerParams(dimension_semantics=("arbitrary",), vmem_limit_bytes=VMEM_LIMIT),
    )(hmid, yg, yg, yg, yg, gwt, p_sample, gple, wg, wp, gfin)


def _alibi_tables():
    slopes = 2.0 ** (-8.0 * (np.arange(N_HEADS, dtype=np.float64) + 1.0) / N_HEADS)
    i = np.arange(CHUNK)[:, None]
    j = np.arange(2 * CHUNK)[None, :]
    dist = i + CHUNK - j
    valid = (dist >= 0) & (dist <= WINDOW)
    tabs = []
    for has_prev in (False, True):
        ok = valid & ((j >= CHUNK) | has_prev)
        tabs.append(np.where(ok[None], -slopes[:, None, None] * dist[None], NEG))
    prompt = np.stack([np.stack([np.concatenate([t[h] for h in grp], axis=0) for grp in (NAT_HEADS, ROT_HEADS)])
                       for t in tabs]).astype(np.float32)
    wb = min(WINDOW, PAST_LEN)
    sample = (-slopes[:, None] * (wb - np.arange(wb))[None, :]).astype(np.float32)
    return prompt, sample


def kernel(x_prompt, x_sample, cache_k_win, cache_v_win, state_ssm_re, state_ssm_im, p_prompt, p_sample, norm_mix, w_in, sinks, ssm_lam_re, ssm_lam_im, ssm_log_step, ssm_b_re, ssm_b_im, ssm_c_re, ssm_c_im, ssm_d, w_glu, b_glu, norm_attn_out, norm_ssm_out, w_out, norm_ffn, w_router, b_router, w_up, b_up, w_down, b_down, norm_ple, w_ple_gate, w_ple_proj, norm_final):
    nst = SSM_G * SSM_P
    bias_np, sbias_np = _alibi_tables()
    bias = jnp.asarray(bias_np)
    sbias = jnp.asarray(sbias_np)

    lbr, lbi, bbr, bbi = _prep(ssm_lam_re[0], ssm_lam_im[0], ssm_log_step[0], ssm_b_re[0], ssm_b_im[0])
    bblk, lamr, lami, cblk = _s5_blocks(lbr, lbi, bbr, bbi, ssm_c_re[0], ssm_c_im[0])

    gmix = norm_mix[0].reshape(1, D_MODEL)
    win = w_in[0].astype(BF16)
    dskip = ssm_d[0].reshape(1, D_SSM)
    wglu = w_glu[0].astype(BF16)
    bglu = b_glu[0].reshape(1, D_SSM)
    ga = norm_attn_out[0].reshape(1, D_ATTN)
    gs = norm_ssm_out[0].reshape(1, D_SSM)
    wout = w_out[0].astype(BF16)
    sink = sinks[0]

    gffn = norm_ffn[0].reshape(1, D_MODEL)
    wrt = w_router[0].T
    br = b_router[0].reshape(N_EXPERTS, 1)
    hbuf, k_last, v_last, re_p, im_p, hnbuf, idxbuf, gwbuf, rankbuf, cnt_p = _mixer_prompt(
        x_prompt, sink, gmix, win, bias, bblk, lamr, lami, cblk, dskip, wglu, bglu, ga, gs, wout, gffn, wrt, br)

    ck = cache_k_win[0].reshape(DEC_BATCH, WINDOW, D_KV)
    cv = cache_v_win[0].reshape(DEC_BATCH, WINDOW, D_KV)
    hmid, k_new, v_new, re_s, im_s, hn, idx, gw, rank, counts = _mixer_sample(
        x_sample.reshape(DEC_BATCH, D_MODEL), gmix, win, ck, cv, sbias, sink.reshape(N_HEADS, 1), bblk, lamr, lami,
        cblk, dskip, wglu, bglu, ga, gs, wout, state_ssm_re[0].reshape(DEC_BATCH, nst),
        state_ssm_im[0].reshape(DEC_BATCH, nst), gffn, wrt, br, cnt_p, hbuf, hnbuf, idxbuf, gwbuf, rankbuf)

    offsets, tile_start, tile_row, tile_rows = _tile_plan(counts[:, 0])

    pos = _place(offsets, idx, rank)
    xs = _sc_dispatch(hn, pos)
    pair_ids = (jnp.arange(TOP_K, dtype=jnp.int32)[:, None] * K_STRIDE + jnp.arange(T_PAD, dtype=jnp.int32)[None, :])
    dest = _sc_build_dest(pos.reshape(TOP_K * T_PAD), pair_ids.reshape(TOP_K * T_PAD), xs)
    yg = None
    for c, chunk_range in enumerate(_chunk_ranges(tile_start, tile_row, tile_rows)):
        ys = _experts(tile_start, tile_row, tile_rows, chunk_range, xs, w_up[0], b_up[0], w_down[0], b_down[0])
        yg = _sc_combine_scatter(ys, dest, yg, c)

    gwt = gw.T
    gple = norm_ple[0].reshape(1, D_MODEL)
    wg = w_ple_gate[0].astype(BF16)
    wp = w_ple_proj[0].astype(BF16)
    gfin = norm_final.reshape(1, D_MODEL)
    y_prompt = _final_prompt(hmid, yg, gwt, p_prompt[0], gple, wg, wp, gfin)
    y_sample = _final_sample(hmid, yg, gwt, p_sample[0].reshape(DEC_BATCH, PLE_DIM), gple, wg, wp, gfin)

    k_win_s = jnp.concatenate([ck[:, 1:], k_new[:, None, :]], axis=1)
    v_win_s = jnp.concatenate([cv[:, 1:], v_new[:, None, :]], axis=1)
    kv5 = (1, BATCH, CHUNK, N_KV, HEAD_DIM)
    skv5 = (1, DEC_BATCH, WINDOW, N_KV, HEAD_DIM)
    return (y_prompt, y_sample.reshape(DEC_BATCH, 1, D_MODEL),
            k_last.reshape(kv5), v_last.reshape(kv5),
            re_p.reshape(1, BATCH, SSM_G, SSM_P), im_p.reshape(1, BATCH, SSM_G, SSM_P),
            k_win_s.reshape(skv5), v_win_s.reshape(skv5),
            re_s.reshape(1, DEC_BATCH, SSM_G, SSM_P), im_s.reshape(1, DEC_BATCH, SSM_G, SSM_P))
```

```python
import functools

import numpy as np
import jax
import jax.numpy as jnp
from jax import lax
from jax.experimental import pallas as pl
from jax.experimental.pallas import tpu as pltpu
from jax.experimental.pallas import tpu_sc as plsc

F32 = jnp.float32
BF16 = jnp.bfloat16

D_MODEL = 1024
BATCH = 8
SEQ = 2048
DEC_BATCH = 128
PAST_LEN = 16384
HEAD_DIM = 64
D_ATTN = 512
N_HEADS = 8
N_KV = 2
D_KV = N_KV * HEAD_DIM
WINDOW = 128
D_SSM = 512
SSM_H = 16
SSM_G = 32
SSM_P = 64
D_IN = D_ATTN + 2 * D_KV + D_SSM
N_EXPERTS = 32
TOP_K = 4
D_FF = 1024
SWIGLU_LIMIT = 7.0
SWIGLU_ALPHA = 1.702
PLE_DIM = 256
EPS = 1e-5
NEG = -1e30

CHUNK = 128
N_CHUNKS = SEQ // CHUNK
ROWS = BATCH * CHUNK
PITCH = CHUNK + 8
NAT_HEADS = (0, 2, 5, 7)
ROT_HEADS = (1, 3, 4, 6)
SUB_T = 32
SUB_ROWS = SUB_T * BATCH
N_SBLK = 4
SBLK = 512
T_REAL = BATCH * SEQ + DEC_BATCH
T_PAD = T_REAL + 128
SAMPLE_ROW0 = BATCH * SEQ
ROUTE_BLOCK = 512
TM = 1024
TM_HALF = 512
P_ROWS = (T_PAD * TOP_K + N_EXPERTS * (TM_HALF - 1)) // TM_HALF * TM_HALF
MAX_TILES = P_ROWS // TM + N_EXPERTS + 1
SC_ROWS = 40
SC_ROWS_COMBINE = 32
N_PARTS = 4
PART_CHUNKS = N_CHUNKS // N_PARTS
D_PACK = D_MODEL // 2
VMEM_LIMIT = 56 * 1024 * 1024


def _rms(x, g):
    return x * lax.rsqrt(jnp.mean(x * x, axis=-1, keepdims=True) + EPS) * g


def _sigmoid(x):
    return 1.0 / (1.0 + jnp.exp(-x))


def _gelu_tanh(x):
    c = np.float32(np.sqrt(2.0 / np.pi))
    return 0.5 * x * (1.0 + jnp.tanh(c * (x + 0.044715 * (x * x * x))))


def _bdot(a, b):
    return jnp.dot(a.astype(BF16), b, preferred_element_type=F32)


def _pack_pairs(x):
    n = x.shape[1] // 2
    lo = lax.bitcast_convert_type(x[:, :n].astype(BF16).astype(F32), jnp.int32)
    hi = lax.bitcast_convert_type(x[:, n:].astype(BF16).astype(F32), jnp.int32)
    return lax.shift_right_logical(lo, 16) | hi


def _unpack_pairs(w):
    lo = lax.bitcast_convert_type(lax.shift_left(w, 16), F32)
    hi = lax.bitcast_convert_type(w & jnp.int32(-65536), F32)
    return jnp.concatenate([lo, hi], axis=1)


def _full(shape):
    n = len(shape)
    return pl.BlockSpec(shape, lambda *_: (0,) * n)


def _prep_kernel(lr_ref, li_ref, ls_ref, br_ref, bi_ref, lbr_ref, lbi_ref, bbr_ref, bbi_ref):
    lr = lr_ref[...]
    li = li_ref[...]
    step = jnp.exp(ls_ref[...])
    zr = lr * step
    zi = li * step
    mag = jnp.exp(zr)
    lbr = mag * jnp.cos(zi)
    lbi = mag * jnp.sin(zi)
    lbr_ref[...] = lbr
    lbi_ref[...] = lbi
    nr = lbr - 1.0
    den = lr * lr + li * li
    cr = (nr * lr + lbi * li) / den
    ci = (lbi * lr - nr * li) / den
    br = br_ref[...]
    bi = bi_ref[...]
    bbr_ref[...] = cr * br - ci * bi
    bbi_ref[...] = cr * bi + ci * br


def _prep(lam_re, lam_im, log_step, b_re, b_im):
    g, p, h = SSM_G, SSM_P, SSM_H
    out = pl.pallas_call(
        _prep_kernel,
        out_shape=[jax.ShapeDtypeStruct((g, 1, p), F32), jax.ShapeDtypeStruct((g, 1, p), F32),
                   jax.ShapeDtypeStruct((g, h, p), F32), jax.ShapeDtypeStruct((g, h, p), F32)],
        name="s5_prep",
    )(lam_re.reshape(g, 1, p), lam_im.reshape(g, 1, p), log_step.reshape(g, 1, 1),
      jnp.transpose(b_re, (0, 2, 1)), jnp.transpose(b_im, (0, 2, 1)))
    return out


def _s5_blocks(lbr, lbi, bbr, bbi, c_re, c_im):
    eye = jnp.eye(8, dtype=F32)
    shp = (N_SBLK, 8, SSM_H, SSM_P)

    def in_map(b):
        return (b.reshape(shp)[:, :, :, None, :] * eye[None, :, None, :, None]).reshape(N_SBLK, 128, SBLK)

    def out_map(c):
        ct = jnp.transpose(c.reshape(shp), (0, 1, 3, 2))
        return (ct[:, :, :, None, :] * eye[None, :, None, :, None]).reshape(N_SBLK, SBLK, 128)

    bblk = jnp.concatenate([in_map(bbr), in_map(bbi)], axis=-1).astype(BF16)
    cblk = jnp.concatenate([out_map(c_re), -out_map(c_im)], axis=1).astype(BF16)
    return bblk, lbr.reshape(N_SBLK, 1, SBLK), lbi.reshape(N_SBLK, 1, SBLK), cblk


def _ssm_post(y_lin, u, dskip, wglu, bglu, gs):
    y = _gelu_tanh(y_lin + dskip * u)
    y = y * _sigmoid(_bdot(y, wglu) + bglu)
    return _rms(y, gs)


def _mixer_prompt_kernel(sinks_ref, x_ref, gmix_ref, win_ref, bias_ref, bblk_ref, lamr_ref, lami_ref,
                         cblk_ref, dskip_ref, wglu_ref, bglu_ref, ga_ref, gs_ref, wout_ref,
                         gffn_ref, wrt_ref, br_ref, tri_ref,
                         hmid_ref, klast_ref, vlast_ref, sre_ref, sim_ref, hn_ref, idx_ref, gw_ref, rank_ref, cnt_ref,
                         proj_s, u_s, kprev_s, kprevr_s, vprev_s, vprevr_s, attn_s, ssm_s, utb_s, bu_s, xs_s, ytb_s):
    c = pl.program_id(0)

    @pl.when(c == 0)
    def _():
        zkv = jnp.zeros(kprev_s.shape, BF16)
        kprev_s[...] = zkv
        kprevr_s[...] = zkv
        vprev_s[...] = zkv
        vprevr_s[...] = zkv
        sre_ref[...] = jnp.zeros(sre_ref.shape, F32)
        sim_ref[...] = jnp.zeros(sim_ref.shape, F32)
        cnt_ref[...] = jnp.zeros(cnt_ref.shape, jnp.int32)

    x = x_ref[...].reshape(ROWS, D_MODEL)
    proj = _bdot(_rms(x, gmix_ref[...]), win_ref[...])
    u0 = D_ATTN + 2 * D_KV
    proj_s[...] = proj[:, 0:u0]
    for jb in range(D_SSM // 128):
        for b in range(BATCH):
            u_s[jb, b * PITCH:b * PITCH + CHUNK, :] = proj[b * CHUNK:(b + 1) * CHUNK, u0 + 128 * jb:u0 + 128 * (jb + 1)]

    lo = lax.broadcasted_iota(jnp.int32, (CHUNK, 128), 1) < HEAD_DIM
    hi = jnp.logical_not(lo)
    table = jnp.minimum(c, 1)
    hrow = lax.broadcasted_iota(jnp.int32, (4 * CHUNK, 1), 0) // CHUNK

    def sink_col(heads):
        col = jnp.full((4 * CHUNK, 1), sinks_ref[heads[3]], F32)
        for n in (2, 1, 0):
            col = jnp.where(hrow == n, sinks_ref[heads[n]], col)
        return col

    sink_nat = sink_col(NAT_HEADS)
    sink_rot = sink_col(ROT_HEADS)

    def attend(q, k, v, bias, sink):
        s = lax.dot_general(q, k, (((1,), (1,)), ((), ())), preferred_element_type=F32) + bias
        m = jnp.maximum(jnp.max(s, axis=-1, keepdims=True), sink)
        p = jnp.exp(s - m)
        den = jnp.sum(p, axis=-1, keepdims=True) + jnp.exp(sink - m)
        return jnp.dot(p.astype(BF16), v, preferred_element_type=F32) / den

    def attn_body(b, carry):
        r0 = pl.multiple_of(b * CHUNK, CHUNK)
        rows = pl.ds(r0, CHUNK)
        kb = proj_s[rows, D_ATTN:D_ATTN + D_KV]
        vb = proj_s[rows, D_ATTN + D_KV:D_ATTN + 2 * D_KV]
        kb16 = kb.astype(BF16)
        vb16 = vb.astype(BF16)
        kbr16 = pltpu.roll(kb, HEAD_DIM, 1).astype(BF16)
        vbr16 = pltpu.roll(vb, HEAD_DIM, 1).astype(BF16)
        k_nat = jnp.concatenate([kprev_s[b], kb16], axis=0)
        k_rot = jnp.concatenate([kprevr_s[b], kbr16], axis=0)
        v_nat = jnp.concatenate([vprev_s[b], vb16], axis=0)
        v_rot = jnp.concatenate([vprevr_s[b], vbr16], axis=0)
        q2 = [proj_s[rows, 128 * jq:128 * (jq + 1)] * (HEAD_DIM ** -0.5) for jq in range(N_HEADS // 2)]
        q_nat = jnp.concatenate([jnp.where(lo if h % 2 == 0 else hi, q2[h // 2], 0.0) for h in NAT_HEADS],
                                axis=0).astype(BF16)
        q_rot = jnp.concatenate([jnp.where(lo if h % 2 == 0 else hi, q2[h // 2], 0.0) for h in ROT_HEADS],
                                axis=0).astype(BF16)
        o_nat = attend(q_nat, k_nat, v_nat, bias_ref[table, 0], sink_nat)
        o_rot = attend(q_rot, k_rot, v_rot, bias_ref[table, 1], sink_rot)
        for jq in range(N_HEADS // 2):
            blk = slice(CHUNK * jq, CHUNK * (jq + 1))
            even, odd = (o_nat, o_rot) if jq < 2 else (o_rot, o_nat)
            attn_s[rows, 128 * jq:128 * (jq + 1)] = jnp.where(lo, even[blk], odd[blk])
        kprev_s[b] = kb16
        kprevr_s[b] = kbr16
        vprev_s[b] = vb16
        vprevr_s[b] = vbr16
        return carry

    lax.fori_loop(0, BATCH, attn_body, 0)

    for sc in range(CHUNK // SUB_T):
        t0 = sc * SUB_T
        for i in range(SUB_T):
            for jb in range(D_SSM // 128):
                utb_s[i * BATCH:(i + 1) * BATCH, 128 * jb:128 * (jb + 1)] = (
                    u_s[jb, pl.ds(t0 + i, BATCH, stride=PITCH), :])
        u_tb = utb_s[...]
        for j in range(N_SBLK):
            bu_s[...] = _bdot(u_tb[:, 128 * j:128 * (j + 1)], bblk_ref[j])
            lr = jnp.broadcast_to(lamr_ref[j], (BATCH, SBLK))
            li = jnp.broadcast_to(lami_ref[j], (BATCH, SBLK))
            cols = slice(SBLK * j, SBLK * (j + 1))

            def step(i, carry):
                sr, si = carry
                r = pl.ds(pl.multiple_of(i * BATCH, BATCH), BATCH)
                nr = lr * sr - li * si + bu_s[r, 0:SBLK]
                ni = lr * si + li * sr + bu_s[r, SBLK:2 * SBLK]
                xs_s[r, 0:SBLK] = nr
                xs_s[r, SBLK:2 * SBLK] = ni
                return nr, ni

            sr, si = lax.fori_loop(0, SUB_T, step, (sre_ref[:, cols], sim_ref[:, cols]), unroll=True)
            sre_ref[:, cols] = sr
            sim_ref[:, cols] = si
            ytb_s[:, 128 * j:128 * (j + 1)] = _bdot(xs_s[...], cblk_ref[j])
        yn = _ssm_post(ytb_s[...], u_tb, dskip_ref[...], wglu_ref[...], bglu_ref[...], gs_ref[...])
        for i in range(SUB_T):
            for jb in range(D_SSM // 128):
                ssm_s[jb, pl.ds(t0 + i, BATCH, stride=PITCH), :] = (
                    yn[i * BATCH:(i + 1) * BATCH, 128 * jb:128 * (jb + 1)])

    an = _rms(attn_s[...], ga_ref[...])
    sn = jnp.concatenate(
        [jnp.concatenate([ssm_s[jb, b * PITCH:b * PITCH + CHUNK, :] for b in range(BATCH)], axis=0)
         for jb in range(D_SSM // 128)], axis=1)
    h = x + _bdot(an, wout_ref[0:D_ATTN, :]) + _bdot(sn, wout_ref[D_ATTN:2 * D_ATTN, :])
    hmid_ref[...] = h
    _route(h, gffn_ref, wrt_ref, br_ref, tri_ref, cnt_ref, hn_ref, idx_ref, gw_ref, rank_ref)

    @pl.when(c == N_CHUNKS - 1)
    def _():
        klast_ref[...] = proj_s[:, D_ATTN:D_ATTN + D_KV].reshape(BATCH, CHUNK, D_KV)
        vlast_ref[...] = proj_s[:, D_ATTN + D_KV:D_ATTN + 2 * D_KV].reshape(BATCH, CHUNK, D_KV)


def _mixer_prompt(x_prompt, sinks, gmix, win, bias, bblk, lamr, lami, cblk, dskip, wglu, bglu, ga, gs, wout,
                  gffn, wrt, br):
    smem = pl.BlockSpec(memory_space=pltpu.SMEM)
    in_specs = [
        smem,
        pl.BlockSpec((BATCH, CHUNK, D_MODEL), lambda c: (0, c, 0)),
        _full((1, D_MODEL)), _full((D_MODEL, D_IN)), _full((2, 2, 4 * CHUNK, 2 * CHUNK)),
        _full((N_SBLK, 128, 2 * SBLK)), _full((N_SBLK, 1, SBLK)), _full((N_SBLK, 1, SBLK)),
        _full((N_SBLK, 2 * SBLK, 128)), _full((1, D_SSM)), _full((D_SSM, D_SSM)), _full((1, D_SSM)),
        _full((1, D_ATTN)), _full((1, D_SSM)), _full((D_MODEL, D_MODEL)),
        _full((1, D_MODEL)), _full((N_EXPERTS, D_MODEL)), _full((N_EXPERTS, 1)), _full((ROUTE_BLOCK, ROUTE_BLOCK)),
    ]
    route_specs, route_shapes = _route_specs(ROWS, lambda c: c)
    out_shape = [
        jax.ShapeDtypeStruct((T_PAD, D_MODEL), F32),
        jax.ShapeDtypeStruct((BATCH, CHUNK, D_KV), F32),
        jax.ShapeDtypeStruct((BATCH, CHUNK, D_KV), F32),
        jax.ShapeDtypeStruct((BATCH, SSM_G * SSM_P), F32),
        jax.ShapeDtypeStruct((BATCH, SSM_G * SSM_P), F32),
    ] + route_shapes
    out_specs = [
        pl.BlockSpec((ROWS, D_MODEL), lambda c: (c, 0)),
        _full((BATCH, CHUNK, D_KV)), _full((BATCH, CHUNK, D_KV)),
        _full((BATCH, SSM_G * SSM_P)), _full((BATCH, SSM_G * SSM_P)),
    ] + route_specs
    kv_scr = pltpu.VMEM((BATCH, CHUNK, D_KV), BF16)
    scratch = [
        pltpu.VMEM((ROWS, D_ATTN + 2 * D_KV), F32), pltpu.VMEM((D_SSM // 128, BATCH * PITCH, 128), F32),
        kv_scr, kv_scr, kv_scr, kv_scr,
        pltpu.VMEM((ROWS, D_ATTN), F32), pltpu.VMEM((D_SSM // 128, BATCH * PITCH, 128), F32),
        pltpu.VMEM((SUB_ROWS, D_SSM), F32), pltpu.VMEM((SUB_ROWS, 2 * SBLK), F32),
        pltpu.VMEM((SUB_ROWS, 2 * SBLK), F32), pltpu.VMEM((SUB_ROWS, D_SSM), F32),
    ]
    return pl.pallas_call(
        _mixer_prompt_kernel, grid=(N_CHUNKS,), in_specs=in_specs, out_specs=out_specs, out_shape=out_shape,
        scratch_shapes=scratch, name="mixer_prompt",
        compiler_params=pltpu.CompilerParams(dimension_semantics=("arbitrary",), vmem_limit_bytes=VMEM_LIMIT),
    )(sinks, x_prompt, gmix, win, bias, bblk, lamr, lami, cblk, dskip, wglu, bglu, ga, gs, wout,
      gffn, wrt, br, _tri(ROUTE_BLOCK))


SGRP = 16
SPITCH = DEC_BATCH + 8
N_SGRP = DEC_BATCH // SGRP


def _mixer_sample_kernel(x_ref, gmix_ref, win_ref, ck_ref, cv_ref, sbias_ref, sinkc_ref, bblk_ref, lamr_ref,
                         lami_ref, cblk_ref, dskip_ref, wglu_ref, bglu_ref, ga_ref, gs_ref, wout_ref,
                         x0r_ref, x0i_ref, gffn_ref, wrt_ref, br_ref, tri_ref, cnt_in_ref,
                         hbuf_ref, hnbuf_ref, idxbuf_ref, gwbuf_ref, rankbuf_ref,
                         hmid_ref, knew_ref, vnew_ref, sre_ref, sim_ref, hn_ref, idx_ref, gw_ref, rank_ref, cnt_ref,
                         proj_s, qall_s, oall_s, kn8_s, vn8_s):
    del hbuf_ref, hnbuf_ref, idxbuf_ref, gwbuf_ref, rankbuf_ref
    g = pl.program_id(0)
    lo = lax.broadcasted_iota(jnp.int32, (DEC_BATCH, 128), 1) < HEAD_DIM

    @pl.when(g == 0)
    def _():
        proj = _bdot(_rms(x_ref[...], gmix_ref[...]), win_ref[...])
        proj_s[...] = proj
        for h in range(N_HEADS):
            jq, half, kv = h // 2, h % 2, h // 4
            q2 = proj[:, 128 * jq:128 * (jq + 1)] * (HEAD_DIM ** -0.5)
            if half != kv:
                q2 = pltpu.roll(q2, HEAD_DIM, 1)
            hr = slice(h * SPITCH, h * SPITCH + DEC_BATCH)
            qall_s[hr, :] = jnp.where(lo if kv == 0 else jnp.logical_not(lo), q2, 0.0)
            kn8_s[hr, :] = proj[:, D_ATTN:D_ATTN + D_KV]
            vn8_s[hr, :] = proj[:, D_ATTN + D_KV:D_ATTN + 2 * D_KV]

    def head_rows(ref):
        return jnp.stack([ref[pl.ds(g * SGRP + ii, N_HEADS, stride=SPITCH), :] for ii in range(SGRP)], axis=0)

    qe = head_rows(qall_s)
    knew = head_rows(kn8_s)
    vnew = head_rows(vn8_s)
    s = jnp.einsum('bhd,bkd->bhk', qe.astype(BF16), ck_ref[...].astype(BF16),
                   preferred_element_type=F32) + sbias_ref[...][None]
    s_new = jnp.sum(qe * knew, axis=-1, keepdims=True)
    sink = sinkc_ref[...][None]
    m = jnp.maximum(jnp.maximum(jnp.max(s, axis=-1, keepdims=True), s_new), sink)
    p = jnp.exp(s - m)
    p_new = jnp.exp(s_new - m)
    den = jnp.sum(p, axis=-1, keepdims=True) + p_new + jnp.exp(sink - m)
    o = (jnp.einsum('bhk,bkd->bhd', p.astype(BF16), cv_ref[...].astype(BF16),
                    preferred_element_type=F32) + p_new * vnew) / den
    for ii in range(SGRP):
        oall_s[pl.ds(g * SGRP + ii, N_HEADS, stride=SPITCH), :] = o[ii]

    @pl.when(g == N_SGRP - 1)
    def _():
        proj = proj_s[...]
        blocks = []
        for jq in range(N_HEADS // 2):
            oa = oall_s[(2 * jq) * SPITCH:(2 * jq) * SPITCH + DEC_BATCH, :]
            ob = oall_s[(2 * jq + 1) * SPITCH:(2 * jq + 1) * SPITCH + DEC_BATCH, :]
            if jq // 2 == 0:
                blocks.append(jnp.where(lo, oa, pltpu.roll(ob, HEAD_DIM, 1)))
            else:
                blocks.append(jnp.where(lo, pltpu.roll(oa, HEAD_DIM, 1), ob))
        attn = jnp.concatenate(blocks, axis=1)
        u = proj[:, D_ATTN + 2 * D_KV:]
        ys = []
        for j in range(N_SBLK):
            bu = _bdot(u[:, 128 * j:128 * (j + 1)], bblk_ref[j])
            lr = lamr_ref[j]
            li = lami_ref[j]
            cols = slice(SBLK * j, SBLK * (j + 1))
            sr = x0r_ref[:, cols]
            si = x0i_ref[:, cols]
            nr = lr * sr - li * si + bu[:, 0:SBLK]
            ni = lr * si + li * sr + bu[:, SBLK:2 * SBLK]
            sre_ref[:, cols] = nr
            sim_ref[:, cols] = ni
            ys.append(_bdot(jnp.concatenate([nr, ni], axis=1), cblk_ref[j]))
        sn = _ssm_post(jnp.concatenate(ys, axis=1), u, dskip_ref[...], wglu_ref[...], bglu_ref[...], gs_ref[...])
        an = _rms(attn, ga_ref[...])
        h = x_ref[...] + _bdot(an, wout_ref[0:D_ATTN, :]) + _bdot(sn, wout_ref[D_ATTN:2 * D_ATTN, :])
        h = jnp.concatenate([h, jnp.zeros((DEC_BATCH, D_MODEL), F32)], axis=0)
        hmid_ref[...] = h
        cnt_ref[...] = cnt_in_ref[...]
        _route(h, gffn_ref, wrt_ref, br_ref, tri_ref, cnt_ref, hn_ref, idx_ref, gw_ref, rank_ref)
        knew_ref[...] = proj[:, D_ATTN:D_ATTN + D_KV]
        vnew_ref[...] = proj[:, D_ATTN + D_KV:D_ATTN + 2 * D_KV]


def _mixer_sample(x_s, gmix, win, ck, cv, sbias, sinkc, bblk, lamr, lami, cblk, dskip, wglu, bglu, ga, gs, wout,
                  x0r, x0i, gffn, wrt, br, cnt_in, hbuf, hnbuf, idxbuf, gwbuf, rankbuf):
    nst = SSM_G * SSM_P
    in_specs = [
        _full((DEC_BATCH, D_MODEL)), _full((1, D_MODEL)), _full((D_MODEL, D_IN)),
        pl.BlockSpec((SGRP, WINDOW, D_KV), lambda g: (g, 0, 0)),
        pl.BlockSpec((SGRP, WINDOW, D_KV), lambda g: (g, 0, 0)),
        _full((N_HEADS, WINDOW)), _full((N_HEADS, 1)),
        _full((N_SBLK, 128, 2 * SBLK)), _full((N_SBLK, 1, SBLK)), _full((N_SBLK, 1, SBLK)),
        _full((N_SBLK, 2 * SBLK, 128)), _full((1, D_SSM)), _full((D_SSM, D_SSM)), _full((1, D_SSM)),
        _full((1, D_ATTN)), _full((1, D_SSM)), _full((D_MODEL, D_MODEL)),
        _full((DEC_BATCH, nst)), _full((DEC_BATCH, nst)),
        _full((1, D_MODEL)), _full((N_EXPERTS, D_MODEL)), _full((N_EXPERTS, 1)),
        _full((2 * DEC_BATCH, 2 * DEC_BATCH)), _full((N_EXPERTS, 1)),
    ] + [pl.BlockSpec(memory_space=pl.ANY)] * 5
    tail_block = SAMPLE_ROW0 // (2 * DEC_BATCH)
    route_specs, route_shapes = _route_specs(2 * DEC_BATCH, lambda g: tail_block)
    out_shape = [
        jax.ShapeDtypeStruct((T_PAD, D_MODEL), F32),
        jax.ShapeDtypeStruct((DEC_BATCH, D_KV), F32), jax.ShapeDtypeStruct((DEC_BATCH, D_KV), F32),
        jax.ShapeDtypeStruct((DEC_BATCH, nst), F32), jax.ShapeDtypeStruct((DEC_BATCH, nst), F32),
    ] + route_shapes
    out_specs = [
        pl.BlockSpec((2 * DEC_BATCH, D_MODEL), lambda g: (tail_block, 0)),
        _full((DEC_BATCH, D_KV)), _full((DEC_BATCH, D_KV)),
        _full((DEC_BATCH, nst)), _full((DEC_BATCH, nst)),
    ] + route_specs
    head_rows = pltpu.VMEM((N_HEADS * SPITCH, 128), F32)
    scratch = [pltpu.VMEM((DEC_BATCH, D_IN), F32), head_rows, head_rows, head_rows, head_rows]
    return pl.pallas_call(
        _mixer_sample_kernel, grid=(N_SGRP,), in_specs=in_specs, out_specs=out_specs, out_shape=out_shape,
        scratch_shapes=scratch, input_output_aliases={24: 0, 25: 5, 26: 6, 27: 7, 28: 8}, name="mixer_sample",
        compiler_params=pltpu.CompilerParams(dimension_semantics=("arbitrary",), vmem_limit_bytes=VMEM_LIMIT),
    )(x_s, gmix, win, ck, cv, sbias, sinkc, bblk, lamr, lami, cblk, dskip, wglu, bglu, ga, gs, wout, x0r, x0i,
      gffn, wrt, br, _tri(2 * DEC_BATCH), cnt_in, hbuf, hnbuf, idxbuf, gwbuf, rankbuf)


def _route(h, g_ref, wrt_ref, br_ref, tri_ref, cnt_ref, hn_ref, idx_ref, gw_ref, rank_ref):
    hn = _rms(h, g_ref[...])
    hn_ref[...] = _pack_pairs(hn)
    hn_hi = hn.astype(BF16)
    hn_lo = (hn - hn_hi.astype(F32)).astype(BF16)
    w = wrt_ref[...]
    w_hi = w.astype(BF16)
    w_lo = (w - w_hi.astype(F32)).astype(BF16)
    nt = (((1,), (1,)), ((), ()))
    logits = (lax.dot_general(w_hi, hn_hi, nt, preferred_element_type=F32)
              + lax.dot_general(w_lo, hn_hi, nt, preferred_element_type=F32)
              + lax.dot_general(w_hi, hn_lo, nt, preferred_element_type=F32)) + br_ref[...]
    eidx = lax.broadcasted_iota(jnp.int32, logits.shape, 0)
    vals, onehots = [], []
    l = logits
    for k in range(TOP_K):
        m = jnp.max(l, axis=0, keepdims=True)
        ik = jnp.min(jnp.where(l == m, eidx, N_EXPERTS), axis=0, keepdims=True)
        oh = eidx == ik
        idx_ref[k:k + 1, :] = ik
        vals.append(m)
        onehots.append(oh)
        l = jnp.where(oh, -jnp.inf, l)
    exps = [jnp.exp(v - vals[0]) for v in vals]
    den = exps[0] + exps[1] + exps[2] + exps[3]
    for k in range(TOP_K):
        gw_ref[k:k + 1, :] = exps[k] / den
    member = jnp.zeros(logits.shape, F32)
    for oh in onehots:
        member = member + jnp.where(oh, 1.0, 0.0)
    wblk = tri_ref.shape[0]
    base = cnt_ref[...].astype(F32)
    befores = []
    for cb in range(h.shape[0] // wblk):
        mblk = member[:, cb * wblk:(cb + 1) * wblk]
        befores.append(jnp.dot(mblk.astype(BF16), tri_ref[...], preferred_element_type=F32) + base)
        base = base + jnp.sum(mblk, axis=1, keepdims=True)
    before = jnp.concatenate(befores, axis=1)
    for k in range(TOP_K):
        rank_ref[k:k + 1, :] = jnp.sum(jnp.where(onehots[k], before, 0.0), axis=0, keepdims=True).astype(jnp.int32)
    cnt_ref[...] = base.astype(jnp.int32)


def _route_specs(rows, block):
    specs = [pl.BlockSpec((rows, D_PACK), lambda i: (block(i), 0)), pl.BlockSpec((TOP_K, rows), lambda i: (0, block(i))),
             pl.BlockSpec((TOP_K, rows), lambda i: (0, block(i))), pl.BlockSpec((TOP_K, rows), lambda i: (0, block(i))),
             _full((N_EXPERTS, 1))]
    shapes = [jax.ShapeDtypeStruct((T_PAD, D_PACK), jnp.int32), jax.ShapeDtypeStruct((TOP_K, T_PAD), jnp.int32),
              jax.ShapeDtypeStruct((TOP_K, T_PAD), F32), jax.ShapeDtypeStruct((TOP_K, T_PAD), jnp.int32),
              jax.ShapeDtypeStruct((N_EXPERTS, 1), jnp.int32)]
    return specs, shapes


def _tri(n):
    return jnp.asarray(np.triu(np.ones((n, n), np.float32), 1), BF16)


def _place_kernel(off_ref, idx_ref, rank_ref, pos_ref):
    idx = idx_ref[...]
    pos = rank_ref[...]
    for e in range(N_EXPERTS):
        pos = pos + jnp.where(idx == e, off_ref[e], 0)
    pos_ref[...] = pos


def _place(offsets, idx, rank):
    return pl.pallas_call(
        _place_kernel,
        in_specs=[pl.BlockSpec(memory_space=pltpu.SMEM), pl.BlockSpec(memory_space=pltpu.VMEM),
                  pl.BlockSpec(memory_space=pltpu.VMEM)],
        out_specs=pl.BlockSpec(memory_space=pltpu.VMEM),
        out_shape=jax.ShapeDtypeStruct((TOP_K, T_PAD), jnp.int32), name="place",
    )(offsets, idx, rank)


def _sc_mesh():
    return plsc.VectorSubcoreMesh(core_axis_name="core", subcore_axis_name="subcore")


def _sc_dispatch(rows, pos):
    n, d = rows.shape
    nblk = n // SC_ROWS
    pos_w = pos.reshape(TOP_K, nblk, SC_ROWS).transpose(1, 0, 2)

    @functools.partial(pl.kernel, out_type=jax.ShapeDtypeStruct((P_ROWS, d), rows.dtype), mesh=_sc_mesh(),
                       scratch_types=[], name="dispatch")
    def run(x_hbm, i_hbm, o_hbm):
        def body(x_vmem, i_vmem):
            for k in range(TOP_K):
                pltpu.sync_copy(x_vmem, o_hbm.at[i_vmem.at[0, k]])

        pltpu.emit_pipeline(
            body, grid=(nblk,),
            in_specs=[pl.BlockSpec((SC_ROWS, d), lambda i: (i, 0)),
                      pl.BlockSpec((1, TOP_K, SC_ROWS), lambda i: (i, 0, 0))],
            out_specs=[], core_axis_name=("core", "subcore"), dimension_semantics=(pltpu.PARALLEL,),
        )(x_hbm, i_hbm)

    return run(rows, pos_w)


def _sc_combine(rows, flat_pos):
    _, d = rows.shape
    n = flat_pos.shape[0]
    w = SC_ROWS_COMBINE
    nblk = n // w
    pos_w = flat_pos.reshape(nblk, 1, w)

    @functools.partial(pl.kernel, out_type=jax.ShapeDtypeStruct((n, d), rows.dtype), mesh=_sc_mesh(),
                       scratch_types=[], name="combine")
    def run(x_hbm, i_hbm, o_hbm):
        def body(i_vmem, o_vmem):
            pltpu.sync_copy(x_hbm.at[i_vmem.at[0, 0]], o_vmem)

        pltpu.emit_pipeline(
            body, grid=(nblk,),
            in_specs=[pl.BlockSpec((1, 1, w), lambda i: (i, 0, 0))],
            out_specs=[pl.BlockSpec((w, d), lambda i: (i, 0))],
            core_axis_name=("core", "subcore"), dimension_semantics=(pltpu.PARALLEL,),
        )(i_hbm, o_hbm)

    return run(rows, pos_w)


def _experts_kernel(ts_ref, trow_ref, tsz_ref, xs_hbm, w1_hbm, b1_ref, w2_hbm, b2_ref, ys_hbm,
                    w1_st, w2_st, w1_s, w2_s, xbuf, ybuf, wsem, xsem, ysem):
    e = pl.program_id(0)
    n_valid = ts_ref[N_EXPERTS]

    def w_copies(ex, slot):
        return (pltpu.make_async_copy(w1_hbm.at[ex], w1_st.at[slot], wsem.at[0, slot]),
                pltpu.make_async_copy(w2_hbm.at[ex], w2_st.at[slot], wsem.at[1, slot]))

    def x_copy(t, slot, rows):
        src = xs_hbm.at[pl.ds(pl.multiple_of(trow_ref[t], TM_HALF), rows)]
        return pltpu.make_async_copy(src, xbuf.at[slot, pl.ds(0, rows)], xsem.at[slot])

    def y_copy(t, slot, rows):
        dst = ys_hbm.at[pl.ds(pl.multiple_of(trow_ref[t], TM_HALF), rows)]
        return pltpu.make_async_copy(ybuf.at[slot, pl.ds(0, rows)], dst, ysem.at[slot])

    def by_size(t, fn):
        for rows in (TM, TM_HALF):
            @pl.when(tsz_ref[t] == rows)
            def _():
                fn(rows)

    @pl.when(e == 0)
    def _():
        for c in w_copies(0, 0):
            c.start()
        by_size(0, lambda rows: x_copy(0, 0, rows).start())

    @pl.when(e + 1 < N_EXPERTS)
    def _():
        for c in w_copies(e + 1, (e + 1) % 2):
            c.start()

    wslot = e % 2
    for c in w_copies(e, wslot):
        c.wait()
    for r in range(4):
        rs = slice(256 * r, 256 * (r + 1))
        w1_s[rs, :] = w1_st[wslot, rs, :].astype(BF16)
        w2_s[rs, :] = w2_st[wslot, rs, :].astype(BF16)

    def tile(t, carry):
        slot = t % 2

        @pl.when(t + 1 < n_valid)
        def _():
            by_size(t + 1, lambda rows: x_copy(t + 1, 1 - slot, rows).start())

        by_size(t, lambda rows: x_copy(t, slot, rows).wait())

        @pl.when(t >= 2)
        def _():
            by_size(t - 2, lambda rows: y_copy(t - 2, slot, rows).wait())

        def compute(rows):
            hdn = _bdot(_unpack_pairs(xbuf[slot, 0:rows]), w1_s[...]) + b1_ref[0]
            gt = jnp.minimum(hdn[:, :D_FF], SWIGLU_LIMIT)
            up = jnp.clip(hdn[:, D_FF:], -SWIGLU_LIMIT, SWIGLU_LIMIT)
            act = (up + 1.0) * gt * _sigmoid(SWIGLU_ALPHA * gt)
            ybuf[slot, 0:rows] = _pack_pairs(_bdot(act, w2_s[...]) + b2_ref[0])
            y_copy(t, slot, rows).start()

        by_size(t, compute)
        return carry

    lax.fori_loop(ts_ref[e], ts_ref[e + 1], tile, 0)

    @pl.when(e == N_EXPERTS - 1)
    def _():
        @pl.when(n_valid >= 2)
        def _():
            by_size(n_valid - 2, lambda rows: y_copy(n_valid - 2, n_valid % 2, rows).wait())

        by_size(n_valid - 1, lambda rows: y_copy(n_valid - 1, (n_valid - 1) % 2, rows).wait())


def _tile_plan(counts):
    units = (counts + (TM_HALF - 1)) // TM_HALF
    unit_end = jnp.cumsum(units)
    offsets = ((unit_end - units) * TM_HALF).astype(jnp.int32)
    n_full = units // 2
    tiles_per = n_full + units % 2
    tile_end = jnp.cumsum(tiles_per)
    tile_start = jnp.concatenate([jnp.zeros((1,), jnp.int32), tile_end.astype(jnp.int32)])
    t = jnp.arange(MAX_TILES, dtype=jnp.int32)[:, None]
    done = tile_end[None, :] <= t
    first_tile = jnp.max(jnp.where(done, tile_end[None, :], 0), axis=1)
    first_row = jnp.max(jnp.where(done, unit_end[None, :] * TM_HALF, 0), axis=1)
    is_half = jnp.any((units[None, :] % 2 == 1) & (t == tile_end[None, :] - 1), axis=1)
    valid = t[:, 0] < tile_end[-1]
    tile_row = jnp.where(valid, first_row + (t[:, 0] - first_tile) * TM, 0).astype(jnp.int32)
    tile_rows = jnp.where(valid, jnp.where(is_half, TM_HALF, TM), 0).astype(jnp.int32)
    return offsets, tile_start, tile_row, tile_rows


def _experts(tile_start, tile_row, tile_rows, xs, w_up, b_up, w_down, b_down):
    wsel = lambda e, *_: (e, 0, 0)
    hbm = pl.BlockSpec(memory_space=pl.ANY)
    grid_spec = pltpu.PrefetchScalarGridSpec(
        num_scalar_prefetch=3, grid=(N_EXPERTS,),
        in_specs=[hbm, hbm, pl.BlockSpec((1, 1, 2 * D_FF), wsel), hbm, pl.BlockSpec((1, 1, D_MODEL), wsel)],
        out_specs=hbm,
        scratch_shapes=[pltpu.VMEM((2, D_MODEL, 2 * D_FF), F32), pltpu.VMEM((2, D_FF, D_MODEL), F32),
                        pltpu.VMEM((D_MODEL, 2 * D_FF), BF16), pltpu.VMEM((D_FF, D_MODEL), BF16),
                        pltpu.VMEM((2, TM, D_PACK), jnp.int32), pltpu.VMEM((2, TM, D_PACK), jnp.int32),
                        pltpu.SemaphoreType.DMA((2, 2)), pltpu.SemaphoreType.DMA((2,)),
                        pltpu.SemaphoreType.DMA((2,))],
    )
    return pl.pallas_call(
        _experts_kernel, grid_spec=grid_spec, out_shape=jax.ShapeDtypeStruct((P_ROWS, D_PACK), jnp.int32),
        name="experts",
        compiler_params=pltpu.CompilerParams(dimension_semantics=("arbitrary",), vmem_limit_bytes=VMEM_LIMIT),
    )(tile_start, tile_row, tile_rows, xs, w_up, b_up.reshape(N_EXPERTS, 1, 2 * D_FF), w_down,
      b_down.reshape(N_EXPERTS, 1, D_MODEL))


def _final_kernel(h_ref, yg_ref, gw_ref, p_ref, gple_ref, wg_ref, wp_ref, gfin_ref, *rest):
    out_ref = rest[-1]
    rows = h_ref.shape[0]
    gw = gw_ref[...]
    h = h_ref[...]
    for k in range(TOP_K):
        h = h + gw[:, k:k + 1] * _unpack_pairs(yg_ref[k])
    gate = _sigmoid(_bdot(_rms(h, gple_ref[...]), wg_ref[...]))
    h = h + gate * _bdot(p_ref[...].reshape(rows, PLE_DIM), wp_ref[...])
    out_ref[...] = _rms(h, gfin_ref[...]).reshape(out_ref.shape)


def _final_prompt(part, hmid, yg, gwt, p_prompt, gple, wg, wp, gfin, y_prev):
    nb = 4
    rows = nb * CHUNK
    nbh = BATCH // nb
    c0 = part * PART_CHUNKS
    rb = lambda c, b: ((c0 + c) * nbh + b, 0)
    in_specs = [pl.BlockSpec((rows, D_MODEL), rb),
                pl.BlockSpec((TOP_K, rows, D_PACK), lambda c, b: (0, c * nbh + b, 0)),
                pl.BlockSpec((rows, TOP_K), rb),
                pl.BlockSpec((nb, CHUNK, PLE_DIM), lambda c, b: (b, c0 + c, 0)),
                _full((1, D_MODEL)), _full((D_MODEL, D_MODEL)), _full((PLE_DIM, D_MODEL)), _full((1, D_MODEL))]
    args = [hmid, yg, gwt, p_prompt, gple, wg, wp, gfin]
    aliases = {}
    if y_prev is not None:
        in_specs.append(pl.BlockSpec(memory_space=pl.ANY))
        args.append(y_prev)
        aliases = {len(args) - 1: 0}
    return pl.pallas_call(
        _final_kernel, grid=(PART_CHUNKS, nbh), in_specs=in_specs,
        out_specs=pl.BlockSpec((nb, CHUNK, D_MODEL), lambda c, b: (b, c0 + c, 0)),
        out_shape=jax.ShapeDtypeStruct((BATCH, SEQ, D_MODEL), F32), name="final_prompt",
        input_output_aliases=aliases,
        compiler_params=pltpu.CompilerParams(dimension_semantics=("arbitrary", "arbitrary"),
                                             vmem_limit_bytes=VMEM_LIMIT),
    )(*args)


def _final_sample(hmid, yg, gwt, p_sample, gple, wg, wp, gfin):
    blk = SAMPLE_ROW0 // DEC_BATCH
    return pl.pallas_call(
        _final_kernel, grid=(1,),
        in_specs=[pl.BlockSpec((DEC_BATCH, D_MODEL), lambda i: (blk, 0)),
                  pl.BlockSpec((TOP_K, DEC_BATCH, D_PACK), lambda i: (0, PART_CHUNKS * ROWS // DEC_BATCH, 0)),
                  pl.BlockSpec((DEC_BATCH, TOP_K), lambda i: (blk, 0)),
                  _full((DEC_BATCH, PLE_DIM)),
                  _full((1, D_MODEL)), _full((D_MODEL, D_MODEL)), _full((PLE_DIM, D_MODEL)), _full((1, D_MODEL))],
        out_specs=_full((DEC_BATCH, D_MODEL)),
        out_shape=jax.ShapeDtypeStruct((DEC_BATCH, D_MODEL), F32), name="final_sample",
        compiler_params=pltpu.CompilerParams(dimension_semantics=("arbitrary",), vmem_limit_bytes=VMEM_LIMIT),
    )(hmid, yg, gwt, p_sample, gple, wg, wp, gfin)


def _alibi_tables():
    slopes = 2.0 ** (-8.0 * (np.arange(N_HEADS, dtype=np.float64) + 1.0) / N_HEADS)
    i = np.arange(CHUNK)[:, None]
    j = np.arange(2 * CHUNK)[None, :]
    dist = i + CHUNK - j
    valid = (dist >= 0) & (dist <= WINDOW)
    tabs = []
    for has_prev in (False, True):
        ok = valid & ((j >= CHUNK) | has_prev)
        tabs.append(np.where(ok[None], -slopes[:, None, None] * dist[None], NEG))
    prompt = np.stack([np.stack([np.concatenate([t[h] for h in grp], axis=0) for grp in (NAT_HEADS, ROT_HEADS)])
                       for t in tabs]).astype(np.float32)
    wb = min(WINDOW, PAST_LEN)
    sample = (-slopes[:, None] * (wb - np.arange(wb))[None, :]).astype(np.float32)
    return prompt, sample


def kernel(x_prompt, x_sample, cache_k_win, cache_v_win, state_ssm_re, state_ssm_im, p_prompt, p_sample, norm_mix, w_in, sinks, ssm_lam_re, ssm_lam_im, ssm_log_step, ssm_b_re, ssm_b_im, ssm_c_re, ssm_c_im, ssm_d, w_glu, b_glu, norm_attn_out, norm_ssm_out, w_out, norm_ffn, w_router, b_router, w_up, b_up, w_down, b_down, norm_ple, w_ple_gate, w_ple_proj, norm_final):
    nst = SSM_G * SSM_P
    bias_np, sbias_np = _alibi_tables()
    bias = jnp.asarray(bias_np)
    sbias = jnp.asarray(sbias_np)

    lbr, lbi, bbr, bbi = _prep(ssm_lam_re[0], ssm_lam_im[0], ssm_log_step[0], ssm_b_re[0], ssm_b_im[0])
    bblk, lamr, lami, cblk = _s5_blocks(lbr, lbi, bbr, bbi, ssm_c_re[0], ssm_c_im[0])

    gmix = norm_mix[0].reshape(1, D_MODEL)
    win = w_in[0].astype(BF16)
    dskip = ssm_d[0].reshape(1, D_SSM)
    wglu = w_glu[0].astype(BF16)
    bglu = b_glu[0].reshape(1, D_SSM)
    ga = norm_attn_out[0].reshape(1, D_ATTN)
    gs = norm_ssm_out[0].reshape(1, D_SSM)
    wout = w_out[0].astype(BF16)
    sink = sinks[0]

    gffn = norm_ffn[0].reshape(1, D_MODEL)
    wrt = w_router[0].T
    br = b_router[0].reshape(N_EXPERTS, 1)
    hbuf, k_last, v_last, re_p, im_p, hnbuf, idxbuf, gwbuf, rankbuf, cnt_p = _mixer_prompt(
        x_prompt, sink, gmix, win, bias, bblk, lamr, lami, cblk, dskip, wglu, bglu, ga, gs, wout, gffn, wrt, br)

    ck = cache_k_win[0].reshape(DEC_BATCH, WINDOW, D_KV)
    cv = cache_v_win[0].reshape(DEC_BATCH, WINDOW, D_KV)
    hmid, k_new, v_new, re_s, im_s, hn, idx, gw, rank, counts = _mixer_sample(
        x_sample.reshape(DEC_BATCH, D_MODEL), gmix, win, ck, cv, sbias, sink.reshape(N_HEADS, 1), bblk, lamr, lami,
        cblk, dskip, wglu, bglu, ga, gs, wout, state_ssm_re[0].reshape(DEC_BATCH, nst),
        state_ssm_im[0].reshape(DEC_BATCH, nst), gffn, wrt, br, cnt_p, hbuf, hnbuf, idxbuf, gwbuf, rankbuf)

    offsets, tile_start, tile_row, tile_rows = _tile_plan(counts[:, 0])

    pos = _place(offsets, idx, rank)
    xs = _sc_dispatch(hn, pos)
    ys = _experts(tile_start, tile_row, tile_rows, xs, w_up[0], b_up[0], w_down[0], b_down[0])
    ygs = []
    for part in range(N_PARTS):
        r0 = part * PART_CHUNKS * ROWS
        r1 = T_PAD if part == N_PARTS - 1 else r0 + PART_CHUNKS * ROWS
        ygs.append(_sc_combine(ys, pos[:, r0:r1].reshape(TOP_K * (r1 - r0))).reshape(TOP_K, r1 - r0, D_PACK))

    gwt = gw.T
    gple = norm_ple[0].reshape(1, D_MODEL)
    wg = w_ple_gate[0].astype(BF16)
    wp = w_ple_proj[0].astype(BF16)
    gfin = norm_final.reshape(1, D_MODEL)
    y_prompt = None
    for part in range(N_PARTS):
        y_prompt = _final_prompt(part, hmid, ygs[part], gwt, p_prompt[0], gple, wg, wp, gfin, y_prompt)
    y_sample = _final_sample(hmid, ygs[-1], gwt, p_sample[0].reshape(DEC_BATCH, PLE_DIM), gple, wg, wp, gfin)

    k_win_s = jnp.concatenate([ck[:, 1:], k_new[:, None, :]], axis=1)
    v_win_s = jnp.concatenate([cv[:, 1:], v_new[:, None, :]], axis=1)
    kv5 = (1, BATCH, CHUNK, N_KV, HEAD_DIM)
    skv5 = (1, DEC_BATCH, WINDOW, N_KV, HEAD_DIM)
    return (y_prompt, y_sample.reshape(DEC_BATCH, 1, D_MODEL),
            k_last.reshape(kv5), v_last.reshape(kv5),
            re_p.reshape(1, BATCH, SSM_G, SSM_P), im_p.reshape(1, BATCH, SSM_G, SSM_P),
            k_win_s.reshape(skv5), v_win_s.reshape(skv5),
            re_s.reshape(1, DEC_BATCH, SSM_G, SSM_P), im_s.reshape(1, DEC_BATCH, SSM_G, SSM_P))
```

```python
import functools

import numpy as np
import jax
import jax.numpy as jnp
from jax import lax
from jax.experimental import pallas as pl
from jax.experimental.pallas import tpu as pltpu
from jax.experimental.pallas import tpu_sc as plsc

F32 = jnp.float32
BF16 = jnp.bfloat16

D_MODEL = 1024
BATCH = 8
SEQ = 2048
DEC_BATCH = 128
PAST_LEN = 16384
HEAD_DIM = 64
D_ATTN = 512
N_HEADS = 8
N_KV = 2
D_KV = N_KV * HEAD_DIM
WINDOW = 128
D_SSM = 512
SSM_H = 16
SSM_G = 32
SSM_P = 64
D_IN = D_ATTN + 2 * D_KV + D_SSM
N_EXPERTS = 32
TOP_K = 4
D_FF = 1024
SWIGLU_LIMIT = 7.0
SWIGLU_ALPHA = 1.702
PLE_DIM = 256
EPS = 1e-5
NEG = -1e30

CHUNK = 128
N_CHUNKS = SEQ // CHUNK
ROWS = BATCH * CHUNK
PITCH = CHUNK + 8
NAT_HEADS = (0, 2, 5, 7)
ROT_HEADS = (1, 3, 4, 6)
SUB_T = 32
SUB_ROWS = SUB_T * BATCH
N_SBLK = 4
SBLK = 512
T_REAL = BATCH * SEQ + DEC_BATCH
T_PAD = T_REAL + 128
SAMPLE_ROW0 = BATCH * SEQ
ROUTE_BLOCK = 512
TILE_SIZES = (1024, 512, 256)
TM = TILE_SIZES[0]
TM_UNIT = TILE_SIZES[-1]
P_ROWS = (T_PAD * TOP_K + N_EXPERTS * (TM_UNIT - 1)) // TM_UNIT * TM_UNIT
MAX_TILES = P_ROWS // TM + 2 * N_EXPERTS + 1
SC_ROWS = 40
SC_ROWS_COMBINE = 32
N_PARTS = 4
PART_CHUNKS = N_CHUNKS // N_PARTS
D_PACK = D_MODEL // 2
VMEM_LIMIT = 56 * 1024 * 1024


def _rms(x, g):
    return x * lax.rsqrt(jnp.mean(x * x, axis=-1, keepdims=True) + EPS) * g


def _sigmoid(x):
    return 1.0 / (1.0 + jnp.exp(-x))


def _gelu_tanh(x):
    c = np.float32(np.sqrt(2.0 / np.pi))
    return 0.5 * x * (1.0 + jnp.tanh(c * (x + 0.044715 * (x * x * x))))


def _bdot(a, b):
    return jnp.dot(a.astype(BF16), b, preferred_element_type=F32)


def _pack_pairs(x):
    n = x.shape[1] // 2
    lo = lax.bitcast_convert_type(x[:, :n].astype(BF16).astype(F32), jnp.int32)
    hi = lax.bitcast_convert_type(x[:, n:].astype(BF16).astype(F32), jnp.int32)
    return lax.shift_right_logical(lo, 16) | hi


def _unpack_pairs(w):
    lo = lax.bitcast_convert_type(lax.shift_left(w, 16), F32)
    hi = lax.bitcast_convert_type(w & jnp.int32(-65536), F32)
    return jnp.concatenate([lo, hi], axis=1)


def _full(shape):
    n = len(shape)
    return pl.BlockSpec(shape, lambda *_: (0,) * n)


def _prep_kernel(lr_ref, li_ref, ls_ref, br_ref, bi_ref, lbr_ref, lbi_ref, bbr_ref, bbi_ref):
    lr = lr_ref[...]
    li = li_ref[...]
    step = jnp.exp(ls_ref[...])
    zr = lr * step
    zi = li * step
    mag = jnp.exp(zr)
    lbr = mag * jnp.cos(zi)
    lbi = mag * jnp.sin(zi)
    lbr_ref[...] = lbr
    lbi_ref[...] = lbi
    nr = lbr - 1.0
    den = lr * lr + li * li
    cr = (nr * lr + lbi * li) / den
    ci = (lbi * lr - nr * li) / den
    br = br_ref[...]
    bi = bi_ref[...]
    bbr_ref[...] = cr * br - ci * bi
    bbi_ref[...] = cr * bi + ci * br


def _prep(lam_re, lam_im, log_step, b_re, b_im):
    g, p, h = SSM_G, SSM_P, SSM_H
    out = pl.pallas_call(
        _prep_kernel,
        out_shape=[jax.ShapeDtypeStruct((g, 1, p), F32), jax.ShapeDtypeStruct((g, 1, p), F32),
                   jax.ShapeDtypeStruct((g, h, p), F32), jax.ShapeDtypeStruct((g, h, p), F32)],
        name="s5_prep",
    )(lam_re.reshape(g, 1, p), lam_im.reshape(g, 1, p), log_step.reshape(g, 1, 1),
      jnp.transpose(b_re, (0, 2, 1)), jnp.transpose(b_im, (0, 2, 1)))
    return out


def _s5_blocks(lbr, lbi, bbr, bbi, c_re, c_im):
    eye = jnp.eye(8, dtype=F32)
    shp = (N_SBLK, 8, SSM_H, SSM_P)

    def in_map(b):
        return (b.reshape(shp)[:, :, :, None, :] * eye[None, :, None, :, None]).reshape(N_SBLK, 128, SBLK)

    def out_map(c):
        ct = jnp.transpose(c.reshape(shp), (0, 1, 3, 2))
        return (ct[:, :, :, None, :] * eye[None, :, None, :, None]).reshape(N_SBLK, SBLK, 128)

    bblk = jnp.concatenate([in_map(bbr), in_map(bbi)], axis=-1).astype(BF16)
    cblk = jnp.concatenate([out_map(c_re), -out_map(c_im)], axis=1).astype(BF16)
    return bblk, lbr.reshape(N_SBLK, 1, SBLK), lbi.reshape(N_SBLK, 1, SBLK), cblk


def _ssm_post(y_lin, u, dskip, wglu, bglu, gs):
    y = _gelu_tanh(y_lin + dskip * u)
    y = y * _sigmoid(_bdot(y, wglu) + bglu)
    return _rms(y, gs)


def _mixer_prompt_kernel(sinks_ref, x_ref, gmix_ref, win_ref, bias_ref, bblk_ref, lamr_ref, lami_ref,
                         cblk_ref, dskip_ref, wglu_ref, bglu_ref, ga_ref, gs_ref, wout_ref,
                         gffn_ref, wrt_ref, br_ref, tri_ref,
                         hmid_ref, klast_ref, vlast_ref, sre_ref, sim_ref, hn_ref, idx_ref, gw_ref, rank_ref, cnt_ref,
                         proj_s, u_s, kprev_s, kprevr_s, vprev_s, vprevr_s, attn_s, ssm_s, utb_s, bu_s, xs_s, ytb_s):
    c = pl.program_id(0)

    @pl.when(c == 0)
    def _():
        zkv = jnp.zeros(kprev_s.shape, BF16)
        kprev_s[...] = zkv
        kprevr_s[...] = zkv
        vprev_s[...] = zkv
        vprevr_s[...] = zkv
        sre_ref[...] = jnp.zeros(sre_ref.shape, F32)
        sim_ref[...] = jnp.zeros(sim_ref.shape, F32)
        cnt_ref[...] = jnp.zeros(cnt_ref.shape, jnp.int32)

    x = x_ref[...].reshape(ROWS, D_MODEL)
    proj = _bdot(_rms(x, gmix_ref[...]), win_ref[...])
    u0 = D_ATTN + 2 * D_KV
    proj_s[...] = proj[:, 0:u0]
    for jb in range(D_SSM // 128):
        for b in range(BATCH):
            u_s[jb, b * PITCH:b * PITCH + CHUNK, :] = proj[b * CHUNK:(b + 1) * CHUNK, u0 + 128 * jb:u0 + 128 * (jb + 1)]

    lo = lax.broadcasted_iota(jnp.int32, (CHUNK, 128), 1) < HEAD_DIM
    hi = jnp.logical_not(lo)
    table = jnp.minimum(c, 1)
    hrow = lax.broadcasted_iota(jnp.int32, (4 * CHUNK, 1), 0) // CHUNK

    def sink_col(heads):
        col = jnp.full((4 * CHUNK, 1), sinks_ref[heads[3]], F32)
        for n in (2, 1, 0):
            col = jnp.where(hrow == n, sinks_ref[heads[n]], col)
        return col

    sink_nat = sink_col(NAT_HEADS)
    sink_rot = sink_col(ROT_HEADS)

    def attend(q, k, v, bias, sink):
        s = lax.dot_general(q, k, (((1,), (1,)), ((), ())), preferred_element_type=F32) + bias
        m = jnp.maximum(jnp.max(s, axis=-1, keepdims=True), sink)
        p = jnp.exp(s - m)
        den = jnp.sum(p, axis=-1, keepdims=True) + jnp.exp(sink - m)
        return jnp.dot(p.astype(BF16), v, preferred_element_type=F32) / den

    def attn_body(b, carry):
        r0 = pl.multiple_of(b * CHUNK, CHUNK)
        rows = pl.ds(r0, CHUNK)
        kb = proj_s[rows, D_ATTN:D_ATTN + D_KV]
        vb = proj_s[rows, D_ATTN + D_KV:D_ATTN + 2 * D_KV]
        kb16 = kb.astype(BF16)
        vb16 = vb.astype(BF16)
        kbr16 = pltpu.roll(kb, HEAD_DIM, 1).astype(BF16)
        vbr16 = pltpu.roll(vb, HEAD_DIM, 1).astype(BF16)
        k_nat = jnp.concatenate([kprev_s[b], kb16], axis=0)
        k_rot = jnp.concatenate([kprevr_s[b], kbr16], axis=0)
        v_nat = jnp.concatenate([vprev_s[b], vb16], axis=0)
        v_rot = jnp.concatenate([vprevr_s[b], vbr16], axis=0)
        q2 = [proj_s[rows, 128 * jq:128 * (jq + 1)] * (HEAD_DIM ** -0.5) for jq in range(N_HEADS // 2)]
        q_nat = jnp.concatenate([jnp.where(lo if h % 2 == 0 else hi, q2[h // 2], 0.0) for h in NAT_HEADS],
                                axis=0).astype(BF16)
        q_rot = jnp.concatenate([jnp.where(lo if h % 2 == 0 else hi, q2[h // 2], 0.0) for h in ROT_HEADS],
                                axis=0).astype(BF16)
        o_nat = attend(q_nat, k_nat, v_nat, bias_ref[table, 0], sink_nat)
        o_rot = attend(q_rot, k_rot, v_rot, bias_ref[table, 1], sink_rot)
        for jq in range(N_HEADS // 2):
            blk = slice(CHUNK * jq, CHUNK * (jq + 1))
            even, odd = (o_nat, o_rot) if jq < 2 else (o_rot, o_nat)
            attn_s[rows, 128 * jq:128 * (jq + 1)] = jnp.where(lo, even[blk], odd[blk])
        kprev_s[b] = kb16
        kprevr_s[b] = kbr16
        vprev_s[b] = vb16
        vprevr_s[b] = vbr16
        return carry

    lax.fori_loop(0, BATCH, attn_body, 0)

    for sc in range(CHUNK // SUB_T):
        t0 = sc * SUB_T
        for i in range(SUB_T):
            for jb in range(D_SSM // 128):
                utb_s[i * BATCH:(i + 1) * BATCH, 128 * jb:128 * (jb + 1)] = (
                    u_s[jb, pl.ds(t0 + i, BATCH, stride=PITCH), :])
        u_tb = utb_s[...]
        for j in range(N_SBLK):
            bu_s[...] = _bdot(u_tb[:, 128 * j:128 * (j + 1)], bblk_ref[j])
            lr = jnp.broadcast_to(lamr_ref[j], (BATCH, SBLK))
            li = jnp.broadcast_to(lami_ref[j], (BATCH, SBLK))
            cols = slice(SBLK * j, SBLK * (j + 1))

            def step(i, carry):
                sr, si = carry
                r = pl.ds(pl.multiple_of(i * BATCH, BATCH), BATCH)
                nr = lr * sr - li * si + bu_s[r, 0:SBLK]
                ni = lr * si + li * sr + bu_s[r, SBLK:2 * SBLK]
                xs_s[r, 0:SBLK] = nr
                xs_s[r, SBLK:2 * SBLK] = ni
                return nr, ni

            sr, si = lax.fori_loop(0, SUB_T, step, (sre_ref[:, cols], sim_ref[:, cols]), unroll=True)
            sre_ref[:, cols] = sr
            sim_ref[:, cols] = si
            ytb_s[:, 128 * j:128 * (j + 1)] = _bdot(xs_s[...], cblk_ref[j])
        yn = _ssm_post(ytb_s[...], u_tb, dskip_ref[...], wglu_ref[...], bglu_ref[...], gs_ref[...])
        for i in range(SUB_T):
            for jb in range(D_SSM // 128):
                ssm_s[jb, pl.ds(t0 + i, BATCH, stride=PITCH), :] = (
                    yn[i * BATCH:(i + 1) * BATCH, 128 * jb:128 * (jb + 1)])

    an = _rms(attn_s[...], ga_ref[...])
    sn = jnp.concatenate(
        [jnp.concatenate([ssm_s[jb, b * PITCH:b * PITCH + CHUNK, :] for b in range(BATCH)], axis=0)
         for jb in range(D_SSM // 128)], axis=1)
    h = x + _bdot(an, wout_ref[0:D_ATTN, :]) + _bdot(sn, wout_ref[D_ATTN:2 * D_ATTN, :])
    hmid_ref[...] = h
    _route(h, gffn_ref, wrt_ref, br_ref, tri_ref, cnt_ref, hn_ref, idx_ref, gw_ref, rank_ref)

    @pl.when(c == N_CHUNKS - 1)
    def _():
        klast_ref[...] = proj_s[:, D_ATTN:D_ATTN + D_KV].reshape(BATCH, CHUNK, D_KV)
        vlast_ref[...] = proj_s[:, D_ATTN + D_KV:D_ATTN + 2 * D_KV].reshape(BATCH, CHUNK, D_KV)


def _mixer_prompt(x_prompt, sinks, gmix, win, bias, bblk, lamr, lami, cblk, dskip, wglu, bglu, ga, gs, wout,
                  gffn, wrt, br):
    smem = pl.BlockSpec(memory_space=pltpu.SMEM)
    in_specs = [
        smem,
        pl.BlockSpec((BATCH, CHUNK, D_MODEL), lambda c: (0, c, 0)),
        _full((1, D_MODEL)), _full((D_MODEL, D_IN)), _full((2, 2, 4 * CHUNK, 2 * CHUNK)),
        _full((N_SBLK, 128, 2 * SBLK)), _full((N_SBLK, 1, SBLK)), _full((N_SBLK, 1, SBLK)),
        _full((N_SBLK, 2 * SBLK, 128)), _full((1, D_SSM)), _full((D_SSM, D_SSM)), _full((1, D_SSM)),
        _full((1, D_ATTN)), _full((1, D_SSM)), _full((D_MODEL, D_MODEL)),
        _full((1, D_MODEL)), _full((N_EXPERTS, D_MODEL)), _full((N_EXPERTS, 1)), _full((ROUTE_BLOCK, ROUTE_BLOCK)),
    ]
    route_specs, route_shapes = _route_specs(ROWS, lambda c: c)
    out_shape = [
        jax.ShapeDtypeStruct((T_PAD, D_MODEL), F32),
        jax.ShapeDtypeStruct((BATCH, CHUNK, D_KV), F32),
        jax.ShapeDtypeStruct((BATCH, CHUNK, D_KV), F32),
        jax.ShapeDtypeStruct((BATCH, SSM_G * SSM_P), F32),
        jax.ShapeDtypeStruct((BATCH, SSM_G * SSM_P), F32),
    ] + route_shapes
    out_specs = [
        pl.BlockSpec((ROWS, D_MODEL), lambda c: (c, 0)),
        _full((BATCH, CHUNK, D_KV)), _full((BATCH, CHUNK, D_KV)),
        _full((BATCH, SSM_G * SSM_P)), _full((BATCH, SSM_G * SSM_P)),
    ] + route_specs
    kv_scr = pltpu.VMEM((BATCH, CHUNK, D_KV), BF16)
    scratch = [
        pltpu.VMEM((ROWS, D_ATTN + 2 * D_KV), F32), pltpu.VMEM((D_SSM // 128, BATCH * PITCH, 128), F32),
        kv_scr, kv_scr, kv_scr, kv_scr,
        pltpu.VMEM((ROWS, D_ATTN), F32), pltpu.VMEM((D_SSM // 128, BATCH * PITCH, 128), F32),
        pltpu.VMEM((SUB_ROWS, D_SSM), F32), pltpu.VMEM((SUB_ROWS, 2 * SBLK), F32),
        pltpu.VMEM((SUB_ROWS, 2 * SBLK), F32), pltpu.VMEM((SUB_ROWS, D_SSM), F32),
    ]
    return pl.pallas_call(
        _mixer_prompt_kernel, grid=(N_CHUNKS,), in_specs=in_specs, out_specs=out_specs, out_shape=out_shape,
        scratch_shapes=scratch, name="mixer_prompt",
        compiler_params=pltpu.CompilerParams(dimension_semantics=("arbitrary",), vmem_limit_bytes=VMEM_LIMIT),
    )(sinks, x_prompt, gmix, win, bias, bblk, lamr, lami, cblk, dskip, wglu, bglu, ga, gs, wout,
      gffn, wrt, br, _tri(ROUTE_BLOCK))


SGRP = 16
SPITCH = DEC_BATCH + 8
N_SGRP = DEC_BATCH // SGRP


def _mixer_sample_kernel(x_ref, gmix_ref, win_ref, ck_ref, cv_ref, sbias_ref, sinkc_ref, bblk_ref, lamr_ref,
                         lami_ref, cblk_ref, dskip_ref, wglu_ref, bglu_ref, ga_ref, gs_ref, wout_ref,
                         x0r_ref, x0i_ref, gffn_ref, wrt_ref, br_ref, tri_ref, cnt_in_ref,
                         hbuf_ref, hnbuf_ref, idxbuf_ref, gwbuf_ref, rankbuf_ref,
                         hmid_ref, knew_ref, vnew_ref, sre_ref, sim_ref, hn_ref, idx_ref, gw_ref, rank_ref, cnt_ref,
                         proj_s, qall_s, oall_s, kn8_s, vn8_s):
    del hbuf_ref, hnbuf_ref, idxbuf_ref, gwbuf_ref, rankbuf_ref
    g = pl.program_id(0)
    lo = lax.broadcasted_iota(jnp.int32, (DEC_BATCH, 128), 1) < HEAD_DIM

    @pl.when(g == 0)
    def _():
        proj = _bdot(_rms(x_ref[...], gmix_ref[...]), win_ref[...])
        proj_s[...] = proj
        for h in range(N_HEADS):
            jq, half, kv = h // 2, h % 2, h // 4
            q2 = proj[:, 128 * jq:128 * (jq + 1)] * (HEAD_DIM ** -0.5)
            if half != kv:
                q2 = pltpu.roll(q2, HEAD_DIM, 1)
            hr = slice(h * SPITCH, h * SPITCH + DEC_BATCH)
            qall_s[hr, :] = jnp.where(lo if kv == 0 else jnp.logical_not(lo), q2, 0.0)
            kn8_s[hr, :] = proj[:, D_ATTN:D_ATTN + D_KV]
            vn8_s[hr, :] = proj[:, D_ATTN + D_KV:D_ATTN + 2 * D_KV]

    def head_rows(ref):
        return jnp.stack([ref[pl.ds(g * SGRP + ii, N_HEADS, stride=SPITCH), :] for ii in range(SGRP)], axis=0)

    qe = head_rows(qall_s)
    knew = head_rows(kn8_s)
    vnew = head_rows(vn8_s)
    s = jnp.einsum('bhd,bkd->bhk', qe.astype(BF16), ck_ref[...].astype(BF16),
                   preferred_element_type=F32) + sbias_ref[...][None]
    s_new = jnp.sum(qe * knew, axis=-1, keepdims=True)
    sink = sinkc_ref[...][None]
    m = jnp.maximum(jnp.maximum(jnp.max(s, axis=-1, keepdims=True), s_new), sink)
    p = jnp.exp(s - m)
    p_new = jnp.exp(s_new - m)
    den = jnp.sum(p, axis=-1, keepdims=True) + p_new + jnp.exp(sink - m)
    o = (jnp.einsum('bhk,bkd->bhd', p.astype(BF16), cv_ref[...].astype(BF16),
                    preferred_element_type=F32) + p_new * vnew) / den
    for ii in range(SGRP):
        oall_s[pl.ds(g * SGRP + ii, N_HEADS, stride=SPITCH), :] = o[ii]

    @pl.when(g == N_SGRP - 1)
    def _():
        proj = proj_s[...]
        blocks = []
        for jq in range(N_HEADS // 2):
            oa = oall_s[(2 * jq) * SPITCH:(2 * jq) * SPITCH + DEC_BATCH, :]
            ob = oall_s[(2 * jq + 1) * SPITCH:(2 * jq + 1) * SPITCH + DEC_BATCH, :]
            if jq // 2 == 0:
                blocks.append(jnp.where(lo, oa, pltpu.roll(ob, HEAD_DIM, 1)))
            else:
                blocks.append(jnp.where(lo, pltpu.roll(oa, HEAD_DIM, 1), ob))
        attn = jnp.concatenate(blocks, axis=1)
        u = proj[:, D_ATTN + 2 * D_KV:]
        ys = []
        for j in range(N_SBLK):
            bu = _bdot(u[:, 128 * j:128 * (j + 1)], bblk_ref[j])
            lr = lamr_ref[j]
            li = lami_ref[j]
            cols = slice(SBLK * j, SBLK * (j + 1))
            sr = x0r_ref[:, cols]
            si = x0i_ref[:, cols]
            nr = lr * sr - li * si + bu[:, 0:SBLK]
            ni = lr * si + li * sr + bu[:, SBLK:2 * SBLK]
            sre_ref[:, cols] = nr
            sim_ref[:, cols] = ni
            ys.append(_bdot(jnp.concatenate([nr, ni], axis=1), cblk_ref[j]))
        sn = _ssm_post(jnp.concatenate(ys, axis=1), u, dskip_ref[...], wglu_ref[...], bglu_ref[...], gs_ref[...])
        an = _rms(attn, ga_ref[...])
        h = x_ref[...] + _bdot(an, wout_ref[0:D_ATTN, :]) + _bdot(sn, wout_ref[D_ATTN:2 * D_ATTN, :])
        h = jnp.concatenate([h, jnp.zeros((DEC_BATCH, D_MODEL), F32)], axis=0)
        hmid_ref[...] = h
        cnt_ref[...] = cnt_in_ref[...]
        _route(h, gffn_ref, wrt_ref, br_ref, tri_ref, cnt_ref, hn_ref, idx_ref, gw_ref, rank_ref)
        knew_ref[...] = proj[:, D_ATTN:D_ATTN + D_KV]
        vnew_ref[...] = proj[:, D_ATTN + D_KV:D_ATTN + 2 * D_KV]


def _mixer_sample(x_s, gmix, win, ck, cv, sbias, sinkc, bblk, lamr, lami, cblk, dskip, wglu, bglu, ga, gs, wout,
                  x0r, x0i, gffn, wrt, br, cnt_in, hbuf, hnbuf, idxbuf, gwbuf, rankbuf):
    nst = SSM_G * SSM_P
    in_specs = [
        _full((DEC_BATCH, D_MODEL)), _full((1, D_MODEL)), _full((D_MODEL, D_IN)),
        pl.BlockSpec((SGRP, WINDOW, D_KV), lambda g: (g, 0, 0)),
        pl.BlockSpec((SGRP, WINDOW, D_KV), lambda g: (g, 0, 0)),
        _full((N_HEADS, WINDOW)), _full((N_HEADS, 1)),
        _full((N_SBLK, 128, 2 * SBLK)), _full((N_SBLK, 1, SBLK)), _full((N_SBLK, 1, SBLK)),
        _full((N_SBLK, 2 * SBLK, 128)), _full((1, D_SSM)), _full((D_SSM, D_SSM)), _full((1, D_SSM)),
        _full((1, D_ATTN)), _full((1, D_SSM)), _full((D_MODEL, D_MODEL)),
        _full((DEC_BATCH, nst)), _full((DEC_BATCH, nst)),
        _full((1, D_MODEL)), _full((N_EXPERTS, D_MODEL)), _full((N_EXPERTS, 1)),
        _full((2 * DEC_BATCH, 2 * DEC_BATCH)), _full((N_EXPERTS, 1)),
    ] + [pl.BlockSpec(memory_space=pl.ANY)] * 5
    tail_block = SAMPLE_ROW0 // (2 * DEC_BATCH)
    route_specs, route_shapes = _route_specs(2 * DEC_BATCH, lambda g: tail_block)
    out_shape = [
        jax.ShapeDtypeStruct((T_PAD, D_MODEL), F32),
        jax.ShapeDtypeStruct((DEC_BATCH, D_KV), F32), jax.ShapeDtypeStruct((DEC_BATCH, D_KV), F32),
        jax.ShapeDtypeStruct((DEC_BATCH, nst), F32), jax.ShapeDtypeStruct((DEC_BATCH, nst), F32),
    ] + route_shapes
    out_specs = [
        pl.BlockSpec((2 * DEC_BATCH, D_MODEL), lambda g: (tail_block, 0)),
        _full((DEC_BATCH, D_KV)), _full((DEC_BATCH, D_KV)),
        _full((DEC_BATCH, nst)), _full((DEC_BATCH, nst)),
    ] + route_specs
    head_rows = pltpu.VMEM((N_HEADS * SPITCH, 128), F32)
    scratch = [pltpu.VMEM((DEC_BATCH, D_IN), F32), head_rows, head_rows, head_rows, head_rows]
    return pl.pallas_call(
        _mixer_sample_kernel, grid=(N_SGRP,), in_specs=in_specs, out_specs=out_specs, out_shape=out_shape,
        scratch_shapes=scratch, input_output_aliases={24: 0, 25: 5, 26: 6, 27: 7, 28: 8}, name="mixer_sample",
        compiler_params=pltpu.CompilerParams(dimension_semantics=("arbitrary",), vmem_limit_bytes=VMEM_LIMIT),
    )(x_s, gmix, win, ck, cv, sbias, sinkc, bblk, lamr, lami, cblk, dskip, wglu, bglu, ga, gs, wout, x0r, x0i,
      gffn, wrt, br, _tri(2 * DEC_BATCH), cnt_in, hbuf, hnbuf, idxbuf, gwbuf, rankbuf)


def _route(h, g_ref, wrt_ref, br_ref, tri_ref, cnt_ref, hn_ref, idx_ref, gw_ref, rank_ref):
    hn = _rms(h, g_ref[...])
    hn_ref[...] = _pack_pairs(hn)
    hn_hi = hn.astype(BF16)
    hn_lo = (hn - hn_hi.astype(F32)).astype(BF16)
    w = wrt_ref[...]
    w_hi = w.astype(BF16)
    w_lo = (w - w_hi.astype(F32)).astype(BF16)
    nt = (((1,), (1,)), ((), ()))
    logits = (lax.dot_general(w_hi, hn_hi, nt, preferred_element_type=F32)
              + lax.dot_general(w_lo, hn_hi, nt, preferred_element_type=F32)
              + lax.dot_general(w_hi, hn_lo, nt, preferred_element_type=F32)) + br_ref[...]
    eidx = lax.broadcasted_iota(jnp.int32, logits.shape, 0)
    vals, onehots = [], []
    l = logits
    for k in range(TOP_K):
        m = jnp.max(l, axis=0, keepdims=True)
        ik = jnp.min(jnp.where(l == m, eidx, N_EXPERTS), axis=0, keepdims=True)
        oh = eidx == ik
        idx_ref[k:k + 1, :] = ik
        vals.append(m)
        onehots.append(oh)
        l = jnp.where(oh, -jnp.inf, l)
    exps = [jnp.exp(v - vals[0]) for v in vals]
    den = exps[0] + exps[1] + exps[2] + exps[3]
    for k in range(TOP_K):
        gw_ref[k:k + 1, :] = exps[k] / den
    member = jnp.zeros(logits.shape, F32)
    for oh in onehots:
        member = member + jnp.where(oh, 1.0, 0.0)
    wblk = tri_ref.shape[0]
    base = cnt_ref[...].astype(F32)
    befores = []
    for cb in range(h.shape[0] // wblk):
        mblk = member[:, cb * wblk:(cb + 1) * wblk]
        befores.append(jnp.dot(mblk.astype(BF16), tri_ref[...], preferred_element_type=F32) + base)
        base = base + jnp.sum(mblk, axis=1, keepdims=True)
    before = jnp.concatenate(befores, axis=1)
    for k in range(TOP_K):
        rank_ref[k:k + 1, :] = jnp.sum(jnp.where(onehots[k], before, 0.0), axis=0, keepdims=True).astype(jnp.int32)
    cnt_ref[...] = base.astype(jnp.int32)


def _route_specs(rows, block):
    specs = [pl.BlockSpec((rows, D_PACK), lambda i: (block(i), 0)), pl.BlockSpec((TOP_K, rows), lambda i: (0, block(i))),
             pl.BlockSpec((TOP_K, rows), lambda i: (0, block(i))), pl.BlockSpec((TOP_K, rows), lambda i: (0, block(i))),
             _full((N_EXPERTS, 1))]
    shapes = [jax.ShapeDtypeStruct((T_PAD, D_PACK), jnp.int32), jax.ShapeDtypeStruct((TOP_K, T_PAD), jnp.int32),
              jax.ShapeDtypeStruct((TOP_K, T_PAD), F32), jax.ShapeDtypeStruct((TOP_K, T_PAD), jnp.int32),
              jax.ShapeDtypeStruct((N_EXPERTS, 1), jnp.int32)]
    return specs, shapes


def _tri(n):
    return jnp.asarray(np.triu(np.ones((n, n), np.float32), 1), BF16)


def _place_kernel(off_ref, idx_ref, rank_ref, pos_ref):
    idx = idx_ref[...]
    pos = rank_ref[...]
    for e in range(N_EXPERTS):
        pos = pos + jnp.where(idx == e, off_ref[e], 0)
    pos_ref[...] = pos


def _place(offsets, idx, rank):
    return pl.pallas_call(
        _place_kernel,
        in_specs=[pl.BlockSpec(memory_space=pltpu.SMEM), pl.BlockSpec(memory_space=pltpu.VMEM),
                  pl.BlockSpec(memory_space=pltpu.VMEM)],
        out_specs=pl.BlockSpec(memory_space=pltpu.VMEM),
        out_shape=jax.ShapeDtypeStruct((TOP_K, T_PAD), jnp.int32), name="place",
    )(offsets, idx, rank)


def _sc_mesh():
    return plsc.VectorSubcoreMesh(core_axis_name="core", subcore_axis_name="subcore")


def _sc_dispatch(rows, pos):
    n, d = rows.shape
    nblk = n // SC_ROWS
    pos_w = pos.reshape(TOP_K, nblk, SC_ROWS).transpose(1, 0, 2)

    @functools.partial(pl.kernel, out_type=jax.ShapeDtypeStruct((P_ROWS, d), rows.dtype), mesh=_sc_mesh(),
                       scratch_types=[], name="dispatch")
    def run(x_hbm, i_hbm, o_hbm):
        def body(x_vmem, i_vmem):
            for k in range(TOP_K):
                pltpu.sync_copy(x_vmem, o_hbm.at[i_vmem.at[0, k]])

        pltpu.emit_pipeline(
            body, grid=(nblk,),
            in_specs=[pl.BlockSpec((SC_ROWS, d), lambda i: (i, 0)),
                      pl.BlockSpec((1, TOP_K, SC_ROWS), lambda i: (i, 0, 0))],
            out_specs=[], core_axis_name=("core", "subcore"), dimension_semantics=(pltpu.PARALLEL,),
        )(x_hbm, i_hbm)

    return run(rows, pos_w)


def _sc_combine(rows, flat_pos):
    _, d = rows.shape
    n = flat_pos.shape[0]
    w = SC_ROWS_COMBINE
    nblk = n // w
    pos_w = flat_pos.reshape(nblk, 1, w)

    @functools.partial(pl.kernel, out_type=jax.ShapeDtypeStruct((n, d), rows.dtype), mesh=_sc_mesh(),
                       scratch_types=[], name="combine")
    def run(x_hbm, i_hbm, o_hbm):
        def body(i_vmem, o_vmem):
            pltpu.sync_copy(x_hbm.at[i_vmem.at[0, 0]], o_vmem)

        pltpu.emit_pipeline(
            body, grid=(nblk,),
            in_specs=[pl.BlockSpec((1, 1, w), lambda i: (i, 0, 0))],
            out_specs=[pl.BlockSpec((w, d), lambda i: (i, 0))],
            core_axis_name=("core", "subcore"), dimension_semantics=(pltpu.PARALLEL,),
        )(i_hbm, o_hbm)

    return run(rows, pos_w)


def _experts_kernel(ts_ref, trow_ref, tsz_ref, xs_hbm, w1_hbm, b1_ref, w2_hbm, b2_ref, ys_hbm,
                    w1_st, w2_st, w1_s, w2_s, xbuf, ybuf, wsem, xsem, ysem):
    e = pl.program_id(0)
    n_valid = ts_ref[N_EXPERTS]

    def w_copies(ex, slot):
        return (pltpu.make_async_copy(w1_hbm.at[ex], w1_st.at[slot], wsem.at[0, slot]),
                pltpu.make_async_copy(w2_hbm.at[ex], w2_st.at[slot], wsem.at[1, slot]))

    def x_copy(t, slot, rows):
        src = xs_hbm.at[pl.ds(pl.multiple_of(trow_ref[t], TM_UNIT), rows)]
        return pltpu.make_async_copy(src, xbuf.at[slot, pl.ds(0, rows)], xsem.at[slot])

    def y_copy(t, slot, rows):
        dst = ys_hbm.at[pl.ds(pl.multiple_of(trow_ref[t], TM_UNIT), rows)]
        return pltpu.make_async_copy(ybuf.at[slot, pl.ds(0, rows)], dst, ysem.at[slot])

    def by_size(t, fn):
        for rows in TILE_SIZES:
            @pl.when(tsz_ref[t] == rows)
            def _():
                fn(rows)

    @pl.when(e == 0)
    def _():
        for c in w_copies(0, 0):
            c.start()
        by_size(0, lambda rows: x_copy(0, 0, rows).start())

    @pl.when(e + 1 < N_EXPERTS)
    def _():
        for c in w_copies(e + 1, (e + 1) % 2):
            c.start()

    wslot = e % 2
    for c in w_copies(e, wslot):
        c.wait()
    for r in range(4):
        rs = slice(256 * r, 256 * (r + 1))
        w1_s[rs, :] = w1_st[wslot, rs, :].astype(BF16)
        w2_s[rs, :] = w2_st[wslot, rs, :].astype(BF16)

    def tile(t, carry):
        slot = t % 2

        @pl.when(t + 1 < n_valid)
        def _():
            by_size(t + 1, lambda rows: x_copy(t + 1, 1 - slot, rows).start())

        by_size(t, lambda rows: x_copy(t, slot, rows).wait())

        @pl.when(t >= 2)
        def _():
            by_size(t - 2, lambda rows: y_copy(t - 2, slot, rows).wait())

        def compute(rows):
            hdn = _bdot(_unpack_pairs(xbuf[slot, 0:rows]), w1_s[...]) + b1_ref[0]
            gt = jnp.minimum(hdn[:, :D_FF], SWIGLU_LIMIT)
            up = jnp.clip(hdn[:, D_FF:], -SWIGLU_LIMIT, SWIGLU_LIMIT)
            act = (up + 1.0) * gt * _sigmoid(SWIGLU_ALPHA * gt)
            ybuf[slot, 0:rows] = _pack_pairs(_bdot(act, w2_s[...]) + b2_ref[0])
            y_copy(t, slot, rows).start()

        by_size(t, compute)
        return carry

    lax.fori_loop(ts_ref[e], ts_ref[e + 1], tile, 0)

    @pl.when(e == N_EXPERTS - 1)
    def _():
        @pl.when(n_valid >= 2)
        def _():
            by_size(n_valid - 2, lambda rows: y_copy(n_valid - 2, n_valid % 2, rows).wait())

        by_size(n_valid - 1, lambda rows: y_copy(n_valid - 1, (n_valid - 1) % 2, rows).wait())


def _tile_plan(counts):
    units = (counts + (TM_UNIT - 1)) // TM_UNIT
    unit_end = jnp.cumsum(units)
    offsets = ((unit_end - units) * TM_UNIT).astype(jnp.int32)
    per_full = TM // TM_UNIT
    n_full = units // per_full
    rest = units % per_full
    has_mid = (rest >= 2).astype(jnp.int32)
    tiles_per = n_full + has_mid + rest % 2
    tile_end = jnp.cumsum(tiles_per)
    tile_start = jnp.concatenate([jnp.zeros((1,), jnp.int32), tile_end.astype(jnp.int32)])
    t = jnp.arange(MAX_TILES, dtype=jnp.int32)[:, None]
    done = tile_end[None, :] <= t
    mine = jnp.sum(done.astype(jnp.int32), axis=1, keepdims=True) == jnp.arange(N_EXPERTS, dtype=jnp.int32)[None, :]
    pick = lambda v: jnp.sum(jnp.where(mine, v[None, :], 0), axis=1)
    j = t[:, 0] - jnp.max(jnp.where(done, tile_end[None, :], 0), axis=1)
    nf, mid = pick(n_full), pick(has_mid)
    rows = jnp.where(j < nf, TM, jnp.where((j == nf) & (mid == 1), TILE_SIZES[1], TILE_SIZES[2]))
    row0 = pick(offsets) + jnp.minimum(j, nf) * TM + jnp.where(j > nf, TILE_SIZES[1], 0)
    valid = t[:, 0] < tile_end[-1]
    tile_row = jnp.where(valid, row0, 0).astype(jnp.int32)
    tile_rows = jnp.where(valid, rows, 0).astype(jnp.int32)
    return offsets, tile_start, tile_row, tile_rows


def _experts(tile_start, tile_row, tile_rows, xs, w_up, b_up, w_down, b_down):
    wsel = lambda e, *_: (e, 0, 0)
    hbm = pl.BlockSpec(memory_space=pl.ANY)
    grid_spec = pltpu.PrefetchScalarGridSpec(
        num_scalar_prefetch=3, grid=(N_EXPERTS,),
        in_specs=[hbm, hbm, pl.BlockSpec((1, 1, 2 * D_FF), wsel), hbm, pl.BlockSpec((1, 1, D_MODEL), wsel)],
        out_specs=hbm,
        scratch_shapes=[pltpu.VMEM((2, D_MODEL, 2 * D_FF), F32), pltpu.VMEM((2, D_FF, D_MODEL), F32),
                        pltpu.VMEM((D_MODEL, 2 * D_FF), BF16), pltpu.VMEM((D_FF, D_MODEL), BF16),
                        pltpu.VMEM((2, TM, D_PACK), jnp.int32), pltpu.VMEM((2, TM, D_PACK), jnp.int32),
                        pltpu.SemaphoreType.DMA((2, 2)), pltpu.SemaphoreType.DMA((2,)),
                        pltpu.SemaphoreType.DMA((2,))],
    )
    return pl.pallas_call(
        _experts_kernel, grid_spec=grid_spec, out_shape=jax.ShapeDtypeStruct((P_ROWS, D_PACK), jnp.int32),
        name="experts",
        compiler_params=pltpu.CompilerParams(dimension_semantics=("arbitrary",), vmem_limit_bytes=VMEM_LIMIT),
    )(tile_start, tile_row, tile_rows, xs, w_up, b_up.reshape(N_EXPERTS, 1, 2 * D_FF), w_down,
      b_down.reshape(N_EXPERTS, 1, D_MODEL))


def _final_kernel(h_ref, yg_ref, gw_ref, p_ref, gple_ref, wg_ref, wp_ref, gfin_ref, *rest):
    out_ref = rest[-1]
    rows = h_ref.shape[0]
    gw = gw_ref[...]
    h = h_ref[...]
    for k in range(TOP_K):
        h = h + gw[:, k:k + 1] * _unpack_pairs(yg_ref[k])
    gate = _sigmoid(_bdot(_rms(h, gple_ref[...]), wg_ref[...]))
    h = h + gate * _bdot(p_ref[...].reshape(rows, PLE_DIM), wp_ref[...])
    out_ref[...] = _rms(h, gfin_ref[...]).reshape(out_ref.shape)


def _final_prompt(part, hmid, yg, gwt, p_prompt, gple, wg, wp, gfin, y_prev):
    nb = 4
    rows = nb * CHUNK
    nbh = BATCH // nb
    c0 = part * PART_CHUNKS
    rb = lambda c, b: ((c0 + c) * nbh + b, 0)
    in_specs = [pl.BlockSpec((rows, D_MODEL), rb),
                pl.BlockSpec((TOP_K, rows, D_PACK), lambda c, b: (0, c * nbh + b, 0)),
                pl.BlockSpec((rows, TOP_K), rb),
                pl.BlockSpec((nb, CHUNK, PLE_DIM), lambda c, b: (b, c0 + c, 0)),
                _full((1, D_MODEL)), _full((D_MODEL, D_MODEL)), _full((PLE_DIM, D_MODEL)), _full((1, D_MODEL))]
    args = [hmid, yg, gwt, p_prompt, gple, wg, wp, gfin]
    aliases = {}
    if y_prev is not None:
        in_specs.append(pl.BlockSpec(memory_space=pl.ANY))
        args.append(y_prev)
        aliases = {len(args) - 1: 0}
    return pl.pallas_call(
        _final_kernel, grid=(PART_CHUNKS, nbh), in_specs=in_specs,
        out_specs=pl.BlockSpec((nb, CHUNK, D_MODEL), lambda c, b: (b, c0 + c, 0)),
        out_shape=jax.ShapeDtypeStruct((BATCH, SEQ, D_MODEL), F32), name="final_prompt",
        input_output_aliases=aliases,
        compiler_params=pltpu.CompilerParams(dimension_semantics=("arbitrary", "arbitrary"),
                                             vmem_limit_bytes=VMEM_LIMIT),
    )(*args)


def _final_sample(hmid, yg, gwt, p_sample, gple, wg, wp, gfin):
    blk = SAMPLE_ROW0 // DEC_BATCH
    return pl.pallas_call(
        _final_kernel, grid=(1,),
        in_specs=[pl.BlockSpec((DEC_BATCH, D_MODEL), lambda i: (blk, 0)),
                  pl.BlockSpec((TOP_K, DEC_BATCH, D_PACK), lambda i: (0, PART_CHUNKS * ROWS // DEC_BATCH, 0)),
                  pl.BlockSpec((DEC_BATCH, TOP_K), lambda i: (blk, 0)),
                  _full((DEC_BATCH, PLE_DIM)),
                  _full((1, D_MODEL)), _full((D_MODEL, D_MODEL)), _full((PLE_DIM, D_MODEL)), _full((1, D_MODEL))],
        out_specs=_full((DEC_BATCH, D_MODEL)),
        out_shape=jax.ShapeDtypeStruct((DEC_BATCH, D_MODEL), F32), name="final_sample",
        compiler_params=pltpu.CompilerParams(dimension_semantics=("arbitrary",), vmem_limit_bytes=VMEM_LIMIT),
    )(hmid, yg, gwt, p_sample, gple, wg, wp, gfin)


def _alibi_tables():
    slopes = 2.0 ** (-8.0 * (np.arange(N_HEADS, dtype=np.float64) + 1.0) / N_HEADS)
    i = np.arange(CHUNK)[:, None]
    j = np.arange(2 * CHUNK)[None, :]
    dist = i + CHUNK - j
    valid = (dist >= 0) & (dist <= WINDOW)
    tabs = []
    for has_prev in (False, True):
        ok = valid & ((j >= CHUNK) | has_prev)
        tabs.append(np.where(ok[None], -slopes[:, None, None] * dist[None], NEG))
    prompt = np.stack([np.stack([np.concatenate([t[h] for h in grp], axis=0) for grp in (NAT_HEADS, ROT_HEADS)])
                       for t in tabs]).astype(np.float32)
    wb = min(WINDOW, PAST_LEN)
    sample = (-slopes[:, None] * (wb - np.arange(wb))[None, :]).astype(np.float32)
    return prompt, sample


def kernel(x_prompt, x_sample, cache_k_win, cache_v_win, state_ssm_re, state_ssm_im, p_prompt, p_sample, norm_mix, w_in, sinks, ssm_lam_re, ssm_lam_im, ssm_log_step, ssm_b_re, ssm_b_im, ssm_c_re, ssm_c_im, ssm_d, w_glu, b_glu, norm_attn_out, norm_ssm_out, w_out, norm_ffn, w_router, b_router, w_up, b_up, w_down, b_down, norm_ple, w_ple_gate, w_ple_proj, norm_final):
    nst = SSM_G * SSM_P
    bias_np, sbias_np = _alibi_tables()
    bias = jnp.asarray(bias_np)
    sbias = jnp.asarray(sbias_np)

    lbr, lbi, bbr, bbi = _prep(ssm_lam_re[0], ssm_lam_im[0], ssm_log_step[0], ssm_b_re[0], ssm_b_im[0])
    bblk, lamr, lami, cblk = _s5_blocks(lbr, lbi, bbr, bbi, ssm_c_re[0], ssm_c_im[0])

    gmix = norm_mix[0].reshape(1, D_MODEL)
    win = w_in[0].astype(BF16)
    dskip = ssm_d[0].reshape(1, D_SSM)
    wglu = w_glu[0].astype(BF16)
    bglu = b_glu[0].reshape(1, D_SSM)
    ga = norm_attn_out[0].reshape(1, D_ATTN)
    gs = norm_ssm_out[0].reshape(1, D_SSM)
    wout = w_out[0].astype(BF16)
    sink = sinks[0]

    gffn = norm_ffn[0].reshape(1, D_MODEL)
    wrt = w_router[0].T
    br = b_router[0].reshape(N_EXPERTS, 1)
    hbuf, k_last, v_last, re_p, im_p, hnbuf, idxbuf, gwbuf, rankbuf, cnt_p = _mixer_prompt(
        x_prompt, sink, gmix, win, bias, bblk, lamr, lami, cblk, dskip, wglu, bglu, ga, gs, wout, gffn, wrt, br)

    ck = cache_k_win[0].reshape(DEC_BATCH, WINDOW, D_KV)
    cv = cache_v_win[0].reshape(DEC_BATCH, WINDOW, D_KV)
    hmid, k_new, v_new, re_s, im_s, hn, idx, gw, rank, counts = _mixer_sample(
        x_sample.reshape(DEC_BATCH, D_MODEL), gmix, win, ck, cv, sbias, sink.reshape(N_HEADS, 1), bblk, lamr, lami,
        cblk, dskip, wglu, bglu, ga, gs, wout, state_ssm_re[0].reshape(DEC_BATCH, nst),
        state_ssm_im[0].reshape(DEC_BATCH, nst), gffn, wrt, br, cnt_p, hbuf, hnbuf, idxbuf, gwbuf, rankbuf)

    offsets, tile_start, tile_row, tile_rows = _tile_plan(counts[:, 0])

    pos = _place(offsets, idx, rank)
    xs = _sc_dispatch(hn, pos)
    ys = _experts(tile_start, tile_row, tile_rows, xs, w_up[0], b_up[0], w_down[0], b_down[0])
    ygs = []
    for part in range(N_PARTS):
        r0 = part * PART_CHUNKS * ROWS
        r1 = T_PAD if part == N_PARTS - 1 else r0 + PART_CHUNKS * ROWS
        ygs.append(_sc_combine(ys, pos[:, r0:r1].reshape(TOP_K * (r1 - r0))).reshape(TOP_K, r1 - r0, D_PACK))

    gwt = gw.T
    gple = norm_ple[0].reshape(1, D_MODEL)
    wg = w_ple_gate[0].astype(BF16)
    wp = w_ple_proj[0].astype(BF16)
    gfin = norm_final.reshape(1, D_MODEL)
    y_prompt = None
    for part in range(N_PARTS):
        y_prompt = _final_prompt(part, hmid, ygs[part], gwt, p_prompt[0], gple, wg, wp, gfin, y_prompt)
    y_sample = _final_sample(hmid, ygs[-1], gwt, p_sample[0].reshape(DEC_BATCH, PLE_DIM), gple, wg, wp, gfin)

    k_win_s = jnp.concatenate([ck[:, 1:], k_new[:, None, :]], axis=1)
    v_win_s = jnp.concatenate([cv[:, 1:], v_new[:, None, :]], axis=1)
    kv5 = (1, BATCH, CHUNK, N_KV, HEAD_DIM)
    skv5 = (1, DEC_BATCH, WINDOW, N_KV, HEAD_DIM)
    return (y_prompt, y_sample.reshape(DEC_BATCH, 1, D_MODEL),
            k_last.reshape(kv5), v_last.reshape(kv5),
            re_p.reshape(1, BATCH, SSM_G, SSM_P), im_p.reshape(1, BATCH, SSM_G, SSM_P),
            k_win_s.reshape(skv5), v_win_s.reshape(skv5),
            re_s.reshape(1, DEC_BATCH, SSM_G, SSM_P), im_s.reshape(1, DEC_BATCH, SSM_G, SSM_P))
```

```python
import functools

import numpy as np
import jax
import jax.numpy as jnp
from jax import lax
from jax.experimental import pallas as pl
from jax.experimental.pallas import tpu as pltpu
from jax.experimental.pallas import tpu_sc as plsc
from jax._src.pallas import mpmd

F32 = jnp.float32
BF16 = jnp.bfloat16

D_MODEL = 1024
BATCH = 8
SEQ = 2048
DEC_BATCH = 128
PAST_LEN = 16384
HEAD_DIM = 64
D_ATTN = 512
N_HEADS = 8
N_KV = 2
D_KV = N_KV * HEAD_DIM
WINDOW = 128
D_SSM = 512
SSM_H = 16
SSM_G = 32
SSM_P = 64
D_IN = D_ATTN + 2 * D_KV + D_SSM
N_EXPERTS = 32
TOP_K = 4
D_FF = 1024
SWIGLU_LIMIT = 7.0
SWIGLU_ALPHA = 1.702
PLE_DIM = 256
EPS = 1e-5
NEG = -1e30

CHUNK = 128
N_CHUNKS = SEQ // CHUNK
ROWS = BATCH * CHUNK
PITCH = CHUNK + 8
NAT_HEADS = (0, 2, 5, 7)
ROT_HEADS = (1, 3, 4, 6)
SUB_T = 32
SUB_ROWS = SUB_T * BATCH
N_SBLK = 4
SBLK = 512
T_REAL = BATCH * SEQ + DEC_BATCH
T_PAD = T_REAL + 128
SAMPLE_ROW0 = BATCH * SEQ
ROUTE_BLOCK = 512
TILE_SIZES = (1024, 512, 256)
TM = TILE_SIZES[0]
TM_UNIT = TILE_SIZES[-1]
P_ROWS = (T_PAD * TOP_K + N_EXPERTS * (TM_UNIT - 1)) // TM_UNIT * TM_UNIT
MAX_TILES = P_ROWS // TM + 2 * N_EXPERTS + 1
CHUNK_UNIT = 1024
CHUNK_UNITS = (60, 16)
CHUNK_BOUNDS = tuple(CHUNK_UNIT * sum(CHUNK_UNITS[:c]) for c in range(len(CHUNK_UNITS) + 1))
P_ALLOC = CHUNK_BOUNDS[-1]
DEST_SLICE = P_ALLOC // 32
K_STRIDE = 16896
TRASH_ROW = TOP_K * K_STRIDE
N_TRASH = 2048
YG_ROWS = TRASH_ROW + N_TRASH
SC_ROWS = 40
SC_ROWS_COMBINE = 32
D_PACK = D_MODEL // 2
VMEM_LIMIT = 56 * 1024 * 1024


def _rms(x, g):
    return x * lax.rsqrt(jnp.mean(x * x, axis=-1, keepdims=True) + EPS) * g


def _sigmoid(x):
    return 1.0 / (1.0 + jnp.exp(-x))


def _gelu_tanh(x):
    c = np.float32(np.sqrt(2.0 / np.pi))
    return 0.5 * x * (1.0 + jnp.tanh(c * (x + 0.044715 * (x * x * x))))


def _bdot(a, b):
    return jnp.dot(a.astype(BF16), b, preferred_element_type=F32)


def _pack_pairs(x):
    n = x.shape[1] // 2
    lo = lax.bitcast_convert_type(x[:, :n].astype(BF16).astype(F32), jnp.int32)
    hi = lax.bitcast_convert_type(x[:, n:].astype(BF16).astype(F32), jnp.int32)
    return lax.shift_right_logical(lo, 16) | hi


def _unpack_pairs(w):
    lo = lax.bitcast_convert_type(lax.shift_left(w, 16), F32)
    hi = lax.bitcast_convert_type(w & jnp.int32(-65536), F32)
    return jnp.concatenate([lo, hi], axis=1)


def _full(shape):
    n = len(shape)
    return pl.BlockSpec(shape, lambda *_: (0,) * n)


def _prep_kernel(lr_ref, li_ref, ls_ref, br_ref, bi_ref, lbr_ref, lbi_ref, bbr_ref, bbi_ref):
    lr = lr_ref[...]
    li = li_ref[...]
    step = jnp.exp(ls_ref[...])
    zr = lr * step
    zi = li * step
    mag = jnp.exp(zr)
    lbr = mag * jnp.cos(zi)
    lbi = mag * jnp.sin(zi)
    lbr_ref[...] = lbr
    lbi_ref[...] = lbi
    nr = lbr - 1.0
    den = lr * lr + li * li
    cr = (nr * lr + lbi * li) / den
    ci = (lbi * lr - nr * li) / den
    br = br_ref[...]
    bi = bi_ref[...]
    bbr_ref[...] = cr * br - ci * bi
    bbi_ref[...] = cr * bi + ci * br


def _prep(lam_re, lam_im, log_step, b_re, b_im):
    g, p, h = SSM_G, SSM_P, SSM_H
    out = pl.pallas_call(
        _prep_kernel,
        out_shape=[jax.ShapeDtypeStruct((g, 1, p), F32), jax.ShapeDtypeStruct((g, 1, p), F32),
                   jax.ShapeDtypeStruct((g, h, p), F32), jax.ShapeDtypeStruct((g, h, p), F32)],
        name="s5_prep",
    )(lam_re.reshape(g, 1, p), lam_im.reshape(g, 1, p), log_step.reshape(g, 1, 1),
      jnp.transpose(b_re, (0, 2, 1)), jnp.transpose(b_im, (0, 2, 1)))
    return out


def _s5_blocks(lbr, lbi, bbr, bbi, c_re, c_im):
    eye = jnp.eye(8, dtype=F32)
    shp = (N_SBLK, 8, SSM_H, SSM_P)

    def in_map(b):
        return (b.reshape(shp)[:, :, :, None, :] * eye[None, :, None, :, None]).reshape(N_SBLK, 128, SBLK)

    def out_map(c):
        ct = jnp.transpose(c.reshape(shp), (0, 1, 3, 2))
        return (ct[:, :, :, None, :] * eye[None, :, None, :, None]).reshape(N_SBLK, SBLK, 128)

    bblk = jnp.concatenate([in_map(bbr), in_map(bbi)], axis=-1).astype(BF16)
    cblk = jnp.concatenate([out_map(c_re), -out_map(c_im)], axis=1).astype(BF16)
    return bblk, lbr.reshape(N_SBLK, 1, SBLK), lbi.reshape(N_SBLK, 1, SBLK), cblk


def _ssm_post(y_lin, u, dskip, wglu, bglu, gs):
    y = _gelu_tanh(y_lin + dskip * u)
    y = y * _sigmoid(_bdot(y, wglu) + bglu)
    return _rms(y, gs)


def _mixer_prompt_kernel(sinks_ref, x_ref, gmix_ref, win_ref, bias_ref, bblk_ref, lamr_ref, lami_ref,
                         cblk_ref, dskip_ref, wglu_ref, bglu_ref, ga_ref, gs_ref, wout_ref,
                         gffn_ref, wrt_ref, br_ref, tri_ref,
                         hmid_ref, klast_ref, vlast_ref, sre_ref, sim_ref, hn_ref, idx_ref, gw_ref, rank_ref, cnt_ref,
                         proj_s, u_s, kprev_s, kprevr_s, vprev_s, vprevr_s, attn_s, ssm_s, utb_s, bu_s, xs_s, ytb_s):
    c = pl.program_id(0)

    @pl.when(c == 0)
    def _():
        zkv = jnp.zeros(kprev_s.shape, BF16)
        kprev_s[...] = zkv
        kprevr_s[...] = zkv
        vprev_s[...] = zkv
        vprevr_s[...] = zkv
        sre_ref[...] = jnp.zeros(sre_ref.shape, F32)
        sim_ref[...] = jnp.zeros(sim_ref.shape, F32)
        cnt_ref[...] = jnp.zeros(cnt_ref.shape, jnp.int32)

    x = x_ref[...].reshape(ROWS, D_MODEL)
    proj = _bdot(_rms(x, gmix_ref[...]), win_ref[...])
    u0 = D_ATTN + 2 * D_KV
    proj_s[...] = proj[:, 0:u0]
    for jb in range(D_SSM // 128):
        for b in range(BATCH):
            u_s[jb, b * PITCH:b * PITCH + CHUNK, :] = proj[b * CHUNK:(b + 1) * CHUNK, u0 + 128 * jb:u0 + 128 * (jb + 1)]

    lo = lax.broadcasted_iota(jnp.int32, (CHUNK, 128), 1) < HEAD_DIM
    hi = jnp.logical_not(lo)
    table = jnp.minimum(c, 1)
    hrow = lax.broadcasted_iota(jnp.int32, (4 * CHUNK, 1), 0) // CHUNK

    def sink_col(heads):
        col = jnp.full((4 * CHUNK, 1), sinks_ref[heads[3]], F32)
        for n in (2, 1, 0):
            col = jnp.where(hrow == n, sinks_ref[heads[n]], col)
        return col

    sink_nat = sink_col(NAT_HEADS)
    sink_rot = sink_col(ROT_HEADS)

    def attend(q, k, v, bias, sink):
        s = lax.dot_general(q, k, (((1,), (1,)), ((), ())), preferred_element_type=F32) + bias
        m = jnp.maximum(jnp.max(s, axis=-1, keepdims=True), sink)
        p = jnp.exp(s - m)
        den = jnp.sum(p, axis=-1, keepdims=True) + jnp.exp(sink - m)
        return jnp.dot(p.astype(BF16), v, preferred_element_type=F32) / den

    def attn_body(b, carry):
        r0 = pl.multiple_of(b * CHUNK, CHUNK)
        rows = pl.ds(r0, CHUNK)
        kb = proj_s[rows, D_ATTN:D_ATTN + D_KV]
        vb = proj_s[rows, D_ATTN + D_KV:D_ATTN + 2 * D_KV]
        kb16 = kb.astype(BF16)
        vb16 = vb.astype(BF16)
        kbr16 = pltpu.roll(kb, HEAD_DIM, 1).astype(BF16)
        vbr16 = pltpu.roll(vb, HEAD_DIM, 1).astype(BF16)
        k_nat = jnp.concatenate([kprev_s[b], kb16], axis=0)
        k_rot = jnp.concatenate([kprevr_s[b], kbr16], axis=0)
        v_nat = jnp.concatenate([vprev_s[b], vb16], axis=0)
        v_rot = jnp.concatenate([vprevr_s[b], vbr16], axis=0)
        q2 = [proj_s[rows, 128 * jq:128 * (jq + 1)] * (HEAD_DIM ** -0.5) for jq in range(N_HEADS // 2)]
        q_nat = jnp.concatenate([jnp.where(lo if h % 2 == 0 else hi, q2[h // 2], 0.0) for h in NAT_HEADS],
                                axis=0).astype(BF16)
        q_rot = jnp.concatenate([jnp.where(lo if h % 2 == 0 else hi, q2[h // 2], 0.0) for h in ROT_HEADS],
                                axis=0).astype(BF16)
        o_nat = attend(q_nat, k_nat, v_nat, bias_ref[table, 0], sink_nat)
        o_rot = attend(q_rot, k_rot, v_rot, bias_ref[table, 1], sink_rot)
        for jq in range(N_HEADS // 2):
            blk = slice(CHUNK * jq, CHUNK * (jq + 1))
            even, odd = (o_nat, o_rot) if jq < 2 else (o_rot, o_nat)
            attn_s[rows, 128 * jq:128 * (jq + 1)] = jnp.where(lo, even[blk], odd[blk])
        kprev_s[b] = kb16
        kprevr_s[b] = kbr16
        vprev_s[b] = vb16
        vprevr_s[b] = vbr16
        return carry

    lax.fori_loop(0, BATCH, attn_body, 0)

    for sc in range(CHUNK // SUB_T):
        t0 = sc * SUB_T
        for i in range(SUB_T):
            for jb in range(D_SSM // 128):
                utb_s[i * BATCH:(i + 1) * BATCH, 128 * jb:128 * (jb + 1)] = (
                    u_s[jb, pl.ds(t0 + i, BATCH, stride=PITCH), :])
        u_tb = utb_s[...]
        for j in range(N_SBLK):
            bu_s[...] = _bdot(u_tb[:, 128 * j:128 * (j + 1)], bblk_ref[j])
            lr = jnp.broadcast_to(lamr_ref[j], (BATCH, SBLK))
            li = jnp.broadcast_to(lami_ref[j], (BATCH, SBLK))
            cols = slice(SBLK * j, SBLK * (j + 1))

            def step(i, carry):
                sr, si = carry
                r = pl.ds(pl.multiple_of(i * BATCH, BATCH), BATCH)
                nr = lr * sr - li * si + bu_s[r, 0:SBLK]
                ni = lr * si + li * sr + bu_s[r, SBLK:2 * SBLK]
                xs_s[r, 0:SBLK] = nr
                xs_s[r, SBLK:2 * SBLK] = ni
                return nr, ni

            sr, si = lax.fori_loop(0, SUB_T, step, (sre_ref[:, cols], sim_ref[:, cols]), unroll=True)
            sre_ref[:, cols] = sr
            sim_ref[:, cols] = si
            ytb_s[:, 128 * j:128 * (j + 1)] = _bdot(xs_s[...], cblk_ref[j])
        yn = _ssm_post(ytb_s[...], u_tb, dskip_ref[...], wglu_ref[...], bglu_ref[...], gs_ref[...])
        for i in range(SUB_T):
            for jb in range(D_SSM // 128):
                ssm_s[jb, pl.ds(t0 + i, BATCH, stride=PITCH), :] = (
                    yn[i * BATCH:(i + 1) * BATCH, 128 * jb:128 * (jb + 1)])

    an = _rms(attn_s[...], ga_ref[...])
    sn = jnp.concatenate(
        [jnp.concatenate([ssm_s[jb, b * PITCH:b * PITCH + CHUNK, :] for b in range(BATCH)], axis=0)
         for jb in range(D_SSM // 128)], axis=1)
    h = x + _bdot(an, wout_ref[0:D_ATTN, :]) + _bdot(sn, wout_ref[D_ATTN:2 * D_ATTN, :])
    hmid_ref[...] = h
    _route(h, gffn_ref, wrt_ref, br_ref, tri_ref, cnt_ref, hn_ref, idx_ref, gw_ref, rank_ref)

    @pl.when(c == N_CHUNKS - 1)
    def _():
        klast_ref[...] = proj_s[:, D_ATTN:D_ATTN + D_KV].reshape(BATCH, CHUNK, D_KV)
        vlast_ref[...] = proj_s[:, D_ATTN + D_KV:D_ATTN + 2 * D_KV].reshape(BATCH, CHUNK, D_KV)


def _mixer_prompt(x_prompt, sinks, gmix, win, bias, bblk, lamr, lami, cblk, dskip, wglu, bglu, ga, gs, wout,
                  gffn, wrt, br):
    smem = pl.BlockSpec(memory_space=pltpu.SMEM)
    in_specs = [
        smem,
        pl.BlockSpec((BATCH, CHUNK, D_MODEL), lambda c: (0, c, 0)),
        _full((1, D_MODEL)), _full((D_MODEL, D_IN)), _full((2, 2, 4 * CHUNK, 2 * CHUNK)),
        _full((N_SBLK, 128, 2 * SBLK)), _full((N_SBLK, 1, SBLK)), _full((N_SBLK, 1, SBLK)),
        _full((N_SBLK, 2 * SBLK, 128)), _full((1, D_SSM)), _full((D_SSM, D_SSM)), _full((1, D_SSM)),
        _full((1, D_ATTN)), _full((1, D_SSM)), _full((D_MODEL, D_MODEL)),
        _full((1, D_MODEL)), _full((N_EXPERTS, D_MODEL)), _full((N_EXPERTS, 1)), _full((ROUTE_BLOCK, ROUTE_BLOCK)),
    ]
    route_specs, route_shapes = _route_specs(ROWS, lambda c: c)
    out_shape = [
        jax.ShapeDtypeStruct((T_PAD, D_MODEL), F32),
        jax.ShapeDtypeStruct((BATCH, CHUNK, D_KV), F32),
        jax.ShapeDtypeStruct((BATCH, CHUNK, D_KV), F32),
        jax.ShapeDtypeStruct((BATCH, SSM_G * SSM_P), F32),
        jax.ShapeDtypeStruct((BATCH, SSM_G * SSM_P), F32),
    ] + route_shapes
    out_specs = [
        pl.BlockSpec((ROWS, D_MODEL), lambda c: (c, 0)),
        _full((BATCH, CHUNK, D_KV)), _full((BATCH, CHUNK, D_KV)),
        _full((BATCH, SSM_G * SSM_P)), _full((BATCH, SSM_G * SSM_P)),
    ] + route_specs
    kv_scr = pltpu.VMEM((BATCH, CHUNK, D_KV), BF16)
    scratch = [
        pltpu.VMEM((ROWS, D_ATTN + 2 * D_KV), F32), pltpu.VMEM((D_SSM // 128, BATCH * PITCH, 128), F32),
        kv_scr, kv_scr, kv_scr, kv_scr,
        pltpu.VMEM((ROWS, D_ATTN), F32), pltpu.VMEM((D_SSM // 128, BATCH * PITCH, 128), F32),
        pltpu.VMEM((SUB_ROWS, D_SSM), F32), pltpu.VMEM((SUB_ROWS, 2 * SBLK), F32),
        pltpu.VMEM((SUB_ROWS, 2 * SBLK), F32), pltpu.VMEM((SUB_ROWS, D_SSM), F32),
    ]
    return pl.pallas_call(
        _mixer_prompt_kernel, grid=(N_CHUNKS,), in_specs=in_specs, out_specs=out_specs, out_shape=out_shape,
        scratch_shapes=scratch, name="mixer_prompt",
        compiler_params=pltpu.CompilerParams(dimension_semantics=("arbitrary",), vmem_limit_bytes=VMEM_LIMIT),
    )(sinks, x_prompt, gmix, win, bias, bblk, lamr, lami, cblk, dskip, wglu, bglu, ga, gs, wout,
      gffn, wrt, br, _tri(ROUTE_BLOCK))


SGRP = 16
SPITCH = DEC_BATCH + 8
N_SGRP = DEC_BATCH // SGRP


def _mixer_sample_kernel(x_ref, gmix_ref, win_ref, ck_ref, cv_ref, sbias_ref, sinkc_ref, bblk_ref, lamr_ref,
                         lami_ref, cblk_ref, dskip_ref, wglu_ref, bglu_ref, ga_ref, gs_ref, wout_ref,
                         x0r_ref, x0i_ref, gffn_ref, wrt_ref, br_ref, tri_ref, cnt_in_ref,
                         hbuf_ref, hnbuf_ref, idxbuf_ref, gwbuf_ref, rankbuf_ref,
                         hmid_ref, knew_ref, vnew_ref, sre_ref, sim_ref, hn_ref, idx_ref, gw_ref, rank_ref, cnt_ref,
                         proj_s, qall_s, oall_s, kn8_s, vn8_s):
    del hbuf_ref, hnbuf_ref, idxbuf_ref, gwbuf_ref, rankbuf_ref
    g = pl.program_id(0)
    lo = lax.broadcasted_iota(jnp.int32, (DEC_BATCH, 128), 1) < HEAD_DIM

    @pl.when(g == 0)
    def _():
        proj = _bdot(_rms(x_ref[...], gmix_ref[...]), win_ref[...])
        proj_s[...] = proj
        for h in range(N_HEADS):
            jq, half, kv = h // 2, h % 2, h // 4
            q2 = proj[:, 128 * jq:128 * (jq + 1)] * (HEAD_DIM ** -0.5)
            if half != kv:
                q2 = pltpu.roll(q2, HEAD_DIM, 1)
            hr = slice(h * SPITCH, h * SPITCH + DEC_BATCH)
            qall_s[hr, :] = jnp.where(lo if kv == 0 else jnp.logical_not(lo), q2, 0.0)
            kn8_s[hr, :] = proj[:, D_ATTN:D_ATTN + D_KV]
            vn8_s[hr, :] = proj[:, D_ATTN + D_KV:D_ATTN + 2 * D_KV]

    def head_rows(ref):
        return jnp.stack([ref[pl.ds(g * SGRP + ii, N_HEADS, stride=SPITCH), :] for ii in range(SGRP)], axis=0)

    qe = head_rows(qall_s)
    knew = head_rows(kn8_s)
    vnew = head_rows(vn8_s)
    s = jnp.einsum('bhd,bkd->bhk', qe.astype(BF16), ck_ref[...].astype(BF16),
                   preferred_element_type=F32) + sbias_ref[...][None]
    s_new = jnp.sum(qe * knew, axis=-1, keepdims=True)
    sink = sinkc_ref[...][None]
    m = jnp.maximum(jnp.maximum(jnp.max(s, axis=-1, keepdims=True), s_new), sink)
    p = jnp.exp(s - m)
    p_new = jnp.exp(s_new - m)
    den = jnp.sum(p, axis=-1, keepdims=True) + p_new + jnp.exp(sink - m)
    o = (jnp.einsum('bhk,bkd->bhd', p.astype(BF16), cv_ref[...].astype(BF16),
                    preferred_element_type=F32) + p_new * vnew) / den
    for ii in range(SGRP):
        oall_s[pl.ds(g * SGRP + ii, N_HEADS, stride=SPITCH), :] = o[ii]

    @pl.when(g == N_SGRP - 1)
    def _():
        proj = proj_s[...]
        blocks = []
        for jq in range(N_HEADS // 2):
            oa = oall_s[(2 * jq) * SPITCH:(2 * jq) * SPITCH + DEC_BATCH, :]
            ob = oall_s[(2 * jq + 1) * SPITCH:(2 * jq + 1) * SPITCH + DEC_BATCH, :]
            if jq // 2 == 0:
                blocks.append(jnp.where(lo, oa, pltpu.roll(ob, HEAD_DIM, 1)))
            else:
                blocks.append(jnp.where(lo, pltpu.roll(oa, HEAD_DIM, 1), ob))
        attn = jnp.concatenate(blocks, axis=1)
        u = proj[:, D_ATTN + 2 * D_KV:]
        ys = []
        for j in range(N_SBLK):
            bu = _bdot(u[:, 128 * j:128 * (j + 1)], bblk_ref[j])
            lr = lamr_ref[j]
            li = lami_ref[j]
            cols = slice(SBLK * j, SBLK * (j + 1))
            sr = x0r_ref[:, cols]
            si = x0i_ref[:, cols]
            nr = lr * sr - li * si + bu[:, 0:SBLK]
            ni = lr * si + li * sr + bu[:, SBLK:2 * SBLK]
            sre_ref[:, cols] = nr
            sim_ref[:, cols] = ni
            ys.append(_bdot(jnp.concatenate([nr, ni], axis=1), cblk_ref[j]))
        sn = _ssm_post(jnp.concatenate(ys, axis=1), u, dskip_ref[...], wglu_ref[...], bglu_ref[...], gs_ref[...])
        an = _rms(attn, ga_ref[...])
        h = x_ref[...] + _bdot(an, wout_ref[0:D_ATTN, :]) + _bdot(sn, wout_ref[D_ATTN:2 * D_ATTN, :])
        h = jnp.concatenate([h, jnp.zeros((DEC_BATCH, D_MODEL), F32)], axis=0)
        hmid_ref[...] = h
        cnt_ref[...] = cnt_in_ref[...]
        _route(h, gffn_ref, wrt_ref, br_ref, tri_ref, cnt_ref, hn_ref, idx_ref, gw_ref, rank_ref)
        knew_ref[...] = proj[:, D_ATTN:D_ATTN + D_KV]
        vnew_ref[...] = proj[:, D_ATTN + D_KV:D_ATTN + 2 * D_KV]


def _mixer_sample(x_s, gmix, win, ck, cv, sbias, sinkc, bblk, lamr, lami, cblk, dskip, wglu, bglu, ga, gs, wout,
                  x0r, x0i, gffn, wrt, br, cnt_in, hbuf, hnbuf, idxbuf, gwbuf, rankbuf):
    nst = SSM_G * SSM_P
    in_specs = [
        _full((DEC_BATCH, D_MODEL)), _full((1, D_MODEL)), _full((D_MODEL, D_IN)),
        pl.BlockSpec((SGRP, WINDOW, D_KV), lambda g: (g, 0, 0)),
        pl.BlockSpec((SGRP, WINDOW, D_KV), lambda g: (g, 0, 0)),
        _full((N_HEADS, WINDOW)), _full((N_HEADS, 1)),
        _full((N_SBLK, 128, 2 * SBLK)), _full((N_SBLK, 1, SBLK)), _full((N_SBLK, 1, SBLK)),
        _full((N_SBLK, 2 * SBLK, 128)), _full((1, D_SSM)), _full((D_SSM, D_SSM)), _full((1, D_SSM)),
        _full((1, D_ATTN)), _full((1, D_SSM)), _full((D_MODEL, D_MODEL)),
        _full((DEC_BATCH, nst)), _full((DEC_BATCH, nst)),
        _full((1, D_MODEL)), _full((N_EXPERTS, D_MODEL)), _full((N_EXPERTS, 1)),
        _full((2 * DEC_BATCH, 2 * DEC_BATCH)), _full((N_EXPERTS, 1)),
    ] + [pl.BlockSpec(memory_space=pl.ANY)] * 5
    tail_block = SAMPLE_ROW0 // (2 * DEC_BATCH)
    route_specs, route_shapes = _route_specs(2 * DEC_BATCH, lambda g: tail_block)
    out_shape = [
        jax.ShapeDtypeStruct((T_PAD, D_MODEL), F32),
        jax.ShapeDtypeStruct((DEC_BATCH, D_KV), F32), jax.ShapeDtypeStruct((DEC_BATCH, D_KV), F32),
        jax.ShapeDtypeStruct((DEC_BATCH, nst), F32), jax.ShapeDtypeStruct((DEC_BATCH, nst), F32),
    ] + route_shapes
    out_specs = [
        pl.BlockSpec((2 * DEC_BATCH, D_MODEL), lambda g: (tail_block, 0)),
        _full((DEC_BATCH, D_KV)), _full((DEC_BATCH, D_KV)),
        _full((DEC_BATCH, nst)), _full((DEC_BATCH, nst)),
    ] + route_specs
    head_rows = pltpu.VMEM((N_HEADS * SPITCH, 128), F32)
    scratch = [pltpu.VMEM((DEC_BATCH, D_IN), F32), head_rows, head_rows, head_rows, head_rows]
    return pl.pallas_call(
        _mixer_sample_kernel, grid=(N_SGRP,), in_specs=in_specs, out_specs=out_specs, out_shape=out_shape,
        scratch_shapes=scratch, input_output_aliases={24: 0, 25: 5, 26: 6, 27: 7, 28: 8}, name="mixer_sample",
        compiler_params=pltpu.CompilerParams(dimension_semantics=("arbitrary",), vmem_limit_bytes=VMEM_LIMIT),
    )(x_s, gmix, win, ck, cv, sbias, sinkc, bblk, lamr, lami, cblk, dskip, wglu, bglu, ga, gs, wout, x0r, x0i,
      gffn, wrt, br, _tri(2 * DEC_BATCH), cnt_in, hbuf, hnbuf, idxbuf, gwbuf, rankbuf)


def _route(h, g_ref, wrt_ref, br_ref, tri_ref, cnt_ref, hn_ref, idx_ref, gw_ref, rank_ref):
    hn = _rms(h, g_ref[...])
    hn_ref[...] = _pack_pairs(hn)
    hn_hi = hn.astype(BF16)
    hn_lo = (hn - hn_hi.astype(F32)).astype(BF16)
    w = wrt_ref[...]
    w_hi = w.astype(BF16)
    w_lo = (w - w_hi.astype(F32)).astype(BF16)
    nt = (((1,), (1,)), ((), ()))
    logits = (lax.dot_general(w_hi, hn_hi, nt, preferred_element_type=F32)
              + lax.dot_general(w_lo, hn_hi, nt, preferred_element_type=F32)
              + lax.dot_general(w_hi, hn_lo, nt, preferred_element_type=F32)) + br_ref[...]
    eidx = lax.broadcasted_iota(jnp.int32, logits.shape, 0)
    vals, onehots = [], []
    l = logits
    for k in range(TOP_K):
        m = jnp.max(l, axis=0, keepdims=True)
        ik = jnp.min(jnp.where(l == m, eidx, N_EXPERTS), axis=0, keepdims=True)
        oh = eidx == ik
        idx_ref[k:k + 1, :] = ik
        vals.append(m)
        onehots.append(oh)
        l = jnp.where(oh, -jnp.inf, l)
    exps = [jnp.exp(v - vals[0]) for v in vals]
    den = exps[0] + exps[1] + exps[2] + exps[3]
    for k in range(TOP_K):
        gw_ref[k:k + 1, :] = exps[k] / den
    member = jnp.zeros(logits.shape, F32)
    for oh in onehots:
        member = member + jnp.where(oh, 1.0, 0.0)
    wblk = tri_ref.shape[0]
    base = cnt_ref[...].astype(F32)
    befores = []
    for cb in range(h.shape[0] // wblk):
        mblk = member[:, cb * wblk:(cb + 1) * wblk]
        befores.append(jnp.dot(mblk.astype(BF16), tri_ref[...], preferred_element_type=F32) + base)
        base = base + jnp.sum(mblk, axis=1, keepdims=True)
    before = jnp.concatenate(befores, axis=1)
    for k in range(TOP_K):
        rank_ref[k:k + 1, :] = jnp.sum(jnp.where(onehots[k], before, 0.0), axis=0, keepdims=True).astype(jnp.int32)
    cnt_ref[...] = base.astype(jnp.int32)


def _route_specs(rows, block):
    specs = [pl.BlockSpec((rows, D_PACK), lambda i: (block(i), 0)), pl.BlockSpec((TOP_K, rows), lambda i: (0, block(i))),
             pl.BlockSpec((TOP_K, rows), lambda i: (0, block(i))), pl.BlockSpec((TOP_K, rows), lambda i: (0, block(i))),
             _full((N_EXPERTS, 1))]
    shapes = [jax.ShapeDtypeStruct((T_PAD, D_PACK), jnp.int32), jax.ShapeDtypeStruct((TOP_K, T_PAD), jnp.int32),
              jax.ShapeDtypeStruct((TOP_K, T_PAD), F32), jax.ShapeDtypeStruct((TOP_K, T_PAD), jnp.int32),
              jax.ShapeDtypeStruct((N_EXPERTS, 1), jnp.int32)]
    return specs, shapes


def _tri(n):
    return jnp.asarray(np.triu(np.ones((n, n), np.float32), 1), BF16)


def _place_kernel(off_ref, idx_ref, rank_ref, pos_ref):
    idx = idx_ref[...]
    pos = rank_ref[...]
    for e in range(N_EXPERTS):
        pos = pos + jnp.where(idx == e, off_ref[e], 0)
    pos_ref[...] = pos


def _place(offsets, idx, rank):
    return pl.pallas_call(
        _place_kernel,
        in_specs=[pl.BlockSpec(memory_space=pltpu.SMEM), pl.BlockSpec(memory_space=pltpu.VMEM),
                  pl.BlockSpec(memory_space=pltpu.VMEM)],
        out_specs=pl.BlockSpec(memory_space=pltpu.VMEM),
        out_shape=jax.ShapeDtypeStruct((TOP_K, T_PAD), jnp.int32), name="place",
    )(offsets, idx, rank)


def _sc_mesh():
    return plsc.VectorSubcoreMesh(core_axis_name="core", subcore_axis_name="subcore")


def _sc_dispatch(rows, pos):
    n, d = rows.shape
    nblk = n // SC_ROWS
    pos_w = pos.reshape(TOP_K, nblk, SC_ROWS).transpose(1, 0, 2)

    @functools.partial(pl.kernel, out_type=jax.ShapeDtypeStruct((P_ALLOC, d), rows.dtype), mesh=_sc_mesh(),
                       scratch_types=[], name="dispatch")
    def run(x_hbm, i_hbm, o_hbm):
        def body(x_vmem, i_vmem):
            for k in range(TOP_K):
                pltpu.sync_copy(x_vmem, o_hbm.at[i_vmem.at[0, k]])

        pltpu.emit_pipeline(
            body, grid=(nblk,),
            in_specs=[pl.BlockSpec((SC_ROWS, d), lambda i: (i, 0)),
                      pl.BlockSpec((1, TOP_K, SC_ROWS), lambda i: (i, 0, 0))],
            out_specs=[], core_axis_name=("core", "subcore"), dimension_semantics=(pltpu.PARALLEL,),
        )(x_hbm, i_hbm)

    return run(rows, pos_w)


def _sc_build_dest(flat_pos, flat_ids, after):
    n = flat_pos.shape[0]
    stage = n // 32
    lanes = 16
    unroll = 5
    params = pltpu.CompilerParams(needs_layout_passes=False)

    @functools.partial(pl.kernel, out_type=jax.ShapeDtypeStruct((P_ALLOC,), jnp.int32), mesh=_sc_mesh(),
                       scratch_types=[pltpu.VMEM((DEST_SLICE,), jnp.int32), pltpu.VMEM((stage,), jnp.int32),
                                      pltpu.VMEM((stage,), jnp.int32)],
                       compiler_params=params, name="build_dest")
    def run(pos_hbm, ids_hbm, after_hbm, o_hbm, local, pbuf, ibuf):
        del after_hbm
        lo = (lax.axis_index("core") * 16 + lax.axis_index("subcore")) * DEST_SLICE

        @pl.loop(0, DEST_SLICE, step=lanes)
        def _(i):
            local[pl.ds(i, lanes)] = TRASH_ROW + ((lo + i + lax.iota(jnp.int32, lanes)) & (N_TRASH - 1))

        @pl.loop(0, n // stage)
        def _(c):
            pltpu.sync_copy(pos_hbm.at[pl.ds(c * stage, stage)], pbuf)
            pltpu.sync_copy(ids_hbm.at[pl.ds(c * stage, stage)], ibuf)

            @pl.loop(0, stage, step=lanes * unroll)
            def _(j0):
                for u in range(unroll):
                    j = j0 + u * lanes
                    rel = pbuf[pl.ds(j, lanes)] - lo
                    plsc.store_scatter(local, [rel], ibuf[pl.ds(j, lanes)], mask=(rel >= 0) & (rel < DEST_SLICE))

        pltpu.sync_copy(local, o_hbm.at[pl.ds(lo, DEST_SLICE)])

    return run(flat_pos, flat_ids, after)


def _sc_combine_scatter(rows, dest, yg_prev, chunk):
    w = SC_ROWS_COMBINE
    nblk = (CHUNK_BOUNDS[chunk + 1] - CHUNK_BOUNDS[chunk]) // w
    b0 = CHUNK_BOUNDS[chunk] // w
    dest_w = dest.reshape(P_ALLOC // w, 1, w)

    def body_fn(x_hbm, i_hbm, *rest):
        o_hbm = rest[-1]

        def body(x_vmem, i_vmem):
            pltpu.sync_copy(x_vmem, o_hbm.at[i_vmem.at[0, 0]])

        pltpu.emit_pipeline(
            body, grid=(nblk,),
            in_specs=[pl.BlockSpec((w, D_PACK), lambda i: (b0 + i, 0)),
                      pl.BlockSpec((1, 1, w), lambda i: (b0 + i, 0, 0))],
            out_specs=[], core_axis_name=("core", "subcore"), dimension_semantics=(pltpu.PARALLEL,),
        )(x_hbm, i_hbm)

    out_type = jax.ShapeDtypeStruct((YG_ROWS, D_PACK), rows.dtype)
    if yg_prev is None:
        return pl.kernel(body_fn, out_type=out_type, mesh=_sc_mesh(), scratch_types=[],
                         name="combine")(rows, dest_w)
    return mpmd._mpmd_map([(_sc_mesh(), body_fn)], out_type, input_output_aliases={2: 0}, scratch_types=(),
                          name="combine")(rows, dest_w, yg_prev)


def _experts_kernel(ts_ref, trow_ref, tsz_ref, rng_ref, xs_hbm, w1_hbm, b1_hbm, w2_hbm, b2_hbm, ys_hbm,
                    w1_st, w2_st, b1_st, b2_st, w1_s, w2_s, xbuf, ybuf, wsem, xsem, ysem):
    elo, ehi, tlo, thi = rng_ref[0], rng_ref[1], rng_ref[2], rng_ref[3]

    def w_copies(ex, slot):
        return (pltpu.make_async_copy(w1_hbm.at[ex], w1_st.at[slot], wsem.at[0, slot]),
                pltpu.make_async_copy(w2_hbm.at[ex], w2_st.at[slot], wsem.at[1, slot]),
                pltpu.make_async_copy(b1_hbm.at[ex], b1_st.at[slot], wsem.at[2, slot]),
                pltpu.make_async_copy(b2_hbm.at[ex], b2_st.at[slot], wsem.at[3, slot]))

    def x_copy(t, slot, rows):
        src = xs_hbm.at[pl.ds(pl.multiple_of(trow_ref[t], TM_UNIT), rows)]
        return pltpu.make_async_copy(src, xbuf.at[slot, pl.ds(0, rows)], xsem.at[slot])

    def y_copy(t, slot, rows):
        dst = ys_hbm.at[pl.ds(pl.multiple_of(trow_ref[t], TM_UNIT), rows)]
        return pltpu.make_async_copy(ybuf.at[slot, pl.ds(0, rows)], dst, ysem.at[slot])

    def by_size(t, fn):
        for rows in TILE_SIZES:
            @pl.when(tsz_ref[t] == rows)
            def _():
                fn(rows)

    @pl.when(tlo < thi)
    def _():
        for c in w_copies(elo, 0):
            c.start()
        by_size(tlo, lambda rows: x_copy(tlo, tlo % 2, rows).start())

    def expert(e, carry):
        wslot = (e - elo) % 2

        @pl.when(e + 1 < ehi)
        def _():
            for c in w_copies(e + 1, 1 - wslot):
                c.start()

        for c in w_copies(e, wslot):
            c.wait()
        for r in range(4):
            rs = slice(256 * r, 256 * (r + 1))
            w1_s[rs, :] = w1_st[wslot, rs, :].astype(BF16)
            w2_s[rs, :] = w2_st[wslot, rs, :].astype(BF16)
        b1 = b1_st[wslot]
        b2 = b2_st[wslot]

        def tile(t, carry):
            slot = t % 2

            @pl.when(t + 1 < thi)
            def _():
                by_size(t + 1, lambda rows: x_copy(t + 1, 1 - slot, rows).start())

            by_size(t, lambda rows: x_copy(t, slot, rows).wait())

            @pl.when(t - 2 >= tlo)
            def _():
                by_size(t - 2, lambda rows: y_copy(t - 2, slot, rows).wait())

            def compute(rows):
                hdn = _bdot(_unpack_pairs(xbuf[slot, 0:rows]), w1_s[...]) + b1
                gt = jnp.minimum(hdn[:, :D_FF], SWIGLU_LIMIT)
                up = jnp.clip(hdn[:, D_FF:], -SWIGLU_LIMIT, SWIGLU_LIMIT)
                act = (up + 1.0) * gt * _sigmoid(SWIGLU_ALPHA * gt)
                ybuf[slot, 0:rows] = _pack_pairs(_bdot(act, w2_s[...]) + b2)
                y_copy(t, slot, rows).start()

            by_size(t, compute)
            return carry

        lax.fori_loop(jnp.maximum(ts_ref[e], tlo), jnp.minimum(ts_ref[e + 1], thi), tile, 0)
        return carry

    lax.fori_loop(elo, ehi, expert, 0)

    @pl.when(thi - tlo >= 2)
    def _():
        by_size(thi - 2, lambda rows: y_copy(thi - 2, thi % 2, rows).wait())

    @pl.when(thi - tlo >= 1)
    def _():
        by_size(thi - 1, lambda rows: y_copy(thi - 1, (thi - 1) % 2, rows).wait())


def _tile_plan(counts):
    units = (counts + (TM_UNIT - 1)) // TM_UNIT
    unit_end = jnp.cumsum(units)
    offsets = ((unit_end - units) * TM_UNIT).astype(jnp.int32)
    per_full = TM // TM_UNIT
    n_full = units // per_full
    rest = units % per_full
    has_mid = (rest >= 2).astype(jnp.int32)
    tiles_per = n_full + has_mid + rest % 2
    tile_end = jnp.cumsum(tiles_per)
    tile_start = jnp.concatenate([jnp.zeros((1,), jnp.int32), tile_end.astype(jnp.int32)])
    t = jnp.arange(MAX_TILES, dtype=jnp.int32)[:, None]
    done = tile_end[None, :] <= t
    mine = jnp.sum(done.astype(jnp.int32), axis=1, keepdims=True) == jnp.arange(N_EXPERTS, dtype=jnp.int32)[None, :]
    pick = lambda v: jnp.sum(jnp.where(mine, v[None, :], 0), axis=1)
    j = t[:, 0] - jnp.max(jnp.where(done, tile_end[None, :], 0), axis=1)
    nf, mid = pick(n_full), pick(has_mid)
    rows = jnp.where(j < nf, TM, jnp.where((j == nf) & (mid == 1), TILE_SIZES[1], TILE_SIZES[2]))
    row0 = pick(offsets) + jnp.minimum(j, nf) * TM + jnp.where(j > nf, TILE_SIZES[1], 0)
    valid = t[:, 0] < tile_end[-1]
    tile_row = jnp.where(valid, row0, 0).astype(jnp.int32)
    tile_rows = jnp.where(valid, rows, 0).astype(jnp.int32)
    return offsets, tile_start, tile_row, tile_rows


def _chunk_ranges(tile_start, tile_row, tile_rows):
    valid = tile_rows > 0
    tile_end = tile_start[1:]
    out = []
    for c in range(len(CHUNK_UNITS)):
        r0, r1 = CHUNK_BOUNDS[c], CHUNK_BOUNDS[c + 1]
        tlo = jnp.sum((valid & (tile_row + tile_rows <= r0)).astype(jnp.int32))
        thi = jnp.sum((valid & (tile_row < r1)).astype(jnp.int32))
        elo = jnp.sum((tile_end <= tlo).astype(jnp.int32))
        ehi = jnp.where(thi > tlo, jnp.sum((tile_end <= thi - 1).astype(jnp.int32)) + 1, elo)
        out.append(jnp.stack([elo, ehi, tlo, thi]).astype(jnp.int32))
    return out


def _experts(tile_start, tile_row, tile_rows, chunk_range, xs, w_up, b_up, w_down, b_down):
    hbm = pl.BlockSpec(memory_space=pl.ANY)
    grid_spec = pltpu.PrefetchScalarGridSpec(
        num_scalar_prefetch=4, grid=(1,),
        in_specs=[hbm, hbm, hbm, hbm, hbm],
        out_specs=hbm,
        scratch_shapes=[pltpu.VMEM((2, D_MODEL, 2 * D_FF), F32), pltpu.VMEM((2, D_FF, D_MODEL), F32),
                        pltpu.VMEM((2, 1, 2 * D_FF), F32), pltpu.VMEM((2, 1, D_MODEL), F32),
                        pltpu.VMEM((D_MODEL, 2 * D_FF), BF16), pltpu.VMEM((D_FF, D_MODEL), BF16),
                        pltpu.VMEM((2, TM, D_PACK), jnp.int32), pltpu.VMEM((2, TM, D_PACK), jnp.int32),
                        pltpu.SemaphoreType.DMA((4, 2)), pltpu.SemaphoreType.DMA((2,)),
                        pltpu.SemaphoreType.DMA((2,))],
    )
    return pl.pallas_call(
        _experts_kernel, grid_spec=grid_spec, out_shape=jax.ShapeDtypeStruct((P_ALLOC, D_PACK), jnp.int32),
        name="experts",
        compiler_params=pltpu.CompilerParams(dimension_semantics=("arbitrary",), vmem_limit_bytes=VMEM_LIMIT),
    )(tile_start, tile_row, tile_rows, chunk_range, xs, w_up, b_up.reshape(N_EXPERTS, 1, 2 * D_FF), w_down,
      b_down.reshape(N_EXPERTS, 1, D_MODEL))


def _final_kernel(h_ref, yg0_ref, yg1_ref, yg2_ref, yg3_ref, gw_ref, p_ref, gple_ref, wg_ref, wp_ref, gfin_ref,
                  out_ref):
    rows = h_ref.shape[0]
    gw = gw_ref[...]
    h = h_ref[...]
    for k, yg_ref in enumerate((yg0_ref, yg1_ref, yg2_ref, yg3_ref)):
        h = h + gw[:, k:k + 1] * _unpack_pairs(yg_ref[...])
    gate = _sigmoid(_bdot(_rms(h, gple_ref[...]), wg_ref[...]))
    h = h + gate * _bdot(p_ref[...].reshape(rows, PLE_DIM), wp_ref[...])
    out_ref[...] = _rms(h, gfin_ref[...]).reshape(out_ref.shape)


def _final_prompt(hmid, yg, gwt, p_prompt, gple, wg, wp, gfin):
    nb = 4
    rows = nb * CHUNK
    nbh = BATCH // nb
    rb = lambda c, b: (c * nbh + b, 0)
    slot = lambda k: pl.BlockSpec((rows, D_PACK), lambda c, b: (k * (K_STRIDE // rows) + c * nbh + b, 0))
    return pl.pallas_call(
        _final_kernel, grid=(N_CHUNKS, nbh),
        in_specs=[pl.BlockSpec((rows, D_MODEL), rb), slot(0), slot(1), slot(2), slot(3),
                  pl.BlockSpec((rows, TOP_K), rb),
                  pl.BlockSpec((nb, CHUNK, PLE_DIM), lambda c, b: (b, c, 0)),
                  _full((1, D_MODEL)), _full((D_MODEL, D_MODEL)), _full((PLE_DIM, D_MODEL)), _full((1, D_MODEL))],
        out_specs=pl.BlockSpec((nb, CHUNK, D_MODEL), lambda c, b: (b, c, 0)),
        out_shape=jax.ShapeDtypeStruct((BATCH, SEQ, D_MODEL), F32), name="final_prompt",
        compiler_params=pltpu.CompilerParams(dimension_semantics=("arbitrary", "arbitrary"),
                                             vmem_limit_bytes=VMEM_LIMIT),
    )(hmid, yg, yg, yg, yg, gwt, p_prompt, gple, wg, wp, gfin)


def _final_sample(hmid, yg, gwt, p_sample, gple, wg, wp, gfin):
    blk = SAMPLE_ROW0 // DEC_BATCH
    slot = lambda k: pl.BlockSpec((DEC_BATCH, D_PACK), lambda i: (k * (K_STRIDE // DEC_BATCH) + blk, 0))
    return pl.pallas_call(
        _final_kernel, grid=(1,),
        in_specs=[pl.BlockSpec((DEC_BATCH, D_MODEL), lambda i: (blk, 0)), slot(0), slot(1), slot(2), slot(3),
                  pl.BlockSpec((DEC_BATCH, TOP_K), lambda i: (blk, 0)),
                  _full((DEC_BATCH, PLE_DIM)),
                  _full((1, D_MODEL)), _full((D_MODEL, D_MODEL)), _full((PLE_DIM, D_MODEL)), _full((1, D_MODEL))],
        out_specs=_full((DEC_BATCH, D_MODEL)),
        out_shape=jax.ShapeDtypeStruct((DEC_BATCH, D_MODEL), F32), name="final_sample",
        compiler_params=pltpu.CompilerParams(dimension_semantics=("arbitrary",), vmem_limit_bytes=VMEM_LIMIT),
    )(hmid, yg, yg, yg, yg, gwt, p_sample, gple, wg, wp, gfin)


def _alibi_tables():
    slopes = 2.0 ** (-8.0 * (np.arange(N_HEADS, dtype=np.float64) + 1.0) / N_HEADS)
    i = np.arange(CHUNK)[:, None]
    j = np.arange(2 * CHUNK)[None, :]
    dist = i + CHUNK - j
    valid = (dist >= 0) & (dist <= WINDOW)
    tabs = []
    for has_prev in (False, True):
        ok = valid & ((j >= CHUNK) | has_prev)
        tabs.append(np.where(ok[None], -slopes[:, None, None] * dist[None], NEG))
    prompt = np.stack([np.stack([np.concatenate([t[h] for h in grp], axis=0) for grp in (NAT_HEADS, ROT_HEADS)])
                       for t in tabs]).astype(np.float32)
    wb = min(WINDOW, PAST_LEN)
    sample = (-slopes[:, None] * (wb - np.arange(wb))[None, :]).astype(np.float32)
    return prompt, sample


def kernel(x_prompt, x_sample, cache_k_win, cache_v_win, state_ssm_re, state_ssm_im, p_prompt, p_sample, norm_mix, w_in, sinks, ssm_lam_re, ssm_lam_im, ssm_log_step, ssm_b_re, ssm_b_im, ssm_c_re, ssm_c_im, ssm_d, w_glu, b_glu, norm_attn_out, norm_ssm_out, w_out, norm_ffn, w_router, b_router, w_up, b_up, w_down, b_down, norm_ple, w_ple_gate, w_ple_proj, norm_final):
    nst = SSM_G * SSM_P
    bias_np, sbias_np = _alibi_tables()
    bias = jnp.asarray(bias_np)
    sbias = jnp.asarray(sbias_np)

    lbr, lbi, bbr, bbi = _prep(ssm_lam_re[0], ssm_lam_im[0], ssm_log_step[0], ssm_b_re[0], ssm_b_im[0])
    bblk, lamr, lami, cblk = _s5_blocks(lbr, lbi, bbr, bbi, ssm_c_re[0], ssm_c_im[0])

    gmix = norm_mix[0].reshape(1, D_MODEL)
    win = w_in[0].astype(BF16)
    dskip = ssm_d[0].reshape(1, D_SSM)
    wglu = w_glu[0].astype(BF16)
    bglu = b_glu[0].reshape(1, D_SSM)
    ga = norm_attn_out[0].reshape(1, D_ATTN)
    gs = norm_ssm_out[0].reshape(1, D_SSM)
    wout = w_out[0].astype(BF16)
    sink = sinks[0]

    gffn = norm_ffn[0].reshape(1, D_MODEL)
    wrt = w_router[0].T
    br = b_router[0].reshape(N_EXPERTS, 1)
    hbuf, k_last, v_last, re_p, im_p, hnbuf, idxbuf, gwbuf, rankbuf, cnt_p = _mixer_prompt(
        x_prompt, sink, gmix, win, bias, bblk, lamr, lami, cblk, dskip, wglu, bglu, ga, gs, wout, gffn, wrt, br)

    ck = cache_k_win[0].reshape(DEC_BATCH, WINDOW, D_KV)
    cv = cache_v_win[0].reshape(DEC_BATCH, WINDOW, D_KV)
    hmid, k_new, v_new, re_s, im_s, hn, idx, gw, rank, counts = _mixer_sample(
        x_sample.reshape(DEC_BATCH, D_MODEL), gmix, win, ck, cv, sbias, sink.reshape(N_HEADS, 1), bblk, lamr, lami,
        cblk, dskip, wglu, bglu, ga, gs, wout, state_ssm_re[0].reshape(DEC_BATCH, nst),
        state_ssm_im[0].reshape(DEC_BATCH, nst), gffn, wrt, br, cnt_p, hbuf, hnbuf, idxbuf, gwbuf, rankbuf)

    offsets, tile_start, tile_row, tile_rows = _tile_plan(counts[:, 0])

    pos = _place(offsets, idx, rank)
    xs = _sc_dispatch(hn, pos)
    pair_ids = (jnp.arange(TOP_K, dtype=jnp.int32)[:, None] * K_STRIDE + jnp.arange(T_PAD, dtype=jnp.int32)[None, :])
    dest = _sc_build_dest(pos.reshape(TOP_K * T_PAD), pair_ids.reshape(TOP_K * T_PAD), xs)
    yg = None
    for c, chunk_range in enumerate(_chunk_ranges(tile_start, tile_row, tile_rows)):
        ys = _experts(tile_start, tile_row, tile_rows, chunk_range, xs, w_up[0], b_up[0], w_down[0], b_down[0])
        yg = _sc_combine_scatter(ys, dest, yg, c)

    gwt = gw.T
    gple = norm_ple[0].reshape(1, D_MODEL)
    wg = w_ple_gate[0].astype(BF16)
    wp = w_ple_proj[0].astype(BF16)
    gfin = norm_final.reshape(1, D_MODEL)
    y_prompt = _final_prompt(hmid, yg, gwt, p_prompt[0], gple, wg, wp, gfin)
    y_sample = _final_sample(hmid, yg, gwt, p_sample[0].reshape(DEC_BATCH, PLE_DIM), gple, wg, wp, gfin)

    k_win_s = jnp.concatenate([ck[:, 1:], k_new[:, None, :]], axis=1)
    v_win_s = jnp.concatenate([cv[:, 1:], v_new[:, None, :]], axis=1)
    kv5 = (1, BATCH, CHUNK, N_KV, HEAD_DIM)
    skv5 = (1, DEC_BATCH, WINDOW, N_KV, HEAD_DIM)
    return (y_prompt, y_sample.reshape(DEC_BATCH, 1, D_MODEL),
            k_last.reshape(kv5), v_last.reshape(kv5),
            re_p.reshape(1, BATCH, SSM_G, SSM_P), im_p.reshape(1, BATCH, SSM_G, SSM_P),
            k_win_s.reshape(skv5), v_win_s.reshape(skv5),
            re_s.reshape(1, DEC_BATCH, SSM_G, SSM_P), im_s.reshape(1, DEC_BATCH, SSM_G, SSM_P))
```

```python
import functools

import numpy as np
import jax
import jax.numpy as jnp
from jax import lax
from jax.experimental import pallas as pl
from jax.experimental.pallas import tpu as pltpu
from jax.experimental.pallas import tpu_sc as plsc
from jax._src.pallas import mpmd

F32 = jnp.float32
BF16 = jnp.bfloat16

D_MODEL = 1024
BATCH = 8
SEQ = 2048
DEC_BATCH = 128
PAST_LEN = 16384
HEAD_DIM = 64
D_ATTN = 512
N_HEADS = 8
N_KV = 2
D_KV = N_KV * HEAD_DIM
WINDOW = 128
D_SSM = 512
SSM_H = 16
SSM_G = 32
SSM_P = 64
D_IN = D_ATTN + 2 * D_KV + D_SSM
N_EXPERTS = 32
TOP_K = 4
D_FF = 1024
SWIGLU_LIMIT = 7.0
SWIGLU_ALPHA = 1.702
PLE_DIM = 256
EPS = 1e-5
NEG = -1e30

CHUNK = 128
N_CHUNKS = SEQ // CHUNK
ROWS = BATCH * CHUNK
PITCH = CHUNK + 8
NAT_HEADS = (0, 2, 5, 7)
ROT_HEADS = (1, 3, 4, 6)
SUB_T = 32
SUB_ROWS = SUB_T * BATCH
N_SBLK = 4
SBLK = 512
T_REAL = BATCH * SEQ + DEC_BATCH
T_PAD = T_REAL + 128
SAMPLE_ROW0 = BATCH * SEQ
ROUTE_BLOCK = 512
TILE_SIZES = (1024, 512, 256)
TM = TILE_SIZES[0]
TM_UNIT = TILE_SIZES[-1]
P_ROWS = (T_PAD * TOP_K + N_EXPERTS * (TM_UNIT - 1)) // TM_UNIT * TM_UNIT
MAX_TILES = P_ROWS // TM + 2 * N_EXPERTS + 1
CHUNK_UNIT = 1024
CHUNK_UNITS = (60, 16)
CHUNK_BOUNDS = tuple(CHUNK_UNIT * sum(CHUNK_UNITS[:c]) for c in range(len(CHUNK_UNITS) + 1))
P_ALLOC = CHUNK_BOUNDS[-1]
DEST_SLICE = P_ALLOC // 32
K_STRIDE = 16896
TRASH_ROW = TOP_K * K_STRIDE
N_TRASH = 2048
YG_ROWS = TRASH_ROW + N_TRASH
SC_ROWS = 40
SC_ROWS_COMBINE = 32
D_PACK = D_MODEL // 2
VMEM_LIMIT = 56 * 1024 * 1024


def _rms(x, g):
    return x * lax.rsqrt(jnp.mean(x * x, axis=-1, keepdims=True) + EPS) * g


def _sigmoid(x):
    return 1.0 / (1.0 + jnp.exp(-x))


def _gelu_tanh(x):
    c = np.float32(np.sqrt(2.0 / np.pi))
    return 0.5 * x * (1.0 + jnp.tanh(c * (x + 0.044715 * (x * x * x))))


def _bdot(a, b):
    return jnp.dot(a.astype(BF16), b, preferred_element_type=F32)


def _pack_pairs(x):
    n = x.shape[1] // 2
    lo = lax.bitcast_convert_type(x[:, :n].astype(BF16).astype(F32), jnp.int32)
    hi = lax.bitcast_convert_type(x[:, n:].astype(BF16).astype(F32), jnp.int32)
    return lax.shift_right_logical(lo, 16) | hi


def _unpack_pairs(w):
    lo = lax.bitcast_convert_type(lax.shift_left(w, 16), F32)
    hi = lax.bitcast_convert_type(w & jnp.int32(-65536), F32)
    return jnp.concatenate([lo, hi], axis=1)


def _full(shape):
    n = len(shape)
    return pl.BlockSpec(shape, lambda *_: (0,) * n)


def _prep_kernel(lr_ref, li_ref, ls_ref, br_ref, bi_ref, lbr_ref, lbi_ref, bbr_ref, bbi_ref):
    lr = lr_ref[...]
    li = li_ref[...]
    step = jnp.exp(ls_ref[...])
    zr = lr * step
    zi = li * step
    mag = jnp.exp(zr)
    lbr = mag * jnp.cos(zi)
    lbi = mag * jnp.sin(zi)
    lbr_ref[...] = lbr
    lbi_ref[...] = lbi
    nr = lbr - 1.0
    den = lr * lr + li * li
    cr = (nr * lr + lbi * li) / den
    ci = (lbi * lr - nr * li) / den
    br = br_ref[...]
    bi = bi_ref[...]
    bbr_ref[...] = cr * br - ci * bi
    bbi_ref[...] = cr * bi + ci * br


def _prep(lam_re, lam_im, log_step, b_re, b_im):
    g, p, h = SSM_G, SSM_P, SSM_H
    out = pl.pallas_call(
        _prep_kernel,
        out_shape=[jax.ShapeDtypeStruct((g, 1, p), F32), jax.ShapeDtypeStruct((g, 1, p), F32),
                   jax.ShapeDtypeStruct((g, h, p), F32), jax.ShapeDtypeStruct((g, h, p), F32)],
        name="s5_prep",
    )(lam_re.reshape(g, 1, p), lam_im.reshape(g, 1, p), log_step.reshape(g, 1, 1),
      jnp.transpose(b_re, (0, 2, 1)), jnp.transpose(b_im, (0, 2, 1)))
    return out


def _s5_blocks(lbr, lbi, bbr, bbi, c_re, c_im):
    eye = jnp.eye(8, dtype=F32)
    shp = (N_SBLK, 8, SSM_H, SSM_P)

    def in_map(b):
        return (b.reshape(shp)[:, :, :, None, :] * eye[None, :, None, :, None]).reshape(N_SBLK, 128, SBLK)

    def out_map(c):
        ct = jnp.transpose(c.reshape(shp), (0, 1, 3, 2))
        return (ct[:, :, :, None, :] * eye[None, :, None, :, None]).reshape(N_SBLK, SBLK, 128)

    bblk = jnp.concatenate([in_map(bbr), in_map(bbi)], axis=-1).astype(BF16)
    cblk = jnp.concatenate([out_map(c_re), -out_map(c_im)], axis=1).astype(BF16)
    return bblk, lbr.reshape(N_SBLK, 1, SBLK), lbi.reshape(N_SBLK, 1, SBLK), cblk


def _ssm_post(y_lin, u, dskip, wglu, bglu, gs):
    y = _gelu_tanh(y_lin + dskip * u)
    y = y * _sigmoid(_bdot(y, wglu) + bglu)
    return _rms(y, gs)


def _mixer_prompt_kernel(sinks_ref, x_ref, gmix_ref, win_ref, bias_ref, bblk_ref, lamr_ref, lami_ref,
                         cblk_ref, dskip_ref, wglu_ref, bglu_ref, ga_ref, gs_ref, wout_ref,
                         gffn_ref, wrt_ref, br_ref, tri_ref,
                         hmid_ref, klast_ref, vlast_ref, sre_ref, sim_ref, hn_ref, idx_ref, gw_ref, rank_ref, cnt_ref,
                         proj_s, u_s, kprev_s, kprevr_s, vprev_s, vprevr_s, attn_s, ssm_s, utb_s, bu_s, xs_s, ytb_s):
    c = pl.program_id(0)

    @pl.when(c == 0)
    def _():
        zkv = jnp.zeros(kprev_s.shape, BF16)
        kprev_s[...] = zkv
        kprevr_s[...] = zkv
        vprev_s[...] = zkv
        vprevr_s[...] = zkv
        sre_ref[...] = jnp.zeros(sre_ref.shape, F32)
        sim_ref[...] = jnp.zeros(sim_ref.shape, F32)
        cnt_ref[...] = jnp.zeros(cnt_ref.shape, jnp.int32)

    x = x_ref[...].reshape(ROWS, D_MODEL)
    proj = _bdot(_rms(x, gmix_ref[...]), win_ref[...])
    u0 = D_ATTN + 2 * D_KV
    proj_s[...] = proj[:, 0:u0]
    for jb in range(D_SSM // 128):
        for b in range(BATCH):
            u_s[jb, b * PITCH:b * PITCH + CHUNK, :] = proj[b * CHUNK:(b + 1) * CHUNK, u0 + 128 * jb:u0 + 128 * (jb + 1)]

    lo = lax.broadcasted_iota(jnp.int32, (CHUNK, 128), 1) < HEAD_DIM
    hi = jnp.logical_not(lo)
    table = jnp.minimum(c, 1)
    hrow = lax.broadcasted_iota(jnp.int32, (4 * CHUNK, 1), 0) // CHUNK

    def sink_col(heads):
        col = jnp.full((4 * CHUNK, 1), sinks_ref[heads[3]], F32)
        for n in (2, 1, 0):
            col = jnp.where(hrow == n, sinks_ref[heads[n]], col)
        return col

    sink_nat = sink_col(NAT_HEADS)
    sink_rot = sink_col(ROT_HEADS)

    def attend(q, k, v, bias, sink):
        s = lax.dot_general(q, k, (((1,), (1,)), ((), ())), preferred_element_type=F32) + bias
        m = jnp.maximum(jnp.max(s, axis=-1, keepdims=True), sink)
        p = jnp.exp(s - m)
        den = jnp.sum(p, axis=-1, keepdims=True) + jnp.exp(sink - m)
        return jnp.dot(p.astype(BF16), v, preferred_element_type=F32) / den

    def attn_body(b, carry):
        r0 = pl.multiple_of(b * CHUNK, CHUNK)
        rows = pl.ds(r0, CHUNK)
        kb = proj_s[rows, D_ATTN:D_ATTN + D_KV]
        vb = proj_s[rows, D_ATTN + D_KV:D_ATTN + 2 * D_KV]
        kb16 = kb.astype(BF16)
        vb16 = vb.astype(BF16)
        kbr16 = pltpu.roll(kb, HEAD_DIM, 1).astype(BF16)
        vbr16 = pltpu.roll(vb, HEAD_DIM, 1).astype(BF16)
        k_nat = jnp.concatenate([kprev_s[b], kb16], axis=0)
        k_rot = jnp.concatenate([kprevr_s[b], kbr16], axis=0)
        v_nat = jnp.concatenate([vprev_s[b], vb16], axis=0)
        v_rot = jnp.concatenate([vprevr_s[b], vbr16], axis=0)
        q2 = [proj_s[rows, 128 * jq:128 * (jq + 1)] * (HEAD_DIM ** -0.5) for jq in range(N_HEADS // 2)]
        q_nat = jnp.concatenate([jnp.where(lo if h % 2 == 0 else hi, q2[h // 2], 0.0) for h in NAT_HEADS],
                                axis=0).astype(BF16)
        q_rot = jnp.concatenate([jnp.where(lo if h % 2 == 0 else hi, q2[h // 2], 0.0) for h in ROT_HEADS],
                                axis=0).astype(BF16)
        o_nat = attend(q_nat, k_nat, v_nat, bias_ref[table, 0], sink_nat)
        o_rot = attend(q_rot, k_rot, v_rot, bias_ref[table, 1], sink_rot)
        for jq in range(N_HEADS // 2):
            blk = slice(CHUNK * jq, CHUNK * (jq + 1))
            even, odd = (o_nat, o_rot) if jq < 2 else (o_rot, o_nat)
            attn_s[rows, 128 * jq:128 * (jq + 1)] = jnp.where(lo, even[blk], odd[blk])
        kprev_s[b] = kb16
        kprevr_s[b] = kbr16
        vprev_s[b] = vb16
        vprevr_s[b] = vbr16
        return carry

    lax.fori_loop(0, BATCH, attn_body, 0)

    for sc in range(CHUNK // SUB_T):
        t0 = sc * SUB_T
        for i in range(SUB_T):
            for jb in range(D_SSM // 128):
                utb_s[i * BATCH:(i + 1) * BATCH, 128 * jb:128 * (jb + 1)] = (
                    u_s[jb, pl.ds(t0 + i, BATCH, stride=PITCH), :])
        u_tb = utb_s[...]
        for j in range(N_SBLK):
            bu_s[...] = _bdot(u_tb[:, 128 * j:128 * (j + 1)], bblk_ref[j])
            lr = jnp.broadcast_to(lamr_ref[j], (BATCH, SBLK))
            li = jnp.broadcast_to(lami_ref[j], (BATCH, SBLK))
            cols = slice(SBLK * j, SBLK * (j + 1))

            def step(i, carry):
                sr, si = carry
                r = pl.ds(pl.multiple_of(i * BATCH, BATCH), BATCH)
                nr = lr * sr - li * si + bu_s[r, 0:SBLK]
                ni = lr * si + li * sr + bu_s[r, SBLK:2 * SBLK]
                xs_s[r, 0:SBLK] = nr
                xs_s[r, SBLK:2 * SBLK] = ni
                return nr, ni

            sr, si = lax.fori_loop(0, SUB_T, step, (sre_ref[:, cols], sim_ref[:, cols]), unroll=True)
            sre_ref[:, cols] = sr
            sim_ref[:, cols] = si
            ytb_s[:, 128 * j:128 * (j + 1)] = _bdot(xs_s[...], cblk_ref[j])
        yn = _ssm_post(ytb_s[...], u_tb, dskip_ref[...], wglu_ref[...], bglu_ref[...], gs_ref[...])
        for i in range(SUB_T):
            for jb in range(D_SSM // 128):
                ssm_s[jb, pl.ds(t0 + i, BATCH, stride=PITCH), :] = (
                    yn[i * BATCH:(i + 1) * BATCH, 128 * jb:128 * (jb + 1)])

    an = _rms(attn_s[...], ga_ref[...])
    sn = jnp.concatenate(
        [jnp.concatenate([ssm_s[jb, b * PITCH:b * PITCH + CHUNK, :] for b in range(BATCH)], axis=0)
         for jb in range(D_SSM // 128)], axis=1)
    h = x + _bdot(an, wout_ref[0:D_ATTN, :]) + _bdot(sn, wout_ref[D_ATTN:2 * D_ATTN, :])
    hmid_ref[...] = h
    _route(h, gffn_ref, wrt_ref, br_ref, tri_ref, cnt_ref, hn_ref, idx_ref, gw_ref, rank_ref)

    @pl.when(c == N_CHUNKS - 1)
    def _():
        klast_ref[...] = proj_s[:, D_ATTN:D_ATTN + D_KV].reshape(BATCH, CHUNK, D_KV)
        vlast_ref[...] = proj_s[:, D_ATTN + D_KV:D_ATTN + 2 * D_KV].reshape(BATCH, CHUNK, D_KV)


def _mixer_prompt(x_prompt, sinks, gmix, win, bias, bblk, lamr, lami, cblk, dskip, wglu, bglu, ga, gs, wout,
                  gffn, wrt, br):
    smem = pl.BlockSpec(memory_space=pltpu.SMEM)
    in_specs = [
        smem,
        pl.BlockSpec((BATCH, CHUNK, D_MODEL), lambda c: (0, c, 0)),
        _full((1, D_MODEL)), _full((D_MODEL, D_IN)), _full((2, 2, 4 * CHUNK, 2 * CHUNK)),
        _full((N_SBLK, 128, 2 * SBLK)), _full((N_SBLK, 1, SBLK)), _full((N_SBLK, 1, SBLK)),
        _full((N_SBLK, 2 * SBLK, 128)), _full((1, D_SSM)), _full((D_SSM, D_SSM)), _full((1, D_SSM)),
        _full((1, D_ATTN)), _full((1, D_SSM)), _full((D_MODEL, D_MODEL)),
        _full((1, D_MODEL)), _full((N_EXPERTS, D_MODEL)), _full((N_EXPERTS, 1)), _full((ROUTE_BLOCK, ROUTE_BLOCK)),
    ]
    route_specs, route_shapes = _route_specs(ROWS, lambda c: c)
    out_shape = [
        jax.ShapeDtypeStruct((T_PAD, D_MODEL), F32),
        jax.ShapeDtypeStruct((BATCH, CHUNK, D_KV), F32),
        jax.ShapeDtypeStruct((BATCH, CHUNK, D_KV), F32),
        jax.ShapeDtypeStruct((BATCH, SSM_G * SSM_P), F32),
        jax.ShapeDtypeStruct((BATCH, SSM_G * SSM_P), F32),
    ] + route_shapes
    out_specs = [
        pl.BlockSpec((ROWS, D_MODEL), lambda c: (c, 0)),
        _full((BATCH, CHUNK, D_KV)), _full((BATCH, CHUNK, D_KV)),
        _full((BATCH, SSM_G * SSM_P)), _full((BATCH, SSM_G * SSM_P)),
    ] + route_specs
    kv_scr = pltpu.VMEM((BATCH, CHUNK, D_KV), BF16)
    scratch = [
        pltpu.VMEM((ROWS, D_ATTN + 2 * D_KV), F32), pltpu.VMEM((D_SSM // 128, BATCH * PITCH, 128), F32),
        kv_scr, kv_scr, kv_scr, kv_scr,
        pltpu.VMEM((ROWS, D_ATTN), F32), pltpu.VMEM((D_SSM // 128, BATCH * PITCH, 128), F32),
        pltpu.VMEM((SUB_ROWS, D_SSM), F32), pltpu.VMEM((SUB_ROWS, 2 * SBLK), F32),
        pltpu.VMEM((SUB_ROWS, 2 * SBLK), F32), pltpu.VMEM((SUB_ROWS, D_SSM), F32),
    ]
    return pl.pallas_call(
        _mixer_prompt_kernel, grid=(N_CHUNKS,), in_specs=in_specs, out_specs=out_specs, out_shape=out_shape,
        scratch_shapes=scratch, name="mixer_prompt",
        compiler_params=pltpu.CompilerParams(dimension_semantics=("arbitrary",), vmem_limit_bytes=VMEM_LIMIT),
    )(sinks, x_prompt, gmix, win, bias, bblk, lamr, lami, cblk, dskip, wglu, bglu, ga, gs, wout,
      gffn, wrt, br, _tri(ROUTE_BLOCK))


SGRP = 16
SPITCH = DEC_BATCH + 8
N_SGRP = DEC_BATCH // SGRP


def _mixer_sample_kernel(x_ref, gmix_ref, win_ref, ck_ref, cv_ref, sbias_ref, sinkc_ref, bblk_ref, lamr_ref,
                         lami_ref, cblk_ref, dskip_ref, wglu_ref, bglu_ref, ga_ref, gs_ref, wout_ref,
                         x0r_ref, x0i_ref, gffn_ref, wrt_ref, br_ref, tri_ref, cnt_in_ref,
                         hbuf_ref, hnbuf_ref, idxbuf_ref, gwbuf_ref, rankbuf_ref,
                         hmid_ref, knew_ref, vnew_ref, sre_ref, sim_ref, hn_ref, idx_ref, gw_ref, rank_ref, cnt_ref,
                         proj_s, qall_s, oall_s, kn8_s, vn8_s):
    del hbuf_ref, hnbuf_ref, idxbuf_ref, gwbuf_ref, rankbuf_ref
    g = pl.program_id(0)
    lo = lax.broadcasted_iota(jnp.int32, (DEC_BATCH, 128), 1) < HEAD_DIM

    @pl.when(g == 0)
    def _():
        proj = _bdot(_rms(x_ref[...], gmix_ref[...]), win_ref[...])
        proj_s[...] = proj
        for h in range(N_HEADS):
            jq, half, kv = h // 2, h % 2, h // 4
            q2 = proj[:, 128 * jq:128 * (jq + 1)] * (HEAD_DIM ** -0.5)
            if half != kv:
                q2 = pltpu.roll(q2, HEAD_DIM, 1)
            hr = slice(h * SPITCH, h * SPITCH + DEC_BATCH)
            qall_s[hr, :] = jnp.where(lo if kv == 0 else jnp.logical_not(lo), q2, 0.0)
            kn8_s[hr, :] = proj[:, D_ATTN:D_ATTN + D_KV]
            vn8_s[hr, :] = proj[:, D_ATTN + D_KV:D_ATTN + 2 * D_KV]

    def head_rows(ref):
        return jnp.stack([ref[pl.ds(g * SGRP + ii, N_HEADS, stride=SPITCH), :] for ii in range(SGRP)], axis=0)

    qe = head_rows(qall_s)
    knew = head_rows(kn8_s)
    vnew = head_rows(vn8_s)
    s = jnp.einsum('bhd,bkd->bhk', qe.astype(BF16), ck_ref[...].astype(BF16),
                   preferred_element_type=F32) + sbias_ref[...][None]
    s_new = jnp.sum(qe * knew, axis=-1, keepdims=True)
    sink = sinkc_ref[...][None]
    m = jnp.maximum(jnp.maximum(jnp.max(s, axis=-1, keepdims=True), s_new), sink)
    p = jnp.exp(s - m)
    p_new = jnp.exp(s_new - m)
    den = jnp.sum(p, axis=-1, keepdims=True) + p_new + jnp.exp(sink - m)
    o = (jnp.einsum('bhk,bkd->bhd', p.astype(BF16), cv_ref[...].astype(BF16),
                    preferred_element_type=F32) + p_new * vnew) / den
    for ii in range(SGRP):
        oall_s[pl.ds(g * SGRP + ii, N_HEADS, stride=SPITCH), :] = o[ii]

    @pl.when(g == N_SGRP - 1)
    def _():
        proj = proj_s[...]
        blocks = []
        for jq in range(N_HEADS // 2):
            oa = oall_s[(2 * jq) * SPITCH:(2 * jq) * SPITCH + DEC_BATCH, :]
            ob = oall_s[(2 * jq + 1) * SPITCH:(2 * jq + 1) * SPITCH + DEC_BATCH, :]
            if jq // 2 == 0:
                blocks.append(jnp.where(lo, oa, pltpu.roll(ob, HEAD_DIM, 1)))
            else:
                blocks.append(jnp.where(lo, pltpu.roll(oa, HEAD_DIM, 1), ob))
        attn = jnp.concatenate(blocks, axis=1)
        u = proj[:, D_ATTN + 2 * D_KV:]
        ys = []
        for j in range(N_SBLK):
            bu = _bdot(u[:, 128 * j:128 * (j + 1)], bblk_ref[j])
            lr = lamr_ref[j]
            li = lami_ref[j]
            cols = slice(SBLK * j, SBLK * (j + 1))
            sr = x0r_ref[:, cols]
            si = x0i_ref[:, cols]
            nr = lr * sr - li * si + bu[:, 0:SBLK]
            ni = lr * si + li * sr + bu[:, SBLK:2 * SBLK]
            sre_ref[:, cols] = nr
            sim_ref[:, cols] = ni
            ys.append(_bdot(jnp.concatenate([nr, ni], axis=1), cblk_ref[j]))
        sn = _ssm_post(jnp.concatenate(ys, axis=1), u, dskip_ref[...], wglu_ref[...], bglu_ref[...], gs_ref[...])
        an = _rms(attn, ga_ref[...])
        h = x_ref[...] + _bdot(an, wout_ref[0:D_ATTN, :]) + _bdot(sn, wout_ref[D_ATTN:2 * D_ATTN, :])
        h = jnp.concatenate([h, jnp.zeros((DEC_BATCH, D_MODEL), F32)], axis=0)
        hmid_ref[...] = h
        cnt_ref[...] = cnt_in_ref[...]
        _route(h, gffn_ref, wrt_ref, br_ref, tri_ref, cnt_ref, hn_ref, idx_ref, gw_ref, rank_ref)
        knew_ref[...] = proj[:, D_ATTN:D_ATTN + D_KV]
        vnew_ref[...] = proj[:, D_ATTN + D_KV:D_ATTN + 2 * D_KV]


def _mixer_sample(x_s, gmix, win, ck, cv, sbias, sinkc, bblk, lamr, lami, cblk, dskip, wglu, bglu, ga, gs, wout,
                  x0r, x0i, gffn, wrt, br, cnt_in, hbuf, hnbuf, idxbuf, gwbuf, rankbuf):
    nst = SSM_G * SSM_P
    in_specs = [
        _full((DEC_BATCH, D_MODEL)), _full((1, D_MODEL)), _full((D_MODEL, D_IN)),
        pl.BlockSpec((SGRP, WINDOW, D_KV), lambda g: (g, 0, 0)),
        pl.BlockSpec((SGRP, WINDOW, D_KV), lambda g: (g, 0, 0)),
        _full((N_HEADS, WINDOW)), _full((N_HEADS, 1)),
        _full((N_SBLK, 128, 2 * SBLK)), _full((N_SBLK, 1, SBLK)), _full((N_SBLK, 1, SBLK)),
        _full((N_SBLK, 2 * SBLK, 128)), _full((1, D_SSM)), _full((D_SSM, D_SSM)), _full((1, D_SSM)),
        _full((1, D_ATTN)), _full((1, D_SSM)), _full((D_MODEL, D_MODEL)),
        _full((DEC_BATCH, nst)), _full((DEC_BATCH, nst)),
        _full((1, D_MODEL)), _full((N_EXPERTS, D_MODEL)), _full((N_EXPERTS, 1)),
        _full((2 * DEC_BATCH, 2 * DEC_BATCH)), _full((N_EXPERTS, 1)),
    ] + [pl.BlockSpec(memory_space=pl.ANY)] * 5
    tail_block = SAMPLE_ROW0 // (2 * DEC_BATCH)
    route_specs, route_shapes = _route_specs(2 * DEC_BATCH, lambda g: tail_block)
    out_shape = [
        jax.ShapeDtypeStruct((T_PAD, D_MODEL), F32),
        jax.ShapeDtypeStruct((DEC_BATCH, D_KV), F32), jax.ShapeDtypeStruct((DEC_BATCH, D_KV), F32),
        jax.ShapeDtypeStruct((DEC_BATCH, nst), F32), jax.ShapeDtypeStruct((DEC_BATCH, nst), F32),
    ] + route_shapes
    out_specs = [
        pl.BlockSpec((2 * DEC_BATCH, D_MODEL), lambda g: (tail_block, 0)),
        _full((DEC_BATCH, D_KV)), _full((DEC_BATCH, D_KV)),
        _full((DEC_BATCH, nst)), _full((DEC_BATCH, nst)),
    ] + route_specs
    head_rows = pltpu.VMEM((N_HEADS * SPITCH, 128), F32)
    scratch = [pltpu.VMEM((DEC_BATCH, D_IN), F32), head_rows, head_rows, head_rows, head_rows]
    return pl.pallas_call(
        _mixer_sample_kernel, grid=(N_SGRP,), in_specs=in_specs, out_specs=out_specs, out_shape=out_shape,
        scratch_shapes=scratch, input_output_aliases={24: 0, 25: 5, 26: 6, 27: 7, 28: 8}, name="mixer_sample",
        compiler_params=pltpu.CompilerParams(dimension_semantics=("arbitrary",), vmem_limit_bytes=VMEM_LIMIT),
    )(x_s, gmix, win, ck, cv, sbias, sinkc, bblk, lamr, lami, cblk, dskip, wglu, bglu, ga, gs, wout, x0r, x0i,
      gffn, wrt, br, _tri(2 * DEC_BATCH), cnt_in, hbuf, hnbuf, idxbuf, gwbuf, rankbuf)


def _route(h, g_ref, wrt_ref, br_ref, tri_ref, cnt_ref, hn_ref, idx_ref, gw_ref, rank_ref):
    hn = _rms(h, g_ref[...])
    hn_ref[...] = _pack_pairs(hn)
    hn_hi = hn.astype(BF16)
    hn_lo = (hn - hn_hi.astype(F32)).astype(BF16)
    w = wrt_ref[...]
    w_hi = w.astype(BF16)
    w_lo = (w - w_hi.astype(F32)).astype(BF16)
    nt = (((1,), (1,)), ((), ()))
    logits = (lax.dot_general(w_hi, hn_hi, nt, preferred_element_type=F32)
              + lax.dot_general(w_lo, hn_hi, nt, preferred_element_type=F32)
              + lax.dot_general(w_hi, hn_lo, nt, preferred_element_type=F32)) + br_ref[...]
    eidx = lax.broadcasted_iota(jnp.int32, logits.shape, 0)
    vals, onehots = [], []
    l = logits
    for k in range(TOP_K):
        m = jnp.max(l, axis=0, keepdims=True)
        ik = jnp.min(jnp.where(l == m, eidx, N_EXPERTS), axis=0, keepdims=True)
        oh = eidx == ik
        idx_ref[k:k + 1, :] = ik
        vals.append(m)
        onehots.append(oh)
        l = jnp.where(oh, -jnp.inf, l)
    exps = [jnp.exp(v - vals[0]) for v in vals]
    den = exps[0] + exps[1] + exps[2] + exps[3]
    for k in range(TOP_K):
        gw_ref[k:k + 1, :] = exps[k] / den
    member = jnp.zeros(logits.shape, F32)
    for oh in onehots:
        member = member + jnp.where(oh, 1.0, 0.0)
    wblk = tri_ref.shape[0]
    base = cnt_ref[...].astype(F32)
    befores = []
    for cb in range(h.shape[0] // wblk):
        mblk = member[:, cb * wblk:(cb + 1) * wblk]
        befores.append(jnp.dot(mblk.astype(BF16), tri_ref[...], preferred_element_type=F32) + base)
        base = base + jnp.sum(mblk, axis=1, keepdims=True)
    before = jnp.concatenate(befores, axis=1)
    for k in range(TOP_K):
        rank_ref[k:k + 1, :] = jnp.sum(jnp.where(onehots[k], before, 0.0), axis=0, keepdims=True).astype(jnp.int32)
    cnt_ref[...] = base.astype(jnp.int32)


def _route_specs(rows, block):
    specs = [pl.BlockSpec((rows, D_PACK), lambda i: (block(i), 0)), pl.BlockSpec((TOP_K, rows), lambda i: (0, block(i))),
             pl.BlockSpec((TOP_K, rows), lambda i: (0, block(i))), pl.BlockSpec((TOP_K, rows), lambda i: (0, block(i))),
             _full((N_EXPERTS, 1))]
    shapes = [jax.ShapeDtypeStruct((T_PAD, D_PACK), jnp.int32), jax.ShapeDtypeStruct((TOP_K, T_PAD), jnp.int32),
              jax.ShapeDtypeStruct((TOP_K, T_PAD), F32), jax.ShapeDtypeStruct((TOP_K, T_PAD), jnp.int32),
              jax.ShapeDtypeStruct((N_EXPERTS, 1), jnp.int32)]
    return specs, shapes


def _tri(n):
    return jnp.asarray(np.triu(np.ones((n, n), np.float32), 1), BF16)


def _place_kernel(off_ref, idx_ref, rank_ref, pos_ref):
    idx = idx_ref[...]
    pos = rank_ref[...]
    for e in range(N_EXPERTS):
        pos = pos + jnp.where(idx == e, off_ref[e], 0)
    pos_ref[...] = pos


def _place(offsets, idx, rank):
    return pl.pallas_call(
        _place_kernel,
        in_specs=[pl.BlockSpec(memory_space=pltpu.SMEM), pl.BlockSpec(memory_space=pltpu.VMEM),
                  pl.BlockSpec(memory_space=pltpu.VMEM)],
        out_specs=pl.BlockSpec(memory_space=pltpu.VMEM),
        out_shape=jax.ShapeDtypeStruct((TOP_K, T_PAD), jnp.int32), name="place",
    )(offsets, idx, rank)


def _sc_mesh():
    return plsc.VectorSubcoreMesh(core_axis_name="core", subcore_axis_name="subcore")


def _sc_dispatch(rows, pos):
    n, d = rows.shape
    nblk = n // SC_ROWS
    pos_w = pos.reshape(TOP_K, nblk, SC_ROWS).transpose(1, 0, 2)

    @functools.partial(pl.kernel, out_type=jax.ShapeDtypeStruct((P_ALLOC, d), rows.dtype), mesh=_sc_mesh(),
                       scratch_types=[], name="dispatch")
    def run(x_hbm, i_hbm, o_hbm):
        def body(x_vmem, i_vmem):
            for k in range(TOP_K):
                pltpu.sync_copy(x_vmem, o_hbm.at[i_vmem.at[0, k]])

        pltpu.emit_pipeline(
            body, grid=(nblk,),
            in_specs=[pl.BlockSpec((SC_ROWS, d), lambda i: (i, 0)),
                      pl.BlockSpec((1, TOP_K, SC_ROWS), lambda i: (i, 0, 0))],
            out_specs=[], core_axis_name=("core", "subcore"), dimension_semantics=(pltpu.PARALLEL,),
        )(x_hbm, i_hbm)

    return run(rows, pos_w)


def _sc_build_dest(flat_pos, flat_ids, after):
    n = flat_pos.shape[0]
    stage = n // 32
    lanes = 16
    unroll = 5
    params = pltpu.CompilerParams(needs_layout_passes=False)

    @functools.partial(pl.kernel, out_type=jax.ShapeDtypeStruct((P_ALLOC,), jnp.int32), mesh=_sc_mesh(),
                       scratch_types=[pltpu.VMEM((DEST_SLICE,), jnp.int32), pltpu.VMEM((stage,), jnp.int32),
                                      pltpu.VMEM((stage,), jnp.int32)],
                       compiler_params=params, name="build_dest")
    def run(pos_hbm, ids_hbm, after_hbm, o_hbm, local, pbuf, ibuf):
        del after_hbm
        lo = (lax.axis_index("core") * 16 + lax.axis_index("subcore")) * DEST_SLICE

        @pl.loop(0, DEST_SLICE, step=lanes)
        def _(i):
            local[pl.ds(i, lanes)] = TRASH_ROW + ((lo + i + lax.iota(jnp.int32, lanes)) & (N_TRASH - 1))

        @pl.loop(0, n // stage)
        def _(c):
            pltpu.sync_copy(pos_hbm.at[pl.ds(c * stage, stage)], pbuf)
            pltpu.sync_copy(ids_hbm.at[pl.ds(c * stage, stage)], ibuf)

            @pl.loop(0, stage, step=lanes * unroll)
            def _(j0):
                for u in range(unroll):
                    j = j0 + u * lanes
                    rel = pbuf[pl.ds(j, lanes)] - lo
                    plsc.store_scatter(local, [rel], ibuf[pl.ds(j, lanes)], mask=(rel >= 0) & (rel < DEST_SLICE))

        pltpu.sync_copy(local, o_hbm.at[pl.ds(lo, DEST_SLICE)])

    return run(flat_pos, flat_ids, after)


def _sc_combine_scatter(rows, dest, yg_prev, chunk):
    w = SC_ROWS_COMBINE
    nblk = (CHUNK_BOUNDS[chunk + 1] - CHUNK_BOUNDS[chunk]) // w
    b0 = CHUNK_BOUNDS[chunk] // w
    dest_w = dest.reshape(P_ALLOC // w, 1, w)

    def body_fn(x_hbm, i_hbm, *rest):
        o_hbm = rest[-1]

        def body(x_vmem, i_vmem):
            pltpu.sync_copy(x_vmem, o_hbm.at[i_vmem.at[0, 0]])

        pltpu.emit_pipeline(
            body, grid=(nblk,),
            in_specs=[pl.BlockSpec((w, D_PACK), lambda i: (b0 + i, 0)),
                      pl.BlockSpec((1, 1, w), lambda i: (b0 + i, 0, 0))],
            out_specs=[], core_axis_name=("core", "subcore"), dimension_semantics=(pltpu.PARALLEL,),
        )(x_hbm, i_hbm)

    out_type = jax.ShapeDtypeStruct((YG_ROWS, D_PACK), rows.dtype)
    if yg_prev is None:
        return pl.kernel(body_fn, out_type=out_type, mesh=_sc_mesh(), scratch_types=[],
                         name="combine")(rows, dest_w)
    return mpmd._mpmd_map([(_sc_mesh(), body_fn)], out_type, input_output_aliases={2: 0}, scratch_types=(),
                          name="combine")(rows, dest_w, yg_prev)


def _experts_kernel(ts_ref, trow_ref, tsz_ref, rng_ref, xs_hbm, w1_hbm, b1_hbm, w2_hbm, b2_hbm, *rest, windows):
    if windows:
        (ck_hbm, cv_hbm, kn_hbm, vn_hbm, ys_hbm, kw_hbm, vw_hbm,
         w1_st, w2_st, b1_st, b2_st, w1_s, w2_s, xbuf, ybuf, wsem, xsem, ysem, csem) = rest
        keep = WINDOW - 1
        window_copies = [
            pltpu.make_async_copy(ck_hbm.at[:, pl.ds(1, keep)], kw_hbm.at[:, pl.ds(0, keep)], csem.at[0]),
            pltpu.make_async_copy(cv_hbm.at[:, pl.ds(1, keep)], vw_hbm.at[:, pl.ds(0, keep)], csem.at[1]),
            pltpu.make_async_copy(kn_hbm, kw_hbm.at[:, pl.ds(keep, 1)], csem.at[2]),
            pltpu.make_async_copy(vn_hbm, vw_hbm.at[:, pl.ds(keep, 1)], csem.at[3]),
        ]
        for c in window_copies:
            c.start()
    else:
        ys_hbm, w1_st, w2_st, b1_st, b2_st, w1_s, w2_s, xbuf, ybuf, wsem, xsem, ysem = rest
        window_copies = []
    elo, ehi, tlo, thi = rng_ref[0], rng_ref[1], rng_ref[2], rng_ref[3]

    def w_copies(ex, slot):
        return (pltpu.make_async_copy(w1_hbm.at[ex], w1_st.at[slot], wsem.at[0, slot]),
                pltpu.make_async_copy(w2_hbm.at[ex], w2_st.at[slot], wsem.at[1, slot]),
                pltpu.make_async_copy(b1_hbm.at[ex], b1_st.at[slot], wsem.at[2, slot]),
                pltpu.make_async_copy(b2_hbm.at[ex], b2_st.at[slot], wsem.at[3, slot]))

    def x_copy(t, slot, rows):
        src = xs_hbm.at[pl.ds(pl.multiple_of(trow_ref[t], TM_UNIT), rows)]
        return pltpu.make_async_copy(src, xbuf.at[slot, pl.ds(0, rows)], xsem.at[slot])

    def y_copy(t, slot, rows):
        dst = ys_hbm.at[pl.ds(pl.multiple_of(trow_ref[t], TM_UNIT), rows)]
        return pltpu.make_async_copy(ybuf.at[slot, pl.ds(0, rows)], dst, ysem.at[slot])

    def by_size(t, fn):
        for rows in TILE_SIZES:
            @pl.when(tsz_ref[t] == rows)
            def _():
                fn(rows)

    @pl.when(tlo < thi)
    def _():
        for c in w_copies(elo, 0):
            c.start()
        by_size(tlo, lambda rows: x_copy(tlo, tlo % 2, rows).start())

    def expert(e, carry):
        wslot = (e - elo) % 2

        @pl.when(e + 1 < ehi)
        def _():
            for c in w_copies(e + 1, 1 - wslot):
                c.start()

        for c in w_copies(e, wslot):
            c.wait()
        for r in range(4):
            rs = slice(256 * r, 256 * (r + 1))
            w1_s[rs, :] = w1_st[wslot, rs, :].astype(BF16)
            w2_s[rs, :] = w2_st[wslot, rs, :].astype(BF16)
        b1 = b1_st[wslot]
        b2 = b2_st[wslot]

        def tile(t, carry):
            slot = t % 2

            @pl.when(t + 1 < thi)
            def _():
                by_size(t + 1, lambda rows: x_copy(t + 1, 1 - slot, rows).start())

            by_size(t, lambda rows: x_copy(t, slot, rows).wait())

            @pl.when(t - 2 >= tlo)
            def _():
                by_size(t - 2, lambda rows: y_copy(t - 2, slot, rows).wait())

            def compute(rows):
                hdn = _bdot(_unpack_pairs(xbuf[slot, 0:rows]), w1_s[...]) + b1
                gt = jnp.minimum(hdn[:, :D_FF], SWIGLU_LIMIT)
                up = jnp.clip(hdn[:, D_FF:], -SWIGLU_LIMIT, SWIGLU_LIMIT)
                act = (up + 1.0) * gt * _sigmoid(SWIGLU_ALPHA * gt)
                ybuf[slot, 0:rows] = _pack_pairs(_bdot(act, w2_s[...]) + b2)
                y_copy(t, slot, rows).start()

            by_size(t, compute)
            return carry

        lax.fori_loop(jnp.maximum(ts_ref[e], tlo), jnp.minimum(ts_ref[e + 1], thi), tile, 0)
        return carry

    lax.fori_loop(elo, ehi, expert, 0)

    @pl.when(thi - tlo >= 2)
    def _():
        by_size(thi - 2, lambda rows: y_copy(thi - 2, thi % 2, rows).wait())

    @pl.when(thi - tlo >= 1)
    def _():
        by_size(thi - 1, lambda rows: y_copy(thi - 1, (thi - 1) % 2, rows).wait())

    for c in window_copies:
        c.wait()


def _tile_plan(counts):
    units = (counts + (TM_UNIT - 1)) // TM_UNIT
    unit_end = jnp.cumsum(units)
    offsets = ((unit_end - units) * TM_UNIT).astype(jnp.int32)
    per_full = TM // TM_UNIT
    n_full = units // per_full
    rest = units % per_full
    has_mid = (rest >= 2).astype(jnp.int32)
    tiles_per = n_full + has_mid + rest % 2
    tile_end = jnp.cumsum(tiles_per)
    tile_start = jnp.concatenate([jnp.zeros((1,), jnp.int32), tile_end.astype(jnp.int32)])
    t = jnp.arange(MAX_TILES, dtype=jnp.int32)[:, None]
    done = tile_end[None, :] <= t
    mine = jnp.sum(done.astype(jnp.int32), axis=1, keepdims=True) == jnp.arange(N_EXPERTS, dtype=jnp.int32)[None, :]
    pick = lambda v: jnp.sum(jnp.where(mine, v[None, :], 0), axis=1)
    j = t[:, 0] - jnp.max(jnp.where(done, tile_end[None, :], 0), axis=1)
    nf, mid = pick(n_full), pick(has_mid)
    rows = jnp.where(j < nf, TM, jnp.where((j == nf) & (mid == 1), TILE_SIZES[1], TILE_SIZES[2]))
    row0 = pick(offsets) + jnp.minimum(j, nf) * TM + jnp.where(j > nf, TILE_SIZES[1], 0)
    valid = t[:, 0] < tile_end[-1]
    tile_row = jnp.where(valid, row0, 0).astype(jnp.int32)
    tile_rows = jnp.where(valid, rows, 0).astype(jnp.int32)
    return offsets, tile_start, tile_row, tile_rows


def _chunk_ranges(tile_start, tile_row, tile_rows):
    valid = tile_rows > 0
    tile_end = tile_start[1:]
    out = []
    for c in range(len(CHUNK_UNITS)):
        r0, r1 = CHUNK_BOUNDS[c], CHUNK_BOUNDS[c + 1]
        tlo = jnp.sum((valid & (tile_row + tile_rows <= r0)).astype(jnp.int32))
        thi = jnp.sum((valid & (tile_row < r1)).astype(jnp.int32))
        elo = jnp.sum((tile_end <= tlo).astype(jnp.int32))
        ehi = jnp.where(thi > tlo, jnp.sum((tile_end <= thi - 1).astype(jnp.int32)) + 1, elo)
        out.append(jnp.stack([elo, ehi, tlo, thi]).astype(jnp.int32))
    return out


def _experts(tile_start, tile_row, tile_rows, chunk_range, xs, w_up, b_up, w_down, b_down, windows=None):
    hbm = pl.BlockSpec(memory_space=pl.ANY)
    scratch = [pltpu.VMEM((2, D_MODEL, 2 * D_FF), F32), pltpu.VMEM((2, D_FF, D_MODEL), F32),
               pltpu.VMEM((2, 1, 2 * D_FF), F32), pltpu.VMEM((2, 1, D_MODEL), F32),
               pltpu.VMEM((D_MODEL, 2 * D_FF), BF16), pltpu.VMEM((D_FF, D_MODEL), BF16),
               pltpu.VMEM((2, TM, D_PACK), jnp.int32), pltpu.VMEM((2, TM, D_PACK), jnp.int32),
               pltpu.SemaphoreType.DMA((4, 2)), pltpu.SemaphoreType.DMA((2,)), pltpu.SemaphoreType.DMA((2,))]
    out_shape = [jax.ShapeDtypeStruct((P_ALLOC, D_PACK), jnp.int32)]
    extra = ()
    if windows is not None:
        extra = tuple(windows)
        out_shape += [jax.ShapeDtypeStruct(windows[0].shape, F32), jax.ShapeDtypeStruct(windows[1].shape, F32)]
        scratch.append(pltpu.SemaphoreType.DMA((4,)))
    grid_spec = pltpu.PrefetchScalarGridSpec(
        num_scalar_prefetch=4, grid=(1,), in_specs=[hbm] * (5 + len(extra)), out_specs=[hbm] * len(out_shape),
        scratch_shapes=scratch)
    out = pl.pallas_call(
        functools.partial(_experts_kernel, windows=windows is not None), grid_spec=grid_spec, out_shape=out_shape,
        name="experts",
        compiler_params=pltpu.CompilerParams(dimension_semantics=("arbitrary",), vmem_limit_bytes=VMEM_LIMIT),
    )(tile_start, tile_row, tile_rows, chunk_range, xs, w_up, b_up.reshape(N_EXPERTS, 1, 2 * D_FF), w_down,
      b_down.reshape(N_EXPERTS, 1, D_MODEL), *extra)
    return out if windows is not None else out[0]


def _final_kernel(h_ref, yg0_ref, yg1_ref, yg2_ref, yg3_ref, gw_ref, p_ref, gple_ref, wg_ref, wp_ref, gfin_ref,
                  out_ref):
    rows = h_ref.shape[0]
    gw = gw_ref[...]
    h = h_ref[...]
    for k, yg_ref in enumerate((yg0_ref, yg1_ref, yg2_ref, yg3_ref)):
        h = h + gw[:, k:k + 1] * _unpack_pairs(yg_ref[...])
    gate = _sigmoid(_bdot(_rms(h, gple_ref[...]), wg_ref[...]))
    h = h + gate * _bdot(p_ref[...].reshape(rows, PLE_DIM), wp_ref[...])
    out_ref[...] = _rms(h, gfin_ref[...]).reshape(out_ref.shape)


def _final_prompt(hmid, yg, gwt, p_prompt, gple, wg, wp, gfin):
    nb = 4
    rows = nb * CHUNK
    nbh = BATCH // nb
    rb = lambda c, b: (c * nbh + b, 0)
    slot = lambda k: pl.BlockSpec((rows, D_PACK), lambda c, b: (k * (K_STRIDE // rows) + c * nbh + b, 0))
    return pl.pallas_call(
        _final_kernel, grid=(N_CHUNKS, nbh),
        in_specs=[pl.BlockSpec((rows, D_MODEL), rb), slot(0), slot(1), slot(2), slot(3),
                  pl.BlockSpec((rows, TOP_K), rb),
                  pl.BlockSpec((nb, CHUNK, PLE_DIM), lambda c, b: (b, c, 0)),
                  _full((1, D_MODEL)), _full((D_MODEL, D_MODEL)), _full((PLE_DIM, D_MODEL)), _full((1, D_MODEL))],
        out_specs=pl.BlockSpec((nb, CHUNK, D_MODEL), lambda c, b: (b, c, 0)),
        out_shape=jax.ShapeDtypeStruct((BATCH, SEQ, D_MODEL), F32), name="final_prompt",
        compiler_params=pltpu.CompilerParams(dimension_semantics=("arbitrary", "arbitrary"),
                                             vmem_limit_bytes=VMEM_LIMIT),
    )(hmid, yg, yg, yg, yg, gwt, p_prompt, gple, wg, wp, gfin)


def _final_sample(hmid, yg, gwt, p_sample, gple, wg, wp, gfin):
    blk = SAMPLE_ROW0 // DEC_BATCH
    slot = lambda k: pl.BlockSpec((DEC_BATCH, D_PACK), lambda i: (k * (K_STRIDE // DEC_BATCH) + blk, 0))
    return pl.pallas_call(
        _final_kernel, grid=(1,),
        in_specs=[pl.BlockSpec((DEC_BATCH, D_MODEL), lambda i: (blk, 0)), slot(0), slot(1), slot(2), slot(3),
                  pl.BlockSpec((DEC_BATCH, TOP_K), lambda i: (blk, 0)),
                  _full((DEC_BATCH, PLE_DIM)),
                  _full((1, D_MODEL)), _full((D_MODEL, D_MODEL)), _full((PLE_DIM, D_MODEL)), _full((1, D_MODEL))],
        out_specs=_full((DEC_BATCH, D_MODEL)),
        out_shape=jax.ShapeDtypeStruct((DEC_BATCH, D_MODEL), F32), name="final_sample",
        compiler_params=pltpu.CompilerParams(dimension_semantics=("arbitrary",), vmem_limit_bytes=VMEM_LIMIT),
    )(hmid, yg, yg, yg, yg, gwt, p_sample, gple, wg, wp, gfin)


def _alibi_tables():
    slopes = 2.0 ** (-8.0 * (np.arange(N_HEADS, dtype=np.float64) + 1.0) / N_HEADS)
    i = np.arange(CHUNK)[:, None]
    j = np.arange(2 * CHUNK)[None, :]
    dist = i + CHUNK - j
    valid = (dist >= 0) & (dist <= WINDOW)
    tabs = []
    for has_prev in (False, True):
        ok = valid & ((j >= CHUNK) | has_prev)
        tabs.append(np.where(ok[None], -slopes[:, None, None] * dist[None], NEG))
    prompt = np.stack([np.stack([np.concatenate([t[h] for h in grp], axis=0) for grp in (NAT_HEADS, ROT_HEADS)])
                       for t in tabs]).astype(np.float32)
    wb = min(WINDOW, PAST_LEN)
    sample = (-slopes[:, None] * (wb - np.arange(wb))[None, :]).astype(np.float32)
    return prompt, sample


def kernel(x_prompt, x_sample, cache_k_win, cache_v_win, state_ssm_re, state_ssm_im, p_prompt, p_sample, norm_mix, w_in, sinks, ssm_lam_re, ssm_lam_im, ssm_log_step, ssm_b_re, ssm_b_im, ssm_c_re, ssm_c_im, ssm_d, w_glu, b_glu, norm_attn_out, norm_ssm_out, w_out, norm_ffn, w_router, b_router, w_up, b_up, w_down, b_down, norm_ple, w_ple_gate, w_ple_proj, norm_final):
    nst = SSM_G * SSM_P
    bias_np, sbias_np = _alibi_tables()
    bias = jnp.asarray(bias_np)
    sbias = jnp.asarray(sbias_np)

    lbr, lbi, bbr, bbi = _prep(ssm_lam_re[0], ssm_lam_im[0], ssm_log_step[0], ssm_b_re[0], ssm_b_im[0])
    bblk, lamr, lami, cblk = _s5_blocks(lbr, lbi, bbr, bbi, ssm_c_re[0], ssm_c_im[0])

    gmix = norm_mix[0].reshape(1, D_MODEL)
    win = w_in[0].astype(BF16)
    dskip = ssm_d[0].reshape(1, D_SSM)
    wglu = w_glu[0].astype(BF16)
    bglu = b_glu[0].reshape(1, D_SSM)
    ga = norm_attn_out[0].reshape(1, D_ATTN)
    gs = norm_ssm_out[0].reshape(1, D_SSM)
    wout = w_out[0].astype(BF16)
    sink = sinks[0]

    gffn = norm_ffn[0].reshape(1, D_MODEL)
    wrt = w_router[0].T
    br = b_router[0].reshape(N_EXPERTS, 1)
    hbuf, k_last, v_last, re_p, im_p, hnbuf, idxbuf, gwbuf, rankbuf, cnt_p = _mixer_prompt(
        x_prompt, sink, gmix, win, bias, bblk, lamr, lami, cblk, dskip, wglu, bglu, ga, gs, wout, gffn, wrt, br)

    ck = cache_k_win[0].reshape(DEC_BATCH, WINDOW, D_KV)
    cv = cache_v_win[0].reshape(DEC_BATCH, WINDOW, D_KV)
    hmid, k_new, v_new, re_s, im_s, hn, idx, gw, rank, counts = _mixer_sample(
        x_sample.reshape(DEC_BATCH, D_MODEL), gmix, win, ck, cv, sbias, sink.reshape(N_HEADS, 1), bblk, lamr, lami,
        cblk, dskip, wglu, bglu, ga, gs, wout, state_ssm_re[0].reshape(DEC_BATCH, nst),
        state_ssm_im[0].reshape(DEC_BATCH, nst), gffn, wrt, br, cnt_p, hbuf, hnbuf, idxbuf, gwbuf, rankbuf)

    offsets, tile_start, tile_row, tile_rows = _tile_plan(counts[:, 0])

    pos = _place(offsets, idx, rank)
    xs = _sc_dispatch(hn, pos)
    pair_ids = (jnp.arange(TOP_K, dtype=jnp.int32)[:, None] * K_STRIDE + jnp.arange(T_PAD, dtype=jnp.int32)[None, :])
    dest = _sc_build_dest(pos.reshape(TOP_K * T_PAD), pair_ids.reshape(TOP_K * T_PAD), xs)
    yg = None
    windows = (cache_k_win[0], cache_v_win[0], k_new.reshape(DEC_BATCH, 1, N_KV, HEAD_DIM),
               v_new.reshape(DEC_BATCH, 1, N_KV, HEAD_DIM))
    for c, chunk_range in enumerate(_chunk_ranges(tile_start, tile_row, tile_rows)):
        ys = _experts(tile_start, tile_row, tile_rows, chunk_range, xs, w_up[0], b_up[0], w_down[0], b_down[0],
                      windows if c == 0 else None)
        if c == 0:
            ys, k_win_s, v_win_s = ys
        yg = _sc_combine_scatter(ys, dest, yg, c)

    gwt = gw.T
    gple = norm_ple[0].reshape(1, D_MODEL)
    wg = w_ple_gate[0].astype(BF16)
    wp = w_ple_proj[0].astype(BF16)
    gfin = norm_final.reshape(1, D_MODEL)
    y_prompt = _final_prompt(hmid, yg, gwt, p_prompt[0], gple, wg, wp, gfin)
    y_sample = _final_sample(hmid, yg, gwt, p_sample[0].reshape(DEC_BATCH, PLE_DIM), gple, wg, wp, gfin)

    kv5 = (1, BATCH, CHUNK, N_KV, HEAD_DIM)
    skv5 = (1, DEC_BATCH, WINDOW, N_KV, HEAD_DIM)
    return (y_prompt, y_sample.reshape(DEC_BATCH, 1, D_MODEL),
            k_last.reshape(kv5), v_last.reshape(kv5),
            re_p.reshape(1, BATCH, SSM_G, SSM_P), im_p.reshape(1, BATCH, SSM_G, SSM_P),
            k_win_s.reshape(skv5), v_win_s.reshape(skv5),
            re_s.reshape(1, DEC_BATCH, SSM_G, SSM_P), im_s.reshape(1, DEC_BATCH, SSM_G, SSM_P))
```

```python
import functools

import numpy as np
import jax
import jax.numpy as jnp
from jax import lax
from jax.experimental import pallas as pl
from jax.experimental.pallas import tpu as pltpu
from jax.experimental.pallas import tpu_sc as plsc

F32 = jnp.float32
BF16 = jnp.bfloat16

D_MODEL = 1024
BATCH = 8
SEQ = 2048
DEC_BATCH = 128
PAST_LEN = 16384
HEAD_DIM = 64
D_ATTN = 512
N_HEADS = 8
N_KV = 2
D_KV = N_KV * HEAD_DIM
WINDOW = 128
D_SSM = 512
SSM_H = 16
SSM_G = 32
SSM_P = 64
D_IN = D_ATTN + 2 * D_KV + D_SSM
N_EXPERTS = 32
TOP_K = 4
D_FF = 1024
SWIGLU_LIMIT = 7.0
SWIGLU_ALPHA = 1.702
PLE_DIM = 256
EPS = 1e-5
NEG = -1e30

CHUNK = 128
N_CHUNKS = SEQ // CHUNK
ROWS = BATCH * CHUNK
PITCH = CHUNK + 8
NAT_HEADS = (0, 2, 5, 7)
ROT_HEADS = (1, 3, 4, 6)
ATTN_UNROLL = 2
SUB_T = 32
SUB_ROWS = SUB_T * BATCH
N_SBLK = 4
SBLK = 512
T_REAL = BATCH * SEQ + DEC_BATCH
T_PAD = T_REAL + 128
SAMPLE_ROW0 = BATCH * SEQ
ROUTE_BLOCK = 512
TILE_SIZES = (1024, 512, 256)
TM = TILE_SIZES[0]
TM_UNIT = TILE_SIZES[-1]
P_ROWS = (T_PAD * TOP_K + N_EXPERTS * (TM_UNIT - 1)) // TM_UNIT * TM_UNIT
MAX_TILES = P_ROWS // TM + 2 * N_EXPERTS + 1
SC_ROWS = 40
SC_ROWS_COMBINE = 32
N_PARTS = 4
PART_CHUNKS = N_CHUNKS // N_PARTS
D_PACK = D_MODEL // 2
VMEM_LIMIT = 56 * 1024 * 1024


def _rms(x, g):
    return x * lax.rsqrt(jnp.mean(x * x, axis=-1, keepdims=True) + EPS) * g


def _sigmoid(x):
    return 1.0 / (1.0 + jnp.exp(-x))


def _gelu_tanh(x):
    c = np.float32(np.sqrt(2.0 / np.pi))
    return 0.5 * x * (1.0 + jnp.tanh(c * (x + 0.044715 * (x * x * x))))


def _bdot(a, b):
    return jnp.dot(a.astype(BF16), b, preferred_element_type=F32)


def _pack_pairs(x):
    n = x.shape[1] // 2
    lo = lax.bitcast_convert_type(x[:, :n].astype(BF16).astype(F32), jnp.int32)
    hi = lax.bitcast_convert_type(x[:, n:].astype(BF16).astype(F32), jnp.int32)
    return lax.shift_right_logical(lo, 16) | hi


def _unpack_pairs(w):
    lo = lax.bitcast_convert_type(lax.shift_left(w, 16), F32)
    hi = lax.bitcast_convert_type(w & jnp.int32(-65536), F32)
    return jnp.concatenate([lo, hi], axis=1)


def _full(shape):
    n = len(shape)
    return pl.BlockSpec(shape, lambda *_: (0,) * n)


def _prep_kernel(lr_ref, li_ref, ls_ref, br_ref, bi_ref, lbr_ref, lbi_ref, bbr_ref, bbi_ref):
    lr = lr_ref[...]
    li = li_ref[...]
    step = jnp.exp(ls_ref[...])
    zr = lr * step
    zi = li * step
    mag = jnp.exp(zr)
    lbr = mag * jnp.cos(zi)
    lbi = mag * jnp.sin(zi)
    lbr_ref[...] = lbr
    lbi_ref[...] = lbi
    nr = lbr - 1.0
    den = lr * lr + li * li
    cr = (nr * lr + lbi * li) / den
    ci = (lbi * lr - nr * li) / den
    br = br_ref[...]
    bi = bi_ref[...]
    bbr_ref[...] = cr * br - ci * bi
    bbi_ref[...] = cr * bi + ci * br


def _prep(lam_re, lam_im, log_step, b_re, b_im):
    g, p, h = SSM_G, SSM_P, SSM_H
    out = pl.pallas_call(
        _prep_kernel,
        out_shape=[jax.ShapeDtypeStruct((g, 1, p), F32), jax.ShapeDtypeStruct((g, 1, p), F32),
                   jax.ShapeDtypeStruct((g, h, p), F32), jax.ShapeDtypeStruct((g, h, p), F32)],
        name="s5_prep",
    )(lam_re.reshape(g, 1, p), lam_im.reshape(g, 1, p), log_step.reshape(g, 1, 1),
      jnp.transpose(b_re, (0, 2, 1)), jnp.transpose(b_im, (0, 2, 1)))
    return out


def _s5_blocks(lbr, lbi, bbr, bbi, c_re, c_im):
    eye = jnp.eye(8, dtype=F32)
    shp = (N_SBLK, 8, SSM_H, SSM_P)

    def in_map(b):
        return (b.reshape(shp)[:, :, :, None, :] * eye[None, :, None, :, None]).reshape(N_SBLK, 128, SBLK)

    def out_map(c):
        ct = jnp.transpose(c.reshape(shp), (0, 1, 3, 2))
        return (ct[:, :, :, None, :] * eye[None, :, None, :, None]).reshape(N_SBLK, SBLK, 128)

    bblk = jnp.concatenate([in_map(bbr), in_map(bbi)], axis=-1).astype(BF16)
    cblk = jnp.concatenate([out_map(c_re), -out_map(c_im)], axis=1).astype(BF16)
    return bblk, lbr.reshape(N_SBLK, 1, SBLK), lbi.reshape(N_SBLK, 1, SBLK), cblk


def _ssm_post(y_lin, u, dskip, wglu, bglu, gs):
    y = _gelu_tanh(y_lin + dskip * u)
    y = y * _sigmoid(_bdot(y, wglu) + bglu)
    return _rms(y, gs)


def _mixer_prompt_kernel(sinks_ref, x_ref, gmix_ref, win_ref, bias_ref, bblk_ref, lamr_ref, lami_ref,
                         cblk_ref, dskip_ref, wglu_ref, bglu_ref, ga_ref, gs_ref, wout_ref,
                         gffn_ref, wrt_ref, br_ref, tri_ref,
                         hmid_ref, klast_ref, vlast_ref, sre_ref, sim_ref, hn_ref, idx_ref, gw_ref, rank_ref, cnt_ref,
                         proj_s, u_s, kprev_s, kprevr_s, vprev_s, vprevr_s, attn_s, ssm_s, utb_s, bu_s, xs_s, ytb_s):
    c = pl.program_id(0)

    @pl.when(c == 0)
    def _():
        zkv = jnp.zeros(kprev_s.shape, BF16)
        kprev_s[...] = zkv
        kprevr_s[...] = zkv
        vprev_s[...] = zkv
        vprevr_s[...] = zkv
        sre_ref[...] = jnp.zeros(sre_ref.shape, F32)
        sim_ref[...] = jnp.zeros(sim_ref.shape, F32)
        cnt_ref[...] = jnp.zeros(cnt_ref.shape, jnp.int32)

    x = x_ref[...].reshape(ROWS, D_MODEL)
    proj = _bdot(_rms(x, gmix_ref[...]), win_ref[...])
    u0 = D_ATTN + 2 * D_KV
    proj_s[...] = proj[:, 0:u0]
    for jb in range(D_SSM // 128):
        for b in range(BATCH):
            u_s[jb, b * PITCH:b * PITCH + CHUNK, :] = proj[b * CHUNK:(b + 1) * CHUNK, u0 + 128 * jb:u0 + 128 * (jb + 1)]

    lo = lax.broadcasted_iota(jnp.int32, (CHUNK, 128), 1) < HEAD_DIM
    hi = jnp.logical_not(lo)
    table = jnp.minimum(c, 1)
    hrow = lax.broadcasted_iota(jnp.int32, (4 * CHUNK, 1), 0) // CHUNK

    def sink_col(heads):
        col = jnp.full((4 * CHUNK, 1), sinks_ref[heads[3]], F32)
        for n in (2, 1, 0):
            col = jnp.where(hrow == n, sinks_ref[heads[n]], col)
        return col

    sink_nat = sink_col(NAT_HEADS)
    sink_rot = sink_col(ROT_HEADS)

    def attend(q, k, v, bias, sink):
        s = lax.dot_general(q, k, (((1,), (1,)), ((), ())), preferred_element_type=F32) + bias
        m = jnp.maximum(jnp.max(s, axis=-1, keepdims=True), sink)
        p = jnp.exp(s - m)
        den = jnp.sum(p, axis=-1, keepdims=True) + jnp.exp(sink - m)
        return jnp.dot(p.astype(BF16), v, preferred_element_type=F32) / den

    def attn_body(b, carry):
        r0 = pl.multiple_of(b * CHUNK, CHUNK)
        rows = pl.ds(r0, CHUNK)
        kb = proj_s[rows, D_ATTN:D_ATTN + D_KV]
        vb = proj_s[rows, D_ATTN + D_KV:D_ATTN + 2 * D_KV]
        kb16 = kb.astype(BF16)
        vb16 = vb.astype(BF16)
        kbr16 = pltpu.roll(kb, HEAD_DIM, 1).astype(BF16)
        vbr16 = pltpu.roll(vb, HEAD_DIM, 1).astype(BF16)
        k_nat = jnp.concatenate([kprev_s[b], kb16], axis=0)
        k_rot = jnp.concatenate([kprevr_s[b], kbr16], axis=0)
        v_nat = jnp.concatenate([vprev_s[b], vb16], axis=0)
        v_rot = jnp.concatenate([vprevr_s[b], vbr16], axis=0)
        q2 = [proj_s[rows, 128 * jq:128 * (jq + 1)] * (HEAD_DIM ** -0.5) for jq in range(N_HEADS // 2)]
        q_nat = jnp.concatenate([jnp.where(lo if h % 2 == 0 else hi, q2[h // 2], 0.0) for h in NAT_HEADS],
                                axis=0).astype(BF16)
        q_rot = jnp.concatenate([jnp.where(lo if h % 2 == 0 else hi, q2[h // 2], 0.0) for h in ROT_HEADS],
                                axis=0).astype(BF16)
        o_nat = attend(q_nat, k_nat, v_nat, bias_ref[table, 0], sink_nat)
        o_rot = attend(q_rot, k_rot, v_rot, bias_ref[table, 1], sink_rot)
        for jq in range(N_HEADS // 2):
            blk = slice(CHUNK * jq, CHUNK * (jq + 1))
            even, odd = (o_nat, o_rot) if jq < 2 else (o_rot, o_nat)
            attn_s[rows, 128 * jq:128 * (jq + 1)] = jnp.where(lo, even[blk], odd[blk])
        kprev_s[b] = kb16
        kprevr_s[b] = kbr16
        vprev_s[b] = vb16
        vprevr_s[b] = vbr16
        return carry

    def attn_group(i, carry):
        for n in range(ATTN_UNROLL):
            attn_body(ATTN_UNROLL * i + n, carry)
        return carry

    lax.fori_loop(0, BATCH // ATTN_UNROLL, attn_group, 0)

    for sc in range(CHUNK // SUB_T):
        t0 = sc * SUB_T
        for i in range(SUB_T):
            for jb in range(D_SSM // 128):
                utb_s[i * BATCH:(i + 1) * BATCH, 128 * jb:128 * (jb + 1)] = (
                    u_s[jb, pl.ds(t0 + i, BATCH, stride=PITCH), :])
        u_tb = utb_s[...]
        for j in range(N_SBLK):
            bu_s[...] = _bdot(u_tb[:, 128 * j:128 * (j + 1)], bblk_ref[j])
            lr = jnp.broadcast_to(lamr_ref[j], (BATCH, SBLK))
            li = jnp.broadcast_to(lami_ref[j], (BATCH, SBLK))
            cols = slice(SBLK * j, SBLK * (j + 1))

            def step(i, carry):
                sr, si = carry
                r = pl.ds(pl.multiple_of(i * BATCH, BATCH), BATCH)
                nr = lr * sr - li * si + bu_s[r, 0:SBLK]
                ni = lr * si + li * sr + bu_s[r, SBLK:2 * SBLK]
                xs_s[r, 0:SBLK] = nr
                xs_s[r, SBLK:2 * SBLK] = ni
                return nr, ni

            sr, si = lax.fori_loop(0, SUB_T, step, (sre_ref[:, cols], sim_ref[:, cols]), unroll=True)
            sre_ref[:, cols] = sr
            sim_ref[:, cols] = si
            ytb_s[:, 128 * j:128 * (j + 1)] = _bdot(xs_s[...], cblk_ref[j])
        yn = _ssm_post(ytb_s[...], u_tb, dskip_ref[...], wglu_ref[...], bglu_ref[...], gs_ref[...])
        for i in range(SUB_T):
            for jb in range(D_SSM // 128):
                ssm_s[jb, pl.ds(t0 + i, BATCH, stride=PITCH), :] = (
                    yn[i * BATCH:(i + 1) * BATCH, 128 * jb:128 * (jb + 1)])

    an = _rms(attn_s[...], ga_ref[...])
    sn = jnp.concatenate(
        [jnp.concatenate([ssm_s[jb, b * PITCH:b * PITCH + CHUNK, :] for b in range(BATCH)], axis=0)
         for jb in range(D_SSM // 128)], axis=1)
    h = x + _bdot(an, wout_ref[0:D_ATTN, :]) + _bdot(sn, wout_ref[D_ATTN:2 * D_ATTN, :])
    hmid_ref[...] = h
    _route(h, gffn_ref, wrt_ref, br_ref, tri_ref, cnt_ref, hn_ref, idx_ref, gw_ref, rank_ref)

    @pl.when(c == N_CHUNKS - 1)
    def _():
        klast_ref[...] = proj_s[:, D_ATTN:D_ATTN + D_KV].reshape(BATCH, CHUNK, D_KV)
        vlast_ref[...] = proj_s[:, D_ATTN + D_KV:D_ATTN + 2 * D_KV].reshape(BATCH, CHUNK, D_KV)


def _mixer_prompt(x_prompt, sinks, gmix, win, bias, bblk, lamr, lami, cblk, dskip, wglu, bglu, ga, gs, wout,
                  gffn, wrt, br):
    smem = pl.BlockSpec(memory_space=pltpu.SMEM)
    in_specs = [
        smem,
        pl.BlockSpec((BATCH, CHUNK, D_MODEL), lambda c: (0, c, 0)),
        _full((1, D_MODEL)), _full((D_MODEL, D_IN)), _full((2, 2, 4 * CHUNK, 2 * CHUNK)),
        _full((N_SBLK, 128, 2 * SBLK)), _full((N_SBLK, 1, SBLK)), _full((N_SBLK, 1, SBLK)),
        _full((N_SBLK, 2 * SBLK, 128)), _full((1, D_SSM)), _full((D_SSM, D_SSM)), _full((1, D_SSM)),
        _full((1, D_ATTN)), _full((1, D_SSM)), _full((D_MODEL, D_MODEL)),
        _full((1, D_MODEL)), _full((N_EXPERTS, D_MODEL)), _full((N_EXPERTS, 1)), _full((ROUTE_BLOCK, ROUTE_BLOCK)),
    ]
    route_specs, route_shapes = _route_specs(ROWS, lambda c: c)
    out_shape = [
        jax.ShapeDtypeStruct((T_PAD, D_MODEL), F32),
        jax.ShapeDtypeStruct((BATCH, CHUNK, D_KV), F32),
        jax.ShapeDtypeStruct((BATCH, CHUNK, D_KV), F32),
        jax.ShapeDtypeStruct((BATCH, SSM_G * SSM_P), F32),
        jax.ShapeDtypeStruct((BATCH, SSM_G * SSM_P), F32),
    ] + route_shapes
    out_specs = [
        pl.BlockSpec((ROWS, D_MODEL), lambda c: (c, 0)),
        _full((BATCH, CHUNK, D_KV)), _full((BATCH, CHUNK, D_KV)),
        _full((BATCH, SSM_G * SSM_P)), _full((BATCH, SSM_G * SSM_P)),
    ] + route_specs
    kv_scr = pltpu.VMEM((BATCH, CHUNK, D_KV), BF16)
    scratch = [
        pltpu.VMEM((ROWS, D_ATTN + 2 * D_KV), F32), pltpu.VMEM((D_SSM // 128, BATCH * PITCH, 128), F32),
        kv_scr, kv_scr, kv_scr, kv_scr,
        pltpu.VMEM((ROWS, D_ATTN), F32), pltpu.VMEM((D_SSM // 128, BATCH * PITCH, 128), F32),
        pltpu.VMEM((SUB_ROWS, D_SSM), F32), pltpu.VMEM((SUB_ROWS, 2 * SBLK), F32),
        pltpu.VMEM((SUB_ROWS, 2 * SBLK), F32), pltpu.VMEM((SUB_ROWS, D_SSM), F32),
    ]
    return pl.pallas_call(
        _mixer_prompt_kernel, grid=(N_CHUNKS,), in_specs=in_specs, out_specs=out_specs, out_shape=out_shape,
        scratch_shapes=scratch, name="mixer_prompt",
        compiler_params=pltpu.CompilerParams(dimension_semantics=("arbitrary",), vmem_limit_bytes=VMEM_LIMIT),
    )(sinks, x_prompt, gmix, win, bias, bblk, lamr, lami, cblk, dskip, wglu, bglu, ga, gs, wout,
      gffn, wrt, br, _tri(ROUTE_BLOCK))


SGRP = 16
SPITCH = DEC_BATCH + 8
N_SGRP = DEC_BATCH // SGRP


def _mixer_sample_kernel(x_ref, gmix_ref, win_ref, ck_ref, cv_ref, sbias_ref, sinkc_ref, bblk_ref, lamr_ref,
                         lami_ref, cblk_ref, dskip_ref, wglu_ref, bglu_ref, ga_ref, gs_ref, wout_ref,
                         x0r_ref, x0i_ref, gffn_ref, wrt_ref, br_ref, tri_ref, cnt_in_ref,
                         hbuf_ref, hnbuf_ref, idxbuf_ref, gwbuf_ref, rankbuf_ref,
                         hmid_ref, knew_ref, vnew_ref, sre_ref, sim_ref, hn_ref, idx_ref, gw_ref, rank_ref, cnt_ref,
                         proj_s, qall_s, oall_s, kn8_s, vn8_s):
    del hbuf_ref, hnbuf_ref, idxbuf_ref, gwbuf_ref, rankbuf_ref
    g = pl.program_id(0)
    lo = lax.broadcasted_iota(jnp.int32, (DEC_BATCH, 128), 1) < HEAD_DIM

    @pl.when(g == 0)
    def _():
        proj = _bdot(_rms(x_ref[...], gmix_ref[...]), win_ref[...])
        proj_s[...] = proj
        for h in range(N_HEADS):
            jq, half, kv = h // 2, h % 2, h // 4
            q2 = proj[:, 128 * jq:128 * (jq + 1)] * (HEAD_DIM ** -0.5)
            if half != kv:
                q2 = pltpu.roll(q2, HEAD_DIM, 1)
            hr = slice(h * SPITCH, h * SPITCH + DEC_BATCH)
            qall_s[hr, :] = jnp.where(lo if kv == 0 else jnp.logical_not(lo), q2, 0.0)
            kn8_s[hr, :] = proj[:, D_ATTN:D_ATTN + D_KV]
            vn8_s[hr, :] = proj[:, D_ATTN + D_KV:D_ATTN + 2 * D_KV]

    def head_rows(ref):
        return jnp.stack([ref[pl.ds(g * SGRP + ii, N_HEADS, stride=SPITCH), :] for ii in range(SGRP)], axis=0)

    qe = head_rows(qall_s)
    knew = head_rows(kn8_s)
    vnew = head_rows(vn8_s)
    s = jnp.einsum('bhd,bkd->bhk', qe.astype(BF16), ck_ref[...].astype(BF16),
                   preferred_element_type=F32) + sbias_ref[...][None]
    s_new = jnp.sum(qe * knew, axis=-1, keepdims=True)
    sink = sinkc_ref[...][None]
    m = jnp.maximum(jnp.maximum(jnp.max(s, axis=-1, keepdims=True), s_new), sink)
    p = jnp.exp(s - m)
    p_new = jnp.exp(s_new - m)
    den = jnp.sum(p, axis=-1, keepdims=True) + p_new + jnp.exp(sink - m)
    o = (jnp.einsum('bhk,bkd->bhd', p.astype(BF16), cv_ref[...].astype(BF16),
                    preferred_element_type=F32) + p_new * vnew) / den
    for ii in range(SGRP):
        oall_s[pl.ds(g * SGRP + ii, N_HEADS, stride=SPITCH), :] = o[ii]

    @pl.when(g == N_SGRP - 1)
    def _():
        proj = proj_s[...]
        blocks = []
        for jq in range(N_HEADS // 2):
            oa = oall_s[(2 * jq) * SPITCH:(2 * jq) * SPITCH + DEC_BATCH, :]
            ob = oall_s[(2 * jq + 1) * SPITCH:(2 * jq + 1) * SPITCH + DEC_BATCH, :]
            if jq // 2 == 0:
                blocks.append(jnp.where(lo, oa, pltpu.roll(ob, HEAD_DIM, 1)))
            else:
                blocks.append(jnp.where(lo, pltpu.roll(oa, HEAD_DIM, 1), ob))
        attn = jnp.concatenate(blocks, axis=1)
        u = proj[:, D_ATTN + 2 * D_KV:]
        ys = []
        for j in range(N_SBLK):
            bu = _bdot(u[:, 128 * j:128 * (j + 1)], bblk_ref[j])
            lr = lamr_ref[j]
            li = lami_ref[j]
            cols = slice(SBLK * j, SBLK * (j + 1))
            sr = x0r_ref[:, cols]
            si = x0i_ref[:, cols]
            nr = lr * sr - li * si + bu[:, 0:SBLK]
            ni = lr * si + li * sr + bu[:, SBLK:2 * SBLK]
            sre_ref[:, cols] = nr
            sim_ref[:, cols] = ni
            ys.append(_bdot(jnp.concatenate([nr, ni], axis=1), cblk_ref[j]))
        sn = _ssm_post(jnp.concatenate(ys, axis=1), u, dskip_ref[...], wglu_ref[...], bglu_ref[...], gs_ref[...])
        an = _rms(attn, ga_ref[...])
        h = x_ref[...] + _bdot(an, wout_ref[0:D_ATTN, :]) + _bdot(sn, wout_ref[D_ATTN:2 * D_ATTN, :])
        h = jnp.concatenate([h, jnp.zeros((DEC_BATCH, D_MODEL), F32)], axis=0)
        hmid_ref[...] = h
        cnt_ref[...] = cnt_in_ref[...]
        _route(h, gffn_ref, wrt_ref, br_ref, tri_ref, cnt_ref, hn_ref, idx_ref, gw_ref, rank_ref)
        knew_ref[...] = proj[:, D_ATTN:D_ATTN + D_KV]
        vnew_ref[...] = proj[:, D_ATTN + D_KV:D_ATTN + 2 * D_KV]


def _mixer_sample(x_s, gmix, win, ck, cv, sbias, sinkc, bblk, lamr, lami, cblk, dskip, wglu, bglu, ga, gs, wout,
                  x0r, x0i, gffn, wrt, br, cnt_in, hbuf, hnbuf, idxbuf, gwbuf, rankbuf):
    nst = SSM_G * SSM_P
    in_specs = [
        _full((DEC_BATCH, D_MODEL)), _full((1, D_MODEL)), _full((D_MODEL, D_IN)),
        pl.BlockSpec((SGRP, WINDOW, D_KV), lambda g: (g, 0, 0)),
        pl.BlockSpec((SGRP, WINDOW, D_KV), lambda g: (g, 0, 0)),
        _full((N_HEADS, WINDOW)), _full((N_HEADS, 1)),
        _full((N_SBLK, 128, 2 * SBLK)), _full((N_SBLK, 1, SBLK)), _full((N_SBLK, 1, SBLK)),
        _full((N_SBLK, 2 * SBLK, 128)), _full((1, D_SSM)), _full((D_SSM, D_SSM)), _full((1, D_SSM)),
        _full((1, D_ATTN)), _full((1, D_SSM)), _full((D_MODEL, D_MODEL)),
        _full((DEC_BATCH, nst)), _full((DEC_BATCH, nst)),
        _full((1, D_MODEL)), _full((N_EXPERTS, D_MODEL)), _full((N_EXPERTS, 1)),
        _full((2 * DEC_BATCH, 2 * DEC_BATCH)), _full((N_EXPERTS, 1)),
    ] + [pl.BlockSpec(memory_space=pl.ANY)] * 5
    tail_block = SAMPLE_ROW0 // (2 * DEC_BATCH)
    route_specs, route_shapes = _route_specs(2 * DEC_BATCH, lambda g: tail_block)
    out_shape = [
        jax.ShapeDtypeStruct((T_PAD, D_MODEL), F32),
        jax.ShapeDtypeStruct((DEC_BATCH, D_KV), F32), jax.ShapeDtypeStruct((DEC_BATCH, D_KV), F32),
        jax.ShapeDtypeStruct((DEC_BATCH, nst), F32), jax.ShapeDtypeStruct((DEC_BATCH, nst), F32),
    ] + route_shapes
    out_specs = [
        pl.BlockSpec((2 * DEC_BATCH, D_MODEL), lambda g: (tail_block, 0)),
        _full((DEC_BATCH, D_KV)), _full((DEC_BATCH, D_KV)),
        _full((DEC_BATCH, nst)), _full((DEC_BATCH, nst)),
    ] + route_specs
    head_rows = pltpu.VMEM((N_HEADS * SPITCH, 128), F32)
    scratch = [pltpu.VMEM((DEC_BATCH, D_IN), F32), head_rows, head_rows, head_rows, head_rows]
    return pl.pallas_call(
        _mixer_sample_kernel, grid=(N_SGRP,), in_specs=in_specs, out_specs=out_specs, out_shape=out_shape,
        scratch_shapes=scratch, input_output_aliases={24: 0, 25: 5, 26: 6, 27: 7, 28: 8}, name="mixer_sample",
        compiler_params=pltpu.CompilerParams(dimension_semantics=("arbitrary",), vmem_limit_bytes=VMEM_LIMIT),
    )(x_s, gmix, win, ck, cv, sbias, sinkc, bblk, lamr, lami, cblk, dskip, wglu, bglu, ga, gs, wout, x0r, x0i,
      gffn, wrt, br, _tri(2 * DEC_BATCH), cnt_in, hbuf, hnbuf, idxbuf, gwbuf, rankbuf)


def _route(h, g_ref, wrt_ref, br_ref, tri_ref, cnt_ref, hn_ref, idx_ref, gw_ref, rank_ref):
    hn = _rms(h, g_ref[...])
    hn_ref[...] = _pack_pairs(hn)
    hn_hi = hn.astype(BF16)
    hn_lo = (hn - hn_hi.astype(F32)).astype(BF16)
    w = wrt_ref[...]
    w_hi = w.astype(BF16)
    w_lo = (w - w_hi.astype(F32)).astype(BF16)
    nt = (((1,), (1,)), ((), ()))
    logits = (lax.dot_general(w_hi, hn_hi, nt, preferred_element_type=F32)
              + lax.dot_general(w_lo, hn_hi, nt, preferred_element_type=F32)
              + lax.dot_general(w_hi, hn_lo, nt, preferred_element_type=F32)) + br_ref[...]
    eidx = lax.broadcasted_iota(jnp.int32, logits.shape, 0)
    vals, onehots = [], []
    l = logits
    for k in range(TOP_K):
        m = jnp.max(l, axis=0, keepdims=True)
        ik = jnp.min(jnp.where(l == m, eidx, N_EXPERTS), axis=0, keepdims=True)
        oh = eidx == ik
        idx_ref[k:k + 1, :] = ik
        vals.append(m)
        onehots.append(oh)
        l = jnp.where(oh, -jnp.inf, l)
    exps = [jnp.exp(v - vals[0]) for v in vals]
    den = exps[0] + exps[1] + exps[2] + exps[3]
    for k in range(TOP_K):
        gw_ref[k:k + 1, :] = exps[k] / den
    member = jnp.zeros(logits.shape, F32)
    for oh in onehots:
        member = member + jnp.where(oh, 1.0, 0.0)
    wblk = tri_ref.shape[0]
    base = cnt_ref[...].astype(F32)
    befores = []
    for cb in range(h.shape[0] // wblk):
        mblk = member[:, cb * wblk:(cb + 1) * wblk]
        befores.append(jnp.dot(mblk.astype(BF16), tri_ref[...], preferred_element_type=F32) + base)
        base = base + jnp.sum(mblk, axis=1, keepdims=True)
    before = jnp.concatenate(befores, axis=1)
    for k in range(TOP_K):
        rank_ref[k:k + 1, :] = jnp.sum(jnp.where(onehots[k], before, 0.0), axis=0, keepdims=True).astype(jnp.int32)
    cnt_ref[...] = base.astype(jnp.int32)


def _route_specs(rows, block):
    specs = [pl.BlockSpec((rows, D_PACK), lambda i: (block(i), 0)), pl.BlockSpec((TOP_K, rows), lambda i: (0, block(i))),
             pl.BlockSpec((TOP_K, rows), lambda i: (0, block(i))), pl.BlockSpec((TOP_K, rows), lambda i: (0, block(i))),
             _full((N_EXPERTS, 1))]
    shapes = [jax.ShapeDtypeStruct((T_PAD, D_PACK), jnp.int32), jax.ShapeDtypeStruct((TOP_K, T_PAD), jnp.int32),
              jax.ShapeDtypeStruct((TOP_K, T_PAD), F32), jax.ShapeDtypeStruct((TOP_K, T_PAD), jnp.int32),
              jax.ShapeDtypeStruct((N_EXPERTS, 1), jnp.int32)]
    return specs, shapes


def _tri(n):
    return jnp.asarray(np.triu(np.ones((n, n), np.float32), 1), BF16)


def _place_kernel(off_ref, idx_ref, rank_ref, pos_ref):
    idx = idx_ref[...]
    pos = rank_ref[...]
    for e in range(N_EXPERTS):
        pos = pos + jnp.where(idx == e, off_ref[e], 0)
    pos_ref[...] = pos


def _place(offsets, idx, rank):
    return pl.pallas_call(
        _place_kernel,
        in_specs=[pl.BlockSpec(memory_space=pltpu.SMEM), pl.BlockSpec(memory_space=pltpu.VMEM),
                  pl.BlockSpec(memory_space=pltpu.VMEM)],
        out_specs=pl.BlockSpec(memory_space=pltpu.VMEM),
        out_shape=jax.ShapeDtypeStruct((TOP_K, T_PAD), jnp.int32), name="place",
    )(offsets, idx, rank)


def _sc_mesh():
    return plsc.VectorSubcoreMesh(core_axis_name="core", subcore_axis_name="subcore")


def _sc_dispatch(rows, pos):
    n, d = rows.shape
    nblk = n // SC_ROWS
    pos_w = pos.reshape(TOP_K, nblk, SC_ROWS).transpose(1, 0, 2)

    @functools.partial(pl.kernel, out_type=jax.ShapeDtypeStruct((P_ROWS, d), rows.dtype), mesh=_sc_mesh(),
                       scratch_types=[], name="dispatch")
    def run(x_hbm, i_hbm, o_hbm):
        def body(x_vmem, i_vmem):
            for k in range(TOP_K):
                pltpu.sync_copy(x_vmem, o_hbm.at[i_vmem.at[0, k]])

        pltpu.emit_pipeline(
            body, grid=(nblk,),
            in_specs=[pl.BlockSpec((SC_ROWS, d), lambda i: (i, 0)),
                      pl.BlockSpec((1, TOP_K, SC_ROWS), lambda i: (i, 0, 0))],
            out_specs=[], core_axis_name=("core", "subcore"), dimension_semantics=(pltpu.PARALLEL,),
        )(x_hbm, i_hbm)

    return run(rows, pos_w)


def _sc_combine(rows, flat_pos):
    _, d = rows.shape
    n = flat_pos.shape[0]
    w = SC_ROWS_COMBINE
    nblk = n // w
    pos_w = flat_pos.reshape(nblk, 1, w)

    @functools.partial(pl.kernel, out_type=jax.ShapeDtypeStruct((n, d), rows.dtype), mesh=_sc_mesh(),
                       scratch_types=[], name="combine")
    def run(x_hbm, i_hbm, o_hbm):
        def body(i_vmem, o_vmem):
            pltpu.sync_copy(x_hbm.at[i_vmem.at[0, 0]], o_vmem)

        pltpu.emit_pipeline(
            body, grid=(nblk,),
            in_specs=[pl.BlockSpec((1, 1, w), lambda i: (i, 0, 0))],
            out_specs=[pl.BlockSpec((w, d), lambda i: (i, 0))],
            core_axis_name=("core", "subcore"), dimension_semantics=(pltpu.PARALLEL,),
        )(i_hbm, o_hbm)

    return run(rows, pos_w)


def _experts_kernel(ts_ref, trow_ref, tsz_ref, xs_hbm, w1_hbm, b1_ref, w2_hbm, b2_ref, ys_hbm,
                    w1_st, w2_st, w1_s, w2_s, xbuf, ybuf, wsem, xsem, ysem):
    e = pl.program_id(0)
    n_valid = ts_ref[N_EXPERTS]

    def w_copies(ex, slot):
        return (pltpu.make_async_copy(w1_hbm.at[ex], w1_st.at[slot], wsem.at[0, slot]),
                pltpu.make_async_copy(w2_hbm.at[ex], w2_st.at[slot], wsem.at[1, slot]))

    def x_copy(t, slot, rows):
        src = xs_hbm.at[pl.ds(pl.multiple_of(trow_ref[t], TM_UNIT), rows)]
        return pltpu.make_async_copy(src, xbuf.at[slot, pl.ds(0, rows)], xsem.at[slot])

    def y_copy(t, slot, rows):
        dst = ys_hbm.at[pl.ds(pl.multiple_of(trow_ref[t], TM_UNIT), rows)]
        return pltpu.make_async_copy(ybuf.at[slot, pl.ds(0, rows)], dst, ysem.at[slot])

    def by_size(t, fn):
        for rows in TILE_SIZES:
            @pl.when(tsz_ref[t] == rows)
            def _():
                fn(rows)

    @pl.when(e == 0)
    def _():
        for c in w_copies(0, 0):
            c.start()
        by_size(0, lambda rows: x_copy(0, 0, rows).start())

    @pl.when(e + 1 < N_EXPERTS)
    def _():
        for c in w_copies(e + 1, (e + 1) % 2):
            c.start()

    wslot = e % 2
    for c in w_copies(e, wslot):
        c.wait()
    for r in range(4):
        rs = slice(256 * r, 256 * (r + 1))
        w1_s[rs, :] = w1_st[wslot, rs, :].astype(BF16)
        w2_s[rs, :] = w2_st[wslot, rs, :].astype(BF16)

    def tile(t, carry):
        slot = t % 2

        @pl.when(t + 1 < n_valid)
        def _():
            by_size(t + 1, lambda rows: x_copy(t + 1, 1 - slot, rows).start())

        by_size(t, lambda rows: x_copy(t, slot, rows).wait())

        @pl.when(t >= 2)
        def _():
            by_size(t - 2, lambda rows: y_copy(t - 2, slot, rows).wait())

        def compute(rows):
            hdn = _bdot(_unpack_pairs(xbuf[slot, 0:rows]), w1_s[...]) + b1_ref[0]
            gt = jnp.minimum(hdn[:, :D_FF], SWIGLU_LIMIT)
            up = jnp.clip(hdn[:, D_FF:], -SWIGLU_LIMIT, SWIGLU_LIMIT)
            act = (up + 1.0) * gt * _sigmoid(SWIGLU_ALPHA * gt)
            ybuf[slot, 0:rows] = _pack_pairs(_bdot(act, w2_s[...]) + b2_ref[0])
            y_copy(t, slot, rows).start()

        by_size(t, compute)
        return carry

    lax.fori_loop(ts_ref[e], ts_ref[e + 1], tile, 0)

    @pl.when(e == N_EXPERTS - 1)
    def _():
        @pl.when(n_valid >= 2)
        def _():
            by_size(n_valid - 2, lambda rows: y_copy(n_valid - 2, n_valid % 2, rows).wait())

        by_size(n_valid - 1, lambda rows: y_copy(n_valid - 1, (n_valid - 1) % 2, rows).wait())


def _tile_plan(counts):
    units = (counts + (TM_UNIT - 1)) // TM_UNIT
    unit_end = jnp.cumsum(units)
    offsets = ((unit_end - units) * TM_UNIT).astype(jnp.int32)
    per_full = TM // TM_UNIT
    n_full = units // per_full
    rest = units % per_full
    has_mid = (rest >= 2).astype(jnp.int32)
    tiles_per = n_full + has_mid + rest % 2
    tile_end = jnp.cumsum(tiles_per)
    tile_start = jnp.concatenate([jnp.zeros((1,), jnp.int32), tile_end.astype(jnp.int32)])
    t = jnp.arange(MAX_TILES, dtype=jnp.int32)[:, None]
    done = tile_end[None, :] <= t
    mine = jnp.sum(done.astype(jnp.int32), axis=1, keepdims=True) == jnp.arange(N_EXPERTS, dtype=jnp.int32)[None, :]
    pick = lambda v: jnp.sum(jnp.where(mine, v[None, :], 0), axis=1)
    j = t[:, 0] - jnp.max(jnp.where(done, tile_end[None, :], 0), axis=1)
    nf, mid = pick(n_full), pick(has_mid)
    rows = jnp.where(j < nf, TM, jnp.where((j == nf) & (mid == 1), TILE_SIZES[1], TILE_SIZES[2]))
    row0 = pick(offsets) + jnp.minimum(j, nf) * TM + jnp.where(j > nf, TILE_SIZES[1], 0)
    valid = t[:, 0] < tile_end[-1]
    tile_row = jnp.where(valid, row0, 0).astype(jnp.int32)
    tile_rows = jnp.where(valid, rows, 0).astype(jnp.int32)
    return offsets, tile_start, tile_row, tile_rows


def _experts(tile_start, tile_row, tile_rows, xs, w_up, b_up, w_down, b_down):
    wsel = lambda e, *_: (e, 0, 0)
    hbm = pl.BlockSpec(memory_space=pl.ANY)
    grid_spec = pltpu.PrefetchScalarGridSpec(
        num_scalar_prefetch=3, grid=(N_EXPERTS,),
        in_specs=[hbm, hbm, pl.BlockSpec((1, 1, 2 * D_FF), wsel), hbm, pl.BlockSpec((1, 1, D_MODEL), wsel)],
        out_specs=hbm,
        scratch_shapes=[pltpu.VMEM((2, D_MODEL, 2 * D_FF), F32), pltpu.VMEM((2, D_FF, D_MODEL), F32),
                        pltpu.VMEM((D_MODEL, 2 * D_FF), BF16), pltpu.VMEM((D_FF, D_MODEL), BF16),
                        pltpu.VMEM((2, TM, D_PACK), jnp.int32), pltpu.VMEM((2, TM, D_PACK), jnp.int32),
                        pltpu.SemaphoreType.DMA((2, 2)), pltpu.SemaphoreType.DMA((2,)),
                        pltpu.SemaphoreType.DMA((2,))],
    )
    return pl.pallas_call(
        _experts_kernel, grid_spec=grid_spec, out_shape=jax.ShapeDtypeStruct((P_ROWS, D_PACK), jnp.int32),
        name="experts",
        compiler_params=pltpu.CompilerParams(dimension_semantics=("arbitrary",), vmem_limit_bytes=VMEM_LIMIT),
    )(tile_start, tile_row, tile_rows, xs, w_up, b_up.reshape(N_EXPERTS, 1, 2 * D_FF), w_down,
      b_down.reshape(N_EXPERTS, 1, D_MODEL))


def _final_kernel(h_ref, yg_ref, gw_ref, p_ref, gple_ref, wg_ref, wp_ref, gfin_ref, *rest):
    out_ref = rest[-1]
    rows = h_ref.shape[0]
    gw = gw_ref[...]
    h = h_ref[...]
    for k in range(TOP_K):
        h = h + gw[:, k:k + 1] * _unpack_pairs(yg_ref[k])
    gate = _sigmoid(_bdot(_rms(h, gple_ref[...]), wg_ref[...]))
    h = h + gate * _bdot(p_ref[...].reshape(rows, PLE_DIM), wp_ref[...])
    out_ref[...] = _rms(h, gfin_ref[...]).reshape(out_ref.shape)


def _final_prompt(part, hmid, yg, gwt, p_prompt, gple, wg, wp, gfin, y_prev):
    nb = 4
    rows = nb * CHUNK
    nbh = BATCH // nb
    c0 = part * PART_CHUNKS
    rb = lambda c, b: ((c0 + c) * nbh + b, 0)
    in_specs = [pl.BlockSpec((rows, D_MODEL), rb),
                pl.BlockSpec((TOP_K, rows, D_PACK), lambda c, b: (0, c * nbh + b, 0)),
                pl.BlockSpec((rows, TOP_K), rb),
                pl.BlockSpec((nb, CHUNK, PLE_DIM), lambda c, b: (b, c0 + c, 0)),
                _full((1, D_MODEL)), _full((D_MODEL, D_MODEL)), _full((PLE_DIM, D_MODEL)), _full((1, D_MODEL))]
    args = [hmid, yg, gwt, p_prompt, gple, wg, wp, gfin]
    aliases = {}
    if y_prev is not None:
        in_specs.append(pl.BlockSpec(memory_space=pl.ANY))
        args.append(y_prev)
        aliases = {len(args) - 1: 0}
    return pl.pallas_call(
        _final_kernel, grid=(PART_CHUNKS, nbh), in_specs=in_specs,
        out_specs=pl.BlockSpec((nb, CHUNK, D_MODEL), lambda c, b: (b, c0 + c, 0)),
        out_shape=jax.ShapeDtypeStruct((BATCH, SEQ, D_MODEL), F32), name="final_prompt",
        input_output_aliases=aliases,
        compiler_params=pltpu.CompilerParams(dimension_semantics=("arbitrary", "arbitrary"),
                                             vmem_limit_bytes=VMEM_LIMIT),
    )(*args)


def _final_sample(hmid, yg, gwt, p_sample, gple, wg, wp, gfin):
    blk = SAMPLE_ROW0 // DEC_BATCH
    return pl.pallas_call(
        _final_kernel, grid=(1,),
        in_specs=[pl.BlockSpec((DEC_BATCH, D_MODEL), lambda i: (blk, 0)),
                  pl.BlockSpec((TOP_K, DEC_BATCH, D_PACK), lambda i: (0, PART_CHUNKS * ROWS // DEC_BATCH, 0)),
                  pl.BlockSpec((DEC_BATCH, TOP_K), lambda i: (blk, 0)),
                  _full((DEC_BATCH, PLE_DIM)),
                  _full((1, D_MODEL)), _full((D_MODEL, D_MODEL)), _full((PLE_DIM, D_MODEL)), _full((1, D_MODEL))],
        out_specs=_full((DEC_BATCH, D_MODEL)),
        out_shape=jax.ShapeDtypeStruct((DEC_BATCH, D_MODEL), F32), name="final_sample",
        compiler_params=pltpu.CompilerParams(dimension_semantics=("arbitrary",), vmem_limit_bytes=VMEM_LIMIT),
    )(hmid, yg, gwt, p_sample, gple, wg, wp, gfin)


def _alibi_tables():
    slopes = 2.0 ** (-8.0 * (np.arange(N_HEADS, dtype=np.float64) + 1.0) / N_HEADS)
    i = np.arange(CHUNK)[:, None]
    j = np.arange(2 * CHUNK)[None, :]
    dist = i + CHUNK - j
    valid = (dist >= 0) & (dist <= WINDOW)
    tabs = []
    for has_prev in (False, True):
        ok = valid & ((j >= CHUNK) | has_prev)
        tabs.append(np.where(ok[None], -slopes[:, None, None] * dist[None], NEG))
    prompt = np.stack([np.stack([np.concatenate([t[h] for h in grp], axis=0) for grp in (NAT_HEADS, ROT_HEADS)])
                       for t in tabs]).astype(np.float32)
    wb = min(WINDOW, PAST_LEN)
    sample = (-slopes[:, None] * (wb - np.arange(wb))[None, :]).astype(np.float32)
    return prompt, sample


def kernel(x_prompt, x_sample, cache_k_win, cache_v_win, state_ssm_re, state_ssm_im, p_prompt, p_sample, norm_mix, w_in, sinks, ssm_lam_re, ssm_lam_im, ssm_log_step, ssm_b_re, ssm_b_im, ssm_c_re, ssm_c_im, ssm_d, w_glu, b_glu, norm_attn_out, norm_ssm_out, w_out, norm_ffn, w_router, b_router, w_up, b_up, w_down, b_down, norm_ple, w_ple_gate, w_ple_proj, norm_final):
    nst = SSM_G * SSM_P
    bias_np, sbias_np = _alibi_tables()
    bias = jnp.asarray(bias_np)
    sbias = jnp.asarray(sbias_np)

    lbr, lbi, bbr, bbi = _prep(ssm_lam_re[0], ssm_lam_im[0], ssm_log_step[0], ssm_b_re[0], ssm_b_im[0])
    bblk, lamr, lami, cblk = _s5_blocks(lbr, lbi, bbr, bbi, ssm_c_re[0], ssm_c_im[0])

    gmix = norm_mix[0].reshape(1, D_MODEL)
    win = w_in[0].astype(BF16)
    dskip = ssm_d[0].reshape(1, D_SSM)
    wglu = w_glu[0].astype(BF16)
    bglu = b_glu[0].reshape(1, D_SSM)
    ga = norm_attn_out[0].reshape(1, D_ATTN)
    gs = norm_ssm_out[0].reshape(1, D_SSM)
    wout = w_out[0].astype(BF16)
    sink = sinks[0]

    gffn = norm_ffn[0].reshape(1, D_MODEL)
    wrt = w_router[0].T
    br = b_router[0].reshape(N_EXPERTS, 1)
    hbuf, k_last, v_last, re_p, im_p, hnbuf, idxbuf, gwbuf, rankbuf, cnt_p = _mixer_prompt(
        x_prompt, sink, gmix, win, bias, bblk, lamr, lami, cblk, dskip, wglu, bglu, ga, gs, wout, gffn, wrt, br)

    ck = cache_k_win[0].reshape(DEC_BATCH, WINDOW, D_KV)
    cv = cache_v_win[0].reshape(DEC_BATCH, WINDOW, D_KV)
    hmid, k_new, v_new, re_s, im_s, hn, idx, gw, rank, counts = _mixer_sample(
        x_sample.reshape(DEC_BATCH, D_MODEL), gmix, win, ck, cv, sbias, sink.reshape(N_HEADS, 1), bblk, lamr, lami,
        cblk, dskip, wglu, bglu, ga, gs, wout, state_ssm_re[0].reshape(DEC_BATCH, nst),
        state_ssm_im[0].reshape(DEC_BATCH, nst), gffn, wrt, br, cnt_p, hbuf, hnbuf, idxbuf, gwbuf, rankbuf)

    offsets, tile_start, tile_row, tile_rows = _tile_plan(counts[:, 0])

    pos = _place(offsets, idx, rank)
    xs = _sc_dispatch(hn, pos)
    ys = _experts(tile_start, tile_row, tile_rows, xs, w_up[0], b_up[0], w_down[0], b_down[0])
    ygs = []
    for part in range(N_PARTS):
        r0 = part * PART_CHUNKS * ROWS
        r1 = T_PAD if part == N_PARTS - 1 else r0 + PART_CHUNKS * ROWS
        ygs.append(_sc_combine(ys, pos[:, r0:r1].reshape(TOP_K * (r1 - r0))).reshape(TOP_K, r1 - r0, D_PACK))

    gwt = gw.T
    gple = norm_ple[0].reshape(1, D_MODEL)
    wg = w_ple_gate[0].astype(BF16)
    wp = w_ple_proj[0].astype(BF16)
    gfin = norm_final.reshape(1, D_MODEL)
    y_prompt = None
    for part in range(N_PARTS):
        y_prompt = _final_prompt(part, hmid, ygs[part], gwt, p_prompt[0], gple, wg, wp, gfin, y_prompt)
    y_sample = _final_sample(hmid, ygs[-1], gwt, p_sample[0].reshape(DEC_BATCH, PLE_DIM), gple, wg, wp, gfin)

    k_win_s = jnp.concatenate([ck[:, 1:], k_new[:, None, :]], axis=1)
    v_win_s = jnp.concatenate([cv[:, 1:], v_new[:, None, :]], axis=1)
    kv5 = (1, BATCH, CHUNK, N_KV, HEAD_DIM)
    skv5 = (1, DEC_BATCH, WINDOW, N_KV, HEAD_DIM)
    return (y_prompt, y_sample.reshape(DEC_BATCH, 1, D_MODEL),
            k_last.reshape(kv5), v_last.reshape(kv5),
            re_p.reshape(1, BATCH, SSM_G, SSM_P), im_p.reshape(1, BATCH, SSM_G, SSM_P),
            k_win_s.reshape(skv5), v_win_s.reshape(skv5),
            re_s.reshape(1, DEC_BATCH, SSM_G, SSM_P), im_s.reshape(1, DEC_BATCH, SSM_G, SSM_P))
```

```python
import functools

import numpy as np
import jax
import jax.numpy as jnp
from jax import lax
from jax.experimental import pallas as pl
from jax.experimental.pallas import tpu as pltpu
from jax.experimental.pallas import tpu_sc as plsc

F32 = jnp.float32
BF16 = jnp.bfloat16

D_MODEL = 1024
BATCH = 8
SEQ = 2048
DEC_BATCH = 128
PAST_LEN = 16384
HEAD_DIM = 64
D_ATTN = 512
N_HEADS = 8
N_KV = 2
D_KV = N_KV * HEAD_DIM
WINDOW = 128
D_SSM = 512
SSM_H = 16
SSM_G = 32
SSM_P = 64
D_IN = D_ATTN + 2 * D_KV + D_SSM
N_EXPERTS = 32
TOP_K = 4
D_FF = 1024
SWIGLU_LIMIT = 7.0
SWIGLU_ALPHA = 1.702
PLE_DIM = 256
EPS = 1e-5
NEG = -1e30

CHUNK = 128
N_CHUNKS = SEQ // CHUNK
ROWS = BATCH * CHUNK
PITCH = CHUNK + 8
NAT_HEADS = (0, 2, 5, 7)
ROT_HEADS = (1, 3, 4, 6)
ATTN_UNROLL = 2
SUB_T = 64
SUB_ROWS = SUB_T * BATCH
N_SBLK = 4
SBLK = 512
T_REAL = BATCH * SEQ + DEC_BATCH
T_PAD = T_REAL + 128
SAMPLE_ROW0 = BATCH * SEQ
ROUTE_BLOCK = 512
TILE_SIZES = (1024, 512, 256)
TM = TILE_SIZES[0]
TM_UNIT = TILE_SIZES[-1]
P_ROWS = (T_PAD * TOP_K + N_EXPERTS * (TM_UNIT - 1)) // TM_UNIT * TM_UNIT
MAX_TILES = P_ROWS // TM + 2 * N_EXPERTS + 1
SC_ROWS = 40
SC_ROWS_COMBINE = 32
N_PARTS = 4
PART_CHUNKS = N_CHUNKS // N_PARTS
D_PACK = D_MODEL // 2
VMEM_LIMIT = 56 * 1024 * 1024


def _rms(x, g):
    return x * lax.rsqrt(jnp.mean(x * x, axis=-1, keepdims=True) + EPS) * g


def _sigmoid(x):
    return 1.0 / (1.0 + jnp.exp(-x))


def _gelu_tanh(x):
    c = np.float32(np.sqrt(2.0 / np.pi))
    return 0.5 * x * (1.0 + jnp.tanh(c * (x + 0.044715 * (x * x * x))))


def _bdot(a, b):
    return jnp.dot(a.astype(BF16), b, preferred_element_type=F32)


def _pack_pairs(x):
    n = x.shape[1] // 2
    lo = lax.bitcast_convert_type(x[:, :n].astype(BF16).astype(F32), jnp.int32)
    hi = lax.bitcast_convert_type(x[:, n:].astype(BF16).astype(F32), jnp.int32)
    return lax.shift_right_logical(lo, 16) | hi


def _unpack_pairs(w):
    lo = lax.bitcast_convert_type(lax.shift_left(w, 16), F32)
    hi = lax.bitcast_convert_type(w & jnp.int32(-65536), F32)
    return jnp.concatenate([lo, hi], axis=1)


def _full(shape):
    n = len(shape)
    return pl.BlockSpec(shape, lambda *_: (0,) * n)


def _prep_kernel(lr_ref, li_ref, ls_ref, br_ref, bi_ref, lbr_ref, lbi_ref, bbr_ref, bbi_ref):
    lr = lr_ref[...]
    li = li_ref[...]
    step = jnp.exp(ls_ref[...])
    zr = lr * step
    zi = li * step
    mag = jnp.exp(zr)
    lbr = mag * jnp.cos(zi)
    lbi = mag * jnp.sin(zi)
    lbr_ref[...] = lbr
    lbi_ref[...] = lbi
    nr = lbr - 1.0
    den = lr * lr + li * li
    cr = (nr * lr + lbi * li) / den
    ci = (lbi * lr - nr * li) / den
    br = br_ref[...]
    bi = bi_ref[...]
    bbr_ref[...] = cr * br - ci * bi
    bbi_ref[...] = cr * bi + ci * br


def _prep(lam_re, lam_im, log_step, b_re, b_im):
    g, p, h = SSM_G, SSM_P, SSM_H
    out = pl.pallas_call(
        _prep_kernel,
        out_shape=[jax.ShapeDtypeStruct((g, 1, p), F32), jax.ShapeDtypeStruct((g, 1, p), F32),
                   jax.ShapeDtypeStruct((g, h, p), F32), jax.ShapeDtypeStruct((g, h, p), F32)],
        name="s5_prep",
    )(lam_re.reshape(g, 1, p), lam_im.reshape(g, 1, p), log_step.reshape(g, 1, 1),
      jnp.transpose(b_re, (0, 2, 1)), jnp.transpose(b_im, (0, 2, 1)))
    return out


def _s5_blocks(lbr, lbi, bbr, bbi, c_re, c_im):
    eye = jnp.eye(8, dtype=F32)
    shp = (N_SBLK, 8, SSM_H, SSM_P)

    def in_map(b):
        return (b.reshape(shp)[:, :, :, None, :] * eye[None, :, None, :, None]).reshape(N_SBLK, 128, SBLK)

    def out_map(c):
        ct = jnp.transpose(c.reshape(shp), (0, 1, 3, 2))
        return (ct[:, :, :, None, :] * eye[None, :, None, :, None]).reshape(N_SBLK, SBLK, 128)

    bblk = jnp.concatenate([in_map(bbr), in_map(bbi)], axis=-1).astype(BF16)
    cblk = jnp.concatenate([out_map(c_re), -out_map(c_im)], axis=1).astype(BF16)
    return bblk, lbr.reshape(N_SBLK, 1, SBLK), lbi.reshape(N_SBLK, 1, SBLK), cblk


def _ssm_post(y_lin, u, dskip, wglu, bglu, gs):
    y = _gelu_tanh(y_lin + dskip * u)
    y = y * _sigmoid(_bdot(y, wglu) + bglu)
    return _rms(y, gs)


def _mixer_prompt_kernel(sinks_ref, x_ref, gmix_ref, win_ref, bias_ref, bblk_ref, lamr_ref, lami_ref,
                         cblk_ref, dskip_ref, wglu_ref, bglu_ref, ga_ref, gs_ref, wout_ref,
                         gffn_ref, wrt_ref, br_ref, tri_ref,
                         hmid_ref, klast_ref, vlast_ref, sre_ref, sim_ref, hn_ref, idx_ref, gw_ref, rank_ref, cnt_ref,
                         proj_s, u_s, kprev_s, kprevr_s, vprev_s, vprevr_s, attn_s, ssm_s, utb_s, bu_s, xs_s, ytb_s):
    c = pl.program_id(0)

    @pl.when(c == 0)
    def _():
        zkv = jnp.zeros(kprev_s.shape, BF16)
        kprev_s[...] = zkv
        kprevr_s[...] = zkv
        vprev_s[...] = zkv
        vprevr_s[...] = zkv
        sre_ref[...] = jnp.zeros(sre_ref.shape, F32)
        sim_ref[...] = jnp.zeros(sim_ref.shape, F32)
        cnt_ref[...] = jnp.zeros(cnt_ref.shape, jnp.int32)

    x = x_ref[...].reshape(ROWS, D_MODEL)
    proj = _bdot(_rms(x, gmix_ref[...]), win_ref[...])
    u0 = D_ATTN + 2 * D_KV
    proj_s[...] = proj[:, 0:u0]
    for jb in range(D_SSM // 128):
        for b in range(BATCH):
            u_s[jb, b * PITCH:b * PITCH + CHUNK, :] = proj[b * CHUNK:(b + 1) * CHUNK, u0 + 128 * jb:u0 + 128 * (jb + 1)]

    lo = lax.broadcasted_iota(jnp.int32, (CHUNK, 128), 1) < HEAD_DIM
    hi = jnp.logical_not(lo)
    table = jnp.minimum(c, 1)
    hrow = lax.broadcasted_iota(jnp.int32, (4 * CHUNK, 1), 0) // CHUNK

    def sink_col(heads):
        col = jnp.full((4 * CHUNK, 1), sinks_ref[heads[3]], F32)
        for n in (2, 1, 0):
            col = jnp.where(hrow == n, sinks_ref[heads[n]], col)
        return col

    sink_nat = sink_col(NAT_HEADS)
    sink_rot = sink_col(ROT_HEADS)

    def attend(q, k, v, bias, sink):
        s = lax.dot_general(q, k, (((1,), (1,)), ((), ())), preferred_element_type=F32) + bias
        m = jnp.maximum(jnp.max(s, axis=-1, keepdims=True), sink)
        p = jnp.exp(s - m)
        den = jnp.sum(p, axis=-1, keepdims=True) + jnp.exp(sink - m)
        return jnp.dot(p.astype(BF16), v, preferred_element_type=F32) / den

    def attn_body(b, carry):
        r0 = pl.multiple_of(b * CHUNK, CHUNK)
        rows = pl.ds(r0, CHUNK)
        kb = proj_s[rows, D_ATTN:D_ATTN + D_KV]
        vb = proj_s[rows, D_ATTN + D_KV:D_ATTN + 2 * D_KV]
        kb16 = kb.astype(BF16)
        vb16 = vb.astype(BF16)
        kbr16 = pltpu.roll(kb, HEAD_DIM, 1).astype(BF16)
        vbr16 = pltpu.roll(vb, HEAD_DIM, 1).astype(BF16)
        k_nat = jnp.concatenate([kprev_s[b], kb16], axis=0)
        k_rot = jnp.concatenate([kprevr_s[b], kbr16], axis=0)
        v_nat = jnp.concatenate([vprev_s[b], vb16], axis=0)
        v_rot = jnp.concatenate([vprevr_s[b], vbr16], axis=0)
        q2 = [proj_s[rows, 128 * jq:128 * (jq + 1)] * (HEAD_DIM ** -0.5) for jq in range(N_HEADS // 2)]
        q_nat = jnp.concatenate([jnp.where(lo if h % 2 == 0 else hi, q2[h // 2], 0.0) for h in NAT_HEADS],
                                axis=0).astype(BF16)
        q_rot = jnp.concatenate([jnp.where(lo if h % 2 == 0 else hi, q2[h // 2], 0.0) for h in ROT_HEADS],
                                axis=0).astype(BF16)
        o_nat = attend(q_nat, k_nat, v_nat, bias_ref[table, 0], sink_nat)
        o_rot = attend(q_rot, k_rot, v_rot, bias_ref[table, 1], sink_rot)
        for jq in range(N_HEADS // 2):
            blk = slice(CHUNK * jq, CHUNK * (jq + 1))
            even, odd = (o_nat, o_rot) if jq < 2 else (o_rot, o_nat)
            attn_s[rows, 128 * jq:128 * (jq + 1)] = jnp.where(lo, even[blk], odd[blk])
        kprev_s[b] = kb16
        kprevr_s[b] = kbr16
        vprev_s[b] = vb16
        vprevr_s[b] = vbr16
        return carry

    def attn_group(i, carry):
        for n in range(ATTN_UNROLL):
            attn_body(ATTN_UNROLL * i + n, carry)
        return carry

    lax.fori_loop(0, BATCH // ATTN_UNROLL, attn_group, 0)

    for sc in range(CHUNK // SUB_T):
        t0 = sc * SUB_T
        for i in range(SUB_T):
            for jb in range(D_SSM // 128):
                utb_s[i * BATCH:(i + 1) * BATCH, 128 * jb:128 * (jb + 1)] = (
                    u_s[jb, pl.ds(t0 + i, BATCH, stride=PITCH), :])
        u_tb = utb_s[...]
        for j in range(N_SBLK):
            bu_s[...] = _bdot(u_tb[:, 128 * j:128 * (j + 1)], bblk_ref[j])
            lr = jnp.broadcast_to(lamr_ref[j], (BATCH, SBLK))
            li = jnp.broadcast_to(lami_ref[j], (BATCH, SBLK))
            cols = slice(SBLK * j, SBLK * (j + 1))

            def step(i, carry):
                sr, si = carry
                r = pl.ds(pl.multiple_of(i * BATCH, BATCH), BATCH)
                nr = lr * sr - li * si + bu_s[r, 0:SBLK]
                ni = lr * si + li * sr + bu_s[r, SBLK:2 * SBLK]
                xs_s[r, 0:SBLK] = nr
                xs_s[r, SBLK:2 * SBLK] = ni
                return nr, ni

            sr, si = lax.fori_loop(0, SUB_T, step, (sre_ref[:, cols], sim_ref[:, cols]), unroll=True)
            sre_ref[:, cols] = sr
            sim_ref[:, cols] = si
            ytb_s[:, 128 * j:128 * (j + 1)] = _bdot(xs_s[...], cblk_ref[j])
        yn = _ssm_post(ytb_s[...], u_tb, dskip_ref[...], wglu_ref[...], bglu_ref[...], gs_ref[...])
        for i in range(SUB_T):
            for jb in range(D_SSM // 128):
                ssm_s[jb, pl.ds(t0 + i, BATCH, stride=PITCH), :] = (
                    yn[i * BATCH:(i + 1) * BATCH, 128 * jb:128 * (jb + 1)])

    an = _rms(attn_s[...], ga_ref[...])
    sn = jnp.concatenate(
        [jnp.concatenate([ssm_s[jb, b * PITCH:b * PITCH + CHUNK, :] for b in range(BATCH)], axis=0)
         for jb in range(D_SSM // 128)], axis=1)
    h = x + _bdot(an, wout_ref[0:D_ATTN, :]) + _bdot(sn, wout_ref[D_ATTN:2 * D_ATTN, :])
    hmid_ref[...] = h
    _route(h, gffn_ref, wrt_ref, br_ref, tri_ref, cnt_ref, hn_ref, idx_ref, gw_ref, rank_ref)

    @pl.when(c == N_CHUNKS - 1)
    def _():
        klast_ref[...] = proj_s[:, D_ATTN:D_ATTN + D_KV].reshape(BATCH, CHUNK, D_KV)
        vlast_ref[...] = proj_s[:, D_ATTN + D_KV:D_ATTN + 2 * D_KV].reshape(BATCH, CHUNK, D_KV)


def _mixer_prompt(x_prompt, sinks, gmix, win, bias, bblk, lamr, lami, cblk, dskip, wglu, bglu, ga, gs, wout,
                  gffn, wrt, br):
    smem = pl.BlockSpec(memory_space=pltpu.SMEM)
    in_specs = [
        smem,
        pl.BlockSpec((BATCH, CHUNK, D_MODEL), lambda c: (0, c, 0)),
        _full((1, D_MODEL)), _full((D_MODEL, D_IN)), _full((2, 2, 4 * CHUNK, 2 * CHUNK)),
        _full((N_SBLK, 128, 2 * SBLK)), _full((N_SBLK, 1, SBLK)), _full((N_SBLK, 1, SBLK)),
        _full((N_SBLK, 2 * SBLK, 128)), _full((1, D_SSM)), _full((D_SSM, D_SSM)), _full((1, D_SSM)),
        _full((1, D_ATTN)), _full((1, D_SSM)), _full((D_MODEL, D_MODEL)),
        _full((1, D_MODEL)), _full((N_EXPERTS, D_MODEL)), _full((N_EXPERTS, 1)), _full((ROUTE_BLOCK, ROUTE_BLOCK)),
    ]
    route_specs, route_shapes = _route_specs(ROWS, lambda c: c)
    out_shape = [
        jax.ShapeDtypeStruct((T_PAD, D_MODEL), F32),
        jax.ShapeDtypeStruct((BATCH, CHUNK, D_KV), F32),
        jax.ShapeDtypeStruct((BATCH, CHUNK, D_KV), F32),
        jax.ShapeDtypeStruct((BATCH, SSM_G * SSM_P), F32),
        jax.ShapeDtypeStruct((BATCH, SSM_G * SSM_P), F32),
    ] + route_shapes
    out_specs = [
        pl.BlockSpec((ROWS, D_MODEL), lambda c: (c, 0)),
        _full((BATCH, CHUNK, D_KV)), _full((BATCH, CHUNK, D_KV)),
        _full((BATCH, SSM_G * SSM_P)), _full((BATCH, SSM_G * SSM_P)),
    ] + route_specs
    kv_scr = pltpu.VMEM((BATCH, CHUNK, D_KV), BF16)
    scratch = [
        pltpu.VMEM((ROWS, D_ATTN + 2 * D_KV), F32), pltpu.VMEM((D_SSM // 128, BATCH * PITCH, 128), F32),
        kv_scr, kv_scr, kv_scr, kv_scr,
        pltpu.VMEM((ROWS, D_ATTN), F32), pltpu.VMEM((D_SSM // 128, BATCH * PITCH, 128), F32),
        pltpu.VMEM((SUB_ROWS, D_SSM), F32), pltpu.VMEM((SUB_ROWS, 2 * SBLK), F32),
        pltpu.VMEM((SUB_ROWS, 2 * SBLK), F32), pltpu.VMEM((SUB_ROWS, D_SSM), F32),
    ]
    return pl.pallas_call(
        _mixer_prompt_kernel, grid=(N_CHUNKS,), in_specs=in_specs, out_specs=out_specs, out_shape=out_shape,
        scratch_shapes=scratch, name="mixer_prompt",
        compiler_params=pltpu.CompilerParams(dimension_semantics=("arbitrary",), vmem_limit_bytes=VMEM_LIMIT),
    )(sinks, x_prompt, gmix, win, bias, bblk, lamr, lami, cblk, dskip, wglu, bglu, ga, gs, wout,
      gffn, wrt, br, _tri(ROUTE_BLOCK))


SGRP = 16
SPITCH = DEC_BATCH + 8
N_SGRP = DEC_BATCH // SGRP


def _mixer_sample_kernel(x_ref, gmix_ref, win_ref, ck_ref, cv_ref, sbias_ref, sinkc_ref, bblk_ref, lamr_ref,
                         lami_ref, cblk_ref, dskip_ref, wglu_ref, bglu_ref, ga_ref, gs_ref, wout_ref,
                         x0r_ref, x0i_ref, gffn_ref, wrt_ref, br_ref, tri_ref, cnt_in_ref,
                         hbuf_ref, hnbuf_ref, idxbuf_ref, gwbuf_ref, rankbuf_ref,
                         hmid_ref, knew_ref, vnew_ref, sre_ref, sim_ref, hn_ref, idx_ref, gw_ref, rank_ref, cnt_ref,
                         proj_s, qall_s, oall_s, kn8_s, vn8_s):
    del hbuf_ref, hnbuf_ref, idxbuf_ref, gwbuf_ref, rankbuf_ref
    g = pl.program_id(0)
    lo = lax.broadcasted_iota(jnp.int32, (DEC_BATCH, 128), 1) < HEAD_DIM

    @pl.when(g == 0)
    def _():
        proj = _bdot(_rms(x_ref[...], gmix_ref[...]), win_ref[...])
        proj_s[...] = proj
        for h in range(N_HEADS):
            jq, half, kv = h // 2, h % 2, h // 4
            q2 = proj[:, 128 * jq:128 * (jq + 1)] * (HEAD_DIM ** -0.5)
            if half != kv:
                q2 = pltpu.roll(q2, HEAD_DIM, 1)
            hr = slice(h * SPITCH, h * SPITCH + DEC_BATCH)
            qall_s[hr, :] = jnp.where(lo if kv == 0 else jnp.logical_not(lo), q2, 0.0)
            kn8_s[hr, :] = proj[:, D_ATTN:D_ATTN + D_KV]
            vn8_s[hr, :] = proj[:, D_ATTN + D_KV:D_ATTN + 2 * D_KV]

    def head_rows(ref):
        return jnp.stack([ref[pl.ds(g * SGRP + ii, N_HEADS, stride=SPITCH), :] for ii in range(SGRP)], axis=0)

    qe = head_rows(qall_s)
    knew = head_rows(kn8_s)
    vnew = head_rows(vn8_s)
    s = jnp.einsum('bhd,bkd->bhk', qe.astype(BF16), ck_ref[...].astype(BF16),
                   preferred_element_type=F32) + sbias_ref[...][None]
    s_new = jnp.sum(qe * knew, axis=-1, keepdims=True)
    sink = sinkc_ref[...][None]
    m = jnp.maximum(jnp.maximum(jnp.max(s, axis=-1, keepdims=True), s_new), sink)
    p = jnp.exp(s - m)
    p_new = jnp.exp(s_new - m)
    den = jnp.sum(p, axis=-1, keepdims=True) + p_new + jnp.exp(sink - m)
    o = (jnp.einsum('bhk,bkd->bhd', p.astype(BF16), cv_ref[...].astype(BF16),
                    preferred_element_type=F32) + p_new * vnew) / den
    for ii in range(SGRP):
        oall_s[pl.ds(g * SGRP + ii, N_HEADS, stride=SPITCH), :] = o[ii]

    @pl.when(g == N_SGRP - 1)
    def _():
        proj = proj_s[...]
        blocks = []
        for jq in range(N_HEADS // 2):
            oa = oall_s[(2 * jq) * SPITCH:(2 * jq) * SPITCH + DEC_BATCH, :]
            ob = oall_s[(2 * jq + 1) * SPITCH:(2 * jq + 1) * SPITCH + DEC_BATCH, :]
            if jq // 2 == 0:
                blocks.append(jnp.where(lo, oa, pltpu.roll(ob, HEAD_DIM, 1)))
            else:
                blocks.append(jnp.where(lo, pltpu.roll(oa, HEAD_DIM, 1), ob))
        attn = jnp.concatenate(blocks, axis=1)
        u = proj[:, D_ATTN + 2 * D_KV:]
        ys = []
        for j in range(N_SBLK):
            bu = _bdot(u[:, 128 * j:128 * (j + 1)], bblk_ref[j])
            lr = lamr_ref[j]
            li = lami_ref[j]
            cols = slice(SBLK * j, SBLK * (j + 1))
            sr = x0r_ref[:, cols]
            si = x0i_ref[:, cols]
            nr = lr * sr - li * si + bu[:, 0:SBLK]
            ni = lr * si + li * sr + bu[:, SBLK:2 * SBLK]
            sre_ref[:, cols] = nr
            sim_ref[:, cols] = ni
            ys.append(_bdot(jnp.concatenate([nr, ni], axis=1), cblk_ref[j]))
        sn = _ssm_post(jnp.concatenate(ys, axis=1), u, dskip_ref[...], wglu_ref[...], bglu_ref[...], gs_ref[...])
        an = _rms(attn, ga_ref[...])
        h = x_ref[...] + _bdot(an, wout_ref[0:D_ATTN, :]) + _bdot(sn, wout_ref[D_ATTN:2 * D_ATTN, :])
        h = jnp.concatenate([h, jnp.zeros((DEC_BATCH, D_MODEL), F32)], axis=0)
        hmid_ref[...] = h
        cnt_ref[...] = cnt_in_ref[...]
        _route(h, gffn_ref, wrt_ref, br_ref, tri_ref, cnt_ref, hn_ref, idx_ref, gw_ref, rank_ref)
        knew_ref[...] = proj[:, D_ATTN:D_ATTN + D_KV]
        vnew_ref[...] = proj[:, D_ATTN + D_KV:D_ATTN + 2 * D_KV]


def _mixer_sample(x_s, gmix, win, ck, cv, sbias, sinkc, bblk, lamr, lami, cblk, dskip, wglu, bglu, ga, gs, wout,
                  x0r, x0i, gffn, wrt, br, cnt_in, hbuf, hnbuf, idxbuf, gwbuf, rankbuf):
    nst = SSM_G * SSM_P
    in_specs = [
        _full((DEC_BATCH, D_MODEL)), _full((1, D_MODEL)), _full((D_MODEL, D_IN)),
        pl.BlockSpec((SGRP, WINDOW, D_KV), lambda g: (g, 0, 0)),
        pl.BlockSpec((SGRP, WINDOW, D_KV), lambda g: (g, 0, 0)),
        _full((N_HEADS, WINDOW)), _full((N_HEADS, 1)),
        _full((N_SBLK, 128, 2 * SBLK)), _full((N_SBLK, 1, SBLK)), _full((N_SBLK, 1, SBLK)),
        _full((N_SBLK, 2 * SBLK, 128)), _full((1, D_SSM)), _full((D_SSM, D_SSM)), _full((1, D_SSM)),
        _full((1, D_ATTN)), _full((1, D_SSM)), _full((D_MODEL, D_MODEL)),
        _full((DEC_BATCH, nst)), _full((DEC_BATCH, nst)),
        _full((1, D_MODEL)), _full((N_EXPERTS, D_MODEL)), _full((N_EXPERTS, 1)),
        _full((2 * DEC_BATCH, 2 * DEC_BATCH)), _full((N_EXPERTS, 1)),
    ] + [pl.BlockSpec(memory_space=pl.ANY)] * 5
    tail_block = SAMPLE_ROW0 // (2 * DEC_BATCH)
    route_specs, route_shapes = _route_specs(2 * DEC_BATCH, lambda g: tail_block)
    out_shape = [
        jax.ShapeDtypeStruct((T_PAD, D_MODEL), F32),
        jax.ShapeDtypeStruct((DEC_BATCH, D_KV), F32), jax.ShapeDtypeStruct((DEC_BATCH, D_KV), F32),
        jax.ShapeDtypeStruct((DEC_BATCH, nst), F32), jax.ShapeDtypeStruct((DEC_BATCH, nst), F32),
    ] + route_shapes
    out_specs = [
        pl.BlockSpec((2 * DEC_BATCH, D_MODEL), lambda g: (tail_block, 0)),
        _full((DEC_BATCH, D_KV)), _full((DEC_BATCH, D_KV)),
        _full((DEC_BATCH, nst)), _full((DEC_BATCH, nst)),
    ] + route_specs
    head_rows = pltpu.VMEM((N_HEADS * SPITCH, 128), F32)
    scratch = [pltpu.VMEM((DEC_BATCH, D_IN), F32), head_rows, head_rows, head_rows, head_rows]
    return pl.pallas_call(
        _mixer_sample_kernel, grid=(N_SGRP,), in_specs=in_specs, out_specs=out_specs, out_shape=out_shape,
        scratch_shapes=scratch, input_output_aliases={24: 0, 25: 5, 26: 6, 27: 7, 28: 8}, name="mixer_sample",
        compiler_params=pltpu.CompilerParams(dimension_semantics=("arbitrary",), vmem_limit_bytes=VMEM_LIMIT),
    )(x_s, gmix, win, ck, cv, sbias, sinkc, bblk, lamr, lami, cblk, dskip, wglu, bglu, ga, gs, wout, x0r, x0i,
      gffn, wrt, br, _tri(2 * DEC_BATCH), cnt_in, hbuf, hnbuf, idxbuf, gwbuf, rankbuf)


def _route(h, g_ref, wrt_ref, br_ref, tri_ref, cnt_ref, hn_ref, idx_ref, gw_ref, rank_ref):
    hn = _rms(h, g_ref[...])
    hn_ref[...] = _pack_pairs(hn)
    hn_hi = hn.astype(BF16)
    hn_lo = (hn - hn_hi.astype(F32)).astype(BF16)
    w = wrt_ref[...]
    w_hi = w.astype(BF16)
    w_lo = (w - w_hi.astype(F32)).astype(BF16)
    nt = (((1,), (1,)), ((), ()))
    logits = (lax.dot_general(w_hi, hn_hi, nt, preferred_element_type=F32)
              + lax.dot_general(w_lo, hn_hi, nt, preferred_element_type=F32)
              + lax.dot_general(w_hi, hn_lo, nt, preferred_element_type=F32)) + br_ref[...]
    eidx = lax.broadcasted_iota(jnp.int32, logits.shape, 0)
    vals, onehots = [], []
    l = logits
    for k in range(TOP_K):
        m = jnp.max(l, axis=0, keepdims=True)
        ik = jnp.min(jnp.where(l == m, eidx, N_EXPERTS), axis=0, keepdims=True)
        oh = eidx == ik
        idx_ref[k:k + 1, :] = ik
        vals.append(m)
        onehots.append(oh)
        l = jnp.where(oh, -jnp.inf, l)
    exps = [jnp.exp(v - vals[0]) for v in vals]
    den = exps[0] + exps[1] + exps[2] + exps[3]
    for k in range(TOP_K):
        gw_ref[k:k + 1, :] = exps[k] / den
    member = jnp.zeros(logits.shape, F32)
    for oh in onehots:
        member = member + jnp.where(oh, 1.0, 0.0)
    wblk = tri_ref.shape[0]
    base = cnt_ref[...].astype(F32)
    befores = []
    for cb in range(h.shape[0] // wblk):
        mblk = member[:, cb * wblk:(cb + 1) * wblk]
        befores.append(jnp.dot(mblk.astype(BF16), tri_ref[...], preferred_element_type=F32) + base)
        base = base + jnp.sum(mblk, axis=1, keepdims=True)
    before = jnp.concatenate(befores, axis=1)
    for k in range(TOP_K):
        rank_ref[k:k + 1, :] = jnp.sum(jnp.where(onehots[k], before, 0.0), axis=0, keepdims=True).astype(jnp.int32)
    cnt_ref[...] = base.astype(jnp.int32)


def _route_specs(rows, block):
    specs = [pl.BlockSpec((rows, D_PACK), lambda i: (block(i), 0)), pl.BlockSpec((TOP_K, rows), lambda i: (0, block(i))),
             pl.BlockSpec((TOP_K, rows), lambda i: (0, block(i))), pl.BlockSpec((TOP_K, rows), lambda i: (0, block(i))),
             _full((N_EXPERTS, 1))]
    shapes = [jax.ShapeDtypeStruct((T_PAD, D_PACK), jnp.int32), jax.ShapeDtypeStruct((TOP_K, T_PAD), jnp.int32),
              jax.ShapeDtypeStruct((TOP_K, T_PAD), F32), jax.ShapeDtypeStruct((TOP_K, T_PAD), jnp.int32),
              jax.ShapeDtypeStruct((N_EXPERTS, 1), jnp.int32)]
    return specs, shapes


def _tri(n):
    return jnp.asarray(np.triu(np.ones((n, n), np.float32), 1), BF16)


def _place_kernel(off_ref, idx_ref, rank_ref, pos_ref):
    idx = idx_ref[...]
    pos = rank_ref[...]
    for e in range(N_EXPERTS):
        pos = pos + jnp.where(idx == e, off_ref[e], 0)
    pos_ref[...] = pos


def _place(offsets, idx, rank):
    return pl.pallas_call(
        _place_kernel,
        in_specs=[pl.BlockSpec(memory_space=pltpu.SMEM), pl.BlockSpec(memory_space=pltpu.VMEM),
                  pl.BlockSpec(memory_space=pltpu.VMEM)],
        out_specs=pl.BlockSpec(memory_space=pltpu.VMEM),
        out_shape=jax.ShapeDtypeStruct((TOP_K, T_PAD), jnp.int32), name="place",
    )(offsets, idx, rank)


def _sc_mesh():
    return plsc.VectorSubcoreMesh(core_axis_name="core", subcore_axis_name="subcore")


def _sc_dispatch(rows, pos):
    n, d = rows.shape
    nblk = n // SC_ROWS
    pos_w = pos.reshape(TOP_K, nblk, SC_ROWS).transpose(1, 0, 2)

    @functools.partial(pl.kernel, out_type=jax.ShapeDtypeStruct((P_ROWS, d), rows.dtype), mesh=_sc_mesh(),
                       scratch_types=[], name="dispatch")
    def run(x_hbm, i_hbm, o_hbm):
        def body(x_vmem, i_vmem):
            for k in range(TOP_K):
                pltpu.sync_copy(x_vmem, o_hbm.at[i_vmem.at[0, k]])

        pltpu.emit_pipeline(
            body, grid=(nblk,),
            in_specs=[pl.BlockSpec((SC_ROWS, d), lambda i: (i, 0)),
                      pl.BlockSpec((1, TOP_K, SC_ROWS), lambda i: (i, 0, 0))],
            out_specs=[], core_axis_name=("core", "subcore"), dimension_semantics=(pltpu.PARALLEL,),
        )(x_hbm, i_hbm)

    return run(rows, pos_w)


def _sc_combine(rows, flat_pos):
    _, d = rows.shape
    n = flat_pos.shape[0]
    w = SC_ROWS_COMBINE
    nblk = n // w
    pos_w = flat_pos.reshape(nblk, 1, w)

    @functools.partial(pl.kernel, out_type=jax.ShapeDtypeStruct((n, d), rows.dtype), mesh=_sc_mesh(),
                       scratch_types=[], name="combine")
    def run(x_hbm, i_hbm, o_hbm):
        def body(i_vmem, o_vmem):
            pltpu.sync_copy(x_hbm.at[i_vmem.at[0, 0]], o_vmem)

        pltpu.emit_pipeline(
            body, grid=(nblk,),
            in_specs=[pl.BlockSpec((1, 1, w), lambda i: (i, 0, 0))],
            out_specs=[pl.BlockSpec((w, d), lambda i: (i, 0))],
            core_axis_name=("core", "subcore"), dimension_semantics=(pltpu.PARALLEL,),
        )(i_hbm, o_hbm)

    return run(rows, pos_w)


def _experts_kernel(ts_ref, trow_ref, tsz_ref, xs_hbm, w1_hbm, b1_ref, w2_hbm, b2_ref, ys_hbm,
                    w1_st, w2_st, w1_s, w2_s, xbuf, ybuf, wsem, xsem, ysem):
    e = pl.program_id(0)
    n_valid = ts_ref[N_EXPERTS]

    def w_copies(ex, slot):
        return (pltpu.make_async_copy(w1_hbm.at[ex], w1_st.at[slot], wsem.at[0, slot]),
                pltpu.make_async_copy(w2_hbm.at[ex], w2_st.at[slot], wsem.at[1, slot]))

    def x_copy(t, slot, rows):
        src = xs_hbm.at[pl.ds(pl.multiple_of(trow_ref[t], TM_UNIT), rows)]
        return pltpu.make_async_copy(src, xbuf.at[slot, pl.ds(0, rows)], xsem.at[slot])

    def y_copy(t, slot, rows):
        dst = ys_hbm.at[pl.ds(pl.multiple_of(trow_ref[t], TM_UNIT), rows)]
        return pltpu.make_async_copy(ybuf.at[slot, pl.ds(0, rows)], dst, ysem.at[slot])

    def by_size(t, fn):
        for rows in TILE_SIZES:
            @pl.when(tsz_ref[t] == rows)
            def _():
                fn(rows)

    @pl.when(e == 0)
    def _():
        for c in w_copies(0, 0):
            c.start()
        by_size(0, lambda rows: x_copy(0, 0, rows).start())

    @pl.when(e + 1 < N_EXPERTS)
    def _():
        for c in w_copies(e + 1, (e + 1) % 2):
            c.start()

    wslot = e % 2
    for c in w_copies(e, wslot):
        c.wait()
    for r in range(4):
        rs = slice(256 * r, 256 * (r + 1))
        w1_s[rs, :] = w1_st[wslot, rs, :].astype(BF16)
        w2_s[rs, :] = w2_st[wslot, rs, :].astype(BF16)

    def tile(t, carry):
        slot = t % 2

        @pl.when(t + 1 < n_valid)
        def _():
            by_size(t + 1, lambda rows: x_copy(t + 1, 1 - slot, rows).start())

        by_size(t, lambda rows: x_copy(t, slot, rows).wait())

        @pl.when(t >= 2)
        def _():
            by_size(t - 2, lambda rows: y_copy(t - 2, slot, rows).wait())

        def compute(rows):
            hdn = _bdot(_unpack_pairs(xbuf[slot, 0:rows]), w1_s[...]) + b1_ref[0]
            gt = jnp.minimum(hdn[:, :D_FF], SWIGLU_LIMIT)
            up = jnp.clip(hdn[:, D_FF:], -SWIGLU_LIMIT, SWIGLU_LIMIT)
            act = (up + 1.0) * gt * _sigmoid(SWIGLU_ALPHA * gt)
            ybuf[slot, 0:rows] = _pack_pairs(_bdot(act, w2_s[...]) + b2_ref[0])
            y_copy(t, slot, rows).start()

        by_size(t, compute)
        return carry

    lax.fori_loop(ts_ref[e], ts_ref[e + 1], tile, 0)

    @pl.when(e == N_EXPERTS - 1)
    def _():
        @pl.when(n_valid >= 2)
        def _():
            by_size(n_valid - 2, lambda rows: y_copy(n_valid - 2, n_valid % 2, rows).wait())

        by_size(n_valid - 1, lambda rows: y_copy(n_valid - 1, (n_valid - 1) % 2, rows).wait())


def _tile_plan(counts):
    units = (counts + (TM_UNIT - 1)) // TM_UNIT
    unit_end = jnp.cumsum(units)
    offsets = ((unit_end - units) * TM_UNIT).astype(jnp.int32)
    per_full = TM // TM_UNIT
    n_full = units // per_full
    rest = units % per_full
    has_mid = (rest >= 2).astype(jnp.int32)
    tiles_per = n_full + has_mid + rest % 2
    tile_end = jnp.cumsum(tiles_per)
    tile_start = jnp.concatenate([jnp.zeros((1,), jnp.int32), tile_end.astype(jnp.int32)])
    t = jnp.arange(MAX_TILES, dtype=jnp.int32)[:, None]
    done = tile_end[None, :] <= t
    mine = jnp.sum(done.astype(jnp.int32), axis=1, keepdims=True) == jnp.arange(N_EXPERTS, dtype=jnp.int32)[None, :]
    pick = lambda v: jnp.sum(jnp.where(mine, v[None, :], 0), axis=1)
    j = t[:, 0] - jnp.max(jnp.where(done, tile_end[None, :], 0), axis=1)
    nf, mid = pick(n_full), pick(has_mid)
    rows = jnp.where(j < nf, TM, jnp.where((j == nf) & (mid == 1), TILE_SIZES[1], TILE_SIZES[2]))
    row0 = pick(offsets) + jnp.minimum(j, nf) * TM + jnp.where(j > nf, TILE_SIZES[1], 0)
    valid = t[:, 0] < tile_end[-1]
    tile_row = jnp.where(valid, row0, 0).astype(jnp.int32)
    tile_rows = jnp.where(valid, rows, 0).astype(jnp.int32)
    return offsets, tile_start, tile_row, tile_rows


def _experts(tile_start, tile_row, tile_rows, xs, w_up, b_up, w_down, b_down):
    wsel = lambda e, *_: (e, 0, 0)
    hbm = pl.BlockSpec(memory_space=pl.ANY)
    grid_spec = pltpu.PrefetchScalarGridSpec(
        num_scalar_prefetch=3, grid=(N_EXPERTS,),
        in_specs=[hbm, hbm, pl.BlockSpec((1, 1, 2 * D_FF), wsel), hbm, pl.BlockSpec((1, 1, D_MODEL), wsel)],
        out_specs=hbm,
        scratch_shapes=[pltpu.VMEM((2, D_MODEL, 2 * D_FF), F32), pltpu.VMEM((2, D_FF, D_MODEL), F32),
                        pltpu.VMEM((D_MODEL, 2 * D_FF), BF16), pltpu.VMEM((D_FF, D_MODEL), BF16),
                        pltpu.VMEM((2, TM, D_PACK), jnp.int32), pltpu.VMEM((2, TM, D_PACK), jnp.int32),
                        pltpu.SemaphoreType.DMA((2, 2)), pltpu.SemaphoreType.DMA((2,)),
                        pltpu.SemaphoreType.DMA((2,))],
    )
    return pl.pallas_call(
        _experts_kernel, grid_spec=grid_spec, out_shape=jax.ShapeDtypeStruct((P_ROWS, D_PACK), jnp.int32),
        name="experts",
        compiler_params=pltpu.CompilerParams(dimension_semantics=("arbitrary",), vmem_limit_bytes=VMEM_LIMIT),
    )(tile_start, tile_row, tile_rows, xs, w_up, b_up.reshape(N_EXPERTS, 1, 2 * D_FF), w_down,
      b_down.reshape(N_EXPERTS, 1, D_MODEL))


def _final_kernel(h_ref, yg_ref, gw_ref, p_ref, gple_ref, wg_ref, wp_ref, gfin_ref, *rest):
    out_ref = rest[-1]
    rows = h_ref.shape[0]
    gw = gw_ref[...]
    h = h_ref[...]
    for k in range(TOP_K):
        h = h + gw[:, k:k + 1] * _unpack_pairs(yg_ref[k])
    gate = _sigmoid(_bdot(_rms(h, gple_ref[...]), wg_ref[...]))
    h = h + gate * _bdot(p_ref[...].reshape(rows, PLE_DIM), wp_ref[...])
    out_ref[...] = _rms(h, gfin_ref[...]).reshape(out_ref.shape)


def _final_prompt(part, hmid, yg, gwt, p_prompt, gple, wg, wp, gfin, y_prev):
    nb = 4
    rows = nb * CHUNK
    nbh = BATCH // nb
    c0 = part * PART_CHUNKS
    rb = lambda c, b: ((c0 + c) * nbh + b, 0)
    in_specs = [pl.BlockSpec((rows, D_MODEL), rb),
                pl.BlockSpec((TOP_K, rows, D_PACK), lambda c, b: (0, c * nbh + b, 0)),
                pl.BlockSpec((rows, TOP_K), rb),
                pl.BlockSpec((nb, CHUNK, PLE_DIM), lambda c, b: (b, c0 + c, 0)),
                _full((1, D_MODEL)), _full((D_MODEL, D_MODEL)), _full((PLE_DIM, D_MODEL)), _full((1, D_MODEL))]
    args = [hmid, yg, gwt, p_prompt, gple, wg, wp, gfin]
    aliases = {}
    if y_prev is not None:
        in_specs.append(pl.BlockSpec(memory_space=pl.ANY))
        args.append(y_prev)
        aliases = {len(args) - 1: 0}
    return pl.pallas_call(
        _final_kernel, grid=(PART_CHUNKS, nbh), in_specs=in_specs,
        out_specs=pl.BlockSpec((nb, CHUNK, D_MODEL), lambda c, b: (b, c0 + c, 0)),
        out_shape=jax.ShapeDtypeStruct((BATCH, SEQ, D_MODEL), F32), name="final_prompt",
        input_output_aliases=aliases,
        compiler_params=pltpu.CompilerParams(dimension_semantics=("arbitrary", "arbitrary"),
                                             vmem_limit_bytes=VMEM_LIMIT),
    )(*args)


def _final_sample(hmid, yg, gwt, p_sample, gple, wg, wp, gfin):
    blk = SAMPLE_ROW0 // DEC_BATCH
    return pl.pallas_call(
        _final_kernel, grid=(1,),
        in_specs=[pl.BlockSpec((DEC_BATCH, D_MODEL), lambda i: (blk, 0)),
                  pl.BlockSpec((TOP_K, DEC_BATCH, D_PACK), lambda i: (0, PART_CHUNKS * ROWS // DEC_BATCH, 0)),
                  pl.BlockSpec((DEC_BATCH, TOP_K), lambda i: (blk, 0)),
                  _full((DEC_BATCH, PLE_DIM)),
                  _full((1, D_MODEL)), _full((D_MODEL, D_MODEL)), _full((PLE_DIM, D_MODEL)), _full((1, D_MODEL))],
        out_specs=_full((DEC_BATCH, D_MODEL)),
        out_shape=jax.ShapeDtypeStruct((DEC_BATCH, D_MODEL), F32), name="final_sample",
        compiler_params=pltpu.CompilerParams(dimension_semantics=("arbitrary",), vmem_limit_bytes=VMEM_LIMIT),
    )(hmid, yg, gwt, p_sample, gple, wg, wp, gfin)


def _alibi_tables():
    slopes = 2.0 ** (-8.0 * (np.arange(N_HEADS, dtype=np.float64) + 1.0) / N_HEADS)
    i = np.arange(CHUNK)[:, None]
    j = np.arange(2 * CHUNK)[None, :]
    dist = i + CHUNK - j
    valid = (dist >= 0) & (dist <= WINDOW)
    tabs = []
    for has_prev in (False, True):
        ok = valid & ((j >= CHUNK) | has_prev)
        tabs.append(np.where(ok[None], -slopes[:, None, None] * dist[None], NEG))
    prompt = np.stack([np.stack([np.concatenate([t[h] for h in grp], axis=0) for grp in (NAT_HEADS, ROT_HEADS)])
                       for t in tabs]).astype(np.float32)
    wb = min(WINDOW, PAST_LEN)
    sample = (-slopes[:, None] * (wb - np.arange(wb))[None, :]).astype(np.float32)
    return prompt, sample


def kernel(x_prompt, x_sample, cache_k_win, cache_v_win, state_ssm_re, state_ssm_im, p_prompt, p_sample, norm_mix, w_in, sinks, ssm_lam_re, ssm_lam_im, ssm_log_step, ssm_b_re, ssm_b_im, ssm_c_re, ssm_c_im, ssm_d, w_glu, b_glu, norm_attn_out, norm_ssm_out, w_out, norm_ffn, w_router, b_router, w_up, b_up, w_down, b_down, norm_ple, w_ple_gate, w_ple_proj, norm_final):
    nst = SSM_G * SSM_P
    bias_np, sbias_np = _alibi_tables()
    bias = jnp.asarray(bias_np)
    sbias = jnp.asarray(sbias_np)

    lbr, lbi, bbr, bbi = _prep(ssm_lam_re[0], ssm_lam_im[0], ssm_log_step[0], ssm_b_re[0], ssm_b_im[0])
    bblk, lamr, lami, cblk = _s5_blocks(lbr, lbi, bbr, bbi, ssm_c_re[0], ssm_c_im[0])

    gmix = norm_mix[0].reshape(1, D_MODEL)
    win = w_in[0].astype(BF16)
    dskip = ssm_d[0].reshape(1, D_SSM)
    wglu = w_glu[0].astype(BF16)
    bglu = b_glu[0].reshape(1, D_SSM)
    ga = norm_attn_out[0].reshape(1, D_ATTN)
    gs = norm_ssm_out[0].reshape(1, D_SSM)
    wout = w_out[0].astype(BF16)
    sink = sinks[0]

    gffn = norm_ffn[0].reshape(1, D_MODEL)
    wrt = w_router[0].T
    br = b_router[0].reshape(N_EXPERTS, 1)
    hbuf, k_last, v_last, re_p, im_p, hnbuf, idxbuf, gwbuf, rankbuf, cnt_p = _mixer_prompt(
        x_prompt, sink, gmix, win, bias, bblk, lamr, lami, cblk, dskip, wglu, bglu, ga, gs, wout, gffn, wrt, br)

    ck = cache_k_win[0].reshape(DEC_BATCH, WINDOW, D_KV)
    cv = cache_v_win[0].reshape(DEC_BATCH, WINDOW, D_KV)
    hmid, k_new, v_new, re_s, im_s, hn, idx, gw, rank, counts = _mixer_sample(
        x_sample.reshape(DEC_BATCH, D_MODEL), gmix, win, ck, cv, sbias, sink.reshape(N_HEADS, 1), bblk, lamr, lami,
        cblk, dskip, wglu, bglu, ga, gs, wout, state_ssm_re[0].reshape(DEC_BATCH, nst),
        state_ssm_im[0].reshape(DEC_BATCH, nst), gffn, wrt, br, cnt_p, hbuf, hnbuf, idxbuf, gwbuf, rankbuf)

    offsets, tile_start, tile_row, tile_rows = _tile_plan(counts[:, 0])

    pos = _place(offsets, idx, rank)
    xs = _sc_dispatch(hn, pos)
    ys = _experts(tile_start, tile_row, tile_rows, xs, w_up[0], b_up[0], w_down[0], b_down[0])
    ygs = []
    for part in range(N_PARTS):
        r0 = part * PART_CHUNKS * ROWS
        r1 = T_PAD if part == N_PARTS - 1 else r0 + PART_CHUNKS * ROWS
        ygs.append(_sc_combine(ys, pos[:, r0:r1].reshape(TOP_K * (r1 - r0))).reshape(TOP_K, r1 - r0, D_PACK))

    gwt = gw.T
    gple = norm_ple[0].reshape(1, D_MODEL)
    wg = w_ple_gate[0].astype(BF16)
    wp = w_ple_proj[0].astype(BF16)
    gfin = norm_final.reshape(1, D_MODEL)
    y_prompt = None
    for part in range(N_PARTS):
        y_prompt = _final_prompt(part, hmid, ygs[part], gwt, p_prompt[0], gple, wg, wp, gfin, y_prompt)
    y_sample = _final_sample(hmid, ygs[-1], gwt, p_sample[0].reshape(DEC_BATCH, PLE_DIM), gple, wg, wp, gfin)

    k_win_s = jnp.concatenate([ck[:, 1:], k_new[:, None, :]], axis=1)
    v_win_s = jnp.concatenate([cv[:, 1:], v_new[:, None, :]], axis=1)
    kv5 = (1, BATCH, CHUNK, N_KV, HEAD_DIM)
    skv5 = (1, DEC_BATCH, WINDOW, N_KV, HEAD_DIM)
    return (y_prompt, y_sample.reshape(DEC_BATCH, 1, D_MODEL),
            k_last.reshape(kv5), v_last.reshape(kv5),
            re_p.reshape(1, BATCH, SSM_G, SSM_P), im_p.reshape(1, BATCH, SSM_G, SSM_P),
            k_win_s.reshape(skv5), v_win_s.reshape(skv5),
            re_s.reshape(1, DEC_BATCH, SSM_G, SSM_P), im_s.reshape(1, DEC_BATCH, SSM_G, SSM_P))
```

```python
import functools

import numpy as np
import jax
import jax.numpy as jnp
from jax import lax
from jax.experimental import pallas as pl
from jax.experimental.pallas import tpu as pltpu
from jax.experimental.pallas import tpu_sc as plsc

F32 = jnp.float32
BF16 = jnp.bfloat16

D_MODEL = 1024
BATCH = 8
SEQ = 2048
DEC_BATCH = 128
PAST_LEN = 16384
HEAD_DIM = 64
D_ATTN = 512
N_HEADS = 8
N_KV = 2
D_KV = N_KV * HEAD_DIM
WINDOW = 128
D_SSM = 512
SSM_H = 16
SSM_G = 32
SSM_P = 64
D_IN = D_ATTN + 2 * D_KV + D_SSM
N_EXPERTS = 32
TOP_K = 4
D_FF = 1024
SWIGLU_LIMIT = 7.0
SWIGLU_ALPHA = 1.702
PLE_DIM = 256
EPS = 1e-5
NEG = -1e30

CHUNK = 128
N_CHUNKS = SEQ // CHUNK
ROWS = BATCH * CHUNK
PITCH = CHUNK + 8
NAT_HEADS = (0, 2, 5, 7)
ROT_HEADS = (1, 3, 4, 6)
ATTN_UNROLL = 2
SUB_T = 64
SUB_ROWS = SUB_T * BATCH
N_SBLK = 4
SBLK = 512
T_REAL = BATCH * SEQ + DEC_BATCH
T_PAD = T_REAL + 128
SAMPLE_ROW0 = BATCH * SEQ
ROUTE_BLOCK = 512
TILE_SIZES = (1024, 512, 256)
TM = TILE_SIZES[0]
TM_UNIT = TILE_SIZES[-1]
P_ROWS = (T_PAD * TOP_K + N_EXPERTS * (TM_UNIT - 1)) // TM_UNIT * TM_UNIT
MAX_TILES = P_ROWS // TM + 2 * N_EXPERTS + 1
SC_ROWS = 40
SC_ROWS_COMBINE = 32
N_PARTS = 2
PART_CHUNKS = N_CHUNKS // N_PARTS
D_PACK = D_MODEL // 2
VMEM_LIMIT = 56 * 1024 * 1024


def _rms(x, g):
    return x * lax.rsqrt(jnp.mean(x * x, axis=-1, keepdims=True) + EPS) * g


def _sigmoid(x):
    return 1.0 / (1.0 + jnp.exp(-x))


def _gelu_tanh(x):
    c = np.float32(np.sqrt(2.0 / np.pi))
    return 0.5 * x * (1.0 + jnp.tanh(c * (x + 0.044715 * (x * x * x))))


def _bdot(a, b):
    return jnp.dot(a.astype(BF16), b, preferred_element_type=F32)


def _pack_pairs(x):
    n = x.shape[1] // 2
    lo = lax.bitcast_convert_type(x[:, :n].astype(BF16).astype(F32), jnp.int32)
    hi = lax.bitcast_convert_type(x[:, n:].astype(BF16).astype(F32), jnp.int32)
    return lax.shift_right_logical(lo, 16) | hi


def _unpack_pairs(w):
    lo = lax.bitcast_convert_type(lax.shift_left(w, 16), F32)
    hi = lax.bitcast_convert_type(w & jnp.int32(-65536), F32)
    return jnp.concatenate([lo, hi], axis=1)


def _full(shape):
    n = len(shape)
    return pl.BlockSpec(shape, lambda *_: (0,) * n)


def _prep_kernel(lr_ref, li_ref, ls_ref, br_ref, bi_ref, lbr_ref, lbi_ref, bbr_ref, bbi_ref):
    lr = lr_ref[...]
    li = li_ref[...]
    step = jnp.exp(ls_ref[...])
    zr = lr * step
    zi = li * step
    mag = jnp.exp(zr)
    lbr = mag * jnp.cos(zi)
    lbi = mag * jnp.sin(zi)
    lbr_ref[...] = lbr
    lbi_ref[...] = lbi
    nr = lbr - 1.0
    den = lr * lr + li * li
    cr = (nr * lr + lbi * li) / den
    ci = (lbi * lr - nr * li) / den
    br = br_ref[...]
    bi = bi_ref[...]
    bbr_ref[...] = cr * br - ci * bi
    bbi_ref[...] = cr * bi + ci * br


def _prep(lam_re, lam_im, log_step, b_re, b_im):
    g, p, h = SSM_G, SSM_P, SSM_H
    out = pl.pallas_call(
        _prep_kernel,
        out_shape=[jax.ShapeDtypeStruct((g, 1, p), F32), jax.ShapeDtypeStruct((g, 1, p), F32),
                   jax.ShapeDtypeStruct((g, h, p), F32), jax.ShapeDtypeStruct((g, h, p), F32)],
        name="s5_prep",
    )(lam_re.reshape(g, 1, p), lam_im.reshape(g, 1, p), log_step.reshape(g, 1, 1),
      jnp.transpose(b_re, (0, 2, 1)), jnp.transpose(b_im, (0, 2, 1)))
    return out


def _s5_blocks(lbr, lbi, bbr, bbi, c_re, c_im):
    eye = jnp.eye(8, dtype=F32)
    shp = (N_SBLK, 8, SSM_H, SSM_P)

    def in_map(b):
        return (b.reshape(shp)[:, :, :, None, :] * eye[None, :, None, :, None]).reshape(N_SBLK, 128, SBLK)

    def out_map(c):
        ct = jnp.transpose(c.reshape(shp), (0, 1, 3, 2))
        return (ct[:, :, :, None, :] * eye[None, :, None, :, None]).reshape(N_SBLK, SBLK, 128)

    bblk = jnp.concatenate([in_map(bbr), in_map(bbi)], axis=-1).astype(BF16)
    cblk = jnp.concatenate([out_map(c_re), -out_map(c_im)], axis=1).astype(BF16)
    return bblk, lbr.reshape(N_SBLK, 1, SBLK), lbi.reshape(N_SBLK, 1, SBLK), cblk


def _ssm_post(y_lin, u, dskip, wglu, bglu, gs):
    y = _gelu_tanh(y_lin + dskip * u)
    y = y * _sigmoid(_bdot(y, wglu) + bglu)
    return _rms(y, gs)


def _mixer_prompt_kernel(sinks_ref, x_ref, gmix_ref, win_ref, bias_ref, bblk_ref, lamr_ref, lami_ref,
                         cblk_ref, dskip_ref, wglu_ref, bglu_ref, ga_ref, gs_ref, wout_ref,
                         gffn_ref, wrt_ref, br_ref, tri_ref,
                         hmid_ref, klast_ref, vlast_ref, sre_ref, sim_ref, hn_ref, idx_ref, gw_ref, rank_ref, cnt_ref,
                         proj_s, u_s, kprev_s, kprevr_s, vprev_s, vprevr_s, attn_s, ssm_s, utb_s, bu_s, xs_s, ytb_s):
    c = pl.program_id(0)

    @pl.when(c == 0)
    def _():
        zkv = jnp.zeros(kprev_s.shape, BF16)
        kprev_s[...] = zkv
        kprevr_s[...] = zkv
        vprev_s[...] = zkv
        vprevr_s[...] = zkv
        sre_ref[...] = jnp.zeros(sre_ref.shape, F32)
        sim_ref[...] = jnp.zeros(sim_ref.shape, F32)
        cnt_ref[...] = jnp.zeros(cnt_ref.shape, jnp.int32)

    x = x_ref[...].reshape(ROWS, D_MODEL)
    proj = _bdot(_rms(x, gmix_ref[...]), win_ref[...])
    u0 = D_ATTN + 2 * D_KV
    proj_s[...] = proj[:, 0:u0]
    for jb in range(D_SSM // 128):
        for b in range(BATCH):
            u_s[jb, b * PITCH:b * PITCH + CHUNK, :] = proj[b * CHUNK:(b + 1) * CHUNK, u0 + 128 * jb:u0 + 128 * (jb + 1)]

    lo = lax.broadcasted_iota(jnp.int32, (CHUNK, 128), 1) < HEAD_DIM
    hi = jnp.logical_not(lo)
    table = jnp.minimum(c, 1)
    hrow = lax.broadcasted_iota(jnp.int32, (4 * CHUNK, 1), 0) // CHUNK

    def sink_col(heads):
        col = jnp.full((4 * CHUNK, 1), sinks_ref[heads[3]], F32)
        for n in (2, 1, 0):
            col = jnp.where(hrow == n, sinks_ref[heads[n]], col)
        return col

    sink_nat = sink_col(NAT_HEADS)
    sink_rot = sink_col(ROT_HEADS)

    def attend(q, k, v, bias, sink):
        s = lax.dot_general(q, k, (((1,), (1,)), ((), ())), preferred_element_type=F32) + bias
        m = jnp.maximum(jnp.max(s, axis=-1, keepdims=True), sink)
        p = jnp.exp(s - m)
        den = jnp.sum(p, axis=-1, keepdims=True) + jnp.exp(sink - m)
        return jnp.dot(p.astype(BF16), v, preferred_element_type=F32) / den

    def attn_body(b, carry):
        r0 = pl.multiple_of(b * CHUNK, CHUNK)
        rows = pl.ds(r0, CHUNK)
        kb = proj_s[rows, D_ATTN:D_ATTN + D_KV]
        vb = proj_s[rows, D_ATTN + D_KV:D_ATTN + 2 * D_KV]
        kb16 = kb.astype(BF16)
        vb16 = vb.astype(BF16)
        kbr16 = pltpu.roll(kb, HEAD_DIM, 1).astype(BF16)
        vbr16 = pltpu.roll(vb, HEAD_DIM, 1).astype(BF16)
        k_nat = jnp.concatenate([kprev_s[b], kb16], axis=0)
        k_rot = jnp.concatenate([kprevr_s[b], kbr16], axis=0)
        v_nat = jnp.concatenate([vprev_s[b], vb16], axis=0)
        v_rot = jnp.concatenate([vprevr_s[b], vbr16], axis=0)
        q2 = [proj_s[rows, 128 * jq:128 * (jq + 1)] * (HEAD_DIM ** -0.5) for jq in range(N_HEADS // 2)]
        q_nat = jnp.concatenate([jnp.where(lo if h % 2 == 0 else hi, q2[h // 2], 0.0) for h in NAT_HEADS],
                                axis=0).astype(BF16)
        q_rot = jnp.concatenate([jnp.where(lo if h % 2 == 0 else hi, q2[h // 2], 0.0) for h in ROT_HEADS],
                                axis=0).astype(BF16)
        o_nat = attend(q_nat, k_nat, v_nat, bias_ref[table, 0], sink_nat)
        o_rot = attend(q_rot, k_rot, v_rot, bias_ref[table, 1], sink_rot)
        for jq in range(N_HEADS // 2):
            blk = slice(CHUNK * jq, CHUNK * (jq + 1))
            even, odd = (o_nat, o_rot) if jq < 2 else (o_rot, o_nat)
            attn_s[rows, 128 * jq:128 * (jq + 1)] = jnp.where(lo, even[blk], odd[blk])
        kprev_s[b] = kb16
        kprevr_s[b] = kbr16
        vprev_s[b] = vb16
        vprevr_s[b] = vbr16
        return carry

    def attn_group(i, carry):
        for n in range(ATTN_UNROLL):
            attn_body(ATTN_UNROLL * i + n, carry)
        return carry

    lax.fori_loop(0, BATCH // ATTN_UNROLL, attn_group, 0)

    for sc in range(CHUNK // SUB_T):
        t0 = sc * SUB_T
        for i in range(SUB_T):
            for jb in range(D_SSM // 128):
                utb_s[i * BATCH:(i + 1) * BATCH, 128 * jb:128 * (jb + 1)] = (
                    u_s[jb, pl.ds(t0 + i, BATCH, stride=PITCH), :])
        u_tb = utb_s[...]
        for j in range(N_SBLK):
            bu_s[...] = _bdot(u_tb[:, 128 * j:128 * (j + 1)], bblk_ref[j])
            lr = jnp.broadcast_to(lamr_ref[j], (BATCH, SBLK))
            li = jnp.broadcast_to(lami_ref[j], (BATCH, SBLK))
            cols = slice(SBLK * j, SBLK * (j + 1))

            def step(i, carry):
                sr, si = carry
                r = pl.ds(pl.multiple_of(i * BATCH, BATCH), BATCH)
                nr = lr * sr - li * si + bu_s[r, 0:SBLK]
                ni = lr * si + li * sr + bu_s[r, SBLK:2 * SBLK]
                xs_s[r, 0:SBLK] = nr
                xs_s[r, SBLK:2 * SBLK] = ni
                return nr, ni

            sr, si = lax.fori_loop(0, SUB_T, step, (sre_ref[:, cols], sim_ref[:, cols]), unroll=True)
            sre_ref[:, cols] = sr
            sim_ref[:, cols] = si
            ytb_s[:, 128 * j:128 * (j + 1)] = _bdot(xs_s[...], cblk_ref[j])
        yn = _ssm_post(ytb_s[...], u_tb, dskip_ref[...], wglu_ref[...], bglu_ref[...], gs_ref[...])
        for i in range(SUB_T):
            for jb in range(D_SSM // 128):
                ssm_s[jb, pl.ds(t0 + i, BATCH, stride=PITCH), :] = (
                    yn[i * BATCH:(i + 1) * BATCH, 128 * jb:128 * (jb + 1)])

    an = _rms(attn_s[...], ga_ref[...])
    sn = jnp.concatenate(
        [jnp.concatenate([ssm_s[jb, b * PITCH:b * PITCH + CHUNK, :] for b in range(BATCH)], axis=0)
         for jb in range(D_SSM // 128)], axis=1)
    h = x + _bdot(an, wout_ref[0:D_ATTN, :]) + _bdot(sn, wout_ref[D_ATTN:2 * D_ATTN, :])
    hmid_ref[...] = h
    _route(h, gffn_ref, wrt_ref, br_ref, tri_ref, cnt_ref, hn_ref, idx_ref, gw_ref, rank_ref)

    @pl.when(c == N_CHUNKS - 1)
    def _():
        klast_ref[...] = proj_s[:, D_ATTN:D_ATTN + D_KV].reshape(BATCH, CHUNK, D_KV)
        vlast_ref[...] = proj_s[:, D_ATTN + D_KV:D_ATTN + 2 * D_KV].reshape(BATCH, CHUNK, D_KV)


def _mixer_prompt(x_prompt, sinks, gmix, win, bias, bblk, lamr, lami, cblk, dskip, wglu, bglu, ga, gs, wout,
                  gffn, wrt, br):
    smem = pl.BlockSpec(memory_space=pltpu.SMEM)
    in_specs = [
        smem,
        pl.BlockSpec((BATCH, CHUNK, D_MODEL), lambda c: (0, c, 0)),
        _full((1, D_MODEL)), _full((D_MODEL, D_IN)), _full((2, 2, 4 * CHUNK, 2 * CHUNK)),
        _full((N_SBLK, 128, 2 * SBLK)), _full((N_SBLK, 1, SBLK)), _full((N_SBLK, 1, SBLK)),
        _full((N_SBLK, 2 * SBLK, 128)), _full((1, D_SSM)), _full((D_SSM, D_SSM)), _full((1, D_SSM)),
        _full((1, D_ATTN)), _full((1, D_SSM)), _full((D_MODEL, D_MODEL)),
        _full((1, D_MODEL)), _full((N_EXPERTS, D_MODEL)), _full((N_EXPERTS, 1)), _full((ROUTE_BLOCK, ROUTE_BLOCK)),
    ]
    route_specs, route_shapes = _route_specs(ROWS, lambda c: c)
    out_shape = [
        jax.ShapeDtypeStruct((T_PAD, D_MODEL), F32),
        jax.ShapeDtypeStruct((BATCH, CHUNK, D_KV), F32),
        jax.ShapeDtypeStruct((BATCH, CHUNK, D_KV), F32),
        jax.ShapeDtypeStruct((BATCH, SSM_G * SSM_P), F32),
        jax.ShapeDtypeStruct((BATCH, SSM_G * SSM_P), F32),
    ] + route_shapes
    out_specs = [
        pl.BlockSpec((ROWS, D_MODEL), lambda c: (c, 0)),
        _full((BATCH, CHUNK, D_KV)), _full((BATCH, CHUNK, D_KV)),
        _full((BATCH, SSM_G * SSM_P)), _full((BATCH, SSM_G * SSM_P)),
    ] + route_specs
    kv_scr = pltpu.VMEM((BATCH, CHUNK, D_KV), BF16)
    scratch = [
        pltpu.VMEM((ROWS, D_ATTN + 2 * D_KV), F32), pltpu.VMEM((D_SSM // 128, BATCH * PITCH, 128), F32),
        kv_scr, kv_scr, kv_scr, kv_scr,
        pltpu.VMEM((ROWS, D_ATTN), F32), pltpu.VMEM((D_SSM // 128, BATCH * PITCH, 128), F32),
        pltpu.VMEM((SUB_ROWS, D_SSM), F32), pltpu.VMEM((SUB_ROWS, 2 * SBLK), F32),
        pltpu.VMEM((SUB_ROWS, 2 * SBLK), F32), pltpu.VMEM((SUB_ROWS, D_SSM), F32),
    ]
    return pl.pallas_call(
        _mixer_prompt_kernel, grid=(N_CHUNKS,), in_specs=in_specs, out_specs=out_specs, out_shape=out_shape,
        scratch_shapes=scratch, name="mixer_prompt",
        compiler_params=pltpu.CompilerParams(dimension_semantics=("arbitrary",), vmem_limit_bytes=VMEM_LIMIT),
    )(sinks, x_prompt, gmix, win, bias, bblk, lamr, lami, cblk, dskip, wglu, bglu, ga, gs, wout,
      gffn, wrt, br, _tri(ROUTE_BLOCK))


SGRP = 16
SPITCH = DEC_BATCH + 8
N_SGRP = DEC_BATCH // SGRP


def _mixer_sample_kernel(x_ref, gmix_ref, win_ref, ck_ref, cv_ref, sbias_ref, sinkc_ref, bblk_ref, lamr_ref,
                         lami_ref, cblk_ref, dskip_ref, wglu_ref, bglu_ref, ga_ref, gs_ref, wout_ref,
                         x0r_ref, x0i_ref, gffn_ref, wrt_ref, br_ref, tri_ref, cnt_in_ref,
                         hbuf_ref, hnbuf_ref, idxbuf_ref, gwbuf_ref, rankbuf_ref,
                         hmid_ref, knew_ref, vnew_ref, sre_ref, sim_ref, hn_ref, idx_ref, gw_ref, rank_ref, cnt_ref,
                         proj_s, qall_s, oall_s, kn8_s, vn8_s):
    del hbuf_ref, hnbuf_ref, idxbuf_ref, gwbuf_ref, rankbuf_ref
    g = pl.program_id(0)
    lo = lax.broadcasted_iota(jnp.int32, (DEC_BATCH, 128), 1) < HEAD_DIM

    @pl.when(g == 0)
    def _():
        proj = _bdot(_rms(x_ref[...], gmix_ref[...]), win_ref[...])
        proj_s[...] = proj
        for h in range(N_HEADS):
            jq, half, kv = h // 2, h % 2, h // 4
            q2 = proj[:, 128 * jq:128 * (jq + 1)] * (HEAD_DIM ** -0.5)
            if half != kv:
                q2 = pltpu.roll(q2, HEAD_DIM, 1)
            hr = slice(h * SPITCH, h * SPITCH + DEC_BATCH)
            qall_s[hr, :] = jnp.where(lo if kv == 0 else jnp.logical_not(lo), q2, 0.0)
            kn8_s[hr, :] = proj[:, D_ATTN:D_ATTN + D_KV]
            vn8_s[hr, :] = proj[:, D_ATTN + D_KV:D_ATTN + 2 * D_KV]

    def head_rows(ref):
        return jnp.stack([ref[pl.ds(g * SGRP + ii, N_HEADS, stride=SPITCH), :] for ii in range(SGRP)], axis=0)

    qe = head_rows(qall_s)
    knew = head_rows(kn8_s)
    vnew = head_rows(vn8_s)
    s = jnp.einsum('bhd,bkd->bhk', qe.astype(BF16), ck_ref[...].astype(BF16),
                   preferred_element_type=F32) + sbias_ref[...][None]
    s_new = jnp.sum(qe * knew, axis=-1, keepdims=True)
    sink = sinkc_ref[...][None]
    m = jnp.maximum(jnp.maximum(jnp.max(s, axis=-1, keepdims=True), s_new), sink)
    p = jnp.exp(s - m)
    p_new = jnp.exp(s_new - m)
    den = jnp.sum(p, axis=-1, keepdims=True) + p_new + jnp.exp(sink - m)
    o = (jnp.einsum('bhk,bkd->bhd', p.astype(BF16), cv_ref[...].astype(BF16),
                    preferred_element_type=F32) + p_new * vnew) / den
    for ii in range(SGRP):
        oall_s[pl.ds(g * SGRP + ii, N_HEADS, stride=SPITCH), :] = o[ii]

    @pl.when(g == N_SGRP - 1)
    def _():
        proj = proj_s[...]
        blocks = []
        for jq in range(N_HEADS // 2):
            oa = oall_s[(2 * jq) * SPITCH:(2 * jq) * SPITCH + DEC_BATCH, :]
            ob = oall_s[(2 * jq + 1) * SPITCH:(2 * jq + 1) * SPITCH + DEC_BATCH, :]
            if jq // 2 == 0:
                blocks.append(jnp.where(lo, oa, pltpu.roll(ob, HEAD_DIM, 1)))
            else:
                blocks.append(jnp.where(lo, pltpu.roll(oa, HEAD_DIM, 1), ob))
        attn = jnp.concatenate(blocks, axis=1)
        u = proj[:, D_ATTN + 2 * D_KV:]
        ys = []
        for j in range(N_SBLK):
            bu = _bdot(u[:, 128 * j:128 * (j + 1)], bblk_ref[j])
            lr = lamr_ref[j]
            li = lami_ref[j]
            cols = slice(SBLK * j, SBLK * (j + 1))
            sr = x0r_ref[:, cols]
            si = x0i_ref[:, cols]
            nr = lr * sr - li * si + bu[:, 0:SBLK]
            ni = lr * si + li * sr + bu[:, SBLK:2 * SBLK]
            sre_ref[:, cols] = nr
            sim_ref[:, cols] = ni
            ys.append(_bdot(jnp.concatenate([nr, ni], axis=1), cblk_ref[j]))
        sn = _ssm_post(jnp.concatenate(ys, axis=1), u, dskip_ref[...], wglu_ref[...], bglu_ref[...], gs_ref[...])
        an = _rms(attn, ga_ref[...])
        h = x_ref[...] + _bdot(an, wout_ref[0:D_ATTN, :]) + _bdot(sn, wout_ref[D_ATTN:2 * D_ATTN, :])
        h = jnp.concatenate([h, jnp.zeros((DEC_BATCH, D_MODEL), F32)], axis=0)
        hmid_ref[...] = h
        cnt_ref[...] = cnt_in_ref[...]
        _route(h, gffn_ref, wrt_ref, br_ref, tri_ref, cnt_ref, hn_ref, idx_ref, gw_ref, rank_ref)
        knew_ref[...] = proj[:, D_ATTN:D_ATTN + D_KV]
        vnew_ref[...] = proj[:, D_ATTN + D_KV:D_ATTN + 2 * D_KV]


def _mixer_sample(x_s, gmix, win, ck, cv, sbias, sinkc, bblk, lamr, lami, cblk, dskip, wglu, bglu, ga, gs, wout,
                  x0r, x0i, gffn, wrt, br, cnt_in, hbuf, hnbuf, idxbuf, gwbuf, rankbuf):
    nst = SSM_G * SSM_P
    in_specs = [
        _full((DEC_BATCH, D_MODEL)), _full((1, D_MODEL)), _full((D_MODEL, D_IN)),
        pl.BlockSpec((SGRP, WINDOW, D_KV), lambda g: (g, 0, 0)),
        pl.BlockSpec((SGRP, WINDOW, D_KV), lambda g: (g, 0, 0)),
        _full((N_HEADS, WINDOW)), _full((N_HEADS, 1)),
        _full((N_SBLK, 128, 2 * SBLK)), _full((N_SBLK, 1, SBLK)), _full((N_SBLK, 1, SBLK)),
        _full((N_SBLK, 2 * SBLK, 128)), _full((1, D_SSM)), _full((D_SSM, D_SSM)), _full((1, D_SSM)),
        _full((1, D_ATTN)), _full((1, D_SSM)), _full((D_MODEL, D_MODEL)),
        _full((DEC_BATCH, nst)), _full((DEC_BATCH, nst)),
        _full((1, D_MODEL)), _full((N_EXPERTS, D_MODEL)), _full((N_EXPERTS, 1)),
        _full((2 * DEC_BATCH, 2 * DEC_BATCH)), _full((N_EXPERTS, 1)),
    ] + [pl.BlockSpec(memory_space=pl.ANY)] * 5
    tail_block = SAMPLE_ROW0 // (2 * DEC_BATCH)
    route_specs, route_shapes = _route_specs(2 * DEC_BATCH, lambda g: tail_block)
    out_shape = [
        jax.ShapeDtypeStruct((T_PAD, D_MODEL), F32),
        jax.ShapeDtypeStruct((DEC_BATCH, D_KV), F32), jax.ShapeDtypeStruct((DEC_BATCH, D_KV), F32),
        jax.ShapeDtypeStruct((DEC_BATCH, nst), F32), jax.ShapeDtypeStruct((DEC_BATCH, nst), F32),
    ] + route_shapes
    out_specs = [
        pl.BlockSpec((2 * DEC_BATCH, D_MODEL), lambda g: (tail_block, 0)),
        _full((DEC_BATCH, D_KV)), _full((DEC_BATCH, D_KV)),
        _full((DEC_BATCH, nst)), _full((DEC_BATCH, nst)),
    ] + route_specs
    head_rows = pltpu.VMEM((N_HEADS * SPITCH, 128), F32)
    scratch = [pltpu.VMEM((DEC_BATCH, D_IN), F32), head_rows, head_rows, head_rows, head_rows]
    return pl.pallas_call(
        _mixer_sample_kernel, grid=(N_SGRP,), in_specs=in_specs, out_specs=out_specs, out_shape=out_shape,
        scratch_shapes=scratch, input_output_aliases={24: 0, 25: 5, 26: 6, 27: 7, 28: 8}, name="mixer_sample",
        compiler_params=pltpu.CompilerParams(dimension_semantics=("arbitrary",), vmem_limit_bytes=VMEM_LIMIT),
    )(x_s, gmix, win, ck, cv, sbias, sinkc, bblk, lamr, lami, cblk, dskip, wglu, bglu, ga, gs, wout, x0r, x0i,
      gffn, wrt, br, _tri(2 * DEC_BATCH), cnt_in, hbuf, hnbuf, idxbuf, gwbuf, rankbuf)


def _route(h, g_ref, wrt_ref, br_ref, tri_ref, cnt_ref, hn_ref, idx_ref, gw_ref, rank_ref):
    hn = _rms(h, g_ref[...])
    hn_ref[...] = _pack_pairs(hn)
    hn_hi = hn.astype(BF16)
    hn_lo = (hn - hn_hi.astype(F32)).astype(BF16)
    w = wrt_ref[...]
    w_hi = w.astype(BF16)
    w_lo = (w - w_hi.astype(F32)).astype(BF16)
    nt = (((1,), (1,)), ((), ()))
    logits = (lax.dot_general(w_hi, hn_hi, nt, preferred_element_type=F32)
              + lax.dot_general(w_lo, hn_hi, nt, preferred_element_type=F32)
              + lax.dot_general(w_hi, hn_lo, nt, preferred_element_type=F32)) + br_ref[...]
    eidx = lax.broadcasted_iota(jnp.int32, logits.shape, 0)
    vals, onehots = [], []
    l = logits
    for k in range(TOP_K):
        m = jnp.max(l, axis=0, keepdims=True)
        ik = jnp.min(jnp.where(l == m, eidx, N_EXPERTS), axis=0, keepdims=True)
        oh = eidx == ik
        idx_ref[k:k + 1, :] = ik
        vals.append(m)
        onehots.append(oh)
        l = jnp.where(oh, -jnp.inf, l)
    exps = [jnp.exp(v - vals[0]) for v in vals]
    den = exps[0] + exps[1] + exps[2] + exps[3]
    for k in range(TOP_K):
        gw_ref[k:k + 1, :] = exps[k] / den
    member = jnp.zeros(logits.shape, F32)
    for oh in onehots:
        member = member + jnp.where(oh, 1.0, 0.0)
    wblk = tri_ref.shape[0]
    base = cnt_ref[...].astype(F32)
    befores = []
    for cb in range(h.shape[0] // wblk):
        mblk = member[:, cb * wblk:(cb + 1) * wblk]
        befores.append(jnp.dot(mblk.astype(BF16), tri_ref[...], preferred_element_type=F32) + base)
        base = base + jnp.sum(mblk, axis=1, keepdims=True)
    before = jnp.concatenate(befores, axis=1)
    for k in range(TOP_K):
        rank_ref[k:k + 1, :] = jnp.sum(jnp.where(onehots[k], before, 0.0), axis=0, keepdims=True).astype(jnp.int32)
    cnt_ref[...] = base.astype(jnp.int32)


def _route_specs(rows, block):
    specs = [pl.BlockSpec((rows, D_PACK), lambda i: (block(i), 0)), pl.BlockSpec((TOP_K, rows), lambda i: (0, block(i))),
             pl.BlockSpec((TOP_K, rows), lambda i: (0, block(i))), pl.BlockSpec((TOP_K, rows), lambda i: (0, block(i))),
             _full((N_EXPERTS, 1))]
    shapes = [jax.ShapeDtypeStruct((T_PAD, D_PACK), jnp.int32), jax.ShapeDtypeStruct((TOP_K, T_PAD), jnp.int32),
              jax.ShapeDtypeStruct((TOP_K, T_PAD), F32), jax.ShapeDtypeStruct((TOP_K, T_PAD), jnp.int32),
              jax.ShapeDtypeStruct((N_EXPERTS, 1), jnp.int32)]
    return specs, shapes


def _tri(n):
    return jnp.asarray(np.triu(np.ones((n, n), np.float32), 1), BF16)


def _place_kernel(off_ref, idx_ref, rank_ref, pos_ref):
    idx = idx_ref[...]
    pos = rank_ref[...]
    for e in range(N_EXPERTS):
        pos = pos + jnp.where(idx == e, off_ref[e], 0)
    pos_ref[...] = pos


def _place(offsets, idx, rank):
    return pl.pallas_call(
        _place_kernel,
        in_specs=[pl.BlockSpec(memory_space=pltpu.SMEM), pl.BlockSpec(memory_space=pltpu.VMEM),
                  pl.BlockSpec(memory_space=pltpu.VMEM)],
        out_specs=pl.BlockSpec(memory_space=pltpu.VMEM),
        out_shape=jax.ShapeDtypeStruct((TOP_K, T_PAD), jnp.int32), name="place",
    )(offsets, idx, rank)


def _sc_mesh():
    return plsc.VectorSubcoreMesh(core_axis_name="core", subcore_axis_name="subcore")


def _sc_dispatch(rows, pos):
    n, d = rows.shape
    nblk = n // SC_ROWS
    pos_w = pos.reshape(TOP_K, nblk, SC_ROWS).transpose(1, 0, 2)

    @functools.partial(pl.kernel, out_type=jax.ShapeDtypeStruct((P_ROWS, d), rows.dtype), mesh=_sc_mesh(),
                       scratch_types=[], name="dispatch")
    def run(x_hbm, i_hbm, o_hbm):
        def body(x_vmem, i_vmem):
            for k in range(TOP_K):
                pltpu.sync_copy(x_vmem, o_hbm.at[i_vmem.at[0, k]])

        pltpu.emit_pipeline(
            body, grid=(nblk,),
            in_specs=[pl.BlockSpec((SC_ROWS, d), lambda i: (i, 0)),
                      pl.BlockSpec((1, TOP_K, SC_ROWS), lambda i: (i, 0, 0))],
            out_specs=[], core_axis_name=("core", "subcore"), dimension_semantics=(pltpu.PARALLEL,),
        )(x_hbm, i_hbm)

    return run(rows, pos_w)


def _sc_combine(rows, flat_pos):
    _, d = rows.shape
    n = flat_pos.shape[0]
    w = SC_ROWS_COMBINE
    nblk = n // w
    pos_w = flat_pos.reshape(nblk, 1, w)

    @functools.partial(pl.kernel, out_type=jax.ShapeDtypeStruct((n, d), rows.dtype), mesh=_sc_mesh(),
                       scratch_types=[], name="combine")
    def run(x_hbm, i_hbm, o_hbm):
        def body(i_vmem, o_vmem):
            pltpu.sync_copy(x_hbm.at[i_vmem.at[0, 0]], o_vmem)

        pltpu.emit_pipeline(
            body, grid=(nblk,),
            in_specs=[pl.BlockSpec((1, 1, w), lambda i: (i, 0, 0))],
            out_specs=[pl.BlockSpec((w, d), lambda i: (i, 0))],
            core_axis_name=("core", "subcore"), dimension_semantics=(pltpu.PARALLEL,),
        )(i_hbm, o_hbm)

    return run(rows, pos_w)


def _experts_kernel(ts_ref, trow_ref, tsz_ref, xs_hbm, w1_hbm, b1_ref, w2_hbm, b2_ref, ys_hbm,
                    w1_st, w2_st, w1_s, w2_s, xbuf, ybuf, wsem, xsem, ysem):
    e = pl.program_id(0)
    n_valid = ts_ref[N_EXPERTS]

    def w_copies(ex, slot):
        return (pltpu.make_async_copy(w1_hbm.at[ex], w1_st.at[slot], wsem.at[0, slot]),
                pltpu.make_async_copy(w2_hbm.at[ex], w2_st.at[slot], wsem.at[1, slot]))

    def x_copy(t, slot, rows):
        src = xs_hbm.at[pl.ds(pl.multiple_of(trow_ref[t], TM_UNIT), rows)]
        return pltpu.make_async_copy(src, xbuf.at[slot, pl.ds(0, rows)], xsem.at[slot])

    def y_copy(t, slot, rows):
        dst = ys_hbm.at[pl.ds(pl.multiple_of(trow_ref[t], TM_UNIT), rows)]
        return pltpu.make_async_copy(ybuf.at[slot, pl.ds(0, rows)], dst, ysem.at[slot])

    def by_size(t, fn):
        for rows in TILE_SIZES:
            @pl.when(tsz_ref[t] == rows)
            def _():
                fn(rows)

    @pl.when(e == 0)
    def _():
        for c in w_copies(0, 0):
            c.start()
        by_size(0, lambda rows: x_copy(0, 0, rows).start())

    @pl.when(e + 1 < N_EXPERTS)
    def _():
        for c in w_copies(e + 1, (e + 1) % 2):
            c.start()

    wslot = e % 2
    for c in w_copies(e, wslot):
        c.wait()
    for r in range(4):
        rs = slice(256 * r, 256 * (r + 1))
        w1_s[rs, :] = w1_st[wslot, rs, :].astype(BF16)
        w2_s[rs, :] = w2_st[wslot, rs, :].astype(BF16)

    def tile(t, carry):
        slot = t % 2

        @pl.when(t + 1 < n_valid)
        def _():
            by_size(t + 1, lambda rows: x_copy(t + 1, 1 - slot, rows).start())

        by_size(t, lambda rows: x_copy(t, slot, rows).wait())

        @pl.when(t >= 2)
        def _():
            by_size(t - 2, lambda rows: y_copy(t - 2, slot, rows).wait())

        def compute(rows):
            hdn = _bdot(_unpack_pairs(xbuf[slot, 0:rows]), w1_s[...]) + b1_ref[0]
            gt = jnp.minimum(hdn[:, :D_FF], SWIGLU_LIMIT)
            up = jnp.clip(hdn[:, D_FF:], -SWIGLU_LIMIT, SWIGLU_LIMIT)
            act = (up + 1.0) * gt * _sigmoid(SWIGLU_ALPHA * gt)
            ybuf[slot, 0:rows] = _pack_pairs(_bdot(act, w2_s[...]) + b2_ref[0])
            y_copy(t, slot, rows).start()

        by_size(t, compute)
        return carry

    lax.fori_loop(ts_ref[e], ts_ref[e + 1], tile, 0)

    @pl.when(e == N_EXPERTS - 1)
    def _():
        @pl.when(n_valid >= 2)
        def _():
            by_size(n_valid - 2, lambda rows: y_copy(n_valid - 2, n_valid % 2, rows).wait())

        by_size(n_valid - 1, lambda rows: y_copy(n_valid - 1, (n_valid - 1) % 2, rows).wait())


def _tile_plan(counts):
    units = (counts + (TM_UNIT - 1)) // TM_UNIT
    unit_end = jnp.cumsum(units)
    offsets = ((unit_end - units) * TM_UNIT).astype(jnp.int32)
    per_full = TM // TM_UNIT
    n_full = units // per_full
    rest = units % per_full
    has_mid = (rest >= 2).astype(jnp.int32)
    tiles_per = n_full + has_mid + rest % 2
    tile_end = jnp.cumsum(tiles_per)
    tile_start = jnp.concatenate([jnp.zeros((1,), jnp.int32), tile_end.astype(jnp.int32)])
    t = jnp.arange(MAX_TILES, dtype=jnp.int32)[:, None]
    done = tile_end[None, :] <= t
    mine = jnp.sum(done.astype(jnp.int32), axis=1, keepdims=True) == jnp.arange(N_EXPERTS, dtype=jnp.int32)[None, :]
    pick = lambda v: jnp.sum(jnp.where(mine, v[None, :], 0), axis=1)
    j = t[:, 0] - jnp.max(jnp.where(done, tile_end[None, :], 0), axis=1)
    nf, mid = pick(n_full), pick(has_mid)
    rows = jnp.where(j < nf, TM, jnp.where((j == nf) & (mid == 1), TILE_SIZES[1], TILE_SIZES[2]))
    row0 = pick(offsets) + jnp.minimum(j, nf) * TM + jnp.where(j > nf, TILE_SIZES[1], 0)
    valid = t[:, 0] < tile_end[-1]
    tile_row = jnp.where(valid, row0, 0).astype(jnp.int32)
    tile_rows = jnp.where(valid, rows, 0).astype(jnp.int32)
    return offsets, tile_start, tile_row, tile_rows


def _experts(tile_start, tile_row, tile_rows, xs, w_up, b_up, w_down, b_down):
    wsel = lambda e, *_: (e, 0, 0)
    hbm = pl.BlockSpec(memory_space=pl.ANY)
    grid_spec = pltpu.PrefetchScalarGridSpec(
        num_scalar_prefetch=3, grid=(N_EXPERTS,),
        in_specs=[hbm, hbm, pl.BlockSpec((1, 1, 2 * D_FF), wsel), hbm, pl.BlockSpec((1, 1, D_MODEL), wsel)],
        out_specs=hbm,
        scratch_shapes=[pltpu.VMEM((2, D_MODEL, 2 * D_FF), F32), pltpu.VMEM((2, D_FF, D_MODEL), F32),
                        pltpu.VMEM((D_MODEL, 2 * D_FF), BF16), pltpu.VMEM((D_FF, D_MODEL), BF16),
                        pltpu.VMEM((2, TM, D_PACK), jnp.int32), pltpu.VMEM((2, TM, D_PACK), jnp.int32),
                        pltpu.SemaphoreType.DMA((2, 2)), pltpu.SemaphoreType.DMA((2,)),
                        pltpu.SemaphoreType.DMA((2,))],
    )
    return pl.pallas_call(
        _experts_kernel, grid_spec=grid_spec, out_shape=jax.ShapeDtypeStruct((P_ROWS, D_PACK), jnp.int32),
        name="experts",
        compiler_params=pltpu.CompilerParams(dimension_semantics=("arbitrary",), vmem_limit_bytes=VMEM_LIMIT),
    )(tile_start, tile_row, tile_rows, xs, w_up, b_up.reshape(N_EXPERTS, 1, 2 * D_FF), w_down,
      b_down.reshape(N_EXPERTS, 1, D_MODEL))


def _final_kernel(h_ref, yg_ref, gw_ref, p_ref, gple_ref, wg_ref, wp_ref, gfin_ref, *rest):
    out_ref = rest[-1]
    rows = h_ref.shape[0]
    gw = gw_ref[...]
    h = h_ref[...]
    for k in range(TOP_K):
        h = h + gw[:, k:k + 1] * _unpack_pairs(yg_ref[k])
    gate = _sigmoid(_bdot(_rms(h, gple_ref[...]), wg_ref[...]))
    h = h + gate * _bdot(p_ref[...].reshape(rows, PLE_DIM), wp_ref[...])
    out_ref[...] = _rms(h, gfin_ref[...]).reshape(out_ref.shape)


def _final_prompt(part, hmid, yg, gwt, p_prompt, gple, wg, wp, gfin, y_prev):
    nb = 4
    rows = nb * CHUNK
    nbh = BATCH // nb
    c0 = part * PART_CHUNKS
    rb = lambda c, b: ((c0 + c) * nbh + b, 0)
    in_specs = [pl.BlockSpec((rows, D_MODEL), rb),
                pl.BlockSpec((TOP_K, rows, D_PACK), lambda c, b: (0, c * nbh + b, 0)),
                pl.BlockSpec((rows, TOP_K), rb),
                pl.BlockSpec((nb, CHUNK, PLE_DIM), lambda c, b: (b, c0 + c, 0)),
                _full((1, D_MODEL)), _full((D_MODEL, D_MODEL)), _full((PLE_DIM, D_MODEL)), _full((1, D_MODEL))]
    args = [hmid, yg, gwt, p_prompt, gple, wg, wp, gfin]
    aliases = {}
    if y_prev is not None:
        in_specs.append(pl.BlockSpec(memory_space=pl.ANY))
        args.append(y_prev)
        aliases = {len(args) - 1: 0}
    return pl.pallas_call(
        _final_kernel, grid=(PART_CHUNKS, nbh), in_specs=in_specs,
        out_specs=pl.BlockSpec((nb, CHUNK, D_MODEL), lambda c, b: (b, c0 + c, 0)),
        out_shape=jax.ShapeDtypeStruct((BATCH, SEQ, D_MODEL), F32), name="final_prompt",
        input_output_aliases=aliases,
        compiler_params=pltpu.CompilerParams(dimension_semantics=("arbitrary", "arbitrary"),
                                             vmem_limit_bytes=VMEM_LIMIT),
    )(*args)


def _final_sample(hmid, yg, gwt, p_sample, gple, wg, wp, gfin):
    blk = SAMPLE_ROW0 // DEC_BATCH
    return pl.pallas_call(
        _final_kernel, grid=(1,),
        in_specs=[pl.BlockSpec((DEC_BATCH, D_MODEL), lambda i: (blk, 0)),
                  pl.BlockSpec((TOP_K, DEC_BATCH, D_PACK), lambda i: (0, PART_CHUNKS * ROWS // DEC_BATCH, 0)),
                  pl.BlockSpec((DEC_BATCH, TOP_K), lambda i: (blk, 0)),
                  _full((DEC_BATCH, PLE_DIM)),
                  _full((1, D_MODEL)), _full((D_MODEL, D_MODEL)), _full((PLE_DIM, D_MODEL)), _full((1, D_MODEL))],
        out_specs=_full((DEC_BATCH, D_MODEL)),
        out_shape=jax.ShapeDtypeStruct((DEC_BATCH, D_MODEL), F32), name="final_sample",
        compiler_params=pltpu.CompilerParams(dimension_semantics=("arbitrary",), vmem_limit_bytes=VMEM_LIMIT),
    )(hmid, yg, gwt, p_sample, gple, wg, wp, gfin)


def _alibi_tables():
    slopes = 2.0 ** (-8.0 * (np.arange(N_HEADS, dtype=np.float64) + 1.0) / N_HEADS)
    i = np.arange(CHUNK)[:, None]
    j = np.arange(2 * CHUNK)[None, :]
    dist = i + CHUNK - j
    valid = (dist >= 0) & (dist <= WINDOW)
    tabs = []
    for has_prev in (False, True):
        ok = valid & ((j >= CHUNK) | has_prev)
        tabs.append(np.where(ok[None], -slopes[:, None, None] * dist[None], NEG))
    prompt = np.stack([np.stack([np.concatenate([t[h] for h in grp], axis=0) for grp in (NAT_HEADS, ROT_HEADS)])
                       for t in tabs]).astype(np.float32)
    wb = min(WINDOW, PAST_LEN)
    sample = (-slopes[:, None] * (wb - np.arange(wb))[None, :]).astype(np.float32)
    return prompt, sample


def kernel(x_prompt, x_sample, cache_k_win, cache_v_win, state_ssm_re, state_ssm_im, p_prompt, p_sample, norm_mix, w_in, sinks, ssm_lam_re, ssm_lam_im, ssm_log_step, ssm_b_re, ssm_b_im, ssm_c_re, ssm_c_im, ssm_d, w_glu, b_glu, norm_attn_out, norm_ssm_out, w_out, norm_ffn, w_router, b_router, w_up, b_up, w_down, b_down, norm_ple, w_ple_gate, w_ple_proj, norm_final):
    nst = SSM_G * SSM_P
    bias_np, sbias_np = _alibi_tables()
    bias = jnp.asarray(bias_np)
    sbias = jnp.asarray(sbias_np)

    lbr, lbi, bbr, bbi = _prep(ssm_lam_re[0], ssm_lam_im[0], ssm_log_step[0], ssm_b_re[0], ssm_b_im[0])
    bblk, lamr, lami, cblk = _s5_blocks(lbr, lbi, bbr, bbi, ssm_c_re[0], ssm_c_im[0])

    gmix = norm_mix[0].reshape(1, D_MODEL)
    win = w_in[0].astype(BF16)
    dskip = ssm_d[0].reshape(1, D_SSM)
    wglu = w_glu[0].astype(BF16)
    bglu = b_glu[0].reshape(1, D_SSM)
    ga = norm_attn_out[0].reshape(1, D_ATTN)
    gs = norm_ssm_out[0].reshape(1, D_SSM)
    wout = w_out[0].astype(BF16)
    sink = sinks[0]

    gffn = norm_ffn[0].reshape(1, D_MODEL)
    wrt = w_router[0].T
    br = b_router[0].reshape(N_EXPERTS, 1)
    hbuf, k_last, v_last, re_p, im_p, hnbuf, idxbuf, gwbuf, rankbuf, cnt_p = _mixer_prompt(
        x_prompt, sink, gmix, win, bias, bblk, lamr, lami, cblk, dskip, wglu, bglu, ga, gs, wout, gffn, wrt, br)

    ck = cache_k_win[0].reshape(DEC_BATCH, WINDOW, D_KV)
    cv = cache_v_win[0].reshape(DEC_BATCH, WINDOW, D_KV)
    hmid, k_new, v_new, re_s, im_s, hn, idx, gw, rank, counts = _mixer_sample(
        x_sample.reshape(DEC_BATCH, D_MODEL), gmix, win, ck, cv, sbias, sink.reshape(N_HEADS, 1), bblk, lamr, lami,
        cblk, dskip, wglu, bglu, ga, gs, wout, state_ssm_re[0].reshape(DEC_BATCH, nst),
        state_ssm_im[0].reshape(DEC_BATCH, nst), gffn, wrt, br, cnt_p, hbuf, hnbuf, idxbuf, gwbuf, rankbuf)

    offsets, tile_start, tile_row, tile_rows = _tile_plan(counts[:, 0])

    pos = _place(offsets, idx, rank)
    xs = _sc_dispatch(hn, pos)
    ys = _experts(tile_start, tile_row, tile_rows, xs, w_up[0], b_up[0], w_down[0], b_down[0])
    ygs = []
    for part in range(N_PARTS):
        r0 = part * PART_CHUNKS * ROWS
        r1 = T_PAD if part == N_PARTS - 1 else r0 + PART_CHUNKS * ROWS
        ygs.append(_sc_combine(ys, pos[:, r0:r1].reshape(TOP_K * (r1 - r0))).reshape(TOP_K, r1 - r0, D_PACK))

    gwt = gw.T
    gple = norm_ple[0].reshape(1, D_MODEL)
    wg = w_ple_gate[0].astype(BF16)
    wp = w_ple_proj[0].astype(BF16)
    gfin = norm_final.reshape(1, D_MODEL)
    y_prompt = None
    for part in range(N_PARTS):
        y_prompt = _final_prompt(part, hmid, ygs[part], gwt, p_prompt[0], gple, wg, wp, gfin, y_prompt)
    y_sample = _final_sample(hmid, ygs[-1], gwt, p_sample[0].reshape(DEC_BATCH, PLE_DIM), gple, wg, wp, gfin)

    k_win_s = jnp.concatenate([ck[:, 1:], k_new[:, None, :]], axis=1)
    v_win_s = jnp.concatenate([cv[:, 1:], v_new[:, None, :]], axis=1)
    kv5 = (1, BATCH, CHUNK, N_KV, HEAD_DIM)
    skv5 = (1, DEC_BATCH, WINDOW, N_KV, HEAD_DIM)
    return (y_prompt, y_sample.reshape(DEC_BATCH, 1, D_MODEL),
            k_last.reshape(kv5), v_last.reshape(kv5),
            re_p.reshape(1, BATCH, SSM_G, SSM_P), im_p.reshape(1, BATCH, SSM_G, SSM_P),
            k_win_s.reshape(skv5), v_win_s.reshape(skv5),
            re_s.reshape(1, DEC_BATCH, SSM_G, SSM_P), im_s.reshape(1, DEC_BATCH, SSM_G, SSM_P))
```

```python
import functools

import numpy as np
import jax
import jax.numpy as jnp
from jax import lax
from jax.experimental import pallas as pl
from jax.experimental.pallas import tpu as pltpu
from jax.experimental.pallas import tpu_sc as plsc

F32 = jnp.float32
BF16 = jnp.bfloat16

D_MODEL = 1024
BATCH = 8
SEQ = 2048
DEC_BATCH = 128
PAST_LEN = 16384
HEAD_DIM = 64
D_ATTN = 512
N_HEADS = 8
N_KV = 2
D_KV = N_KV * HEAD_DIM
WINDOW = 128
D_SSM = 512
SSM_H = 16
SSM_G = 32
SSM_P = 64
D_IN = D_ATTN + 2 * D_KV + D_SSM
N_EXPERTS = 32
TOP_K = 4
D_FF = 1024
SWIGLU_LIMIT = 7.0
SWIGLU_ALPHA = 1.702
PLE_DIM = 256
EPS = 1e-5
NEG = -1e30

CHUNK = 128
N_CHUNKS = SEQ // CHUNK
ROWS = BATCH * CHUNK
PITCH = CHUNK + 8
NAT_HEADS = (0, 2, 5, 7)
ROT_HEADS = (1, 3, 4, 6)
ATTN_UNROLL = 2
SUB_T = 64
SUB_ROWS = SUB_T * BATCH
N_SBLK = 4
SBLK = 512
T_REAL = BATCH * SEQ + DEC_BATCH
T_PAD = T_REAL + 128
SAMPLE_ROW0 = BATCH * SEQ
ROUTE_BLOCK = 512
TILE_SIZES = (1024, 512, 256)
TM = TILE_SIZES[0]
TM_UNIT = TILE_SIZES[-1]
P_ROWS = (T_PAD * TOP_K + N_EXPERTS * (TM_UNIT - 1)) // TM_UNIT * TM_UNIT
MAX_TILES = P_ROWS // TM + 2 * N_EXPERTS + 1
SC_ROWS = 40
SC_ROWS_COMBINE = 32
N_PARTS = 8
PART_CHUNKS = N_CHUNKS // N_PARTS
D_PACK = D_MODEL // 2
VMEM_LIMIT = 56 * 1024 * 1024


def _rms(x, g):
    return x * lax.rsqrt(jnp.mean(x * x, axis=-1, keepdims=True) + EPS) * g


def _sigmoid(x):
    return 1.0 / (1.0 + jnp.exp(-x))


def _gelu_tanh(x):
    c = np.float32(np.sqrt(2.0 / np.pi))
    return 0.5 * x * (1.0 + jnp.tanh(c * (x + 0.044715 * (x * x * x))))


def _bdot(a, b):
    return jnp.dot(a.astype(BF16), b, preferred_element_type=F32)


def _pack_pairs(x):
    n = x.shape[1] // 2
    lo = lax.bitcast_convert_type(x[:, :n].astype(BF16).astype(F32), jnp.int32)
    hi = lax.bitcast_convert_type(x[:, n:].astype(BF16).astype(F32), jnp.int32)
    return lax.shift_right_logical(lo, 16) | hi


def _unpack_pairs(w):
    lo = lax.bitcast_convert_type(lax.shift_left(w, 16), F32)
    hi = lax.bitcast_convert_type(w & jnp.int32(-65536), F32)
    return jnp.concatenate([lo, hi], axis=1)


def _full(shape):
    n = len(shape)
    return pl.BlockSpec(shape, lambda *_: (0,) * n)


def _prep_kernel(lr_ref, li_ref, ls_ref, br_ref, bi_ref, lbr_ref, lbi_ref, bbr_ref, bbi_ref):
    lr = lr_ref[...]
    li = li_ref[...]
    step = jnp.exp(ls_ref[...])
    zr = lr * step
    zi = li * step
    mag = jnp.exp(zr)
    lbr = mag * jnp.cos(zi)
    lbi = mag * jnp.sin(zi)
    lbr_ref[...] = lbr
    lbi_ref[...] = lbi
    nr = lbr - 1.0
    den = lr * lr + li * li
    cr = (nr * lr + lbi * li) / den
    ci = (lbi * lr - nr * li) / den
    br = br_ref[...]
    bi = bi_ref[...]
    bbr_ref[...] = cr * br - ci * bi
    bbi_ref[...] = cr * bi + ci * br


def _prep(lam_re, lam_im, log_step, b_re, b_im):
    g, p, h = SSM_G, SSM_P, SSM_H
    out = pl.pallas_call(
        _prep_kernel,
        out_shape=[jax.ShapeDtypeStruct((g, 1, p), F32), jax.ShapeDtypeStruct((g, 1, p), F32),
                   jax.ShapeDtypeStruct((g, h, p), F32), jax.ShapeDtypeStruct((g, h, p), F32)],
        name="s5_prep",
    )(lam_re.reshape(g, 1, p), lam_im.reshape(g, 1, p), log_step.reshape(g, 1, 1),
      jnp.transpose(b_re, (0, 2, 1)), jnp.transpose(b_im, (0, 2, 1)))
    return out


def _s5_blocks(lbr, lbi, bbr, bbi, c_re, c_im):
    eye = jnp.eye(8, dtype=F32)
    shp = (N_SBLK, 8, SSM_H, SSM_P)

    def in_map(b):
        return (b.reshape(shp)[:, :, :, None, :] * eye[None, :, None, :, None]).reshape(N_SBLK, 128, SBLK)

    def out_map(c):
        ct = jnp.transpose(c.reshape(shp), (0, 1, 3, 2))
        return (ct[:, :, :, None, :] * eye[None, :, None, :, None]).reshape(N_SBLK, SBLK, 128)

    bblk = jnp.concatenate([in_map(bbr), in_map(bbi)], axis=-1).astype(BF16)
    cblk = jnp.concatenate([out_map(c_re), -out_map(c_im)], axis=1).astype(BF16)
    return bblk, lbr.reshape(N_SBLK, 1, SBLK), lbi.reshape(N_SBLK, 1, SBLK), cblk


def _ssm_post(y_lin, u, dskip, wglu, bglu, gs):
    y = _gelu_tanh(y_lin + dskip * u)
    y = y * _sigmoid(_bdot(y, wglu) + bglu)
    return _rms(y, gs)


def _mixer_prompt_kernel(sinks_ref, x_ref, gmix_ref, win_ref, bias_ref, bblk_ref, lamr_ref, lami_ref,
                         cblk_ref, dskip_ref, wglu_ref, bglu_ref, ga_ref, gs_ref, wout_ref,
                         gffn_ref, wrt_ref, br_ref, tri_ref,
                         hmid_ref, klast_ref, vlast_ref, sre_ref, sim_ref, hn_ref, idx_ref, gw_ref, rank_ref, cnt_ref,
                         proj_s, u_s, kprev_s, kprevr_s, vprev_s, vprevr_s, attn_s, ssm_s, utb_s, bu_s, xs_s, ytb_s):
    c = pl.program_id(0)

    @pl.when(c == 0)
    def _():
        zkv = jnp.zeros(kprev_s.shape, BF16)
        kprev_s[...] = zkv
        kprevr_s[...] = zkv
        vprev_s[...] = zkv
        vprevr_s[...] = zkv
        sre_ref[...] = jnp.zeros(sre_ref.shape, F32)
        sim_ref[...] = jnp.zeros(sim_ref.shape, F32)
        cnt_ref[...] = jnp.zeros(cnt_ref.shape, jnp.int32)

    x = x_ref[...].reshape(ROWS, D_MODEL)
    proj = _bdot(_rms(x, gmix_ref[...]), win_ref[...])
    u0 = D_ATTN + 2 * D_KV
    proj_s[...] = proj[:, 0:u0]
    for jb in range(D_SSM // 128):
        for b in range(BATCH):
            u_s[jb, b * PITCH:b * PITCH + CHUNK, :] = proj[b * CHUNK:(b + 1) * CHUNK, u0 + 128 * jb:u0 + 128 * (jb + 1)]

    lo = lax.broadcasted_iota(jnp.int32, (CHUNK, 128), 1) < HEAD_DIM
    hi = jnp.logical_not(lo)
    table = jnp.minimum(c, 1)
    hrow = lax.broadcasted_iota(jnp.int32, (4 * CHUNK, 1), 0) // CHUNK

    def sink_col(heads):
        col = jnp.full((4 * CHUNK, 1), sinks_ref[heads[3]], F32)
        for n in (2, 1, 0):
            col = jnp.where(hrow == n, sinks_ref[heads[n]], col)
        return col

    sink_nat = sink_col(NAT_HEADS)
    sink_rot = sink_col(ROT_HEADS)

    def attend(q, k, v, bias, sink):
        s = lax.dot_general(q, k, (((1,), (1,)), ((), ())), preferred_element_type=F32) + bias
        m = jnp.maximum(jnp.max(s, axis=-1, keepdims=True), sink)
        p = jnp.exp(s - m)
        den = jnp.sum(p, axis=-1, keepdims=True) + jnp.exp(sink - m)
        return jnp.dot(p.astype(BF16), v, preferred_element_type=F32) / den

    def attn_body(b, carry):
        r0 = pl.multiple_of(b * CHUNK, CHUNK)
        rows = pl.ds(r0, CHUNK)
        kb = proj_s[rows, D_ATTN:D_ATTN + D_KV]
        vb = proj_s[rows, D_ATTN + D_KV:D_ATTN + 2 * D_KV]
        kb16 = kb.astype(BF16)
        vb16 = vb.astype(BF16)
        kbr16 = pltpu.roll(kb, HEAD_DIM, 1).astype(BF16)
        vbr16 = pltpu.roll(vb, HEAD_DIM, 1).astype(BF16)
        k_nat = jnp.concatenate([kprev_s[b], kb16], axis=0)
        k_rot = jnp.concatenate([kprevr_s[b], kbr16], axis=0)
        v_nat = jnp.concatenate([vprev_s[b], vb16], axis=0)
        v_rot = jnp.concatenate([vprevr_s[b], vbr16], axis=0)
        q2 = [proj_s[rows, 128 * jq:128 * (jq + 1)] * (HEAD_DIM ** -0.5) for jq in range(N_HEADS // 2)]
        q_nat = jnp.concatenate([jnp.where(lo if h % 2 == 0 else hi, q2[h // 2], 0.0) for h in NAT_HEADS],
                                axis=0).astype(BF16)
        q_rot = jnp.concatenate([jnp.where(lo if h % 2 == 0 else hi, q2[h // 2], 0.0) for h in ROT_HEADS],
                                axis=0).astype(BF16)
        o_nat = attend(q_nat, k_nat, v_nat, bias_ref[table, 0], sink_nat)
        o_rot = attend(q_rot, k_rot, v_rot, bias_ref[table, 1], sink_rot)
        for jq in range(N_HEADS // 2):
            blk = slice(CHUNK * jq, CHUNK * (jq + 1))
            even, odd = (o_nat, o_rot) if jq < 2 else (o_rot, o_nat)
            attn_s[rows, 128 * jq:128 * (jq + 1)] = jnp.where(lo, even[blk], odd[blk])
        kprev_s[b] = kb16
        kprevr_s[b] = kbr16
        vprev_s[b] = vb16
        vprevr_s[b] = vbr16
        return carry

    def attn_group(i, carry):
        for n in range(ATTN_UNROLL):
            attn_body(ATTN_UNROLL * i + n, carry)
        return carry

    lax.fori_loop(0, BATCH // ATTN_UNROLL, attn_group, 0)

    for sc in range(CHUNK // SUB_T):
        t0 = sc * SUB_T
        for i in range(SUB_T):
            for jb in range(D_SSM // 128):
                utb_s[i * BATCH:(i + 1) * BATCH, 128 * jb:128 * (jb + 1)] = (
                    u_s[jb, pl.ds(t0 + i, BATCH, stride=PITCH), :])
        u_tb = utb_s[...]
        for j in range(N_SBLK):
            bu_s[...] = _bdot(u_tb[:, 128 * j:128 * (j + 1)], bblk_ref[j])
            lr = jnp.broadcast_to(lamr_ref[j], (BATCH, SBLK))
            li = jnp.broadcast_to(lami_ref[j], (BATCH, SBLK))
            cols = slice(SBLK * j, SBLK * (j + 1))

            def step(i, carry):
                sr, si = carry
                r = pl.ds(pl.multiple_of(i * BATCH, BATCH), BATCH)
                nr = lr * sr - li * si + bu_s[r, 0:SBLK]
                ni = lr * si + li * sr + bu_s[r, SBLK:2 * SBLK]
                xs_s[r, 0:SBLK] = nr
                xs_s[r, SBLK:2 * SBLK] = ni
                return nr, ni

            sr, si = lax.fori_loop(0, SUB_T, step, (sre_ref[:, cols], sim_ref[:, cols]), unroll=True)
            sre_ref[:, cols] = sr
            sim_ref[:, cols] = si
            ytb_s[:, 128 * j:128 * (j + 1)] = _bdot(xs_s[...], cblk_ref[j])
        yn = _ssm_post(ytb_s[...], u_tb, dskip_ref[...], wglu_ref[...], bglu_ref[...], gs_ref[...])
        for i in range(SUB_T):
            for jb in range(D_SSM // 128):
                ssm_s[jb, pl.ds(t0 + i, BATCH, stride=PITCH), :] = (
                    yn[i * BATCH:(i + 1) * BATCH, 128 * jb:128 * (jb + 1)])

    an = _rms(attn_s[...], ga_ref[...])
    sn = jnp.concatenate(
        [jnp.concatenate([ssm_s[jb, b * PITCH:b * PITCH + CHUNK, :] for b in range(BATCH)], axis=0)
         for jb in range(D_SSM // 128)], axis=1)
    h = x + _bdot(an, wout_ref[0:D_ATTN, :]) + _bdot(sn, wout_ref[D_ATTN:2 * D_ATTN, :])
    hmid_ref[...] = h
    _route(h, gffn_ref, wrt_ref, br_ref, tri_ref, cnt_ref, hn_ref, idx_ref, gw_ref, rank_ref)

    @pl.when(c == N_CHUNKS - 1)
    def _():
        klast_ref[...] = proj_s[:, D_ATTN:D_ATTN + D_KV].reshape(BATCH, CHUNK, D_KV)
        vlast_ref[...] = proj_s[:, D_ATTN + D_KV:D_ATTN + 2 * D_KV].reshape(BATCH, CHUNK, D_KV)


def _mixer_prompt(x_prompt, sinks, gmix, win, bias, bblk, lamr, lami, cblk, dskip, wglu, bglu, ga, gs, wout,
                  gffn, wrt, br):
    smem = pl.BlockSpec(memory_space=pltpu.SMEM)
    in_specs = [
        smem,
        pl.BlockSpec((BATCH, CHUNK, D_MODEL), lambda c: (0, c, 0)),
        _full((1, D_MODEL)), _full((D_MODEL, D_IN)), _full((2, 2, 4 * CHUNK, 2 * CHUNK)),
        _full((N_SBLK, 128, 2 * SBLK)), _full((N_SBLK, 1, SBLK)), _full((N_SBLK, 1, SBLK)),
        _full((N_SBLK, 2 * SBLK, 128)), _full((1, D_SSM)), _full((D_SSM, D_SSM)), _full((1, D_SSM)),
        _full((1, D_ATTN)), _full((1, D_SSM)), _full((D_MODEL, D_MODEL)),
        _full((1, D_MODEL)), _full((N_EXPERTS, D_MODEL)), _full((N_EXPERTS, 1)), _full((ROUTE_BLOCK, ROUTE_BLOCK)),
    ]
    route_specs, route_shapes = _route_specs(ROWS, lambda c: c)
    out_shape = [
        jax.ShapeDtypeStruct((T_PAD, D_MODEL), F32),
        jax.ShapeDtypeStruct((BATCH, CHUNK, D_KV), F32),
        jax.ShapeDtypeStruct((BATCH, CHUNK, D_KV), F32),
        jax.ShapeDtypeStruct((BATCH, SSM_G * SSM_P), F32),
        jax.ShapeDtypeStruct((BATCH, SSM_G * SSM_P), F32),
    ] + route_shapes
    out_specs = [
        pl.BlockSpec((ROWS, D_MODEL), lambda c: (c, 0)),
        _full((BATCH, CHUNK, D_KV)), _full((BATCH, CHUNK, D_KV)),
        _full((BATCH, SSM_G * SSM_P)), _full((BATCH, SSM_G * SSM_P)),
    ] + route_specs
    kv_scr = pltpu.VMEM((BATCH, CHUNK, D_KV), BF16)
    scratch = [
        pltpu.VMEM((ROWS, D_ATTN + 2 * D_KV), F32), pltpu.VMEM((D_SSM // 128, BATCH * PITCH, 128), F32),
        kv_scr, kv_scr, kv_scr, kv_scr,
        pltpu.VMEM((ROWS, D_ATTN), F32), pltpu.VMEM((D_SSM // 128, BATCH * PITCH, 128), F32),
        pltpu.VMEM((SUB_ROWS, D_SSM), F32), pltpu.VMEM((SUB_ROWS, 2 * SBLK), F32),
        pltpu.VMEM((SUB_ROWS, 2 * SBLK), F32), pltpu.VMEM((SUB_ROWS, D_SSM), F32),
    ]
    return pl.pallas_call(
        _mixer_prompt_kernel, grid=(N_CHUNKS,), in_specs=in_specs, out_specs=out_specs, out_shape=out_shape,
        scratch_shapes=scratch, name="mixer_prompt",
        compiler_params=pltpu.CompilerParams(dimension_semantics=("arbitrary",), vmem_limit_bytes=VMEM_LIMIT),
    )(sinks, x_prompt, gmix, win, bias, bblk, lamr, lami, cblk, dskip, wglu, bglu, ga, gs, wout,
      gffn, wrt, br, _tri(ROUTE_BLOCK))


SGRP = 16
SPITCH = DEC_BATCH + 8
N_SGRP = DEC_BATCH // SGRP


def _mixer_sample_kernel(x_ref, gmix_ref, win_ref, ck_ref, cv_ref, sbias_ref, sinkc_ref, bblk_ref, lamr_ref,
                         lami_ref, cblk_ref, dskip_ref, wglu_ref, bglu_ref, ga_ref, gs_ref, wout_ref,
                         x0r_ref, x0i_ref, gffn_ref, wrt_ref, br_ref, tri_ref, cnt_in_ref,
                         hbuf_ref, hnbuf_ref, idxbuf_ref, gwbuf_ref, rankbuf_ref,
                         hmid_ref, knew_ref, vnew_ref, sre_ref, sim_ref, hn_ref, idx_ref, gw_ref, rank_ref, cnt_ref,
                         proj_s, qall_s, oall_s, kn8_s, vn8_s):
    del hbuf_ref, hnbuf_ref, idxbuf_ref, gwbuf_ref, rankbuf_ref
    g = pl.program_id(0)
    lo = lax.broadcasted_iota(jnp.int32, (DEC_BATCH, 128), 1) < HEAD_DIM

    @pl.when(g == 0)
    def _():
        proj = _bdot(_rms(x_ref[...], gmix_ref[...]), win_ref[...])
        proj_s[...] = proj
        for h in range(N_HEADS):
            jq, half, kv = h // 2, h % 2, h // 4
            q2 = proj[:, 128 * jq:128 * (jq + 1)] * (HEAD_DIM ** -0.5)
            if half != kv:
                q2 = pltpu.roll(q2, HEAD_DIM, 1)
            hr = slice(h * SPITCH, h * SPITCH + DEC_BATCH)
            qall_s[hr, :] = jnp.where(lo if kv == 0 else jnp.logical_not(lo), q2, 0.0)
            kn8_s[hr, :] = proj[:, D_ATTN:D_ATTN + D_KV]
            vn8_s[hr, :] = proj[:, D_ATTN + D_KV:D_ATTN + 2 * D_KV]

    def head_rows(ref):
        return jnp.stack([ref[pl.ds(g * SGRP + ii, N_HEADS, stride=SPITCH), :] for ii in range(SGRP)], axis=0)

    qe = head_rows(qall_s)
    knew = head_rows(kn8_s)
    vnew = head_rows(vn8_s)
    s = jnp.einsum('bhd,bkd->bhk', qe.astype(BF16), ck_ref[...].astype(BF16),
                   preferred_element_type=F32) + sbias_ref[...][None]
    s_new = jnp.sum(qe * knew, axis=-1, keepdims=True)
    sink = sinkc_ref[...][None]
    m = jnp.maximum(jnp.maximum(jnp.max(s, axis=-1, keepdims=True), s_new), sink)
    p = jnp.exp(s - m)
    p_new = jnp.exp(s_new - m)
    den = jnp.sum(p, axis=-1, keepdims=True) + p_new + jnp.exp(sink - m)
    o = (jnp.einsum('bhk,bkd->bhd', p.astype(BF16), cv_ref[...].astype(BF16),
                    preferred_element_type=F32) + p_new * vnew) / den
    for ii in range(SGRP):
        oall_s[pl.ds(g * SGRP + ii, N_HEADS, stride=SPITCH), :] = o[ii]

    @pl.when(g == N_SGRP - 1)
    def _():
        proj = proj_s[...]
        blocks = []
        for jq in range(N_HEADS // 2):
            oa = oall_s[(2 * jq) * SPITCH:(2 * jq) * SPITCH + DEC_BATCH, :]
            ob = oall_s[(2 * jq + 1) * SPITCH:(2 * jq + 1) * SPITCH + DEC_BATCH, :]
            if jq // 2 == 0:
                blocks.append(jnp.where(lo, oa, pltpu.roll(ob, HEAD_DIM, 1)))
            else:
                blocks.append(jnp.where(lo, pltpu.roll(oa, HEAD_DIM, 1), ob))
        attn = jnp.concatenate(blocks, axis=1)
        u = proj[:, D_ATTN + 2 * D_KV:]
        ys = []
        for j in range(N_SBLK):
            bu = _bdot(u[:, 128 * j:128 * (j + 1)], bblk_ref[j])
            lr = lamr_ref[j]
            li = lami_ref[j]
            cols = slice(SBLK * j, SBLK * (j + 1))
            sr = x0r_ref[:, cols]
            si = x0i_ref[:, cols]
            nr = lr * sr - li * si + bu[:, 0:SBLK]
            ni = lr * si + li * sr + bu[:, SBLK:2 * SBLK]
            sre_ref[:, cols] = nr
            sim_ref[:, cols] = ni
            ys.append(_bdot(jnp.concatenate([nr, ni], axis=1), cblk_ref[j]))
        sn = _ssm_post(jnp.concatenate(ys, axis=1), u, dskip_ref[...], wglu_ref[...], bglu_ref[...], gs_ref[...])
        an = _rms(attn, ga_ref[...])
        h = x_ref[...] + _bdot(an, wout_ref[0:D_ATTN, :]) + _bdot(sn, wout_ref[D_ATTN:2 * D_ATTN, :])
        h = jnp.concatenate([h, jnp.zeros((DEC_BATCH, D_MODEL), F32)], axis=0)
        hmid_ref[...] = h
        cnt_ref[...] = cnt_in_ref[...]
        _route(h, gffn_ref, wrt_ref, br_ref, tri_ref, cnt_ref, hn_ref, idx_ref, gw_ref, rank_ref)
        knew_ref[...] = proj[:, D_ATTN:D_ATTN + D_KV]
        vnew_ref[...] = proj[:, D_ATTN + D_KV:D_ATTN + 2 * D_KV]


def _mixer_sample(x_s, gmix, win, ck, cv, sbias, sinkc, bblk, lamr, lami, cblk, dskip, wglu, bglu, ga, gs, wout,
                  x0r, x0i, gffn, wrt, br, cnt_in, hbuf, hnbuf, idxbuf, gwbuf, rankbuf):
    nst = SSM_G * SSM_P
    in_specs = [
        _full((DEC_BATCH, D_MODEL)), _full((1, D_MODEL)), _full((D_MODEL, D_IN)),
        pl.BlockSpec((SGRP, WINDOW, D_KV), lambda g: (g, 0, 0)),
        pl.BlockSpec((SGRP, WINDOW, D_KV), lambda g: (g, 0, 0)),
        _full((N_HEADS, WINDOW)), _full((N_HEADS, 1)),
        _full((N_SBLK, 128, 2 * SBLK)), _full((N_SBLK, 1, SBLK)), _full((N_SBLK, 1, SBLK)),
        _full((N_SBLK, 2 * SBLK, 128)), _full((1, D_SSM)), _full((D_SSM, D_SSM)), _full((1, D_SSM)),
        _full((1, D_ATTN)), _full((1, D_SSM)), _full((D_MODEL, D_MODEL)),
        _full((DEC_BATCH, nst)), _full((DEC_BATCH, nst)),
        _full((1, D_MODEL)), _full((N_EXPERTS, D_MODEL)), _full((N_EXPERTS, 1)),
        _full((2 * DEC_BATCH, 2 * DEC_BATCH)), _full((N_EXPERTS, 1)),
    ] + [pl.BlockSpec(memory_space=pl.ANY)] * 5
    tail_block = SAMPLE_ROW0 // (2 * DEC_BATCH)
    route_specs, route_shapes = _route_specs(2 * DEC_BATCH, lambda g: tail_block)
    out_shape = [
        jax.ShapeDtypeStruct((T_PAD, D_MODEL), F32),
        jax.ShapeDtypeStruct((DEC_BATCH, D_KV), F32), jax.ShapeDtypeStruct((DEC_BATCH, D_KV), F32),
        jax.ShapeDtypeStruct((DEC_BATCH, nst), F32), jax.ShapeDtypeStruct((DEC_BATCH, nst), F32),
    ] + route_shapes
    out_specs = [
        pl.BlockSpec((2 * DEC_BATCH, D_MODEL), lambda g: (tail_block, 0)),
        _full((DEC_BATCH, D_KV)), _full((DEC_BATCH, D_KV)),
        _full((DEC_BATCH, nst)), _full((DEC_BATCH, nst)),
    ] + route_specs
    head_rows = pltpu.VMEM((N_HEADS * SPITCH, 128), F32)
    scratch = [pltpu.VMEM((DEC_BATCH, D_IN), F32), head_rows, head_rows, head_rows, head_rows]
    return pl.pallas_call(
        _mixer_sample_kernel, grid=(N_SGRP,), in_specs=in_specs, out_specs=out_specs, out_shape=out_shape,
        scratch_shapes=scratch, input_output_aliases={24: 0, 25: 5, 26: 6, 27: 7, 28: 8}, name="mixer_sample",
        compiler_params=pltpu.CompilerParams(dimension_semantics=("arbitrary",), vmem_limit_bytes=VMEM_LIMIT),
    )(x_s, gmix, win, ck, cv, sbias, sinkc, bblk, lamr, lami, cblk, dskip, wglu, bglu, ga, gs, wout, x0r, x0i,
      gffn, wrt, br, _tri(2 * DEC_BATCH), cnt_in, hbuf, hnbuf, idxbuf, gwbuf, rankbuf)


def _route(h, g_ref, wrt_ref, br_ref, tri_ref, cnt_ref, hn_ref, idx_ref, gw_ref, rank_ref):
    hn = _rms(h, g_ref[...])
    hn_ref[...] = _pack_pairs(hn)
    hn_hi = hn.astype(BF16)
    hn_lo = (hn - hn_hi.astype(F32)).astype(BF16)
    w = wrt_ref[...]
    w_hi = w.astype(BF16)
    w_lo = (w - w_hi.astype(F32)).astype(BF16)
    nt = (((1,), (1,)), ((), ()))
    logits = (lax.dot_general(w_hi, hn_hi, nt, preferred_element_type=F32)
              + lax.dot_general(w_lo, hn_hi, nt, preferred_element_type=F32)
              + lax.dot_general(w_hi, hn_lo, nt, preferred_element_type=F32)) + br_ref[...]
    eidx = lax.broadcasted_iota(jnp.int32, logits.shape, 0)
    vals, onehots = [], []
    l = logits
    for k in range(TOP_K):
        m = jnp.max(l, axis=0, keepdims=True)
        ik = jnp.min(jnp.where(l == m, eidx, N_EXPERTS), axis=0, keepdims=True)
        oh = eidx == ik
        idx_ref[k:k + 1, :] = ik
        vals.append(m)
        onehots.append(oh)
        l = jnp.where(oh, -jnp.inf, l)
    exps = [jnp.exp(v - vals[0]) for v in vals]
    den = exps[0] + exps[1] + exps[2] + exps[3]
    for k in range(TOP_K):
        gw_ref[k:k + 1, :] = exps[k] / den
    member = jnp.zeros(logits.shape, F32)
    for oh in onehots:
        member = member + jnp.where(oh, 1.0, 0.0)
    wblk = tri_ref.shape[0]
    base = cnt_ref[...].astype(F32)
    befores = []
    for cb in range(h.shape[0] // wblk):
        mblk = member[:, cb * wblk:(cb + 1) * wblk]
        befores.append(jnp.dot(mblk.astype(BF16), tri_ref[...], preferred_element_type=F32) + base)
        base = base + jnp.sum(mblk, axis=1, keepdims=True)
    before = jnp.concatenate(befores, axis=1)
    for k in range(TOP_K):
        rank_ref[k:k + 1, :] = jnp.sum(jnp.where(onehots[k], before, 0.0), axis=0, keepdims=True).astype(jnp.int32)
    cnt_ref[...] = base.astype(jnp.int32)


def _route_specs(rows, block):
    specs = [pl.BlockSpec((rows, D_PACK), lambda i: (block(i), 0)), pl.BlockSpec((TOP_K, rows), lambda i: (0, block(i))),
             pl.BlockSpec((TOP_K, rows), lambda i: (0, block(i))), pl.BlockSpec((TOP_K, rows), lambda i: (0, block(i))),
             _full((N_EXPERTS, 1))]
    shapes = [jax.ShapeDtypeStruct((T_PAD, D_PACK), jnp.int32), jax.ShapeDtypeStruct((TOP_K, T_PAD), jnp.int32),
              jax.ShapeDtypeStruct((TOP_K, T_PAD), F32), jax.ShapeDtypeStruct((TOP_K, T_PAD), jnp.int32),
              jax.ShapeDtypeStruct((N_EXPERTS, 1), jnp.int32)]
    return specs, shapes


def _tri(n):
    return jnp.asarray(np.triu(np.ones((n, n), np.float32), 1), BF16)


def _place_kernel(off_ref, idx_ref, rank_ref, pos_ref):
    idx = idx_ref[...]
    pos = rank_ref[...]
    for e in range(N_EXPERTS):
        pos = pos + jnp.where(idx == e, off_ref[e], 0)
    pos_ref[...] = pos


def _place(offsets, idx, rank):
    return pl.pallas_call(
        _place_kernel,
        in_specs=[pl.BlockSpec(memory_space=pltpu.SMEM), pl.BlockSpec(memory_space=pltpu.VMEM),
                  pl.BlockSpec(memory_space=pltpu.VMEM)],
        out_specs=pl.BlockSpec(memory_space=pltpu.VMEM),
        out_shape=jax.ShapeDtypeStruct((TOP_K, T_PAD), jnp.int32), name="place",
    )(offsets, idx, rank)


def _sc_mesh():
    return plsc.VectorSubcoreMesh(core_axis_name="core", subcore_axis_name="subcore")


def _sc_dispatch(rows, pos):
    n, d = rows.shape
    nblk = n // SC_ROWS
    pos_w = pos.reshape(TOP_K, nblk, SC_ROWS).transpose(1, 0, 2)

    @functools.partial(pl.kernel, out_type=jax.ShapeDtypeStruct((P_ROWS, d), rows.dtype), mesh=_sc_mesh(),
                       scratch_types=[], name="dispatch")
    def run(x_hbm, i_hbm, o_hbm):
        def body(x_vmem, i_vmem):
            for k in range(TOP_K):
                pltpu.sync_copy(x_vmem, o_hbm.at[i_vmem.at[0, k]])

        pltpu.emit_pipeline(
            body, grid=(nblk,),
            in_specs=[pl.BlockSpec((SC_ROWS, d), lambda i: (i, 0)),
                      pl.BlockSpec((1, TOP_K, SC_ROWS), lambda i: (i, 0, 0))],
            out_specs=[], core_axis_name=("core", "subcore"), dimension_semantics=(pltpu.PARALLEL,),
        )(x_hbm, i_hbm)

    return run(rows, pos_w)


def _sc_combine(rows, flat_pos):
    _, d = rows.shape
    n = flat_pos.shape[0]
    w = SC_ROWS_COMBINE
    nblk = n // w
    pos_w = flat_pos.reshape(nblk, 1, w)

    @functools.partial(pl.kernel, out_type=jax.ShapeDtypeStruct((n, d), rows.dtype), mesh=_sc_mesh(),
                       scratch_types=[], name="combine")
    def run(x_hbm, i_hbm, o_hbm):
        def body(i_vmem, o_vmem):
            pltpu.sync_copy(x_hbm.at[i_vmem.at[0, 0]], o_vmem)

        pltpu.emit_pipeline(
            body, grid=(nblk,),
            in_specs=[pl.BlockSpec((1, 1, w), lambda i: (i, 0, 0))],
            out_specs=[pl.BlockSpec((w, d), lambda i: (i, 0))],
            core_axis_name=("core", "subcore"), dimension_semantics=(pltpu.PARALLEL,),
        )(i_hbm, o_hbm)

    return run(rows, pos_w)


def _experts_kernel(ts_ref, trow_ref, tsz_ref, xs_hbm, w1_hbm, b1_ref, w2_hbm, b2_ref, ys_hbm,
                    w1_st, w2_st, w1_s, w2_s, xbuf, ybuf, wsem, xsem, ysem):
    e = pl.program_id(0)
    n_valid = ts_ref[N_EXPERTS]

    def w_copies(ex, slot):
        return (pltpu.make_async_copy(w1_hbm.at[ex], w1_st.at[slot], wsem.at[0, slot]),
                pltpu.make_async_copy(w2_hbm.at[ex], w2_st.at[slot], wsem.at[1, slot]))

    def x_copy(t, slot, rows):
        src = xs_hbm.at[pl.ds(pl.multiple_of(trow_ref[t], TM_UNIT), rows)]
        return pltpu.make_async_copy(src, xbuf.at[slot, pl.ds(0, rows)], xsem.at[slot])

    def y_copy(t, slot, rows):
        dst = ys_hbm.at[pl.ds(pl.multiple_of(trow_ref[t], TM_UNIT), rows)]
        return pltpu.make_async_copy(ybuf.at[slot, pl.ds(0, rows)], dst, ysem.at[slot])

    def by_size(t, fn):
        for rows in TILE_SIZES:
            @pl.when(tsz_ref[t] == rows)
            def _():
                fn(rows)

    @pl.when(e == 0)
    def _():
        for c in w_copies(0, 0):
            c.start()
        by_size(0, lambda rows: x_copy(0, 0, rows).start())

    @pl.when(e + 1 < N_EXPERTS)
    def _():
        for c in w_copies(e + 1, (e + 1) % 2):
            c.start()

    wslot = e % 2
    for c in w_copies(e, wslot):
        c.wait()
    for r in range(4):
        rs = slice(256 * r, 256 * (r + 1))
        w1_s[rs, :] = w1_st[wslot, rs, :].astype(BF16)
        w2_s[rs, :] = w2_st[wslot, rs, :].astype(BF16)

    def tile(t, carry):
        slot = t % 2

        @pl.when(t + 1 < n_valid)
        def _():
            by_size(t + 1, lambda rows: x_copy(t + 1, 1 - slot, rows).start())

        by_size(t, lambda rows: x_copy(t, slot, rows).wait())

        @pl.when(t >= 2)
        def _():
            by_size(t - 2, lambda rows: y_copy(t - 2, slot, rows).wait())

        def compute(rows):
            hdn = _bdot(_unpack_pairs(xbuf[slot, 0:rows]), w1_s[...]) + b1_ref[0]
            gt = jnp.minimum(hdn[:, :D_FF], SWIGLU_LIMIT)
            up = jnp.clip(hdn[:, D_FF:], -SWIGLU_LIMIT, SWIGLU_LIMIT)
            act = (up + 1.0) * gt * _sigmoid(SWIGLU_ALPHA * gt)
            ybuf[slot, 0:rows] = _pack_pairs(_bdot(act, w2_s[...]) + b2_ref[0])
            y_copy(t, slot, rows).start()

        by_size(t, compute)
        return carry

    lax.fori_loop(ts_ref[e], ts_ref[e + 1], tile, 0)

    @pl.when(e == N_EXPERTS - 1)
    def _():
        @pl.when(n_valid >= 2)
        def _():
            by_size(n_valid - 2, lambda rows: y_copy(n_valid - 2, n_valid % 2, rows).wait())

        by_size(n_valid - 1, lambda rows: y_copy(n_valid - 1, (n_valid - 1) % 2, rows).wait())


def _tile_plan(counts):
    units = (counts + (TM_UNIT - 1)) // TM_UNIT
    unit_end = jnp.cumsum(units)
    offsets = ((unit_end - units) * TM_UNIT).astype(jnp.int32)
    per_full = TM // TM_UNIT
    n_full = units // per_full
    rest = units % per_full
    has_mid = (rest >= 2).astype(jnp.int32)
    tiles_per = n_full + has_mid + rest % 2
    tile_end = jnp.cumsum(tiles_per)
    tile_start = jnp.concatenate([jnp.zeros((1,), jnp.int32), tile_end.astype(jnp.int32)])
    t = jnp.arange(MAX_TILES, dtype=jnp.int32)[:, None]
    done = tile_end[None, :] <= t
    mine = jnp.sum(done.astype(jnp.int32), axis=1, keepdims=True) == jnp.arange(N_EXPERTS, dtype=jnp.int32)[None, :]
    pick = lambda v: jnp.sum(jnp.where(mine, v[None, :], 0), axis=1)
    j = t[:, 0] - jnp.max(jnp.where(done, tile_end[None, :], 0), axis=1)
    nf, mid = pick(n_full), pick(has_mid)
    rows = jnp.where(j < nf, TM, jnp.where((j == nf) & (mid == 1), TILE_SIZES[1], TILE_SIZES[2]))
    row0 = pick(offsets) + jnp.minimum(j, nf) * TM + jnp.where(j > nf, TILE_SIZES[1], 0)
    valid = t[:, 0] < tile_end[-1]
    tile_row = jnp.where(valid, row0, 0).astype(jnp.int32)
    tile_rows = jnp.where(valid, rows, 0).astype(jnp.int32)
    return offsets, tile_start, tile_row, tile_rows


def _experts(tile_start, tile_row, tile_rows, xs, w_up, b_up, w_down, b_down):
    wsel = lambda e, *_: (e, 0, 0)
    hbm = pl.BlockSpec(memory_space=pl.ANY)
    grid_spec = pltpu.PrefetchScalarGridSpec(
        num_scalar_prefetch=3, grid=(N_EXPERTS,),
        in_specs=[hbm, hbm, pl.BlockSpec((1, 1, 2 * D_FF), wsel), hbm, pl.BlockSpec((1, 1, D_MODEL), wsel)],
        out_specs=hbm,
        scratch_shapes=[pltpu.VMEM((2, D_MODEL, 2 * D_FF), F32), pltpu.VMEM((2, D_FF, D_MODEL), F32),
                        pltpu.VMEM((D_MODEL, 2 * D_FF), BF16), pltpu.VMEM((D_FF, D_MODEL), BF16),
                        pltpu.VMEM((2, TM, D_PACK), jnp.int32), pltpu.VMEM((2, TM, D_PACK), jnp.int32),
                        pltpu.SemaphoreType.DMA((2, 2)), pltpu.SemaphoreType.DMA((2,)),
                        pltpu.SemaphoreType.DMA((2,))],
    )
    return pl.pallas_call(
        _experts_kernel, grid_spec=grid_spec, out_shape=jax.ShapeDtypeStruct((P_ROWS, D_PACK), jnp.int32),
        name="experts",
        compiler_params=pltpu.CompilerParams(dimension_semantics=("arbitrary",), vmem_limit_bytes=VMEM_LIMIT),
    )(tile_start, tile_row, tile_rows, xs, w_up, b_up.reshape(N_EXPERTS, 1, 2 * D_FF), w_down,
      b_down.reshape(N_EXPERTS, 1, D_MODEL))


def _final_kernel(h_ref, yg_ref, gw_ref, p_ref, gple_ref, wg_ref, wp_ref, gfin_ref, *rest):
    out_ref = rest[-1]
    rows = h_ref.shape[0]
    gw = gw_ref[...]
    h = h_ref[...]
    for k in range(TOP_K):
        h = h + gw[:, k:k + 1] * _unpack_pairs(yg_ref[k])
    gate = _sigmoid(_bdot(_rms(h, gple_ref[...]), wg_ref[...]))
    h = h + gate * _bdot(p_ref[...].reshape(rows, PLE_DIM), wp_ref[...])
    out_ref[...] = _rms(h, gfin_ref[...]).reshape(out_ref.shape)


def _final_prompt(part, hmid, yg, gwt, p_prompt, gple, wg, wp, gfin, y_prev):
    nb = 4
    rows = nb * CHUNK
    nbh = BATCH // nb
    c0 = part * PART_CHUNKS
    rb = lambda c, b: ((c0 + c) * nbh + b, 0)
    in_specs = [pl.BlockSpec((rows, D_MODEL), rb),
                pl.BlockSpec((TOP_K, rows, D_PACK), lambda c, b: (0, c * nbh + b, 0)),
                pl.BlockSpec((rows, TOP_K), rb),
                pl.BlockSpec((nb, CHUNK, PLE_DIM), lambda c, b: (b, c0 + c, 0)),
                _full((1, D_MODEL)), _full((D_MODEL, D_MODEL)), _full((PLE_DIM, D_MODEL)), _full((1, D_MODEL))]
    args = [hmid, yg, gwt, p_prompt, gple, wg, wp, gfin]
    aliases = {}
    if y_prev is not None:
        in_specs.append(pl.BlockSpec(memory_space=pl.ANY))
        args.append(y_prev)
        aliases = {len(args) - 1: 0}
    return pl.pallas_call(
        _final_kernel, grid=(PART_CHUNKS, nbh), in_specs=in_specs,
        out_specs=pl.BlockSpec((nb, CHUNK, D_MODEL), lambda c, b: (b, c0 + c, 0)),
        out_shape=jax.ShapeDtypeStruct((BATCH, SEQ, D_MODEL), F32), name="final_prompt",
        input_output_aliases=aliases,
        compiler_params=pltpu.CompilerParams(dimension_semantics=("arbitrary", "arbitrary"),
                                             vmem_limit_bytes=VMEM_LIMIT),
    )(*args)


def _final_sample(hmid, yg, gwt, p_sample, gple, wg, wp, gfin):
    blk = SAMPLE_ROW0 // DEC_BATCH
    return pl.pallas_call(
        _final_kernel, grid=(1,),
        in_specs=[pl.BlockSpec((DEC_BATCH, D_MODEL), lambda i: (blk, 0)),
                  pl.BlockSpec((TOP_K, DEC_BATCH, D_PACK), lambda i: (0, PART_CHUNKS * ROWS // DEC_BATCH, 0)),
                  pl.BlockSpec((DEC_BATCH, TOP_K), lambda i: (blk, 0)),
                  _full((DEC_BATCH, PLE_DIM)),
                  _full((1, D_MODEL)), _full((D_MODEL, D_MODEL)), _full((PLE_DIM, D_MODEL)), _full((1, D_MODEL))],
        out_specs=_full((DEC_BATCH, D_MODEL)),
        out_shape=jax.ShapeDtypeStruct((DEC_BATCH, D_MODEL), F32), name="final_sample",
        compiler_params=pltpu.CompilerParams(dimension_semantics=("arbitrary",), vmem_limit_bytes=VMEM_LIMIT),
    )(hmid, yg, gwt, p_sample, gple, wg, wp, gfin)


def _alibi_tables():
    slopes = 2.0 ** (-8.0 * (np.arange(N_HEADS, dtype=np.float64) + 1.0) / N_HEADS)
    i = np.arange(CHUNK)[:, None]
    j = np.arange(2 * CHUNK)[None, :]
    dist = i + CHUNK - j
    valid = (dist >= 0) & (dist <= WINDOW)
    tabs = []
    for has_prev in (False, True):
        ok = valid & ((j >= CHUNK) | has_prev)
        tabs.append(np.where(ok[None], -slopes[:, None, None] * dist[None], NEG))
    prompt = np.stack([np.stack([np.concatenate([t[h] for h in grp], axis=0) for grp in (NAT_HEADS, ROT_HEADS)])
                       for t in tabs]).astype(np.float32)
    wb = min(WINDOW, PAST_LEN)
    sample = (-slopes[:, None] * (wb - np.arange(wb))[None, :]).astype(np.float32)
    return prompt, sample


def kernel(x_prompt, x_sample, cache_k_win, cache_v_win, state_ssm_re, state_ssm_im, p_prompt, p_sample, norm_mix, w_in, sinks, ssm_lam_re, ssm_lam_im, ssm_log_step, ssm_b_re, ssm_b_im, ssm_c_re, ssm_c_im, ssm_d, w_glu, b_glu, norm_attn_out, norm_ssm_out, w_out, norm_ffn, w_router, b_router, w_up, b_up, w_down, b_down, norm_ple, w_ple_gate, w_ple_proj, norm_final):
    nst = SSM_G * SSM_P
    bias_np, sbias_np = _alibi_tables()
    bias = jnp.asarray(bias_np)
    sbias = jnp.asarray(sbias_np)

    lbr, lbi, bbr, bbi = _prep(ssm_lam_re[0], ssm_lam_im[0], ssm_log_step[0], ssm_b_re[0], ssm_b_im[0])
    bblk, lamr, lami, cblk = _s5_blocks(lbr, lbi, bbr, bbi, ssm_c_re[0], ssm_c_im[0])

    gmix = norm_mix[0].reshape(1, D_MODEL)
    win = w_in[0].astype(BF16)
    dskip = ssm_d[0].reshape(1, D_SSM)
    wglu = w_glu[0].astype(BF16)
    bglu = b_glu[0].reshape(1, D_SSM)
    ga = norm_attn_out[0].reshape(1, D_ATTN)
    gs = norm_ssm_out[0].reshape(1, D_SSM)
    wout = w_out[0].astype(BF16)
    sink = sinks[0]

    gffn = norm_ffn[0].reshape(1, D_MODEL)
    wrt = w_router[0].T
    br = b_router[0].reshape(N_EXPERTS, 1)
    hbuf, k_last, v_last, re_p, im_p, hnbuf, idxbuf, gwbuf, rankbuf, cnt_p = _mixer_prompt(
        x_prompt, sink, gmix, win, bias, bblk, lamr, lami, cblk, dskip, wglu, bglu, ga, gs, wout, gffn, wrt, br)

    ck = cache_k_win[0].reshape(DEC_BATCH, WINDOW, D_KV)
    cv = cache_v_win[0].reshape(DEC_BATCH, WINDOW, D_KV)
    hmid, k_new, v_new, re_s, im_s, hn, idx, gw, rank, counts = _mixer_sample(
        x_sample.reshape(DEC_BATCH, D_MODEL), gmix, win, ck, cv, sbias, sink.reshape(N_HEADS, 1), bblk, lamr, lami,
        cblk, dskip, wglu, bglu, ga, gs, wout, state_ssm_re[0].reshape(DEC_BATCH, nst),
        state_ssm_im[0].reshape(DEC_BATCH, nst), gffn, wrt, br, cnt_p, hbuf, hnbuf, idxbuf, gwbuf, rankbuf)

    offsets, tile_start, tile_row, tile_rows = _tile_plan(counts[:, 0])

    pos = _place(offsets, idx, rank)
    xs = _sc_dispatch(hn, pos)
    ys = _experts(tile_start, tile_row, tile_rows, xs, w_up[0], b_up[0], w_down[0], b_down[0])
    ygs = []
    for part in range(N_PARTS):
        r0 = part * PART_CHUNKS * ROWS
        r1 = T_PAD if part == N_PARTS - 1 else r0 + PART_CHUNKS * ROWS
        ygs.append(_sc_combine(ys, pos[:, r0:r1].reshape(TOP_K * (r1 - r0))).reshape(TOP_K, r1 - r0, D_PACK))

    gwt = gw.T
    gple = norm_ple[0].reshape(1, D_MODEL)
    wg = w_ple_gate[0].astype(BF16)
    wp = w_ple_proj[0].astype(BF16)
    gfin = norm_final.reshape(1, D_MODEL)
    y_prompt = None
    for part in range(N_PARTS):
        y_prompt = _final_prompt(part, hmid, ygs[part], gwt, p_prompt[0], gple, wg, wp, gfin, y_prompt)
    y_sample = _final_sample(hmid, ygs[-1], gwt, p_sample[0].reshape(DEC_BATCH, PLE_DIM), gple, wg, wp, gfin)

    k_win_s = jnp.concatenate([ck[:, 1:], k_new[:, None, :]], axis=1)
    v_win_s = jnp.concatenate([cv[:, 1:], v_new[:, None, :]], axis=1)
    kv5 = (1, BATCH, CHUNK, N_KV, HEAD_DIM)
    skv5 = (1, DEC_BATCH, WINDOW, N_KV, HEAD_DIM)
    return (y_prompt, y_sample.reshape(DEC_BATCH, 1, D_MODEL),
            k_last.reshape(kv5), v_last.reshape(kv5),
            re_p.reshape(1, BATCH, SSM_G, SSM_P), im_p.reshape(1, BATCH, SSM_G, SSM_P),
            k_win_s.reshape(skv5), v_win_s.reshape(skv5),
            re_s.reshape(1, DEC_BATCH, SSM_G, SSM_P), im_s.reshape(1, DEC_BATCH, SSM_G, SSM_P))
```

```python
import functools

import numpy as np
import jax
import jax.numpy as jnp
from jax import lax
from jax.experimental import pallas as pl
from jax.experimental.pallas import tpu as pltpu
from jax.experimental.pallas import tpu_sc as plsc

F32 = jnp.float32
BF16 = jnp.bfloat16

D_MODEL = 1024
BATCH = 8
SEQ = 2048
DEC_BATCH = 128
PAST_LEN = 16384
HEAD_DIM = 64
D_ATTN = 512
N_HEADS = 8
N_KV = 2
D_KV = N_KV * HEAD_DIM
WINDOW = 128
D_SSM = 512
SSM_H = 16
SSM_G = 32
SSM_P = 64
D_IN = D_ATTN + 2 * D_KV + D_SSM
N_EXPERTS = 32
TOP_K = 4
D_FF = 1024
SWIGLU_LIMIT = 7.0
SWIGLU_ALPHA = 1.702
PLE_DIM = 256
EPS = 1e-5
NEG = -1e30

CHUNK = 128
N_CHUNKS = SEQ // CHUNK
ROWS = BATCH * CHUNK
PITCH = CHUNK + 8
NAT_HEADS = (0, 2, 5, 7)
ROT_HEADS = (1, 3, 4, 6)
ATTN_UNROLL = 2
SUB_T = 64
SUB_ROWS = SUB_T * BATCH
N_SBLK = 4
SBLK = 512
T_REAL = BATCH * SEQ + DEC_BATCH
T_PAD = T_REAL + 128
SAMPLE_ROW0 = BATCH * SEQ
ROUTE_BLOCK = 512
TILE_SIZES = (1024, 512, 256)
TM = TILE_SIZES[0]
TM_UNIT = TILE_SIZES[-1]
P_ROWS = (T_PAD * TOP_K + N_EXPERTS * (TM_UNIT - 1)) // TM_UNIT * TM_UNIT
MAX_TILES = P_ROWS // TM + 2 * N_EXPERTS + 1
SC_ROWS = 40
SC_ROWS_COMBINE = 32
N_PARTS = 4
PART_CHUNKS = N_CHUNKS // N_PARTS
D_PACK = D_MODEL // 2
VMEM_LIMIT = 56 * 1024 * 1024


def _rms(x, g):
    return x * lax.rsqrt(jnp.mean(x * x, axis=-1, keepdims=True) + EPS) * g


def _sigmoid(x):
    return 1.0 / (1.0 + jnp.exp(-x))


def _gelu_tanh(x):
    c = np.float32(np.sqrt(2.0 / np.pi))
    return 0.5 * x * (1.0 + jnp.tanh(c * (x + 0.044715 * (x * x * x))))


def _bdot(a, b):
    return jnp.dot(a.astype(BF16), b, preferred_element_type=F32)


def _pack_pairs(x):
    n = x.shape[1] // 2
    lo = lax.bitcast_convert_type(x[:, :n].astype(BF16).astype(F32), jnp.int32)
    hi = lax.bitcast_convert_type(x[:, n:].astype(BF16).astype(F32), jnp.int32)
    return lax.shift_right_logical(lo, 16) | hi


def _unpack_pairs(w):
    lo = lax.bitcast_convert_type(lax.shift_left(w, 16), F32)
    hi = lax.bitcast_convert_type(w & jnp.int32(-65536), F32)
    return jnp.concatenate([lo, hi], axis=1)


def _full(shape):
    n = len(shape)
    return pl.BlockSpec(shape, lambda *_: (0,) * n)


def _prep_kernel(lr_ref, li_ref, ls_ref, br_ref, bi_ref, lbr_ref, lbi_ref, bbr_ref, bbi_ref):
    lr = lr_ref[...]
    li = li_ref[...]
    step = jnp.exp(ls_ref[...])
    zr = lr * step
    zi = li * step
    mag = jnp.exp(zr)
    lbr = mag * jnp.cos(zi)
    lbi = mag * jnp.sin(zi)
    lbr_ref[...] = lbr
    lbi_ref[...] = lbi
    nr = lbr - 1.0
    den = lr * lr + li * li
    cr = (nr * lr + lbi * li) / den
    ci = (lbi * lr - nr * li) / den
    br = br_ref[...]
    bi = bi_ref[...]
    bbr_ref[...] = cr * br - ci * bi
    bbi_ref[...] = cr * bi + ci * br


def _prep(lam_re, lam_im, log_step, b_re, b_im):
    g, p, h = SSM_G, SSM_P, SSM_H
    out = pl.pallas_call(
        _prep_kernel,
        out_shape=[jax.ShapeDtypeStruct((g, 1, p), F32), jax.ShapeDtypeStruct((g, 1, p), F32),
                   jax.ShapeDtypeStruct((g, h, p), F32), jax.ShapeDtypeStruct((g, h, p), F32)],
        name="s5_prep",
    )(lam_re.reshape(g, 1, p), lam_im.reshape(g, 1, p), log_step.reshape(g, 1, 1),
      jnp.transpose(b_re, (0, 2, 1)), jnp.transpose(b_im, (0, 2, 1)))
    return out


def _s5_blocks(lbr, lbi, bbr, bbi, c_re, c_im):
    eye = jnp.eye(8, dtype=F32)
    shp = (N_SBLK, 8, SSM_H, SSM_P)

    def in_map(b):
        return (b.reshape(shp)[:, :, :, None, :] * eye[None, :, None, :, None]).reshape(N_SBLK, 128, SBLK)

    def out_map(c):
        ct = jnp.transpose(c.reshape(shp), (0, 1, 3, 2))
        return (ct[:, :, :, None, :] * eye[None, :, None, :, None]).reshape(N_SBLK, SBLK, 128)

    bblk = jnp.concatenate([in_map(bbr), in_map(bbi)], axis=-1).astype(BF16)
    cblk = jnp.concatenate([out_map(c_re), -out_map(c_im)], axis=1).astype(BF16)
    return bblk, lbr.reshape(N_SBLK, 1, SBLK), lbi.reshape(N_SBLK, 1, SBLK), cblk


def _ssm_post(y_lin, u, dskip, wglu, bglu, gs):
    y = _gelu_tanh(y_lin + dskip * u)
    y = y * _sigmoid(_bdot(y, wglu) + bglu)
    return _rms(y, gs)


def _mixer_prompt_kernel(sinks_ref, x_ref, gmix_ref, win_ref, bias_ref, bblk_ref, lamr_ref, lami_ref,
                         cblk_ref, dskip_ref, wglu_ref, bglu_ref, ga_ref, gs_ref, wout_ref,
                         gffn_ref, wrt_ref, br_ref, tri_ref,
                         hmid_ref, klast_ref, vlast_ref, sre_ref, sim_ref, hn_ref, idx_ref, gw_ref, rank_ref, cnt_ref,
                         proj_s, u_s, kprev_s, kprevr_s, vprev_s, vprevr_s, attn_s, ssm_s, utb_s, bu_s, xs_s, ytb_s):
    c = pl.program_id(0)

    @pl.when(c == 0)
    def _():
        zkv = jnp.zeros(kprev_s.shape, BF16)
        kprev_s[...] = zkv
        kprevr_s[...] = zkv
        vprev_s[...] = zkv
        vprevr_s[...] = zkv
        sre_ref[...] = jnp.zeros(sre_ref.shape, F32)
        sim_ref[...] = jnp.zeros(sim_ref.shape, F32)
        cnt_ref[...] = jnp.zeros(cnt_ref.shape, jnp.int32)

    x = x_ref[...].reshape(ROWS, D_MODEL)
    proj = _bdot(_rms(x, gmix_ref[...]), win_ref[...])
    u0 = D_ATTN + 2 * D_KV
    proj_s[...] = proj[:, 0:u0]
    for jb in range(D_SSM // 128):
        for b in range(BATCH):
            u_s[jb, b * PITCH:b * PITCH + CHUNK, :] = proj[b * CHUNK:(b + 1) * CHUNK, u0 + 128 * jb:u0 + 128 * (jb + 1)]

    lo = lax.broadcasted_iota(jnp.int32, (CHUNK, 128), 1) < HEAD_DIM
    hi = jnp.logical_not(lo)
    table = jnp.minimum(c, 1)
    hrow = lax.broadcasted_iota(jnp.int32, (4 * CHUNK, 1), 0) // CHUNK

    def sink_col(heads):
        col = jnp.full((4 * CHUNK, 1), sinks_ref[heads[3]], F32)
        for n in (2, 1, 0):
            col = jnp.where(hrow == n, sinks_ref[heads[n]], col)
        return col

    sink_nat = sink_col(NAT_HEADS)
    sink_rot = sink_col(ROT_HEADS)

    def attend(q, k, v, bias, sink):
        s = lax.dot_general(q, k, (((1,), (1,)), ((), ())), preferred_element_type=F32) + bias
        m = jnp.maximum(jnp.max(s, axis=-1, keepdims=True), sink)
        p = jnp.exp(s - m)
        den = jnp.sum(p, axis=-1, keepdims=True) + jnp.exp(sink - m)
        return jnp.dot(p.astype(BF16), v, preferred_element_type=F32) / den

    def attn_body(b, carry):
        r0 = pl.multiple_of(b * CHUNK, CHUNK)
        rows = pl.ds(r0, CHUNK)
        kb = proj_s[rows, D_ATTN:D_ATTN + D_KV]
        vb = proj_s[rows, D_ATTN + D_KV:D_ATTN + 2 * D_KV]
        kb16 = kb.astype(BF16)
        vb16 = vb.astype(BF16)
        kbr16 = pltpu.roll(kb, HEAD_DIM, 1).astype(BF16)
        vbr16 = pltpu.roll(vb, HEAD_DIM, 1).astype(BF16)
        k_nat = jnp.concatenate([kprev_s[b], kb16], axis=0)
        k_rot = jnp.concatenate([kprevr_s[b], kbr16], axis=0)
        v_nat = jnp.concatenate([vprev_s[b], vb16], axis=0)
        v_rot = jnp.concatenate([vprevr_s[b], vbr16], axis=0)
        q2 = [proj_s[rows, 128 * jq:128 * (jq + 1)] * (HEAD_DIM ** -0.5) for jq in range(N_HEADS // 2)]
        q_nat = jnp.concatenate([jnp.where(lo if h % 2 == 0 else hi, q2[h // 2], 0.0) for h in NAT_HEADS],
                                axis=0).astype(BF16)
        q_rot = jnp.concatenate([jnp.where(lo if h % 2 == 0 else hi, q2[h // 2], 0.0) for h in ROT_HEADS],
                                axis=0).astype(BF16)
        o_nat = attend(q_nat, k_nat, v_nat, bias_ref[table, 0], sink_nat)
        o_rot = attend(q_rot, k_rot, v_rot, bias_ref[table, 1], sink_rot)
        for jq in range(N_HEADS // 2):
            blk = slice(CHUNK * jq, CHUNK * (jq + 1))
            even, odd = (o_nat, o_rot) if jq < 2 else (o_rot, o_nat)
            attn_s[rows, 128 * jq:128 * (jq + 1)] = jnp.where(lo, even[blk], odd[blk])
        kprev_s[b] = kb16
        kprevr_s[b] = kbr16
        vprev_s[b] = vb16
        vprevr_s[b] = vbr16
        return carry

    def attn_group(i, carry):
        for n in range(ATTN_UNROLL):
            attn_body(ATTN_UNROLL * i + n, carry)
        return carry

    lax.fori_loop(0, BATCH // ATTN_UNROLL, attn_group, 0)

    for sc in range(CHUNK // SUB_T):
        t0 = sc * SUB_T
        for i in range(SUB_T):
            for jb in range(D_SSM // 128):
                utb_s[i * BATCH:(i + 1) * BATCH, 128 * jb:128 * (jb + 1)] = (
                    u_s[jb, pl.ds(t0 + i, BATCH, stride=PITCH), :])
        u_tb = utb_s[...]
        for j in range(N_SBLK):
            bu_s[...] = _bdot(u_tb[:, 128 * j:128 * (j + 1)], bblk_ref[j])
            lr = jnp.broadcast_to(lamr_ref[j], (BATCH, SBLK))
            li = jnp.broadcast_to(lami_ref[j], (BATCH, SBLK))
            cols = slice(SBLK * j, SBLK * (j + 1))

            def step(i, carry):
                sr, si = carry
                r = pl.ds(pl.multiple_of(i * BATCH, BATCH), BATCH)
                nr = lr * sr - li * si + bu_s[r, 0:SBLK]
                ni = lr * si + li * sr + bu_s[r, SBLK:2 * SBLK]
                xs_s[r, 0:SBLK] = nr
                xs_s[r, SBLK:2 * SBLK] = ni
                return nr, ni

            sr, si = lax.fori_loop(0, SUB_T, step, (sre_ref[:, cols], sim_ref[:, cols]), unroll=True)
            sre_ref[:, cols] = sr
            sim_ref[:, cols] = si
            ytb_s[:, 128 * j:128 * (j + 1)] = _bdot(xs_s[...], cblk_ref[j])
        yn = _ssm_post(ytb_s[...], u_tb, dskip_ref[...], wglu_ref[...], bglu_ref[...], gs_ref[...])
        for i in range(SUB_T):
            for jb in range(D_SSM // 128):
                ssm_s[jb, pl.ds(t0 + i, BATCH, stride=PITCH), :] = (
                    yn[i * BATCH:(i + 1) * BATCH, 128 * jb:128 * (jb + 1)])

    an = _rms(attn_s[...], ga_ref[...])
    sn = jnp.concatenate(
        [jnp.concatenate([ssm_s[jb, b * PITCH:b * PITCH + CHUNK, :] for b in range(BATCH)], axis=0)
         for jb in range(D_SSM // 128)], axis=1)
    h = x + _bdot(an, wout_ref[0:D_ATTN, :]) + _bdot(sn, wout_ref[D_ATTN:2 * D_ATTN, :])
    hmid_ref[...] = h
    _route(h, gffn_ref, wrt_ref, br_ref, tri_ref, cnt_ref, hn_ref, idx_ref, gw_ref, rank_ref)

    @pl.when(c == N_CHUNKS - 1)
    def _():
        klast_ref[...] = proj_s[:, D_ATTN:D_ATTN + D_KV].reshape(BATCH, CHUNK, D_KV)
        vlast_ref[...] = proj_s[:, D_ATTN + D_KV:D_ATTN + 2 * D_KV].reshape(BATCH, CHUNK, D_KV)


def _mixer_prompt(x_prompt, sinks, gmix, win, bias, bblk, lamr, lami, cblk, dskip, wglu, bglu, ga, gs, wout,
                  gffn, wrt, br):
    smem = pl.BlockSpec(memory_space=pltpu.SMEM)
    in_specs = [
        smem,
        pl.BlockSpec((BATCH, CHUNK, D_MODEL), lambda c: (0, c, 0)),
        _full((1, D_MODEL)), _full((D_MODEL, D_IN)), _full((2, 2, 4 * CHUNK, 2 * CHUNK)),
        _full((N_SBLK, 128, 2 * SBLK)), _full((N_SBLK, 1, SBLK)), _full((N_SBLK, 1, SBLK)),
        _full((N_SBLK, 2 * SBLK, 128)), _full((1, D_SSM)), _full((D_SSM, D_SSM)), _full((1, D_SSM)),
        _full((1, D_ATTN)), _full((1, D_SSM)), _full((D_MODEL, D_MODEL)),
        _full((1, D_MODEL)), _full((N_EXPERTS, D_MODEL)), _full((N_EXPERTS, 1)), _full((ROUTE_BLOCK, ROUTE_BLOCK)),
    ]
    route_specs, route_shapes = _route_specs(ROWS, lambda c: c)
    out_shape = [
        jax.ShapeDtypeStruct((T_PAD, D_MODEL), F32),
        jax.ShapeDtypeStruct((BATCH, CHUNK, D_KV), F32),
        jax.ShapeDtypeStruct((BATCH, CHUNK, D_KV), F32),
        jax.ShapeDtypeStruct((BATCH, SSM_G * SSM_P), F32),
        jax.ShapeDtypeStruct((BATCH, SSM_G * SSM_P), F32),
    ] + route_shapes
    out_specs = [
        pl.BlockSpec((ROWS, D_MODEL), lambda c: (c, 0)),
        _full((BATCH, CHUNK, D_KV)), _full((BATCH, CHUNK, D_KV)),
        _full((BATCH, SSM_G * SSM_P)), _full((BATCH, SSM_G * SSM_P)),
    ] + route_specs
    kv_scr = pltpu.VMEM((BATCH, CHUNK, D_KV), BF16)
    scratch = [
        pltpu.VMEM((ROWS, D_ATTN + 2 * D_KV), F32), pltpu.VMEM((D_SSM // 128, BATCH * PITCH, 128), F32),
        kv_scr, kv_scr, kv_scr, kv_scr,
        pltpu.VMEM((ROWS, D_ATTN), F32), pltpu.VMEM((D_SSM // 128, BATCH * PITCH, 128), F32),
        pltpu.VMEM((SUB_ROWS, D_SSM), F32), pltpu.VMEM((SUB_ROWS, 2 * SBLK), F32),
        pltpu.VMEM((SUB_ROWS, 2 * SBLK), F32), pltpu.VMEM((SUB_ROWS, D_SSM), F32),
    ]
    return pl.pallas_call(
        _mixer_prompt_kernel, grid=(N_CHUNKS,), in_specs=in_specs, out_specs=out_specs, out_shape=out_shape,
        scratch_shapes=scratch, name="mixer_prompt",
        compiler_params=pltpu.CompilerParams(dimension_semantics=("arbitrary",), vmem_limit_bytes=VMEM_LIMIT),
    )(sinks, x_prompt, gmix, win, bias, bblk, lamr, lami, cblk, dskip, wglu, bglu, ga, gs, wout,
      gffn, wrt, br, _tri(ROUTE_BLOCK))


SGRP = 16
SPITCH = DEC_BATCH + 8
N_SGRP = DEC_BATCH // SGRP


def _mixer_sample_kernel(x_ref, gmix_ref, win_ref, ck_ref, cv_ref, sbias_ref, sinkc_ref, bblk_ref, lamr_ref,
                         lami_ref, cblk_ref, dskip_ref, wglu_ref, bglu_ref, ga_ref, gs_ref, wout_ref,
                         x0r_ref, x0i_ref, gffn_ref, wrt_ref, br_ref, tri_ref, cnt_in_ref,
                         hbuf_ref, hnbuf_ref, idxbuf_ref, gwbuf_ref, rankbuf_ref,
                         hmid_ref, knew_ref, vnew_ref, sre_ref, sim_ref, hn_ref, idx_ref, gw_ref, rank_ref, cnt_ref,
                         proj_s, qall_s, oall_s, kn8_s, vn8_s):
    del hbuf_ref, hnbuf_ref, idxbuf_ref, gwbuf_ref, rankbuf_ref
    g = pl.program_id(0)
    lo = lax.broadcasted_iota(jnp.int32, (DEC_BATCH, 128), 1) < HEAD_DIM

    @pl.when(g == 0)
    def _():
        proj = _bdot(_rms(x_ref[...], gmix_ref[...]), win_ref[...])
        proj_s[...] = proj
        for h in range(N_HEADS):
            jq, half, kv = h // 2, h % 2, h // 4
            q2 = proj[:, 128 * jq:128 * (jq + 1)] * (HEAD_DIM ** -0.5)
            if half != kv:
                q2 = pltpu.roll(q2, HEAD_DIM, 1)
            hr = slice(h * SPITCH, h * SPITCH + DEC_BATCH)
            qall_s[hr, :] = jnp.where(lo if kv == 0 else jnp.logical_not(lo), q2, 0.0)
            kn8_s[hr, :] = proj[:, D_ATTN:D_ATTN + D_KV]
            vn8_s[hr, :] = proj[:, D_ATTN + D_KV:D_ATTN + 2 * D_KV]

    def head_rows(ref):
        return jnp.stack([ref[pl.ds(g * SGRP + ii, N_HEADS, stride=SPITCH), :] for ii in range(SGRP)], axis=0)

    qe = head_rows(qall_s)
    knew = head_rows(kn8_s)
    vnew = head_rows(vn8_s)
    s = jnp.einsum('bhd,bkd->bhk', qe.astype(BF16), ck_ref[...].astype(BF16),
                   preferred_element_type=F32) + sbias_ref[...][None]
    s_new = jnp.sum(qe * knew, axis=-1, keepdims=True)
    sink = sinkc_ref[...][None]
    m = jnp.maximum(jnp.maximum(jnp.max(s, axis=-1, keepdims=True), s_new), sink)
    p = jnp.exp(s - m)
    p_new = jnp.exp(s_new - m)
    den = jnp.sum(p, axis=-1, keepdims=True) + p_new + jnp.exp(sink - m)
    o = (jnp.einsum('bhk,bkd->bhd', p.astype(BF16), cv_ref[...].astype(BF16),
                    preferred_element_type=F32) + p_new * vnew) / den
    for ii in range(SGRP):
        oall_s[pl.ds(g * SGRP + ii, N_HEADS, stride=SPITCH), :] = o[ii]

    @pl.when(g == N_SGRP - 1)
    def _():
        proj = proj_s[...]
        blocks = []
        for jq in range(N_HEADS // 2):
            oa = oall_s[(2 * jq) * SPITCH:(2 * jq) * SPITCH + DEC_BATCH, :]
            ob = oall_s[(2 * jq + 1) * SPITCH:(2 * jq + 1) * SPITCH + DEC_BATCH, :]
            if jq // 2 == 0:
                blocks.append(jnp.where(lo, oa, pltpu.roll(ob, HEAD_DIM, 1)))
            else:
                blocks.append(jnp.where(lo, pltpu.roll(oa, HEAD_DIM, 1), ob))
        attn = jnp.concatenate(blocks, axis=1)
        u = proj[:, D_ATTN + 2 * D_KV:]
        ys = []
        for j in range(N_SBLK):
            bu = _bdot(u[:, 128 * j:128 * (j + 1)], bblk_ref[j])
            lr = lamr_ref[j]
            li = lami_ref[j]
            cols = slice(SBLK * j, SBLK * (j + 1))
            sr = x0r_ref[:, cols]
            si = x0i_ref[:, cols]
            nr = lr * sr - li * si + bu[:, 0:SBLK]
            ni = lr * si + li * sr + bu[:, SBLK:2 * SBLK]
            sre_ref[:, cols] = nr
            sim_ref[:, cols] = ni
            ys.append(_bdot(jnp.concatenate([nr, ni], axis=1), cblk_ref[j]))
        sn = _ssm_post(jnp.concatenate(ys, axis=1), u, dskip_ref[...], wglu_ref[...], bglu_ref[...], gs_ref[...])
        an = _rms(attn, ga_ref[...])
        h = x_ref[...] + _bdot(an, wout_ref[0:D_ATTN, :]) + _bdot(sn, wout_ref[D_ATTN:2 * D_ATTN, :])
        h = jnp.concatenate([h, jnp.zeros((DEC_BATCH, D_MODEL), F32)], axis=0)
        hmid_ref[...] = h
        cnt_ref[...] = cnt_in_ref[...]
        _route(h, gffn_ref, wrt_ref, br_ref, tri_ref, cnt_ref, hn_ref, idx_ref, gw_ref, rank_ref)
        knew_ref[...] = proj[:, D_ATTN:D_ATTN + D_KV]
        vnew_ref[...] = proj[:, D_ATTN + D_KV:D_ATTN + 2 * D_KV]


def _mixer_sample(x_s, gmix, win, ck, cv, sbias, sinkc, bblk, lamr, lami, cblk, dskip, wglu, bglu, ga, gs, wout,
                  x0r, x0i, gffn, wrt, br, cnt_in, hbuf, hnbuf, idxbuf, gwbuf, rankbuf):
    nst = SSM_G * SSM_P
    in_specs = [
        _full((DEC_BATCH, D_MODEL)), _full((1, D_MODEL)), _full((D_MODEL, D_IN)),
        pl.BlockSpec((SGRP, WINDOW, D_KV), lambda g: (g, 0, 0)),
        pl.BlockSpec((SGRP, WINDOW, D_KV), lambda g: (g, 0, 0)),
        _full((N_HEADS, WINDOW)), _full((N_HEADS, 1)),
        _full((N_SBLK, 128, 2 * SBLK)), _full((N_SBLK, 1, SBLK)), _full((N_SBLK, 1, SBLK)),
        _full((N_SBLK, 2 * SBLK, 128)), _full((1, D_SSM)), _full((D_SSM, D_SSM)), _full((1, D_SSM)),
        _full((1, D_ATTN)), _full((1, D_SSM)), _full((D_MODEL, D_MODEL)),
        _full((DEC_BATCH, nst)), _full((DEC_BATCH, nst)),
        _full((1, D_MODEL)), _full((N_EXPERTS, D_MODEL)), _full((N_EXPERTS, 1)),
        _full((2 * DEC_BATCH, 2 * DEC_BATCH)), _full((N_EXPERTS, 1)),
    ] + [pl.BlockSpec(memory_space=pl.ANY)] * 5
    tail_block = SAMPLE_ROW0 // (2 * DEC_BATCH)
    route_specs, route_shapes = _route_specs(2 * DEC_BATCH, lambda g: tail_block)
    out_shape = [
        jax.ShapeDtypeStruct((T_PAD, D_MODEL), F32),
        jax.ShapeDtypeStruct((DEC_BATCH, D_KV), F32), jax.ShapeDtypeStruct((DEC_BATCH, D_KV), F32),
        jax.ShapeDtypeStruct((DEC_BATCH, nst), F32), jax.ShapeDtypeStruct((DEC_BATCH, nst), F32),
    ] + route_shapes
    out_specs = [
        pl.BlockSpec((2 * DEC_BATCH, D_MODEL), lambda g: (tail_block, 0)),
        _full((DEC_BATCH, D_KV)), _full((DEC_BATCH, D_KV)),
        _full((DEC_BATCH, nst)), _full((DEC_BATCH, nst)),
    ] + route_specs
    head_rows = pltpu.VMEM((N_HEADS * SPITCH, 128), F32)
    scratch = [pltpu.VMEM((DEC_BATCH, D_IN), F32), head_rows, head_rows, head_rows, head_rows]
    return pl.pallas_call(
        _mixer_sample_kernel, grid=(N_SGRP,), in_specs=in_specs, out_specs=out_specs, out_shape=out_shape,
        scratch_shapes=scratch, input_output_aliases={24: 0, 25: 5, 26: 6, 27: 7, 28: 8}, name="mixer_sample",
        compiler_params=pltpu.CompilerParams(dimension_semantics=("arbitrary",), vmem_limit_bytes=VMEM_LIMIT),
    )(x_s, gmix, win, ck, cv, sbias, sinkc, bblk, lamr, lami, cblk, dskip, wglu, bglu, ga, gs, wout, x0r, x0i,
      gffn, wrt, br, _tri(2 * DEC_BATCH), cnt_in, hbuf, hnbuf, idxbuf, gwbuf, rankbuf)


def _route(h, g_ref, wrt_ref, br_ref, tri_ref, cnt_ref, hn_ref, idx_ref, gw_ref, rank_ref):
    hn = _rms(h, g_ref[...])
    hn_ref[...] = _pack_pairs(hn)
    hn_hi = hn.astype(BF16)
    hn_lo = (hn - hn_hi.astype(F32)).astype(BF16)
    w = wrt_ref[...]
    w_hi = w.astype(BF16)
    w_lo = (w - w_hi.astype(F32)).astype(BF16)
    nt = (((1,), (1,)), ((), ()))
    both = lax.dot_general(jnp.concatenate([w_hi, w_lo], axis=0), hn_hi, nt, preferred_element_type=F32)
    logits = (both[:N_EXPERTS] + both[N_EXPERTS:]
              + lax.dot_general(w_hi, hn_lo, nt, preferred_element_type=F32)) + br_ref[...]
    eidx = lax.broadcasted_iota(jnp.int32, logits.shape, 0)
    vals, onehots = [], []
    l = logits
    for k in range(TOP_K):
        m = jnp.max(l, axis=0, keepdims=True)
        ik = jnp.min(jnp.where(l == m, eidx, N_EXPERTS), axis=0, keepdims=True)
        oh = eidx == ik
        idx_ref[k:k + 1, :] = ik
        vals.append(m)
        onehots.append(oh)
        l = jnp.where(oh, -jnp.inf, l)
    exps = [jnp.exp(v - vals[0]) for v in vals]
    den = exps[0] + exps[1] + exps[2] + exps[3]
    for k in range(TOP_K):
        gw_ref[k:k + 1, :] = exps[k] / den
    member = jnp.zeros(logits.shape, F32)
    for oh in onehots:
        member = member + jnp.where(oh, 1.0, 0.0)
    wblk = tri_ref.shape[0]
    base = cnt_ref[...].astype(F32)
    befores = []
    for cb in range(h.shape[0] // wblk):
        mblk = member[:, cb * wblk:(cb + 1) * wblk]
        befores.append(jnp.dot(mblk.astype(BF16), tri_ref[...], preferred_element_type=F32) + base)
        base = base + jnp.sum(mblk, axis=1, keepdims=True)
    before = jnp.concatenate(befores, axis=1)
    for k in range(TOP_K):
        rank_ref[k:k + 1, :] = jnp.sum(jnp.where(onehots[k], before, 0.0), axis=0, keepdims=True).astype(jnp.int32)
    cnt_ref[...] = base.astype(jnp.int32)


def _route_specs(rows, block):
    specs = [pl.BlockSpec((rows, D_PACK), lambda i: (block(i), 0)), pl.BlockSpec((TOP_K, rows), lambda i: (0, block(i))),
             pl.BlockSpec((TOP_K, rows), lambda i: (0, block(i))), pl.BlockSpec((TOP_K, rows), lambda i: (0, block(i))),
             _full((N_EXPERTS, 1))]
    shapes = [jax.ShapeDtypeStruct((T_PAD, D_PACK), jnp.int32), jax.ShapeDtypeStruct((TOP_K, T_PAD), jnp.int32),
              jax.ShapeDtypeStruct((TOP_K, T_PAD), F32), jax.ShapeDtypeStruct((TOP_K, T_PAD), jnp.int32),
              jax.ShapeDtypeStruct((N_EXPERTS, 1), jnp.int32)]
    return specs, shapes


def _tri(n):
    return jnp.asarray(np.triu(np.ones((n, n), np.float32), 1), BF16)


def _place_kernel(off_ref, idx_ref, rank_ref, pos_ref):
    idx = idx_ref[...]
    pos = rank_ref[...]
    for e in range(N_EXPERTS):
        pos = pos + jnp.where(idx == e, off_ref[e], 0)
    pos_ref[...] = pos


def _place(offsets, idx, rank):
    return pl.pallas_call(
        _place_kernel,
        in_specs=[pl.BlockSpec(memory_space=pltpu.SMEM), pl.BlockSpec(memory_space=pltpu.VMEM),
                  pl.BlockSpec(memory_space=pltpu.VMEM)],
        out_specs=pl.BlockSpec(memory_space=pltpu.VMEM),
        out_shape=jax.ShapeDtypeStruct((TOP_K, T_PAD), jnp.int32), name="place",
    )(offsets, idx, rank)


def _sc_mesh():
    return plsc.VectorSubcoreMesh(core_axis_name="core", subcore_axis_name="subcore")


def _sc_dispatch(rows, pos):
    n, d = rows.shape
    nblk = n // SC_ROWS
    pos_w = pos.reshape(TOP_K, nblk, SC_ROWS).transpose(1, 0, 2)

    @functools.partial(pl.kernel, out_type=jax.ShapeDtypeStruct((P_ROWS, d), rows.dtype), mesh=_sc_mesh(),
                       scratch_types=[], name="dispatch")
    def run(x_hbm, i_hbm, o_hbm):
        def body(x_vmem, i_vmem):
            for k in range(TOP_K):
                pltpu.sync_copy(x_vmem, o_hbm.at[i_vmem.at[0, k]])

        pltpu.emit_pipeline(
            body, grid=(nblk,),
            in_specs=[pl.BlockSpec((SC_ROWS, d), lambda i: (i, 0)),
                      pl.BlockSpec((1, TOP_K, SC_ROWS), lambda i: (i, 0, 0))],
            out_specs=[], core_axis_name=("core", "subcore"), dimension_semantics=(pltpu.PARALLEL,),
        )(x_hbm, i_hbm)

    return run(rows, pos_w)


def _sc_combine(rows, flat_pos):
    _, d = rows.shape
    n = flat_pos.shape[0]
    w = SC_ROWS_COMBINE
    nblk = n // w
    pos_w = flat_pos.reshape(nblk, 1, w)

    @functools.partial(pl.kernel, out_type=jax.ShapeDtypeStruct((n, d), rows.dtype), mesh=_sc_mesh(),
                       scratch_types=[], name="combine")
    def run(x_hbm, i_hbm, o_hbm):
        def body(i_vmem, o_vmem):
            pltpu.sync_copy(x_hbm.at[i_vmem.at[0, 0]], o_vmem)

        pltpu.emit_pipeline(
            body, grid=(nblk,),
            in_specs=[pl.BlockSpec((1, 1, w), lambda i: (i, 0, 0))],
            out_specs=[pl.BlockSpec((w, d), lambda i: (i, 0))],
            core_axis_name=("core", "subcore"), dimension_semantics=(pltpu.PARALLEL,),
        )(i_hbm, o_hbm)

    return run(rows, pos_w)


def _experts_kernel(ts_ref, trow_ref, tsz_ref, xs_hbm, w1_hbm, b1_ref, w2_hbm, b2_ref, ys_hbm,
                    w1_st, w2_st, w1_s, w2_s, xbuf, ybuf, wsem, xsem, ysem):
    e = pl.program_id(0)
    n_valid = ts_ref[N_EXPERTS]

    def w_copies(ex, slot):
        return (pltpu.make_async_copy(w1_hbm.at[ex], w1_st.at[slot], wsem.at[0, slot]),
                pltpu.make_async_copy(w2_hbm.at[ex], w2_st.at[slot], wsem.at[1, slot]))

    def x_copy(t, slot, rows):
        src = xs_hbm.at[pl.ds(pl.multiple_of(trow_ref[t], TM_UNIT), rows)]
        return pltpu.make_async_copy(src, xbuf.at[slot, pl.ds(0, rows)], xsem.at[slot])

    def y_copy(t, slot, rows):
        dst = ys_hbm.at[pl.ds(pl.multiple_of(trow_ref[t], TM_UNIT), rows)]
        return pltpu.make_async_copy(ybuf.at[slot, pl.ds(0, rows)], dst, ysem.at[slot])

    def by_size(t, fn):
        for rows in TILE_SIZES:
            @pl.when(tsz_ref[t] == rows)
            def _():
                fn(rows)

    @pl.when(e == 0)
    def _():
        for c in w_copies(0, 0):
            c.start()
        by_size(0, lambda rows: x_copy(0, 0, rows).start())

    @pl.when(e + 1 < N_EXPERTS)
    def _():
        for c in w_copies(e + 1, (e + 1) % 2):
            c.start()

    wslot = e % 2
    for c in w_copies(e, wslot):
        c.wait()
    for r in range(4):
        rs = slice(256 * r, 256 * (r + 1))
        w1_s[rs, :] = w1_st[wslot, rs, :].astype(BF16)
        w2_s[rs, :] = w2_st[wslot, rs, :].astype(BF16)

    def tile(t, carry):
        slot = t % 2

        @pl.when(t + 1 < n_valid)
        def _():
            by_size(t + 1, lambda rows: x_copy(t + 1, 1 - slot, rows).start())

        by_size(t, lambda rows: x_copy(t, slot, rows).wait())

        @pl.when(t >= 2)
        def _():
            by_size(t - 2, lambda rows: y_copy(t - 2, slot, rows).wait())

        def compute(rows):
            hdn = _bdot(_unpack_pairs(xbuf[slot, 0:rows]), w1_s[...]) + b1_ref[0]
            gt = jnp.minimum(hdn[:, :D_FF], SWIGLU_LIMIT)
            up = jnp.clip(hdn[:, D_FF:], -SWIGLU_LIMIT, SWIGLU_LIMIT)
            act = (up + 1.0) * gt * _sigmoid(SWIGLU_ALPHA * gt)
            ybuf[slot, 0:rows] = _pack_pairs(_bdot(act, w2_s[...]) + b2_ref[0])
            y_copy(t, slot, rows).start()

        by_size(t, compute)
        return carry

    lax.fori_loop(ts_ref[e], ts_ref[e + 1], tile, 0)

    @pl.when(e == N_EXPERTS - 1)
    def _():
        @pl.when(n_valid >= 2)
        def _():
            by_size(n_valid - 2, lambda rows: y_copy(n_valid - 2, n_valid % 2, rows).wait())

        by_size(n_valid - 1, lambda rows: y_copy(n_valid - 1, (n_valid - 1) % 2, rows).wait())


def _tile_plan(counts):
    units = (counts + (TM_UNIT - 1)) // TM_UNIT
    unit_end = jnp.cumsum(units)
    offsets = ((unit_end - units) * TM_UNIT).astype(jnp.int32)
    per_full = TM // TM_UNIT
    n_full = units // per_full
    rest = units % per_full
    has_mid = (rest >= 2).astype(jnp.int32)
    tiles_per = n_full + has_mid + rest % 2
    tile_end = jnp.cumsum(tiles_per)
    tile_start = jnp.concatenate([jnp.zeros((1,), jnp.int32), tile_end.astype(jnp.int32)])
    t = jnp.arange(MAX_TILES, dtype=jnp.int32)[:, None]
    done = tile_end[None, :] <= t
    mine = jnp.sum(done.astype(jnp.int32), axis=1, keepdims=True) == jnp.arange(N_EXPERTS, dtype=jnp.int32)[None, :]
    pick = lambda v: jnp.sum(jnp.where(mine, v[None, :], 0), axis=1)
    j = t[:, 0] - jnp.max(jnp.where(done, tile_end[None, :], 0), axis=1)
    nf, mid = pick(n_full), pick(has_mid)
    rows = jnp.where(j < nf, TM, jnp.where((j == nf) & (mid == 1), TILE_SIZES[1], TILE_SIZES[2]))
    row0 = pick(offsets) + jnp.minimum(j, nf) * TM + jnp.where(j > nf, TILE_SIZES[1], 0)
    valid = t[:, 0] < tile_end[-1]
    tile_row = jnp.where(valid, row0, 0).astype(jnp.int32)
    tile_rows = jnp.where(valid, rows, 0).astype(jnp.int32)
    return offsets, tile_start, tile_row, tile_rows


def _experts(tile_start, tile_row, tile_rows, xs, w_up, b_up, w_down, b_down):
    wsel = lambda e, *_: (e, 0, 0)
    hbm = pl.BlockSpec(memory_space=pl.ANY)
    grid_spec = pltpu.PrefetchScalarGridSpec(
        num_scalar_prefetch=3, grid=(N_EXPERTS,),
        in_specs=[hbm, hbm, pl.BlockSpec((1, 1, 2 * D_FF), wsel), hbm, pl.BlockSpec((1, 1, D_MODEL), wsel)],
        out_specs=hbm,
        scratch_shapes=[pltpu.VMEM((2, D_MODEL, 2 * D_FF), F32), pltpu.VMEM((2, D_FF, D_MODEL), F32),
                        pltpu.VMEM((D_MODEL, 2 * D_FF), BF16), pltpu.VMEM((D_FF, D_MODEL), BF16),
                        pltpu.VMEM((2, TM, D_PACK), jnp.int32), pltpu.VMEM((2, TM, D_PACK), jnp.int32),
                        pltpu.SemaphoreType.DMA((2, 2)), pltpu.SemaphoreType.DMA((2,)),
                        pltpu.SemaphoreType.DMA((2,))],
    )
    return pl.pallas_call(
        _experts_kernel, grid_spec=grid_spec, out_shape=jax.ShapeDtypeStruct((P_ROWS, D_PACK), jnp.int32),
        name="experts",
        compiler_params=pltpu.CompilerParams(dimension_semantics=("arbitrary",), vmem_limit_bytes=VMEM_LIMIT),
    )(tile_start, tile_row, tile_rows, xs, w_up, b_up.reshape(N_EXPERTS, 1, 2 * D_FF), w_down,
      b_down.reshape(N_EXPERTS, 1, D_MODEL))


def _final_kernel(h_ref, yg_ref, gw_ref, p_ref, gple_ref, wg_ref, wp_ref, gfin_ref, *rest):
    out_ref = rest[-1]
    rows = h_ref.shape[0]
    gw = gw_ref[...]
    h = h_ref[...]
    for k in range(TOP_K):
        h = h + gw[:, k:k + 1] * _unpack_pairs(yg_ref[k])
    gate = _sigmoid(_bdot(_rms(h, gple_ref[...]), wg_ref[...]))
    h = h + gate * _bdot(p_ref[...].reshape(rows, PLE_DIM), wp_ref[...])
    out_ref[...] = _rms(h, gfin_ref[...]).reshape(out_ref.shape)


def _final_prompt(part, hmid, yg, gwt, p_prompt, gple, wg, wp, gfin, y_prev):
    nb = 4
    rows = nb * CHUNK
    nbh = BATCH // nb
    c0 = part * PART_CHUNKS
    rb = lambda c, b: ((c0 + c) * nbh + b, 0)
    in_specs = [pl.BlockSpec((rows, D_MODEL), rb),
                pl.BlockSpec((TOP_K, rows, D_PACK), lambda c, b: (0, c * nbh + b, 0)),
                pl.BlockSpec((rows, TOP_K), rb),
                pl.BlockSpec((nb, CHUNK, PLE_DIM), lambda c, b: (b, c0 + c, 0)),
                _full((1, D_MODEL)), _full((D_MODEL, D_MODEL)), _full((PLE_DIM, D_MODEL)), _full((1, D_MODEL))]
    args = [hmid, yg, gwt, p_prompt, gple, wg, wp, gfin]
    aliases = {}
    if y_prev is not None:
        in_specs.append(pl.BlockSpec(memory_space=pl.ANY))
        args.append(y_prev)
        aliases = {len(args) - 1: 0}
    return pl.pallas_call(
        _final_kernel, grid=(PART_CHUNKS, nbh), in_specs=in_specs,
        out_specs=pl.BlockSpec((nb, CHUNK, D_MODEL), lambda c, b: (b, c0 + c, 0)),
        out_shape=jax.ShapeDtypeStruct((BATCH, SEQ, D_MODEL), F32), name="final_prompt",
        input_output_aliases=aliases,
        compiler_params=pltpu.CompilerParams(dimension_semantics=("arbitrary", "arbitrary"),
                                             vmem_limit_bytes=VMEM_LIMIT),
    )(*args)


def _final_sample(hmid, yg, gwt, p_sample, gple, wg, wp, gfin):
    blk = SAMPLE_ROW0 // DEC_BATCH
    return pl.pallas_call(
        _final_kernel, grid=(1,),
        in_specs=[pl.BlockSpec((DEC_BATCH, D_MODEL), lambda i: (blk, 0)),
                  pl.BlockSpec((TOP_K, DEC_BATCH, D_PACK), lambda i: (0, PART_CHUNKS * ROWS // DEC_BATCH, 0)),
                  pl.BlockSpec((DEC_BATCH, TOP_K), lambda i: (blk, 0)),
                  _full((DEC_BATCH, PLE_DIM)),
                  _full((1, D_MODEL)), _full((D_MODEL, D_MODEL)), _full((PLE_DIM, D_MODEL)), _full((1, D_MODEL))],
        out_specs=_full((DEC_BATCH, D_MODEL)),
        out_shape=jax.ShapeDtypeStruct((DEC_BATCH, D_MODEL), F32), name="final_sample",
        compiler_params=pltpu.CompilerParams(dimension_semantics=("arbitrary",), vmem_limit_bytes=VMEM_LIMIT),
    )(hmid, yg, gwt, p_sample, gple, wg, wp, gfin)


def _alibi_tables():
    slopes = 2.0 ** (-8.0 * (np.arange(N_HEADS, dtype=np.float64) + 1.0) / N_HEADS)
    i = np.arange(CHUNK)[:, None]
    j = np.arange(2 * CHUNK)[None, :]
    dist = i + CHUNK - j
    valid = (dist >= 0) & (dist <= WINDOW)
    tabs = []
    for has_prev in (False, True):
        ok = valid & ((j >= CHUNK) | has_prev)
        tabs.append(np.where(ok[None], -slopes[:, None, None] * dist[None], NEG))
    prompt = np.stack([np.stack([np.concatenate([t[h] for h in grp], axis=0) for grp in (NAT_HEADS, ROT_HEADS)])
                       for t in tabs]).astype(np.float32)
    wb = min(WINDOW, PAST_LEN)
    sample = (-slopes[:, None] * (wb - np.arange(wb))[None, :]).astype(np.float32)
    return prompt, sample


def kernel(x_prompt, x_sample, cache_k_win, cache_v_win, state_ssm_re, state_ssm_im, p_prompt, p_sample, norm_mix, w_in, sinks, ssm_lam_re, ssm_lam_im, ssm_log_step, ssm_b_re, ssm_b_im, ssm_c_re, ssm_c_im, ssm_d, w_glu, b_glu, norm_attn_out, norm_ssm_out, w_out, norm_ffn, w_router, b_router, w_up, b_up, w_down, b_down, norm_ple, w_ple_gate, w_ple_proj, norm_final):
    nst = SSM_G * SSM_P
    bias_np, sbias_np = _alibi_tables()
    bias = jnp.asarray(bias_np)
    sbias = jnp.asarray(sbias_np)

    lbr, lbi, bbr, bbi = _prep(ssm_lam_re[0], ssm_lam_im[0], ssm_log_step[0], ssm_b_re[0], ssm_b_im[0])
    bblk, lamr, lami, cblk = _s5_blocks(lbr, lbi, bbr, bbi, ssm_c_re[0], ssm_c_im[0])

    gmix = norm_mix[0].reshape(1, D_MODEL)
    win = w_in[0].astype(BF16)
    dskip = ssm_d[0].reshape(1, D_SSM)
    wglu = w_glu[0].astype(BF16)
    bglu = b_glu[0].reshape(1, D_SSM)
    ga = norm_attn_out[0].reshape(1, D_ATTN)
    gs = norm_ssm_out[0].reshape(1, D_SSM)
    wout = w_out[0].astype(BF16)
    sink = sinks[0]

    gffn = norm_ffn[0].reshape(1, D_MODEL)
    wrt = w_router[0].T
    br = b_router[0].reshape(N_EXPERTS, 1)
    hbuf, k_last, v_last, re_p, im_p, hnbuf, idxbuf, gwbuf, rankbuf, cnt_p = _mixer_prompt(
        x_prompt, sink, gmix, win, bias, bblk, lamr, lami, cblk, dskip, wglu, bglu, ga, gs, wout, gffn, wrt, br)

    ck = cache_k_win[0].reshape(DEC_BATCH, WINDOW, D_KV)
    cv = cache_v_win[0].reshape(DEC_BATCH, WINDOW, D_KV)
    hmid, k_new, v_new, re_s, im_s, hn, idx, gw, rank, counts = _mixer_sample(
        x_sample.reshape(DEC_BATCH, D_MODEL), gmix, win, ck, cv, sbias, sink.reshape(N_HEADS, 1), bblk, lamr, lami,
        cblk, dskip, wglu, bglu, ga, gs, wout, state_ssm_re[0].reshape(DEC_BATCH, nst),
        state_ssm_im[0].reshape(DEC_BATCH, nst), gffn, wrt, br, cnt_p, hbuf, hnbuf, idxbuf, gwbuf, rankbuf)

    offsets, tile_start, tile_row, tile_rows = _tile_plan(counts[:, 0])

    pos = _place(offsets, idx, rank)
    xs = _sc_dispatch(hn, pos)
    ys = _experts(tile_start, tile_row, tile_rows, xs, w_up[0], b_up[0], w_down[0], b_down[0])
    ygs = []
    for part in range(N_PARTS):
        r0 = part * PART_CHUNKS * ROWS
        r1 = T_PAD if part == N_PARTS - 1 else r0 + PART_CHUNKS * ROWS
        ygs.append(_sc_combine(ys, pos[:, r0:r1].reshape(TOP_K * (r1 - r0))).reshape(TOP_K, r1 - r0, D_PACK))

    gwt = gw.T
    gple = norm_ple[0].reshape(1, D_MODEL)
    wg = w_ple_gate[0].astype(BF16)
    wp = w_ple_proj[0].astype(BF16)
    gfin = norm_final.reshape(1, D_MODEL)
    y_prompt = None
    for part in range(N_PARTS):
        y_prompt = _final_prompt(part, hmid, ygs[part], gwt, p_prompt[0], gple, wg, wp, gfin, y_prompt)
    y_sample = _final_sample(hmid, ygs[-1], gwt, p_sample[0].reshape(DEC_BATCH, PLE_DIM), gple, wg, wp, gfin)

    k_win_s = jnp.concatenate([ck[:, 1:], k_new[:, None, :]], axis=1)
    v_win_s = jnp.concatenate([cv[:, 1:], v_new[:, None, :]], axis=1)
    kv5 = (1, BATCH, CHUNK, N_KV, HEAD_DIM)
    skv5 = (1, DEC_BATCH, WINDOW, N_KV, HEAD_DIM)
    return (y_prompt, y_sample.reshape(DEC_BATCH, 1, D_MODEL),
            k_last.reshape(kv5), v_last.reshape(kv5),
            re_p.reshape(1, BATCH, SSM_G, SSM_P), im_p.reshape(1, BATCH, SSM_G, SSM_P),
            k_win_s.reshape(skv5), v_win_s.reshape(skv5),
            re_s.reshape(1, DEC_BATCH, SSM_G, SSM_P), im_s.reshape(1, DEC_BATCH, SSM_G, SSM_P))
```

```python
import functools

import numpy as np
import jax
import jax.numpy as jnp
from jax import lax
from jax.experimental import pallas as pl
from jax.experimental.pallas import tpu as pltpu
from jax.experimental.pallas import tpu_sc as plsc

F32 = jnp.float32
BF16 = jnp.bfloat16

D_MODEL = 1024
BATCH = 8
SEQ = 2048
DEC_BATCH = 128
PAST_LEN = 16384
HEAD_DIM = 64
D_ATTN = 512
N_HEADS = 8
N_KV = 2
D_KV = N_KV * HEAD_DIM
WINDOW = 128
D_SSM = 512
SSM_H = 16
SSM_G = 32
SSM_P = 64
D_IN = D_ATTN + 2 * D_KV + D_SSM
N_EXPERTS = 32
TOP_K = 4
D_FF = 1024
SWIGLU_LIMIT = 7.0
SWIGLU_ALPHA = 1.702
PLE_DIM = 256
EPS = 1e-5
NEG = -1e30

CHUNK = 128
N_CHUNKS = SEQ // CHUNK
ROWS = BATCH * CHUNK
PITCH = CHUNK + 8
NAT_HEADS = (0, 2, 5, 7)
ROT_HEADS = (1, 3, 4, 6)
ATTN_UNROLL = 2
SUB_T = 64
SUB_ROWS = SUB_T * BATCH
N_SBLK = 4
SBLK = 512
T_REAL = BATCH * SEQ + DEC_BATCH
T_PAD = T_REAL + 128
SAMPLE_ROW0 = BATCH * SEQ
ROUTE_BLOCK = 512
TILE_SIZES = (1024, 512, 256)
TM = TILE_SIZES[0]
TM_UNIT = TILE_SIZES[-1]
P_ROWS = (T_PAD * TOP_K + N_EXPERTS * (TM_UNIT - 1)) // TM_UNIT * TM_UNIT
MAX_TILES = P_ROWS // TM + 2 * N_EXPERTS + 1
SC_ROWS = 40
SC_ROWS_COMBINE = 32
N_PARTS = 4
PART_CHUNKS = N_CHUNKS // N_PARTS
D_PACK = D_MODEL // 2
VMEM_LIMIT = 56 * 1024 * 1024


def _rms(x, g):
    return x * lax.rsqrt(jnp.mean(x * x, axis=-1, keepdims=True) + EPS) * g


def _sigmoid(x):
    return 1.0 / (1.0 + jnp.exp(-x))


def _gelu_tanh(x):
    c = np.float32(np.sqrt(2.0 / np.pi))
    return 0.5 * x * (1.0 + jnp.tanh(c * (x + 0.044715 * (x * x * x))))


def _bdot(a, b):
    return jnp.dot(a.astype(BF16), b, preferred_element_type=F32)


def _pack_pairs(x):
    n = x.shape[1] // 2
    lo = lax.bitcast_convert_type(x[:, :n].astype(BF16).astype(F32), jnp.int32)
    hi = lax.bitcast_convert_type(x[:, n:].astype(BF16).astype(F32), jnp.int32)
    return lax.shift_right_logical(lo, 16) | hi


def _unpack_pairs(w):
    lo = lax.bitcast_convert_type(lax.shift_left(w, 16), F32)
    hi = lax.bitcast_convert_type(w & jnp.int32(-65536), F32)
    return jnp.concatenate([lo, hi], axis=1)


def _full(shape):
    n = len(shape)
    return pl.BlockSpec(shape, lambda *_: (0,) * n)


def _prep_kernel(lr_ref, li_ref, ls_ref, br_ref, bi_ref, lbr_ref, lbi_ref, bbr_ref, bbi_ref):
    lr = lr_ref[...]
    li = li_ref[...]
    step = jnp.exp(ls_ref[...])
    zr = lr * step
    zi = li * step
    mag = jnp.exp(zr)
    lbr = mag * jnp.cos(zi)
    lbi = mag * jnp.sin(zi)
    lbr_ref[...] = lbr
    lbi_ref[...] = lbi
    nr = lbr - 1.0
    den = lr * lr + li * li
    cr = (nr * lr + lbi * li) / den
    ci = (lbi * lr - nr * li) / den
    br = br_ref[...]
    bi = bi_ref[...]
    bbr_ref[...] = cr * br - ci * bi
    bbi_ref[...] = cr * bi + ci * br


def _prep(lam_re, lam_im, log_step, b_re, b_im):
    g, p, h = SSM_G, SSM_P, SSM_H
    out = pl.pallas_call(
        _prep_kernel,
        out_shape=[jax.ShapeDtypeStruct((g, 1, p), F32), jax.ShapeDtypeStruct((g, 1, p), F32),
                   jax.ShapeDtypeStruct((g, h, p), F32), jax.ShapeDtypeStruct((g, h, p), F32)],
        name="s5_prep",
    )(lam_re.reshape(g, 1, p), lam_im.reshape(g, 1, p), log_step.reshape(g, 1, 1),
      jnp.transpose(b_re, (0, 2, 1)), jnp.transpose(b_im, (0, 2, 1)))
    return out


def _s5_blocks(lbr, lbi, bbr, bbi, c_re, c_im):
    eye = jnp.eye(8, dtype=F32)
    shp = (N_SBLK, 8, SSM_H, SSM_P)

    def in_map(b):
        return (b.reshape(shp)[:, :, :, None, :] * eye[None, :, None, :, None]).reshape(N_SBLK, 128, SBLK)

    def out_map(c):
        ct = jnp.transpose(c.reshape(shp), (0, 1, 3, 2))
        return (ct[:, :, :, None, :] * eye[None, :, None, :, None]).reshape(N_SBLK, SBLK, 128)

    bblk = jnp.concatenate([in_map(bbr), in_map(bbi)], axis=-1).astype(BF16)
    cblk = jnp.concatenate([out_map(c_re), -out_map(c_im)], axis=1).astype(BF16)
    return bblk, lbr.reshape(N_SBLK, 1, SBLK), lbi.reshape(N_SBLK, 1, SBLK), cblk


def _ssm_post(y_lin, u, dskip, wglu, bglu, gs):
    y = _gelu_tanh(y_lin + dskip * u)
    y = y * _sigmoid(_bdot(y, wglu) + bglu)
    return _rms(y, gs)


def _mixer_prompt_kernel(sinks_ref, x_ref, gmix_ref, win_ref, bias_ref, bblk_ref, lamr_ref, lami_ref,
                         cblk_ref, dskip_ref, wglu_ref, bglu_ref, ga_ref, gs_ref, wout_ref,
                         gffn_ref, wrt_ref, br_ref, tri_ref,
                         hmid_ref, klast_ref, vlast_ref, sre_ref, sim_ref, hn_ref, idx_ref, gw_ref, rank_ref, cnt_ref,
                         proj_s, u_s, kprev_s, kprevr_s, vprev_s, vprevr_s, attn_s, ssm_s, utb_s, bu_s, xs_s, ytb_s):
    c = pl.program_id(0)

    @pl.when(c == 0)
    def _():
        zkv = jnp.zeros(kprev_s.shape, BF16)
        kprev_s[...] = zkv
        kprevr_s[...] = zkv
        vprev_s[...] = zkv
        vprevr_s[...] = zkv
        sre_ref[...] = jnp.zeros(sre_ref.shape, F32)
        sim_ref[...] = jnp.zeros(sim_ref.shape, F32)
        cnt_ref[...] = jnp.zeros(cnt_ref.shape, jnp.int32)

    x = x_ref[...].reshape(ROWS, D_MODEL)
    proj = _bdot(_rms(x, gmix_ref[...]), win_ref[...])
    u0 = D_ATTN + 2 * D_KV
    proj_s[...] = proj[:, 0:u0]
    for jb in range(D_SSM // 128):
        for b in range(BATCH):
            u_s[jb, b * PITCH:b * PITCH + CHUNK, :] = proj[b * CHUNK:(b + 1) * CHUNK, u0 + 128 * jb:u0 + 128 * (jb + 1)]

    lo = lax.broadcasted_iota(jnp.int32, (CHUNK, 128), 1) < HEAD_DIM
    hi = jnp.logical_not(lo)
    table = jnp.minimum(c, 1)
    hrow = lax.broadcasted_iota(jnp.int32, (4 * CHUNK, 1), 0) // CHUNK

    def sink_col(heads):
        col = jnp.full((4 * CHUNK, 1), sinks_ref[heads[3]], F32)
        for n in (2, 1, 0):
            col = jnp.where(hrow == n, sinks_ref[heads[n]], col)
        return col

    sink_nat = sink_col(NAT_HEADS)
    sink_rot = sink_col(ROT_HEADS)

    def attend(q, k, v, bias, sink):
        s = lax.dot_general(q, k, (((1,), (1,)), ((), ())), preferred_element_type=F32) + bias
        m = jnp.maximum(jnp.max(s, axis=-1, keepdims=True), sink)
        p = jnp.exp(s - m)
        den = jnp.sum(p, axis=-1, keepdims=True) + jnp.exp(sink - m)
        return jnp.dot(p.astype(BF16), v, preferred_element_type=F32) / den

    def attn_body(b, carry):
        r0 = pl.multiple_of(b * CHUNK, CHUNK)
        rows = pl.ds(r0, CHUNK)
        kb = proj_s[rows, D_ATTN:D_ATTN + D_KV]
        vb = proj_s[rows, D_ATTN + D_KV:D_ATTN + 2 * D_KV]
        kb16 = kb.astype(BF16)
        vb16 = vb.astype(BF16)
        kbr16 = pltpu.roll(kb, HEAD_DIM, 1).astype(BF16)
        vbr16 = pltpu.roll(vb, HEAD_DIM, 1).astype(BF16)
        k_nat = jnp.concatenate([kprev_s[b], kb16], axis=0)
        k_rot = jnp.concatenate([kprevr_s[b], kbr16], axis=0)
        v_nat = jnp.concatenate([vprev_s[b], vb16], axis=0)
        v_rot = jnp.concatenate([vprevr_s[b], vbr16], axis=0)
        q2 = [proj_s[rows, 128 * jq:128 * (jq + 1)] * (HEAD_DIM ** -0.5) for jq in range(N_HEADS // 2)]
        q_nat = jnp.concatenate([jnp.where(lo if h % 2 == 0 else hi, q2[h // 2], 0.0) for h in NAT_HEADS],
                                axis=0).astype(BF16)
        q_rot = jnp.concatenate([jnp.where(lo if h % 2 == 0 else hi, q2[h // 2], 0.0) for h in ROT_HEADS],
                                axis=0).astype(BF16)
        o_nat = attend(q_nat, k_nat, v_nat, bias_ref[table, 0], sink_nat)
        o_rot = attend(q_rot, k_rot, v_rot, bias_ref[table, 1], sink_rot)
        for jq in range(N_HEADS // 2):
            blk = slice(CHUNK * jq, CHUNK * (jq + 1))
            even, odd = (o_nat, o_rot) if jq < 2 else (o_rot, o_nat)
            attn_s[rows, 128 * jq:128 * (jq + 1)] = jnp.where(lo, even[blk], odd[blk])
        kprev_s[b] = kb16
        kprevr_s[b] = kbr16
        vprev_s[b] = vb16
        vprevr_s[b] = vbr16
        return carry

    def attn_group(i, carry):
        for n in range(ATTN_UNROLL):
            attn_body(ATTN_UNROLL * i + n, carry)
        return carry

    lax.fori_loop(0, BATCH // ATTN_UNROLL, attn_group, 0)

    for sc in range(CHUNK // SUB_T):
        t0 = sc * SUB_T
        for i in range(SUB_T):
            for jb in range(D_SSM // 128):
                utb_s[i * BATCH:(i + 1) * BATCH, 128 * jb:128 * (jb + 1)] = (
                    u_s[jb, pl.ds(t0 + i, BATCH, stride=PITCH), :])
        u_tb = utb_s[...]
        for j in range(N_SBLK):
            bu_s[...] = _bdot(u_tb[:, 128 * j:128 * (j + 1)], bblk_ref[j])
            lr = jnp.broadcast_to(lamr_ref[j], (BATCH, SBLK))
            li = jnp.broadcast_to(lami_ref[j], (BATCH, SBLK))
            cols = slice(SBLK * j, SBLK * (j + 1))

            def step(i, carry):
                sr, si = carry
                r = pl.ds(pl.multiple_of(i * BATCH, BATCH), BATCH)
                nr = lr * sr - li * si + bu_s[r, 0:SBLK]
                ni = lr * si + li * sr + bu_s[r, SBLK:2 * SBLK]
                xs_s[r, 0:SBLK] = nr
                xs_s[r, SBLK:2 * SBLK] = ni
                return nr, ni

            sr, si = lax.fori_loop(0, SUB_T, step, (sre_ref[:, cols], sim_ref[:, cols]), unroll=True)
            sre_ref[:, cols] = sr
            sim_ref[:, cols] = si
            ytb_s[:, 128 * j:128 * (j + 1)] = _bdot(xs_s[...], cblk_ref[j])
        yn = _ssm_post(ytb_s[...], u_tb, dskip_ref[...], wglu_ref[...], bglu_ref[...], gs_ref[...])
        for i in range(SUB_T):
            for jb in range(D_SSM // 128):
                ssm_s[jb, pl.ds(t0 + i, BATCH, stride=PITCH), :] = (
                    yn[i * BATCH:(i + 1) * BATCH, 128 * jb:128 * (jb + 1)])

    an = _rms(attn_s[...], ga_ref[...])
    sn = jnp.concatenate(
        [jnp.concatenate([ssm_s[jb, b * PITCH:b * PITCH + CHUNK, :] for b in range(BATCH)], axis=0)
         for jb in range(D_SSM // 128)], axis=1)
    h = x + _bdot(jnp.concatenate([an.astype(BF16), sn.astype(BF16)], axis=1), wout_ref[...])
    hmid_ref[...] = h
    _route(h, gffn_ref, wrt_ref, br_ref, tri_ref, cnt_ref, hn_ref, idx_ref, gw_ref, rank_ref)

    @pl.when(c == N_CHUNKS - 1)
    def _():
        klast_ref[...] = proj_s[:, D_ATTN:D_ATTN + D_KV].reshape(BATCH, CHUNK, D_KV)
        vlast_ref[...] = proj_s[:, D_ATTN + D_KV:D_ATTN + 2 * D_KV].reshape(BATCH, CHUNK, D_KV)


def _mixer_prompt(x_prompt, sinks, gmix, win, bias, bblk, lamr, lami, cblk, dskip, wglu, bglu, ga, gs, wout,
                  gffn, wrt, br):
    smem = pl.BlockSpec(memory_space=pltpu.SMEM)
    in_specs = [
        smem,
        pl.BlockSpec((BATCH, CHUNK, D_MODEL), lambda c: (0, c, 0)),
        _full((1, D_MODEL)), _full((D_MODEL, D_IN)), _full((2, 2, 4 * CHUNK, 2 * CHUNK)),
        _full((N_SBLK, 128, 2 * SBLK)), _full((N_SBLK, 1, SBLK)), _full((N_SBLK, 1, SBLK)),
        _full((N_SBLK, 2 * SBLK, 128)), _full((1, D_SSM)), _full((D_SSM, D_SSM)), _full((1, D_SSM)),
        _full((1, D_ATTN)), _full((1, D_SSM)), _full((D_MODEL, D_MODEL)),
        _full((1, D_MODEL)), _full((N_EXPERTS, D_MODEL)), _full((N_EXPERTS, 1)), _full((ROUTE_BLOCK, ROUTE_BLOCK)),
    ]
    route_specs, route_shapes = _route_specs(ROWS, lambda c: c)
    out_shape = [
        jax.ShapeDtypeStruct((T_PAD, D_MODEL), F32),
        jax.ShapeDtypeStruct((BATCH, CHUNK, D_KV), F32),
        jax.ShapeDtypeStruct((BATCH, CHUNK, D_KV), F32),
        jax.ShapeDtypeStruct((BATCH, SSM_G * SSM_P), F32),
        jax.ShapeDtypeStruct((BATCH, SSM_G * SSM_P), F32),
    ] + route_shapes
    out_specs = [
        pl.BlockSpec((ROWS, D_MODEL), lambda c: (c, 0)),
        _full((BATCH, CHUNK, D_KV)), _full((BATCH, CHUNK, D_KV)),
        _full((BATCH, SSM_G * SSM_P)), _full((BATCH, SSM_G * SSM_P)),
    ] + route_specs
    kv_scr = pltpu.VMEM((BATCH, CHUNK, D_KV), BF16)
    scratch = [
        pltpu.VMEM((ROWS, D_ATTN + 2 * D_KV), F32), pltpu.VMEM((D_SSM // 128, BATCH * PITCH, 128), F32),
        kv_scr, kv_scr, kv_scr, kv_scr,
        pltpu.VMEM((ROWS, D_ATTN), F32), pltpu.VMEM((D_SSM // 128, BATCH * PITCH, 128), F32),
        pltpu.VMEM((SUB_ROWS, D_SSM), F32), pltpu.VMEM((SUB_ROWS, 2 * SBLK), F32),
        pltpu.VMEM((SUB_ROWS, 2 * SBLK), F32), pltpu.VMEM((SUB_ROWS, D_SSM), F32),
    ]
    return pl.pallas_call(
        _mixer_prompt_kernel, grid=(N_CHUNKS,), in_specs=in_specs, out_specs=out_specs, out_shape=out_shape,
        scratch_shapes=scratch, name="mixer_prompt",
        compiler_params=pltpu.CompilerParams(dimension_semantics=("arbitrary",), vmem_limit_bytes=VMEM_LIMIT),
    )(sinks, x_prompt, gmix, win, bias, bblk, lamr, lami, cblk, dskip, wglu, bglu, ga, gs, wout,
      gffn, wrt, br, _tri(ROUTE_BLOCK))


SGRP = 16
SPITCH = DEC_BATCH + 8
N_SGRP = DEC_BATCH // SGRP


def _mixer_sample_kernel(x_ref, gmix_ref, win_ref, ck_ref, cv_ref, sbias_ref, sinkc_ref, bblk_ref, lamr_ref,
                         lami_ref, cblk_ref, dskip_ref, wglu_ref, bglu_ref, ga_ref, gs_ref, wout_ref,
                         x0r_ref, x0i_ref, gffn_ref, wrt_ref, br_ref, tri_ref, cnt_in_ref,
                         hbuf_ref, hnbuf_ref, idxbuf_ref, gwbuf_ref, rankbuf_ref,
                         hmid_ref, knew_ref, vnew_ref, sre_ref, sim_ref, hn_ref, idx_ref, gw_ref, rank_ref, cnt_ref,
                         proj_s, qall_s, oall_s, kn8_s, vn8_s):
    del hbuf_ref, hnbuf_ref, idxbuf_ref, gwbuf_ref, rankbuf_ref
    g = pl.program_id(0)
    lo = lax.broadcasted_iota(jnp.int32, (DEC_BATCH, 128), 1) < HEAD_DIM

    @pl.when(g == 0)
    def _():
        proj = _bdot(_rms(x_ref[...], gmix_ref[...]), win_ref[...])
        proj_s[...] = proj
        for h in range(N_HEADS):
            jq, half, kv = h // 2, h % 2, h // 4
            q2 = proj[:, 128 * jq:128 * (jq + 1)] * (HEAD_DIM ** -0.5)
            if half != kv:
                q2 = pltpu.roll(q2, HEAD_DIM, 1)
            hr = slice(h * SPITCH, h * SPITCH + DEC_BATCH)
            qall_s[hr, :] = jnp.where(lo if kv == 0 else jnp.logical_not(lo), q2, 0.0)
            kn8_s[hr, :] = proj[:, D_ATTN:D_ATTN + D_KV]
            vn8_s[hr, :] = proj[:, D_ATTN + D_KV:D_ATTN + 2 * D_KV]

    def head_rows(ref):
        return jnp.stack([ref[pl.ds(g * SGRP + ii, N_HEADS, stride=SPITCH), :] for ii in range(SGRP)], axis=0)

    qe = head_rows(qall_s)
    knew = head_rows(kn8_s)
    vnew = head_rows(vn8_s)
    s = jnp.einsum('bhd,bkd->bhk', qe.astype(BF16), ck_ref[...].astype(BF16),
                   preferred_element_type=F32) + sbias_ref[...][None]
    s_new = jnp.sum(qe * knew, axis=-1, keepdims=True)
    sink = sinkc_ref[...][None]
    m = jnp.maximum(jnp.maximum(jnp.max(s, axis=-1, keepdims=True), s_new), sink)
    p = jnp.exp(s - m)
    p_new = jnp.exp(s_new - m)
    den = jnp.sum(p, axis=-1, keepdims=True) + p_new + jnp.exp(sink - m)
    o = (jnp.einsum('bhk,bkd->bhd', p.astype(BF16), cv_ref[...].astype(BF16),
                    preferred_element_type=F32) + p_new * vnew) / den
    for ii in range(SGRP):
        oall_s[pl.ds(g * SGRP + ii, N_HEADS, stride=SPITCH), :] = o[ii]

    @pl.when(g == N_SGRP - 1)
    def _():
        proj = proj_s[...]
        blocks = []
        for jq in range(N_HEADS // 2):
            oa = oall_s[(2 * jq) * SPITCH:(2 * jq) * SPITCH + DEC_BATCH, :]
            ob = oall_s[(2 * jq + 1) * SPITCH:(2 * jq + 1) * SPITCH + DEC_BATCH, :]
            if jq // 2 == 0:
                blocks.append(jnp.where(lo, oa, pltpu.roll(ob, HEAD_DIM, 1)))
            else:
                blocks.append(jnp.where(lo, pltpu.roll(oa, HEAD_DIM, 1), ob))
        attn = jnp.concatenate(blocks, axis=1)
        u = proj[:, D_ATTN + 2 * D_KV:]
        ys = []
        for j in range(N_SBLK):
            bu = _bdot(u[:, 128 * j:128 * (j + 1)], bblk_ref[j])
            lr = lamr_ref[j]
            li = lami_ref[j]
            cols = slice(SBLK * j, SBLK * (j + 1))
            sr = x0r_ref[:, cols]
            si = x0i_ref[:, cols]
            nr = lr * sr - li * si + bu[:, 0:SBLK]
            ni = lr * si + li * sr + bu[:, SBLK:2 * SBLK]
            sre_ref[:, cols] = nr
            sim_ref[:, cols] = ni
            ys.append(_bdot(jnp.concatenate([nr, ni], axis=1), cblk_ref[j]))
        sn = _ssm_post(jnp.concatenate(ys, axis=1), u, dskip_ref[...], wglu_ref[...], bglu_ref[...], gs_ref[...])
        an = _rms(attn, ga_ref[...])
        h = x_ref[...] + _bdot(jnp.concatenate([an.astype(BF16), sn.astype(BF16)], axis=1), wout_ref[...])
        h = jnp.concatenate([h, jnp.zeros((DEC_BATCH, D_MODEL), F32)], axis=0)
        hmid_ref[...] = h
        cnt_ref[...] = cnt_in_ref[...]
        _route(h, gffn_ref, wrt_ref, br_ref, tri_ref, cnt_ref, hn_ref, idx_ref, gw_ref, rank_ref)
        knew_ref[...] = proj[:, D_ATTN:D_ATTN + D_KV]
        vnew_ref[...] = proj[:, D_ATTN + D_KV:D_ATTN + 2 * D_KV]


def _mixer_sample(x_s, gmix, win, ck, cv, sbias, sinkc, bblk, lamr, lami, cblk, dskip, wglu, bglu, ga, gs, wout,
                  x0r, x0i, gffn, wrt, br, cnt_in, hbuf, hnbuf, idxbuf, gwbuf, rankbuf):
    nst = SSM_G * SSM_P
    in_specs = [
        _full((DEC_BATCH, D_MODEL)), _full((1, D_MODEL)), _full((D_MODEL, D_IN)),
        pl.BlockSpec((SGRP, WINDOW, D_KV), lambda g: (g, 0, 0)),
        pl.BlockSpec((SGRP, WINDOW, D_KV), lambda g: (g, 0, 0)),
        _full((N_HEADS, WINDOW)), _full((N_HEADS, 1)),
        _full((N_SBLK, 128, 2 * SBLK)), _full((N_SBLK, 1, SBLK)), _full((N_SBLK, 1, SBLK)),
        _full((N_SBLK, 2 * SBLK, 128)), _full((1, D_SSM)), _full((D_SSM, D_SSM)), _full((1, D_SSM)),
        _full((1, D_ATTN)), _full((1, D_SSM)), _full((D_MODEL, D_MODEL)),
        _full((DEC_BATCH, nst)), _full((DEC_BATCH, nst)),
        _full((1, D_MODEL)), _full((N_EXPERTS, D_MODEL)), _full((N_EXPERTS, 1)),
        _full((2 * DEC_BATCH, 2 * DEC_BATCH)), _full((N_EXPERTS, 1)),
    ] + [pl.BlockSpec(memory_space=pl.ANY)] * 5
    tail_block = SAMPLE_ROW0 // (2 * DEC_BATCH)
    route_specs, route_shapes = _route_specs(2 * DEC_BATCH, lambda g: tail_block)
    out_shape = [
        jax.ShapeDtypeStruct((T_PAD, D_MODEL), F32),
        jax.ShapeDtypeStruct((DEC_BATCH, D_KV), F32), jax.ShapeDtypeStruct((DEC_BATCH, D_KV), F32),
        jax.ShapeDtypeStruct((DEC_BATCH, nst), F32), jax.ShapeDtypeStruct((DEC_BATCH, nst), F32),
    ] + route_shapes
    out_specs = [
        pl.BlockSpec((2 * DEC_BATCH, D_MODEL), lambda g: (tail_block, 0)),
        _full((DEC_BATCH, D_KV)), _full((DEC_BATCH, D_KV)),
        _full((DEC_BATCH, nst)), _full((DEC_BATCH, nst)),
    ] + route_specs
    head_rows = pltpu.VMEM((N_HEADS * SPITCH, 128), F32)
    scratch = [pltpu.VMEM((DEC_BATCH, D_IN), F32), head_rows, head_rows, head_rows, head_rows]
    return pl.pallas_call(
        _mixer_sample_kernel, grid=(N_SGRP,), in_specs=in_specs, out_specs=out_specs, out_shape=out_shape,
        scratch_shapes=scratch, input_output_aliases={24: 0, 25: 5, 26: 6, 27: 7, 28: 8}, name="mixer_sample",
        compiler_params=pltpu.CompilerParams(dimension_semantics=("arbitrary",), vmem_limit_bytes=VMEM_LIMIT),
    )(x_s, gmix, win, ck, cv, sbias, sinkc, bblk, lamr, lami, cblk, dskip, wglu, bglu, ga, gs, wout, x0r, x0i,
      gffn, wrt, br, _tri(2 * DEC_BATCH), cnt_in, hbuf, hnbuf, idxbuf, gwbuf, rankbuf)


def _route(h, g_ref, wrt_ref, br_ref, tri_ref, cnt_ref, hn_ref, idx_ref, gw_ref, rank_ref):
    hn = _rms(h, g_ref[...])
    hn_ref[...] = _pack_pairs(hn)
    hn_hi = hn.astype(BF16)
    hn_lo = (hn - hn_hi.astype(F32)).astype(BF16)
    w = wrt_ref[...]
    w_hi = w.astype(BF16)
    w_lo = (w - w_hi.astype(F32)).astype(BF16)
    nt = (((1,), (1,)), ((), ()))
    both = lax.dot_general(jnp.concatenate([w_hi, w_lo], axis=0), hn_hi, nt, preferred_element_type=F32)
    logits = (both[:N_EXPERTS] + both[N_EXPERTS:]
              + lax.dot_general(w_hi, hn_lo, nt, preferred_element_type=F32)) + br_ref[...]
    eidx = lax.broadcasted_iota(jnp.int32, logits.shape, 0)
    vals, onehots = [], []
    l = logits
    for k in range(TOP_K):
        m = jnp.max(l, axis=0, keepdims=True)
        ik = jnp.min(jnp.where(l == m, eidx, N_EXPERTS), axis=0, keepdims=True)
        oh = eidx == ik
        idx_ref[k:k + 1, :] = ik
        vals.append(m)
        onehots.append(oh)
        l = jnp.where(oh, -jnp.inf, l)
    exps = [jnp.exp(v - vals[0]) for v in vals]
    den = exps[0] + exps[1] + exps[2] + exps[3]
    for k in range(TOP_K):
        gw_ref[k:k + 1, :] = exps[k] / den
    member = jnp.zeros(logits.shape, F32)
    for oh in onehots:
        member = member + jnp.where(oh, 1.0, 0.0)
    wblk = tri_ref.shape[0]
    base = cnt_ref[...].astype(F32)
    befores = []
    for cb in range(h.shape[0] // wblk):
        mblk = member[:, cb * wblk:(cb + 1) * wblk]
        befores.append(jnp.dot(mblk.astype(BF16), tri_ref[...], preferred_element_type=F32) + base)
        base = base + jnp.sum(mblk, axis=1, keepdims=True)
    before = jnp.concatenate(befores, axis=1)
    for k in range(TOP_K):
        rank_ref[k:k + 1, :] = jnp.sum(jnp.where(onehots[k], before, 0.0), axis=0, keepdims=True).astype(jnp.int32)
    cnt_ref[...] = base.astype(jnp.int32)


def _route_specs(rows, block):
    specs = [pl.BlockSpec((rows, D_PACK), lambda i: (block(i), 0)), pl.BlockSpec((TOP_K, rows), lambda i: (0, block(i))),
             pl.BlockSpec((TOP_K, rows), lambda i: (0, block(i))), pl.BlockSpec((TOP_K, rows), lambda i: (0, block(i))),
             _full((N_EXPERTS, 1))]
    shapes = [jax.ShapeDtypeStruct((T_PAD, D_PACK), jnp.int32), jax.ShapeDtypeStruct((TOP_K, T_PAD), jnp.int32),
              jax.ShapeDtypeStruct((TOP_K, T_PAD), F32), jax.ShapeDtypeStruct((TOP_K, T_PAD), jnp.int32),
              jax.ShapeDtypeStruct((N_EXPERTS, 1), jnp.int32)]
    return specs, shapes


def _tri(n):
    return jnp.asarray(np.triu(np.ones((n, n), np.float32), 1), BF16)


def _place_kernel(off_ref, idx_ref, rank_ref, pos_ref):
    idx = idx_ref[...]
    pos = rank_ref[...]
    for e in range(N_EXPERTS):
        pos = pos + jnp.where(idx == e, off_ref[e], 0)
    pos_ref[...] = pos


def _place(offsets, idx, rank):
    return pl.pallas_call(
        _place_kernel,
        in_specs=[pl.BlockSpec(memory_space=pltpu.SMEM), pl.BlockSpec(memory_space=pltpu.VMEM),
                  pl.BlockSpec(memory_space=pltpu.VMEM)],
        out_specs=pl.BlockSpec(memory_space=pltpu.VMEM),
        out_shape=jax.ShapeDtypeStruct((TOP_K, T_PAD), jnp.int32), name="place",
    )(offsets, idx, rank)


def _sc_mesh():
    return plsc.VectorSubcoreMesh(core_axis_name="core", subcore_axis_name="subcore")


def _sc_dispatch(rows, pos):
    n, d = rows.shape
    nblk = n // SC_ROWS
    pos_w = pos.reshape(TOP_K, nblk, SC_ROWS).transpose(1, 0, 2)

    @functools.partial(pl.kernel, out_type=jax.ShapeDtypeStruct((P_ROWS, d), rows.dtype), mesh=_sc_mesh(),
                       scratch_types=[], name="dispatch")
    def run(x_hbm, i_hbm, o_hbm):
        def body(x_vmem, i_vmem):
            for k in range(TOP_K):
                pltpu.sync_copy(x_vmem, o_hbm.at[i_vmem.at[0, k]])

        pltpu.emit_pipeline(
            body, grid=(nblk,),
            in_specs=[pl.BlockSpec((SC_ROWS, d), lambda i: (i, 0)),
                      pl.BlockSpec((1, TOP_K, SC_ROWS), lambda i: (i, 0, 0))],
            out_specs=[], core_axis_name=("core", "subcore"), dimension_semantics=(pltpu.PARALLEL,),
        )(x_hbm, i_hbm)

    return run(rows, pos_w)


def _sc_combine(rows, flat_pos):
    _, d = rows.shape
    n = flat_pos.shape[0]
    w = SC_ROWS_COMBINE
    nblk = n // w
    pos_w = flat_pos.reshape(nblk, 1, w)

    @functools.partial(pl.kernel, out_type=jax.ShapeDtypeStruct((n, d), rows.dtype), mesh=_sc_mesh(),
                       scratch_types=[], name="combine")
    def run(x_hbm, i_hbm, o_hbm):
        def body(i_vmem, o_vmem):
            pltpu.sync_copy(x_hbm.at[i_vmem.at[0, 0]], o_vmem)

        pltpu.emit_pipeline(
            body, grid=(nblk,),
            in_specs=[pl.BlockSpec((1, 1, w), lambda i: (i, 0, 0))],
            out_specs=[pl.BlockSpec((w, d), lambda i: (i, 0))],
            core_axis_name=("core", "subcore"), dimension_semantics=(pltpu.PARALLEL,),
        )(i_hbm, o_hbm)

    return run(rows, pos_w)


def _experts_kernel(ts_ref, trow_ref, tsz_ref, xs_hbm, w1_hbm, b1_ref, w2_hbm, b2_ref, ys_hbm,
                    w1_st, w2_st, w1_s, w2_s, xbuf, ybuf, wsem, xsem, ysem):
    e = pl.program_id(0)
    n_valid = ts_ref[N_EXPERTS]

    def w_copies(ex, slot):
        return (pltpu.make_async_copy(w1_hbm.at[ex], w1_st.at[slot], wsem.at[0, slot]),
                pltpu.make_async_copy(w2_hbm.at[ex], w2_st.at[slot], wsem.at[1, slot]))

    def x_copy(t, slot, rows):
        src = xs_hbm.at[pl.ds(pl.multiple_of(trow_ref[t], TM_UNIT), rows)]
        return pltpu.make_async_copy(src, xbuf.at[slot, pl.ds(0, rows)], xsem.at[slot])

    def y_copy(t, slot, rows):
        dst = ys_hbm.at[pl.ds(pl.multiple_of(trow_ref[t], TM_UNIT), rows)]
        return pltpu.make_async_copy(ybuf.at[slot, pl.ds(0, rows)], dst, ysem.at[slot])

    def by_size(t, fn):
        for rows in TILE_SIZES:
            @pl.when(tsz_ref[t] == rows)
            def _():
                fn(rows)

    @pl.when(e == 0)
    def _():
        for c in w_copies(0, 0):
            c.start()
        by_size(0, lambda rows: x_copy(0, 0, rows).start())

    @pl.when(e + 1 < N_EXPERTS)
    def _():
        for c in w_copies(e + 1, (e + 1) % 2):
            c.start()

    wslot = e % 2
    for c in w_copies(e, wslot):
        c.wait()
    for r in range(4):
        rs = slice(256 * r, 256 * (r + 1))
        w1_s[rs, :] = w1_st[wslot, rs, :].astype(BF16)
        w2_s[rs, :] = w2_st[wslot, rs, :].astype(BF16)

    def tile(t, carry):
        slot = t % 2

        @pl.when(t + 1 < n_valid)
        def _():
            by_size(t + 1, lambda rows: x_copy(t + 1, 1 - slot, rows).start())

        by_size(t, lambda rows: x_copy(t, slot, rows).wait())

        @pl.when(t >= 2)
        def _():
            by_size(t - 2, lambda rows: y_copy(t - 2, slot, rows).wait())

        def compute(rows):
            hdn = _bdot(_unpack_pairs(xbuf[slot, 0:rows]), w1_s[...]) + b1_ref[0]
            gt = jnp.minimum(hdn[:, :D_FF], SWIGLU_LIMIT)
            up = jnp.clip(hdn[:, D_FF:], -SWIGLU_LIMIT, SWIGLU_LIMIT)
            act = (up + 1.0) * gt * _sigmoid(SWIGLU_ALPHA * gt)
            ybuf[slot, 0:rows] = _pack_pairs(_bdot(act, w2_s[...]) + b2_ref[0])
            y_copy(t, slot, rows).start()

        by_size(t, compute)
        return carry

    lax.fori_loop(ts_ref[e], ts_ref[e + 1], tile, 0)

    @pl.when(e == N_EXPERTS - 1)
    def _():
        @pl.when(n_valid >= 2)
        def _():
            by_size(n_valid - 2, lambda rows: y_copy(n_valid - 2, n_valid % 2, rows).wait())

        by_size(n_valid - 1, lambda rows: y_copy(n_valid - 1, (n_valid - 1) % 2, rows).wait())


def _tile_plan(counts):
    units = (counts + (TM_UNIT - 1)) // TM_UNIT
    unit_end = jnp.cumsum(units)
    offsets = ((unit_end - units) * TM_UNIT).astype(jnp.int32)
    per_full = TM // TM_UNIT
    n_full = units // per_full
    rest = units % per_full
    has_mid = (rest >= 2).astype(jnp.int32)
    tiles_per = n_full + has_mid + rest % 2
    tile_end = jnp.cumsum(tiles_per)
    tile_start = jnp.concatenate([jnp.zeros((1,), jnp.int32), tile_end.astype(jnp.int32)])
    t = jnp.arange(MAX_TILES, dtype=jnp.int32)[:, None]
    done = tile_end[None, :] <= t
    mine = jnp.sum(done.astype(jnp.int32), axis=1, keepdims=True) == jnp.arange(N_EXPERTS, dtype=jnp.int32)[None, :]
    pick = lambda v: jnp.sum(jnp.where(mine, v[None, :], 0), axis=1)
    j = t[:, 0] - jnp.max(jnp.where(done, tile_end[None, :], 0), axis=1)
    nf, mid = pick(n_full), pick(has_mid)
    rows = jnp.where(j < nf, TM, jnp.where((j == nf) & (mid == 1), TILE_SIZES[1], TILE_SIZES[2]))
    row0 = pick(offsets) + jnp.minimum(j, nf) * TM + jnp.where(j > nf, TILE_SIZES[1], 0)
    valid = t[:, 0] < tile_end[-1]
    tile_row = jnp.where(valid, row0, 0).astype(jnp.int32)
    tile_rows = jnp.where(valid, rows, 0).astype(jnp.int32)
    return offsets, tile_start, tile_row, tile_rows


def _experts(tile_start, tile_row, tile_rows, xs, w_up, b_up, w_down, b_down):
    wsel = lambda e, *_: (e, 0, 0)
    hbm = pl.BlockSpec(memory_space=pl.ANY)
    grid_spec = pltpu.PrefetchScalarGridSpec(
        num_scalar_prefetch=3, grid=(N_EXPERTS,),
        in_specs=[hbm, hbm, pl.BlockSpec((1, 1, 2 * D_FF), wsel), hbm, pl.BlockSpec((1, 1, D_MODEL), wsel)],
        out_specs=hbm,
        scratch_shapes=[pltpu.VMEM((2, D_MODEL, 2 * D_FF), F32), pltpu.VMEM((2, D_FF, D_MODEL), F32),
                        pltpu.VMEM((D_MODEL, 2 * D_FF), BF16), pltpu.VMEM((D_FF, D_MODEL), BF16),
                        pltpu.VMEM((2, TM, D_PACK), jnp.int32), pltpu.VMEM((2, TM, D_PACK), jnp.int32),
                        pltpu.SemaphoreType.DMA((2, 2)), pltpu.SemaphoreType.DMA((2,)),
                        pltpu.SemaphoreType.DMA((2,))],
    )
    return pl.pallas_call(
        _experts_kernel, grid_spec=grid_spec, out_shape=jax.ShapeDtypeStruct((P_ROWS, D_PACK), jnp.int32),
        name="experts",
        compiler_params=pltpu.CompilerParams(dimension_semantics=("arbitrary",), vmem_limit_bytes=VMEM_LIMIT),
    )(tile_start, tile_row, tile_rows, xs, w_up, b_up.reshape(N_EXPERTS, 1, 2 * D_FF), w_down,
      b_down.reshape(N_EXPERTS, 1, D_MODEL))


def _final_kernel(h_ref, yg_ref, gw_ref, p_ref, gple_ref, wg_ref, wp_ref, gfin_ref, *rest):
    out_ref = rest[-1]
    rows = h_ref.shape[0]
    gw = gw_ref[...]
    h = h_ref[...]
    for k in range(TOP_K):
        h = h + gw[:, k:k + 1] * _unpack_pairs(yg_ref[k])
    gate = _sigmoid(_bdot(_rms(h, gple_ref[...]), wg_ref[...]))
    h = h + gate * _bdot(p_ref[...].reshape(rows, PLE_DIM), wp_ref[...])
    out_ref[...] = _rms(h, gfin_ref[...]).reshape(out_ref.shape)


def _final_prompt(part, hmid, yg, gwt, p_prompt, gple, wg, wp, gfin, y_prev):
    nb = 4
    rows = nb * CHUNK
    nbh = BATCH // nb
    c0 = part * PART_CHUNKS
    rb = lambda c, b: ((c0 + c) * nbh + b, 0)
    in_specs = [pl.BlockSpec((rows, D_MODEL), rb),
                pl.BlockSpec((TOP_K, rows, D_PACK), lambda c, b: (0, c * nbh + b, 0)),
                pl.BlockSpec((rows, TOP_K), rb),
                pl.BlockSpec((nb, CHUNK, PLE_DIM), lambda c, b: (b, c0 + c, 0)),
                _full((1, D_MODEL)), _full((D_MODEL, D_MODEL)), _full((PLE_DIM, D_MODEL)), _full((1, D_MODEL))]
    args = [hmid, yg, gwt, p_prompt, gple, wg, wp, gfin]
    aliases = {}
    if y_prev is not None:
        in_specs.append(pl.BlockSpec(memory_space=pl.ANY))
        args.append(y_prev)
        aliases = {len(args) - 1: 0}
    return pl.pallas_call(
        _final_kernel, grid=(PART_CHUNKS, nbh), in_specs=in_specs,
        out_specs=pl.BlockSpec((nb, CHUNK, D_MODEL), lambda c, b: (b, c0 + c, 0)),
        out_shape=jax.ShapeDtypeStruct((BATCH, SEQ, D_MODEL), F32), name="final_prompt",
        input_output_aliases=aliases,
        compiler_params=pltpu.CompilerParams(dimension_semantics=("arbitrary", "arbitrary"),
                                             vmem_limit_bytes=VMEM_LIMIT),
    )(*args)


def _final_sample(hmid, yg, gwt, p_sample, gple, wg, wp, gfin):
    blk = SAMPLE_ROW0 // DEC_BATCH
    return pl.pallas_call(
        _final_kernel, grid=(1,),
        in_specs=[pl.BlockSpec((DEC_BATCH, D_MODEL), lambda i: (blk, 0)),
                  pl.BlockSpec((TOP_K, DEC_BATCH, D_PACK), lambda i: (0, PART_CHUNKS * ROWS // DEC_BATCH, 0)),
                  pl.BlockSpec((DEC_BATCH, TOP_K), lambda i: (blk, 0)),
                  _full((DEC_BATCH, PLE_DIM)),
                  _full((1, D_MODEL)), _full((D_MODEL, D_MODEL)), _full((PLE_DIM, D_MODEL)), _full((1, D_MODEL))],
        out_specs=_full((DEC_BATCH, D_MODEL)),
        out_shape=jax.ShapeDtypeStruct((DEC_BATCH, D_MODEL), F32), name="final_sample",
        compiler_params=pltpu.CompilerParams(dimension_semantics=("arbitrary",), vmem_limit_bytes=VMEM_LIMIT),
    )(hmid, yg, gwt, p_sample, gple, wg, wp, gfin)


def _alibi_tables():
    slopes = 2.0 ** (-8.0 * (np.arange(N_HEADS, dtype=np.float64) + 1.0) / N_HEADS)
    i = np.arange(CHUNK)[:, None]
    j = np.arange(2 * CHUNK)[None, :]
    dist = i + CHUNK - j
    valid = (dist >= 0) & (dist <= WINDOW)
    tabs = []
    for has_prev in (False, True):
        ok = valid & ((j >= CHUNK) | has_prev)
        tabs.append(np.where(ok[None], -slopes[:, None, None] * dist[None], NEG))
    prompt = np.stack([np.stack([np.concatenate([t[h] for h in grp], axis=0) for grp in (NAT_HEADS, ROT_HEADS)])
                       for t in tabs]).astype(np.float32)
    wb = min(WINDOW, PAST_LEN)
    sample = (-slopes[:, None] * (wb - np.arange(wb))[None, :]).astype(np.float32)
    return prompt, sample


def kernel(x_prompt, x_sample, cache_k_win, cache_v_win, state_ssm_re, state_ssm_im, p_prompt, p_sample, norm_mix, w_in, sinks, ssm_lam_re, ssm_lam_im, ssm_log_step, ssm_b_re, ssm_b_im, ssm_c_re, ssm_c_im, ssm_d, w_glu, b_glu, norm_attn_out, norm_ssm_out, w_out, norm_ffn, w_router, b_router, w_up, b_up, w_down, b_down, norm_ple, w_ple_gate, w_ple_proj, norm_final):
    nst = SSM_G * SSM_P
    bias_np, sbias_np = _alibi_tables()
    bias = jnp.asarray(bias_np)
    sbias = jnp.asarray(sbias_np)

    lbr, lbi, bbr, bbi = _prep(ssm_lam_re[0], ssm_lam_im[0], ssm_log_step[0], ssm_b_re[0], ssm_b_im[0])
    bblk, lamr, lami, cblk = _s5_blocks(lbr, lbi, bbr, bbi, ssm_c_re[0], ssm_c_im[0])

    gmix = norm_mix[0].reshape(1, D_MODEL)
    win = w_in[0].astype(BF16)
    dskip = ssm_d[0].reshape(1, D_SSM)
    wglu = w_glu[0].astype(BF16)
    bglu = b_glu[0].reshape(1, D_SSM)
    ga = norm_attn_out[0].reshape(1, D_ATTN)
    gs = norm_ssm_out[0].reshape(1, D_SSM)
    wout = w_out[0].astype(BF16)
    sink = sinks[0]

    gffn = norm_ffn[0].reshape(1, D_MODEL)
    wrt = w_router[0].T
    br = b_router[0].reshape(N_EXPERTS, 1)
    hbuf, k_last, v_last, re_p, im_p, hnbuf, idxbuf, gwbuf, rankbuf, cnt_p = _mixer_prompt(
        x_prompt, sink, gmix, win, bias, bblk, lamr, lami, cblk, dskip, wglu, bglu, ga, gs, wout, gffn, wrt, br)

    ck = cache_k_win[0].reshape(DEC_BATCH, WINDOW, D_KV)
    cv = cache_v_win[0].reshape(DEC_BATCH, WINDOW, D_KV)
    hmid, k_new, v_new, re_s, im_s, hn, idx, gw, rank, counts = _mixer_sample(
        x_sample.reshape(DEC_BATCH, D_MODEL), gmix, win, ck, cv, sbias, sink.reshape(N_HEADS, 1), bblk, lamr, lami,
        cblk, dskip, wglu, bglu, ga, gs, wout, state_ssm_re[0].reshape(DEC_BATCH, nst),
        state_ssm_im[0].reshape(DEC_BATCH, nst), gffn, wrt, br, cnt_p, hbuf, hnbuf, idxbuf, gwbuf, rankbuf)

    offsets, tile_start, tile_row, tile_rows = _tile_plan(counts[:, 0])

    pos = _place(offsets, idx, rank)
    xs = _sc_dispatch(hn, pos)
    ys = _experts(tile_start, tile_row, tile_rows, xs, w_up[0], b_up[0], w_down[0], b_down[0])
    ygs = []
    for part in range(N_PARTS):
        r0 = part * PART_CHUNKS * ROWS
        r1 = T_PAD if part == N_PARTS - 1 else r0 + PART_CHUNKS * ROWS
        ygs.append(_sc_combine(ys, pos[:, r0:r1].reshape(TOP_K * (r1 - r0))).reshape(TOP_K, r1 - r0, D_PACK))

    gwt = gw.T
    gple = norm_ple[0].reshape(1, D_MODEL)
    wg = w_ple_gate[0].astype(BF16)
    wp = w_ple_proj[0].astype(BF16)
    gfin = norm_final.reshape(1, D_MODEL)
    y_prompt = None
    for part in range(N_PARTS):
        y_prompt = _final_prompt(part, hmid, ygs[part], gwt, p_prompt[0], gple, wg, wp, gfin, y_prompt)
    y_sample = _final_sample(hmid, ygs[-1], gwt, p_sample[0].reshape(DEC_BATCH, PLE_DIM), gple, wg, wp, gfin)

    k_win_s = jnp.concatenate([ck[:, 1:], k_new[:, None, :]], axis=1)
    v_win_s = jnp.concatenate([cv[:, 1:], v_new[:, None, :]], axis=1)
    kv5 = (1, BATCH, CHUNK, N_KV, HEAD_DIM)
    skv5 = (1, DEC_BATCH, WINDOW, N_KV, HEAD_DIM)
    return (y_prompt, y_sample.reshape(DEC_BATCH, 1, D_MODEL),
            k_last.reshape(kv5), v_last.reshape(kv5),
            re_p.reshape(1, BATCH, SSM_G, SSM_P), im_p.reshape(1, BATCH, SSM_G, SSM_P),
            k_win_s.reshape(skv5), v_win_s.reshape(skv5),
            re_s.reshape(1, DEC_BATCH, SSM_G, SSM_P), im_s.reshape(1, DEC_BATCH, SSM_G, SSM_P))
```

```python
import functools

import numpy as np
import jax
import jax.numpy as jnp
from jax import lax
from jax.experimental import pallas as pl
from jax.experimental.pallas import tpu as pltpu
from jax.experimental.pallas import tpu_sc as plsc

F32 = jnp.float32
BF16 = jnp.bfloat16

D_MODEL = 1024
BATCH = 8
SEQ = 2048
DEC_BATCH = 128
PAST_LEN = 16384
HEAD_DIM = 64
D_ATTN = 512
N_HEADS = 8
N_KV = 2
D_KV = N_KV * HEAD_DIM
WINDOW = 128
D_SSM = 512
SSM_H = 16
SSM_G = 32
SSM_P = 64
D_IN = D_ATTN + 2 * D_KV + D_SSM
N_EXPERTS = 32
TOP_K = 4
D_FF = 1024
SWIGLU_LIMIT = 7.0
SWIGLU_ALPHA = 1.702
PLE_DIM = 256
EPS = 1e-5
NEG = -1e30

CHUNK = 128
N_CHUNKS = SEQ // CHUNK
ROWS = BATCH * CHUNK
PITCH = CHUNK + 8
NAT_HEADS = (0, 2, 5, 7)
ROT_HEADS = (1, 3, 4, 6)
ATTN_UNROLL = 2
SUB_T = 64
SUB_ROWS = SUB_T * BATCH
N_SBLK = 4
SBLK = 512
T_REAL = BATCH * SEQ + DEC_BATCH
T_PAD = T_REAL + 128
SAMPLE_ROW0 = BATCH * SEQ
ROUTE_BLOCK = 512
TILE_SIZES = (1024, 512, 256)
TM = TILE_SIZES[0]
TM_UNIT = TILE_SIZES[-1]
P_ROWS = (T_PAD * TOP_K + N_EXPERTS * (TM_UNIT - 1)) // TM_UNIT * TM_UNIT
MAX_TILES = P_ROWS // TM + 2 * N_EXPERTS + 1
SC_ROWS = 40
SC_ROWS_COMBINE = 32
N_PARTS = 4
PART_CHUNKS = N_CHUNKS // N_PARTS
D_PACK = D_MODEL // 2
VMEM_LIMIT = 56 * 1024 * 1024


def _rms(x, g):
    return x * lax.rsqrt(jnp.mean(x * x, axis=-1, keepdims=True) + EPS) * g


def _sigmoid(x):
    return 1.0 / (1.0 + jnp.exp(-x))


def _gelu_tanh(x):
    c = np.float32(np.sqrt(2.0 / np.pi))
    return 0.5 * x * (1.0 + jnp.tanh(c * (x + 0.044715 * (x * x * x))))


def _bdot(a, b):
    return jnp.dot(a.astype(BF16), b, preferred_element_type=F32)


def _pack_pairs(x):
    n = x.shape[1] // 2
    lo = lax.bitcast_convert_type(x[:, :n].astype(BF16).astype(F32), jnp.int32)
    hi = lax.bitcast_convert_type(x[:, n:].astype(BF16).astype(F32), jnp.int32)
    return lax.shift_right_logical(lo, 16) | hi


def _unpack_pairs(w):
    lo = lax.bitcast_convert_type(lax.shift_left(w, 16), F32)
    hi = lax.bitcast_convert_type(w & jnp.int32(-65536), F32)
    return jnp.concatenate([lo, hi], axis=1)


def _full(shape):
    n = len(shape)
    return pl.BlockSpec(shape, lambda *_: (0,) * n)


def _prep_kernel(lr_ref, li_ref, ls_ref, br_ref, bi_ref, lbr_ref, lbi_ref, bbr_ref, bbi_ref):
    lr = lr_ref[...]
    li = li_ref[...]
    step = jnp.exp(ls_ref[...])
    zr = lr * step
    zi = li * step
    mag = jnp.exp(zr)
    lbr = mag * jnp.cos(zi)
    lbi = mag * jnp.sin(zi)
    lbr_ref[...] = lbr
    lbi_ref[...] = lbi
    nr = lbr - 1.0
    den = lr * lr + li * li
    cr = (nr * lr + lbi * li) / den
    ci = (lbi * lr - nr * li) / den
    br = br_ref[...]
    bi = bi_ref[...]
    bbr_ref[...] = cr * br - ci * bi
    bbi_ref[...] = cr * bi + ci * br


def _prep(lam_re, lam_im, log_step, b_re, b_im):
    g, p, h = SSM_G, SSM_P, SSM_H
    out = pl.pallas_call(
        _prep_kernel,
        out_shape=[jax.ShapeDtypeStruct((g, 1, p), F32), jax.ShapeDtypeStruct((g, 1, p), F32),
                   jax.ShapeDtypeStruct((g, h, p), F32), jax.ShapeDtypeStruct((g, h, p), F32)],
        name="s5_prep",
    )(lam_re.reshape(g, 1, p), lam_im.reshape(g, 1, p), log_step.reshape(g, 1, 1),
      jnp.transpose(b_re, (0, 2, 1)), jnp.transpose(b_im, (0, 2, 1)))
    return out


def _s5_blocks(lbr, lbi, bbr, bbi, c_re, c_im):
    eye = jnp.eye(8, dtype=F32)
    shp = (N_SBLK, 8, SSM_H, SSM_P)

    def in_map(b):
        return (b.reshape(shp)[:, :, :, None, :] * eye[None, :, None, :, None]).reshape(N_SBLK, 128, SBLK)

    def out_map(c):
        ct = jnp.transpose(c.reshape(shp), (0, 1, 3, 2))
        return (ct[:, :, :, None, :] * eye[None, :, None, :, None]).reshape(N_SBLK, SBLK, 128)

    bblk = jnp.concatenate([in_map(bbr), in_map(bbi)], axis=-1).astype(BF16)
    cblk = jnp.concatenate([out_map(c_re), -out_map(c_im)], axis=1).astype(BF16)
    return bblk, lbr.reshape(N_SBLK, 1, SBLK), lbi.reshape(N_SBLK, 1, SBLK), cblk


def _ssm_post(y_lin, u, dskip, wglu, bglu, gs):
    y = _gelu_tanh(y_lin + dskip * u)
    y = y * _sigmoid(_bdot(y, wglu) + bglu)
    return _rms(y, gs)


def _mixer_prompt_kernel(sinks_ref, x_ref, gmix_ref, win_ref, bias_ref, bblk_ref, lamr_ref, lami_ref,
                         cblk_ref, dskip_ref, wglu_ref, bglu_ref, ga_ref, gs_ref, wout_ref,
                         gffn_ref, wrt_ref, br_ref, tri_ref,
                         hmid_ref, klast_ref, vlast_ref, sre_ref, sim_ref, hn_ref, idx_ref, gw_ref, rank_ref, cnt_ref,
                         proj_s, u_s, kprev_s, kprevr_s, vprev_s, vprevr_s, attn_s, ssm_s, utb_s, bu_s, xs_s, ytb_s):
    c = pl.program_id(0)

    @pl.when(c == 0)
    def _():
        zkv = jnp.zeros(kprev_s.shape, BF16)
        kprev_s[...] = zkv
        kprevr_s[...] = zkv
        vprev_s[...] = zkv
        vprevr_s[...] = zkv
        sre_ref[...] = jnp.zeros(sre_ref.shape, F32)
        sim_ref[...] = jnp.zeros(sim_ref.shape, F32)
        cnt_ref[...] = jnp.zeros(cnt_ref.shape, jnp.int32)

    x = x_ref[...].reshape(ROWS, D_MODEL)
    proj = _bdot(_rms(x, gmix_ref[...]), win_ref[...])
    u0 = D_ATTN + 2 * D_KV
    proj_s[...] = proj[:, 0:u0]
    for jb in range(D_SSM // 128):
        for b in range(BATCH):
            u_s[jb, b * PITCH:b * PITCH + CHUNK, :] = proj[b * CHUNK:(b + 1) * CHUNK, u0 + 128 * jb:u0 + 128 * (jb + 1)]

    lo = lax.broadcasted_iota(jnp.int32, (CHUNK, 128), 1) < HEAD_DIM
    hi = jnp.logical_not(lo)
    table = jnp.minimum(c, 1)
    hrow = lax.broadcasted_iota(jnp.int32, (4 * CHUNK, 1), 0) // CHUNK

    def sink_col(heads):
        col = jnp.full((4 * CHUNK, 1), sinks_ref[heads[3]], F32)
        for n in (2, 1, 0):
            col = jnp.where(hrow == n, sinks_ref[heads[n]], col)
        return col

    sink_nat = sink_col(NAT_HEADS)
    sink_rot = sink_col(ROT_HEADS)

    def attend(q, k, v, bias, sink):
        s = lax.dot_general(q, k, (((1,), (1,)), ((), ())), preferred_element_type=F32) + bias
        m = jnp.maximum(jnp.max(s, axis=-1, keepdims=True), sink)
        p = jnp.exp(s - m)
        den = jnp.sum(p, axis=-1, keepdims=True) + jnp.exp(sink - m)
        return jnp.dot(p.astype(BF16), v, preferred_element_type=F32) / den

    def attn_body(b, carry):
        r0 = pl.multiple_of(b * CHUNK, CHUNK)
        rows = pl.ds(r0, CHUNK)
        kb = proj_s[rows, D_ATTN:D_ATTN + D_KV]
        vb = proj_s[rows, D_ATTN + D_KV:D_ATTN + 2 * D_KV]
        kb16 = kb.astype(BF16)
        vb16 = vb.astype(BF16)
        kbr16 = pltpu.roll(kb, HEAD_DIM, 1).astype(BF16)
        vbr16 = pltpu.roll(vb, HEAD_DIM, 1).astype(BF16)
        k_nat = jnp.concatenate([kprev_s[b], kb16], axis=0)
        k_rot = jnp.concatenate([kprevr_s[b], kbr16], axis=0)
        v_nat = jnp.concatenate([vprev_s[b], vb16], axis=0)
        v_rot = jnp.concatenate([vprevr_s[b], vbr16], axis=0)
        q2 = [proj_s[rows, 128 * jq:128 * (jq + 1)] * (HEAD_DIM ** -0.5) for jq in range(N_HEADS // 2)]
        q_nat = jnp.concatenate([jnp.where(lo if h % 2 == 0 else hi, q2[h // 2], 0.0) for h in NAT_HEADS],
                                axis=0).astype(BF16)
        q_rot = jnp.concatenate([jnp.where(lo if h % 2 == 0 else hi, q2[h // 2], 0.0) for h in ROT_HEADS],
                                axis=0).astype(BF16)
        o_nat = attend(q_nat, k_nat, v_nat, bias_ref[table, 0], sink_nat)
        o_rot = attend(q_rot, k_rot, v_rot, bias_ref[table, 1], sink_rot)
        for jq in range(N_HEADS // 2):
            blk = slice(CHUNK * jq, CHUNK * (jq + 1))
            even, odd = (o_nat, o_rot) if jq < 2 else (o_rot, o_nat)
            attn_s[rows, 128 * jq:128 * (jq + 1)] = jnp.where(lo, even[blk], odd[blk])
        kprev_s[b] = kb16
        kprevr_s[b] = kbr16
        vprev_s[b] = vb16
        vprevr_s[b] = vbr16
        return carry

    def attn_group(i, carry):
        for n in range(ATTN_UNROLL):
            attn_body(ATTN_UNROLL * i + n, carry)
        return carry

    lax.fori_loop(0, BATCH // ATTN_UNROLL, attn_group, 0)

    for sc in range(CHUNK // SUB_T):
        t0 = sc * SUB_T
        for i in range(SUB_T):
            for jb in range(D_SSM // 128):
                utb_s[i * BATCH:(i + 1) * BATCH, 128 * jb:128 * (jb + 1)] = (
                    u_s[jb, pl.ds(t0 + i, BATCH, stride=PITCH), :])
        u_tb = utb_s[...]
        for j in range(N_SBLK):
            bu_s[...] = _bdot(u_tb[:, 128 * j:128 * (j + 1)], bblk_ref[j])
            lr = jnp.broadcast_to(lamr_ref[j], (BATCH, SBLK))
            li = jnp.broadcast_to(lami_ref[j], (BATCH, SBLK))
            cols = slice(SBLK * j, SBLK * (j + 1))

            def step(i, carry):
                sr, si = carry
                r = pl.ds(pl.multiple_of(i * BATCH, BATCH), BATCH)
                nr = lr * sr - li * si + bu_s[r, 0:SBLK]
                ni = lr * si + li * sr + bu_s[r, SBLK:2 * SBLK]
                xs_s[r, 0:SBLK] = nr
                xs_s[r, SBLK:2 * SBLK] = ni
                return nr, ni

            sr, si = lax.fori_loop(0, SUB_T, step, (sre_ref[:, cols], sim_ref[:, cols]), unroll=True)
            sre_ref[:, cols] = sr
            sim_ref[:, cols] = si
            ytb_s[:, 128 * j:128 * (j + 1)] = _bdot(xs_s[...], cblk_ref[j])
        yn = _ssm_post(ytb_s[...], u_tb, dskip_ref[...], wglu_ref[...], bglu_ref[...], gs_ref[...])
        for i in range(SUB_T):
            for jb in range(D_SSM // 128):
                ssm_s[jb, pl.ds(t0 + i, BATCH, stride=PITCH), :] = (
                    yn[i * BATCH:(i + 1) * BATCH, 128 * jb:128 * (jb + 1)])

    an = _rms(attn_s[...], ga_ref[...])
    sn = jnp.concatenate(
        [jnp.concatenate([ssm_s[jb, b * PITCH:b * PITCH + CHUNK, :] for b in range(BATCH)], axis=0)
         for jb in range(D_SSM // 128)], axis=1)
    h = x + _bdot(jnp.concatenate([an.astype(BF16), sn.astype(BF16)], axis=1), wout_ref[...])
    hmid_ref[...] = h
    _route(h, gffn_ref, wrt_ref, br_ref, tri_ref, cnt_ref, hn_ref, idx_ref, gw_ref, rank_ref)

    @pl.when(c == N_CHUNKS - 1)
    def _():
        klast_ref[...] = proj_s[:, D_ATTN:D_ATTN + D_KV].reshape(BATCH, CHUNK, D_KV)
        vlast_ref[...] = proj_s[:, D_ATTN + D_KV:D_ATTN + 2 * D_KV].reshape(BATCH, CHUNK, D_KV)


def _mixer_prompt(x_prompt, sinks, gmix, win, bias, bblk, lamr, lami, cblk, dskip, wglu, bglu, ga, gs, wout,
                  gffn, wrt, br):
    smem = pl.BlockSpec(memory_space=pltpu.SMEM)
    in_specs = [
        smem,
        pl.BlockSpec((BATCH, CHUNK, D_MODEL), lambda c: (0, c, 0)),
        _full((1, D_MODEL)), _full((D_MODEL, D_IN)), _full((2, 2, 4 * CHUNK, 2 * CHUNK)),
        _full((N_SBLK, 128, 2 * SBLK)), _full((N_SBLK, 1, SBLK)), _full((N_SBLK, 1, SBLK)),
        _full((N_SBLK, 2 * SBLK, 128)), _full((1, D_SSM)), _full((D_SSM, D_SSM)), _full((1, D_SSM)),
        _full((1, D_ATTN)), _full((1, D_SSM)), _full((D_MODEL, D_MODEL)),
        _full((1, D_MODEL)), _full((N_EXPERTS, D_MODEL)), _full((N_EXPERTS, 1)), _full((ROUTE_BLOCK, ROUTE_BLOCK)),
    ]
    route_specs, route_shapes = _route_specs(ROWS, lambda c: c)
    out_shape = [
        jax.ShapeDtypeStruct((T_PAD, D_MODEL), F32),
        jax.ShapeDtypeStruct((BATCH, CHUNK, D_KV), F32),
        jax.ShapeDtypeStruct((BATCH, CHUNK, D_KV), F32),
        jax.ShapeDtypeStruct((BATCH, SSM_G * SSM_P), F32),
        jax.ShapeDtypeStruct((BATCH, SSM_G * SSM_P), F32),
    ] + route_shapes
    out_specs = [
        pl.BlockSpec((ROWS, D_MODEL), lambda c: (c, 0)),
        _full((BATCH, CHUNK, D_KV)), _full((BATCH, CHUNK, D_KV)),
        _full((BATCH, SSM_G * SSM_P)), _full((BATCH, SSM_G * SSM_P)),
    ] + route_specs
    kv_scr = pltpu.VMEM((BATCH, CHUNK, D_KV), BF16)
    scratch = [
        pltpu.VMEM((ROWS, D_ATTN + 2 * D_KV), F32), pltpu.VMEM((D_SSM // 128, BATCH * PITCH, 128), F32),
        kv_scr, kv_scr, kv_scr, kv_scr,
        pltpu.VMEM((ROWS, D_ATTN), F32), pltpu.VMEM((D_SSM // 128, BATCH * PITCH, 128), F32),
        pltpu.VMEM((SUB_ROWS, D_SSM), F32), pltpu.VMEM((SUB_ROWS, 2 * SBLK), F32),
        pltpu.VMEM((SUB_ROWS, 2 * SBLK), F32), pltpu.VMEM((SUB_ROWS, D_SSM), F32),
    ]
    return pl.pallas_call(
        _mixer_prompt_kernel, grid=(N_CHUNKS,), in_specs=in_specs, out_specs=out_specs, out_shape=out_shape,
        scratch_shapes=scratch, name="mixer_prompt",
        compiler_params=pltpu.CompilerParams(dimension_semantics=("arbitrary",), vmem_limit_bytes=VMEM_LIMIT),
    )(sinks, x_prompt, gmix, win, bias, bblk, lamr, lami, cblk, dskip, wglu, bglu, ga, gs, wout,
      gffn, wrt, br, _tri(ROUTE_BLOCK))


SGRP = 16
SPITCH = DEC_BATCH + 8
N_SGRP = DEC_BATCH // SGRP


def _mixer_sample_kernel(x_ref, gmix_ref, win_ref, ck_ref, cv_ref, sbias_ref, sinkc_ref, bblk_ref, lamr_ref,
                         lami_ref, cblk_ref, dskip_ref, wglu_ref, bglu_ref, ga_ref, gs_ref, wout_ref,
                         x0r_ref, x0i_ref, gffn_ref, wrt_ref, br_ref, tri_ref, cnt_in_ref,
                         hbuf_ref, hnbuf_ref, idxbuf_ref, gwbuf_ref, rankbuf_ref,
                         hmid_ref, knew_ref, vnew_ref, sre_ref, sim_ref, hn_ref, idx_ref, gw_ref, rank_ref, cnt_ref,
                         proj_s, qall_s, oall_s, kn8_s, vn8_s):
    del hbuf_ref, hnbuf_ref, idxbuf_ref, gwbuf_ref, rankbuf_ref
    g = pl.program_id(0)
    lo = lax.broadcasted_iota(jnp.int32, (DEC_BATCH, 128), 1) < HEAD_DIM

    @pl.when(g == 0)
    def _():
        proj = _bdot(_rms(x_ref[...], gmix_ref[...]), win_ref[...])
        proj_s[...] = proj
        for h in range(N_HEADS):
            jq, half, kv = h // 2, h % 2, h // 4
            q2 = proj[:, 128 * jq:128 * (jq + 1)] * (HEAD_DIM ** -0.5)
            if half != kv:
                q2 = pltpu.roll(q2, HEAD_DIM, 1)
            hr = slice(h * SPITCH, h * SPITCH + DEC_BATCH)
            qall_s[hr, :] = jnp.where(lo if kv == 0 else jnp.logical_not(lo), q2, 0.0)
            kn8_s[hr, :] = proj[:, D_ATTN:D_ATTN + D_KV]
            vn8_s[hr, :] = proj[:, D_ATTN + D_KV:D_ATTN + 2 * D_KV]

    def head_rows(ref):
        return jnp.stack([ref[pl.ds(g * SGRP + ii, N_HEADS, stride=SPITCH), :] for ii in range(SGRP)], axis=0)

    qe = head_rows(qall_s)
    knew = head_rows(kn8_s)
    vnew = head_rows(vn8_s)
    s = jnp.einsum('bhd,bkd->bhk', qe.astype(BF16), ck_ref[...].astype(BF16),
                   preferred_element_type=F32) + sbias_ref[...][None]
    s_new = jnp.sum(qe * knew, axis=-1, keepdims=True)
    sink = sinkc_ref[...][None]
    m = jnp.maximum(jnp.maximum(jnp.max(s, axis=-1, keepdims=True), s_new), sink)
    p = jnp.exp(s - m)
    p_new = jnp.exp(s_new - m)
    den = jnp.sum(p, axis=-1, keepdims=True) + p_new + jnp.exp(sink - m)
    o = (jnp.einsum('bhk,bkd->bhd', p.astype(BF16), cv_ref[...].astype(BF16),
                    preferred_element_type=F32) + p_new * vnew) / den
    for ii in range(SGRP):
        oall_s[pl.ds(g * SGRP + ii, N_HEADS, stride=SPITCH), :] = o[ii]

    @pl.when(g == N_SGRP - 1)
    def _():
        proj = proj_s[...]
        blocks = []
        for jq in range(N_HEADS // 2):
            oa = oall_s[(2 * jq) * SPITCH:(2 * jq) * SPITCH + DEC_BATCH, :]
            ob = oall_s[(2 * jq + 1) * SPITCH:(2 * jq + 1) * SPITCH + DEC_BATCH, :]
            if jq // 2 == 0:
                blocks.append(jnp.where(lo, oa, pltpu.roll(ob, HEAD_DIM, 1)))
            else:
                blocks.append(jnp.where(lo, pltpu.roll(oa, HEAD_DIM, 1), ob))
        attn = jnp.concatenate(blocks, axis=1)
        u = proj[:, D_ATTN + 2 * D_KV:]
        ys = []
        for j in range(N_SBLK):
            bu = _bdot(u[:, 128 * j:128 * (j + 1)], bblk_ref[j])
            lr = lamr_ref[j]
            li = lami_ref[j]
            cols = slice(SBLK * j, SBLK * (j + 1))
            sr = x0r_ref[:, cols]
            si = x0i_ref[:, cols]
            nr = lr * sr - li * si + bu[:, 0:SBLK]
            ni = lr * si + li * sr + bu[:, SBLK:2 * SBLK]
            sre_ref[:, cols] = nr
            sim_ref[:, cols] = ni
            ys.append(_bdot(jnp.concatenate([nr, ni], axis=1), cblk_ref[j]))
        sn = _ssm_post(jnp.concatenate(ys, axis=1), u, dskip_ref[...], wglu_ref[...], bglu_ref[...], gs_ref[...])
        an = _rms(attn, ga_ref[...])
        h = x_ref[...] + _bdot(jnp.concatenate([an.astype(BF16), sn.astype(BF16)], axis=1), wout_ref[...])
        h = jnp.concatenate([h, jnp.zeros((DEC_BATCH, D_MODEL), F32)], axis=0)
        hmid_ref[...] = h
        cnt_ref[...] = cnt_in_ref[...]
        _route(h, gffn_ref, wrt_ref, br_ref, tri_ref, cnt_ref, hn_ref, idx_ref, gw_ref, rank_ref)
        knew_ref[...] = proj[:, D_ATTN:D_ATTN + D_KV]
        vnew_ref[...] = proj[:, D_ATTN + D_KV:D_ATTN + 2 * D_KV]


def _mixer_sample(x_s, gmix, win, ck, cv, sbias, sinkc, bblk, lamr, lami, cblk, dskip, wglu, bglu, ga, gs, wout,
                  x0r, x0i, gffn, wrt, br, cnt_in, hbuf, hnbuf, idxbuf, gwbuf, rankbuf):
    nst = SSM_G * SSM_P
    in_specs = [
        _full((DEC_BATCH, D_MODEL)), _full((1, D_MODEL)), _full((D_MODEL, D_IN)),
        pl.BlockSpec((SGRP, WINDOW, D_KV), lambda g: (g, 0, 0)),
        pl.BlockSpec((SGRP, WINDOW, D_KV), lambda g: (g, 0, 0)),
        _full((N_HEADS, WINDOW)), _full((N_HEADS, 1)),
        _full((N_SBLK, 128, 2 * SBLK)), _full((N_SBLK, 1, SBLK)), _full((N_SBLK, 1, SBLK)),
        _full((N_SBLK, 2 * SBLK, 128)), _full((1, D_SSM)), _full((D_SSM, D_SSM)), _full((1, D_SSM)),
        _full((1, D_ATTN)), _full((1, D_SSM)), _full((D_MODEL, D_MODEL)),
        _full((DEC_BATCH, nst)), _full((DEC_BATCH, nst)),
        _full((1, D_MODEL)), _full((N_EXPERTS, D_MODEL)), _full((N_EXPERTS, 1)),
        _full((2 * DEC_BATCH, 2 * DEC_BATCH)), _full((N_EXPERTS, 1)),
    ] + [pl.BlockSpec(memory_space=pl.ANY)] * 5
    tail_block = SAMPLE_ROW0 // (2 * DEC_BATCH)
    route_specs, route_shapes = _route_specs(2 * DEC_BATCH, lambda g: tail_block)
    out_shape = [
        jax.ShapeDtypeStruct((T_PAD, D_MODEL), F32),
        jax.ShapeDtypeStruct((DEC_BATCH, D_KV), F32), jax.ShapeDtypeStruct((DEC_BATCH, D_KV), F32),
        jax.ShapeDtypeStruct((DEC_BATCH, nst), F32), jax.ShapeDtypeStruct((DEC_BATCH, nst), F32),
    ] + route_shapes
    out_specs = [
        pl.BlockSpec((2 * DEC_BATCH, D_MODEL), lambda g: (tail_block, 0)),
        _full((DEC_BATCH, D_KV)), _full((DEC_BATCH, D_KV)),
        _full((DEC_BATCH, nst)), _full((DEC_BATCH, nst)),
    ] + route_specs
    head_rows = pltpu.VMEM((N_HEADS * SPITCH, 128), F32)
    scratch = [pltpu.VMEM((DEC_BATCH, D_IN), F32), head_rows, head_rows, head_rows, head_rows]
    return pl.pallas_call(
        _mixer_sample_kernel, grid=(N_SGRP,), in_specs=in_specs, out_specs=out_specs, out_shape=out_shape,
        scratch_shapes=scratch, input_output_aliases={24: 0, 25: 5, 26: 6, 27: 7, 28: 8}, name="mixer_sample",
        compiler_params=pltpu.CompilerParams(dimension_semantics=("arbitrary",), vmem_limit_bytes=VMEM_LIMIT),
    )(x_s, gmix, win, ck, cv, sbias, sinkc, bblk, lamr, lami, cblk, dskip, wglu, bglu, ga, gs, wout, x0r, x0i,
      gffn, wrt, br, _tri(2 * DEC_BATCH), cnt_in, hbuf, hnbuf, idxbuf, gwbuf, rankbuf)


def _route(h, g_ref, wrt_ref, br_ref, tri_ref, cnt_ref, hn_ref, idx_ref, gw_ref, rank_ref):
    hn = _rms(h, g_ref[...])
    hn_ref[...] = _pack_pairs(hn)
    hn_hi = hn.astype(BF16)
    hn_lo = (hn - hn_hi.astype(F32)).astype(BF16)
    w = wrt_ref[...]
    w_hi = w.astype(BF16)
    w_lo = (w - w_hi.astype(F32)).astype(BF16)
    nt = (((1,), (1,)), ((), ()))
    both = lax.dot_general(jnp.concatenate([w_hi, w_lo], axis=0), hn_hi, nt, preferred_element_type=F32)
    logits = (both[:N_EXPERTS] + both[N_EXPERTS:]
              + lax.dot_general(w_hi, hn_lo, nt, preferred_element_type=F32)) + br_ref[...]
    eidx = lax.broadcasted_iota(jnp.int32, logits.shape, 0)
    vals, onehots = [], []
    l = logits
    for k in range(TOP_K):
        m = jnp.max(l, axis=0, keepdims=True)
        ik = jnp.min(jnp.where(l == m, eidx, N_EXPERTS), axis=0, keepdims=True)
        oh = eidx == ik
        idx_ref[k:k + 1, :] = ik
        vals.append(m)
        onehots.append(oh)
        l = jnp.where(oh, -jnp.inf, l)
    exps = [jnp.exp(v - vals[0]) for v in vals]
    den = exps[0] + exps[1] + exps[2] + exps[3]
    for k in range(TOP_K):
        gw_ref[k:k + 1, :] = exps[k] / den
    member = jnp.zeros(logits.shape, F32)
    for oh in onehots:
        member = member + jnp.where(oh, 1.0, 0.0)
    wblk = tri_ref.shape[0]
    base = cnt_ref[...].astype(F32)
    befores = []
    for cb in range(h.shape[0] // wblk):
        mblk = member[:, cb * wblk:(cb + 1) * wblk]
        befores.append(jnp.dot(mblk.astype(BF16), tri_ref[...], preferred_element_type=F32) + base)
        base = base + jnp.sum(mblk, axis=1, keepdims=True)
    before = jnp.concatenate(befores, axis=1)
    for k in range(TOP_K):
        rank_ref[k:k + 1, :] = jnp.sum(jnp.where(onehots[k], before, 0.0), axis=0, keepdims=True).astype(jnp.int32)
    cnt_ref[...] = base.astype(jnp.int32)


def _route_specs(rows, block):
    specs = [pl.BlockSpec((rows, D_PACK), lambda i: (block(i), 0)), pl.BlockSpec((TOP_K, rows), lambda i: (0, block(i))),
             pl.BlockSpec((TOP_K, rows), lambda i: (0, block(i))), pl.BlockSpec((TOP_K, rows), lambda i: (0, block(i))),
             _full((N_EXPERTS, 1))]
    shapes = [jax.ShapeDtypeStruct((T_PAD, D_PACK), jnp.int32), jax.ShapeDtypeStruct((TOP_K, T_PAD), jnp.int32),
              jax.ShapeDtypeStruct((TOP_K, T_PAD), F32), jax.ShapeDtypeStruct((TOP_K, T_PAD), jnp.int32),
              jax.ShapeDtypeStruct((N_EXPERTS, 1), jnp.int32)]
    return specs, shapes


def _tri(n):
    return jnp.asarray(np.triu(np.ones((n, n), np.float32), 1), BF16)


def _place_kernel(off_ref, idx_ref, rank_ref, pos_ref):
    idx = idx_ref[...]
    pos = rank_ref[...]
    for e in range(N_EXPERTS):
        pos = pos + jnp.where(idx == e, off_ref[e], 0)
    pos_ref[...] = pos


def _place(offsets, idx, rank):
    return pl.pallas_call(
        _place_kernel,
        in_specs=[pl.BlockSpec(memory_space=pltpu.SMEM), pl.BlockSpec(memory_space=pltpu.VMEM),
                  pl.BlockSpec(memory_space=pltpu.VMEM)],
        out_specs=pl.BlockSpec(memory_space=pltpu.VMEM),
        out_shape=jax.ShapeDtypeStruct((TOP_K, T_PAD), jnp.int32), name="place",
    )(offsets, idx, rank)


def _sc_mesh():
    return plsc.VectorSubcoreMesh(core_axis_name="core", subcore_axis_name="subcore")


def _sc_dispatch(rows, pos):
    n, d = rows.shape
    nblk = n // SC_ROWS
    pos_w = pos.reshape(TOP_K, nblk, SC_ROWS).transpose(1, 0, 2)

    @functools.partial(pl.kernel, out_type=jax.ShapeDtypeStruct((P_ROWS, d), rows.dtype), mesh=_sc_mesh(),
                       scratch_types=[], name="dispatch")
    def run(x_hbm, i_hbm, o_hbm):
        def body(x_vmem, i_vmem):
            for k in range(TOP_K):
                pltpu.sync_copy(x_vmem, o_hbm.at[i_vmem.at[0, k]])

        pltpu.emit_pipeline(
            body, grid=(nblk,),
            in_specs=[pl.BlockSpec((SC_ROWS, d), lambda i: (i, 0)),
                      pl.BlockSpec((1, TOP_K, SC_ROWS), lambda i: (i, 0, 0))],
            out_specs=[], core_axis_name=("core", "subcore"), dimension_semantics=(pltpu.PARALLEL,),
        )(x_hbm, i_hbm)

    return run(rows, pos_w)


def _sc_combine(rows, flat_pos):
    _, d = rows.shape
    n = flat_pos.shape[0]
    w = SC_ROWS_COMBINE
    nblk = n // w
    pos_w = flat_pos.reshape(nblk, 1, w)

    @functools.partial(pl.kernel, out_type=jax.ShapeDtypeStruct((n, d), rows.dtype), mesh=_sc_mesh(),
                       scratch_types=[], name="combine")
    def run(x_hbm, i_hbm, o_hbm):
        def body(i_vmem, o_vmem):
            pltpu.sync_copy(x_hbm.at[i_vmem.at[0, 0]], o_vmem)

        pltpu.emit_pipeline(
            body, grid=(nblk,),
            in_specs=[pl.BlockSpec((1, 1, w), lambda i: (i, 0, 0))],
            out_specs=[pl.BlockSpec((w, d), lambda i: (i, 0))],
            core_axis_name=("core", "subcore"), dimension_semantics=(pltpu.PARALLEL,),
        )(i_hbm, o_hbm)

    return run(rows, pos_w)


def _experts_kernel(ts_ref, trow_ref, tsz_ref, xs_hbm, w1_hbm, b1_ref, w2_hbm, b2_ref, ys_hbm,
                    w1_st, w2_st, w1_s, w2_s, xbuf, ybuf, wsem, xsem, ysem):
    e = pl.program_id(0)
    n_valid = ts_ref[N_EXPERTS]

    def w_copies(ex, slot):
        return (pltpu.make_async_copy(w1_hbm.at[ex], w1_st.at[slot], wsem.at[0, slot]),
                pltpu.make_async_copy(w2_hbm.at[ex], w2_st.at[slot], wsem.at[1, slot]))

    def x_copy(t, slot, rows):
        src = xs_hbm.at[pl.ds(pl.multiple_of(trow_ref[t], TM_UNIT), rows)]
        return pltpu.make_async_copy(src, xbuf.at[slot, pl.ds(0, rows)], xsem.at[slot])

    def y_copy(t, slot, rows):
        dst = ys_hbm.at[pl.ds(pl.multiple_of(trow_ref[t], TM_UNIT), rows)]
        return pltpu.make_async_copy(ybuf.at[slot, pl.ds(0, rows)], dst, ysem.at[slot])

    def by_size(t, fn):
        for rows in TILE_SIZES:
            @pl.when(tsz_ref[t] == rows)
            def _():
                fn(rows)

    @pl.when(e == 0)
    def _():
        for c in w_copies(0, 0):
            c.start()
        by_size(0, lambda rows: x_copy(0, 0, rows).start())

    @pl.when(e + 1 < N_EXPERTS)
    def _():
        for c in w_copies(e + 1, (e + 1) % 2):
            c.start()

    wslot = e % 2
    for c in w_copies(e, wslot):
        c.wait()
    for r in range(4):
        rs = slice(256 * r, 256 * (r + 1))
        w1_s[rs, :] = w1_st[wslot, rs, :].astype(BF16)
        w2_s[rs, :] = w2_st[wslot, rs, :].astype(BF16)

    def tile(t, carry):
        slot = t % 2

        @pl.when(t + 1 < n_valid)
        def _():
            by_size(t + 1, lambda rows: x_copy(t + 1, 1 - slot, rows).start())

        by_size(t, lambda rows: x_copy(t, slot, rows).wait())

        @pl.when(t >= 2)
        def _():
            by_size(t - 2, lambda rows: y_copy(t - 2, slot, rows).wait())

        def compute(rows):
            hdn = _bdot(_unpack_pairs(xbuf[slot, 0:rows]), w1_s[...]) + b1_ref[0]
            gt = jnp.minimum(hdn[:, :D_FF], SWIGLU_LIMIT)
            up = jnp.clip(hdn[:, D_FF:], -SWIGLU_LIMIT, SWIGLU_LIMIT)
            act = (up + 1.0) * gt * _sigmoid(SWIGLU_ALPHA * gt)
            ybuf[slot, 0:rows] = _pack_pairs(_bdot(act, w2_s[...]) + b2_ref[0])
            y_copy(t, slot, rows).start()

        by_size(t, compute)
        return carry

    lax.fori_loop(ts_ref[e], ts_ref[e + 1], tile, 0)

    @pl.when(e == N_EXPERTS - 1)
    def _():
        @pl.when(n_valid >= 2)
        def _():
            by_size(n_valid - 2, lambda rows: y_copy(n_valid - 2, n_valid % 2, rows).wait())

        by_size(n_valid - 1, lambda rows: y_copy(n_valid - 1, (n_valid - 1) % 2, rows).wait())


def _tile_plan(counts):
    units = (counts + (TM_UNIT - 1)) // TM_UNIT
    unit_end = jnp.cumsum(units)
    offsets = ((unit_end - units) * TM_UNIT).astype(jnp.int32)
    per_full = TM // TM_UNIT
    n_full = units // per_full
    rest = units % per_full
    has_mid = (rest >= 2).astype(jnp.int32)
    tiles_per = n_full + has_mid + rest % 2
    tile_end = jnp.cumsum(tiles_per)
    tile_start = jnp.concatenate([jnp.zeros((1,), jnp.int32), tile_end.astype(jnp.int32)])
    t = jnp.arange(MAX_TILES, dtype=jnp.int32)[:, None]
    done = tile_end[None, :] <= t
    mine = jnp.sum(done.astype(jnp.int32), axis=1, keepdims=True) == jnp.arange(N_EXPERTS, dtype=jnp.int32)[None, :]
    pick = lambda v: jnp.sum(jnp.where(mine, v[None, :], 0), axis=1)
    j = t[:, 0] - jnp.max(jnp.where(done, tile_end[None, :], 0), axis=1)
    nf, mid = pick(n_full), pick(has_mid)
    rows = jnp.where(j < nf, TM, jnp.where((j == nf) & (mid == 1), TILE_SIZES[1], TILE_SIZES[2]))
    row0 = pick(offsets) + jnp.minimum(j, nf) * TM + jnp.where(j > nf, TILE_SIZES[1], 0)
    valid = t[:, 0] < tile_end[-1]
    tile_row = jnp.where(valid, row0, 0).astype(jnp.int32)
    tile_rows = jnp.where(valid, rows, 0).astype(jnp.int32)
    return offsets, tile_start, tile_row, tile_rows


def _experts(tile_start, tile_row, tile_rows, xs, w_up, b_up, w_down, b_down):
    wsel = lambda e, *_: (e, 0, 0)
    hbm = pl.BlockSpec(memory_space=pl.ANY)
    grid_spec = pltpu.PrefetchScalarGridSpec(
        num_scalar_prefetch=3, grid=(N_EXPERTS,),
        in_specs=[hbm, hbm, pl.BlockSpec((1, 1, 2 * D_FF), wsel), hbm, pl.BlockSpec((1, 1, D_MODEL), wsel)],
        out_specs=hbm,
        scratch_shapes=[pltpu.VMEM((2, D_MODEL, 2 * D_FF), F32), pltpu.VMEM((2, D_FF, D_MODEL), F32),
                        pltpu.VMEM((D_MODEL, 2 * D_FF), BF16), pltpu.VMEM((D_FF, D_MODEL), BF16),
                        pltpu.VMEM((2, TM, D_PACK), jnp.int32), pltpu.VMEM((2, TM, D_PACK), jnp.int32),
                        pltpu.SemaphoreType.DMA((2, 2)), pltpu.SemaphoreType.DMA((2,)),
                        pltpu.SemaphoreType.DMA((2,))],
    )
    return pl.pallas_call(
        _experts_kernel, grid_spec=grid_spec, out_shape=jax.ShapeDtypeStruct((P_ROWS, D_PACK), jnp.int32),
        name="experts",
        compiler_params=pltpu.CompilerParams(dimension_semantics=("arbitrary",), vmem_limit_bytes=VMEM_LIMIT),
    )(tile_start, tile_row, tile_rows, xs, w_up, b_up.reshape(N_EXPERTS, 1, 2 * D_FF), w_down,
      b_down.reshape(N_EXPERTS, 1, D_MODEL))


def _final_kernel(h_ref, yg_ref, gw_ref, p_ref, gple_ref, wg_ref, wp_ref, gfin_ref, *rest):
    out_ref = rest[-1]
    rows = h_ref.shape[0]
    gw = gw_ref[...]
    h = h_ref[...]
    for k in range(TOP_K):
        h = h + gw[:, k:k + 1] * _unpack_pairs(yg_ref[k])
    gate = _sigmoid(_bdot(_rms(h, gple_ref[...]), wg_ref[...]))
    h = h + gate * _bdot(p_ref[...].reshape(rows, PLE_DIM), wp_ref[...])
    out_ref[...] = _rms(h, gfin_ref[...]).reshape(out_ref.shape)


def _final_prompt(part, hmid, yg, gwt, p_prompt, gple, wg, wp, gfin, y_prev):
    nb = BATCH
    rows = nb * CHUNK
    nbh = BATCH // nb
    c0 = part * PART_CHUNKS
    rb = lambda c, b: ((c0 + c) * nbh + b, 0)
    in_specs = [pl.BlockSpec((rows, D_MODEL), rb),
                pl.BlockSpec((TOP_K, rows, D_PACK), lambda c, b: (0, c * nbh + b, 0)),
                pl.BlockSpec((rows, TOP_K), rb),
                pl.BlockSpec((nb, CHUNK, PLE_DIM), lambda c, b: (b, c0 + c, 0)),
                _full((1, D_MODEL)), _full((D_MODEL, D_MODEL)), _full((PLE_DIM, D_MODEL)), _full((1, D_MODEL))]
    args = [hmid, yg, gwt, p_prompt, gple, wg, wp, gfin]
    aliases = {}
    if y_prev is not None:
        in_specs.append(pl.BlockSpec(memory_space=pl.ANY))
        args.append(y_prev)
        aliases = {len(args) - 1: 0}
    return pl.pallas_call(
        _final_kernel, grid=(PART_CHUNKS, nbh), in_specs=in_specs,
        out_specs=pl.BlockSpec((nb, CHUNK, D_MODEL), lambda c, b: (b, c0 + c, 0)),
        out_shape=jax.ShapeDtypeStruct((BATCH, SEQ, D_MODEL), F32), name="final_prompt",
        input_output_aliases=aliases,
        compiler_params=pltpu.CompilerParams(dimension_semantics=("arbitrary", "arbitrary"),
                                             vmem_limit_bytes=VMEM_LIMIT),
    )(*args)


def _final_sample(hmid, yg, gwt, p_sample, gple, wg, wp, gfin):
    blk = SAMPLE_ROW0 // DEC_BATCH
    return pl.pallas_call(
        _final_kernel, grid=(1,),
        in_specs=[pl.BlockSpec((DEC_BATCH, D_MODEL), lambda i: (blk, 0)),
                  pl.BlockSpec((TOP_K, DEC_BATCH, D_PACK), lambda i: (0, PART_CHUNKS * ROWS // DEC_BATCH, 0)),
                  pl.BlockSpec((DEC_BATCH, TOP_K), lambda i: (blk, 0)),
                  _full((DEC_BATCH, PLE_DIM)),
                  _full((1, D_MODEL)), _full((D_MODEL, D_MODEL)), _full((PLE_DIM, D_MODEL)), _full((1, D_MODEL))],
        out_specs=_full((DEC_BATCH, D_MODEL)),
        out_shape=jax.ShapeDtypeStruct((DEC_BATCH, D_MODEL), F32), name="final_sample",
        compiler_params=pltpu.CompilerParams(dimension_semantics=("arbitrary",), vmem_limit_bytes=VMEM_LIMIT),
    )(hmid, yg, gwt, p_sample, gple, wg, wp, gfin)


def _alibi_tables():
    slopes = 2.0 ** (-8.0 * (np.arange(N_HEADS, dtype=np.float64) + 1.0) / N_HEADS)
    i = np.arange(CHUNK)[:, None]
    j = np.arange(2 * CHUNK)[None, :]
    dist = i + CHUNK - j
    valid = (dist >= 0) & (dist <= WINDOW)
    tabs = []
    for has_prev in (False, True):
        ok = valid & ((j >= CHUNK) | has_prev)
        tabs.append(np.where(ok[None], -slopes[:, None, None] * dist[None], NEG))
    prompt = np.stack([np.stack([np.concatenate([t[h] for h in grp], axis=0) for grp in (NAT_HEADS, ROT_HEADS)])
                       for t in tabs]).astype(np.float32)
    wb = min(WINDOW, PAST_LEN)
    sample = (-slopes[:, None] * (wb - np.arange(wb))[None, :]).astype(np.float32)
    return prompt, sample


def kernel(x_prompt, x_sample, cache_k_win, cache_v_win, state_ssm_re, state_ssm_im, p_prompt, p_sample, norm_mix, w_in, sinks, ssm_lam_re, ssm_lam_im, ssm_log_step, ssm_b_re, ssm_b_im, ssm_c_re, ssm_c_im, ssm_d, w_glu, b_glu, norm_attn_out, norm_ssm_out, w_out, norm_ffn, w_router, b_router, w_up, b_up, w_down, b_down, norm_ple, w_ple_gate, w_ple_proj, norm_final):
    nst = SSM_G * SSM_P
    bias_np, sbias_np = _alibi_tables()
    bias = jnp.asarray(bias_np)
    sbias = jnp.asarray(sbias_np)

    lbr, lbi, bbr, bbi = _prep(ssm_lam_re[0], ssm_lam_im[0], ssm_log_step[0], ssm_b_re[0], ssm_b_im[0])
    bblk, lamr, lami, cblk = _s5_blocks(lbr, lbi, bbr, bbi, ssm_c_re[0], ssm_c_im[0])

    gmix = norm_mix[0].reshape(1, D_MODEL)
    win = w_in[0].astype(BF16)
    dskip = ssm_d[0].reshape(1, D_SSM)
    wglu = w_glu[0].astype(BF16)
    bglu = b_glu[0].reshape(1, D_SSM)
    ga = norm_attn_out[0].reshape(1, D_ATTN)
    gs = norm_ssm_out[0].reshape(1, D_SSM)
    wout = w_out[0].astype(BF16)
    sink = sinks[0]

    gffn = norm_ffn[0].reshape(1, D_MODEL)
    wrt = w_router[0].T
    br = b_router[0].reshape(N_EXPERTS, 1)
    hbuf, k_last, v_last, re_p, im_p, hnbuf, idxbuf, gwbuf, rankbuf, cnt_p = _mixer_prompt(
        x_prompt, sink, gmix, win, bias, bblk, lamr, lami, cblk, dskip, wglu, bglu, ga, gs, wout, gffn, wrt, br)

    ck = cache_k_win[0].reshape(DEC_BATCH, WINDOW, D_KV)
    cv = cache_v_win[0].reshape(DEC_BATCH, WINDOW, D_KV)
    hmid, k_new, v_new, re_s, im_s, hn, idx, gw, rank, counts = _mixer_sample(
        x_sample.reshape(DEC_BATCH, D_MODEL), gmix, win, ck, cv, sbias, sink.reshape(N_HEADS, 1), bblk, lamr, lami,
        cblk, dskip, wglu, bglu, ga, gs, wout, state_ssm_re[0].reshape(DEC_BATCH, nst),
        state_ssm_im[0].reshape(DEC_BATCH, nst), gffn, wrt, br, cnt_p, hbuf, hnbuf, idxbuf, gwbuf, rankbuf)

    offsets, tile_start, tile_row, tile_rows = _tile_plan(counts[:, 0])

    pos = _place(offsets, idx, rank)
    xs = _sc_dispatch(hn, pos)
    ys = _experts(tile_start, tile_row, tile_rows, xs, w_up[0], b_up[0], w_down[0], b_down[0])
    ygs = []
    for part in range(N_PARTS):
        r0 = part * PART_CHUNKS * ROWS
        r1 = T_PAD if part == N_PARTS - 1 else r0 + PART_CHUNKS * ROWS
        ygs.append(_sc_combine(ys, pos[:, r0:r1].reshape(TOP_K * (r1 - r0))).reshape(TOP_K, r1 - r0, D_PACK))

    gwt = gw.T
    gple = norm_ple[0].reshape(1, D_MODEL)
    wg = w_ple_gate[0].astype(BF16)
    wp = w_ple_proj[0].astype(BF16)
    gfin = norm_final.reshape(1, D_MODEL)
    y_prompt = None
    for part in range(N_PARTS):
        y_prompt = _final_prompt(part, hmid, ygs[part], gwt, p_prompt[0], gple, wg, wp, gfin, y_prompt)
    y_sample = _final_sample(hmid, ygs[-1], gwt, p_sample[0].reshape(DEC_BATCH, PLE_DIM), gple, wg, wp, gfin)

    k_win_s = jnp.concatenate([ck[:, 1:], k_new[:, None, :]], axis=1)
    v_win_s = jnp.concatenate([cv[:, 1:], v_new[:, None, :]], axis=1)
    kv5 = (1, BATCH, CHUNK, N_KV, HEAD_DIM)
    skv5 = (1, DEC_BATCH, WINDOW, N_KV, HEAD_DIM)
    return (y_prompt, y_sample.reshape(DEC_BATCH, 1, D_MODEL),
            k_last.reshape(kv5), v_last.reshape(kv5),
            re_p.reshape(1, BATCH, SSM_G, SSM_P), im_p.reshape(1, BATCH, SSM_G, SSM_P),
            k_win_s.reshape(skv5), v_win_s.reshape(skv5),
            re_s.reshape(1, DEC_BATCH, SSM_G, SSM_P), im_s.reshape(1, DEC_BATCH, SSM_G, SSM_P))
```

```python
import functools

import numpy as np
import jax
import jax.numpy as jnp
from jax import lax
from jax.experimental import pallas as pl
from jax.experimental.pallas import tpu as pltpu
from jax.experimental.pallas import tpu_sc as plsc

F32 = jnp.float32
BF16 = jnp.bfloat16

D_MODEL = 1024
BATCH = 8
SEQ = 2048
DEC_BATCH = 128
PAST_LEN = 16384
HEAD_DIM = 64
D_ATTN = 512
N_HEADS = 8
N_KV = 2
D_KV = N_KV * HEAD_DIM
WINDOW = 128
D_SSM = 512
SSM_H = 16
SSM_G = 32
SSM_P = 64
D_IN = D_ATTN + 2 * D_KV + D_SSM
N_EXPERTS = 32
TOP_K = 4
D_FF = 1024
SWIGLU_LIMIT = 7.0
SWIGLU_ALPHA = 1.702
PLE_DIM = 256
EPS = 1e-5
NEG = -1e30

CHUNK = 128
N_CHUNKS = SEQ // CHUNK
ROWS = BATCH * CHUNK
PITCH = CHUNK + 8
NAT_HEADS = (0, 2, 5, 7)
ROT_HEADS = (1, 3, 4, 6)
ATTN_UNROLL = 2
SUB_T = 64
SUB_ROWS = SUB_T * BATCH
N_SBLK = 4
SBLK = 512
T_REAL = BATCH * SEQ + DEC_BATCH
T_PAD = T_REAL + 128
SAMPLE_ROW0 = BATCH * SEQ
ROUTE_BLOCK = 512
TILE_SIZES = (1024, 512, 256)
TM = TILE_SIZES[0]
TM_UNIT = TILE_SIZES[-1]
P_ROWS = (T_PAD * TOP_K + N_EXPERTS * (TM_UNIT - 1)) // TM_UNIT * TM_UNIT
MAX_TILES = P_ROWS // TM + 2 * N_EXPERTS + 1
SC_ROWS = 104
SC_ROWS_COMBINE = 32
N_PARTS = 4
PART_CHUNKS = N_CHUNKS // N_PARTS
D_PACK = D_MODEL // 2
VMEM_LIMIT = 56 * 1024 * 1024


def _rms(x, g):
    return x * lax.rsqrt(jnp.mean(x * x, axis=-1, keepdims=True) + EPS) * g


def _sigmoid(x):
    return 1.0 / (1.0 + jnp.exp(-x))


def _gelu_tanh(x):
    c = np.float32(np.sqrt(2.0 / np.pi))
    return 0.5 * x * (1.0 + jnp.tanh(c * (x + 0.044715 * (x * x * x))))


def _bdot(a, b):
    return jnp.dot(a.astype(BF16), b, preferred_element_type=F32)


def _pack_pairs(x):
    n = x.shape[1] // 2
    lo = lax.bitcast_convert_type(x[:, :n].astype(BF16).astype(F32), jnp.int32)
    hi = lax.bitcast_convert_type(x[:, n:].astype(BF16).astype(F32), jnp.int32)
    return lax.shift_right_logical(lo, 16) | hi


def _unpack_pairs(w):
    lo = lax.bitcast_convert_type(lax.shift_left(w, 16), F32)
    hi = lax.bitcast_convert_type(w & jnp.int32(-65536), F32)
    return jnp.concatenate([lo, hi], axis=1)


def _full(shape):
    n = len(shape)
    return pl.BlockSpec(shape, lambda *_: (0,) * n)


def _prep_kernel(lr_ref, li_ref, ls_ref, br_ref, bi_ref, lbr_ref, lbi_ref, bbr_ref, bbi_ref):
    lr = lr_ref[...]
    li = li_ref[...]
    step = jnp.exp(ls_ref[...])
    zr = lr * step
    zi = li * step
    mag = jnp.exp(zr)
    lbr = mag * jnp.cos(zi)
    lbi = mag * jnp.sin(zi)
    lbr_ref[...] = lbr
    lbi_ref[...] = lbi
    nr = lbr - 1.0
    den = lr * lr + li * li
    cr = (nr * lr + lbi * li) / den
    ci = (lbi * lr - nr * li) / den
    br = br_ref[...]
    bi = bi_ref[...]
    bbr_ref[...] = cr * br - ci * bi
    bbi_ref[...] = cr * bi + ci * br


def _prep(lam_re, lam_im, log_step, b_re, b_im):
    g, p, h = SSM_G, SSM_P, SSM_H
    out = pl.pallas_call(
        _prep_kernel,
        out_shape=[jax.ShapeDtypeStruct((g, 1, p), F32), jax.ShapeDtypeStruct((g, 1, p), F32),
                   jax.ShapeDtypeStruct((g, h, p), F32), jax.ShapeDtypeStruct((g, h, p), F32)],
        name="s5_prep",
    )(lam_re.reshape(g, 1, p), lam_im.reshape(g, 1, p), log_step.reshape(g, 1, 1),
      jnp.transpose(b_re, (0, 2, 1)), jnp.transpose(b_im, (0, 2, 1)))
    return out


def _s5_blocks(lbr, lbi, bbr, bbi, c_re, c_im):
    eye = jnp.eye(8, dtype=F32)
    shp = (N_SBLK, 8, SSM_H, SSM_P)

    def in_map(b):
        return (b.reshape(shp)[:, :, :, None, :] * eye[None, :, None, :, None]).reshape(N_SBLK, 128, SBLK)

    def out_map(c):
        ct = jnp.transpose(c.reshape(shp), (0, 1, 3, 2))
        return (ct[:, :, :, None, :] * eye[None, :, None, :, None]).reshape(N_SBLK, SBLK, 128)

    bblk = jnp.concatenate([in_map(bbr), in_map(bbi)], axis=-1).astype(BF16)
    cblk = jnp.concatenate([out_map(c_re), -out_map(c_im)], axis=1).astype(BF16)
    return bblk, lbr.reshape(N_SBLK, 1, SBLK), lbi.reshape(N_SBLK, 1, SBLK), cblk


def _ssm_post(y_lin, u, dskip, wglu, bglu, gs):
    y = _gelu_tanh(y_lin + dskip * u)
    y = y * _sigmoid(_bdot(y, wglu) + bglu)
    return _rms(y, gs)


def _mixer_prompt_kernel(sinks_ref, x_ref, gmix_ref, win_ref, bias_ref, bblk_ref, lamr_ref, lami_ref,
                         cblk_ref, dskip_ref, wglu_ref, bglu_ref, ga_ref, gs_ref, wout_ref,
                         gffn_ref, wrt_ref, br_ref, tri_ref,
                         hmid_ref, klast_ref, vlast_ref, sre_ref, sim_ref, hn_ref, idx_ref, gw_ref, rank_ref, cnt_ref,
                         proj_s, u_s, kprev_s, kprevr_s, vprev_s, vprevr_s, attn_s, ssm_s, utb_s, bu_s, xs_s, ytb_s):
    c = pl.program_id(0)

    @pl.when(c == 0)
    def _():
        zkv = jnp.zeros(kprev_s.shape, BF16)
        kprev_s[...] = zkv
        kprevr_s[...] = zkv
        vprev_s[...] = zkv
        vprevr_s[...] = zkv
        sre_ref[...] = jnp.zeros(sre_ref.shape, F32)
        sim_ref[...] = jnp.zeros(sim_ref.shape, F32)
        cnt_ref[...] = jnp.zeros(cnt_ref.shape, jnp.int32)

    x = x_ref[...].reshape(ROWS, D_MODEL)
    proj = _bdot(_rms(x, gmix_ref[...]), win_ref[...])
    u0 = D_ATTN + 2 * D_KV
    proj_s[...] = proj[:, 0:u0]
    for jb in range(D_SSM // 128):
        for b in range(BATCH):
            u_s[jb, b * PITCH:b * PITCH + CHUNK, :] = proj[b * CHUNK:(b + 1) * CHUNK, u0 + 128 * jb:u0 + 128 * (jb + 1)]

    lo = lax.broadcasted_iota(jnp.int32, (CHUNK, 128), 1) < HEAD_DIM
    hi = jnp.logical_not(lo)
    table = jnp.minimum(c, 1)
    hrow = lax.broadcasted_iota(jnp.int32, (4 * CHUNK, 1), 0) // CHUNK

    def sink_col(heads):
        col = jnp.full((4 * CHUNK, 1), sinks_ref[heads[3]], F32)
        for n in (2, 1, 0):
            col = jnp.where(hrow == n, sinks_ref[heads[n]], col)
        return col

    sink_nat = sink_col(NAT_HEADS)
    sink_rot = sink_col(ROT_HEADS)

    def attend(q, k, v, bias, sink):
        s = lax.dot_general(q, k, (((1,), (1,)), ((), ())), preferred_element_type=F32) + bias
        m = jnp.maximum(jnp.max(s, axis=-1, keepdims=True), sink)
        p = jnp.exp(s - m)
        den = jnp.sum(p, axis=-1, keepdims=True) + jnp.exp(sink - m)
        return jnp.dot(p.astype(BF16), v, preferred_element_type=F32) / den

    def attn_body(b, carry):
        r0 = pl.multiple_of(b * CHUNK, CHUNK)
        rows = pl.ds(r0, CHUNK)
        kb = proj_s[rows, D_ATTN:D_ATTN + D_KV]
        vb = proj_s[rows, D_ATTN + D_KV:D_ATTN + 2 * D_KV]
        kb16 = kb.astype(BF16)
        vb16 = vb.astype(BF16)
        kbr16 = pltpu.roll(kb, HEAD_DIM, 1).astype(BF16)
        vbr16 = pltpu.roll(vb, HEAD_DIM, 1).astype(BF16)
        k_nat = jnp.concatenate([kprev_s[b], kb16], axis=0)
        k_rot = jnp.concatenate([kprevr_s[b], kbr16], axis=0)
        v_nat = jnp.concatenate([vprev_s[b], vb16], axis=0)
        v_rot = jnp.concatenate([vprevr_s[b], vbr16], axis=0)
        q2 = [proj_s[rows, 128 * jq:128 * (jq + 1)] * (HEAD_DIM ** -0.5) for jq in range(N_HEADS // 2)]
        q_nat = jnp.concatenate([jnp.where(lo if h % 2 == 0 else hi, q2[h // 2], 0.0) for h in NAT_HEADS],
                                axis=0).astype(BF16)
        q_rot = jnp.concatenate([jnp.where(lo if h % 2 == 0 else hi, q2[h // 2], 0.0) for h in ROT_HEADS],
                                axis=0).astype(BF16)
        o_nat = attend(q_nat, k_nat, v_nat, bias_ref[table, 0], sink_nat)
        o_rot = attend(q_rot, k_rot, v_rot, bias_ref[table, 1], sink_rot)
        for jq in range(N_HEADS // 2):
            blk = slice(CHUNK * jq, CHUNK * (jq + 1))
            even, odd = (o_nat, o_rot) if jq < 2 else (o_rot, o_nat)
            attn_s[rows, 128 * jq:128 * (jq + 1)] = jnp.where(lo, even[blk], odd[blk])
        kprev_s[b] = kb16
        kprevr_s[b] = kbr16
        vprev_s[b] = vb16
        vprevr_s[b] = vbr16
        return carry

    def attn_group(i, carry):
        for n in range(ATTN_UNROLL):
            attn_body(ATTN_UNROLL * i + n, carry)
        return carry

    lax.fori_loop(0, BATCH // ATTN_UNROLL, attn_group, 0)

    for sc in range(CHUNK // SUB_T):
        t0 = sc * SUB_T
        for i in range(SUB_T):
            for jb in range(D_SSM // 128):
                utb_s[i * BATCH:(i + 1) * BATCH, 128 * jb:128 * (jb + 1)] = (
                    u_s[jb, pl.ds(t0 + i, BATCH, stride=PITCH), :])
        u_tb = utb_s[...]
        for j in range(N_SBLK):
            bu_s[...] = _bdot(u_tb[:, 128 * j:128 * (j + 1)], bblk_ref[j])
            lr = jnp.broadcast_to(lamr_ref[j], (BATCH, SBLK))
            li = jnp.broadcast_to(lami_ref[j], (BATCH, SBLK))
            cols = slice(SBLK * j, SBLK * (j + 1))

            def step(i, carry):
                sr, si = carry
                r = pl.ds(pl.multiple_of(i * BATCH, BATCH), BATCH)
                nr = lr * sr - li * si + bu_s[r, 0:SBLK]
                ni = lr * si + li * sr + bu_s[r, SBLK:2 * SBLK]
                xs_s[r, 0:SBLK] = nr
                xs_s[r, SBLK:2 * SBLK] = ni
                return nr, ni

            sr, si = lax.fori_loop(0, SUB_T, step, (sre_ref[:, cols], sim_ref[:, cols]), unroll=True)
            sre_ref[:, cols] = sr
            sim_ref[:, cols] = si
            ytb_s[:, 128 * j:128 * (j + 1)] = _bdot(xs_s[...], cblk_ref[j])
        yn = _ssm_post(ytb_s[...], u_tb, dskip_ref[...], wglu_ref[...], bglu_ref[...], gs_ref[...])
        for i in range(SUB_T):
            for jb in range(D_SSM // 128):
                ssm_s[jb, pl.ds(t0 + i, BATCH, stride=PITCH), :] = (
                    yn[i * BATCH:(i + 1) * BATCH, 128 * jb:128 * (jb + 1)])

    an = _rms(attn_s[...], ga_ref[...])
    sn = jnp.concatenate(
        [jnp.concatenate([ssm_s[jb, b * PITCH:b * PITCH + CHUNK, :] for b in range(BATCH)], axis=0)
         for jb in range(D_SSM // 128)], axis=1)
    h = x + _bdot(jnp.concatenate([an.astype(BF16), sn.astype(BF16)], axis=1), wout_ref[...])
    hmid_ref[...] = h
    _route(h, gffn_ref, wrt_ref, br_ref, tri_ref, cnt_ref, hn_ref, idx_ref, gw_ref, rank_ref)

    @pl.when(c == N_CHUNKS - 1)
    def _():
        klast_ref[...] = proj_s[:, D_ATTN:D_ATTN + D_KV].reshape(BATCH, CHUNK, D_KV)
        vlast_ref[...] = proj_s[:, D_ATTN + D_KV:D_ATTN + 2 * D_KV].reshape(BATCH, CHUNK, D_KV)


def _mixer_prompt(x_prompt, sinks, gmix, win, bias, bblk, lamr, lami, cblk, dskip, wglu, bglu, ga, gs, wout,
                  gffn, wrt, br):
    smem = pl.BlockSpec(memory_space=pltpu.SMEM)
    in_specs = [
        smem,
        pl.BlockSpec((BATCH, CHUNK, D_MODEL), lambda c: (0, c, 0)),
        _full((1, D_MODEL)), _full((D_MODEL, D_IN)), _full((2, 2, 4 * CHUNK, 2 * CHUNK)),
        _full((N_SBLK, 128, 2 * SBLK)), _full((N_SBLK, 1, SBLK)), _full((N_SBLK, 1, SBLK)),
        _full((N_SBLK, 2 * SBLK, 128)), _full((1, D_SSM)), _full((D_SSM, D_SSM)), _full((1, D_SSM)),
        _full((1, D_ATTN)), _full((1, D_SSM)), _full((D_MODEL, D_MODEL)),
        _full((1, D_MODEL)), _full((N_EXPERTS, D_MODEL)), _full((N_EXPERTS, 1)), _full((ROUTE_BLOCK, ROUTE_BLOCK)),
    ]
    route_specs, route_shapes = _route_specs(ROWS, lambda c: c)
    out_shape = [
        jax.ShapeDtypeStruct((T_PAD, D_MODEL), F32),
        jax.ShapeDtypeStruct((BATCH, CHUNK, D_KV), F32),
        jax.ShapeDtypeStruct((BATCH, CHUNK, D_KV), F32),
        jax.ShapeDtypeStruct((BATCH, SSM_G * SSM_P), F32),
        jax.ShapeDtypeStruct((BATCH, SSM_G * SSM_P), F32),
    ] + route_shapes
    out_specs = [
        pl.BlockSpec((ROWS, D_MODEL), lambda c: (c, 0)),
        _full((BATCH, CHUNK, D_KV)), _full((BATCH, CHUNK, D_KV)),
        _full((BATCH, SSM_G * SSM_P)), _full((BATCH, SSM_G * SSM_P)),
    ] + route_specs
    kv_scr = pltpu.VMEM((BATCH, CHUNK, D_KV), BF16)
    scratch = [
        pltpu.VMEM((ROWS, D_ATTN + 2 * D_KV), F32), pltpu.VMEM((D_SSM // 128, BATCH * PITCH, 128), F32),
        kv_scr, kv_scr, kv_scr, kv_scr,
        pltpu.VMEM((ROWS, D_ATTN), F32), pltpu.VMEM((D_SSM // 128, BATCH * PITCH, 128), F32),
        pltpu.VMEM((SUB_ROWS, D_SSM), F32), pltpu.VMEM((SUB_ROWS, 2 * SBLK), F32),
        pltpu.VMEM((SUB_ROWS, 2 * SBLK), F32), pltpu.VMEM((SUB_ROWS, D_SSM), F32),
    ]
    return pl.pallas_call(
        _mixer_prompt_kernel, grid=(N_CHUNKS,), in_specs=in_specs, out_specs=out_specs, out_shape=out_shape,
        scratch_shapes=scratch, name="mixer_prompt",
        compiler_params=pltpu.CompilerParams(dimension_semantics=("arbitrary",), vmem_limit_bytes=VMEM_LIMIT),
    )(sinks, x_prompt, gmix, win, bias, bblk, lamr, lami, cblk, dskip, wglu, bglu, ga, gs, wout,
      gffn, wrt, br, _tri(ROUTE_BLOCK))


SGRP = 16
SPITCH = DEC_BATCH + 8
N_SGRP = DEC_BATCH // SGRP


def _mixer_sample_kernel(x_ref, gmix_ref, win_ref, ck_ref, cv_ref, sbias_ref, sinkc_ref, bblk_ref, lamr_ref,
                         lami_ref, cblk_ref, dskip_ref, wglu_ref, bglu_ref, ga_ref, gs_ref, wout_ref,
                         x0r_ref, x0i_ref, gffn_ref, wrt_ref, br_ref, tri_ref, cnt_in_ref,
                         hbuf_ref, hnbuf_ref, idxbuf_ref, gwbuf_ref, rankbuf_ref,
                         hmid_ref, knew_ref, vnew_ref, sre_ref, sim_ref, hn_ref, idx_ref, gw_ref, rank_ref, cnt_ref,
                         proj_s, qall_s, oall_s, kn8_s, vn8_s):
    del hbuf_ref, hnbuf_ref, idxbuf_ref, gwbuf_ref, rankbuf_ref
    g = pl.program_id(0)
    lo = lax.broadcasted_iota(jnp.int32, (DEC_BATCH, 128), 1) < HEAD_DIM

    @pl.when(g == 0)
    def _():
        proj = _bdot(_rms(x_ref[...], gmix_ref[...]), win_ref[...])
        proj_s[...] = proj
        for h in range(N_HEADS):
            jq, half, kv = h // 2, h % 2, h // 4
            q2 = proj[:, 128 * jq:128 * (jq + 1)] * (HEAD_DIM ** -0.5)
            if half != kv:
                q2 = pltpu.roll(q2, HEAD_DIM, 1)
            hr = slice(h * SPITCH, h * SPITCH + DEC_BATCH)
            qall_s[hr, :] = jnp.where(lo if kv == 0 else jnp.logical_not(lo), q2, 0.0)
            kn8_s[hr, :] = proj[:, D_ATTN:D_ATTN + D_KV]
            vn8_s[hr, :] = proj[:, D_ATTN + D_KV:D_ATTN + 2 * D_KV]

    def head_rows(ref):
        return jnp.stack([ref[pl.ds(g * SGRP + ii, N_HEADS, stride=SPITCH), :] for ii in range(SGRP)], axis=0)

    qe = head_rows(qall_s)
    knew = head_rows(kn8_s)
    vnew = head_rows(vn8_s)
    s = jnp.einsum('bhd,bkd->bhk', qe.astype(BF16), ck_ref[...].astype(BF16),
                   preferred_element_type=F32) + sbias_ref[...][None]
    s_new = jnp.sum(qe * knew, axis=-1, keepdims=True)
    sink = sinkc_ref[...][None]
    m = jnp.maximum(jnp.maximum(jnp.max(s, axis=-1, keepdims=True), s_new), sink)
    p = jnp.exp(s - m)
    p_new = jnp.exp(s_new - m)
    den = jnp.sum(p, axis=-1, keepdims=True) + p_new + jnp.exp(sink - m)
    o = (jnp.einsum('bhk,bkd->bhd', p.astype(BF16), cv_ref[...].astype(BF16),
                    preferred_element_type=F32) + p_new * vnew) / den
    for ii in range(SGRP):
        oall_s[pl.ds(g * SGRP + ii, N_HEADS, stride=SPITCH), :] = o[ii]

    @pl.when(g == N_SGRP - 1)
    def _():
        proj = proj_s[...]
        blocks = []
        for jq in range(N_HEADS // 2):
            oa = oall_s[(2 * jq) * SPITCH:(2 * jq) * SPITCH + DEC_BATCH, :]
            ob = oall_s[(2 * jq + 1) * SPITCH:(2 * jq + 1) * SPITCH + DEC_BATCH, :]
            if jq // 2 == 0:
                blocks.append(jnp.where(lo, oa, pltpu.roll(ob, HEAD_DIM, 1)))
            else:
                blocks.append(jnp.where(lo, pltpu.roll(oa, HEAD_DIM, 1), ob))
        attn = jnp.concatenate(blocks, axis=1)
        u = proj[:, D_ATTN + 2 * D_KV:]
        ys = []
        for j in range(N_SBLK):
            bu = _bdot(u[:, 128 * j:128 * (j + 1)], bblk_ref[j])
            lr = lamr_ref[j]
            li = lami_ref[j]
            cols = slice(SBLK * j, SBLK * (j + 1))
            sr = x0r_ref[:, cols]
            si = x0i_ref[:, cols]
            nr = lr * sr - li * si + bu[:, 0:SBLK]
            ni = lr * si + li * sr + bu[:, SBLK:2 * SBLK]
            sre_ref[:, cols] = nr
            sim_ref[:, cols] = ni
            ys.append(_bdot(jnp.concatenate([nr, ni], axis=1), cblk_ref[j]))
        sn = _ssm_post(jnp.concatenate(ys, axis=1), u, dskip_ref[...], wglu_ref[...], bglu_ref[...], gs_ref[...])
        an = _rms(attn, ga_ref[...])
        h = x_ref[...] + _bdot(jnp.concatenate([an.astype(BF16), sn.astype(BF16)], axis=1), wout_ref[...])
        h = jnp.concatenate([h, jnp.zeros((DEC_BATCH, D_MODEL), F32)], axis=0)
        hmid_ref[...] = h
        cnt_ref[...] = cnt_in_ref[...]
        _route(h, gffn_ref, wrt_ref, br_ref, tri_ref, cnt_ref, hn_ref, idx_ref, gw_ref, rank_ref)
        knew_ref[...] = proj[:, D_ATTN:D_ATTN + D_KV]
        vnew_ref[...] = proj[:, D_ATTN + D_KV:D_ATTN + 2 * D_KV]


def _mixer_sample(x_s, gmix, win, ck, cv, sbias, sinkc, bblk, lamr, lami, cblk, dskip, wglu, bglu, ga, gs, wout,
                  x0r, x0i, gffn, wrt, br, cnt_in, hbuf, hnbuf, idxbuf, gwbuf, rankbuf):
    nst = SSM_G * SSM_P
    in_specs = [
        _full((DEC_BATCH, D_MODEL)), _full((1, D_MODEL)), _full((D_MODEL, D_IN)),
        pl.BlockSpec((SGRP, WINDOW, D_KV), lambda g: (g, 0, 0)),
        pl.BlockSpec((SGRP, WINDOW, D_KV), lambda g: (g, 0, 0)),
        _full((N_HEADS, WINDOW)), _full((N_HEADS, 1)),
        _full((N_SBLK, 128, 2 * SBLK)), _full((N_SBLK, 1, SBLK)), _full((N_SBLK, 1, SBLK)),
        _full((N_SBLK, 2 * SBLK, 128)), _full((1, D_SSM)), _full((D_SSM, D_SSM)), _full((1, D_SSM)),
        _full((1, D_ATTN)), _full((1, D_SSM)), _full((D_MODEL, D_MODEL)),
        _full((DEC_BATCH, nst)), _full((DEC_BATCH, nst)),
        _full((1, D_MODEL)), _full((N_EXPERTS, D_MODEL)), _full((N_EXPERTS, 1)),
        _full((2 * DEC_BATCH, 2 * DEC_BATCH)), _full((N_EXPERTS, 1)),
    ] + [pl.BlockSpec(memory_space=pl.ANY)] * 5
    tail_block = SAMPLE_ROW0 // (2 * DEC_BATCH)
    route_specs, route_shapes = _route_specs(2 * DEC_BATCH, lambda g: tail_block)
    out_shape = [
        jax.ShapeDtypeStruct((T_PAD, D_MODEL), F32),
        jax.ShapeDtypeStruct((DEC_BATCH, D_KV), F32), jax.ShapeDtypeStruct((DEC_BATCH, D_KV), F32),
        jax.ShapeDtypeStruct((DEC_BATCH, nst), F32), jax.ShapeDtypeStruct((DEC_BATCH, nst), F32),
    ] + route_shapes
    out_specs = [
        pl.BlockSpec((2 * DEC_BATCH, D_MODEL), lambda g: (tail_block, 0)),
        _full((DEC_BATCH, D_KV)), _full((DEC_BATCH, D_KV)),
        _full((DEC_BATCH, nst)), _full((DEC_BATCH, nst)),
    ] + route_specs
    head_rows = pltpu.VMEM((N_HEADS * SPITCH, 128), F32)
    scratch = [pltpu.VMEM((DEC_BATCH, D_IN), F32), head_rows, head_rows, head_rows, head_rows]
    return pl.pallas_call(
        _mixer_sample_kernel, grid=(N_SGRP,), in_specs=in_specs, out_specs=out_specs, out_shape=out_shape,
        scratch_shapes=scratch, input_output_aliases={24: 0, 25: 5, 26: 6, 27: 7, 28: 8}, name="mixer_sample",
        compiler_params=pltpu.CompilerParams(dimension_semantics=("arbitrary",), vmem_limit_bytes=VMEM_LIMIT),
    )(x_s, gmix, win, ck, cv, sbias, sinkc, bblk, lamr, lami, cblk, dskip, wglu, bglu, ga, gs, wout, x0r, x0i,
      gffn, wrt, br, _tri(2 * DEC_BATCH), cnt_in, hbuf, hnbuf, idxbuf, gwbuf, rankbuf)


def _route(h, g_ref, wrt_ref, br_ref, tri_ref, cnt_ref, hn_ref, idx_ref, gw_ref, rank_ref):
    hn = _rms(h, g_ref[...])
    hn_ref[...] = _pack_pairs(hn)
    hn_hi = hn.astype(BF16)
    hn_lo = (hn - hn_hi.astype(F32)).astype(BF16)
    w = wrt_ref[...]
    w_hi = w.astype(BF16)
    w_lo = (w - w_hi.astype(F32)).astype(BF16)
    nt = (((1,), (1,)), ((), ()))
    both = lax.dot_general(jnp.concatenate([w_hi, w_lo], axis=0), hn_hi, nt, preferred_element_type=F32)
    logits = (both[:N_EXPERTS] + both[N_EXPERTS:]
              + lax.dot_general(w_hi, hn_lo, nt, preferred_element_type=F32)) + br_ref[...]
    eidx = lax.broadcasted_iota(jnp.int32, logits.shape, 0)
    vals, onehots = [], []
    l = logits
    for k in range(TOP_K):
        m = jnp.max(l, axis=0, keepdims=True)
        ik = jnp.min(jnp.where(l == m, eidx, N_EXPERTS), axis=0, keepdims=True)
        oh = eidx == ik
        idx_ref[k:k + 1, :] = ik
        vals.append(m)
        onehots.append(oh)
        l = jnp.where(oh, -jnp.inf, l)
    exps = [jnp.exp(v - vals[0]) for v in vals]
    den = exps[0] + exps[1] + exps[2] + exps[3]
    for k in range(TOP_K):
        gw_ref[k:k + 1, :] = exps[k] / den
    member = jnp.zeros(logits.shape, F32)
    for oh in onehots:
        member = member + jnp.where(oh, 1.0, 0.0)
    wblk = tri_ref.shape[0]
    base = cnt_ref[...].astype(F32)
    befores = []
    for cb in range(h.shape[0] // wblk):
        mblk = member[:, cb * wblk:(cb + 1) * wblk]
        befores.append(jnp.dot(mblk.astype(BF16), tri_ref[...], preferred_element_type=F32) + base)
        base = base + jnp.sum(mblk, axis=1, keepdims=True)
    before = jnp.concatenate(befores, axis=1)
    for k in range(TOP_K):
        rank_ref[k:k + 1, :] = jnp.sum(jnp.where(onehots[k], before, 0.0), axis=0, keepdims=True).astype(jnp.int32)
    cnt_ref[...] = base.astype(jnp.int32)


def _route_specs(rows, block):
    specs = [pl.BlockSpec((rows, D_PACK), lambda i: (block(i), 0)), pl.BlockSpec((TOP_K, rows), lambda i: (0, block(i))),
             pl.BlockSpec((TOP_K, rows), lambda i: (0, block(i))), pl.BlockSpec((TOP_K, rows), lambda i: (0, block(i))),
             _full((N_EXPERTS, 1))]
    shapes = [jax.ShapeDtypeStruct((T_PAD, D_PACK), jnp.int32), jax.ShapeDtypeStruct((TOP_K, T_PAD), jnp.int32),
              jax.ShapeDtypeStruct((TOP_K, T_PAD), F32), jax.ShapeDtypeStruct((TOP_K, T_PAD), jnp.int32),
              jax.ShapeDtypeStruct((N_EXPERTS, 1), jnp.int32)]
    return specs, shapes


def _tri(n):
    return jnp.asarray(np.triu(np.ones((n, n), np.float32), 1), BF16)


def _place_kernel(off_ref, idx_ref, rank_ref, pos_ref):
    idx = idx_ref[...]
    pos = rank_ref[...]
    for e in range(N_EXPERTS):
        pos = pos + jnp.where(idx == e, off_ref[e], 0)
    pos_ref[...] = pos


def _place(offsets, idx, rank):
    return pl.pallas_call(
        _place_kernel,
        in_specs=[pl.BlockSpec(memory_space=pltpu.SMEM), pl.BlockSpec(memory_space=pltpu.VMEM),
                  pl.BlockSpec(memory_space=pltpu.VMEM)],
        out_specs=pl.BlockSpec(memory_space=pltpu.VMEM),
        out_shape=jax.ShapeDtypeStruct((TOP_K, T_PAD), jnp.int32), name="place",
    )(offsets, idx, rank)


def _sc_mesh():
    return plsc.VectorSubcoreMesh(core_axis_name="core", subcore_axis_name="subcore")


def _sc_dispatch(rows, pos):
    n, d = rows.shape
    nblk = n // SC_ROWS
    pos_w = pos.reshape(TOP_K, nblk, SC_ROWS).transpose(1, 0, 2)

    @functools.partial(pl.kernel, out_type=jax.ShapeDtypeStruct((P_ROWS, d), rows.dtype), mesh=_sc_mesh(),
                       scratch_types=[], name="dispatch")
    def run(x_hbm, i_hbm, o_hbm):
        def body(x_vmem, i_vmem):
            for k in range(TOP_K):
                pltpu.sync_copy(x_vmem, o_hbm.at[i_vmem.at[0, k]])

        pltpu.emit_pipeline(
            body, grid=(nblk,),
            in_specs=[pl.BlockSpec((SC_ROWS, d), lambda i: (i, 0)),
                      pl.BlockSpec((1, TOP_K, SC_ROWS), lambda i: (i, 0, 0))],
            out_specs=[], core_axis_name=("core", "subcore"), dimension_semantics=(pltpu.PARALLEL,),
        )(x_hbm, i_hbm)

    return run(rows, pos_w)


def _sc_combine(rows, flat_pos):
    _, d = rows.shape
    n = flat_pos.shape[0]
    w = SC_ROWS_COMBINE
    nblk = n // w
    pos_w = flat_pos.reshape(nblk, 1, w)

    @functools.partial(pl.kernel, out_type=jax.ShapeDtypeStruct((n, d), rows.dtype), mesh=_sc_mesh(),
                       scratch_types=[], name="combine")
    def run(x_hbm, i_hbm, o_hbm):
        def body(i_vmem, o_vmem):
            pltpu.sync_copy(x_hbm.at[i_vmem.at[0, 0]], o_vmem)

        pltpu.emit_pipeline(
            body, grid=(nblk,),
            in_specs=[pl.BlockSpec((1, 1, w), lambda i: (i, 0, 0))],
            out_specs=[pl.BlockSpec((w, d), lambda i: (i, 0))],
            core_axis_name=("core", "subcore"), dimension_semantics=(pltpu.PARALLEL,),
        )(i_hbm, o_hbm)

    return run(rows, pos_w)


def _experts_kernel(ts_ref, trow_ref, tsz_ref, xs_hbm, w1_hbm, b1_ref, w2_hbm, b2_ref, ys_hbm,
                    w1_st, w2_st, w1_s, w2_s, xbuf, ybuf, wsem, xsem, ysem):
    e = pl.program_id(0)
    n_valid = ts_ref[N_EXPERTS]

    def w_copies(ex, slot):
        return (pltpu.make_async_copy(w1_hbm.at[ex], w1_st.at[slot], wsem.at[0, slot]),
                pltpu.make_async_copy(w2_hbm.at[ex], w2_st.at[slot], wsem.at[1, slot]))

    def x_copy(t, slot, rows):
        src = xs_hbm.at[pl.ds(pl.multiple_of(trow_ref[t], TM_UNIT), rows)]
        return pltpu.make_async_copy(src, xbuf.at[slot, pl.ds(0, rows)], xsem.at[slot])

    def y_copy(t, slot, rows):
        dst = ys_hbm.at[pl.ds(pl.multiple_of(trow_ref[t], TM_UNIT), rows)]
        return pltpu.make_async_copy(ybuf.at[slot, pl.ds(0, rows)], dst, ysem.at[slot])

    def by_size(t, fn):
        for rows in TILE_SIZES:
            @pl.when(tsz_ref[t] == rows)
            def _():
                fn(rows)

    @pl.when(e == 0)
    def _():
        for c in w_copies(0, 0):
            c.start()
        by_size(0, lambda rows: x_copy(0, 0, rows).start())

    @pl.when(e + 1 < N_EXPERTS)
    def _():
        for c in w_copies(e + 1, (e + 1) % 2):
            c.start()

    wslot = e % 2
    for c in w_copies(e, wslot):
        c.wait()
    for r in range(4):
        rs = slice(256 * r, 256 * (r + 1))
        w1_s[rs, :] = w1_st[wslot, rs, :].astype(BF16)
        w2_s[rs, :] = w2_st[wslot, rs, :].astype(BF16)

    def tile(t, carry):
        slot = t % 2

        @pl.when(t + 1 < n_valid)
        def _():
            by_size(t + 1, lambda rows: x_copy(t + 1, 1 - slot, rows).start())

        by_size(t, lambda rows: x_copy(t, slot, rows).wait())

        @pl.when(t >= 2)
        def _():
            by_size(t - 2, lambda rows: y_copy(t - 2, slot, rows).wait())

        def compute(rows):
            hdn = _bdot(_unpack_pairs(xbuf[slot, 0:rows]), w1_s[...]) + b1_ref[0]
            gt = jnp.minimum(hdn[:, :D_FF], SWIGLU_LIMIT)
            up = jnp.clip(hdn[:, D_FF:], -SWIGLU_LIMIT, SWIGLU_LIMIT)
            act = (up + 1.0) * gt * _sigmoid(SWIGLU_ALPHA * gt)
            ybuf[slot, 0:rows] = _pack_pairs(_bdot(act, w2_s[...]) + b2_ref[0])
            y_copy(t, slot, rows).start()

        by_size(t, compute)
        return carry

    lax.fori_loop(ts_ref[e], ts_ref[e + 1], tile, 0)

    @pl.when(e == N_EXPERTS - 1)
    def _():
        @pl.when(n_valid >= 2)
        def _():
            by_size(n_valid - 2, lambda rows: y_copy(n_valid - 2, n_valid % 2, rows).wait())

        by_size(n_valid - 1, lambda rows: y_copy(n_valid - 1, (n_valid - 1) % 2, rows).wait())


def _tile_plan(counts):
    units = (counts + (TM_UNIT - 1)) // TM_UNIT
    unit_end = jnp.cumsum(units)
    offsets = ((unit_end - units) * TM_UNIT).astype(jnp.int32)
    per_full = TM // TM_UNIT
    n_full = units // per_full
    rest = units % per_full
    has_mid = (rest >= 2).astype(jnp.int32)
    tiles_per = n_full + has_mid + rest % 2
    tile_end = jnp.cumsum(tiles_per)
    tile_start = jnp.concatenate([jnp.zeros((1,), jnp.int32), tile_end.astype(jnp.int32)])
    t = jnp.arange(MAX_TILES, dtype=jnp.int32)[:, None]
    done = tile_end[None, :] <= t
    mine = jnp.sum(done.astype(jnp.int32), axis=1, keepdims=True) == jnp.arange(N_EXPERTS, dtype=jnp.int32)[None, :]
    pick = lambda v: jnp.sum(jnp.where(mine, v[None, :], 0), axis=1)
    j = t[:, 0] - jnp.max(jnp.where(done, tile_end[None, :], 0), axis=1)
    nf, mid = pick(n_full), pick(has_mid)
    rows = jnp.where(j < nf, TM, jnp.where((j == nf) & (mid == 1), TILE_SIZES[1], TILE_SIZES[2]))
    row0 = pick(offsets) + jnp.minimum(j, nf) * TM + jnp.where(j > nf, TILE_SIZES[1], 0)
    valid = t[:, 0] < tile_end[-1]
    tile_row = jnp.where(valid, row0, 0).astype(jnp.int32)
    tile_rows = jnp.where(valid, rows, 0).astype(jnp.int32)
    return offsets, tile_start, tile_row, tile_rows


def _experts(tile_start, tile_row, tile_rows, xs, w_up, b_up, w_down, b_down):
    wsel = lambda e, *_: (e, 0, 0)
    hbm = pl.BlockSpec(memory_space=pl.ANY)
    grid_spec = pltpu.PrefetchScalarGridSpec(
        num_scalar_prefetch=3, grid=(N_EXPERTS,),
        in_specs=[hbm, hbm, pl.BlockSpec((1, 1, 2 * D_FF), wsel), hbm, pl.BlockSpec((1, 1, D_MODEL), wsel)],
        out_specs=hbm,
        scratch_shapes=[pltpu.VMEM((2, D_MODEL, 2 * D_FF), F32), pltpu.VMEM((2, D_FF, D_MODEL), F32),
                        pltpu.VMEM((D_MODEL, 2 * D_FF), BF16), pltpu.VMEM((D_FF, D_MODEL), BF16),
                        pltpu.VMEM((2, TM, D_PACK), jnp.int32), pltpu.VMEM((2, TM, D_PACK), jnp.int32),
                        pltpu.SemaphoreType.DMA((2, 2)), pltpu.SemaphoreType.DMA((2,)),
                        pltpu.SemaphoreType.DMA((2,))],
    )
    return pl.pallas_call(
        _experts_kernel, grid_spec=grid_spec, out_shape=jax.ShapeDtypeStruct((P_ROWS, D_PACK), jnp.int32),
        name="experts",
        compiler_params=pltpu.CompilerParams(dimension_semantics=("arbitrary",), vmem_limit_bytes=VMEM_LIMIT),
    )(tile_start, tile_row, tile_rows, xs, w_up, b_up.reshape(N_EXPERTS, 1, 2 * D_FF), w_down,
      b_down.reshape(N_EXPERTS, 1, D_MODEL))


def _final_kernel(h_ref, yg_ref, gw_ref, p_ref, gple_ref, wg_ref, wp_ref, gfin_ref, *rest):
    out_ref = rest[-1]
    rows = h_ref.shape[0]
    gw = gw_ref[...]
    h = h_ref[...]
    for k in range(TOP_K):
        h = h + gw[:, k:k + 1] * _unpack_pairs(yg_ref[k])
    gate = _sigmoid(_bdot(_rms(h, gple_ref[...]), wg_ref[...]))
    h = h + gate * _bdot(p_ref[...].reshape(rows, PLE_DIM), wp_ref[...])
    out_ref[...] = _rms(h, gfin_ref[...]).reshape(out_ref.shape)


def _final_prompt(part, hmid, yg, gwt, p_prompt, gple, wg, wp, gfin, y_prev):
    nb = BATCH
    rows = nb * CHUNK
    nbh = BATCH // nb
    c0 = part * PART_CHUNKS
    rb = lambda c, b: ((c0 + c) * nbh + b, 0)
    in_specs = [pl.BlockSpec((rows, D_MODEL), rb),
                pl.BlockSpec((TOP_K, rows, D_PACK), lambda c, b: (0, c * nbh + b, 0)),
                pl.BlockSpec((rows, TOP_K), rb),
                pl.BlockSpec((nb, CHUNK, PLE_DIM), lambda c, b: (b, c0 + c, 0)),
                _full((1, D_MODEL)), _full((D_MODEL, D_MODEL)), _full((PLE_DIM, D_MODEL)), _full((1, D_MODEL))]
    args = [hmid, yg, gwt, p_prompt, gple, wg, wp, gfin]
    aliases = {}
    if y_prev is not None:
        in_specs.append(pl.BlockSpec(memory_space=pl.ANY))
        args.append(y_prev)
        aliases = {len(args) - 1: 0}
    return pl.pallas_call(
        _final_kernel, grid=(PART_CHUNKS, nbh), in_specs=in_specs,
        out_specs=pl.BlockSpec((nb, CHUNK, D_MODEL), lambda c, b: (b, c0 + c, 0)),
        out_shape=jax.ShapeDtypeStruct((BATCH, SEQ, D_MODEL), F32), name="final_prompt",
        input_output_aliases=aliases,
        compiler_params=pltpu.CompilerParams(dimension_semantics=("arbitrary", "arbitrary"),
                                             vmem_limit_bytes=VMEM_LIMIT),
    )(*args)


def _final_sample(hmid, yg, gwt, p_sample, gple, wg, wp, gfin):
    blk = SAMPLE_ROW0 // DEC_BATCH
    return pl.pallas_call(
        _final_kernel, grid=(1,),
        in_specs=[pl.BlockSpec((DEC_BATCH, D_MODEL), lambda i: (blk, 0)),
                  pl.BlockSpec((TOP_K, DEC_BATCH, D_PACK), lambda i: (0, PART_CHUNKS * ROWS // DEC_BATCH, 0)),
                  pl.BlockSpec((DEC_BATCH, TOP_K), lambda i: (blk, 0)),
                  _full((DEC_BATCH, PLE_DIM)),
                  _full((1, D_MODEL)), _full((D_MODEL, D_MODEL)), _full((PLE_DIM, D_MODEL)), _full((1, D_MODEL))],
        out_specs=_full((DEC_BATCH, D_MODEL)),
        out_shape=jax.ShapeDtypeStruct((DEC_BATCH, D_MODEL), F32), name="final_sample",
        compiler_params=pltpu.CompilerParams(dimension_semantics=("arbitrary",), vmem_limit_bytes=VMEM_LIMIT),
    )(hmid, yg, gwt, p_sample, gple, wg, wp, gfin)


def _alibi_tables():
    slopes = 2.0 ** (-8.0 * (np.arange(N_HEADS, dtype=np.float64) + 1.0) / N_HEADS)
    i = np.arange(CHUNK)[:, None]
    j = np.arange(2 * CHUNK)[None, :]
    dist = i + CHUNK - j
    valid = (dist >= 0) & (dist <= WINDOW)
    tabs = []
    for has_prev in (False, True):
        ok = valid & ((j >= CHUNK) | has_prev)
        tabs.append(np.where(ok[None], -slopes[:, None, None] * dist[None], NEG))
    prompt = np.stack([np.stack([np.concatenate([t[h] for h in grp], axis=0) for grp in (NAT_HEADS, ROT_HEADS)])
                       for t in tabs]).astype(np.float32)
    wb = min(WINDOW, PAST_LEN)
    sample = (-slopes[:, None] * (wb - np.arange(wb))[None, :]).astype(np.float32)
    return prompt, sample


def kernel(x_prompt, x_sample, cache_k_win, cache_v_win, state_ssm_re, state_ssm_im, p_prompt, p_sample, norm_mix, w_in, sinks, ssm_lam_re, ssm_lam_im, ssm_log_step, ssm_b_re, ssm_b_im, ssm_c_re, ssm_c_im, ssm_d, w_glu, b_glu, norm_attn_out, norm_ssm_out, w_out, norm_ffn, w_router, b_router, w_up, b_up, w_down, b_down, norm_ple, w_ple_gate, w_ple_proj, norm_final):
    nst = SSM_G * SSM_P
    bias_np, sbias_np = _alibi_tables()
    bias = jnp.asarray(bias_np)
    sbias = jnp.asarray(sbias_np)

    lbr, lbi, bbr, bbi = _prep(ssm_lam_re[0], ssm_lam_im[0], ssm_log_step[0], ssm_b_re[0], ssm_b_im[0])
    bblk, lamr, lami, cblk = _s5_blocks(lbr, lbi, bbr, bbi, ssm_c_re[0], ssm_c_im[0])

    gmix = norm_mix[0].reshape(1, D_MODEL)
    win = w_in[0].astype(BF16)
    dskip = ssm_d[0].reshape(1, D_SSM)
    wglu = w_glu[0].astype(BF16)
    bglu = b_glu[0].reshape(1, D_SSM)
    ga = norm_attn_out[0].reshape(1, D_ATTN)
    gs = norm_ssm_out[0].reshape(1, D_SSM)
    wout = w_out[0].astype(BF16)
    sink = sinks[0]

    gffn = norm_ffn[0].reshape(1, D_MODEL)
    wrt = w_router[0].T
    br = b_router[0].reshape(N_EXPERTS, 1)
    hbuf, k_last, v_last, re_p, im_p, hnbuf, idxbuf, gwbuf, rankbuf, cnt_p = _mixer_prompt(
        x_prompt, sink, gmix, win, bias, bblk, lamr, lami, cblk, dskip, wglu, bglu, ga, gs, wout, gffn, wrt, br)

    ck = cache_k_win[0].reshape(DEC_BATCH, WINDOW, D_KV)
    cv = cache_v_win[0].reshape(DEC_BATCH, WINDOW, D_KV)
    hmid, k_new, v_new, re_s, im_s, hn, idx, gw, rank, counts = _mixer_sample(
        x_sample.reshape(DEC_BATCH, D_MODEL), gmix, win, ck, cv, sbias, sink.reshape(N_HEADS, 1), bblk, lamr, lami,
        cblk, dskip, wglu, bglu, ga, gs, wout, state_ssm_re[0].reshape(DEC_BATCH, nst),
        state_ssm_im[0].reshape(DEC_BATCH, nst), gffn, wrt, br, cnt_p, hbuf, hnbuf, idxbuf, gwbuf, rankbuf)

    offsets, tile_start, tile_row, tile_rows = _tile_plan(counts[:, 0])

    pos = _place(offsets, idx, rank)
    xs = _sc_dispatch(hn, pos)
    ys = _experts(tile_start, tile_row, tile_rows, xs, w_up[0], b_up[0], w_down[0], b_down[0])
    ygs = []
    for part in range(N_PARTS):
        r0 = part * PART_CHUNKS * ROWS
        r1 = T_PAD if part == N_PARTS - 1 else r0 + PART_CHUNKS * ROWS
        ygs.append(_sc_combine(ys, pos[:, r0:r1].reshape(TOP_K * (r1 - r0))).reshape(TOP_K, r1 - r0, D_PACK))

    gwt = gw.T
    gple = norm_ple[0].reshape(1, D_MODEL)
    wg = w_ple_gate[0].astype(BF16)
    wp = w_ple_proj[0].astype(BF16)
    gfin = norm_final.reshape(1, D_MODEL)
    y_prompt = None
    for part in range(N_PARTS):
        y_prompt = _final_prompt(part, hmid, ygs[part], gwt, p_prompt[0], gple, wg, wp, gfin, y_prompt)
    y_sample = _final_sample(hmid, ygs[-1], gwt, p_sample[0].reshape(DEC_BATCH, PLE_DIM), gple, wg, wp, gfin)

    k_win_s = jnp.concatenate([ck[:, 1:], k_new[:, None, :]], axis=1)
    v_win_s = jnp.concatenate([cv[:, 1:], v_new[:, None, :]], axis=1)
    kv5 = (1, BATCH, CHUNK, N_KV, HEAD_DIM)
    skv5 = (1, DEC_BATCH, WINDOW, N_KV, HEAD_DIM)
    return (y_prompt, y_sample.reshape(DEC_BATCH, 1, D_MODEL),
            k_last.reshape(kv5), v_last.reshape(kv5),
            re_p.reshape(1, BATCH, SSM_G, SSM_P), im_p.reshape(1, BATCH, SSM_G, SSM_P),
            k_win_s.reshape(skv5), v_win_s.reshape(skv5),
            re_s.reshape(1, DEC_BATCH, SSM_G, SSM_P), im_s.reshape(1, DEC_BATCH, SSM_G, SSM_P))
```

```python
import functools

import numpy as np
import jax
import jax.numpy as jnp
from jax import lax
from jax.experimental import pallas as pl
from jax.experimental.pallas import tpu as pltpu
from jax.experimental.pallas import tpu_sc as plsc

F32 = jnp.float32
BF16 = jnp.bfloat16

D_MODEL = 1024
BATCH = 8
SEQ = 2048
DEC_BATCH = 128
PAST_LEN = 16384
HEAD_DIM = 64
D_ATTN = 512
N_HEADS = 8
N_KV = 2
D_KV = N_KV * HEAD_DIM
WINDOW = 128
D_SSM = 512
SSM_H = 16
SSM_G = 32
SSM_P = 64
D_IN = D_ATTN + 2 * D_KV + D_SSM
N_EXPERTS = 32
TOP_K = 4
D_FF = 1024
SWIGLU_LIMIT = 7.0
SWIGLU_ALPHA = 1.702
PLE_DIM = 256
EPS = 1e-5
NEG = -1e30

CHUNK = 128
N_CHUNKS = SEQ // CHUNK
ROWS = BATCH * CHUNK
PITCH = CHUNK + 8
NAT_HEADS = (0, 2, 5, 7)
ROT_HEADS = (1, 3, 4, 6)
ATTN_UNROLL = 2
SUB_T = 64
SUB_ROWS = SUB_T * BATCH
N_SBLK = 4
SBLK = 512
T_REAL = BATCH * SEQ + DEC_BATCH
T_PAD = T_REAL + 128
SAMPLE_ROW0 = BATCH * SEQ
ROUTE_BLOCK = 512
TILE_SIZES = (1024, 768, 512, 256)
TM = TILE_SIZES[0]
TM_UNIT = TILE_SIZES[-1]
P_ROWS = (T_PAD * TOP_K + N_EXPERTS * (TM_UNIT - 1)) // TM_UNIT * TM_UNIT
MAX_TILES = P_ROWS // TM + N_EXPERTS + 1
SC_ROWS = 104
SC_ROWS_COMBINE = 32
N_PARTS = 4
PART_CHUNKS = N_CHUNKS // N_PARTS
D_PACK = D_MODEL // 2
VMEM_LIMIT = 56 * 1024 * 1024
SC_WORKERS = 32
assert T_PAD % (SC_WORKERS * SC_ROWS) == 0
assert (TOP_K * PART_CHUNKS * ROWS) % (SC_WORKERS * SC_ROWS_COMBINE) == 0
assert (TOP_K * (T_PAD - (N_PARTS - 1) * PART_CHUNKS * ROWS)) % (SC_WORKERS * SC_ROWS_COMBINE) == 0


def _rms(x, g):
    return x * lax.rsqrt(jnp.mean(x * x, axis=-1, keepdims=True) + EPS) * g


def _sigmoid(x):
    return 1.0 / (1.0 + jnp.exp(-x))


def _gelu_tanh(x):
    c = np.float32(np.sqrt(2.0 / np.pi))
    return 0.5 * x * (1.0 + jnp.tanh(c * (x + 0.044715 * (x * x * x))))


def _bdot(a, b):
    return jnp.dot(a.astype(BF16), b, preferred_element_type=F32)


def _pack_pairs(x):
    n = x.shape[1] // 2
    lo = lax.bitcast_convert_type(x[:, :n].astype(BF16).astype(F32), jnp.int32)
    hi = lax.bitcast_convert_type(x[:, n:].astype(BF16).astype(F32), jnp.int32)
    return lax.shift_right_logical(lo, 16) | hi


def _unpack_pairs(w):
    lo = lax.bitcast_convert_type(lax.shift_left(w, 16), F32)
    hi = lax.bitcast_convert_type(w & jnp.int32(-65536), F32)
    return jnp.concatenate([lo, hi], axis=1)


def _full(shape):
    n = len(shape)
    return pl.BlockSpec(shape, lambda *_: (0,) * n)


def _prep_kernel(lr_ref, li_ref, ls_ref, br_ref, bi_ref, lbr_ref, lbi_ref, bbr_ref, bbi_ref):
    lr = lr_ref[...]
    li = li_ref[...]
    step = jnp.exp(ls_ref[...])
    zr = lr * step
    zi = li * step
    mag = jnp.exp(zr)
    lbr = mag * jnp.cos(zi)
    lbi = mag * jnp.sin(zi)
    lbr_ref[...] = lbr
    lbi_ref[...] = lbi
    nr = lbr - 1.0
    den = lr * lr + li * li
    cr = (nr * lr + lbi * li) / den
    ci = (lbi * lr - nr * li) / den
    br = br_ref[...]
    bi = bi_ref[...]
    bbr_ref[...] = cr * br - ci * bi
    bbi_ref[...] = cr * bi + ci * br


def _prep(lam_re, lam_im, log_step, b_re, b_im):
    g, p, h = SSM_G, SSM_P, SSM_H
    out = pl.pallas_call(
        _prep_kernel,
        out_shape=[jax.ShapeDtypeStruct((g, 1, p), F32), jax.ShapeDtypeStruct((g, 1, p), F32),
                   jax.ShapeDtypeStruct((g, h, p), F32), jax.ShapeDtypeStruct((g, h, p), F32)],
        name="s5_prep",
    )(lam_re.reshape(g, 1, p), lam_im.reshape(g, 1, p), log_step.reshape(g, 1, 1),
      jnp.transpose(b_re, (0, 2, 1)), jnp.transpose(b_im, (0, 2, 1)))
    return out


def _s5_blocks(lbr, lbi, bbr, bbi, c_re, c_im):
    eye = jnp.eye(8, dtype=F32)
    shp = (N_SBLK, 8, SSM_H, SSM_P)

    def in_map(b):
        return (b.reshape(shp)[:, :, :, None, :] * eye[None, :, None, :, None]).reshape(N_SBLK, 128, SBLK)

    def out_map(c):
        ct = jnp.transpose(c.reshape(shp), (0, 1, 3, 2))
        return (ct[:, :, :, None, :] * eye[None, :, None, :, None]).reshape(N_SBLK, SBLK, 128)

    bblk = jnp.concatenate([in_map(bbr), in_map(bbi)], axis=-1).astype(BF16)
    cblk = jnp.concatenate([out_map(c_re), -out_map(c_im)], axis=1).astype(BF16)
    return bblk, lbr.reshape(N_SBLK, 1, SBLK), lbi.reshape(N_SBLK, 1, SBLK), cblk


def _ssm_post(y_lin, u, dskip, wglu, bglu, gs):
    y = _gelu_tanh(y_lin + dskip * u)
    y = y * _sigmoid(_bdot(y, wglu) + bglu)
    return _rms(y, gs)


def _mixer_prompt_kernel(sinks_ref, x_ref, gmix_ref, win_ref, bias_ref, bblk_ref, lamr_ref, lami_ref,
                         cblk_ref, dskip_ref, wglu_ref, bglu_ref, ga_ref, gs_ref, wout_ref,
                         gffn_ref, wrt_ref, br_ref, tri_ref,
                         hmid_ref, klast_ref, vlast_ref, sre_ref, sim_ref, hn_ref, idx_ref, gw_ref, rank_ref, cnt_ref,
                         proj_s, u_s, kprev_s, kprevr_s, vprev_s, vprevr_s, attn_s, ssm_s, utb_s, bu_s, xs_s, ytb_s):
    c = pl.program_id(0)

    @pl.when(c == 0)
    def _():
        zkv = jnp.zeros(kprev_s.shape, BF16)
        kprev_s[...] = zkv
        kprevr_s[...] = zkv
        vprev_s[...] = zkv
        vprevr_s[...] = zkv
        sre_ref[...] = jnp.zeros(sre_ref.shape, F32)
        sim_ref[...] = jnp.zeros(sim_ref.shape, F32)
        cnt_ref[...] = jnp.zeros(cnt_ref.shape, jnp.int32)

    x = x_ref[...].reshape(ROWS, D_MODEL)
    proj = _bdot(_rms(x, gmix_ref[...]), win_ref[...])
    u0 = D_ATTN + 2 * D_KV
    proj_s[...] = proj[:, 0:u0]
    for jb in range(D_SSM // 128):
        for b in range(BATCH):
            u_s[jb, b * PITCH:b * PITCH + CHUNK, :] = proj[b * CHUNK:(b + 1) * CHUNK, u0 + 128 * jb:u0 + 128 * (jb + 1)]

    lo = lax.broadcasted_iota(jnp.int32, (CHUNK, 128), 1) < HEAD_DIM
    hi = jnp.logical_not(lo)
    table = jnp.minimum(c, 1)
    hrow = lax.broadcasted_iota(jnp.int32, (4 * CHUNK, 1), 0) // CHUNK

    def sink_col(heads):
        col = jnp.full((4 * CHUNK, 1), sinks_ref[heads[3]], F32)
        for n in (2, 1, 0):
            col = jnp.where(hrow == n, sinks_ref[heads[n]], col)
        return col

    sink_nat = sink_col(NAT_HEADS)
    sink_rot = sink_col(ROT_HEADS)

    def attend(q, k, v, bias, sink):
        s = lax.dot_general(q, k, (((1,), (1,)), ((), ())), preferred_element_type=F32) + bias
        m = jnp.maximum(jnp.max(s, axis=-1, keepdims=True), sink)
        p = jnp.exp(s - m)
        den = jnp.sum(p, axis=-1, keepdims=True) + jnp.exp(sink - m)
        return jnp.dot(p.astype(BF16), v, preferred_element_type=F32) / den

    def attn_body(b, carry):
        r0 = pl.multiple_of(b * CHUNK, CHUNK)
        rows = pl.ds(r0, CHUNK)
        kb = proj_s[rows, D_ATTN:D_ATTN + D_KV]
        vb = proj_s[rows, D_ATTN + D_KV:D_ATTN + 2 * D_KV]
        kb16 = kb.astype(BF16)
        vb16 = vb.astype(BF16)
        kbr16 = pltpu.roll(kb, HEAD_DIM, 1).astype(BF16)
        vbr16 = pltpu.roll(vb, HEAD_DIM, 1).astype(BF16)
        k_nat = jnp.concatenate([kprev_s[b], kb16], axis=0)
        k_rot = jnp.concatenate([kprevr_s[b], kbr16], axis=0)
        v_nat = jnp.concatenate([vprev_s[b], vb16], axis=0)
        v_rot = jnp.concatenate([vprevr_s[b], vbr16], axis=0)
        q2 = [proj_s[rows, 128 * jq:128 * (jq + 1)] * (HEAD_DIM ** -0.5) for jq in range(N_HEADS // 2)]
        q_nat = jnp.concatenate([jnp.where(lo if h % 2 == 0 else hi, q2[h // 2], 0.0) for h in NAT_HEADS],
                                axis=0).astype(BF16)
        q_rot = jnp.concatenate([jnp.where(lo if h % 2 == 0 else hi, q2[h // 2], 0.0) for h in ROT_HEADS],
                                axis=0).astype(BF16)
        o_nat = attend(q_nat, k_nat, v_nat, bias_ref[table, 0], sink_nat)
        o_rot = attend(q_rot, k_rot, v_rot, bias_ref[table, 1], sink_rot)
        for jq in range(N_HEADS // 2):
            blk = slice(CHUNK * jq, CHUNK * (jq + 1))
            even, odd = (o_nat, o_rot) if jq < 2 else (o_rot, o_nat)
            attn_s[rows, 128 * jq:128 * (jq + 1)] = jnp.where(lo, even[blk], odd[blk])
        kprev_s[b] = kb16
        kprevr_s[b] = kbr16
        vprev_s[b] = vb16
        vprevr_s[b] = vbr16
        return carry

    def attn_group(i, carry):
        for n in range(ATTN_UNROLL):
            attn_body(ATTN_UNROLL * i + n, carry)
        return carry

    lax.fori_loop(0, BATCH // ATTN_UNROLL, attn_group, 0)

    for sc in range(CHUNK // SUB_T):
        t0 = sc * SUB_T
        for i in range(SUB_T):
            for jb in range(D_SSM // 128):
                utb_s[i * BATCH:(i + 1) * BATCH, 128 * jb:128 * (jb + 1)] = (
                    u_s[jb, pl.ds(t0 + i, BATCH, stride=PITCH), :])
        u_tb = utb_s[...]
        for j in range(N_SBLK):
            bu_s[...] = _bdot(u_tb[:, 128 * j:128 * (j + 1)], bblk_ref[j])
            lr = jnp.broadcast_to(lamr_ref[j], (BATCH, SBLK))
            li = jnp.broadcast_to(lami_ref[j], (BATCH, SBLK))
            cols = slice(SBLK * j, SBLK * (j + 1))

            def step(i, carry):
                sr, si = carry
                r = pl.ds(pl.multiple_of(i * BATCH, BATCH), BATCH)
                nr = lr * sr - li * si + bu_s[r, 0:SBLK]
                ni = lr * si + li * sr + bu_s[r, SBLK:2 * SBLK]
                xs_s[r, 0:SBLK] = nr
                xs_s[r, SBLK:2 * SBLK] = ni
                return nr, ni

            sr, si = lax.fori_loop(0, SUB_T, step, (sre_ref[:, cols], sim_ref[:, cols]), unroll=True)
            sre_ref[:, cols] = sr
            sim_ref[:, cols] = si
            ytb_s[:, 128 * j:128 * (j + 1)] = _bdot(xs_s[...], cblk_ref[j])
        yn = _ssm_post(ytb_s[...], u_tb, dskip_ref[...], wglu_ref[...], bglu_ref[...], gs_ref[...])
        for i in range(SUB_T):
            for jb in range(D_SSM // 128):
                ssm_s[jb, pl.ds(t0 + i, BATCH, stride=PITCH), :] = (
                    yn[i * BATCH:(i + 1) * BATCH, 128 * jb:128 * (jb + 1)])

    an = _rms(attn_s[...], ga_ref[...])
    sn = jnp.concatenate(
        [jnp.concatenate([ssm_s[jb, b * PITCH:b * PITCH + CHUNK, :] for b in range(BATCH)], axis=0)
         for jb in range(D_SSM // 128)], axis=1)
    h = x + _bdot(jnp.concatenate([an.astype(BF16), sn.astype(BF16)], axis=1), wout_ref[...])
    hmid_ref[...] = h
    _route(h, gffn_ref, wrt_ref, br_ref, tri_ref, cnt_ref, hn_ref, idx_ref, gw_ref, rank_ref)

    @pl.when(c == N_CHUNKS - 1)
    def _():
        klast_ref[...] = proj_s[:, D_ATTN:D_ATTN + D_KV].reshape(BATCH, CHUNK, D_KV)
        vlast_ref[...] = proj_s[:, D_ATTN + D_KV:D_ATTN + 2 * D_KV].reshape(BATCH, CHUNK, D_KV)


def _mixer_prompt(x_prompt, sinks, gmix, win, bias, bblk, lamr, lami, cblk, dskip, wglu, bglu, ga, gs, wout,
                  gffn, wrt, br):
    smem = pl.BlockSpec(memory_space=pltpu.SMEM)
    in_specs = [
        smem,
        pl.BlockSpec((BATCH, CHUNK, D_MODEL), lambda c: (0, c, 0)),
        _full((1, D_MODEL)), _full((D_MODEL, D_IN)), _full((2, 2, 4 * CHUNK, 2 * CHUNK)),
        _full((N_SBLK, 128, 2 * SBLK)), _full((N_SBLK, 1, SBLK)), _full((N_SBLK, 1, SBLK)),
        _full((N_SBLK, 2 * SBLK, 128)), _full((1, D_SSM)), _full((D_SSM, D_SSM)), _full((1, D_SSM)),
        _full((1, D_ATTN)), _full((1, D_SSM)), _full((D_MODEL, D_MODEL)),
        _full((1, D_MODEL)), _full((N_EXPERTS, D_MODEL)), _full((N_EXPERTS, 1)), _full((ROUTE_BLOCK, ROUTE_BLOCK)),
    ]
    route_specs, route_shapes = _route_specs(ROWS, lambda c: c)
    out_shape = [
        jax.ShapeDtypeStruct((T_PAD, D_MODEL), F32),
        jax.ShapeDtypeStruct((BATCH, CHUNK, D_KV), F32),
        jax.ShapeDtypeStruct((BATCH, CHUNK, D_KV), F32),
        jax.ShapeDtypeStruct((BATCH, SSM_G * SSM_P), F32),
        jax.ShapeDtypeStruct((BATCH, SSM_G * SSM_P), F32),
    ] + route_shapes
    out_specs = [
        pl.BlockSpec((ROWS, D_MODEL), lambda c: (c, 0)),
        _full((BATCH, CHUNK, D_KV)), _full((BATCH, CHUNK, D_KV)),
        _full((BATCH, SSM_G * SSM_P)), _full((BATCH, SSM_G * SSM_P)),
    ] + route_specs
    kv_scr = pltpu.VMEM((BATCH, CHUNK, D_KV), BF16)
    scratch = [
        pltpu.VMEM((ROWS, D_ATTN + 2 * D_KV), F32), pltpu.VMEM((D_SSM // 128, BATCH * PITCH, 128), F32),
        kv_scr, kv_scr, kv_scr, kv_scr,
        pltpu.VMEM((ROWS, D_ATTN), F32), pltpu.VMEM((D_SSM // 128, BATCH * PITCH, 128), F32),
        pltpu.VMEM((SUB_ROWS, D_SSM), F32), pltpu.VMEM((SUB_ROWS, 2 * SBLK), F32),
        pltpu.VMEM((SUB_ROWS, 2 * SBLK), F32), pltpu.VMEM((SUB_ROWS, D_SSM), F32),
    ]
    return pl.pallas_call(
        _mixer_prompt_kernel, grid=(N_CHUNKS,), in_specs=in_specs, out_specs=out_specs, out_shape=out_shape,
        scratch_shapes=scratch, name="mixer_prompt",
        compiler_params=pltpu.CompilerParams(dimension_semantics=("arbitrary",), vmem_limit_bytes=VMEM_LIMIT),
    )(sinks, x_prompt, gmix, win, bias, bblk, lamr, lami, cblk, dskip, wglu, bglu, ga, gs, wout,
      gffn, wrt, br, _tri(ROUTE_BLOCK))


SGRP = 16
SPITCH = DEC_BATCH + 8
N_SGRP = DEC_BATCH // SGRP


def _mixer_sample_kernel(x_ref, gmix_ref, win_ref, ck_ref, cv_ref, sbias_ref, sinkc_ref, bblk_ref, lamr_ref,
                         lami_ref, cblk_ref, dskip_ref, wglu_ref, bglu_ref, ga_ref, gs_ref, wout_ref,
                         x0r_ref, x0i_ref, gffn_ref, wrt_ref, br_ref, tri_ref, cnt_in_ref,
                         hbuf_ref, hnbuf_ref, idxbuf_ref, gwbuf_ref, rankbuf_ref,
                         hmid_ref, knew_ref, vnew_ref, sre_ref, sim_ref, hn_ref, idx_ref, gw_ref, rank_ref, cnt_ref,
                         proj_s, qall_s, oall_s, kn8_s, vn8_s):
    del hbuf_ref, hnbuf_ref, idxbuf_ref, gwbuf_ref, rankbuf_ref
    g = pl.program_id(0)
    lo = lax.broadcasted_iota(jnp.int32, (DEC_BATCH, 128), 1) < HEAD_DIM

    @pl.when(g == 0)
    def _():
        proj = _bdot(_rms(x_ref[...], gmix_ref[...]), win_ref[...])
        proj_s[...] = proj
        for h in range(N_HEADS):
            jq, half, kv = h // 2, h % 2, h // 4
            q2 = proj[:, 128 * jq:128 * (jq + 1)] * (HEAD_DIM ** -0.5)
            if half != kv:
                q2 = pltpu.roll(q2, HEAD_DIM, 1)
            hr = slice(h * SPITCH, h * SPITCH + DEC_BATCH)
            qall_s[hr, :] = jnp.where(lo if kv == 0 else jnp.logical_not(lo), q2, 0.0)
            kn8_s[hr, :] = proj[:, D_ATTN:D_ATTN + D_KV]
            vn8_s[hr, :] = proj[:, D_ATTN + D_KV:D_ATTN + 2 * D_KV]

    def head_rows(ref):
        return jnp.stack([ref[pl.ds(g * SGRP + ii, N_HEADS, stride=SPITCH), :] for ii in range(SGRP)], axis=0)

    qe = head_rows(qall_s)
    knew = head_rows(kn8_s)
    vnew = head_rows(vn8_s)
    s = jnp.einsum('bhd,bkd->bhk', qe.astype(BF16), ck_ref[...].astype(BF16),
                   preferred_element_type=F32) + sbias_ref[...][None]
    s_new = jnp.sum(qe * knew, axis=-1, keepdims=True)
    sink = sinkc_ref[...][None]
    m = jnp.maximum(jnp.maximum(jnp.max(s, axis=-1, keepdims=True), s_new), sink)
    p = jnp.exp(s - m)
    p_new = jnp.exp(s_new - m)
    den = jnp.sum(p, axis=-1, keepdims=True) + p_new + jnp.exp(sink - m)
    o = (jnp.einsum('bhk,bkd->bhd', p.astype(BF16), cv_ref[...].astype(BF16),
                    preferred_element_type=F32) + p_new * vnew) / den
    for ii in range(SGRP):
        oall_s[pl.ds(g * SGRP + ii, N_HEADS, stride=SPITCH), :] = o[ii]

    @pl.when(g == N_SGRP - 1)
    def _():
        proj = proj_s[...]
        blocks = []
        for jq in range(N_HEADS // 2):
            oa = oall_s[(2 * jq) * SPITCH:(2 * jq) * SPITCH + DEC_BATCH, :]
            ob = oall_s[(2 * jq + 1) * SPITCH:(2 * jq + 1) * SPITCH + DEC_BATCH, :]
            if jq // 2 == 0:
                blocks.append(jnp.where(lo, oa, pltpu.roll(ob, HEAD_DIM, 1)))
            else:
                blocks.append(jnp.where(lo, pltpu.roll(oa, HEAD_DIM, 1), ob))
        attn = jnp.concatenate(blocks, axis=1)
        u = proj[:, D_ATTN + 2 * D_KV:]
        ys = []
        for j in range(N_SBLK):
            bu = _bdot(u[:, 128 * j:128 * (j + 1)], bblk_ref[j])
            lr = lamr_ref[j]
            li = lami_ref[j]
            cols = slice(SBLK * j, SBLK * (j + 1))
            sr = x0r_ref[:, cols]
            si = x0i_ref[:, cols]
            nr = lr * sr - li * si + bu[:, 0:SBLK]
            ni = lr * si + li * sr + bu[:, SBLK:2 * SBLK]
            sre_ref[:, cols] = nr
            sim_ref[:, cols] = ni
            ys.append(_bdot(jnp.concatenate([nr, ni], axis=1), cblk_ref[j]))
        sn = _ssm_post(jnp.concatenate(ys, axis=1), u, dskip_ref[...], wglu_ref[...], bglu_ref[...], gs_ref[...])
        an = _rms(attn, ga_ref[...])
        h = x_ref[...] + _bdot(jnp.concatenate([an.astype(BF16), sn.astype(BF16)], axis=1), wout_ref[...])
        h = jnp.concatenate([h, jnp.zeros((DEC_BATCH, D_MODEL), F32)], axis=0)
        hmid_ref[...] = h
        cnt_ref[...] = cnt_in_ref[...]
        _route(h, gffn_ref, wrt_ref, br_ref, tri_ref, cnt_ref, hn_ref, idx_ref, gw_ref, rank_ref)
        knew_ref[...] = proj[:, D_ATTN:D_ATTN + D_KV]
        vnew_ref[...] = proj[:, D_ATTN + D_KV:D_ATTN + 2 * D_KV]


def _mixer_sample(x_s, gmix, win, ck, cv, sbias, sinkc, bblk, lamr, lami, cblk, dskip, wglu, bglu, ga, gs, wout,
                  x0r, x0i, gffn, wrt, br, cnt_in, hbuf, hnbuf, idxbuf, gwbuf, rankbuf):
    nst = SSM_G * SSM_P
    in_specs = [
        _full((DEC_BATCH, D_MODEL)), _full((1, D_MODEL)), _full((D_MODEL, D_IN)),
        pl.BlockSpec((SGRP, WINDOW, D_KV), lambda g: (g, 0, 0)),
        pl.BlockSpec((SGRP, WINDOW, D_KV), lambda g: (g, 0, 0)),
        _full((N_HEADS, WINDOW)), _full((N_HEADS, 1)),
        _full((N_SBLK, 128, 2 * SBLK)), _full((N_SBLK, 1, SBLK)), _full((N_SBLK, 1, SBLK)),
        _full((N_SBLK, 2 * SBLK, 128)), _full((1, D_SSM)), _full((D_SSM, D_SSM)), _full((1, D_SSM)),
        _full((1, D_ATTN)), _full((1, D_SSM)), _full((D_MODEL, D_MODEL)),
        _full((DEC_BATCH, nst)), _full((DEC_BATCH, nst)),
        _full((1, D_MODEL)), _full((N_EXPERTS, D_MODEL)), _full((N_EXPERTS, 1)),
        _full((2 * DEC_BATCH, 2 * DEC_BATCH)), _full((N_EXPERTS, 1)),
    ] + [pl.BlockSpec(memory_space=pl.ANY)] * 5
    tail_block = SAMPLE_ROW0 // (2 * DEC_BATCH)
    route_specs, route_shapes = _route_specs(2 * DEC_BATCH, lambda g: tail_block)
    out_shape = [
        jax.ShapeDtypeStruct((T_PAD, D_MODEL), F32),
        jax.ShapeDtypeStruct((DEC_BATCH, D_KV), F32), jax.ShapeDtypeStruct((DEC_BATCH, D_KV), F32),
        jax.ShapeDtypeStruct((DEC_BATCH, nst), F32), jax.ShapeDtypeStruct((DEC_BATCH, nst), F32),
    ] + route_shapes
    out_specs = [
        pl.BlockSpec((2 * DEC_BATCH, D_MODEL), lambda g: (tail_block, 0)),
        _full((DEC_BATCH, D_KV)), _full((DEC_BATCH, D_KV)),
        _full((DEC_BATCH, nst)), _full((DEC_BATCH, nst)),
    ] + route_specs
    head_rows = pltpu.VMEM((N_HEADS * SPITCH, 128), F32)
    scratch = [pltpu.VMEM((DEC_BATCH, D_IN), F32), head_rows, head_rows, head_rows, head_rows]
    return pl.pallas_call(
        _mixer_sample_kernel, grid=(N_SGRP,), in_specs=in_specs, out_specs=out_specs, out_shape=out_shape,
        scratch_shapes=scratch, input_output_aliases={24: 0, 25: 5, 26: 6, 27: 7, 28: 8}, name="mixer_sample",
        compiler_params=pltpu.CompilerParams(dimension_semantics=("arbitrary",), vmem_limit_bytes=VMEM_LIMIT),
    )(x_s, gmix, win, ck, cv, sbias, sinkc, bblk, lamr, lami, cblk, dskip, wglu, bglu, ga, gs, wout, x0r, x0i,
      gffn, wrt, br, _tri(2 * DEC_BATCH), cnt_in, hbuf, hnbuf, idxbuf, gwbuf, rankbuf)


def _route(h, g_ref, wrt_ref, br_ref, tri_ref, cnt_ref, hn_ref, idx_ref, gw_ref, rank_ref):
    hn = _rms(h, g_ref[...])
    hn_ref[...] = _pack_pairs(hn)
    hn_hi = hn.astype(BF16)
    hn_lo = (hn - hn_hi.astype(F32)).astype(BF16)
    w = wrt_ref[...]
    w_hi = w.astype(BF16)
    w_lo = (w - w_hi.astype(F32)).astype(BF16)
    nt = (((1,), (1,)), ((), ()))
    both = lax.dot_general(jnp.concatenate([w_hi, w_lo], axis=0), hn_hi, nt, preferred_element_type=F32)
    logits = (both[:N_EXPERTS] + both[N_EXPERTS:]
              + lax.dot_general(w_hi, hn_lo, nt, preferred_element_type=F32)) + br_ref[...]
    eidx = lax.broadcasted_iota(jnp.int32, logits.shape, 0)
    vals, onehots = [], []
    l = logits
    for k in range(TOP_K):
        m = jnp.max(l, axis=0, keepdims=True)
        ik = jnp.min(jnp.where(l == m, eidx, N_EXPERTS), axis=0, keepdims=True)
        oh = eidx == ik
        idx_ref[k:k + 1, :] = ik
        vals.append(m)
        onehots.append(oh)
        l = jnp.where(oh, -jnp.inf, l)
    exps = [jnp.exp(v - vals[0]) for v in vals]
    den = exps[0] + exps[1] + exps[2] + exps[3]
    for k in range(TOP_K):
        gw_ref[k:k + 1, :] = exps[k] / den
    member = jnp.zeros(logits.shape, F32)
    for oh in onehots:
        member = member + jnp.where(oh, 1.0, 0.0)
    wblk = tri_ref.shape[0]
    base = cnt_ref[...].astype(F32)
    befores = []
    for cb in range(h.shape[0] // wblk):
        mblk = member[:, cb * wblk:(cb + 1) * wblk]
        befores.append(jnp.dot(mblk.astype(BF16), tri_ref[...], preferred_element_type=F32) + base)
        base = base + jnp.sum(mblk, axis=1, keepdims=True)
    before = jnp.concatenate(befores, axis=1)
    for k in range(TOP_K):
        rank_ref[k:k + 1, :] = jnp.sum(jnp.where(onehots[k], before, 0.0), axis=0, keepdims=True).astype(jnp.int32)
    cnt_ref[...] = base.astype(jnp.int32)


def _route_specs(rows, block):
    specs = [pl.BlockSpec((rows, D_PACK), lambda i: (block(i), 0)), pl.BlockSpec((TOP_K, rows), lambda i: (0, block(i))),
             pl.BlockSpec((TOP_K, rows), lambda i: (0, block(i))), pl.BlockSpec((TOP_K, rows), lambda i: (0, block(i))),
             _full((N_EXPERTS, 1))]
    shapes = [jax.ShapeDtypeStruct((T_PAD, D_PACK), jnp.int32), jax.ShapeDtypeStruct((TOP_K, T_PAD), jnp.int32),
              jax.ShapeDtypeStruct((TOP_K, T_PAD), F32), jax.ShapeDtypeStruct((TOP_K, T_PAD), jnp.int32),
              jax.ShapeDtypeStruct((N_EXPERTS, 1), jnp.int32)]
    return specs, shapes


def _tri(n):
    return jnp.asarray(np.triu(np.ones((n, n), np.float32), 1), BF16)


def _place_kernel(off_ref, idx_ref, rank_ref, pos_ref):
    idx = idx_ref[...]
    pos = rank_ref[...]
    for e in range(N_EXPERTS):
        pos = pos + jnp.where(idx == e, off_ref[e], 0)
    pos_ref[...] = pos


def _place(offsets, idx, rank):
    return pl.pallas_call(
        _place_kernel,
        in_specs=[pl.BlockSpec(memory_space=pltpu.SMEM), pl.BlockSpec(memory_space=pltpu.VMEM),
                  pl.BlockSpec(memory_space=pltpu.VMEM)],
        out_specs=pl.BlockSpec(memory_space=pltpu.VMEM),
        out_shape=jax.ShapeDtypeStruct((TOP_K, T_PAD), jnp.int32), name="place",
    )(offsets, idx, rank)


def _sc_mesh():
    return plsc.VectorSubcoreMesh(core_axis_name="core", subcore_axis_name="subcore")


def _sc_dispatch(rows, pos):
    n, d = rows.shape
    nblk = n // SC_ROWS
    pos_w = pos.reshape(TOP_K, nblk, SC_ROWS).transpose(1, 0, 2)

    @functools.partial(pl.kernel, out_type=jax.ShapeDtypeStruct((P_ROWS, d), rows.dtype), mesh=_sc_mesh(),
                       scratch_types=[], name="dispatch")
    def run(x_hbm, i_hbm, o_hbm):
        def body(x_vmem, i_vmem):
            for k in range(TOP_K):
                pltpu.sync_copy(x_vmem, o_hbm.at[i_vmem.at[0, k]])

        pltpu.emit_pipeline(
            body, grid=(nblk,),
            in_specs=[pl.BlockSpec((SC_ROWS, d), lambda i: (i, 0)),
                      pl.BlockSpec((1, TOP_K, SC_ROWS), lambda i: (i, 0, 0))],
            out_specs=[], core_axis_name=("core", "subcore"), dimension_semantics=(pltpu.PARALLEL,),
        )(x_hbm, i_hbm)

    return run(rows, pos_w)


def _sc_combine(rows, flat_pos):
    _, d = rows.shape
    n = flat_pos.shape[0]
    w = SC_ROWS_COMBINE
    nblk = n // w
    pos_w = flat_pos.reshape(nblk, 1, w)

    @functools.partial(pl.kernel, out_type=jax.ShapeDtypeStruct((n, d), rows.dtype), mesh=_sc_mesh(),
                       scratch_types=[], name="combine")
    def run(x_hbm, i_hbm, o_hbm):
        def body(i_vmem, o_vmem):
            pltpu.sync_copy(x_hbm.at[i_vmem.at[0, 0]], o_vmem)

        pltpu.emit_pipeline(
            body, grid=(nblk,),
            in_specs=[pl.BlockSpec((1, 1, w), lambda i: (i, 0, 0))],
            out_specs=[pl.BlockSpec((w, d), lambda i: (i, 0))],
            core_axis_name=("core", "subcore"), dimension_semantics=(pltpu.PARALLEL,),
        )(i_hbm, o_hbm)

    return run(rows, pos_w)


def _experts_kernel(ts_ref, trow_ref, tsz_ref, xs_hbm, w1_hbm, b1_ref, w2_hbm, b2_ref, ys_hbm,
                    w1_st, w2_st, w1_s, w2_s, xbuf, ybuf, wsem, xsem, ysem):
    e = pl.program_id(0)
    n_valid = ts_ref[N_EXPERTS]

    def w_copies(ex, slot):
        return (pltpu.make_async_copy(w1_hbm.at[ex], w1_st.at[slot], wsem.at[0, slot]),
                pltpu.make_async_copy(w2_hbm.at[ex], w2_st.at[slot], wsem.at[1, slot]))

    def x_copy(t, slot, rows):
        src = xs_hbm.at[pl.ds(pl.multiple_of(trow_ref[t], TM_UNIT), rows)]
        return pltpu.make_async_copy(src, xbuf.at[slot, pl.ds(0, rows)], xsem.at[slot])

    def y_copy(t, slot, rows):
        dst = ys_hbm.at[pl.ds(pl.multiple_of(trow_ref[t], TM_UNIT), rows)]
        return pltpu.make_async_copy(ybuf.at[slot, pl.ds(0, rows)], dst, ysem.at[slot])

    def by_size(t, fn):
        for rows in TILE_SIZES:
            @pl.when(tsz_ref[t] == rows)
            def _():
                fn(rows)

    @pl.when(e == 0)
    def _():
        for c in w_copies(0, 0):
            c.start()
        by_size(0, lambda rows: x_copy(0, 0, rows).start())

    @pl.when(e + 1 < N_EXPERTS)
    def _():
        for c in w_copies(e + 1, (e + 1) % 2):
            c.start()

    wslot = e % 2
    for c in w_copies(e, wslot):
        c.wait()
    for r in range(4):
        rs = slice(256 * r, 256 * (r + 1))
        w1_s[rs, :] = w1_st[wslot, rs, :].astype(BF16)
        w2_s[rs, :] = w2_st[wslot, rs, :].astype(BF16)

    def tile(t, carry):
        slot = t % 2

        @pl.when(t + 1 < n_valid)
        def _():
            by_size(t + 1, lambda rows: x_copy(t + 1, 1 - slot, rows).start())

        by_size(t, lambda rows: x_copy(t, slot, rows).wait())

        @pl.when(t >= 2)
        def _():
            by_size(t - 2, lambda rows: y_copy(t - 2, slot, rows).wait())

        def compute(rows):
            hdn = _bdot(_unpack_pairs(xbuf[slot, 0:rows]), w1_s[...]) + b1_ref[0]
            gt = jnp.minimum(hdn[:, :D_FF], SWIGLU_LIMIT)
            up = jnp.clip(hdn[:, D_FF:], -SWIGLU_LIMIT, SWIGLU_LIMIT)
            act = (up + 1.0) * gt * _sigmoid(SWIGLU_ALPHA * gt)
            ybuf[slot, 0:rows] = _pack_pairs(_bdot(act, w2_s[...]) + b2_ref[0])
            y_copy(t, slot, rows).start()

        by_size(t, compute)
        return carry

    lax.fori_loop(ts_ref[e], ts_ref[e + 1], tile, 0)

    @pl.when(e == N_EXPERTS - 1)
    def _():
        @pl.when(n_valid >= 2)
        def _():
            by_size(n_valid - 2, lambda rows: y_copy(n_valid - 2, n_valid % 2, rows).wait())

        by_size(n_valid - 1, lambda rows: y_copy(n_valid - 1, (n_valid - 1) % 2, rows).wait())


def _tile_plan(counts):
    units = (counts + (TM_UNIT - 1)) // TM_UNIT
    unit_end = jnp.cumsum(units)
    offsets = ((unit_end - units) * TM_UNIT).astype(jnp.int32)
    per_full = TM // TM_UNIT
    n_full = units // per_full
    rest = units % per_full
    tiles_per = n_full + (rest > 0).astype(jnp.int32)
    tile_end = jnp.cumsum(tiles_per)
    tile_start = jnp.concatenate([jnp.zeros((1,), jnp.int32), tile_end.astype(jnp.int32)])
    t = jnp.arange(MAX_TILES, dtype=jnp.int32)[:, None]
    done = tile_end[None, :] <= t
    mine = jnp.sum(done.astype(jnp.int32), axis=1, keepdims=True) == jnp.arange(N_EXPERTS, dtype=jnp.int32)[None, :]
    pick = lambda v: jnp.sum(jnp.where(mine, v[None, :], 0), axis=1)
    j = t[:, 0] - jnp.max(jnp.where(done, tile_end[None, :], 0), axis=1)
    rows = jnp.where(j < pick(n_full), TM, pick(rest) * TM_UNIT)
    row0 = pick(offsets) + j * TM
    valid = t[:, 0] < tile_end[-1]
    tile_row = jnp.where(valid, row0, 0).astype(jnp.int32)
    tile_rows = jnp.where(valid, rows, 0).astype(jnp.int32)
    return offsets, tile_start, tile_row, tile_rows


def _experts(tile_start, tile_row, tile_rows, xs, w_up, b_up, w_down, b_down):
    wsel = lambda e, *_: (e, 0, 0)
    hbm = pl.BlockSpec(memory_space=pl.ANY)
    grid_spec = pltpu.PrefetchScalarGridSpec(
        num_scalar_prefetch=3, grid=(N_EXPERTS,),
        in_specs=[hbm, hbm, pl.BlockSpec((1, 1, 2 * D_FF), wsel), hbm, pl.BlockSpec((1, 1, D_MODEL), wsel)],
        out_specs=hbm,
        scratch_shapes=[pltpu.VMEM((2, D_MODEL, 2 * D_FF), F32), pltpu.VMEM((2, D_FF, D_MODEL), F32),
                        pltpu.VMEM((D_MODEL, 2 * D_FF), BF16), pltpu.VMEM((D_FF, D_MODEL), BF16),
                        pltpu.VMEM((2, TM, D_PACK), jnp.int32), pltpu.VMEM((2, TM, D_PACK), jnp.int32),
                        pltpu.SemaphoreType.DMA((2, 2)), pltpu.SemaphoreType.DMA((2,)),
                        pltpu.SemaphoreType.DMA((2,))],
    )
    return pl.pallas_call(
        _experts_kernel, grid_spec=grid_spec, out_shape=jax.ShapeDtypeStruct((P_ROWS, D_PACK), jnp.int32),
        name="experts",
        compiler_params=pltpu.CompilerParams(dimension_semantics=("arbitrary",), vmem_limit_bytes=VMEM_LIMIT),
    )(tile_start, tile_row, tile_rows, xs, w_up, b_up.reshape(N_EXPERTS, 1, 2 * D_FF), w_down,
      b_down.reshape(N_EXPERTS, 1, D_MODEL))


def _final_kernel(h_ref, yg_ref, gw_ref, p_ref, gple_ref, wg_ref, wp_ref, gfin_ref, *rest):
    out_ref = rest[-1]
    rows = h_ref.shape[0]
    gw = gw_ref[...]
    h = h_ref[...]
    for k in range(TOP_K):
        h = h + gw[:, k:k + 1] * _unpack_pairs(yg_ref[k])
    gate = _sigmoid(_bdot(_rms(h, gple_ref[...]), wg_ref[...]))
    h = h + gate * _bdot(p_ref[...].reshape(rows, PLE_DIM), wp_ref[...])
    out_ref[...] = _rms(h, gfin_ref[...]).reshape(out_ref.shape)


def _final_prompt(part, hmid, yg, gwt, p_prompt, gple, wg, wp, gfin, y_prev):
    nb = BATCH
    rows = nb * CHUNK
    nbh = BATCH // nb
    c0 = part * PART_CHUNKS
    rb = lambda c, b: ((c0 + c) * nbh + b, 0)
    in_specs = [pl.BlockSpec((rows, D_MODEL), rb),
                pl.BlockSpec((TOP_K, rows, D_PACK), lambda c, b: (0, c * nbh + b, 0)),
                pl.BlockSpec((rows, TOP_K), rb),
                pl.BlockSpec((nb, CHUNK, PLE_DIM), lambda c, b: (b, c0 + c, 0)),
                _full((1, D_MODEL)), _full((D_MODEL, D_MODEL)), _full((PLE_DIM, D_MODEL)), _full((1, D_MODEL))]
    args = [hmid, yg, gwt, p_prompt, gple, wg, wp, gfin]
    aliases = {}
    if y_prev is not None:
        in_specs.append(pl.BlockSpec(memory_space=pl.ANY))
        args.append(y_prev)
        aliases = {len(args) - 1: 0}
    return pl.pallas_call(
        _final_kernel, grid=(PART_CHUNKS, nbh), in_specs=in_specs,
        out_specs=pl.BlockSpec((nb, CHUNK, D_MODEL), lambda c, b: (b, c0 + c, 0)),
        out_shape=jax.ShapeDtypeStruct((BATCH, SEQ, D_MODEL), F32), name="final_prompt",
        input_output_aliases=aliases,
        compiler_params=pltpu.CompilerParams(dimension_semantics=("arbitrary", "arbitrary"),
                                             vmem_limit_bytes=VMEM_LIMIT),
    )(*args)


def _final_sample(hmid, yg, gwt, p_sample, gple, wg, wp, gfin):
    blk = SAMPLE_ROW0 // DEC_BATCH
    return pl.pallas_call(
        _final_kernel, grid=(1,),
        in_specs=[pl.BlockSpec((DEC_BATCH, D_MODEL), lambda i: (blk, 0)),
                  pl.BlockSpec((TOP_K, DEC_BATCH, D_PACK), lambda i: (0, PART_CHUNKS * ROWS // DEC_BATCH, 0)),
                  pl.BlockSpec((DEC_BATCH, TOP_K), lambda i: (blk, 0)),
                  _full((DEC_BATCH, PLE_DIM)),
                  _full((1, D_MODEL)), _full((D_MODEL, D_MODEL)), _full((PLE_DIM, D_MODEL)), _full((1, D_MODEL))],
        out_specs=_full((DEC_BATCH, D_MODEL)),
        out_shape=jax.ShapeDtypeStruct((DEC_BATCH, D_MODEL), F32), name="final_sample",
        compiler_params=pltpu.CompilerParams(dimension_semantics=("arbitrary",), vmem_limit_bytes=VMEM_LIMIT),
    )(hmid, yg, gwt, p_sample, gple, wg, wp, gfin)


def _alibi_tables():
    slopes = 2.0 ** (-8.0 * (np.arange(N_HEADS, dtype=np.float64) + 1.0) / N_HEADS)
    i = np.arange(CHUNK)[:, None]
    j = np.arange(2 * CHUNK)[None, :]
    dist = i + CHUNK - j
    valid = (dist >= 0) & (dist <= WINDOW)
    tabs = []
    for has_prev in (False, True):
        ok = valid & ((j >= CHUNK) | has_prev)
        tabs.append(np.where(ok[None], -slopes[:, None, None] * dist[None], NEG))
    prompt = np.stack([np.stack([np.concatenate([t[h] for h in grp], axis=0) for grp in (NAT_HEADS, ROT_HEADS)])
                       for t in tabs]).astype(np.float32)
    wb = min(WINDOW, PAST_LEN)
    sample = (-slopes[:, None] * (wb - np.arange(wb))[None, :]).astype(np.float32)
    return prompt, sample


def kernel(x_prompt, x_sample, cache_k_win, cache_v_win, state_ssm_re, state_ssm_im, p_prompt, p_sample, norm_mix, w_in, sinks, ssm_lam_re, ssm_lam_im, ssm_log_step, ssm_b_re, ssm_b_im, ssm_c_re, ssm_c_im, ssm_d, w_glu, b_glu, norm_attn_out, norm_ssm_out, w_out, norm_ffn, w_router, b_router, w_up, b_up, w_down, b_down, norm_ple, w_ple_gate, w_ple_proj, norm_final):
    nst = SSM_G * SSM_P
    bias_np, sbias_np = _alibi_tables()
    bias = jnp.asarray(bias_np)
    sbias = jnp.asarray(sbias_np)

    lbr, lbi, bbr, bbi = _prep(ssm_lam_re[0], ssm_lam_im[0], ssm_log_step[0], ssm_b_re[0], ssm_b_im[0])
    bblk, lamr, lami, cblk = _s5_blocks(lbr, lbi, bbr, bbi, ssm_c_re[0], ssm_c_im[0])

    gmix = norm_mix[0].reshape(1, D_MODEL)
    win = w_in[0].astype(BF16)
    dskip = ssm_d[0].reshape(1, D_SSM)
    wglu = w_glu[0].astype(BF16)
    bglu = b_glu[0].reshape(1, D_SSM)
    ga = norm_attn_out[0].reshape(1, D_ATTN)
    gs = norm_ssm_out[0].reshape(1, D_SSM)
    wout = w_out[0].astype(BF16)
    sink = sinks[0]

    gffn = norm_ffn[0].reshape(1, D_MODEL)
    wrt = w_router[0].T
    br = b_router[0].reshape(N_EXPERTS, 1)
    hbuf, k_last, v_last, re_p, im_p, hnbuf, idxbuf, gwbuf, rankbuf, cnt_p = _mixer_prompt(
        x_prompt, sink, gmix, win, bias, bblk, lamr, lami, cblk, dskip, wglu, bglu, ga, gs, wout, gffn, wrt, br)

    ck = cache_k_win[0].reshape(DEC_BATCH, WINDOW, D_KV)
    cv = cache_v_win[0].reshape(DEC_BATCH, WINDOW, D_KV)
    hmid, k_new, v_new, re_s, im_s, hn, idx, gw, rank, counts = _mixer_sample(
        x_sample.reshape(DEC_BATCH, D_MODEL), gmix, win, ck, cv, sbias, sink.reshape(N_HEADS, 1), bblk, lamr, lami,
        cblk, dskip, wglu, bglu, ga, gs, wout, state_ssm_re[0].reshape(DEC_BATCH, nst),
        state_ssm_im[0].reshape(DEC_BATCH, nst), gffn, wrt, br, cnt_p, hbuf, hnbuf, idxbuf, gwbuf, rankbuf)

    offsets, tile_start, tile_row, tile_rows = _tile_plan(counts[:, 0])

    pos = _place(offsets, idx, rank)
    xs = _sc_dispatch(hn, pos)
    ys = _experts(tile_start, tile_row, tile_rows, xs, w_up[0], b_up[0], w_down[0], b_down[0])
    ygs = []
    for part in range(N_PARTS):
        r0 = part * PART_CHUNKS * ROWS
        r1 = T_PAD if part == N_PARTS - 1 else r0 + PART_CHUNKS * ROWS
        ygs.append(_sc_combine(ys, pos[:, r0:r1].reshape(TOP_K * (r1 - r0))).reshape(TOP_K, r1 - r0, D_PACK))

    gwt = gw.T
    gple = norm_ple[0].reshape(1, D_MODEL)
    wg = w_ple_gate[0].astype(BF16)
    wp = w_ple_proj[0].astype(BF16)
    gfin = norm_final.reshape(1, D_MODEL)
    y_prompt = None
    for part in range(N_PARTS):
        y_prompt = _final_prompt(part, hmid, ygs[part], gwt, p_prompt[0], gple, wg, wp, gfin, y_prompt)
    y_sample = _final_sample(hmid, ygs[-1], gwt, p_sample[0].reshape(DEC_BATCH, PLE_DIM), gple, wg, wp, gfin)

    k_win_s = jnp.concatenate([ck[:, 1:], k_new[:, None, :]], axis=1)
    v_win_s = jnp.concatenate([cv[:, 1:], v_new[:, None, :]], axis=1)
    kv5 = (1, BATCH, CHUNK, N_KV, HEAD_DIM)
    skv5 = (1, DEC_BATCH, WINDOW, N_KV, HEAD_DIM)
    return (y_prompt, y_sample.reshape(DEC_BATCH, 1, D_MODEL),
            k_last.reshape(kv5), v_last.reshape(kv5),
            re_p.reshape(1, BATCH, SSM_G, SSM_P), im_p.reshape(1, BATCH, SSM_G, SSM_P),
            k_win_s.reshape(skv5), v_win_s.reshape(skv5),
            re_s.reshape(1, DEC_BATCH, SSM_G, SSM_P), im_s.reshape(1, DEC_BATCH, SSM_G, SSM_P))
```
